```python
import math
import jax, jax.numpy as jnp
from jax import lax
import numpy as np

D_MODEL = 1024
BATCH = 8
SEQ = 2048
DEPTH = 1

HEAD_DIM = 64
N_MIX_HEADS = D_MODEL // HEAD_DIM
FOX_HEADS = N_MIX_HEADS // 2
NSA_HEADS = N_MIX_HEADS - FOX_HEADS
NSA_GQA = 4
NSA_KV_GROUPS = NSA_HEADS // NSA_GQA
Q_BLOCK = 128
SEL_Q_CHUNK = 64
CMP_BLOCK = 32
CMP_STRIDE = 16
CMP_HIDDEN = 256
SEL_BLOCK = 64
N_SEL = 16
WINDOW = 512
N_BUCKETS = 32
MAX_DISTANCE = 128
N_EXPERTS = 64
N_EXPERT_GROUPS = 8
TOPK_GROUPS = 4
TOP_K = 8
D_EXPERT = 256
D_SHARED = 256
ROUTED_SCALE = 2.5
LN_EPS = 1e-5
NEG_BIG = -1e30
FORCE_SCORE = 1e4
DEEPNORM_ALPHA = (2 * DEPTH) ** 0.25
DEEPNORM_BETA = (8 * DEPTH) ** -0.25

FOX_W = FOX_HEADS * HEAD_DIM
NSA_W = NSA_HEADS * HEAD_DIM
KV_W = NSA_KV_GROUPS * HEAD_DIM
IN_WIDTHS = (FOX_W, FOX_W, FOX_W, FOX_HEADS, NSA_W, KV_W, KV_W, KV_W, KV_W, KV_W, KV_W, 3 * NSA_HEADS)
IN_IS_VALUE = (False, False, True, False, False, False, True, False, True, False, True, False)
D_IN_PROJ = sum(IN_WIDTHS)

kernel_name = 'fox_nsa_moe_deepnorm_adaln_hybrid'


def layer_norm(x, g, b):
    xf = x.astype(jnp.float32)
    mu = jnp.mean(xf, axis=-1, keepdims=True)
    var = jnp.mean(jnp.square(xf - mu), axis=-1, keepdims=True)
    y = (xf - mu) * lax.rsqrt(var + LN_EPS)
    return (y * g.astype(jnp.float32) + b.astype(jnp.float32)).astype(x.dtype)


def t5_bucket(dist):
    n = jnp.maximum(dist, 0)
    max_exact = N_BUCKETS // 2
    nf = jnp.maximum(n, 1).astype(jnp.float32)
    large = max_exact + (jnp.log(nf / max_exact) / math.log(MAX_DISTANCE / max_exact)
                         * (N_BUCKETS - max_exact)).astype(jnp.int32)
    large = jnp.minimum(large, N_BUCKETS - 1)
    return jnp.where(n < max_exact, n, large)


def forgetting_attention(q, k, v, log_f):
    B, S, H, Dh = q.shape
    scale = Dh ** -0.5
    cum = jnp.cumsum(log_f, axis=1).transpose(0, 2, 1)
    outs = []
    for i in range(S // Q_BLOCK):
        q0, kend = i * Q_BLOCK, (i + 1) * Q_BLOCK
        logits = jnp.einsum('bqhd,bkhd->bhqk', q[:, q0:kend], k[:, :kend]).astype(jnp.float32) * scale
        decay = cum[:, :, q0:kend, None] - cum[:, :, None, :kend]
        qpos = q0 + jnp.arange(Q_BLOCK)
        mask = jnp.arange(kend)[None, :] <= qpos[:, None]
        logits = jnp.where(mask, logits + decay, -jnp.inf)
        p = jax.nn.softmax(logits, axis=-1)
        outs.append(jnp.einsum('bhqk,bkhd->bqhd', p.astype(v.dtype), v[:, :kend]))
    return jnp.concatenate(outs, axis=1)


def compress_blocks(x, blk_idx, pos, w1, w2):
    B, G = x.shape[0], x.shape[2]
    n_cmp = blk_idx.shape[0]
    xb = x[:, blk_idx] + pos[None, None, :, None, :]
    flat = xb.transpose(0, 1, 3, 2, 4).reshape(B, n_cmp, G, CMP_BLOCK * HEAD_DIM)
    return jax.nn.silu(flat @ w1) @ w2


def nsa_attention(q, k_cmp, v_cmp, k_sel, v_sel, k_win, v_win, gate_logits, rel_bias,
                  cmp_pos_k, cmp_w1_k, cmp_w2_k, cmp_pos_v, cmp_w1_v, cmp_w2_v):
    B, S = q.shape[0], q.shape[1]
    G, R, Dh = NSA_KV_GROUPS, NSA_GQA, HEAD_DIM
    scale = Dh ** -0.5
    qg = q.reshape(B, S, G, R, Dh)
    t = jnp.arange(S)

    n_cmp = (S - CMP_BLOCK) // CMP_STRIDE + 1
    blk_idx = np.arange(n_cmp)[:, None] * CMP_STRIDE + np.arange(CMP_BLOCK)[None, :]
    blk_end = jnp.asarray(blk_idx[:, -1])
    kc = compress_blocks(k_cmp, blk_idx, cmp_pos_k, cmp_w1_k, cmp_w2_k)
    vc = compress_blocks(v_cmp, blk_idx, cmp_pos_v, cmp_w1_v, cmp_w2_v)
    dist_c = t[:, None] - blk_end[None, :]
    bias_c = rel_bias[t5_bucket(dist_c)].reshape(S, n_cmp, G, R).transpose(2, 3, 0, 1)
    valid_c = dist_c >= 0
    logits_c = jnp.einsum('bsgrd,bngd->bgrsn', qg, kc).astype(jnp.float32) * scale
    logits_c = jnp.where(valid_c, logits_c + bias_c, NEG_BIG)
    p_c = jax.nn.softmax(logits_c, axis=-1) * valid_c
    o_c = jnp.einsum('bgrsn,bngd->bsgrd', p_c.astype(vc.dtype), vc)

    n_slc = S // SEL_BLOCK
    n_sel = min(N_SEL, n_slc)
    cs = np.arange(n_cmp)[:, None] * CMP_STRIDE
    sj = np.arange(n_slc)[None, :] * SEL_BLOCK
    overlap = ((cs < sj + SEL_BLOCK) & (cs + CMP_BLOCK > sj)).astype(np.float32)
    imp = jnp.einsum('bgrsn,nj->bgsj', p_c, jnp.asarray(overlap))
    q_blk = t // SEL_BLOCK
    j = jnp.arange(n_slc)
    forced = (j[None, :] == 0) | (j[None, :] == q_blk[:, None]) | (j[None, :] == q_blk[:, None] - 1)
    causal_blk = j[None, :] <= q_blk[:, None]
    sel_score = jnp.where(causal_blk, imp + FORCE_SCORE * forced, -FORCE_SCORE)
    _, sel_idx = lax.top_k(sel_score, n_sel)

    kT = k_sel.transpose(0, 2, 1, 3)
    vT = v_sel.transpose(0, 2, 1, 3)
    tbl = rel_bias.reshape(N_BUCKETS, G, R).transpose(1, 0, 2)
    b_ix = jnp.arange(B)[:, None, None, None]
    g_ix = jnp.arange(G)[None, :, None, None]
    n_chunks = S // SEL_Q_CHUNK
    n_tok = n_sel * SEL_BLOCK

    def sel_chunk(args):
        qc, ic, start = args
        tq = start + jnp.arange(SEL_Q_CHUNK)
        pos = (ic[..., None] * SEL_BLOCK + jnp.arange(SEL_BLOCK)).reshape(B, G, SEL_Q_CHUNK, n_tok)
        kg = kT[b_ix, g_ix, pos]
        vg = vT[b_ix, g_ix, pos]
        dist = tq[None, None, :, None] - pos
        bias = tbl[g_ix, t5_bucket(dist)]
        logits = jnp.einsum('bqgrd,bgqtd->bgrqt', qc, kg).astype(jnp.float32) * scale
        logits = jnp.where((dist >= 0)[:, :, None], logits + jnp.moveaxis(bias, -1, 2), -jnp.inf)
        p = jax.nn.softmax(logits, axis=-1)
        return jnp.einsum('bgrqt,bgqtd->bqgrd', p.astype(vg.dtype), vg)

    q_chunks = jnp.moveaxis(qg.reshape(B, n_chunks, SEL_Q_CHUNK, G, R, Dh), 1, 0)
    i_chunks = jnp.moveaxis(sel_idx.reshape(B, G, n_chunks, SEL_Q_CHUNK, n_sel), 2, 0)
    starts = jnp.arange(n_chunks) * SEL_Q_CHUNK
    o_s = lax.map(sel_chunk, (q_chunks, i_chunks, starts))
    o_s = jnp.moveaxis(o_s, 0, 1).reshape(B, S, G, R, Dh)

    nb = S // Q_BLOCK
    span = WINDOW + Q_BLOCK
    kp = jnp.pad(k_win, ((0, 0), (WINDOW, 0), (0, 0), (0, 0)))
    vp = jnp.pad(v_win, ((0, 0), (WINDOW, 0), (0, 0), (0, 0)))
    widx = np.arange(nb)[:, None] * Q_BLOCK + np.arange(span)[None, :]
    kw = kp[:, widx]
    vw = vp[:, widx]
    qw = qg.reshape(B, nb, Q_BLOCK, G, R, Dh)
    dist_w = np.arange(Q_BLOCK)[:, None] - np.arange(span)[None, :] + WINDOW
    key_pos = widx - WINDOW
    valid_w = ((dist_w >= 0) & (dist_w < WINDOW))[None] & (key_pos >= 0)[:, None, :]
    bias_w = rel_bias[t5_bucket(jnp.asarray(dist_w))].reshape(Q_BLOCK, span, G, R).transpose(2, 3, 0, 1)
    logits_w = jnp.einsum('bnqgrd,bnkgd->bngrqk', qw, kw).astype(jnp.float32) * scale
    logits_w = jnp.where(jnp.asarray(valid_w)[None, :, None, None], logits_w + bias_w, -jnp.inf)
    p_w = jax.nn.softmax(logits_w, axis=-1)
    o_w = jnp.einsum('bngrqk,bnkgd->bnqgrd', p_w.astype(vw.dtype), vw).reshape(B, S, G, R, Dh)

    gates = jax.nn.sigmoid(gate_logits.astype(jnp.float32)).reshape(B, S, G, R, 3).astype(q.dtype)
    o = gates[..., 0:1] * o_c + gates[..., 1:2] * o_s + gates[..., 2:3] * o_w
    return o.reshape(B, S, NSA_HEADS * Dh)


def moe_ffn(h, w_router, e_bias, w_gate, w_up, w_down, ws_gate, ws_up, ws_down):
    B, S, D = h.shape
    scores = jax.nn.sigmoid(jnp.einsum('bsd,de->bse', h, w_router).astype(jnp.float32))
    biased = scores + e_bias.astype(jnp.float32)
    grouped = biased.reshape(B, S, N_EXPERT_GROUPS, N_EXPERTS // N_EXPERT_GROUPS)
    group_score = lax.top_k(grouped, 2)[0].sum(-1)
    _, gidx = lax.top_k(group_score, TOPK_GROUPS)
    gmask = jax.nn.one_hot(gidx, N_EXPERT_GROUPS).sum(-2) > 0
    masked = jnp.where(gmask[..., None], grouped, -jnp.inf).reshape(B, S, N_EXPERTS)
    _, eidx = lax.top_k(masked, TOP_K)
    w = jnp.take_along_axis(scores, eidx, axis=-1)
    w = w / jnp.sum(w, axis=-1, keepdims=True) * ROUTED_SCALE
    gates = jnp.sum(jax.nn.one_hot(eidx, N_EXPERTS, dtype=jnp.float32) * w[..., None], axis=-2)

    def per_seq(args):
        xs, gs = args
        a = jnp.einsum('sd,edf->sef', xs, w_gate)
        u = jnp.einsum('sd,edf->sef', xs, w_up)
        act = jax.nn.silu(a) * u * gs[..., None].astype(xs.dtype)
        return jnp.einsum('sef,efd->sd', act, w_down)

    routed = lax.map(per_seq, (h, gates))
    shared = (jax.nn.silu(h @ ws_gate) * (h @ ws_up)) @ ws_down
    return routed + shared


def setup_inputs(seed: int = 0) -> dict:
    key = jax.random.key(seed)
    ks = jax.random.split(key, 32)
    f32 = jnp.float32

    def nrm(k, shape, scale):
        return jax.random.normal(k, shape, f32) * scale

    D = D_MODEL
    in_keys = jax.random.split(ks[4], len(IN_WIDTHS))
    w_in = jnp.concatenate(
        [nrm(kk, (DEPTH, D, wd), D ** -0.5 * (DEEPNORM_BETA if isv else 1.0))
         for kk, wd, isv in zip(in_keys, IN_WIDTHS, IN_IS_VALUE)], axis=-1)
    return {
        'x': nrm(ks[0], (BATCH, SEQ, D), 1.0),
        'c': nrm(ks[1], (BATCH, D), 1.0),
        'w_ada': nrm(ks[2], (DEPTH, D, 6 * D), D ** -0.5),
        'b_ada': nrm(ks[3], (DEPTH, 6 * D), 0.02),
        'w_in': w_in,
        'b_f': 3.0 + nrm(ks[5], (DEPTH, FOX_HEADS), 0.5),
        'cmp_pos_k': nrm(ks[6], (DEPTH, CMP_BLOCK, HEAD_DIM), 0.1),
        'cmp_w1_k': nrm(ks[7], (DEPTH, CMP_BLOCK * HEAD_DIM, CMP_HIDDEN), (CMP_BLOCK * HEAD_DIM) ** -0.5),
        'cmp_w2_k': nrm(ks[8], (DEPTH, CMP_HIDDEN, HEAD_DIM), CMP_HIDDEN ** -0.5),
        'cmp_pos_v': nrm(ks[9], (DEPTH, CMP_BLOCK, HEAD_DIM), 0.1),
        'cmp_w1_v': nrm(ks[10], (DEPTH, CMP_BLOCK * HEAD_DIM, CMP_HIDDEN), (CMP_BLOCK * HEAD_DIM) ** -0.5),
        'cmp_w2_v': nrm(ks[11], (DEPTH, CMP_HIDDEN, HEAD_DIM), CMP_HIDDEN ** -0.5),
        'rel_bias': nrm(ks[12], (N_BUCKETS, NSA_HEADS), 0.5),
        'w_out': nrm(ks[13], (DEPTH, D, D), D ** -0.5 * DEEPNORM_BETA),
        'ln1_g': 1.0 + nrm(ks[14], (DEPTH, D), 0.05),
        'ln1_b': nrm(ks[15], (DEPTH, D), 0.02),
        'w_router': nrm(ks[16], (DEPTH, D, N_EXPERTS), D ** -0.5),
        'e_bias': nrm(ks[17], (DEPTH, N_EXPERTS), 0.01),
        'w_gate': nrm(ks[18], (DEPTH, N_EXPERTS, D, D_EXPERT), D ** -0.5),
        'w_up': nrm(ks[19], (DEPTH, N_EXPERTS, D, D_EXPERT), D ** -0.5),
        'w_down': nrm(ks[20], (DEPTH, N_EXPERTS, D_EXPERT, D), D_EXPERT ** -0.5 * DEEPNORM_BETA),
        'ws_gate': nrm(ks[21], (DEPTH, D, D_SHARED), D ** -0.5),
        'ws_up': nrm(ks[22], (DEPTH, D, D_SHARED), D ** -0.5),
        'ws_down': nrm(ks[23], (DEPTH, D_SHARED, D), D_SHARED ** -0.5 * DEEPNORM_BETA),
        'ln2_g': 1.0 + nrm(ks[24], (DEPTH, D), 0.05),
        'ln2_b': nrm(ks[25], (DEPTH, D), 0.02),
    }


def reference(x, c, w_ada, b_ada, w_in, b_f, cmp_pos_k, cmp_w1_k, cmp_w2_k, cmp_pos_v, cmp_w1_v,
              cmp_w2_v, rel_bias, w_out, ln1_g, ln1_b, w_router, e_bias, w_gate, w_up, w_down,
              ws_gate, ws_up, ws_down, ln2_g, ln2_b):
    B, S, D = x.shape
    split_pts = [int(v) for v in np.cumsum(IN_WIDTHS)[:-1]]
    kv_shape = (B, S, NSA_KV_GROUPS, HEAD_DIM)
    for l in range(DEPTH):
        mod = jnp.einsum('bd,de->be', jax.nn.silu(c), w_ada[l]) + b_ada[l]
        sh1, sc1, g1, sh2, sc2, g2 = jnp.split(mod[:, None, :], 6, axis=-1)

        h = x * (1.0 + sc1) + sh1
        fq, fk, fv, ff, nq, kc, vc, ksl, vsl, kwn, vwn, ng = jnp.split(h @ w_in[l], split_pts, axis=-1)
        log_f = jax.nn.log_sigmoid(ff.astype(jnp.float32) + b_f[l].astype(jnp.float32))
        o_fox = forgetting_attention(fq.reshape(B, S, FOX_HEADS, HEAD_DIM),
                                     fk.reshape(B, S, FOX_HEADS, HEAD_DIM),
                                     fv.reshape(B, S, FOX_HEADS, HEAD_DIM), log_f)
        o_nsa = nsa_attention(nq.reshape(B, S, NSA_HEADS, HEAD_DIM),
                              kc.reshape(kv_shape), vc.reshape(kv_shape),
                              ksl.reshape(kv_shape), vsl.reshape(kv_shape),
                              kwn.reshape(kv_shape), vwn.reshape(kv_shape), ng, rel_bias,
                              cmp_pos_k[l], cmp_w1_k[l], cmp_w2_k[l],
                              cmp_pos_v[l], cmp_w1_v[l], cmp_w2_v[l])
        mixed = jnp.concatenate([o_fox.reshape(B, S, FOX_W), o_nsa], axis=-1) @ w_out[l]
        x = layer_norm(DEEPNORM_ALPHA * x + g1 * mixed, ln1_g[l], ln1_b[l])

        h = x * (1.0 + sc2) + sh2
        ffn = moe_ffn(h, w_router[l], e_bias[l], w_gate[l], w_up[l], w_down[l],
                      ws_gate[l], ws_up[l], ws_down[l])
        x = layer_norm(DEEPNORM_ALPHA * x + g2 * ffn, ln2_g[l], ln2_b[l])
    return x
```

```python
import functools
import math

import jax
import jax.numpy as jnp
import numpy as np
from jax import lax
from jax.experimental import pallas as pl
from jax.experimental.pallas import tpu as pltpu

F32 = jnp.float32
BF16 = jnp.bfloat16

HEAD_DIM = 64
FOX_HEADS = 8
NSA_HEADS = 8
NSA_GQA = 4
NSA_GROUPS = NSA_HEADS // NSA_GQA
CMP_BLOCK = 32
CMP_STRIDE = 16
CMP_HIDDEN = 256
SEL_BLOCK = 64
N_SEL = 16
WINDOW = 512
N_BUCKETS = 32
MAX_DISTANCE = 128
N_EXPERTS = 64
N_EXPERT_GROUPS = 8
GROUP_SIZE = N_EXPERTS // N_EXPERT_GROUPS
TOPK_GROUPS = 4
TOP_K = 8
D_EXPERT = 256
ROUTED_SCALE = 2.5
LN_EPS = 1e-5
NEG_BIG = -1e30
FORCE_SCORE = 1e4

LANES = 128
ATT_TILE = 256
ROW_TILE = 512
MOE_TILE = 1024
VMEM_LIMIT = 48 * 1024 * 1024

NT_DIMS = (((1,), (1,)), ((), ()))


def _dot(a, b):
    return jnp.dot(a, b, preferred_element_type=F32)


def _dot_nt(a, b):
    return lax.dot_general(a, b, NT_DIMS, preferred_element_type=F32)


def _split3(x):
    hi = x.astype(BF16)
    r1 = x - hi.astype(F32)
    mid = r1.astype(BF16)
    lo = (r1 - mid.astype(F32)).astype(BF16)
    return hi, mid, lo


def _silu(x):
    return x / (1.0 + jnp.exp(-x))


def _sigmoid(x):
    return 1.0 / (1.0 + jnp.exp(-x))


def _swap_halves(x):
    return pltpu.roll(x, HEAD_DIM, 1)


def _t5_bucket_np(dist):
    n = np.maximum(dist, 0)
    max_exact = N_BUCKETS // 2
    nf = np.maximum(n, 1).astype(np.float32)
    large = max_exact + (np.log(nf / max_exact) / math.log(MAX_DISTANCE / max_exact)
                         * (N_BUCKETS - max_exact)).astype(np.int32)
    large = np.minimum(large, N_BUCKETS - 1)
    return np.where(n < max_exact, n, large).astype(np.int32)


def _ada_kernel(c_ref, w_ref, b_ref, o_ref):
    c = c_ref[...]
    o_ref[...] = jnp.dot(_silu(c), w_ref[...], preferred_element_type=F32,
                         precision=lax.Precision.HIGHEST) + b_ref[...]


def _ada(c, w_ada, b_ada):
    B, D = c.shape
    n_out = w_ada.shape[1]
    tn = 1024
    return pl.pallas_call(
        _ada_kernel,
        grid=(n_out // tn,),
        in_specs=[pl.BlockSpec((B, D), lambda j: (0, 0)),
                  pl.BlockSpec((D, tn), lambda j: (0, j)),
                  pl.BlockSpec((1, tn), lambda j: (0, j))],
        out_specs=pl.BlockSpec((B, tn), lambda j: (0, j)),
        out_shape=jax.ShapeDtypeStruct((B, n_out), F32),
        compiler_params=pltpu.CompilerParams(dimension_semantics=("arbitrary",),
                                             vmem_limit_bytes=VMEM_LIMIT),
        name="ada",
    )(c, w_ada, b_ada.reshape(1, n_out))


_C_FQ, _C_FK, _C_FV, _C_NQ = 0, 512, 1024, 1536
_C_KV = 2048
_C_MISC = 2816
_IN_COLS = 2944


def _in_proj_kernel(x_ref, mod_ref, w_ref, bf_ref, fq_ref, fk_ref, fv_ref, nq_ref,
                    kc_ref, vc_ref, ks_ref, vs_ref, kw_ref, vw_ref, misc_ref, misct_ref,
                    carry_ref):
    s_idx = pl.program_id(1)
    tm = x_ref.shape[0]
    mod = mod_ref[...]
    h = (x_ref[...] * (1.0 + mod[1:2, :]) + mod[0:1, :]).astype(BF16)

    for ref, c0 in ((fq_ref, _C_FQ), (fk_ref, _C_FK), (fv_ref, _C_FV), (nq_ref, _C_NQ)):
        ref[...] = _dot(h, w_ref[:, c0:c0 + 512]).astype(ref.dtype)
    for k, ref in enumerate((kc_ref, vc_ref, ks_ref, vs_ref, kw_ref, vw_ref)):
        c0 = _C_KV + k * LANES
        ref[...] = _dot(h, w_ref[:, c0:c0 + LANES]).astype(ref.dtype)

    z = _dot(h, w_ref[:, _C_MISC:_C_MISC + LANES]) + bf_ref[...]
    lane = lax.broadcasted_iota(jnp.int32, z.shape, 1)
    is_f = lane < FOX_HEADS
    log_f = jnp.minimum(z, 0.0) - jnp.log(1.0 + jnp.exp(-jnp.abs(z)))
    log_f = jnp.where(is_f, log_f, 0.0)

    row = lax.broadcasted_iota(jnp.int32, (tm, tm), 0)
    col = lax.broadcasted_iota(jnp.int32, (tm, tm), 1)
    tri = jnp.where(row >= col, 1.0, 0.0).astype(BF16)
    hi, mid, lo = _split3(log_f)
    local = _dot(tri, hi) + _dot(tri, mid) + _dot(tri, lo)

    @pl.when(s_idx == 0)
    def _():
        carry_ref[...] = jnp.zeros_like(carry_ref)

    cum = local + carry_ref[...]
    carry_ref[...] = cum[tm - 1:tm, :]
    misc = jnp.where(is_f, cum, _sigmoid(z))
    misc_ref[...] = misc
    misct_ref[...] = misc.T


def _in_proj(x, mod, w_r, bf_row):
    B, S, D = x.shape
    tm = ROW_TILE
    big = lambda: pl.BlockSpec((None, tm, 512), lambda b, s: (b, s, 0))
    small = lambda: pl.BlockSpec((None, tm, LANES), lambda b, s: (b, s, 0))
    out_shape = ([jax.ShapeDtypeStruct((B, S, 512), BF16)] * 4
                 + [jax.ShapeDtypeStruct((B, S, LANES), BF16)] * 6
                 + [jax.ShapeDtypeStruct((B, S, LANES), F32),
                    jax.ShapeDtypeStruct((B, LANES, S), F32)])
    out_specs = ([big() for _ in range(4)] + [small() for _ in range(6)]
                 + [small(), pl.BlockSpec((None, LANES, tm), lambda b, s: (b, 0, s))])
    return pl.pallas_call(
        _in_proj_kernel,
        grid=(B, S // tm),
        in_specs=[pl.BlockSpec((None, tm, D), lambda b, s: (b, s, 0)),
                  pl.BlockSpec((None, 6, D), lambda b, s: (b, 0, 0)),
                  pl.BlockSpec((D, _IN_COLS), lambda b, s: (0, 0)),
                  pl.BlockSpec((1, LANES), lambda b, s: (0, 0))],
        out_specs=out_specs,
        out_shape=out_shape,
        scratch_shapes=[pltpu.VMEM((1, LANES), F32)],
        compiler_params=pltpu.CompilerParams(dimension_semantics=("parallel", "arbitrary"),
                                             vmem_limit_bytes=VMEM_LIMIT),
        name="in_proj",
    )(x, mod, w_r, bf_row)


def _flash_update(carry, s, vt):
    m, l, acc = carry
    m_new = jnp.maximum(m, jnp.max(s, axis=-1, keepdims=True))
    alpha = jnp.exp(m - m_new)
    p = jnp.exp(s - m_new)
    l = alpha * l + jnp.sum(p, axis=-1, keepdims=True)
    rows = acc.shape[0]
    pv = _dot(p.reshape(rows, s.shape[-1]).astype(BF16), vt)
    acc = alpha.reshape(rows, 1) * acc + pv
    return m_new, l, acc


def _fox_kernel(q_ref, k_ref, v_ref, cq_ref, ck_ref, o_ref):
    i = pl.program_id(2)
    tq = q_ref.shape[0]
    tk = tq
    q2 = q_ref[...]
    lane = lax.broadcasted_iota(jnp.int32, (tq, LANES), 1)
    row = lax.broadcasted_iota(jnp.int32, (tq, tk), 0)
    col = lax.broadcasted_iota(jnp.int32, (tq, tk), 1)
    out = jnp.zeros((tq, LANES), F32)
    for hh in range(2):
        in_half = (lane >= hh * HEAD_DIM) & (lane < (hh + 1) * HEAD_DIM)
        qh = jnp.where(in_half, q2.astype(F32), 0.0).astype(BF16)
        cq = cq_ref[:, hh:hh + 1]

        def logits(j, hh=hh, qh=qh, cq=cq):
            kt = k_ref[pl.ds(pl.multiple_of(j * tk, tk), tk), :]
            ck = ck_ref[hh, pl.ds(j, 1), :]
            return _dot_nt(qh, kt) + (cq - ck)

        def body(j, carry):
            vt = v_ref[pl.ds(pl.multiple_of(j * tk, tk), tk), :]
            return _flash_update(carry, logits(j), vt)

        init = (jnp.full((tq, 1), NEG_BIG, F32), jnp.zeros((tq, 1), F32),
                jnp.zeros((tq, LANES), F32))
        carry = lax.fori_loop(0, i, body, init)
        s = jnp.where(col <= row, logits(i), NEG_BIG)
        vt = v_ref[pl.ds(pl.multiple_of(i * tk, tk), tk), :]
        _, l, acc = _flash_update(carry, s, vt)
        out = jnp.where(in_half, acc / l, out)
    o_ref[...] = out.astype(o_ref.dtype)


def _fox(fq, fk, fv, cum_col, cum_row):
    B, S, W = fq.shape
    tq = ATT_TILE
    n_pairs = W // LANES
    return pl.pallas_call(
        _fox_kernel,
        grid=(B, n_pairs, S // tq),
        in_specs=[pl.BlockSpec((None, tq, LANES), lambda b, p, i: (b, i, p)),
                  pl.BlockSpec((None, S, LANES), lambda b, p, i: (b, 0, p)),
                  pl.BlockSpec((None, S, LANES), lambda b, p, i: (b, 0, p)),
                  pl.BlockSpec((None, None, tq, 2), lambda b, p, i: (b, p, i, 0)),
                  pl.BlockSpec((None, None, 2, S // tq, tq), lambda b, p, i: (b, p, 0, 0, 0))],
        out_specs=pl.BlockSpec((None, tq, LANES), lambda b, p, i: (b, i, p)),
        out_shape=jax.ShapeDtypeStruct((B, S, W), BF16),
        compiler_params=pltpu.CompilerParams(
            dimension_semantics=("parallel", "parallel", "arbitrary"),
            vmem_limit_bytes=VMEM_LIMIT),
        name="fox",
    )(fq, fk, fv, cum_col, cum_row)


def _compress_kernel(xk_ref, xv_ref, pk_ref, pv_ref, wak_ref, wbk_ref, wav_ref, wbv_ref,
                     w2k_ref, w2v_ref, ok_ref, ov_ref):
    n_rows = xk_ref.shape[0]
    for x_ref, p_ref, wa_ref, wb_ref, w2_ref, o_ref in (
            (xk_ref, pk_ref, wak_ref, wbk_ref, w2k_ref, ok_ref),
            (xv_ref, pv_ref, wav_ref, wbv_ref, w2v_ref, ov_ref)):
        x = x_ref[...].astype(F32)
        xa = (x + p_ref[0:1, :]).astype(BF16)
        xb = (x + p_ref[1:2, :]).astype(BF16)
        hb = _dot(xb, wb_ref[...])
        h1 = _dot(xa, wa_ref[...]) + pltpu.roll(hb, n_rows - 1, 0)
        o_ref[...] = _dot(_silu(h1).astype(BF16), w2_ref[...]).astype(o_ref.dtype)


def _compress(xk, xv, pk, pv, wak, wbk, wav, wbv, w2k, w2v):
    B, R, C = xk.shape
    xspec = pl.BlockSpec((None, R, C), lambda b: (b, 0, 0))
    full = lambda a: pl.BlockSpec(a.shape, lambda b: (0,) * a.ndim)
    ospec = pl.BlockSpec((None, R, LANES), lambda b: (b, 0, 0))
    return pl.pallas_call(
        _compress_kernel,
        grid=(B,),
        in_specs=[xspec, xspec] + [full(a) for a in (pk, pv, wak, wbk, wav, wbv, w2k, w2v)],
        out_specs=[ospec, ospec],
        out_shape=[jax.ShapeDtypeStruct((B, R, LANES), BF16)] * 2,
        compiler_params=pltpu.CompilerParams(dimension_semantics=("parallel",),
                                             vmem_limit_bytes=VMEM_LIMIT),
        name="compress",
    )(xk, xv, pk, pv, wak, wbk, wav, wbv, w2k, w2v)


def _rank_rows(score):
    n = score.shape[0]
    j = lax.broadcasted_iota(jnp.int32, score.shape, 0)
    rank = jnp.zeros(score.shape, jnp.int32)
    for i in range(n):
        si = score[i:i + 1, :]
        beats = (si > score) | ((si == score) & (j > i))
        rank = rank + jnp.where(beats, 1, 0)
    return rank


def _dup_head(q4, r):
    pair = q4[:, (r // 2) * LANES:(r // 2 + 1) * LANES].astype(F32)
    lane = lax.broadcasted_iota(jnp.int32, pair.shape, 1)
    swapped = _swap_halves(pair)
    if r % 2 == 0:
        return jnp.where(lane < HEAD_DIM, pair, swapped)
    return jnp.where(lane < HEAD_DIM, swapped, pair)


def _pack_heads(o_list, g):
    lane = lax.broadcasted_iota(jnp.int32, o_list[0].shape, 1)
    in_g = (lane >= g * HEAD_DIM) & (lane < (g + 1) * HEAD_DIM)
    both = []
    for o in o_list:
        om = jnp.where(in_g, o, 0.0)
        both.append(om + _swap_halves(om))
    pairs = [jnp.where(lane < HEAD_DIM, both[2 * p], both[2 * p + 1]) for p in range(2)]
    return jnp.concatenate(pairs, axis=1)


def _cmp_sel_kernel(q_ref, kc_ref, vc_ref, bias_ref, gate_ref, ovl_ref, oc_ref, sel_ref):
    g = pl.program_id(1)
    i = pl.program_id(2)
    tq = q_ref.shape[0]
    n_pad = kc_ref.shape[0]
    q4 = q_ref[...]
    lane = lax.broadcasted_iota(jnp.int32, (n_pad, LANES), 1)
    in_g = (lane >= g * HEAD_DIM) & (lane < (g + 1) * HEAD_DIM)
    kc = jnp.where(in_g, kc_ref[...].astype(F32), 0.0).astype(BF16)
    vc = vc_ref[...]
    t = i * tq + lax.broadcasted_iota(jnp.int32, (tq, n_pad), 0)
    n = lax.broadcasted_iota(jnp.int32, (tq, n_pad), 1)
    valid = t >= n * CMP_STRIDE + (CMP_BLOCK - 1)
    gates = gate_ref[...]
    p_sum = jnp.zeros((tq, n_pad), F32)
    outs = []
    for r in range(NSA_GQA):
        qr = _dup_head(q4, r).astype(BF16)
        s = _dot_nt(qr, kc)
        s = jnp.where(valid, s + bias_ref[r], NEG_BIG)
        m = jnp.max(s, axis=-1, keepdims=True)
        p = jnp.exp(s - m)
        p = p / jnp.sum(p, axis=-1, keepdims=True)
        p = jnp.where(valid, p, 0.0)
        p_sum = p_sum + p
        outs.append(_dot(p.astype(BF16), vc) * gates[:, 3 * r:3 * r + 1])
    oc_ref[...] = _pack_heads(outs, g).astype(oc_ref.dtype)

    ovl = ovl_ref[...]
    hi, mid, lo = _split3(p_sum)
    imp = _dot_nt(ovl, hi) + _dot_nt(ovl, mid) + _dot_nt(ovl, lo)
    n_blk = imp.shape[0]
    j = lax.broadcasted_iota(jnp.int32, (n_blk, tq), 0)
    qb = jnp.right_shift(i * tq + lax.broadcasted_iota(jnp.int32, (n_blk, tq), 1),
                         int(math.log2(SEL_BLOCK)))
    forced = (j == 0) | (j == qb) | (j == qb - 1)
    causal = j <= qb
    score = jnp.where(causal, imp + jnp.where(forced, FORCE_SCORE, 0.0), -FORCE_SCORE)
    chosen = (_rank_rows(score) < N_SEL) & causal
    sel = jnp.where(chosen, 1.0, 0.0)
    sel = jnp.concatenate([sel, jnp.zeros((LANES - n_blk, tq), F32)], axis=0)
    sel_ref[...] = sel.T


def _cmp_sel(nq, kcmp, vcmp, bias_c, gates_g, ovl_t):
    B, S, _ = nq.shape
    tq = ATT_TILE
    n_pad = kcmp.shape[1]
    return pl.pallas_call(
        _cmp_sel_kernel,
        grid=(B, NSA_GROUPS, S // tq),
        in_specs=[pl.BlockSpec((None, tq, 2 * LANES), lambda b, g, i: (b, i, g)),
                  pl.BlockSpec((None, n_pad, LANES), lambda b, g, i: (b, 0, 0)),
                  pl.BlockSpec((None, n_pad, LANES), lambda b, g, i: (b, 0, 0)),
                  pl.BlockSpec((NSA_GQA, tq, n_pad), lambda b, g, i: (g, i, 0)),
                  pl.BlockSpec((None, None, tq, 3 * NSA_GQA), lambda b, g, i: (b, g, i, 0)),
                  pl.BlockSpec(ovl_t.shape, lambda b, g, i: (0, 0))],
        out_specs=[pl.BlockSpec((None, tq, 2 * LANES), lambda b, g, i: (b, i, g)),
                   pl.BlockSpec((None, None, tq, LANES), lambda b, g, i: (b, g, i, 0))],
        out_shape=[jax.ShapeDtypeStruct((B, S, NSA_HEADS * HEAD_DIM), BF16),
                   jax.ShapeDtypeStruct((B, NSA_GROUPS, S, LANES), F32)],
        compiler_params=pltpu.CompilerParams(
            dimension_semantics=("parallel", "parallel", "arbitrary"),
            vmem_limit_bytes=VMEM_LIMIT),
        name="cmp_sel",
    )(nq, kcmp, vcmp, bias_c, gates_g, ovl_t)


def _nsa_kernel(q_ref, ks_ref, vs_ref, kw_ref, vw_ref, sel_ref, exp_ref, w3_ref, gate_ref,
                oc_ref, o_ref, madd_ref):
    g = pl.program_id(1)
    i = pl.program_id(2)
    tq = q_ref.shape[0]
    tk = tq
    n_kt = madd_ref.shape[0]
    H = NSA_GQA
    q4 = q_ref[...]
    lane = lax.broadcasted_iota(jnp.int32, (tq, LANES), 1)
    in_g = (lane >= g * HEAD_DIM) & (lane < (g + 1) * HEAD_DIM)
    qs = jnp.concatenate(
        [jnp.where(in_g, _dup_head(q4, r), 0.0).astype(BF16) for r in range(H)], axis=0)

    sel = sel_ref[...].astype(BF16)
    for jj in range(n_kt):
        hit = _dot(sel, exp_ref[:, jj * tk:(jj + 1) * tk])
        madd_ref[jj] = (hit - 1.0) * (-NEG_BIG)

    row = lax.broadcasted_iota(jnp.int32, (tq, tk), 0)
    col = lax.broadcasted_iota(jnp.int32, (tq, tk), 1)

    def logits(k_ref, j):
        kt = k_ref[pl.ds(pl.multiple_of(j * tk, tk), tk), :]
        d = jnp.minimum(i - j, 2)
        return _dot_nt(qs, kt).reshape(H, tq, tk) + w3_ref[:, pl.ds(d, 1)].reshape(H, tq, tk)

    def vtile(v_ref, j):
        return v_ref[pl.ds(pl.multiple_of(j * tk, tk), tk), :]

    def init():
        return (jnp.full((H, tq, 1), NEG_BIG, F32), jnp.zeros((H, tq, 1), F32),
                jnp.zeros((H * tq, LANES), F32))

    def sel_body(j, carry):
        s = logits(ks_ref, j) + madd_ref[j][None]
        return _flash_update(carry, s, vtile(vs_ref, j))

    carry = lax.fori_loop(0, i, sel_body, init())
    s = logits(ks_ref, i) + jnp.where(col <= row, madd_ref[i], NEG_BIG)[None]
    _, l_s, acc_s = _flash_update(carry, s, vtile(vs_ref, i))

    carry = init()
    j2 = jnp.maximum(i - 2, 0)
    s = jnp.where(((col > row) & (i >= 2))[None], logits(kw_ref, j2), NEG_BIG)
    carry = _flash_update(carry, s, vtile(vw_ref, j2))
    j1 = jnp.maximum(i - 1, 0)
    s = jnp.where(i >= 1, logits(kw_ref, j1), NEG_BIG)
    carry = _flash_update(carry, s, vtile(vw_ref, j1))
    s = jnp.where((col <= row)[None], logits(kw_ref, i), NEG_BIG)
    _, l_w, acc_w = _flash_update(carry, s, vtile(vw_ref, i))

    gates = gate_ref[...]
    o_s = acc_s / l_s.reshape(H * tq, 1)
    o_w = acc_w / l_w.reshape(H * tq, 1)
    outs = []
    for r in range(H):
        sl = slice(r * tq, (r + 1) * tq)
        outs.append(o_s[sl] * gates[:, 3 * r + 1:3 * r + 2] + o_w[sl] * gates[:, 3 * r + 2:3 * r + 3])
    o_ref[...] = (_pack_heads(outs, g) + oc_ref[...].astype(F32)).astype(o_ref.dtype)


def _nsa(nq, ks, vs, kw, vw, sel, expand, w3, gates_g, oc):
    B, S, _ = nq.shape
    tq = ATT_TILE
    kv = lambda: pl.BlockSpec((None, S, LANES), lambda b, g, i: (b, 0, 0))
    return pl.pallas_call(
        _nsa_kernel,
        grid=(B, NSA_GROUPS, S // tq),
        in_specs=[pl.BlockSpec((None, tq, 2 * LANES), lambda b, g, i: (b, i, g)),
                  kv(), kv(), kv(), kv(),
                  pl.BlockSpec((None, None, tq, LANES), lambda b, g, i: (b, g, i, 0)),
                  pl.BlockSpec(expand.shape, lambda b, g, i: (0, 0)),
                  pl.BlockSpec((NSA_GQA, 3, tq, tq), lambda b, g, i: (g, 0, 0, 0)),
                  pl.BlockSpec((None, None, tq, 3 * NSA_GQA), lambda b, g, i: (b, g, i, 0)),
                  pl.BlockSpec((None, tq, 2 * LANES), lambda b, g, i: (b, i, g))],
        out_specs=pl.BlockSpec((None, tq, 2 * LANES), lambda b, g, i: (b, i, g)),
        out_shape=jax.ShapeDtypeStruct((B, S, NSA_HEADS * HEAD_DIM), BF16),
        scratch_shapes=[pltpu.VMEM((S // tq, tq, tq), F32)],
        compiler_params=pltpu.CompilerParams(
            dimension_semantics=("parallel", "parallel", "arbitrary"),
            vmem_limit_bytes=VMEM_LIMIT),
        name="nsa",
    )(nq, ks, vs, kw, vw, sel, expand, w3, gates_g, oc)


def _layer_norm(y, g, b):
    mu = jnp.mean(y, axis=-1, keepdims=True)
    yc = y - mu
    var = jnp.mean(yc * yc, axis=-1, keepdims=True)
    return yc * lax.rsqrt(var + LN_EPS) * g + b


def _router_gates_t(h2, wr_t, eb_col):
    tm = h2.shape[0]
    h_parts = _split3(h2)
    w_parts = _split3(wr_t)
    logit = jnp.zeros((N_EXPERTS, tm), F32)
    for a, wp in enumerate(w_parts):
        for b, hp in enumerate(h_parts):
            if a + b <= 2:
                logit = logit + _dot_nt(wp, hp)
    scores = _sigmoid(logit)
    biased = scores + eb_col
    e_in = lax.broadcasted_iota(jnp.int32, (GROUP_SIZE, tm), 0).astype(F32)
    gs_rows = []
    for gi in range(N_EXPERT_GROUPS):
        grp = biased[gi * GROUP_SIZE:(gi + 1) * GROUP_SIZE, :]
        m1 = jnp.max(grp, axis=0, keepdims=True)
        first = jnp.min(jnp.where(grp == m1, e_in, float(GROUP_SIZE)), axis=0, keepdims=True)
        m2 = jnp.max(jnp.where(e_in == first, -jnp.inf, grp), axis=0, keepdims=True)
        gs_rows.append(m1 + m2)
    gscore = jnp.concatenate(gs_rows, axis=0)
    g_keep = _rank_rows(gscore) < TOPK_GROUPS
    keep = jnp.concatenate(
        [jnp.broadcast_to(g_keep[gi:gi + 1, :], (GROUP_SIZE, tm)) for gi in range(N_EXPERT_GROUPS)],
        axis=0)
    masked = jnp.where(keep, biased, -jnp.inf)
    chosen = _rank_rows(masked) < TOP_K
    w = jnp.where(chosen, scores, 0.0)
    return w / jnp.sum(w, axis=0, keepdims=True) * ROUTED_SCALE


def _out_proj_kernel(alpha, of_ref, on_ref, x_ref, mod_ref, w_ref, lg_ref, lb_ref, wr_ref,
                     eb_ref, x1_ref, h2_ref, gate_ref):
    half = of_ref.shape[1]
    mod = mod_ref[...]
    mixed = _dot(of_ref[...], w_ref[0:half, :]) + _dot(on_ref[...], w_ref[half:2 * half, :])
    y = alpha * x_ref[...] + mod[2:3, :] * mixed
    x1 = _layer_norm(y, lg_ref[...], lb_ref[...])
    x1_ref[...] = x1
    h2 = x1 * (1.0 + mod[4:5, :]) + mod[3:4, :]
    h2_ref[...] = h2.astype(h2_ref.dtype)
    gates_t = _router_gates_t(h2, wr_ref[...], eb_ref[...])
    tm = h2.shape[0]
    gates_t = jnp.concatenate([gates_t, jnp.zeros((LANES - N_EXPERTS, tm), F32)], axis=0)
    gate_ref[...] = gates_t.T


def _out_proj(alpha, o_fox, o_nsa, x, mod, w_out, ln_g, ln_b, wr_t, eb_col):
    B, S, D = x.shape
    tm = ROW_TILE
    half = o_fox.shape[-1]
    row = lambda a: pl.BlockSpec(a.shape, lambda b, s: (0, 0))
    return pl.pallas_call(
        functools.partial(_out_proj_kernel, alpha),
        grid=(B, S // tm),
        in_specs=[pl.BlockSpec((None, tm, half), lambda b, s: (b, s, 0)),
                  pl.BlockSpec((None, tm, half), lambda b, s: (b, s, 0)),
                  pl.BlockSpec((None, tm, D), lambda b, s: (b, s, 0)),
                  pl.BlockSpec((None, 6, D), lambda b, s: (b, 0, 0)),
                  row(w_out), row(ln_g), row(ln_b), row(wr_t), row(eb_col)],
        out_specs=[pl.BlockSpec((None, tm, D), lambda b, s: (b, s, 0)),
                   pl.BlockSpec((None, tm, D), lambda b, s: (b, s, 0)),
                   pl.BlockSpec((None, tm, LANES), lambda b, s: (b, s, 0))],
        out_shape=[jax.ShapeDtypeStruct((B, S, D), F32),
                   jax.ShapeDtypeStruct((B, S, D), BF16),
                   jax.ShapeDtypeStruct((B, S, LANES), F32)],
        compiler_params=pltpu.CompilerParams(dimension_semantics=("parallel", "parallel"),
                                             vmem_limit_bytes=VMEM_LIMIT),
        name="out_proj",
    )(o_fox, o_nsa, x, mod, w_out, ln_g, ln_b, wr_t, eb_col)


def _moe_kernel(alpha, h_ref, x1_ref, gate_ref, mod_ref, wgu_ref, wd_ref, sgu_ref, sd_ref,
                lg_ref, lb_ref, o_ref, acc_ref):
    e = pl.program_id(1)
    n_e = pl.num_programs(1)
    h = h_ref[...]
    f = wd_ref.shape[0]

    def expert(wgu, wd, gate):
        a = _dot(h, wgu)
        act = _silu(a[:, :f]) * a[:, f:]
        if gate is not None:
            act = act * gate
        return _dot(act.astype(BF16), wd)

    @pl.when(e == 0)
    def _():
        acc_ref[...] = expert(sgu_ref[...], sd_ref[...], None)

    gates = gate_ref[...]
    lane = lax.broadcasted_iota(jnp.int32, gates.shape, 1)
    g_e = jnp.sum(jnp.where(lane == e, gates, 0.0), axis=-1, keepdims=True)
    acc_ref[...] += expert(wgu_ref[...], wd_ref[...], g_e)

    @pl.when(e == n_e - 1)
    def _():
        mod = mod_ref[...]
        y = alpha * x1_ref[...] + mod[5:6, :] * acc_ref[...]
        o_ref[...] = _layer_norm(y, lg_ref[...], lb_ref[...])


def _moe(alpha, h2, x1, gates, mod, wgu, wd, sgu, sd, ln_g, ln_b):
    B, S, D = x1.shape
    T = B * S
    tm = MOE_TILE
    per_b = S // tm
    h2 = h2.reshape(T, D)
    x1 = x1.reshape(T, D)
    gates = gates.reshape(T, LANES)
    n_e, _, f2 = wgu.shape
    f = wd.shape[1]
    row = lambda a: pl.BlockSpec(a.shape, lambda t, e: (0, 0))
    out = pl.pallas_call(
        functools.partial(_moe_kernel, alpha),
        grid=(T // tm, n_e),
        in_specs=[pl.BlockSpec((tm, D), lambda t, e: (t, 0)),
                  pl.BlockSpec((tm, D), lambda t, e: (t, 0)),
                  pl.BlockSpec((tm, LANES), lambda t, e: (t, 0)),
                  pl.BlockSpec((None, 6, D), lambda t, e: (t // per_b, 0, 0)),
                  pl.BlockSpec((None, D, f2), lambda t, e: (e, 0, 0)),
                  pl.BlockSpec((None, f, D), lambda t, e: (e, 0, 0)),
                  row(sgu), row(sd), row(ln_g), row(ln_b)],
        out_specs=pl.BlockSpec((tm, D), lambda t, e: (t, 0)),
        out_shape=jax.ShapeDtypeStruct((T, D), F32),
        scratch_shapes=[pltpu.VMEM((tm, D), F32)],
        compiler_params=pltpu.CompilerParams(dimension_semantics=("parallel", "arbitrary"),
                                             vmem_limit_bytes=VMEM_LIMIT),
        name="moe",
    )(h2, x1, gates, mod, wgu, wd, sgu, sd, ln_g, ln_b)
    return out.reshape(B, S, D)


def _rearrange_w_in(w):
    scale = HEAD_DIM ** -0.5
    fq, fk, fv = w[:, 0:512], w[:, 512:1024], w[:, 1024:1536]
    ff = w[:, 1536:1544]
    nq = w[:, 1544:2056]
    kv = w[:, 2056:2824]
    ng = w[:, 2824:2848]
    pad = jnp.zeros((w.shape[0], LANES - ff.shape[1] - ng.shape[1]), w.dtype)
    return jnp.concatenate([fq * scale, fk, fv, nq * scale, kv, ff, ng, pad], axis=1).astype(BF16)


def _compress_weights(pos, w1, w2):
    half = CMP_BLOCK // 2
    w1r = w1.reshape(2, half, HEAD_DIM, CMP_HIDDEN)
    zeros = jnp.zeros_like(w1r[0])
    def spread(part):
        g0 = jnp.stack([part, zeros], axis=1).reshape(half * 2 * HEAD_DIM, CMP_HIDDEN)
        g1 = jnp.stack([zeros, part], axis=1).reshape(half * 2 * HEAD_DIM, CMP_HIDDEN)
        return jnp.concatenate([g0, g1], axis=1).astype(BF16)
    wa, wb = spread(w1r[0]), spread(w1r[1])
    z2 = jnp.zeros_like(w2)
    w2bd = jnp.concatenate([jnp.concatenate([w2, z2], axis=1),
                            jnp.concatenate([z2, w2], axis=1)], axis=0).astype(BF16)
    posr = pos.reshape(2, half, 1, HEAD_DIM)
    posr = jnp.broadcast_to(posr, (2, half, NSA_GROUPS, HEAD_DIM)).reshape(2, half * 2 * HEAD_DIM)
    return posr, wa, wb, w2bd


@functools.lru_cache(maxsize=None)
def _static_tables(S):
    tq = ATT_TILE
    n_cmp = (S - CMP_BLOCK) // CMP_STRIDE + 1
    n_pad = S // CMP_STRIDE
    n_slc = S // SEL_BLOCK
    t = np.arange(S)[:, None]
    n = np.arange(n_pad)[None, :]
    bucket_c = _t5_bucket_np(t - (n * CMP_STRIDE + CMP_BLOCK - 1))
    d = np.arange(3)[:, None, None] * tq + np.arange(tq)[None, :, None] - np.arange(tq)[None, None, :]
    bucket_w = _t5_bucket_np(d)
    cs = np.arange(n_pad)[None, :] * CMP_STRIDE
    sj = np.arange(n_slc)[:, None] * SEL_BLOCK
    ovl_t = ((cs < sj + SEL_BLOCK) & (cs + CMP_BLOCK > sj) & (np.arange(n_pad)[None, :] < n_cmp))
    expand = (np.arange(S)[None, :] // SEL_BLOCK) == np.arange(LANES)[:, None]
    return bucket_c, bucket_w, ovl_t.astype(np.float32), expand.astype(np.float32)


def kernel(x, c, w_ada, b_ada, w_in, b_f, cmp_pos_k, cmp_w1_k, cmp_w2_k, cmp_pos_v, cmp_w1_v,
           cmp_w2_v, rel_bias, w_out, ln1_g, ln1_b, w_router, e_bias, w_gate, w_up, w_down,
           ws_gate, ws_up, ws_down, ln2_g, ln2_b):
    B, S, D = x.shape
    depth = w_ada.shape[0]
    alpha = (2 * depth) ** 0.25
    tq = ATT_TILE
    bucket_c, bucket_w, ovl_t, expand = _static_tables(S)
    bias_c = jnp.transpose(rel_bias[bucket_c], (2, 0, 1))
    w3 = jnp.transpose(rel_bias[bucket_w], (3, 0, 1, 2))
    ovl_t = jnp.asarray(ovl_t, BF16)
    expand = jnp.asarray(expand, BF16)

    for l in range(depth):
        mod = _ada(c, w_ada[l], b_ada[l]).reshape(B, 6, D)
        bf_row = jnp.zeros((1, LANES), F32).at[0, :FOX_HEADS].set(b_f[l])
        (fq, fk, fv, nq, kc, vc, ks, vs, kw, vw, misc, misc_t) = _in_proj(
            x, mod, _rearrange_w_in(w_in[l]), bf_row)

        cum_col = misc[:, :, :FOX_HEADS].reshape(B, S, FOX_HEADS // 2, 2).transpose(0, 2, 1, 3)
        cum_row = misc_t[:, :FOX_HEADS, :].reshape(B, FOX_HEADS // 2, 2, S // tq, tq)
        o_fox = _fox(fq, fk, fv, cum_col, cum_row)

        pk, wak, wbk, w2k = _compress_weights(cmp_pos_k[l], cmp_w1_k[l], cmp_w2_k[l])
        pv, wav, wbv, w2v = _compress_weights(cmp_pos_v[l], cmp_w1_v[l], cmp_w2_v[l])
        rows = S // CMP_STRIDE
        kcmp, vcmp = _compress(kc.reshape(B, rows, CMP_STRIDE * LANES),
                               vc.reshape(B, rows, CMP_STRIDE * LANES),
                               pk, pv, wak, wbk, wav, wbv, w2k, w2v)

        gates_g = misc[:, :, FOX_HEADS:FOX_HEADS + 3 * NSA_HEADS].reshape(
            B, S, NSA_GROUPS, 3 * NSA_GQA).transpose(0, 2, 1, 3)
        oc, sel = _cmp_sel(nq, kcmp, vcmp, bias_c, gates_g, ovl_t)
        o_nsa = _nsa(nq, ks, vs, kw, vw, sel, expand, w3, gates_g, oc)

        x1, h2, gates = _out_proj(alpha, o_fox, o_nsa, x, mod, w_out[l].astype(BF16),
                                  ln1_g[l].reshape(1, D), ln1_b[l].reshape(1, D),
                                  w_router[l].T, e_bias[l].reshape(N_EXPERTS, 1))

        wgu = jnp.concatenate([w_gate[l], w_up[l]], axis=-1).astype(BF16)
        sgu = jnp.concatenate([ws_gate[l], ws_up[l]], axis=-1).astype(BF16)
        x = _moe(alpha, h2, x1, gates, mod, wgu, w_down[l].astype(BF16), sgu,
                 ws_down[l].astype(BF16), ln2_g[l].reshape(1, D), ln2_b[l].reshape(1, D))
    return x
```

```python
import functools
import math

import jax
import jax.numpy as jnp
import numpy as np
from jax import lax
from jax.experimental import pallas as pl
from jax.experimental.pallas import tpu as pltpu

F32 = jnp.float32
BF16 = jnp.bfloat16

HEAD_DIM = 64
FOX_HEADS = 8
NSA_HEADS = 8
NSA_GQA = 4
NSA_GROUPS = NSA_HEADS // NSA_GQA
CMP_BLOCK = 32
CMP_STRIDE = 16
CMP_HIDDEN = 256
SEL_BLOCK = 64
N_SEL = 16
WINDOW = 512
N_BUCKETS = 32
MAX_DISTANCE = 128
N_EXPERTS = 64
N_EXPERT_GROUPS = 8
GROUP_SIZE = N_EXPERTS // N_EXPERT_GROUPS
TOPK_GROUPS = 4
TOP_K = 8
D_EXPERT = 256
ROUTED_SCALE = 2.5
LN_EPS = 1e-5
NEG_BIG = -1e30
FORCE_SCORE = 1e4

LANES = 128
ATT_TILE = 256
FOX_TILE = 512
ROW_TILE = 512
MOE_TILE = 1024
VMEM_LIMIT = 48 * 1024 * 1024

NT_DIMS = (((1,), (1,)), ((), ()))


def _dot(a, b):
    return jnp.dot(a, b, preferred_element_type=F32)


def _dot_nt(a, b):
    return lax.dot_general(a, b, NT_DIMS, preferred_element_type=F32)


def _split3(x):
    hi = x.astype(BF16)
    r1 = x - hi.astype(F32)
    mid = r1.astype(BF16)
    lo = (r1 - mid.astype(F32)).astype(BF16)
    return hi, mid, lo


def _silu(x):
    return x / (1.0 + jnp.exp(-x))


def _sigmoid(x):
    return 1.0 / (1.0 + jnp.exp(-x))


def _swap_halves(x):
    return pltpu.roll(x, HEAD_DIM, 1)


def _t5_bucket_np(dist):
    n = np.maximum(dist, 0)
    max_exact = N_BUCKETS // 2
    nf = np.maximum(n, 1).astype(np.float32)
    large = max_exact + (np.log(nf / max_exact) / math.log(MAX_DISTANCE / max_exact)
                         * (N_BUCKETS - max_exact)).astype(np.int32)
    large = np.minimum(large, N_BUCKETS - 1)
    return np.where(n < max_exact, n, large).astype(np.int32)


def _ada_kernel(c_ref, w_ref, b_ref, o_ref):
    c = c_ref[...]
    o_ref[...] = jnp.dot(_silu(c), w_ref[...], preferred_element_type=F32,
                         precision=lax.Precision.HIGHEST) + b_ref[...]


def _ada(c, w_ada, b_ada):
    B, D = c.shape
    n_out = w_ada.shape[1]
    tn = 1024
    return pl.pallas_call(
        _ada_kernel,
        grid=(n_out // tn,),
        in_specs=[pl.BlockSpec((B, D), lambda j: (0, 0)),
                  pl.BlockSpec((D, tn), lambda j: (0, j)),
                  pl.BlockSpec((1, tn), lambda j: (0, j))],
        out_specs=pl.BlockSpec((B, tn), lambda j: (0, j)),
        out_shape=jax.ShapeDtypeStruct((B, n_out), F32),
        compiler_params=pltpu.CompilerParams(dimension_semantics=("arbitrary",),
                                             vmem_limit_bytes=VMEM_LIMIT),
        name="ada",
    )(c, w_ada, b_ada.reshape(1, n_out))


def _bias_table_kernel(bkt_ref, rbt_ref, o_ref):
    bkt = bkt_ref[...]
    k = lax.broadcasted_iota(jnp.int32, (N_BUCKETS, bkt.shape[1]), 0)
    onehot = jnp.where(k == bkt, 1.0, 0.0).astype(BF16)
    hi, mid, lo = _split3(rbt_ref[...])
    o_ref[...] = _dot(hi, onehot) + _dot(mid, onehot) + _dot(lo, onehot)


def _bias_table(bucket, rel_bias_t):
    n = bucket.shape[1]
    chunk = 8192
    n_heads = rel_bias_t.shape[0]
    return pl.pallas_call(
        _bias_table_kernel,
        grid=(n // chunk,),
        in_specs=[pl.BlockSpec((1, chunk), lambda j: (0, j)),
                  pl.BlockSpec(rel_bias_t.shape, lambda j: (0, 0))],
        out_specs=pl.BlockSpec((n_heads, chunk), lambda j: (0, j)),
        out_shape=jax.ShapeDtypeStruct((n_heads, n), F32),
        compiler_params=pltpu.CompilerParams(dimension_semantics=("parallel",),
                                             vmem_limit_bytes=VMEM_LIMIT),
        name="bias_table",
    )(bucket, rel_bias_t)


_C_FQ, _C_FK, _C_FV, _C_NQ = 0, 512, 1024, 1536
_C_KV = 2048
_C_MISC = 2816
_IN_COLS = 2944


def _in_proj_kernel(x_ref, mod_ref, w_ref, bf_ref, fq_ref, fk_ref, fv_ref, nq_ref,
                    kc_ref, vc_ref, ks_ref, vs_ref, kw_ref, vw_ref, misc_ref, misct_ref,
                    carry_ref):
    s_idx = pl.program_id(1)
    tm = x_ref.shape[0]
    mod = mod_ref[...]
    h = (x_ref[...] * (1.0 + mod[1:2, :]) + mod[0:1, :]).astype(BF16)

    for ref, c0 in ((fq_ref, _C_FQ), (fk_ref, _C_FK), (fv_ref, _C_FV), (nq_ref, _C_NQ)):
        ref[...] = _dot(h, w_ref[:, c0:c0 + 512]).astype(ref.dtype)
    for k, ref in enumerate((kc_ref, vc_ref, ks_ref, vs_ref, kw_ref, vw_ref)):
        c0 = _C_KV + k * LANES
        ref[...] = _dot(h, w_ref[:, c0:c0 + LANES]).astype(ref.dtype)

    z = _dot(h, w_ref[:, _C_MISC:_C_MISC + LANES]) + bf_ref[...]
    lane = lax.broadcasted_iota(jnp.int32, z.shape, 1)
    is_f = lane < FOX_HEADS
    log_f = jnp.minimum(z, 0.0) - jnp.log(1.0 + jnp.exp(-jnp.abs(z)))
    log_f = jnp.where(is_f, log_f, 0.0)

    row = lax.broadcasted_iota(jnp.int32, (tm, tm), 0)
    col = lax.broadcasted_iota(jnp.int32, (tm, tm), 1)
    tri = jnp.where(row >= col, 1.0, 0.0).astype(BF16)
    hi, mid, lo = _split3(log_f)
    local = _dot(tri, hi) + _dot(tri, mid) + _dot(tri, lo)

    @pl.when(s_idx == 0)
    def _():
        carry_ref[...] = jnp.zeros_like(carry_ref)

    cum = local + carry_ref[...]
    carry_ref[...] = cum[tm - 1:tm, :]
    misc = jnp.where(is_f, cum, _sigmoid(z))
    misc_ref[...] = misc
    misct_ref[...] = misc.T


def _in_proj(x, mod, w_r, bf_row):
    B, S, D = x.shape
    tm = ROW_TILE
    big = lambda: pl.BlockSpec((None, tm, 512), lambda b, s: (b, s, 0))
    small = lambda: pl.BlockSpec((None, tm, LANES), lambda b, s: (b, s, 0))
    out_shape = ([jax.ShapeDtypeStruct((B, S, 512), BF16)] * 4
                 + [jax.ShapeDtypeStruct((B, S, LANES), BF16)] * 6
                 + [jax.ShapeDtypeStruct((B, S, LANES), F32),
                    jax.ShapeDtypeStruct((B, LANES, S), F32)])
    out_specs = ([big() for _ in range(4)] + [small() for _ in range(6)]
                 + [small(), pl.BlockSpec((None, LANES, tm), lambda b, s: (b, 0, s))])
    return pl.pallas_call(
        _in_proj_kernel,
        grid=(B, S // tm),
        in_specs=[pl.BlockSpec((None, tm, D), lambda b, s: (b, s, 0)),
                  pl.BlockSpec((None, 6, D), lambda b, s: (b, 0, 0)),
                  pl.BlockSpec((D, _IN_COLS), lambda b, s: (0, 0)),
                  pl.BlockSpec((1, LANES), lambda b, s: (0, 0))],
        out_specs=out_specs,
        out_shape=out_shape,
        scratch_shapes=[pltpu.VMEM((1, LANES), F32)],
        compiler_params=pltpu.CompilerParams(dimension_semantics=("parallel", "arbitrary"),
                                             vmem_limit_bytes=VMEM_LIMIT),
        name="in_proj",
    )(x, mod, w_r, bf_row)


def _flash_update(carry, s, vt):
    m, l, acc = carry
    m_new = jnp.maximum(m, jnp.max(s, axis=-1, keepdims=True))
    alpha = jnp.exp(m - m_new)
    p = jnp.exp(s - m_new)
    l = alpha * l + jnp.sum(p, axis=-1, keepdims=True)
    rows = acc.shape[0]
    pv = _dot(p.reshape(rows, s.shape[-1]).astype(BF16), vt)
    acc = alpha.reshape(rows, 1) * acc + pv
    return m_new, l, acc


def _fox_kernel(q_ref, k_ref, v_ref, ck_ref, o_ref):
    i = pl.program_id(2)
    tq = q_ref.shape[0]
    tk = ck_ref.shape[2]
    q2 = q_ref[...].astype(F32)
    lane = lax.broadcasted_iota(jnp.int32, (tq, LANES), 1)
    low = lane < HEAD_DIM
    halves = (low, jnp.logical_not(low))
    qh = [jnp.where(h, q2, 0.0).astype(BF16) for h in halves]
    col_minus_row = (lax.broadcasted_iota(jnp.int32, (tq, tk), 1)
                     - lax.broadcasted_iota(jnp.int32, (tq, tk), 0))

    def step(jj, carry, diagonal):
        k0 = pl.multiple_of(jj * tk, tk)
        kt = k_ref[pl.ds(k0, tk), :]
        vt = v_ref[pl.ds(k0, tk), :]
        new = []
        for hh in range(2):
            s = _dot_nt(qh[hh], kt) - ck_ref[hh, pl.ds(jj, 1), :]
            if diagonal:
                s = jnp.where(col_minus_row <= i * tq - jj * tk, s, NEG_BIG)
            new.append(_flash_update(carry[hh], s, vt))
        return tuple(new)

    init = tuple((jnp.full((tq, 1), NEG_BIG, F32), jnp.zeros((tq, 1), F32),
                  jnp.zeros((tq, LANES), F32)) for _ in range(2))
    n_full = (i * tq) // tk
    carry = lax.fori_loop(0, n_full, lambda jj, c: step(jj, c, False), init)
    carry = step(n_full, carry, True)
    outs = [acc / l for (_, l, acc) in carry]
    o_ref[...] = jnp.where(low, outs[0], outs[1]).astype(o_ref.dtype)


def _fox(fq, fk, fv, cum_row):
    B, S, W = fq.shape
    tq = tk = FOX_TILE
    n_pairs = W // LANES
    cum_row = cum_row.reshape(B, n_pairs, 2, S // tk, tk)
    return pl.pallas_call(
        _fox_kernel,
        grid=(B, n_pairs, S // tq),
        in_specs=[pl.BlockSpec((None, tq, LANES), lambda b, p, i: (b, i, p)),
                  pl.BlockSpec((None, S, LANES), lambda b, p, i: (b, 0, p)),
                  pl.BlockSpec((None, S, LANES), lambda b, p, i: (b, 0, p)),
                  pl.BlockSpec((None, None, 2, S // tk, tk), lambda b, p, i: (b, p, 0, 0, 0))],
        out_specs=pl.BlockSpec((None, tq, LANES), lambda b, p, i: (b, i, p)),
        out_shape=jax.ShapeDtypeStruct((B, S, W), BF16),
        compiler_params=pltpu.CompilerParams(
            dimension_semantics=("parallel", "parallel", "arbitrary"),
            vmem_limit_bytes=VMEM_LIMIT),
        name="fox",
    )(fq, fk, fv, cum_row)


def _compress_kernel(xk_ref, xv_ref, pk_ref, pv_ref, wak_ref, wbk_ref, wav_ref, wbv_ref,
                     w2k_ref, w2v_ref, ok_ref, ov_ref):
    n_rows = xk_ref.shape[0]
    for x_ref, p_ref, wa_ref, wb_ref, w2_ref, o_ref in (
            (xk_ref, pk_ref, wak_ref, wbk_ref, w2k_ref, ok_ref),
            (xv_ref, pv_ref, wav_ref, wbv_ref, w2v_ref, ov_ref)):
        x = x_ref[...].astype(F32)
        xa = (x + p_ref[0:1, :]).astype(BF16)
        xb = (x + p_ref[1:2, :]).astype(BF16)
        hb = _dot(xb, wb_ref[...])
        h1 = _dot(xa, wa_ref[...]) + pltpu.roll(hb, n_rows - 1, 0)
        o_ref[...] = _dot(_silu(h1).astype(BF16), w2_ref[...]).astype(o_ref.dtype)


def _compress(xk, xv, pk, pv, wak, wbk, wav, wbv, w2k, w2v):
    B, R, C = xk.shape
    xspec = pl.BlockSpec((None, R, C), lambda b: (b, 0, 0))
    full = lambda a: pl.BlockSpec(a.shape, lambda b: (0,) * a.ndim)
    ospec = pl.BlockSpec((None, R, LANES), lambda b: (b, 0, 0))
    return pl.pallas_call(
        _compress_kernel,
        grid=(B,),
        in_specs=[xspec, xspec] + [full(a) for a in (pk, pv, wak, wbk, wav, wbv, w2k, w2v)],
        out_specs=[ospec, ospec],
        out_shape=[jax.ShapeDtypeStruct((B, R, LANES), BF16)] * 2,
        compiler_params=pltpu.CompilerParams(dimension_semantics=("parallel",),
                                             vmem_limit_bytes=VMEM_LIMIT),
        name="compress",
    )(xk, xv, pk, pv, wak, wbk, wav, wbv, w2k, w2v)


def _rank_rows(score):
    n = score.shape[0]
    j = lax.broadcasted_iota(jnp.int32, score.shape, 0)
    rank = jnp.zeros(score.shape, jnp.int32)
    for i in range(n):
        si = score[i:i + 1, :]
        beats = (si > score) | ((si == score) & (j > i))
        rank = rank + jnp.where(beats, 1, 0)
    return rank


def _dup_head(q4, r):
    pair = q4[:, (r // 2) * LANES:(r // 2 + 1) * LANES].astype(F32)
    lane = lax.broadcasted_iota(jnp.int32, pair.shape, 1)
    swapped = _swap_halves(pair)
    if r % 2 == 0:
        return jnp.where(lane < HEAD_DIM, pair, swapped)
    return jnp.where(lane < HEAD_DIM, swapped, pair)


def _pack_heads(o_list, g):
    lane = lax.broadcasted_iota(jnp.int32, o_list[0].shape, 1)
    in_g = (lane >= g * HEAD_DIM) & (lane < (g + 1) * HEAD_DIM)
    both = []
    for o in o_list:
        om = jnp.where(in_g, o, 0.0)
        both.append(om + _swap_halves(om))
    pairs = [jnp.where(lane < HEAD_DIM, both[2 * p], both[2 * p + 1]) for p in range(2)]
    return jnp.concatenate(pairs, axis=1)


def _cmp_sel_kernel(q_ref, kc_ref, vc_ref, bias_ref, gate_ref, ovl_ref, oc_ref, sel_ref):
    g = pl.program_id(1)
    i = pl.program_id(2)
    tq = q_ref.shape[0]
    n_pad = kc_ref.shape[0]
    q4 = q_ref[...]
    lane = lax.broadcasted_iota(jnp.int32, (n_pad, LANES), 1)
    in_g = (lane >= g * HEAD_DIM) & (lane < (g + 1) * HEAD_DIM)
    kc = jnp.where(in_g, kc_ref[...].astype(F32), 0.0).astype(BF16)
    vc = vc_ref[...]
    t = i * tq + lax.broadcasted_iota(jnp.int32, (tq, n_pad), 0)
    n = lax.broadcasted_iota(jnp.int32, (tq, n_pad), 1)
    valid = t >= n * CMP_STRIDE + (CMP_BLOCK - 1)
    gates = gate_ref[...]
    p_sum = jnp.zeros((tq, n_pad), F32)
    outs = []
    for r in range(NSA_GQA):
        qr = _dup_head(q4, r).astype(BF16)
        s = _dot_nt(qr, kc)
        s = jnp.where(valid, s + bias_ref[r], NEG_BIG)
        m = jnp.max(s, axis=-1, keepdims=True)
        p = jnp.exp(s - m)
        p = p / jnp.sum(p, axis=-1, keepdims=True)
        p = jnp.where(valid, p, 0.0)
        p_sum = p_sum + p
        outs.append(_dot(p.astype(BF16), vc) * gates[:, 3 * r:3 * r + 1])
    oc_ref[...] = _pack_heads(outs, g).astype(oc_ref.dtype)

    ovl = ovl_ref[...]
    hi, mid, lo = _split3(p_sum)
    imp = _dot_nt(ovl, hi) + _dot_nt(ovl, mid) + _dot_nt(ovl, lo)
    n_blk = imp.shape[0]
    j = lax.broadcasted_iota(jnp.int32, (n_blk, tq), 0)
    qb = jnp.right_shift(i * tq + lax.broadcasted_iota(jnp.int32, (n_blk, tq), 1),
                         int(math.log2(SEL_BLOCK)))
    forced = (j == 0) | (j == qb) | (j == qb - 1)
    causal = j <= qb
    score = jnp.where(causal, imp + jnp.where(forced, FORCE_SCORE, 0.0), -FORCE_SCORE)
    chosen = (_rank_rows(score) < N_SEL) & causal
    sel = jnp.where(chosen, 1.0, 0.0)
    sel = jnp.concatenate([sel, jnp.zeros((LANES - n_blk, tq), F32)], axis=0)
    sel_ref[...] = sel.T


def _cmp_sel(nq, kcmp, vcmp, bias_c, gates_g, ovl_t):
    B, S, _ = nq.shape
    tq = ATT_TILE
    n_pad = kcmp.shape[1]
    return pl.pallas_call(
        _cmp_sel_kernel,
        grid=(B, NSA_GROUPS, S // tq),
        in_specs=[pl.BlockSpec((None, tq, 2 * LANES), lambda b, g, i: (b, i, g)),
                  pl.BlockSpec((None, n_pad, LANES), lambda b, g, i: (b, 0, 0)),
                  pl.BlockSpec((None, n_pad, LANES), lambda b, g, i: (b, 0, 0)),
                  pl.BlockSpec((NSA_GQA, tq, n_pad), lambda b, g, i: (g, i, 0)),
                  pl.BlockSpec((None, None, tq, 3 * NSA_GQA), lambda b, g, i: (b, g, i, 0)),
                  pl.BlockSpec(ovl_t.shape, lambda b, g, i: (0, 0))],
        out_specs=[pl.BlockSpec((None, tq, 2 * LANES), lambda b, g, i: (b, i, g)),
                   pl.BlockSpec((None, None, tq, LANES), lambda b, g, i: (b, g, i, 0))],
        out_shape=[jax.ShapeDtypeStruct((B, S, NSA_HEADS * HEAD_DIM), BF16),
                   jax.ShapeDtypeStruct((B, NSA_GROUPS, S, LANES), F32)],
        compiler_params=pltpu.CompilerParams(
            dimension_semantics=("parallel", "parallel", "arbitrary"),
            vmem_limit_bytes=VMEM_LIMIT),
        name="cmp_sel",
    )(nq, kcmp, vcmp, bias_c, gates_g, ovl_t)


def _nsa_kernel(q_ref, ks_ref, vs_ref, kw_ref, vw_ref, sel_ref, exp_ref, w4_ref, gate_ref,
                oc_ref, o_ref, madd_ref):
    g = pl.program_id(1)
    i = pl.program_id(2)
    tq = q_ref.shape[0]
    n_kt, _, tk = madd_ref.shape
    H = NSA_GQA
    q4 = q_ref[...]
    lane = lax.broadcasted_iota(jnp.int32, (tq, LANES), 1)
    in_g = (lane >= g * HEAD_DIM) & (lane < (g + 1) * HEAD_DIM)
    qs = jnp.concatenate(
        [jnp.where(in_g, _dup_head(q4, r), 0.0).astype(BF16) for r in range(H)], axis=0)

    sel = sel_ref[...].astype(BF16)
    for jj in range(n_kt):
        hit = _dot(sel, exp_ref[:, jj * tk:(jj + 1) * tk])
        madd_ref[jj] = (hit - 1.0) * (-NEG_BIG)

    def sel_step(jj, carry, diagonal):
        k0 = pl.multiple_of(jj * tk, tk)
        dd = jnp.minimum(i - 2 * jj, 3)
        s = (_dot_nt(qs, ks_ref[pl.ds(k0, tk), :]).reshape(H, tq, tk)
             + w4_ref[:, pl.ds(dd, 1)].reshape(H, tq, tk))
        madd = madd_ref[jj]
        if diagonal:
            cmr = (lax.broadcasted_iota(jnp.int32, (tq, tk), 1)
                   - lax.broadcasted_iota(jnp.int32, (tq, tk), 0))
            madd = jnp.where(cmr <= i * tq - jj * tk, madd, NEG_BIG)
        return _flash_update(carry, s + madd[None], vs_ref[pl.ds(k0, tk), :])

    init = (jnp.full((H, tq, 1), NEG_BIG, F32), jnp.zeros((H, tq, 1), F32),
            jnp.zeros((H * tq, LANES), F32))
    n_full = (i * tq) // tk
    carry = lax.fori_loop(0, n_full, lambda jj, c: sel_step(jj, c, False), init)
    _, l_s, acc_s = sel_step(n_full, carry, True)

    tiles = (jnp.maximum(i - 2, 0), jnp.maximum(i - 1, 0), i)
    starts = [pl.multiple_of(j * tq, tq) for j in tiles]
    kt = jnp.concatenate([kw_ref[pl.ds(st, tq), :] for st in starts], axis=0)
    vt = jnp.concatenate([vw_ref[pl.ds(st, tq), :] for st in starts], axis=0)
    bias_w = jnp.concatenate([w4_ref[:, 2], w4_ref[:, 0, :, 0:tq]], axis=-1)
    col = lax.broadcasted_iota(jnp.int32, (tq, 3 * tq), 1)
    cmr = col - lax.broadcasted_iota(jnp.int32, (tq, 3 * tq), 0)
    never = 4 * tq
    valid = (((col < tq) & (cmr > jnp.where(i >= 2, 0, never)))
             | ((col >= jnp.where(i >= 1, tq, never)) & (col < 2 * tq))
             | ((col >= 2 * tq) & (cmr <= 2 * tq)))
    s = _dot_nt(qs, kt).reshape(H, tq, 3 * tq) + bias_w
    s = jnp.where(valid[None], s, NEG_BIG)
    p = jnp.exp(s - jnp.max(s, axis=-1, keepdims=True))
    l_w = jnp.sum(p, axis=-1, keepdims=True)
    acc_w = _dot(p.reshape(H * tq, 3 * tq).astype(BF16), vt)

    gates = gate_ref[...]
    o_s = acc_s / l_s.reshape(H * tq, 1)
    o_w = acc_w / l_w.reshape(H * tq, 1)
    outs = []
    for r in range(H):
        sl = slice(r * tq, (r + 1) * tq)
        outs.append(o_s[sl] * gates[:, 3 * r + 1:3 * r + 2] + o_w[sl] * gates[:, 3 * r + 2:3 * r + 3])
    o_ref[...] = (_pack_heads(outs, g) + oc_ref[...].astype(F32)).astype(o_ref.dtype)


def _nsa(nq, ks, vs, kw, vw, sel, expand, w4, gates_g, oc):
    B, S, _ = nq.shape
    tq = ATT_TILE
    tk = w4.shape[-1]
    kv = lambda: pl.BlockSpec((None, S, LANES), lambda b, g, i: (b, 0, 0))
    return pl.pallas_call(
        _nsa_kernel,
        grid=(B, NSA_GROUPS, S // tq),
        in_specs=[pl.BlockSpec((None, tq, 2 * LANES), lambda b, g, i: (b, i, g)),
                  kv(), kv(), kv(), kv(),
                  pl.BlockSpec((None, None, tq, LANES), lambda b, g, i: (b, g, i, 0)),
                  pl.BlockSpec(expand.shape, lambda b, g, i: (0, 0)),
                  pl.BlockSpec((NSA_GQA, 4, tq, tk), lambda b, g, i: (g, 0, 0, 0)),
                  pl.BlockSpec((None, None, tq, 3 * NSA_GQA), lambda b, g, i: (b, g, i, 0)),
                  pl.BlockSpec((None, tq, 2 * LANES), lambda b, g, i: (b, i, g))],
        out_specs=pl.BlockSpec((None, tq, 2 * LANES), lambda b, g, i: (b, i, g)),
        out_shape=jax.ShapeDtypeStruct((B, S, NSA_HEADS * HEAD_DIM), BF16),
        scratch_shapes=[pltpu.VMEM((S // tk, tq, tk), F32)],
        compiler_params=pltpu.CompilerParams(
            dimension_semantics=("parallel", "parallel", "arbitrary"),
            vmem_limit_bytes=VMEM_LIMIT),
        name="nsa",
    )(nq, ks, vs, kw, vw, sel, expand, w4, gates_g, oc)


def _layer_norm(y, g, b):
    mu = jnp.mean(y, axis=-1, keepdims=True)
    yc = y - mu
    var = jnp.mean(yc * yc, axis=-1, keepdims=True)
    return yc * lax.rsqrt(var + LN_EPS) * g + b


def _router_gates_t(h2, wr_t, eb_col):
    tm = h2.shape[0]
    h_parts = _split3(h2)
    w_parts = _split3(wr_t)
    logit = jnp.zeros((N_EXPERTS, tm), F32)
    for a, wp in enumerate(w_parts):
        for b, hp in enumerate(h_parts):
            if a + b <= 2:
                logit = logit + _dot_nt(wp, hp)
    scores = _sigmoid(logit)
    biased = scores + eb_col
    e_in = lax.broadcasted_iota(jnp.int32, (GROUP_SIZE, tm), 0).astype(F32)
    gs_rows = []
    for gi in range(N_EXPERT_GROUPS):
        grp = biased[gi * GROUP_SIZE:(gi + 1) * GROUP_SIZE, :]
        m1 = jnp.max(grp, axis=0, keepdims=True)
        first = jnp.min(jnp.where(grp == m1, e_in, float(GROUP_SIZE)), axis=0, keepdims=True)
        m2 = jnp.max(jnp.where(e_in == first, -jnp.inf, grp), axis=0, keepdims=True)
        gs_rows.append(m1 + m2)
    gscore = jnp.concatenate(gs_rows, axis=0)
    g_keep = _rank_rows(gscore) < TOPK_GROUPS
    keep = jnp.concatenate(
        [jnp.broadcast_to(g_keep[gi:gi + 1, :], (GROUP_SIZE, tm)) for gi in range(N_EXPERT_GROUPS)],
        axis=0)
    masked = jnp.where(keep, biased, -jnp.inf)
    chosen = _rank_rows(masked) < TOP_K
    w = jnp.where(chosen, scores, 0.0)
    return w / jnp.sum(w, axis=0, keepdims=True) * ROUTED_SCALE


def _out_proj_kernel(alpha, of_ref, on_ref, x_ref, mod_ref, w_ref, lg_ref, lb_ref, wr_ref,
                     eb_ref, x1_ref, h2_ref, gate_ref):
    half = of_ref.shape[1]
    mod = mod_ref[...]
    mixed = _dot(of_ref[...], w_ref[0:half, :]) + _dot(on_ref[...], w_ref[half:2 * half, :])
    y = alpha * x_ref[...] + mod[2:3, :] * mixed
    x1 = _layer_norm(y, lg_ref[...], lb_ref[...])
    x1_ref[...] = x1
    h2 = x1 * (1.0 + mod[4:5, :]) + mod[3:4, :]
    h2_ref[...] = h2.astype(h2_ref.dtype)
    gates_t = _router_gates_t(h2, wr_ref[...], eb_ref[...])
    tm = h2.shape[0]
    gates_t = jnp.concatenate([gates_t, jnp.zeros((LANES - N_EXPERTS, tm), F32)], axis=0)
    gate_ref[...] = gates_t.T


def _out_proj(alpha, o_fox, o_nsa, x, mod, w_out, ln_g, ln_b, wr_t, eb_col):
    B, S, D = x.shape
    tm = ROW_TILE
    half = o_fox.shape[-1]
    row = lambda a: pl.BlockSpec(a.shape, lambda b, s: (0, 0))
    return pl.pallas_call(
        functools.partial(_out_proj_kernel, alpha),
        grid=(B, S // tm),
        in_specs=[pl.BlockSpec((None, tm, half), lambda b, s: (b, s, 0)),
                  pl.BlockSpec((None, tm, half), lambda b, s: (b, s, 0)),
                  pl.BlockSpec((None, tm, D), lambda b, s: (b, s, 0)),
                  pl.BlockSpec((None, 6, D), lambda b, s: (b, 0, 0)),
                  row(w_out), row(ln_g), row(ln_b), row(wr_t), row(eb_col)],
        out_specs=[pl.BlockSpec((None, tm, D), lambda b, s: (b, s, 0)),
                   pl.BlockSpec((None, tm, D), lambda b, s: (b, s, 0)),
                   pl.BlockSpec((None, tm, LANES), lambda b, s: (b, s, 0))],
        out_shape=[jax.ShapeDtypeStruct((B, S, D), F32),
                   jax.ShapeDtypeStruct((B, S, D), BF16),
                   jax.ShapeDtypeStruct((B, S, LANES), F32)],
        compiler_params=pltpu.CompilerParams(dimension_semantics=("parallel", "parallel"),
                                             vmem_limit_bytes=VMEM_LIMIT),
        name="out_proj",
    )(o_fox, o_nsa, x, mod, w_out, ln_g, ln_b, wr_t, eb_col)


def _moe_kernel(alpha, h_ref, x1_ref, gate_ref, mod_ref, wgu_ref, wd_ref, sgu_ref, sd_ref,
                lg_ref, lb_ref, o_ref, acc_ref):
    e = pl.program_id(1)
    n_e = pl.num_programs(1)
    h = h_ref[...]
    f = wd_ref.shape[0]

    def expert(wgu, wd, gate):
        a = _dot(h, wgu)
        act = _silu(a[:, :f]) * a[:, f:]
        if gate is not None:
            act = act * gate
        return _dot(act.astype(BF16), wd)

    @pl.when(e == 0)
    def _():
        acc_ref[...] = expert(sgu_ref[...], sd_ref[...], None)

    gates = gate_ref[...]
    lane = lax.broadcasted_iota(jnp.int32, gates.shape, 1)
    g_e = jnp.sum(jnp.where(lane == e, gates, 0.0), axis=-1, keepdims=True)
    acc_ref[...] += expert(wgu_ref[...], wd_ref[...], g_e)

    @pl.when(e == n_e - 1)
    def _():
        mod = mod_ref[...]
        y = alpha * x1_ref[...] + mod[5:6, :] * acc_ref[...]
        o_ref[...] = _layer_norm(y, lg_ref[...], lb_ref[...])


def _moe(alpha, h2, x1, gates, mod, wgu, wd, sgu, sd, ln_g, ln_b):
    B, S, D = x1.shape
    T = B * S
    tm = MOE_TILE
    per_b = S // tm
    h2 = h2.reshape(T, D)
    x1 = x1.reshape(T, D)
    gates = gates.reshape(T, LANES)
    n_e, _, f2 = wgu.shape
    f = wd.shape[1]
    row = lambda a: pl.BlockSpec(a.shape, lambda t, e: (0, 0))
    out = pl.pallas_call(
        functools.partial(_moe_kernel, alpha),
        grid=(T // tm, n_e),
        in_specs=[pl.BlockSpec((tm, D), lambda t, e: (t, 0)),
                  pl.BlockSpec((tm, D), lambda t, e: (t, 0)),
                  pl.BlockSpec((tm, LANES), lambda t, e: (t, 0)),
                  pl.BlockSpec((None, 6, D), lambda t, e: (t // per_b, 0, 0)),
                  pl.BlockSpec((None, D, f2), lambda t, e: (e, 0, 0)),
                  pl.BlockSpec((None, f, D), lambda t, e: (e, 0, 0)),
                  row(sgu), row(sd), row(ln_g), row(ln_b)],
        out_specs=pl.BlockSpec((tm, D), lambda t, e: (t, 0)),
        out_shape=jax.ShapeDtypeStruct((T, D), F32),
        scratch_shapes=[pltpu.VMEM((tm, D), F32)],
        compiler_params=pltpu.CompilerParams(dimension_semantics=("parallel", "arbitrary"),
                                             vmem_limit_bytes=VMEM_LIMIT),
        name="moe",
    )(h2, x1, gates, mod, wgu, wd, sgu, sd, ln_g, ln_b)
    return out.reshape(B, S, D)


def _rearrange_w_in(w):
    scale = HEAD_DIM ** -0.5
    fq, fk, fv = w[:, 0:512], w[:, 512:1024], w[:, 1024:1536]
    ff = w[:, 1536:1544]
    nq = w[:, 1544:2056]
    kv = w[:, 2056:2824]
    ng = w[:, 2824:2848]
    pad = jnp.zeros((w.shape[0], LANES - ff.shape[1] - ng.shape[1]), w.dtype)
    return jnp.concatenate([fq * scale, fk, fv, nq * scale, kv, ff, ng, pad], axis=1).astype(BF16)


def _compress_weights(pos, w1, w2):
    half = CMP_BLOCK // 2
    w1r = w1.reshape(2, half, HEAD_DIM, CMP_HIDDEN)
    zeros = jnp.zeros_like(w1r[0])
    def spread(part):
        g0 = jnp.stack([part, zeros], axis=1).reshape(half * 2 * HEAD_DIM, CMP_HIDDEN)
        g1 = jnp.stack([zeros, part], axis=1).reshape(half * 2 * HEAD_DIM, CMP_HIDDEN)
        return jnp.concatenate([g0, g1], axis=1).astype(BF16)
    wa, wb = spread(w1r[0]), spread(w1r[1])
    z2 = jnp.zeros_like(w2)
    w2bd = jnp.concatenate([jnp.concatenate([w2, z2], axis=1),
                            jnp.concatenate([z2, w2], axis=1)], axis=0).astype(BF16)
    posr = pos.reshape(2, half, 1, HEAD_DIM)
    posr = jnp.broadcast_to(posr, (2, half, NSA_GROUPS, HEAD_DIM)).reshape(2, half * 2 * HEAD_DIM)
    return posr, wa, wb, w2bd


@functools.lru_cache(maxsize=None)
def _static_tables(S):
    tq = ATT_TILE
    n_cmp = (S - CMP_BLOCK) // CMP_STRIDE + 1
    n_pad = S // CMP_STRIDE
    n_slc = S // SEL_BLOCK
    t = np.arange(S)[:, None]
    n = np.arange(n_pad)[None, :]
    bucket_c = _t5_bucket_np(t - (n * CMP_STRIDE + CMP_BLOCK - 1)).reshape(1, -1)
    d = (np.arange(4)[:, None, None] * tq + np.arange(tq)[None, :, None]
         - np.arange(2 * tq)[None, None, :])
    bucket_w = _t5_bucket_np(d).reshape(1, -1)
    cs = np.arange(n_pad)[None, :] * CMP_STRIDE
    sj = np.arange(n_slc)[:, None] * SEL_BLOCK
    ovl_t = ((cs < sj + SEL_BLOCK) & (cs + CMP_BLOCK > sj) & (np.arange(n_pad)[None, :] < n_cmp))
    expand = (np.arange(S)[None, :] // SEL_BLOCK) == np.arange(LANES)[:, None]
    return bucket_c, bucket_w, ovl_t.astype(np.float32), expand.astype(np.float32)


def kernel(x, c, w_ada, b_ada, w_in, b_f, cmp_pos_k, cmp_w1_k, cmp_w2_k, cmp_pos_v, cmp_w1_v,
           cmp_w2_v, rel_bias, w_out, ln1_g, ln1_b, w_router, e_bias, w_gate, w_up, w_down,
           ws_gate, ws_up, ws_down, ln2_g, ln2_b):
    B, S, D = x.shape
    depth = w_ada.shape[0]
    alpha = (2 * depth) ** 0.25
    tq = ATT_TILE
    bucket_c, bucket_w, ovl_t, expand = _static_tables(S)
    rel_bias_t = rel_bias.T
    bias_c = _bias_table(jnp.asarray(bucket_c), rel_bias_t).reshape(NSA_HEADS, S, S // CMP_STRIDE)
    w4 = _bias_table(jnp.asarray(bucket_w), rel_bias_t).reshape(NSA_HEADS, 4, tq, 2 * tq)
    ovl_t = jnp.asarray(ovl_t, BF16)
    expand = jnp.asarray(expand, BF16)

    for l in range(depth):
        mod = _ada(c, w_ada[l], b_ada[l]).reshape(B, 6, D)
        bf_row = jnp.zeros((1, LANES), F32).at[0, :FOX_HEADS].set(b_f[l])
        (fq, fk, fv, nq, kc, vc, ks, vs, kw, vw, misc, misc_t) = _in_proj(
            x, mod, _rearrange_w_in(w_in[l]), bf_row)

        o_fox = _fox(fq, fk, fv, misc_t[:, :FOX_HEADS, :])

        pk, wak, wbk, w2k = _compress_weights(cmp_pos_k[l], cmp_w1_k[l], cmp_w2_k[l])
        pv, wav, wbv, w2v = _compress_weights(cmp_pos_v[l], cmp_w1_v[l], cmp_w2_v[l])
        rows = S // CMP_STRIDE
        kcmp, vcmp = _compress(kc.reshape(B, rows, CMP_STRIDE * LANES),
                               vc.reshape(B, rows, CMP_STRIDE * LANES),
                               pk, pv, wak, wbk, wav, wbv, w2k, w2v)

        gates_g = misc[:, :, FOX_HEADS:FOX_HEADS + 3 * NSA_HEADS].reshape(
            B, S, NSA_GROUPS, 3 * NSA_GQA).transpose(0, 2, 1, 3)
        oc, sel = _cmp_sel(nq, kcmp, vcmp, bias_c, gates_g, ovl_t)
        o_nsa = _nsa(nq, ks, vs, kw, vw, sel, expand, w4, gates_g, oc)

        x1, h2, gates = _out_proj(alpha, o_fox, o_nsa, x, mod, w_out[l].astype(BF16),
                                  ln1_g[l].reshape(1, D), ln1_b[l].reshape(1, D),
                                  w_router[l].T, e_bias[l].reshape(N_EXPERTS, 1))

        wgu = jnp.concatenate([w_gate[l], w_up[l]], axis=-1).astype(BF16)
        sgu = jnp.concatenate([ws_gate[l], ws_up[l]], axis=-1).astype(BF16)
        x = _moe(alpha, h2, x1, gates, mod, wgu, w_down[l].astype(BF16), sgu,
                 ws_down[l].astype(BF16), ln2_g[l].reshape(1, D), ln2_b[l].reshape(1, D))
    return x
```

```python
import functools
import math

import jax
import jax.numpy as jnp
import numpy as np
from jax import lax
from jax.experimental import pallas as pl
from jax.experimental.pallas import tpu as pltpu

F32 = jnp.float32
BF16 = jnp.bfloat16

HEAD_DIM = 64
FOX_HEADS = 8
NSA_HEADS = 8
NSA_GQA = 4
NSA_GROUPS = NSA_HEADS // NSA_GQA
CMP_BLOCK = 32
CMP_STRIDE = 16
CMP_HIDDEN = 256
SEL_BLOCK = 64
N_SEL = 16
WINDOW = 512
N_BUCKETS = 32
MAX_DISTANCE = 128
N_EXPERTS = 64
N_EXPERT_GROUPS = 8
GROUP_SIZE = N_EXPERTS // N_EXPERT_GROUPS
TOPK_GROUPS = 4
TOP_K = 8
D_EXPERT = 256
ROUTED_SCALE = 2.5
LN_EPS = 1e-5
NEG_BIG = -1e30
FORCE_SCORE = 1e4

LANES = 128
ATT_TILE = 256
FOX_TILE = 512
ROW_TILE = 512
MOE_TILE = 1024
VMEM_LIMIT = 48 * 1024 * 1024

NT_DIMS = (((1,), (1,)), ((), ()))


def _dot(a, b):
    return jnp.dot(a, b, preferred_element_type=F32)


def _dot_nt(a, b):
    return lax.dot_general(a, b, NT_DIMS, preferred_element_type=F32)


def _split3(x):
    hi = x.astype(BF16)
    r1 = x - hi.astype(F32)
    mid = r1.astype(BF16)
    lo = (r1 - mid.astype(F32)).astype(BF16)
    return hi, mid, lo


def _silu(x):
    return x / (1.0 + jnp.exp(-x))


def _sigmoid(x):
    return 1.0 / (1.0 + jnp.exp(-x))


def _swap_halves(x):
    return pltpu.roll(x, HEAD_DIM, 1)


def _t5_bucket_np(dist):
    n = np.maximum(dist, 0)
    max_exact = N_BUCKETS // 2
    nf = np.maximum(n, 1).astype(np.float32)
    large = max_exact + (np.log(nf / max_exact) / math.log(MAX_DISTANCE / max_exact)
                         * (N_BUCKETS - max_exact)).astype(np.int32)
    large = np.minimum(large, N_BUCKETS - 1)
    return np.where(n < max_exact, n, large).astype(np.int32)


def _ada_kernel(c_ref, w_ref, b_ref, o_ref):
    c = c_ref[...]
    o_ref[...] = jnp.dot(_silu(c), w_ref[...], preferred_element_type=F32,
                         precision=lax.Precision.HIGHEST) + b_ref[...]


def _ada(c, w_ada, b_ada):
    B, D = c.shape
    n_out = w_ada.shape[1]
    tn = 1024
    return pl.pallas_call(
        _ada_kernel,
        grid=(n_out // tn,),
        in_specs=[pl.BlockSpec((B, D), lambda j: (0, 0)),
                  pl.BlockSpec((D, tn), lambda j: (0, j)),
                  pl.BlockSpec((1, tn), lambda j: (0, j))],
        out_specs=pl.BlockSpec((B, tn), lambda j: (0, j)),
        out_shape=jax.ShapeDtypeStruct((B, n_out), F32),
        compiler_params=pltpu.CompilerParams(dimension_semantics=("arbitrary",),
                                             vmem_limit_bytes=VMEM_LIMIT),
        name="ada",
    )(c, w_ada, b_ada.reshape(1, n_out))


def _bias_table_kernel(bkt_ref, rbt_ref, o_ref):
    bkt = bkt_ref[...]
    k = lax.broadcasted_iota(jnp.int32, (N_BUCKETS, bkt.shape[1]), 0)
    onehot = jnp.where(k == bkt, 1.0, 0.0).astype(BF16)
    hi, mid, lo = _split3(rbt_ref[...])
    o_ref[...] = _dot(hi, onehot) + _dot(mid, onehot) + _dot(lo, onehot)


def _bias_table(bucket, rel_bias_t):
    n = bucket.shape[1]
    chunk = 8192
    n_heads = rel_bias_t.shape[0]
    return pl.pallas_call(
        _bias_table_kernel,
        grid=(n // chunk,),
        in_specs=[pl.BlockSpec((1, chunk), lambda j: (0, j)),
                  pl.BlockSpec(rel_bias_t.shape, lambda j: (0, 0))],
        out_specs=pl.BlockSpec((n_heads, chunk), lambda j: (0, j)),
        out_shape=jax.ShapeDtypeStruct((n_heads, n), F32),
        compiler_params=pltpu.CompilerParams(dimension_semantics=("parallel",),
                                             vmem_limit_bytes=VMEM_LIMIT),
        name="bias_table",
    )(bucket, rel_bias_t)


_C_FQ, _C_FK, _C_FV, _C_NQ = 0, 512, 1024, 1536
_C_KV = 2048
_C_MISC = 2816
_IN_COLS = 2944


def _in_proj_kernel(x_ref, mod_ref, w_ref, bf_ref, fq_ref, fk_ref, fv_ref, nq_ref,
                    kc_ref, vc_ref, ks_ref, vs_ref, kw_ref, vw_ref, misc_ref, misct_ref,
                    carry_ref):
    s_idx = pl.program_id(1)
    tm = x_ref.shape[0]
    mod = mod_ref[...]
    h = (x_ref[...] * (1.0 + mod[1:2, :]) + mod[0:1, :]).astype(BF16)

    for ref, c0 in ((fq_ref, _C_FQ), (fk_ref, _C_FK), (fv_ref, _C_FV), (nq_ref, _C_NQ)):
        ref[...] = _dot(h, w_ref[:, c0:c0 + 512]).astype(ref.dtype)
    for k, ref in enumerate((kc_ref, vc_ref, ks_ref, vs_ref, kw_ref, vw_ref)):
        c0 = _C_KV + k * LANES
        ref[...] = _dot(h, w_ref[:, c0:c0 + LANES]).astype(ref.dtype)

    z = _dot(h, w_ref[:, _C_MISC:_C_MISC + LANES]) + bf_ref[...]
    lane = lax.broadcasted_iota(jnp.int32, z.shape, 1)
    is_f = lane < FOX_HEADS
    log_f = jnp.minimum(z, 0.0) - jnp.log(1.0 + jnp.exp(-jnp.abs(z)))
    log_f = jnp.where(is_f, log_f, 0.0)

    row = lax.broadcasted_iota(jnp.int32, (tm, tm), 0)
    col = lax.broadcasted_iota(jnp.int32, (tm, tm), 1)
    tri = jnp.where(row >= col, 1.0, 0.0).astype(BF16)
    hi, mid, lo = _split3(log_f)
    local = _dot(tri, hi) + _dot(tri, mid) + _dot(tri, lo)

    @pl.when(s_idx == 0)
    def _():
        carry_ref[...] = jnp.zeros_like(carry_ref)

    cum = local + carry_ref[...]
    carry_ref[...] = cum[tm - 1:tm, :]
    misc = jnp.where(is_f, cum, _sigmoid(z))
    misc_ref[...] = misc
    misct_ref[...] = misc.T


def _in_proj(x, mod, w_r, bf_row):
    B, S, D = x.shape
    tm = ROW_TILE
    big = lambda: pl.BlockSpec((None, tm, 512), lambda b, s: (b, s, 0))
    small = lambda: pl.BlockSpec((None, tm, LANES), lambda b, s: (b, s, 0))
    out_shape = ([jax.ShapeDtypeStruct((B, S, 512), BF16)] * 4
                 + [jax.ShapeDtypeStruct((B, S, LANES), BF16)] * 6
                 + [jax.ShapeDtypeStruct((B, S, LANES), F32),
                    jax.ShapeDtypeStruct((B, LANES, S), F32)])
    out_specs = ([big() for _ in range(4)] + [small() for _ in range(6)]
                 + [small(), pl.BlockSpec((None, LANES, tm), lambda b, s: (b, 0, s))])
    return pl.pallas_call(
        _in_proj_kernel,
        grid=(B, S // tm),
        in_specs=[pl.BlockSpec((None, tm, D), lambda b, s: (b, s, 0)),
                  pl.BlockSpec((None, 6, D), lambda b, s: (b, 0, 0)),
                  pl.BlockSpec((D, _IN_COLS), lambda b, s: (0, 0)),
                  pl.BlockSpec((1, LANES), lambda b, s: (0, 0))],
        out_specs=out_specs,
        out_shape=out_shape,
        scratch_shapes=[pltpu.VMEM((1, LANES), F32)],
        compiler_params=pltpu.CompilerParams(dimension_semantics=("parallel", "arbitrary"),
                                             vmem_limit_bytes=VMEM_LIMIT),
        name="in_proj",
    )(x, mod, w_r, bf_row)


def _flash_update(carry, s, vt):
    m, l, acc = carry
    m_new = jnp.maximum(m, jnp.max(s, axis=-1, keepdims=True))
    alpha = jnp.exp(m - m_new)
    p = jnp.exp(s - m_new)
    l = alpha * l + jnp.sum(p, axis=-1, keepdims=True)
    rows = acc.shape[0]
    pv = _dot(p.reshape(rows, s.shape[-1]).astype(BF16), vt)
    acc = alpha.reshape(rows, 1) * acc + pv
    return m_new, l, acc


def _fox_kernel(q_ref, k_ref, v_ref, ck_ref, o_ref):
    i = pl.program_id(2)
    tq = q_ref.shape[0]
    tk = ck_ref.shape[2]
    q2 = q_ref[...].astype(F32)
    lane = lax.broadcasted_iota(jnp.int32, (tq, LANES), 1)
    low = lane < HEAD_DIM
    halves = (low, jnp.logical_not(low))
    qh = [jnp.where(h, q2, 0.0).astype(BF16) for h in halves]
    col_minus_row = (lax.broadcasted_iota(jnp.int32, (tq, tk), 1)
                     - lax.broadcasted_iota(jnp.int32, (tq, tk), 0))

    def step(jj, carry, diagonal):
        k0 = pl.multiple_of(jj * tk, tk)
        kt = k_ref[pl.ds(k0, tk), :]
        vt = v_ref[pl.ds(k0, tk), :]
        new = []
        for hh in range(2):
            s = _dot_nt(qh[hh], kt) - ck_ref[hh, pl.ds(jj, 1), :]
            if diagonal:
                s = jnp.where(col_minus_row <= i * tq - jj * tk, s, NEG_BIG)
            new.append(_flash_update(carry[hh], s, vt))
        return tuple(new)

    init = tuple((jnp.full((tq, 1), NEG_BIG, F32), jnp.zeros((tq, 1), F32),
                  jnp.zeros((tq, LANES), F32)) for _ in range(2))
    n_full = (i * tq) // tk
    carry = lax.fori_loop(0, n_full, lambda jj, c: step(jj, c, False), init)
    carry = step(n_full, carry, True)
    outs = [acc / l for (_, l, acc) in carry]
    o_ref[...] = jnp.where(low, outs[0], outs[1]).astype(o_ref.dtype)


def _fox(fq, fk, fv, cum_row):
    B, S, W = fq.shape
    tq = tk = FOX_TILE
    n_pairs = W // LANES
    cum_row = cum_row.reshape(B, n_pairs, 2, S // tk, tk)
    return pl.pallas_call(
        _fox_kernel,
        grid=(B, n_pairs, S // tq),
        in_specs=[pl.BlockSpec((None, tq, LANES), lambda b, p, i: (b, i, p)),
                  pl.BlockSpec((None, S, LANES), lambda b, p, i: (b, 0, p)),
                  pl.BlockSpec((None, S, LANES), lambda b, p, i: (b, 0, p)),
                  pl.BlockSpec((None, None, 2, S // tk, tk), lambda b, p, i: (b, p, 0, 0, 0))],
        out_specs=pl.BlockSpec((None, tq, LANES), lambda b, p, i: (b, i, p)),
        out_shape=jax.ShapeDtypeStruct((B, S, W), BF16),
        compiler_params=pltpu.CompilerParams(
            dimension_semantics=("parallel", "parallel", "arbitrary"),
            vmem_limit_bytes=VMEM_LIMIT),
        name="fox",
    )(fq, fk, fv, cum_row)


def _compress_kernel(xk_ref, xv_ref, pk_ref, pv_ref, wak_ref, wbk_ref, wav_ref, wbv_ref,
                     w2k_ref, w2v_ref, ok_ref, ov_ref):
    n_rows = xk_ref.shape[0]
    for x_ref, p_ref, wa_ref, wb_ref, w2_ref, o_ref in (
            (xk_ref, pk_ref, wak_ref, wbk_ref, w2k_ref, ok_ref),
            (xv_ref, pv_ref, wav_ref, wbv_ref, w2v_ref, ov_ref)):
        x = x_ref[...].astype(F32)
        xa = (x + p_ref[0:1, :]).astype(BF16)
        xb = (x + p_ref[1:2, :]).astype(BF16)
        hb = _dot(xb, wb_ref[...])
        h1 = _dot(xa, wa_ref[...]) + pltpu.roll(hb, n_rows - 1, 0)
        o_ref[...] = _dot(_silu(h1).astype(BF16), w2_ref[...]).astype(o_ref.dtype)


def _compress(xk, xv, pk, pv, wak, wbk, wav, wbv, w2k, w2v):
    B, R, C = xk.shape
    xspec = pl.BlockSpec((None, R, C), lambda b: (b, 0, 0))
    full = lambda a: pl.BlockSpec(a.shape, lambda b: (0,) * a.ndim)
    ospec = pl.BlockSpec((None, R, LANES), lambda b: (b, 0, 0))
    return pl.pallas_call(
        _compress_kernel,
        grid=(B,),
        in_specs=[xspec, xspec] + [full(a) for a in (pk, pv, wak, wbk, wav, wbv, w2k, w2v)],
        out_specs=[ospec, ospec],
        out_shape=[jax.ShapeDtypeStruct((B, R, LANES), BF16)] * 2,
        compiler_params=pltpu.CompilerParams(dimension_semantics=("parallel",),
                                             vmem_limit_bytes=VMEM_LIMIT),
        name="compress",
    )(xk, xv, pk, pv, wak, wbk, wav, wbv, w2k, w2v)


def _rank_rows(score):
    n = score.shape[0]
    j = lax.broadcasted_iota(jnp.int32, score.shape, 0)
    rank = jnp.zeros(score.shape, jnp.int32)
    for i in range(n):
        si = score[i:i + 1, :]
        beats = (si > score) | ((si == score) & (j > i))
        rank = rank + jnp.where(beats, 1, 0)
    return rank


def _dup_head(q4, r):
    pair = q4[:, (r // 2) * LANES:(r // 2 + 1) * LANES].astype(F32)
    lane = lax.broadcasted_iota(jnp.int32, pair.shape, 1)
    swapped = _swap_halves(pair)
    if r % 2 == 0:
        return jnp.where(lane < HEAD_DIM, pair, swapped)
    return jnp.where(lane < HEAD_DIM, swapped, pair)


def _pack_heads(o_list, g):
    lane = lax.broadcasted_iota(jnp.int32, o_list[0].shape, 1)
    in_g = (lane >= g * HEAD_DIM) & (lane < (g + 1) * HEAD_DIM)
    both = []
    for o in o_list:
        om = jnp.where(in_g, o, 0.0)
        both.append(om + _swap_halves(om))
    pairs = [jnp.where(lane < HEAD_DIM, both[2 * p], both[2 * p + 1]) for p in range(2)]
    return jnp.concatenate(pairs, axis=1)


def _cmp_sel_kernel(q_ref, kc_ref, vc_ref, bias_ref, gate_ref, ovl_ref, oc_ref, sel_ref):
    g = pl.program_id(1)
    i = pl.program_id(2)
    tq = q_ref.shape[0]
    n_pad = kc_ref.shape[0]
    q4 = q_ref[...]
    lane = lax.broadcasted_iota(jnp.int32, (n_pad, LANES), 1)
    in_g = (lane >= g * HEAD_DIM) & (lane < (g + 1) * HEAD_DIM)
    kc = jnp.where(in_g, kc_ref[...].astype(F32), 0.0).astype(BF16)
    vc = vc_ref[...]
    t = i * tq + lax.broadcasted_iota(jnp.int32, (tq, n_pad), 0)
    n = lax.broadcasted_iota(jnp.int32, (tq, n_pad), 1)
    valid = t >= n * CMP_STRIDE + (CMP_BLOCK - 1)
    gates = gate_ref[...]
    p_sum = jnp.zeros((tq, n_pad), F32)
    outs = []
    for r in range(NSA_GQA):
        qr = _dup_head(q4, r).astype(BF16)
        s = _dot_nt(qr, kc)
        s = jnp.where(valid, s + bias_ref[r], NEG_BIG)
        m = jnp.max(s, axis=-1, keepdims=True)
        p = jnp.exp(s - m)
        p = p / jnp.sum(p, axis=-1, keepdims=True)
        p = jnp.where(valid, p, 0.0)
        p_sum = p_sum + p
        outs.append(_dot(p.astype(BF16), vc) * gates[:, 3 * r:3 * r + 1])
    oc_ref[...] = _pack_heads(outs, g).astype(oc_ref.dtype)

    ovl = ovl_ref[...]
    hi, mid, lo = _split3(p_sum)
    imp = _dot_nt(ovl, hi) + _dot_nt(ovl, mid) + _dot_nt(ovl, lo)
    n_blk = imp.shape[0]
    j = lax.broadcasted_iota(jnp.int32, (n_blk, tq), 0)
    qb = jnp.right_shift(i * tq + lax.broadcasted_iota(jnp.int32, (n_blk, tq), 1),
                         int(math.log2(SEL_BLOCK)))
    forced = (j == 0) | (j == qb) | (j == qb - 1)
    causal = j <= qb
    score = jnp.where(causal, imp + jnp.where(forced, FORCE_SCORE, 0.0), -FORCE_SCORE)
    chosen = (_rank_rows(score) < N_SEL) & causal
    sel = jnp.where(chosen, 1.0, 0.0)
    sel = jnp.concatenate([sel, jnp.zeros((LANES - n_blk, tq), F32)], axis=0)
    sel_ref[...] = sel.T


def _cmp_sel(nq, kcmp, vcmp, bias_c, gates_g, ovl_t):
    B, S, _ = nq.shape
    tq = ATT_TILE
    n_pad = kcmp.shape[1]
    return pl.pallas_call(
        _cmp_sel_kernel,
        grid=(B, NSA_GROUPS, S // tq),
        in_specs=[pl.BlockSpec((None, tq, 2 * LANES), lambda b, g, i: (b, i, g)),
                  pl.BlockSpec((None, n_pad, LANES), lambda b, g, i: (b, 0, 0)),
                  pl.BlockSpec((None, n_pad, LANES), lambda b, g, i: (b, 0, 0)),
                  pl.BlockSpec((NSA_GQA, tq, n_pad), lambda b, g, i: (g, i, 0)),
                  pl.BlockSpec((None, None, tq, 3 * NSA_GQA), lambda b, g, i: (b, g, i, 0)),
                  pl.BlockSpec(ovl_t.shape, lambda b, g, i: (0, 0))],
        out_specs=[pl.BlockSpec((None, tq, 2 * LANES), lambda b, g, i: (b, i, g)),
                   pl.BlockSpec((None, None, tq, LANES), lambda b, g, i: (b, g, i, 0))],
        out_shape=[jax.ShapeDtypeStruct((B, S, NSA_HEADS * HEAD_DIM), BF16),
                   jax.ShapeDtypeStruct((B, NSA_GROUPS, S, LANES), F32)],
        compiler_params=pltpu.CompilerParams(
            dimension_semantics=("parallel", "parallel", "arbitrary"),
            vmem_limit_bytes=VMEM_LIMIT),
        name="cmp_sel",
    )(nq, kcmp, vcmp, bias_c, gates_g, ovl_t)


def _nsa_kernel(q_ref, ks_ref, vs_ref, kw_ref, vw_ref, sel_ref, exp_ref, w4_ref, gate_ref,
                oc_ref, o_ref, madd_ref):
    g = pl.program_id(1)
    i = pl.program_id(2)
    tq = q_ref.shape[0]
    n_kt, _, tk = madd_ref.shape
    H = NSA_GQA
    q4 = q_ref[...]
    lane = lax.broadcasted_iota(jnp.int32, (tq, LANES), 1)
    in_g = (lane >= g * HEAD_DIM) & (lane < (g + 1) * HEAD_DIM)
    qs = jnp.concatenate(
        [jnp.where(in_g, _dup_head(q4, r), 0.0).astype(BF16) for r in range(H)], axis=0)

    sel = sel_ref[...].astype(BF16)
    for jj in range(n_kt):
        hit = _dot(sel, exp_ref[:, jj * tk:(jj + 1) * tk])
        madd_ref[jj] = (hit - 1.0) * (-NEG_BIG)

    def sel_step(jj, carry, diagonal):
        k0 = pl.multiple_of(jj * tk, tk)
        dd = jnp.minimum(i - 2 * jj, 3)
        s = (_dot_nt(qs, ks_ref[pl.ds(k0, tk), :]).reshape(H, tq, tk)
             + w4_ref[:, pl.ds(dd, 1)].reshape(H, tq, tk))
        madd = madd_ref[jj]
        if diagonal:
            cmr = (lax.broadcasted_iota(jnp.int32, (tq, tk), 1)
                   - lax.broadcasted_iota(jnp.int32, (tq, tk), 0))
            madd = jnp.where(cmr <= i * tq - jj * tk, madd, NEG_BIG)
        return _flash_update(carry, s + madd[None], vs_ref[pl.ds(k0, tk), :])

    init = (jnp.full((H, tq, 1), NEG_BIG, F32), jnp.zeros((H, tq, 1), F32),
            jnp.zeros((H * tq, LANES), F32))
    n_full = (i * tq) // tk
    carry = lax.fori_loop(0, n_full, lambda jj, c: sel_step(jj, c, False), init)
    _, l_s, acc_s = sel_step(n_full, carry, True)

    tiles = (jnp.maximum(i - 2, 0), jnp.maximum(i - 1, 0), i)
    starts = [pl.multiple_of(j * tq, tq) for j in tiles]
    kt = jnp.concatenate([kw_ref[pl.ds(st, tq), :] for st in starts], axis=0)
    vt = jnp.concatenate([vw_ref[pl.ds(st, tq), :] for st in starts], axis=0)
    bias_w = jnp.concatenate([w4_ref[:, 2], w4_ref[:, 0, :, 0:tq]], axis=-1)
    col = lax.broadcasted_iota(jnp.int32, (tq, 3 * tq), 1)
    cmr = col - lax.broadcasted_iota(jnp.int32, (tq, 3 * tq), 0)
    never = 4 * tq
    valid = (((col < tq) & (cmr > jnp.where(i >= 2, 0, never)))
             | ((col >= jnp.where(i >= 1, tq, never)) & (col < 2 * tq))
             | ((col >= 2 * tq) & (cmr <= 2 * tq)))
    s = _dot_nt(qs, kt).reshape(H, tq, 3 * tq) + bias_w
    s = jnp.where(valid[None], s, NEG_BIG)
    p = jnp.exp(s - jnp.max(s, axis=-1, keepdims=True))
    l_w = jnp.sum(p, axis=-1, keepdims=True)
    acc_w = _dot(p.reshape(H * tq, 3 * tq).astype(BF16), vt)

    gates = gate_ref[...]
    o_s = acc_s / l_s.reshape(H * tq, 1)
    o_w = acc_w / l_w.reshape(H * tq, 1)
    outs = []
    for r in range(H):
        sl = slice(r * tq, (r + 1) * tq)
        outs.append(o_s[sl] * gates[:, 3 * r + 1:3 * r + 2] + o_w[sl] * gates[:, 3 * r + 2:3 * r + 3])
    o_ref[...] = (_pack_heads(outs, g) + oc_ref[...].astype(F32)).astype(o_ref.dtype)


def _nsa(nq, ks, vs, kw, vw, sel, expand, w4, gates_g, oc):
    B, S, _ = nq.shape
    tq = ATT_TILE
    tk = w4.shape[-1]
    kv = lambda: pl.BlockSpec((None, S, LANES), lambda b, g, i: (b, 0, 0))
    return pl.pallas_call(
        _nsa_kernel,
        grid=(B, NSA_GROUPS, S // tq),
        in_specs=[pl.BlockSpec((None, tq, 2 * LANES), lambda b, g, i: (b, i, g)),
                  kv(), kv(), kv(), kv(),
                  pl.BlockSpec((None, None, tq, LANES), lambda b, g, i: (b, g, i, 0)),
                  pl.BlockSpec(expand.shape, lambda b, g, i: (0, 0)),
                  pl.BlockSpec((NSA_GQA, 4, tq, tk), lambda b, g, i: (g, 0, 0, 0)),
                  pl.BlockSpec((None, None, tq, 3 * NSA_GQA), lambda b, g, i: (b, g, i, 0)),
                  pl.BlockSpec((None, tq, 2 * LANES), lambda b, g, i: (b, i, g))],
        out_specs=pl.BlockSpec((None, tq, 2 * LANES), lambda b, g, i: (b, i, g)),
        out_shape=jax.ShapeDtypeStruct((B, S, NSA_HEADS * HEAD_DIM), BF16),
        scratch_shapes=[pltpu.VMEM((S // tk, tq, tk), F32)],
        compiler_params=pltpu.CompilerParams(
            dimension_semantics=("parallel", "parallel", "arbitrary"),
            vmem_limit_bytes=VMEM_LIMIT),
        name="nsa",
    )(nq, ks, vs, kw, vw, sel, expand, w4, gates_g, oc)


def _layer_norm(y, g, b):
    mu = jnp.mean(y, axis=-1, keepdims=True)
    yc = y - mu
    var = jnp.mean(yc * yc, axis=-1, keepdims=True)
    return yc * lax.rsqrt(var + LN_EPS) * g + b


def _router_gates_t(h2, wr_t, eb_col):
    tm = h2.shape[0]
    h_parts = _split3(h2)
    w_parts = _split3(wr_t)
    logit = jnp.zeros((N_EXPERTS, tm), F32)
    for a, wp in enumerate(w_parts):
        for b, hp in enumerate(h_parts):
            if a + b <= 2:
                logit = logit + _dot_nt(wp, hp)
    scores = _sigmoid(logit)
    biased = scores + eb_col
    e_in = lax.broadcasted_iota(jnp.int32, (GROUP_SIZE, tm), 0).astype(F32)
    gs_rows = []
    for gi in range(N_EXPERT_GROUPS):
        grp = biased[gi * GROUP_SIZE:(gi + 1) * GROUP_SIZE, :]
        m1 = jnp.max(grp, axis=0, keepdims=True)
        first = jnp.min(jnp.where(grp == m1, e_in, float(GROUP_SIZE)), axis=0, keepdims=True)
        m2 = jnp.max(jnp.where(e_in == first, -jnp.inf, grp), axis=0, keepdims=True)
        gs_rows.append(m1 + m2)
    gscore = jnp.concatenate(gs_rows, axis=0)
    g_keep = _rank_rows(gscore) < TOPK_GROUPS
    keep = jnp.concatenate(
        [jnp.broadcast_to(g_keep[gi:gi + 1, :], (GROUP_SIZE, tm)) for gi in range(N_EXPERT_GROUPS)],
        axis=0)
    masked = jnp.where(keep, biased, -jnp.inf)
    chosen = _rank_rows(masked) < TOP_K
    w = jnp.where(chosen, scores, 0.0)
    return w / jnp.sum(w, axis=0, keepdims=True) * ROUTED_SCALE


def _out_proj_kernel(alpha, of_ref, on_ref, x_ref, mod_ref, w_ref, lg_ref, lb_ref, wr_ref,
                     eb_ref, x1_ref, h2_ref, gate_ref, gate_t_ref):
    half = of_ref.shape[1]
    mod = mod_ref[...]
    mixed = _dot(of_ref[...], w_ref[0:half, :]) + _dot(on_ref[...], w_ref[half:2 * half, :])
    y = alpha * x_ref[...] + mod[2:3, :] * mixed
    x1 = _layer_norm(y, lg_ref[...], lb_ref[...])
    x1_ref[...] = x1
    h2 = x1 * (1.0 + mod[4:5, :]) + mod[3:4, :]
    h2_ref[...] = h2.astype(h2_ref.dtype)
    gates_t = _router_gates_t(h2, wr_ref[...], eb_ref[...])
    gate_t_ref[...] = gates_t
    tm = h2.shape[0]
    gates_t = jnp.concatenate([gates_t, jnp.zeros((LANES - N_EXPERTS, tm), F32)], axis=0)
    gate_ref[...] = gates_t.T


def _out_proj(alpha, o_fox, o_nsa, x, mod, w_out, ln_g, ln_b, wr_t, eb_col):
    B, S, D = x.shape
    tm = ROW_TILE
    half = o_fox.shape[-1]
    row = lambda a: pl.BlockSpec(a.shape, lambda b, s: (0, 0))
    return pl.pallas_call(
        functools.partial(_out_proj_kernel, alpha),
        grid=(B, S // tm),
        in_specs=[pl.BlockSpec((None, tm, half), lambda b, s: (b, s, 0)),
                  pl.BlockSpec((None, tm, half), lambda b, s: (b, s, 0)),
                  pl.BlockSpec((None, tm, D), lambda b, s: (b, s, 0)),
                  pl.BlockSpec((None, 6, D), lambda b, s: (b, 0, 0)),
                  row(w_out), row(ln_g), row(ln_b), row(wr_t), row(eb_col)],
        out_specs=[pl.BlockSpec((None, tm, D), lambda b, s: (b, s, 0)),
                   pl.BlockSpec((None, tm, D), lambda b, s: (b, s, 0)),
                   pl.BlockSpec((None, tm, LANES), lambda b, s: (b, s, 0)),
                   pl.BlockSpec((N_EXPERTS, tm), lambda b, s: (0, b * (S // tm) + s))],
        out_shape=[jax.ShapeDtypeStruct((B, S, D), F32),
                   jax.ShapeDtypeStruct((B, S, D), BF16),
                   jax.ShapeDtypeStruct((B, S, LANES), F32),
                   jax.ShapeDtypeStruct((N_EXPERTS, B * S), F32)],
        compiler_params=pltpu.CompilerParams(dimension_semantics=("parallel", "parallel"),
                                             vmem_limit_bytes=VMEM_LIMIT),
        name="out_proj",
    )(o_fox, o_nsa, x, mod, w_out, ln_g, ln_b, wr_t, eb_col)


SORT_TILE = 256
ROW_ALIGN = 16
EXP_TILE = 512
P_CHUNK = 256
COPY_ROWS = (64, 32, 16)


def _strict_upper(n):
    return jnp.where(lax.broadcasted_iota(jnp.int32, (n, n), 0)
                     < lax.broadcasted_iota(jnp.int32, (n, n), 1), 1.0, 0.0).astype(BF16)


def _strict_lower(n):
    return jnp.where(lax.broadcasted_iota(jnp.int32, (n, n), 1)
                     < lax.broadcasted_iota(jnp.int32, (n, n), 0), 1.0, 0.0).astype(BF16)


def _local_rows_bound(ts):
    rows = TOP_K * ts + N_EXPERTS * (ROW_ALIGN - 1)
    return -(-rows // P_CHUNK) * P_CHUNK


def _sorted_tiles_bound(T):
    rows = TOP_K * T + (T // SORT_TILE) * N_EXPERTS * (ROW_ALIGN - 1)
    return -(-rows // EXP_TILE) + N_EXPERTS


def _moe_meta_kernel(gt_ref, tabs_ref, loc_et_ref, loc_te_ref, tot_ref, erow_ref, texp_ref,
                     nused_ref):
    E, T = gt_ref.shape
    mask = jnp.where(gt_ref[...] > 0.0, 1.0, 0.0).astype(BF16)
    t_id = lax.shift_right_logical(lax.broadcasted_iota(jnp.int32, (T, LANES), 0),
                                   int(math.log2(SORT_TILE)))
    tind = jnp.where(t_id == lax.broadcasted_iota(jnp.int32, (T, LANES), 1), 1.0, 0.0)
    cnt = _dot(mask, tind.astype(BF16))
    n16 = jnp.floor((cnt + (ROW_ALIGN - 1.0)) * (1.0 / ROW_ALIGN))
    n16b = n16.astype(BF16)
    q = EXP_TILE // ROW_ALIGN
    len16 = jnp.sum(n16, axis=1, keepdims=True)
    pad16 = jnp.floor((len16 + (q - 1.0)) * (1.0 / q)) * q
    sl = _strict_lower(E)
    hi, mid, lo = _split3(jnp.broadcast_to(pad16, (E, LANES)))
    start16 = _dot(sl, hi) + _dot(sl, mid) + _dot(sl, lo)
    gdst16 = start16 + _dot(n16b, _strict_upper(LANES))
    loc16 = _dot(sl, n16b)

    def t(a):
        return jnp.concatenate([a, jnp.zeros((LANES - E, LANES), F32)], axis=0).T

    scale = float(ROW_ALIGN)
    tabs_ref[0] = (t(gdst16) * scale).astype(jnp.int32)
    tabs_ref[1] = (t(loc16) * scale).astype(jnp.int32)
    tabs_ref[2] = (t(n16) * scale).astype(jnp.int32)
    loc_et_ref[...] = loc16 * scale
    loc_te_ref[...] = t(loc16) * scale
    tot_ref[...] = (jnp.sum(n16, axis=0, keepdims=True) * scale).astype(jnp.int32)
    ends = jnp.concatenate([t(start16 + len16)[0:1, :], t(start16 + pad16)[0:1, :],
                            jnp.zeros((erow_ref.shape[0] - 2, LANES), F32)], axis=0)
    erow_ref[...] = (ends * scale).astype(jnp.int32)
    n_tab = texp_ref.shape[1]
    tile_row16 = (lax.broadcasted_iota(jnp.int32, (E, n_tab), 1) * q).astype(F32)
    owner = jnp.sum(jnp.where(start16[:, 0:1] <= tile_row16, 1.0, 0.0), axis=0, keepdims=True)
    texp_ref[...] = (owner - 1.0).astype(jnp.int32)
    n_used = jnp.sum(pad16, axis=0, keepdims=True) * (1.0 / q)
    nused_ref[...] = jnp.broadcast_to(n_used, (1, LANES)).astype(jnp.int32)


def _moe_meta(gates_t, n_tab):
    E, T = gates_t.shape
    i32 = jnp.int32
    return pl.pallas_call(
        _moe_meta_kernel,
        out_shape=[jax.ShapeDtypeStruct((3, LANES, LANES), i32),
                   jax.ShapeDtypeStruct((E, LANES), F32),
                   jax.ShapeDtypeStruct((LANES, LANES), F32),
                   jax.ShapeDtypeStruct((1, LANES), i32),
                   jax.ShapeDtypeStruct((8, LANES), i32),
                   jax.ShapeDtypeStruct((1, n_tab), i32),
                   jax.ShapeDtypeStruct((1, LANES), i32)],
        compiler_params=pltpu.CompilerParams(vmem_limit_bytes=VMEM_LIMIT),
        name="moe_meta",
    )(gates_t)


def _group_pieces(n, fn):
    big = COPY_ROWS[0]
    n_big = lax.shift_right_logical(n, int(math.log2(big)))

    def body(c, carry):
        fn(c * big, big)
        return carry

    lax.fori_loop(0, n_big, body, 0)
    off = n_big * big
    for size in COPY_ROWS[1:]:
        bit = jnp.bitwise_and(n, size)

        @pl.when(bit != 0)
        def _(off=off, size=size):
            fn(off, size)

        off = off + bit


def _tile_copies(tile, n_exp, tabs, make_copy, wait):
    gdst_s, loc_s, npd_s = tabs

    def e_body(e, carry):
        idx = tile * n_exp + e
        dst0 = gdst_s[idx]
        loc0 = loc_s[idx]

        def fn(off, rows):
            cp = make_copy(pl.multiple_of(loc0 + off, ROW_ALIGN),
                           pl.multiple_of(dst0 + off, ROW_ALIGN), rows)
            if wait:
                cp.wait()
            else:
                cp.start()

        _group_pieces(npd_s[idx], fn)
        return carry

    lax.fori_loop(0, n_exp, e_body, 0)


def _moe_sort_kernel(gdst_s, loc_s, npd_s, tot_s, lend_s, rend_s, gt_ref, h_ref, loc_ref,
                     xs_hbm, buf, zbuf, sem, zsem):
    tau = pl.program_id(0)
    n_t = pl.num_programs(0)
    slot = lax.rem(tau, 2)
    E, ts = gt_ref.shape
    tabs = (gdst_s, loc_s, npd_s)

    def copies(tile, sl, wait):
        def make_copy(loc, dst, rows):
            return pltpu.make_async_copy(buf.at[sl, pl.ds(loc, rows)],
                                         xs_hbm.at[pl.ds(dst, rows)], sem.at[sl])
        _tile_copies(tile, E, tabs, make_copy, wait)

    z_rows = zbuf.shape[0]
    used_rows = rend_s[E - 1]
    n_spare = (xs_hbm.shape[0] - used_rows) // z_rows

    def spare_fill(wait):
        def body(c, carry):
            dst = pl.multiple_of(used_rows + c * z_rows, z_rows)
            cp = pltpu.make_async_copy(zbuf, xs_hbm.at[pl.ds(dst, z_rows)], zsem.at[1])
            if wait:
                cp.wait()
            else:
                cp.start()
            return carry

        lax.fori_loop(0, n_spare, body, 0)

    @pl.when(tau == 0)
    def _():
        zbuf[...] = jnp.zeros_like(zbuf)
        spare_fill(False)

    @pl.when(tau >= 2)
    def _():
        copies(tau - 2, slot, True)

    g = gt_ref[...]
    mask = g > 0.0
    maskb = jnp.where(mask, 1.0, 0.0).astype(BF16)
    pos = _dot(maskb, _strict_upper(ts))
    rank = _dot(_strict_lower(E), maskb)
    lane = lax.broadcasted_iota(jnp.int32, loc_ref.shape, 1)
    loc_col = jnp.sum(jnp.where(lane == tau, loc_ref[...], 0.0), axis=1, keepdims=True)
    lrow = loc_col + pos
    rows_k = []
    for k in range(TOP_K):
        pick = mask & (rank == float(k))
        found = jnp.sum(jnp.where(pick, 1.0, 0.0), axis=0, keepdims=True)
        rows_k.append(jnp.sum(jnp.where(pick, lrow, 0.0), axis=0, keepdims=True) + (found - 1.0))

    h = h_ref[...]
    n_chunks = lax.shift_right_logical(tot_s[tau] + (P_CHUNK - 1), int(math.log2(P_CHUNK)))

    def chunk_body(c, carry):
        r0 = pl.multiple_of(c * P_CHUNK, P_CHUNK)
        r = (r0 + lax.broadcasted_iota(jnp.int32, (P_CHUNK, ts), 0)).astype(F32)
        onehot = jnp.zeros((P_CHUNK, ts), F32)
        for k in range(TOP_K):
            onehot = jnp.where(r == rows_k[k], 1.0, onehot)
        buf[slot, pl.ds(r0, P_CHUNK), :] = _dot(onehot.astype(BF16), h).astype(buf.dtype)
        return carry

    lax.fori_loop(0, n_chunks, chunk_body, 0)
    copies(tau, slot, False)

    @pl.when(tau == n_t - 1)
    def _():
        @pl.when(n_t >= 2)
        def _():
            copies(tau - 1, 1 - slot, True)
        copies(tau, slot, True)
        spare_fill(True)

        sizes =[zbuf.shape[0] >> s for s in range(int(math.log2(zbuf.shape[0] // ROW_ALIGN)) + 1)]

        def fill(wait):
            def e_body(e, carry):
                start = lend_s[e]
                n = rend_s[e] - start
                off = start
                for size in sizes:
                    bit = jnp.bitwise_and(n, size)

                    @pl.when(bit != 0)
                    def _(off=off, size=size):
                        cp = pltpu.make_async_copy(
                            zbuf.at[pl.ds(0, size)],
                            xs_hbm.at[pl.ds(pl.multiple_of(off, ROW_ALIGN), size)], zsem.at[0])
                        if wait:
                            cp.wait()
                        else:
                            cp.start()

                    off = off + bit
                return carry

            lax.fori_loop(0, E, e_body, 0)

        fill(False)
        fill(True)


def _moe_sort(tabs, tot, lend, rend, gates_t, h2, loc_et, n_rows):
    E, T = gates_t.shape
    D = h2.shape[1]
    ts = SORT_TILE
    grid_spec = pltpu.PrefetchScalarGridSpec(
        num_scalar_prefetch=6,
        grid=(T // ts,),
        in_specs=[pl.BlockSpec((E, ts), lambda t, *_: (0, t)),
                  pl.BlockSpec((ts, D), lambda t, *_: (t, 0)),
                  pl.BlockSpec(loc_et.shape, lambda t, *_: (0, 0))],
        out_specs=pl.BlockSpec(memory_space=pl.ANY),
        scratch_shapes=[pltpu.VMEM((2, _local_rows_bound(ts), D), BF16),
                        pltpu.VMEM((EXP_TILE // 2, D), BF16),
                        pltpu.SemaphoreType.DMA((2,)),
                        pltpu.SemaphoreType.DMA((2,))])
    return pl.pallas_call(
        _moe_sort_kernel,
        grid_spec=grid_spec,
        out_shape=jax.ShapeDtypeStruct((n_rows, D), BF16),
        compiler_params=pltpu.CompilerParams(dimension_semantics=("arbitrary",),
                                             vmem_limit_bytes=VMEM_LIMIT),
        name="moe_sort",
    )(*tabs, tot, lend, rend, gates_t, h2, loc_et)


def _moe_expert_kernel(texp_s, nused_s, x_ref, wg_ref, wu_ref, wd_ref, y_ref, wgu_s, wd_s):
    i = pl.program_id(0)
    f = wd_ref.shape[0]

    @pl.when(i < nused_s[0])
    def _():
        @pl.when((i == 0) | (texp_s[i] != texp_s[jnp.maximum(i - 1, 0)]))
        def _():
            wgu_s[:, 0:f] = wg_ref[...].astype(BF16)
            wgu_s[:, f:2 * f] = wu_ref[...].astype(BF16)
            wd_s[...] = wd_ref[...].astype(BF16)

        a = _dot(x_ref[...], wgu_s[...])
        act = _silu(a[:, :f]) * a[:, f:]
        y_ref[...] = _dot(act.astype(BF16), wd_s[...]).astype(y_ref.dtype)

    @pl.when(i >= nused_s[0])
    def _():
        y_ref[...] = jnp.zeros_like(y_ref)


def _moe_expert(texp, nused, xs, w_gate, w_up, w_down, n_tiles):
    n_rows, D = xs.shape
    f = w_gate.shape[-1]
    tm = EXP_TILE

    def tile(i, texp, nused):
        return jnp.maximum(jnp.minimum(i, nused[0] - 1), 0)

    grid_spec = pltpu.PrefetchScalarGridSpec(
        num_scalar_prefetch=2,
        grid=(n_tiles,),
        in_specs=[pl.BlockSpec((tm, D), lambda i, te, nu: (tile(i, te, nu), 0)),
                  pl.BlockSpec((None, D, f), lambda i, te, nu: (te[tile(i, te, nu)], 0, 0)),
                  pl.BlockSpec((None, D, f), lambda i, te, nu: (te[tile(i, te, nu)], 0, 0)),
                  pl.BlockSpec((None, f, D), lambda i, te, nu: (te[tile(i, te, nu)], 0, 0))],
        out_specs=pl.BlockSpec((tm, D), lambda i, te, nu: (i, 0)),
        scratch_shapes=[pltpu.VMEM((D, 2 * f), BF16), pltpu.VMEM((f, D), BF16)])
    return pl.pallas_call(
        _moe_expert_kernel,
        grid_spec=grid_spec,
        out_shape=jax.ShapeDtypeStruct((n_rows, D), BF16),
        compiler_params=pltpu.CompilerParams(dimension_semantics=("arbitrary",),
                                             vmem_limit_bytes=VMEM_LIMIT),
        name="moe_expert",
    )(texp, nused, xs, w_gate, w_up, w_down)


def _moe_combine_kernel(alpha, gdst_s, loc_s, npd_s, tot_s, g_ref, locrow_ref, h_ref, x1_ref,
                        mod_ref, sgu_ref, sd_ref, lg_ref, lb_ref, y_hbm, o_ref, ybuf, acc_ref,
                        sem):
    tau = pl.program_id(0)
    n_t = pl.num_programs(0)
    slot = lax.rem(tau, 2)
    ts, n_lane = g_ref.shape
    tabs = (gdst_s, loc_s, npd_s)

    def copies(tile, sl, wait):
        def make_copy(loc, dst, rows):
            return pltpu.make_async_copy(y_hbm.at[pl.ds(dst, rows)],
                                         ybuf.at[sl, pl.ds(loc, rows)], sem.at[sl])
        _tile_copies(tile, N_EXPERTS, tabs, make_copy, wait)

    @pl.when(tau == 0)
    def _():
        ybuf[...] = jnp.zeros_like(ybuf)
        copies(0, 0, False)

    @pl.when(tau + 1 < n_t)
    def _():
        copies(tau + 1, 1 - slot, False)

    g = g_ref[...]
    mask = g > 0.0
    maskb = jnp.where(mask, 1.0, 0.0).astype(BF16)
    rank = _dot(maskb, _strict_upper(n_lane))
    pos = _dot(_strict_lower(ts), maskb)
    lrow = locrow_ref[...] + pos
    rows_k, gate_k = [], []
    for k in range(TOP_K):
        pick = mask & (rank == float(k))
        found = jnp.sum(jnp.where(pick, 1.0, 0.0), axis=1, keepdims=True)
        rows_k.append(jnp.sum(jnp.where(pick, lrow, 0.0), axis=1, keepdims=True) + (found - 1.0))
        gate_k.append(jnp.sum(jnp.where(pick, g, 0.0), axis=1, keepdims=True))

    f = sd_ref.shape[0]
    a = _dot(h_ref[...], sgu_ref[...])
    acc_ref[...] = _dot((_silu(a[:, :f]) * a[:, f:]).astype(BF16), sd_ref[...])

    copies(tau, slot, True)
    n_chunks = lax.shift_right_logical(tot_s[tau] + (P_CHUNK - 1), int(math.log2(P_CHUNK)))

    def chunk_body(c, carry):
        r0 = pl.multiple_of(c * P_CHUNK, P_CHUNK)
        r = (r0 + lax.broadcasted_iota(jnp.int32, (ts, P_CHUNK), 1)).astype(F32)
        w = jnp.zeros((ts, P_CHUNK), F32)
        for k in range(TOP_K):
            w = jnp.where(r == rows_k[k], gate_k[k], w)
        hi = w.astype(BF16)
        lo = (w - hi.astype(F32)).astype(BF16)
        yb = ybuf[slot, pl.ds(r0, P_CHUNK), :]
        acc_ref[...] += _dot(hi, yb) + _dot(lo, yb)
        return carry

    lax.fori_loop(0, n_chunks, chunk_body, 0)
    y = alpha * x1_ref[...] + mod_ref[5:6, :] * acc_ref[...]
    o_ref[...] = _layer_norm(y, lg_ref[...], lb_ref[...])


def _moe_combine(alpha, tabs, tot, gates, loc_te, h2, x1, mod, sgu, sd, ln_g, ln_b, ys, S):
    T, D = h2.shape
    ts = SORT_TILE
    per_b = S // ts
    row = lambda a: pl.BlockSpec(a.shape, lambda t, *_: (0, 0))
    grid_spec = pltpu.PrefetchScalarGridSpec(
        num_scalar_prefetch=4,
        grid=(T // ts,),
        in_specs=[pl.BlockSpec((ts, LANES), lambda t, *_: (t, 0)),
                  pl.BlockSpec((None, 1, LANES), lambda t, *_: (t, 0, 0)),
                  pl.BlockSpec((ts, D), lambda t, *_: (t, 0)),
                  pl.BlockSpec((ts, D), lambda t, *_: (t, 0)),
                  pl.BlockSpec((None, 6, D), lambda t, *_: (t // per_b, 0, 0)),
                  row(sgu), row(sd), row(ln_g), row(ln_b),
                  pl.BlockSpec(memory_space=pl.ANY)],
        out_specs=pl.BlockSpec((ts, D), lambda t, *_: (t, 0)),
        scratch_shapes=[pltpu.VMEM((2, _local_rows_bound(ts), D), BF16),
                        pltpu.VMEM((ts, D), F32),
                        pltpu.SemaphoreType.DMA((2,))])
    return pl.pallas_call(
        functools.partial(_moe_combine_kernel, alpha),
        grid_spec=grid_spec,
        out_shape=jax.ShapeDtypeStruct((T, D), F32),
        compiler_params=pltpu.CompilerParams(dimension_semantics=("arbitrary",),
                                             vmem_limit_bytes=VMEM_LIMIT),
        name="moe_combine",
    )(*tabs, tot, gates, loc_te, h2, x1, mod, sgu, sd, ln_g, ln_b, ys)


def _moe(alpha, h2, x1, gates, gates_t, mod, w_gate, w_up, w_down, sgu, sd, ln_g, ln_b):
    B, S, D = x1.shape
    T = B * S
    n_t = T // SORT_TILE
    n_tiles = _sorted_tiles_bound(T)
    n_tab = -(-n_tiles // LANES) * LANES
    tabs3, loc_et, loc_te, tot, erow, texp, nused = _moe_meta(gates_t, n_tab)
    tabs = tuple(tabs3[k, :n_t, :N_EXPERTS].reshape(-1) for k in range(3))
    tot = tot[0, :n_t]
    h2 = h2.reshape(T, D)
    xs = _moe_sort(tabs, tot, erow[0, :N_EXPERTS], erow[1, :N_EXPERTS], gates_t, h2, loc_et,
                   n_tiles * EXP_TILE)
    ys = _moe_expert(texp[0], nused[0, :1], xs, w_gate, w_up, w_down, n_tiles)
    out = _moe_combine(alpha, tabs, tot, gates.reshape(T, LANES),
                       loc_te[:n_t].reshape(n_t, 1, LANES), h2, x1.reshape(T, D), mod, sgu, sd,
                       ln_g, ln_b, ys, S)
    return out.reshape(B, S, D)


def _rearrange_w_in(w):
    scale = HEAD_DIM ** -0.5
    fq, fk, fv = w[:, 0:512], w[:, 512:1024], w[:, 1024:1536]
    ff = w[:, 1536:1544]
    nq = w[:, 1544:2056]
    kv = w[:, 2056:2824]
    ng = w[:, 2824:2848]
    pad = jnp.zeros((w.shape[0], LANES - ff.shape[1] - ng.shape[1]), w.dtype)
    return jnp.concatenate([fq * scale, fk, fv, nq * scale, kv, ff, ng, pad], axis=1).astype(BF16)


def _compress_weights(pos, w1, w2):
    half = CMP_BLOCK // 2
    w1r = w1.reshape(2, half, HEAD_DIM, CMP_HIDDEN)
    zeros = jnp.zeros_like(w1r[0])
    def spread(part):
        g0 = jnp.stack([part, zeros], axis=1).reshape(half * 2 * HEAD_DIM, CMP_HIDDEN)
        g1 = jnp.stack([zeros, part], axis=1).reshape(half * 2 * HEAD_DIM, CMP_HIDDEN)
        return jnp.concatenate([g0, g1], axis=1).astype(BF16)
    wa, wb = spread(w1r[0]), spread(w1r[1])
    z2 = jnp.zeros_like(w2)
    w2bd = jnp.concatenate([jnp.concatenate([w2, z2], axis=1),
                            jnp.concatenate([z2, w2], axis=1)], axis=0).astype(BF16)
    posr = pos.reshape(2, half, 1, HEAD_DIM)
    posr = jnp.broadcast_to(posr, (2, half, NSA_GROUPS, HEAD_DIM)).reshape(2, half * 2 * HEAD_DIM)
    return posr, wa, wb, w2bd


@functools.lru_cache(maxsize=None)
def _static_tables(S):
    tq = ATT_TILE
    n_cmp = (S - CMP_BLOCK) // CMP_STRIDE + 1
    n_pad = S // CMP_STRIDE
    n_slc = S // SEL_BLOCK
    t = np.arange(S)[:, None]
    n = np.arange(n_pad)[None, :]
    bucket_c = _t5_bucket_np(t - (n * CMP_STRIDE + CMP_BLOCK - 1)).reshape(1, -1)
    d = (np.arange(4)[:, None, None] * tq + np.arange(tq)[None, :, None]
         - np.arange(2 * tq)[None, None, :])
    bucket_w = _t5_bucket_np(d).reshape(1, -1)
    cs = np.arange(n_pad)[None, :] * CMP_STRIDE
    sj = np.arange(n_slc)[:, None] * SEL_BLOCK
    ovl_t = ((cs < sj + SEL_BLOCK) & (cs + CMP_BLOCK > sj) & (np.arange(n_pad)[None, :] < n_cmp))
    expand = (np.arange(S)[None, :] // SEL_BLOCK) == np.arange(LANES)[:, None]
    return bucket_c, bucket_w, ovl_t.astype(np.float32), expand.astype(np.float32)


def kernel(x, c, w_ada, b_ada, w_in, b_f, cmp_pos_k, cmp_w1_k, cmp_w2_k, cmp_pos_v, cmp_w1_v,
           cmp_w2_v, rel_bias, w_out, ln1_g, ln1_b, w_router, e_bias, w_gate, w_up, w_down,
           ws_gate, ws_up, ws_down, ln2_g, ln2_b):
    B, S, D = x.shape
    depth = w_ada.shape[0]
    alpha = (2 * depth) ** 0.25
    tq = ATT_TILE
    bucket_c, bucket_w, ovl_t, expand = _static_tables(S)
    rel_bias_t = rel_bias.T
    bias_c = _bias_table(jnp.asarray(bucket_c), rel_bias_t).reshape(NSA_HEADS, S, S // CMP_STRIDE)
    w4 = _bias_table(jnp.asarray(bucket_w), rel_bias_t).reshape(NSA_HEADS, 4, tq, 2 * tq)
    ovl_t = jnp.asarray(ovl_t, BF16)
    expand = jnp.asarray(expand, BF16)

    for l in range(depth):
        mod = _ada(c, w_ada[l], b_ada[l]).reshape(B, 6, D)
        bf_row = jnp.zeros((1, LANES), F32).at[0, :FOX_HEADS].set(b_f[l])
        (fq, fk, fv, nq, kc, vc, ks, vs, kw, vw, misc, misc_t) = _in_proj(
            x, mod, _rearrange_w_in(w_in[l]), bf_row)

        o_fox = _fox(fq, fk, fv, misc_t[:, :FOX_HEADS, :])

        pk, wak, wbk, w2k = _compress_weights(cmp_pos_k[l], cmp_w1_k[l], cmp_w2_k[l])
        pv, wav, wbv, w2v = _compress_weights(cmp_pos_v[l], cmp_w1_v[l], cmp_w2_v[l])
        rows = S // CMP_STRIDE
        kcmp, vcmp = _compress(kc.reshape(B, rows, CMP_STRIDE * LANES),
                               vc.reshape(B, rows, CMP_STRIDE * LANES),
                               pk, pv, wak, wbk, wav, wbv, w2k, w2v)

        gates_g = misc[:, :, FOX_HEADS:FOX_HEADS + 3 * NSA_HEADS].reshape(
            B, S, NSA_GROUPS, 3 * NSA_GQA).transpose(0, 2, 1, 3)
        oc, sel = _cmp_sel(nq, kcmp, vcmp, bias_c, gates_g, ovl_t)
        o_nsa = _nsa(nq, ks, vs, kw, vw, sel, expand, w4, gates_g, oc)

        x1, h2, gates, gates_t = _out_proj(
            alpha, o_fox, o_nsa, x, mod, w_out[l].astype(BF16), ln1_g[l].reshape(1, D),
            ln1_b[l].reshape(1, D), w_router[l].T, e_bias[l].reshape(N_EXPERTS, 1))

        sgu = jnp.concatenate([ws_gate[l], ws_up[l]], axis=-1).astype(BF16)
        x = _moe(alpha, h2, x1, gates, gates_t, mod, w_gate[l], w_up[l], w_down[l], sgu,
                 ws_down[l].astype(BF16), ln2_g[l].reshape(1, D), ln2_b[l].reshape(1, D))
    return x
```

```python
import functools
import math

import jax
import jax.numpy as jnp
import numpy as np
from jax import lax
from jax.experimental import pallas as pl
from jax.experimental.pallas import tpu as pltpu

F32 = jnp.float32
BF16 = jnp.bfloat16

HEAD_DIM = 64
FOX_HEADS = 8
NSA_HEADS = 8
NSA_GQA = 4
NSA_GROUPS = NSA_HEADS // NSA_GQA
CMP_BLOCK = 32
CMP_STRIDE = 16
CMP_HIDDEN = 256
SEL_BLOCK = 64
N_SEL = 16
WINDOW = 512
N_BUCKETS = 32
MAX_DISTANCE = 128
N_EXPERTS = 64
N_EXPERT_GROUPS = 8
GROUP_SIZE = N_EXPERTS // N_EXPERT_GROUPS
TOPK_GROUPS = 4
TOP_K = 8
D_EXPERT = 256
ROUTED_SCALE = 2.5
LN_EPS = 1e-5
NEG_BIG = -1e30
FORCE_SCORE = 1e4

LANES = 128
ATT_TILE = 256
FOX_TILE = 512
ROW_TILE = 512
MOE_TILE = 1024
VMEM_LIMIT = 48 * 1024 * 1024

NT_DIMS = (((1,), (1,)), ((), ()))


def _dot(a, b):
    return jnp.dot(a, b, preferred_element_type=F32)


def _dot_nt(a, b):
    return lax.dot_general(a, b, NT_DIMS, preferred_element_type=F32)


def _split3(x):
    hi = x.astype(BF16)
    r1 = x - hi.astype(F32)
    mid = r1.astype(BF16)
    lo = (r1 - mid.astype(F32)).astype(BF16)
    return hi, mid, lo


def _silu(x):
    return x / (1.0 + jnp.exp(-x))


def _sigmoid(x):
    return 1.0 / (1.0 + jnp.exp(-x))


def _swap_halves(x):
    return pltpu.roll(x, HEAD_DIM, 1)


def _t5_bucket_np(dist):
    n = np.maximum(dist, 0)
    max_exact = N_BUCKETS // 2
    nf = np.maximum(n, 1).astype(np.float32)
    large = max_exact + (np.log(nf / max_exact) / math.log(MAX_DISTANCE / max_exact)
                         * (N_BUCKETS - max_exact)).astype(np.int32)
    large = np.minimum(large, N_BUCKETS - 1)
    return np.where(n < max_exact, n, large).astype(np.int32)


def _ada_kernel(c_ref, w_ref, b_ref, o_ref):
    c = c_ref[...]
    o_ref[...] = jnp.dot(_silu(c), w_ref[...], preferred_element_type=F32,
                         precision=lax.Precision.HIGHEST) + b_ref[...]


def _ada(c, w_ada, b_ada):
    B, D = c.shape
    n_out = w_ada.shape[1]
    tn = 1024
    return pl.pallas_call(
        _ada_kernel,
        grid=(n_out // tn,),
        in_specs=[pl.BlockSpec((B, D), lambda j: (0, 0)),
                  pl.BlockSpec((D, tn), lambda j: (0, j)),
                  pl.BlockSpec((1, tn), lambda j: (0, j))],
        out_specs=pl.BlockSpec((B, tn), lambda j: (0, j)),
        out_shape=jax.ShapeDtypeStruct((B, n_out), F32),
        compiler_params=pltpu.CompilerParams(dimension_semantics=("arbitrary",),
                                             vmem_limit_bytes=VMEM_LIMIT),
        name="ada",
    )(c, w_ada, b_ada.reshape(1, n_out))


def _bias_table_kernel(bkt_ref, rbt_ref, o_ref):
    bkt = bkt_ref[...]
    k = lax.broadcasted_iota(jnp.int32, (N_BUCKETS, bkt.shape[1]), 0)
    onehot = jnp.where(k == bkt, 1.0, 0.0).astype(BF16)
    hi, mid, lo = _split3(rbt_ref[...])
    o_ref[...] = _dot(hi, onehot) + _dot(mid, onehot) + _dot(lo, onehot)


def _bias_table(bucket, rel_bias_t):
    n = bucket.shape[1]
    chunk = 8192
    n_heads = rel_bias_t.shape[0]
    return pl.pallas_call(
        _bias_table_kernel,
        grid=(n // chunk,),
        in_specs=[pl.BlockSpec((1, chunk), lambda j: (0, j)),
                  pl.BlockSpec(rel_bias_t.shape, lambda j: (0, 0))],
        out_specs=pl.BlockSpec((n_heads, chunk), lambda j: (0, j)),
        out_shape=jax.ShapeDtypeStruct((n_heads, n), F32),
        compiler_params=pltpu.CompilerParams(dimension_semantics=("parallel",),
                                             vmem_limit_bytes=VMEM_LIMIT),
        name="bias_table",
    )(bucket, rel_bias_t)


_C_FQ, _C_FK, _C_FV, _C_NQ = 0, 512, 1024, 1536
_C_KV = 2048
_C_MISC = 2816
_IN_COLS = 2944


def _in_proj_kernel(x_ref, mod_ref, w_ref, bf_ref, fq_ref, fk_ref, fv_ref, nq_ref,
                    kc_ref, vc_ref, ks_ref, vs_ref, kw_ref, vw_ref, misc_ref, misct_ref,
                    carry_ref):
    s_idx = pl.program_id(1)
    tm = x_ref.shape[0]
    mod = mod_ref[...]
    h = (x_ref[...] * (1.0 + mod[1:2, :]) + mod[0:1, :]).astype(BF16)

    for ref, c0 in ((fq_ref, _C_FQ), (fk_ref, _C_FK), (fv_ref, _C_FV), (nq_ref, _C_NQ)):
        ref[...] = _dot(h, w_ref[:, c0:c0 + 512]).astype(ref.dtype)
    for k, ref in enumerate((kc_ref, vc_ref, ks_ref, vs_ref, kw_ref, vw_ref)):
        c0 = _C_KV + k * LANES
        ref[...] = _dot(h, w_ref[:, c0:c0 + LANES]).astype(ref.dtype)

    z = _dot(h, w_ref[:, _C_MISC:_C_MISC + LANES]) + bf_ref[...]
    lane = lax.broadcasted_iota(jnp.int32, z.shape, 1)
    is_f = lane < FOX_HEADS
    log_f = jnp.minimum(z, 0.0) - jnp.log(1.0 + jnp.exp(-jnp.abs(z)))
    log_f = jnp.where(is_f, log_f, 0.0)

    row = lax.broadcasted_iota(jnp.int32, (tm, tm), 0)
    col = lax.broadcasted_iota(jnp.int32, (tm, tm), 1)
    tri = jnp.where(row >= col, 1.0, 0.0).astype(BF16)
    hi, mid, lo = _split3(log_f)
    local = _dot(tri, hi) + _dot(tri, mid) + _dot(tri, lo)

    @pl.when(s_idx == 0)
    def _():
        carry_ref[...] = jnp.zeros_like(carry_ref)

    cum = local + carry_ref[...]
    carry_ref[...] = cum[tm - 1:tm, :]
    misc = jnp.where(is_f, cum, _sigmoid(z))
    misc_ref[...] = misc
    misct_ref[...] = misc.T


def _in_proj(x, mod, w_r, bf_row):
    B, S, D = x.shape
    tm = ROW_TILE
    big = lambda: pl.BlockSpec((None, tm, 512), lambda b, s: (b, s, 0))
    small = lambda: pl.BlockSpec((None, tm, LANES), lambda b, s: (b, s, 0))
    out_shape = ([jax.ShapeDtypeStruct((B, S, 512), BF16)] * 4
                 + [jax.ShapeDtypeStruct((B, S, LANES), BF16)] * 6
                 + [jax.ShapeDtypeStruct((B, S, LANES), F32),
                    jax.ShapeDtypeStruct((B, LANES, S), F32)])
    out_specs = ([big() for _ in range(4)] + [small() for _ in range(6)]
                 + [small(), pl.BlockSpec((None, LANES, tm), lambda b, s: (b, 0, s))])
    return pl.pallas_call(
        _in_proj_kernel,
        grid=(B, S // tm),
        in_specs=[pl.BlockSpec((None, tm, D), lambda b, s: (b, s, 0)),
                  pl.BlockSpec((None, 6, D), lambda b, s: (b, 0, 0)),
                  pl.BlockSpec((D, _IN_COLS), lambda b, s: (0, 0)),
                  pl.BlockSpec((1, LANES), lambda b, s: (0, 0))],
        out_specs=out_specs,
        out_shape=out_shape,
        scratch_shapes=[pltpu.VMEM((1, LANES), F32)],
        compiler_params=pltpu.CompilerParams(dimension_semantics=("parallel", "arbitrary"),
                                             vmem_limit_bytes=VMEM_LIMIT),
        name="in_proj",
    )(x, mod, w_r, bf_row)


def _flash_update(carry, s, vt):
    m, l, acc = carry
    m_new = jnp.maximum(m, jnp.max(s, axis=-1, keepdims=True))
    alpha = jnp.exp(m - m_new)
    p = jnp.exp(s - m_new)
    l = alpha * l + jnp.sum(p, axis=-1, keepdims=True)
    rows = acc.shape[0]
    pv = _dot(p.reshape(rows, s.shape[-1]).astype(BF16), vt)
    acc = alpha.reshape(rows, 1) * acc + pv
    return m_new, l, acc


def _fox_kernel(q_ref, k_ref, v_ref, ck_ref, o_ref):
    i = pl.program_id(2)
    tq = q_ref.shape[0]
    tk = ck_ref.shape[2]
    q2 = q_ref[...].astype(F32)
    lane = lax.broadcasted_iota(jnp.int32, (tq, LANES), 1)
    low = lane < HEAD_DIM
    halves = (low, jnp.logical_not(low))
    qh = [jnp.where(h, q2, 0.0).astype(BF16) for h in halves]
    col_minus_row = (lax.broadcasted_iota(jnp.int32, (tq, tk), 1)
                     - lax.broadcasted_iota(jnp.int32, (tq, tk), 0))

    def step(jj, carry, diagonal):
        k0 = pl.multiple_of(jj * tk, tk)
        kt = k_ref[pl.ds(k0, tk), :]
        vt = v_ref[pl.ds(k0, tk), :]
        new = []
        for hh in range(2):
            s = _dot_nt(qh[hh], kt) - ck_ref[hh, pl.ds(jj, 1), :]
            if diagonal:
                s = jnp.where(col_minus_row <= i * tq - jj * tk, s, NEG_BIG)
            new.append(_flash_update(carry[hh], s, vt))
        return tuple(new)

    init = tuple((jnp.full((tq, 1), NEG_BIG, F32), jnp.zeros((tq, 1), F32),
                  jnp.zeros((tq, LANES), F32)) for _ in range(2))
    n_full = (i * tq) // tk
    carry = lax.fori_loop(0, n_full, lambda jj, c: step(jj, c, False), init)
    carry = step(n_full, carry, True)
    outs = [acc / l for (_, l, acc) in carry]
    o_ref[...] = jnp.where(low, outs[0], outs[1]).astype(o_ref.dtype)


def _fox(fq, fk, fv, cum_row):
    B, S, W = fq.shape
    tq = tk = FOX_TILE
    n_pairs = W // LANES
    cum_row = cum_row.reshape(B, n_pairs, 2, S // tk, tk)
    return pl.pallas_call(
        _fox_kernel,
        grid=(B, n_pairs, S // tq),
        in_specs=[pl.BlockSpec((None, tq, LANES), lambda b, p, i: (b, i, p)),
                  pl.BlockSpec((None, S, LANES), lambda b, p, i: (b, 0, p)),
                  pl.BlockSpec((None, S, LANES), lambda b, p, i: (b, 0, p)),
                  pl.BlockSpec((None, None, 2, S // tk, tk), lambda b, p, i: (b, p, 0, 0, 0))],
        out_specs=pl.BlockSpec((None, tq, LANES), lambda b, p, i: (b, i, p)),
        out_shape=jax.ShapeDtypeStruct((B, S, W), BF16),
        compiler_params=pltpu.CompilerParams(
            dimension_semantics=("parallel", "parallel", "arbitrary"),
            vmem_limit_bytes=VMEM_LIMIT),
        name="fox",
    )(fq, fk, fv, cum_row)


def _compress_kernel(xk_ref, xv_ref, pk_ref, pv_ref, wak_ref, wbk_ref, wav_ref, wbv_ref,
                     w2k_ref, w2v_ref, ok_ref, ov_ref):
    n_rows = xk_ref.shape[0]
    for x_ref, p_ref, wa_ref, wb_ref, w2_ref, o_ref in (
            (xk_ref, pk_ref, wak_ref, wbk_ref, w2k_ref, ok_ref),
            (xv_ref, pv_ref, wav_ref, wbv_ref, w2v_ref, ov_ref)):
        x = x_ref[...].astype(F32)
        xa = (x + p_ref[0:1, :]).astype(BF16)
        xb = (x + p_ref[1:2, :]).astype(BF16)
        hb = _dot(xb, wb_ref[...])
        h1 = _dot(xa, wa_ref[...]) + pltpu.roll(hb, n_rows - 1, 0)
        o_ref[...] = _dot(_silu(h1).astype(BF16), w2_ref[...]).astype(o_ref.dtype)


def _compress(xk, xv, pk, pv, wak, wbk, wav, wbv, w2k, w2v):
    B, R, C = xk.shape
    xspec = pl.BlockSpec((None, R, C), lambda b: (b, 0, 0))
    full = lambda a: pl.BlockSpec(a.shape, lambda b: (0,) * a.ndim)
    ospec = pl.BlockSpec((None, R, LANES), lambda b: (b, 0, 0))
    return pl.pallas_call(
        _compress_kernel,
        grid=(B,),
        in_specs=[xspec, xspec] + [full(a) for a in (pk, pv, wak, wbk, wav, wbv, w2k, w2v)],
        out_specs=[ospec, ospec],
        out_shape=[jax.ShapeDtypeStruct((B, R, LANES), BF16)] * 2,
        compiler_params=pltpu.CompilerParams(dimension_semantics=("parallel",),
                                             vmem_limit_bytes=VMEM_LIMIT),
        name="compress",
    )(xk, xv, pk, pv, wak, wbk, wav, wbv, w2k, w2v)


def _rank_rows(score):
    n = score.shape[0]
    j = lax.broadcasted_iota(jnp.int32, score.shape, 0)
    rank = jnp.zeros(score.shape, jnp.int32)
    for i in range(n):
        si = score[i:i + 1, :]
        beats = (si > score) | ((si == score) & (j > i))
        rank = rank + jnp.where(beats, 1, 0)
    return rank


def _dup_head(q4, r):
    pair = q4[:, (r // 2) * LANES:(r // 2 + 1) * LANES].astype(F32)
    lane = lax.broadcasted_iota(jnp.int32, pair.shape, 1)
    swapped = _swap_halves(pair)
    if r % 2 == 0:
        return jnp.where(lane < HEAD_DIM, pair, swapped)
    return jnp.where(lane < HEAD_DIM, swapped, pair)


def _pack_heads(o_list, g):
    lane = lax.broadcasted_iota(jnp.int32, o_list[0].shape, 1)
    in_g = (lane >= g * HEAD_DIM) & (lane < (g + 1) * HEAD_DIM)
    both = []
    for o in o_list:
        om = jnp.where(in_g, o, 0.0)
        both.append(om + _swap_halves(om))
    pairs = [jnp.where(lane < HEAD_DIM, both[2 * p], both[2 * p + 1]) for p in range(2)]
    return jnp.concatenate(pairs, axis=1)


def _cmp_sel_kernel(q_ref, kc_ref, vc_ref, bias_ref, gate_ref, ovl_ref, oc_ref, sel_ref):
    g = pl.program_id(1)
    i = pl.program_id(2)
    tq = q_ref.shape[0]
    n_pad = kc_ref.shape[0]
    q4 = q_ref[...]
    lane = lax.broadcasted_iota(jnp.int32, (n_pad, LANES), 1)
    in_g = (lane >= g * HEAD_DIM) & (lane < (g + 1) * HEAD_DIM)
    kc = jnp.where(in_g, kc_ref[...].astype(F32), 0.0).astype(BF16)
    vc = vc_ref[...]
    t = i * tq + lax.broadcasted_iota(jnp.int32, (tq, n_pad), 0)
    n = lax.broadcasted_iota(jnp.int32, (tq, n_pad), 1)
    valid = t >= n * CMP_STRIDE + (CMP_BLOCK - 1)
    gates = gate_ref[...]
    p_sum = jnp.zeros((tq, n_pad), F32)
    outs = []
    for r in range(NSA_GQA):
        qr = _dup_head(q4, r).astype(BF16)
        s = _dot_nt(qr, kc)
        s = jnp.where(valid, s + bias_ref[r], NEG_BIG)
        m = jnp.max(s, axis=-1, keepdims=True)
        p = jnp.exp(s - m)
        p = p / jnp.sum(p, axis=-1, keepdims=True)
        p = jnp.where(valid, p, 0.0)
        p_sum = p_sum + p
        outs.append(_dot(p.astype(BF16), vc) * gates[:, 3 * r:3 * r + 1])
    oc_ref[...] = _pack_heads(outs, g).astype(oc_ref.dtype)

    ovl = ovl_ref[...]
    hi, mid, lo = _split3(p_sum)
    imp = _dot_nt(ovl, hi) + _dot_nt(ovl, mid) + _dot_nt(ovl, lo)
    n_blk = imp.shape[0]
    j = lax.broadcasted_iota(jnp.int32, (n_blk, tq), 0)
    qb = jnp.right_shift(i * tq + lax.broadcasted_iota(jnp.int32, (n_blk, tq), 1),
                         int(math.log2(SEL_BLOCK)))
    forced = (j == 0) | (j == qb) | (j == qb - 1)
    causal = j <= qb
    score = jnp.where(causal, imp + jnp.where(forced, FORCE_SCORE, 0.0), -FORCE_SCORE)
    chosen = (_rank_rows(score) < N_SEL) & causal
    sel = jnp.where(chosen, 1.0, 0.0)
    sel = jnp.concatenate([sel, jnp.zeros((LANES - n_blk, tq), F32)], axis=0)
    sel_ref[...] = sel.T


def _cmp_sel(nq, kcmp, vcmp, bias_c, gates_g, ovl_t):
    B, S, _ = nq.shape
    tq = ATT_TILE
    n_pad = kcmp.shape[1]
    return pl.pallas_call(
        _cmp_sel_kernel,
        grid=(B, NSA_GROUPS, S // tq),
        in_specs=[pl.BlockSpec((None, tq, 2 * LANES), lambda b, g, i: (b, i, g)),
                  pl.BlockSpec((None, n_pad, LANES), lambda b, g, i: (b, 0, 0)),
                  pl.BlockSpec((None, n_pad, LANES), lambda b, g, i: (b, 0, 0)),
                  pl.BlockSpec((NSA_GQA, tq, n_pad), lambda b, g, i: (g, i, 0)),
                  pl.BlockSpec((None, None, tq, 3 * NSA_GQA), lambda b, g, i: (b, g, i, 0)),
                  pl.BlockSpec(ovl_t.shape, lambda b, g, i: (0, 0))],
        out_specs=[pl.BlockSpec((None, tq, 2 * LANES), lambda b, g, i: (b, i, g)),
                   pl.BlockSpec((None, None, tq, LANES), lambda b, g, i: (b, g, i, 0))],
        out_shape=[jax.ShapeDtypeStruct((B, S, NSA_HEADS * HEAD_DIM), BF16),
                   jax.ShapeDtypeStruct((B, NSA_GROUPS, S, LANES), F32)],
        compiler_params=pltpu.CompilerParams(
            dimension_semantics=("parallel", "parallel", "arbitrary"),
            vmem_limit_bytes=VMEM_LIMIT),
        name="cmp_sel",
    )(nq, kcmp, vcmp, bias_c, gates_g, ovl_t)


def _nsa_kernel(q_ref, ks_ref, vs_ref, kw_ref, vw_ref, sel_ref, exp_ref, w4_ref, gate_ref,
                oc_ref, o_ref, madd_ref):
    g = pl.program_id(1)
    i = pl.program_id(2)
    tq = q_ref.shape[0]
    n_kt, _, tk = madd_ref.shape
    H = NSA_GQA
    q4 = q_ref[...]
    lane = lax.broadcasted_iota(jnp.int32, (tq, LANES), 1)
    in_g = (lane >= g * HEAD_DIM) & (lane < (g + 1) * HEAD_DIM)
    qs = jnp.concatenate(
        [jnp.where(in_g, _dup_head(q4, r), 0.0).astype(BF16) for r in range(H)], axis=0)

    sel = sel_ref[...].astype(BF16)
    for jj in range(n_kt):
        hit = _dot(sel, exp_ref[:, jj * tk:(jj + 1) * tk])
        madd_ref[jj] = (hit - 1.0) * (-NEG_BIG)

    def sel_step(jj, carry, diagonal):
        k0 = pl.multiple_of(jj * tk, tk)
        dd = jnp.minimum(i - 2 * jj, 3)
        s = (_dot_nt(qs, ks_ref[pl.ds(k0, tk), :]).reshape(H, tq, tk)
             + w4_ref[:, pl.ds(dd, 1)].reshape(H, tq, tk))
        madd = madd_ref[jj]
        if diagonal:
            cmr = (lax.broadcasted_iota(jnp.int32, (tq, tk), 1)
                   - lax.broadcasted_iota(jnp.int32, (tq, tk), 0))
            madd = jnp.where(cmr <= i * tq - jj * tk, madd, NEG_BIG)
        return _flash_update(carry, s + madd[None], vs_ref[pl.ds(k0, tk), :])

    init = (jnp.full((H, tq, 1), NEG_BIG, F32), jnp.zeros((H, tq, 1), F32),
            jnp.zeros((H * tq, LANES), F32))
    n_full = (i * tq) // tk
    carry = lax.fori_loop(0, n_full, lambda jj, c: sel_step(jj, c, False), init)
    _, l_s, acc_s = sel_step(n_full, carry, True)

    tiles = (jnp.maximum(i - 2, 0), jnp.maximum(i - 1, 0), i)
    starts = [pl.multiple_of(j * tq, tq) for j in tiles]
    kt = jnp.concatenate([kw_ref[pl.ds(st, tq), :] for st in starts], axis=0)
    vt = jnp.concatenate([vw_ref[pl.ds(st, tq), :] for st in starts], axis=0)
    bias_w = jnp.concatenate([w4_ref[:, 2], w4_ref[:, 0, :, 0:tq]], axis=-1)
    col = lax.broadcasted_iota(jnp.int32, (tq, 3 * tq), 1)
    cmr = col - lax.broadcasted_iota(jnp.int32, (tq, 3 * tq), 0)
    never = 4 * tq
    valid = (((col < tq) & (cmr > jnp.where(i >= 2, 0, never)))
             | ((col >= jnp.where(i >= 1, tq, never)) & (col < 2 * tq))
             | ((col >= 2 * tq) & (cmr <= 2 * tq)))
    s = _dot_nt(qs, kt).reshape(H, tq, 3 * tq) + bias_w
    s = jnp.where(valid[None], s, NEG_BIG)
    p = jnp.exp(s - jnp.max(s, axis=-1, keepdims=True))
    l_w = jnp.sum(p, axis=-1, keepdims=True)
    acc_w = _dot(p.reshape(H * tq, 3 * tq).astype(BF16), vt)

    gates = gate_ref[...]
    o_s = acc_s / l_s.reshape(H * tq, 1)
    o_w = acc_w / l_w.reshape(H * tq, 1)
    outs = []
    for r in range(H):
        sl = slice(r * tq, (r + 1) * tq)
        outs.append(o_s[sl] * gates[:, 3 * r + 1:3 * r + 2] + o_w[sl] * gates[:, 3 * r + 2:3 * r + 3])
    o_ref[...] = (_pack_heads(outs, g) + oc_ref[...].astype(F32)).astype(o_ref.dtype)


def _nsa(nq, ks, vs, kw, vw, sel, expand, w4, gates_g, oc):
    B, S, _ = nq.shape
    tq = ATT_TILE
    tk = w4.shape[-1]
    kv = lambda: pl.BlockSpec((None, S, LANES), lambda b, g, i: (b, 0, 0))
    return pl.pallas_call(
        _nsa_kernel,
        grid=(B, NSA_GROUPS, S // tq),
        in_specs=[pl.BlockSpec((None, tq, 2 * LANES), lambda b, g, i: (b, i, g)),
                  kv(), kv(), kv(), kv(),
                  pl.BlockSpec((None, None, tq, LANES), lambda b, g, i: (b, g, i, 0)),
                  pl.BlockSpec(expand.shape, lambda b, g, i: (0, 0)),
                  pl.BlockSpec((NSA_GQA, 4, tq, tk), lambda b, g, i: (g, 0, 0, 0)),
                  pl.BlockSpec((None, None, tq, 3 * NSA_GQA), lambda b, g, i: (b, g, i, 0)),
                  pl.BlockSpec((None, tq, 2 * LANES), lambda b, g, i: (b, i, g))],
        out_specs=pl.BlockSpec((None, tq, 2 * LANES), lambda b, g, i: (b, i, g)),
        out_shape=jax.ShapeDtypeStruct((B, S, NSA_HEADS * HEAD_DIM), BF16),
        scratch_shapes=[pltpu.VMEM((S // tk, tq, tk), F32)],
        compiler_params=pltpu.CompilerParams(
            dimension_semantics=("parallel", "parallel", "arbitrary"),
            vmem_limit_bytes=VMEM_LIMIT),
        name="nsa",
    )(nq, ks, vs, kw, vw, sel, expand, w4, gates_g, oc)


def _layer_norm(y, g, b):
    mu = jnp.mean(y, axis=-1, keepdims=True)
    yc = y - mu
    var = jnp.mean(yc * yc, axis=-1, keepdims=True)
    return yc * lax.rsqrt(var + LN_EPS) * g + b


def _router_gates_t(h2, wr_t, eb_col):
    tm = h2.shape[0]
    h_parts = _split3(h2)
    w_parts = _split3(wr_t)
    logit = jnp.zeros((N_EXPERTS, tm), F32)
    for a, wp in enumerate(w_parts):
        for b, hp in enumerate(h_parts):
            if a + b <= 2:
                logit = logit + _dot_nt(wp, hp)
    scores = _sigmoid(logit)
    biased = scores + eb_col
    e_in = lax.broadcasted_iota(jnp.int32, (GROUP_SIZE, tm), 0).astype(F32)
    gs_rows = []
    for gi in range(N_EXPERT_GROUPS):
        grp = biased[gi * GROUP_SIZE:(gi + 1) * GROUP_SIZE, :]
        m1 = jnp.max(grp, axis=0, keepdims=True)
        first = jnp.min(jnp.where(grp == m1, e_in, float(GROUP_SIZE)), axis=0, keepdims=True)
        m2 = jnp.max(jnp.where(e_in == first, -jnp.inf, grp), axis=0, keepdims=True)
        gs_rows.append(m1 + m2)
    gscore = jnp.concatenate(gs_rows, axis=0)
    g_keep = _rank_rows(gscore) < TOPK_GROUPS
    keep = jnp.concatenate(
        [jnp.broadcast_to(g_keep[gi:gi + 1, :], (GROUP_SIZE, tm)) for gi in range(N_EXPERT_GROUPS)],
        axis=0)
    masked = jnp.where(keep, biased, -jnp.inf)
    chosen = _rank_rows(masked) < TOP_K
    w = jnp.where(chosen, scores, 0.0)
    return w / jnp.sum(w, axis=0, keepdims=True) * ROUTED_SCALE


def _out_proj_kernel(alpha, of_ref, on_ref, x_ref, mod_ref, w_ref, lg_ref, lb_ref, wr_ref,
                     eb_ref, x1_ref, h2_ref, gate_ref, gate_t_ref):
    half = of_ref.shape[1]
    mod = mod_ref[...]
    mixed = _dot(of_ref[...], w_ref[0:half, :]) + _dot(on_ref[...], w_ref[half:2 * half, :])
    y = alpha * x_ref[...] + mod[2:3, :] * mixed
    x1 = _layer_norm(y, lg_ref[...], lb_ref[...])
    x1_ref[...] = x1
    h2 = x1 * (1.0 + mod[4:5, :]) + mod[3:4, :]
    h2_ref[...] = h2.astype(h2_ref.dtype)
    gates_t = _router_gates_t(h2, wr_ref[...], eb_ref[...])
    gate_t_ref[...] = gates_t
    tm = h2.shape[0]
    gates_t = jnp.concatenate([gates_t, jnp.zeros((LANES - N_EXPERTS, tm), F32)], axis=0)
    gate_ref[...] = gates_t.T


def _out_proj(alpha, o_fox, o_nsa, x, mod, w_out, ln_g, ln_b, wr_t, eb_col):
    B, S, D = x.shape
    tm = ROW_TILE
    half = o_fox.shape[-1]
    row = lambda a: pl.BlockSpec(a.shape, lambda b, s: (0, 0))
    return pl.pallas_call(
        functools.partial(_out_proj_kernel, alpha),
        grid=(B, S // tm),
        in_specs=[pl.BlockSpec((None, tm, half), lambda b, s: (b, s, 0)),
                  pl.BlockSpec((None, tm, half), lambda b, s: (b, s, 0)),
                  pl.BlockSpec((None, tm, D), lambda b, s: (b, s, 0)),
                  pl.BlockSpec((None, 6, D), lambda b, s: (b, 0, 0)),
                  row(w_out), row(ln_g), row(ln_b), row(wr_t), row(eb_col)],
        out_specs=[pl.BlockSpec((None, tm, D), lambda b, s: (b, s, 0)),
                   pl.BlockSpec((None, tm, D), lambda b, s: (b, s, 0)),
                   pl.BlockSpec((None, tm, LANES), lambda b, s: (b, s, 0)),
                   pl.BlockSpec((N_EXPERTS, tm), lambda b, s: (0, b * (S // tm) + s))],
        out_shape=[jax.ShapeDtypeStruct((B, S, D), F32),
                   jax.ShapeDtypeStruct((B, S, D), BF16),
                   jax.ShapeDtypeStruct((B, S, LANES), F32),
                   jax.ShapeDtypeStruct((N_EXPERTS, B * S), F32)],
        compiler_params=pltpu.CompilerParams(dimension_semantics=("parallel", "parallel"),
                                             vmem_limit_bytes=VMEM_LIMIT),
        name="out_proj",
    )(o_fox, o_nsa, x, mod, w_out, ln_g, ln_b, wr_t, eb_col)


SORT_TILE = 256
ROW_ALIGN = 16
EXP_TILE = 512
P_CHUNK = 256


def _strict_upper(n):
    return jnp.where(lax.broadcasted_iota(jnp.int32, (n, n), 0)
                     < lax.broadcasted_iota(jnp.int32, (n, n), 1), 1.0, 0.0).astype(BF16)


def _strict_lower(n):
    return jnp.where(lax.broadcasted_iota(jnp.int32, (n, n), 1)
                     < lax.broadcasted_iota(jnp.int32, (n, n), 0), 1.0, 0.0).astype(BF16)


def _local_rows_bound(ts):
    rows = TOP_K * ts + N_EXPERTS * (ROW_ALIGN - 1)
    return -(-rows // P_CHUNK) * P_CHUNK


def _piece_cols(ts):
    return -(-(_local_rows_bound(ts) // ROW_ALIGN) // LANES) * LANES


def _sorted_tiles_bound(T):
    rows = TOP_K * T + (T // SORT_TILE) * N_EXPERTS * (ROW_ALIGN - 1)
    return -(-rows // EXP_TILE) + N_EXPERTS


def _moe_meta_kernel(gt_ref, ptab_ref, loc_et_ref, np_et_ref, loc_te_ref, np_te_ref, tot_ref,
                     erow_ref, texp_ref, nused_ref):
    E, T = gt_ref.shape
    mask = jnp.where(gt_ref[...] > 0.0, 1.0, 0.0).astype(BF16)
    t_id = lax.shift_right_logical(lax.broadcasted_iota(jnp.int32, (T, LANES), 0),
                                   int(math.log2(SORT_TILE)))
    tind = jnp.where(t_id == lax.broadcasted_iota(jnp.int32, (T, LANES), 1), 1.0, 0.0)
    cnt = _dot(mask, tind.astype(BF16))
    n16 = jnp.floor((cnt + (ROW_ALIGN - 1.0)) * (1.0 / ROW_ALIGN))
    n16b = n16.astype(BF16)
    q = EXP_TILE // ROW_ALIGN
    len16 = jnp.sum(n16, axis=1, keepdims=True)
    pad16 = jnp.floor((len16 + (q - 1.0)) * (1.0 / q)) * q
    sl = _strict_lower(E)
    hi, mid, lo = _split3(jnp.broadcast_to(pad16, (E, LANES)))
    start16 = _dot(sl, hi) + _dot(sl, mid) + _dot(sl, lo)
    gdst16 = start16 + _dot(n16b, _strict_upper(LANES))
    loc16 = _dot(sl, n16b)

    def t(a):
        return jnp.concatenate([a, jnp.zeros((LANES - E, LANES), F32)], axis=0).T

    scale = float(ROW_ALIGN)
    loc_et_ref[...] = loc16 * scale
    np_et_ref[...] = n16 * scale
    loc_te_ref[...] = t(loc16) * scale
    np_te_ref[...] = t(n16) * scale
    tot_ref[...] = (jnp.sum(n16, axis=0, keepdims=True) * scale).astype(jnp.int32)

    n_t, n_blk = ptab_ref.shape
    blk = lax.broadcasted_iota(jnp.int32, (E, n_blk), 1).astype(F32)
    for tile in range(n_t):
        lo_c = loc16[:, tile:tile + 1]
        inside = (lo_c <= blk) & (blk < lo_c + n16[:, tile:tile + 1])
        dst = jnp.sum(jnp.where(inside, gdst16[:, tile:tile + 1] + (blk - lo_c), 0.0),
                      axis=0, keepdims=True)
        ptab_ref[tile:tile + 1, :] = (dst * scale).astype(jnp.int32)
    ends = jnp.concatenate([t(start16 + len16)[0:1, :], t(start16 + pad16)[0:1, :],
                            jnp.zeros((erow_ref.shape[0] - 2, LANES), F32)], axis=0)
    erow_ref[...] = (ends * scale).astype(jnp.int32)
    n_tab = texp_ref.shape[1]
    tile_row16 = (lax.broadcasted_iota(jnp.int32, (E, n_tab), 1) * q).astype(F32)
    owner = jnp.sum(jnp.where(start16[:, 0:1] <= tile_row16, 1.0, 0.0), axis=0, keepdims=True)
    texp_ref[...] = (owner - 1.0).astype(jnp.int32)
    n_used = jnp.sum(pad16, axis=0, keepdims=True) * (1.0 / q)
    nused_ref[...] = jnp.broadcast_to(n_used, (1, LANES)).astype(jnp.int32)


def _moe_meta(gates_t, n_tab):
    E, T = gates_t.shape
    i32 = jnp.int32
    return pl.pallas_call(
        _moe_meta_kernel,
        out_shape=[jax.ShapeDtypeStruct((T // SORT_TILE, _piece_cols(SORT_TILE)), i32),
                   jax.ShapeDtypeStruct((E, LANES), F32),
                   jax.ShapeDtypeStruct((E, LANES), F32),
                   jax.ShapeDtypeStruct((LANES, LANES), F32),
                   jax.ShapeDtypeStruct((LANES, LANES), F32),
                   jax.ShapeDtypeStruct((1, LANES), i32),
                   jax.ShapeDtypeStruct((8, LANES), i32),
                   jax.ShapeDtypeStruct((1, n_tab), i32),
                   jax.ShapeDtypeStruct((1, LANES), i32)],
        compiler_params=pltpu.CompilerParams(vmem_limit_bytes=VMEM_LIMIT),
        name="moe_meta",
    )(gates_t)


def _start_pieces(tile, ptab_s, tot_s, n_cols, make_copy):
    n_pieces = lax.shift_right_logical(tot_s[tile], int(math.log2(ROW_ALIGN)))

    def body(b, carry):
        make_copy(pl.multiple_of(b * ROW_ALIGN, ROW_ALIGN),
                  pl.multiple_of(ptab_s[tile * n_cols + b], ROW_ALIGN)).start()
        return carry

    lax.fori_loop(0, n_pieces, body, 0)


def _moe_sort_kernel(ptab_s, tot_s, lend_s, rend_s, gt_ref, h_ref, locrow_ref, nprow_ref,
                     xs_hbm, buf, zbuf, sem, zsem):
    tau = pl.program_id(0)
    n_t = pl.num_programs(0)
    slot = lax.rem(tau, 2)
    E, ts = gt_ref.shape
    D = h_ref.shape[1]
    n_cols = _piece_cols(ts)

    def copies(tile, sl, wait):
        if wait:
            rows = pl.multiple_of(tot_s[tile], ROW_ALIGN)

            @pl.when(rows > 0)
            def _():
                pltpu.make_async_copy(buf.at[sl, pl.ds(0, rows)], xs_hbm.at[pl.ds(0, rows)],
                                      sem.at[sl]).wait()
        else:
            def make_copy(loc, dst):
                return pltpu.make_async_copy(buf.at[sl, pl.ds(loc, ROW_ALIGN)],
                                             xs_hbm.at[pl.ds(dst, ROW_ALIGN)], sem.at[sl])
            _start_pieces(tile, ptab_s, tot_s, n_cols, make_copy)

    z_rows = zbuf.shape[0]
    used_rows = rend_s[E - 1]
    n_spare = (xs_hbm.shape[0] - used_rows) // z_rows

    def spare_fill(wait):
        def body(c, carry):
            dst = pl.multiple_of(used_rows + c * z_rows, z_rows)
            cp = pltpu.make_async_copy(zbuf, xs_hbm.at[pl.ds(dst, z_rows)], zsem.at[1])
            if wait:
                cp.wait()
            else:
                cp.start()
            return carry

        lax.fori_loop(0, n_spare, body, 0)

    @pl.when(tau == 0)
    def _():
        zbuf[...] = jnp.zeros_like(zbuf)
        spare_fill(False)

    @pl.when(tau >= 2)
    def _():
        copies(tau - 2, slot, True)

    g = gt_ref[...]
    mask = g > 0.0
    maskb = jnp.where(mask, 1.0, 0.0).astype(BF16)
    pad = jnp.zeros((LANES - E, ts), F32)
    pos = jnp.where(mask, _dot(maskb, _strict_upper(ts)), -1.0)
    pos = jnp.concatenate([pos, pad], axis=0).astype(BF16)
    g_parts = _split3(jnp.concatenate([g, pad], axis=0))
    lo_row = locrow_ref[...]
    hi_row = lo_row + nprow_ref[...]
    h = h_ref[...]
    n_chunks = lax.shift_right_logical(tot_s[tau] + (P_CHUNK - 1), int(math.log2(P_CHUNK)))
    lane = lax.broadcasted_iota(jnp.int32, (P_CHUNK, LANES), 1)

    def chunk(c):
        r0 = pl.multiple_of(c * P_CHUNK, P_CHUNK)
        r = (r0 + lax.broadcasted_iota(jnp.int32, (P_CHUNK, LANES), 0)).astype(F32)
        inside = (lo_row <= r) & (r < hi_row)
        group = jnp.where(inside, 1.0, 0.0).astype(BF16)
        want = r[:, 0:1] - jnp.sum(jnp.where(inside, lo_row, 0.0), axis=1, keepdims=True)
        hit = _dot(group, pos) == want
        buf[slot, pl.ds(r0, P_CHUNK), 0:D] = _dot(jnp.where(hit, 1.0, 0.0).astype(BF16),
                                                  h).astype(buf.dtype)
        extra = jnp.zeros((P_CHUNK, LANES), F32)
        for k, part in enumerate(g_parts):
            term = jnp.sum(jnp.where(hit, _dot(group, part), 0.0), axis=1, keepdims=True)
            extra = jnp.where(lane == k, term, extra)
        buf[slot, pl.ds(r0, P_CHUNK), D:D + LANES] = extra.astype(buf.dtype)

    def pair_body(c, carry):
        chunk(2 * c)
        chunk(2 * c + 1)
        return carry

    lax.fori_loop(0, lax.shift_right_logical(n_chunks + 1, 1), pair_body, 0)
    copies(tau, slot, False)

    @pl.when(tau == n_t - 1)
    def _():
        @pl.when(n_t >= 2)
        def _():
            copies(tau - 1, 1 - slot, True)
        copies(tau, slot, True)
        spare_fill(True)

        sizes =[zbuf.shape[0] >> s for s in range(int(math.log2(zbuf.shape[0] // ROW_ALIGN)) + 1)]

        def fill(wait):
            def e_body(e, carry):
                start = lend_s[e]
                n = rend_s[e] - start
                off = start
                for size in sizes:
                    bit = jnp.bitwise_and(n, size)

                    @pl.when(bit != 0)
                    def _(off=off, size=size):
                        cp = pltpu.make_async_copy(
                            zbuf.at[pl.ds(0, size)],
                            xs_hbm.at[pl.ds(pl.multiple_of(off, ROW_ALIGN), size)], zsem.at[0])
                        if wait:
                            cp.wait()
                        else:
                            cp.start()

                    off = off + bit
                return carry

            lax.fori_loop(0, E, e_body, 0)

        fill(False)
        fill(True)


def _moe_sort(ptab, tot, lend, rend, gates_t, h2, loc_te, np_te, n_rows):
    E, T = gates_t.shape
    D = h2.shape[1]
    ts = SORT_TILE
    grid_spec = pltpu.PrefetchScalarGridSpec(
        num_scalar_prefetch=4,
        grid=(T // ts,),
        in_specs=[pl.BlockSpec((E, ts), lambda t, *_: (0, t)),
                  pl.BlockSpec((ts, D), lambda t, *_: (t, 0)),
                  pl.BlockSpec((None, 1, LANES), lambda t, *_: (t, 0, 0)),
                  pl.BlockSpec((None, 1, LANES), lambda t, *_: (t, 0, 0))],
        out_specs=pl.BlockSpec(memory_space=pl.ANY),
        scratch_shapes=[pltpu.VMEM((2, _local_rows_bound(ts), D + LANES), BF16),
                        pltpu.VMEM((EXP_TILE // 2, D + LANES), BF16),
                        pltpu.SemaphoreType.DMA((2,)),
                        pltpu.SemaphoreType.DMA((2,))])
    return pl.pallas_call(
        _moe_sort_kernel,
        grid_spec=grid_spec,
        out_shape=jax.ShapeDtypeStruct((n_rows, D + LANES), BF16),
        compiler_params=pltpu.CompilerParams(dimension_semantics=("arbitrary",),
                                             vmem_limit_bytes=VMEM_LIMIT),
        name="moe_sort",
    )(ptab, tot, lend, rend, gates_t, h2, loc_te, np_te)


def _moe_expert_kernel(texp_s, nused_s, x_ref, wg_ref, wu_ref, wd_ref, y_ref, wgu_s, wd_s):
    i = pl.program_id(0)
    f = wd_ref.shape[0]

    @pl.when(i < nused_s[0])
    def _():
        @pl.when((i == 0) | (texp_s[i] != texp_s[jnp.maximum(i - 1, 0)]))
        def _():
            wgu_s[:, 0:f] = wg_ref[...].astype(BF16)
            wgu_s[:, f:2 * f] = wu_ref[...].astype(BF16)
            wd_s[...] = wd_ref[...].astype(BF16)

        d = wd_ref.shape[1]
        gate = jnp.sum(x_ref[:, d:].astype(F32), axis=1, keepdims=True)
        a = _dot(x_ref[:, 0:d], wgu_s[...])
        act = _silu(a[:, :f]) * a[:, f:] * gate
        y_ref[...] = _dot(act.astype(BF16), wd_s[...]).astype(y_ref.dtype)

    @pl.when(i >= nused_s[0])
    def _():
        y_ref[...] = jnp.zeros_like(y_ref)


def _moe_expert(texp, nused, xs, w_gate, w_up, w_down, n_tiles):
    n_rows, xw = xs.shape
    D, f = w_gate.shape[-2:]
    tm = EXP_TILE

    def tile(i, texp, nused):
        return jnp.maximum(jnp.minimum(i, nused[0] - 1), 0)

    grid_spec = pltpu.PrefetchScalarGridSpec(
        num_scalar_prefetch=2,
        grid=(n_tiles,),
        in_specs=[pl.BlockSpec((tm, xw), lambda i, te, nu: (tile(i, te, nu), 0)),
                  pl.BlockSpec((None, D, f), lambda i, te, nu: (te[tile(i, te, nu)], 0, 0)),
                  pl.BlockSpec((None, D, f), lambda i, te, nu: (te[tile(i, te, nu)], 0, 0)),
                  pl.BlockSpec((None, f, D), lambda i, te, nu: (te[tile(i, te, nu)], 0, 0))],
        out_specs=pl.BlockSpec((tm, D), lambda i, te, nu: (i, 0)),
        scratch_shapes=[pltpu.VMEM((D, 2 * f), BF16), pltpu.VMEM((f, D), BF16)])
    return pl.pallas_call(
        _moe_expert_kernel,
        grid_spec=grid_spec,
        out_shape=jax.ShapeDtypeStruct((n_rows, D), BF16),
        compiler_params=pltpu.CompilerParams(dimension_semantics=("arbitrary",),
                                             vmem_limit_bytes=VMEM_LIMIT),
        name="moe_expert",
    )(texp, nused, xs, w_gate, w_up, w_down)


def _moe_combine_kernel(alpha, ptab_s, tot_s, g_ref, loc_ref, np_ref, h_ref, x1_ref,
                        mod_ref, sgu_ref, sd_ref, lg_ref, lb_ref, y_hbm, o_ref, ybuf, acc_ref,
                        sem):
    tau = pl.program_id(0)
    n_t = pl.num_programs(0)
    slot = lax.rem(tau, 2)
    ts, n_lane = g_ref.shape
    E = loc_ref.shape[0]
    n_cols = _piece_cols(ts)

    def copies(tile, sl, wait):
        if wait:
            rows = pl.multiple_of(tot_s[tile], ROW_ALIGN)

            @pl.when(rows > 0)
            def _():
                pltpu.make_async_copy(y_hbm.at[pl.ds(0, rows)], ybuf.at[sl, pl.ds(0, rows)],
                                      sem.at[sl]).wait()
        else:
            def make_copy(loc, dst):
                return pltpu.make_async_copy(y_hbm.at[pl.ds(dst, ROW_ALIGN)],
                                             ybuf.at[sl, pl.ds(loc, ROW_ALIGN)], sem.at[sl])
            _start_pieces(tile, ptab_s, tot_s, n_cols, make_copy)

    @pl.when(tau == 0)
    def _():
        ybuf[...] = jnp.zeros_like(ybuf)
        copies(0, 0, False)

    @pl.when(tau + 1 < n_t)
    def _():
        copies(tau + 1, 1 - slot, False)

    g = g_ref[...]
    mask = g > 0.0
    maskb = jnp.where(mask, 1.0, 0.0).astype(BF16)
    pos = jnp.where(mask, _dot(_strict_lower(ts), maskb), -1.0).astype(BF16)
    lane = lax.broadcasted_iota(jnp.int32, loc_ref.shape, 1)
    lo_col = jnp.sum(jnp.where(lane == tau, loc_ref[...], 0.0), axis=1, keepdims=True)
    hi_col = lo_col + jnp.sum(jnp.where(lane == tau, np_ref[...], 0.0), axis=1, keepdims=True)

    f = sd_ref.shape[0]
    a = _dot(h_ref[...], sgu_ref[...])
    acc_ref[...] = _dot((_silu(a[:, :f]) * a[:, f:]).astype(BF16), sd_ref[...])

    copies(tau, slot, True)
    n_chunks = lax.shift_right_logical(tot_s[tau] + (P_CHUNK - 1), int(math.log2(P_CHUNK)))

    def chunk(c):
        r0 = pl.multiple_of(c * P_CHUNK, P_CHUNK)
        r = (r0 + lax.broadcasted_iota(jnp.int32, (E, P_CHUNK), 1)).astype(F32)
        inside = (lo_col <= r) & (r < hi_col)
        group = jnp.concatenate([jnp.where(inside, 1.0, 0.0),
                                 jnp.zeros((n_lane - E, P_CHUNK), F32)], axis=0).astype(BF16)
        want = r[0:1, :] - jnp.sum(jnp.where(inside, lo_col, 0.0), axis=0, keepdims=True)
        hit = _dot(pos, group) == want
        return _dot(jnp.where(hit, 1.0, 0.0).astype(BF16), ybuf[slot, pl.ds(r0, P_CHUNK), :])

    def pair_body(c, carry):
        acc_ref[...] += chunk(2 * c) + chunk(2 * c + 1)
        return carry

    lax.fori_loop(0, lax.shift_right_logical(n_chunks + 1, 1), pair_body, 0)
    y = alpha * x1_ref[...] + mod_ref[5:6, :] * acc_ref[...]
    o_ref[...] = _layer_norm(y, lg_ref[...], lb_ref[...])


def _moe_combine(alpha, ptab, tot, gates, loc_et, np_et, h2, x1, mod, sgu, sd, ln_g, ln_b, ys,
                 S):
    T, D = h2.shape
    ts = SORT_TILE
    per_b = S // ts
    row = lambda a: pl.BlockSpec(a.shape, lambda t, *_: (0, 0))
    grid_spec = pltpu.PrefetchScalarGridSpec(
        num_scalar_prefetch=2,
        grid=(T // ts,),
        in_specs=[pl.BlockSpec((ts, LANES), lambda t, *_: (t, 0)),
                  row(loc_et), row(np_et),
                  pl.BlockSpec((ts, D), lambda t, *_: (t, 0)),
                  pl.BlockSpec((ts, D), lambda t, *_: (t, 0)),
                  pl.BlockSpec((None, 6, D), lambda t, *_: (t // per_b, 0, 0)),
                  row(sgu), row(sd), row(ln_g), row(ln_b),
                  pl.BlockSpec(memory_space=pl.ANY)],
        out_specs=pl.BlockSpec((ts, D), lambda t, *_: (t, 0)),
        scratch_shapes=[pltpu.VMEM((2, _local_rows_bound(ts), D), BF16),
                        pltpu.VMEM((ts, D), F32),
                        pltpu.SemaphoreType.DMA((2,))])
    return pl.pallas_call(
        functools.partial(_moe_combine_kernel, alpha),
        grid_spec=grid_spec,
        out_shape=jax.ShapeDtypeStruct((T, D), F32),
        compiler_params=pltpu.CompilerParams(dimension_semantics=("arbitrary",),
                                             vmem_limit_bytes=VMEM_LIMIT),
        name="moe_combine",
    )(ptab, tot, gates, loc_et, np_et, h2, x1, mod, sgu, sd, ln_g, ln_b, ys)


def _moe(alpha, h2, x1, gates, gates_t, mod, w_gate, w_up, w_down, sgu, sd, ln_g, ln_b):
    B, S, D = x1.shape
    T = B * S
    n_t = T // SORT_TILE
    n_tiles = _sorted_tiles_bound(T)
    n_tab = -(-n_tiles // LANES) * LANES
    ptab, loc_et, np_et, loc_te, np_te, tot, erow, texp, nused = _moe_meta(gates_t, n_tab)
    ptab = ptab.reshape(-1)
    tot = tot[0, :n_t]
    h2 = h2.reshape(T, D)
    xs = _moe_sort(ptab, tot, erow[0, :N_EXPERTS], erow[1, :N_EXPERTS], gates_t, h2,
                   loc_te[:n_t].reshape(n_t, 1, LANES), np_te[:n_t].reshape(n_t, 1, LANES),
                   n_tiles * EXP_TILE)
    ys = _moe_expert(texp[0], nused[0, :1], xs, w_gate, w_up, w_down, n_tiles)
    out = _moe_combine(alpha, ptab, tot, gates.reshape(T, LANES), loc_et, np_et, h2,
                       x1.reshape(T, D), mod, sgu, sd, ln_g, ln_b, ys, S)
    return out.reshape(B, S, D)


def _rearrange_w_in(w):
    scale = HEAD_DIM ** -0.5
    fq, fk, fv = w[:, 0:512], w[:, 512:1024], w[:, 1024:1536]
    ff = w[:, 1536:1544]
    nq = w[:, 1544:2056]
    kv = w[:, 2056:2824]
    ng = w[:, 2824:2848]
    pad = jnp.zeros((w.shape[0], LANES - ff.shape[1] - ng.shape[1]), w.dtype)
    return jnp.concatenate([fq * scale, fk, fv, nq * scale, kv, ff, ng, pad], axis=1).astype(BF16)


def _compress_weights(pos, w1, w2):
    half = CMP_BLOCK // 2
    w1r = w1.reshape(2, half, HEAD_DIM, CMP_HIDDEN)
    zeros = jnp.zeros_like(w1r[0])
    def spread(part):
        g0 = jnp.stack([part, zeros], axis=1).reshape(half * 2 * HEAD_DIM, CMP_HIDDEN)
        g1 = jnp.stack([zeros, part], axis=1).reshape(half * 2 * HEAD_DIM, CMP_HIDDEN)
        return jnp.concatenate([g0, g1], axis=1).astype(BF16)
    wa, wb = spread(w1r[0]), spread(w1r[1])
    z2 = jnp.zeros_like(w2)
    w2bd = jnp.concatenate([jnp.concatenate([w2, z2], axis=1),
                            jnp.concatenate([z2, w2], axis=1)], axis=0).astype(BF16)
    posr = pos.reshape(2, half, 1, HEAD_DIM)
    posr = jnp.broadcast_to(posr, (2, half, NSA_GROUPS, HEAD_DIM)).reshape(2, half * 2 * HEAD_DIM)
    return posr, wa, wb, w2bd


@functools.lru_cache(maxsize=None)
def _static_tables(S):
    tq = ATT_TILE
    n_cmp = (S - CMP_BLOCK) // CMP_STRIDE + 1
    n_pad = S // CMP_STRIDE
    n_slc = S // SEL_BLOCK
    t = np.arange(S)[:, None]
    n = np.arange(n_pad)[None, :]
    bucket_c = _t5_bucket_np(t - (n * CMP_STRIDE + CMP_BLOCK - 1)).reshape(1, -1)
    d = (np.arange(4)[:, None, None] * tq + np.arange(tq)[None, :, None]
         - np.arange(2 * tq)[None, None, :])
    bucket_w = _t5_bucket_np(d).reshape(1, -1)
    cs = np.arange(n_pad)[None, :] * CMP_STRIDE
    sj = np.arange(n_slc)[:, None] * SEL_BLOCK
    ovl_t = ((cs < sj + SEL_BLOCK) & (cs + CMP_BLOCK > sj) & (np.arange(n_pad)[None, :] < n_cmp))
    expand = (np.arange(S)[None, :] // SEL_BLOCK) == np.arange(LANES)[:, None]
    return bucket_c, bucket_w, ovl_t.astype(np.float32), expand.astype(np.float32)


def kernel(x, c, w_ada, b_ada, w_in, b_f, cmp_pos_k, cmp_w1_k, cmp_w2_k, cmp_pos_v, cmp_w1_v,
           cmp_w2_v, rel_bias, w_out, ln1_g, ln1_b, w_router, e_bias, w_gate, w_up, w_down,
           ws_gate, ws_up, ws_down, ln2_g, ln2_b):
    B, S, D = x.shape
    depth = w_ada.shape[0]
    alpha = (2 * depth) ** 0.25
    tq = ATT_TILE
    bucket_c, bucket_w, ovl_t, expand = _static_tables(S)
    rel_bias_t = rel_bias.T
    bias_c = _bias_table(jnp.asarray(bucket_c), rel_bias_t).reshape(NSA_HEADS, S, S // CMP_STRIDE)
    w4 = _bias_table(jnp.asarray(bucket_w), rel_bias_t).reshape(NSA_HEADS, 4, tq, 2 * tq)
    ovl_t = jnp.asarray(ovl_t, BF16)
    expand = jnp.asarray(expand, BF16)

    for l in range(depth):
        mod = _ada(c, w_ada[l], b_ada[l]).reshape(B, 6, D)
        bf_row = jnp.zeros((1, LANES), F32).at[0, :FOX_HEADS].set(b_f[l])
        (fq, fk, fv, nq, kc, vc, ks, vs, kw, vw, misc, misc_t) = _in_proj(
            x, mod, _rearrange_w_in(w_in[l]), bf_row)

        o_fox = _fox(fq, fk, fv, misc_t[:, :FOX_HEADS, :])

        pk, wak, wbk, w2k = _compress_weights(cmp_pos_k[l], cmp_w1_k[l], cmp_w2_k[l])
        pv, wav, wbv, w2v = _compress_weights(cmp_pos_v[l], cmp_w1_v[l], cmp_w2_v[l])
        rows = S // CMP_STRIDE
        kcmp, vcmp = _compress(kc.reshape(B, rows, CMP_STRIDE * LANES),
                               vc.reshape(B, rows, CMP_STRIDE * LANES),
                               pk, pv, wak, wbk, wav, wbv, w2k, w2v)

        gates_g = misc[:, :, FOX_HEADS:FOX_HEADS + 3 * NSA_HEADS].reshape(
            B, S, NSA_GROUPS, 3 * NSA_GQA).transpose(0, 2, 1, 3)
        oc, sel = _cmp_sel(nq, kcmp, vcmp, bias_c, gates_g, ovl_t)
        o_nsa = _nsa(nq, ks, vs, kw, vw, sel, expand, w4, gates_g, oc)

        x1, h2, gates, gates_t = _out_proj(
            alpha, o_fox, o_nsa, x, mod, w_out[l].astype(BF16), ln1_g[l].reshape(1, D),
            ln1_b[l].reshape(1, D), w_router[l].T, e_bias[l].reshape(N_EXPERTS, 1))

        sgu = jnp.concatenate([ws_gate[l], ws_up[l]], axis=-1).astype(BF16)
        x = _moe(alpha, h2, x1, gates, gates_t, mod, w_gate[l], w_up[l], w_down[l], sgu,
                 ws_down[l].astype(BF16), ln2_g[l].reshape(1, D), ln2_b[l].reshape(1, D))
    return x
```

```python
import functools
import math

import jax
import jax.numpy as jnp
import numpy as np
from jax import lax
from jax.experimental import pallas as pl
from jax.experimental.pallas import tpu as pltpu

F32 = jnp.float32
BF16 = jnp.bfloat16

HEAD_DIM = 64
FOX_HEADS = 8
NSA_HEADS = 8
NSA_GQA = 4
NSA_GROUPS = NSA_HEADS // NSA_GQA
CMP_BLOCK = 32
CMP_STRIDE = 16
CMP_HIDDEN = 256
SEL_BLOCK = 64
N_SEL = 16
WINDOW = 512
N_BUCKETS = 32
MAX_DISTANCE = 128
N_EXPERTS = 64
N_EXPERT_GROUPS = 8
GROUP_SIZE = N_EXPERTS // N_EXPERT_GROUPS
TOPK_GROUPS = 4
TOP_K = 8
D_EXPERT = 256
ROUTED_SCALE = 2.5
LN_EPS = 1e-5
NEG_BIG = -1e30
FORCE_SCORE = 1e4

LANES = 128
ATT_TILE = 256
FOX_TILE = 512
ROW_TILE = 512
MOE_TILE = 1024
VMEM_LIMIT = 48 * 1024 * 1024

NT_DIMS = (((1,), (1,)), ((), ()))


def _dot(a, b):
    return jnp.dot(a, b, preferred_element_type=F32)


def _dot_nt(a, b):
    return lax.dot_general(a, b, NT_DIMS, preferred_element_type=F32)


def _split3(x):
    hi = x.astype(BF16)
    r1 = x - hi.astype(F32)
    mid = r1.astype(BF16)
    lo = (r1 - mid.astype(F32)).astype(BF16)
    return hi, mid, lo


def _silu(x):
    return x / (1.0 + jnp.exp(-x))


def _sigmoid(x):
    return 1.0 / (1.0 + jnp.exp(-x))


def _swap_halves(x):
    return pltpu.roll(x, HEAD_DIM, 1)


def _t5_bucket_np(dist):
    n = np.maximum(dist, 0)
    max_exact = N_BUCKETS // 2
    nf = np.maximum(n, 1).astype(np.float32)
    large = max_exact + (np.log(nf / max_exact) / math.log(MAX_DISTANCE / max_exact)
                         * (N_BUCKETS - max_exact)).astype(np.int32)
    large = np.minimum(large, N_BUCKETS - 1)
    return np.where(n < max_exact, n, large).astype(np.int32)


def _ada_kernel(c_ref, w_ref, b_ref, o_ref):
    c = c_ref[...]
    o_ref[...] = jnp.dot(_silu(c), w_ref[...], preferred_element_type=F32,
                         precision=lax.Precision.HIGHEST) + b_ref[...]


def _ada(c, w_ada, b_ada):
    B, D = c.shape
    n_out = w_ada.shape[1]
    tn = 1024
    return pl.pallas_call(
        _ada_kernel,
        grid=(n_out // tn,),
        in_specs=[pl.BlockSpec((B, D), lambda j: (0, 0)),
                  pl.BlockSpec((D, tn), lambda j: (0, j)),
                  pl.BlockSpec((1, tn), lambda j: (0, j))],
        out_specs=pl.BlockSpec((B, tn), lambda j: (0, j)),
        out_shape=jax.ShapeDtypeStruct((B, n_out), F32),
        compiler_params=pltpu.CompilerParams(dimension_semantics=("arbitrary",),
                                             vmem_limit_bytes=VMEM_LIMIT),
        name="ada",
    )(c, w_ada, b_ada.reshape(1, n_out))


def _bias_table_kernel(bkt_ref, rbt_ref, o_ref):
    bkt = bkt_ref[...]
    k = lax.broadcasted_iota(jnp.int32, (N_BUCKETS, bkt.shape[1]), 0)
    onehot = jnp.where(k == bkt, 1.0, 0.0).astype(BF16)
    hi, mid, lo = _split3(rbt_ref[...])
    o_ref[...] = _dot(hi, onehot) + _dot(mid, onehot) + _dot(lo, onehot)


def _bias_table(bucket, rel_bias_t):
    n = bucket.shape[1]
    chunk = 8192
    n_heads = rel_bias_t.shape[0]
    return pl.pallas_call(
        _bias_table_kernel,
        grid=(n // chunk,),
        in_specs=[pl.BlockSpec((1, chunk), lambda j: (0, j)),
                  pl.BlockSpec(rel_bias_t.shape, lambda j: (0, 0))],
        out_specs=pl.BlockSpec((n_heads, chunk), lambda j: (0, j)),
        out_shape=jax.ShapeDtypeStruct((n_heads, n), F32),
        compiler_params=pltpu.CompilerParams(dimension_semantics=("parallel",),
                                             vmem_limit_bytes=VMEM_LIMIT),
        name="bias_table",
    )(bucket, rel_bias_t)


_C_FQ, _C_FK, _C_NQ = 0, 512, 1024
_C_K4 = 1536
_C_FV = 2048
_C_SV = 3072
_C_WV = 3328
_C_MISC = 3584
_IN_COLS = 3712
LOG2E = math.log2(math.e)


def _in_proj_kernel(x_ref, mod_ref, w_ref, bf_ref, ones_ref, fq_ref, fk_ref, nq_ref, kc_ref,
                    ks_ref, kw_ref, vc_ref, fv_ref, vs_ref, vw_ref, misc_ref, misct_ref,
                    carry_ref):
    s_idx = pl.program_id(1)
    tm = x_ref.shape[0]
    mod = mod_ref[...]
    h = (x_ref[...] * (1.0 + mod[1:2, :]) + mod[0:1, :]).astype(BF16)

    for ref, c0 in ((fq_ref, _C_FQ), (fk_ref, _C_FK), (nq_ref, _C_NQ)):
        ref[...] = _dot(h, w_ref[:, c0:c0 + 512]).astype(ref.dtype)
    for k, ref in enumerate((kc_ref, ks_ref, kw_ref, vc_ref)):
        c0 = _C_K4 + k * LANES
        ref[...] = _dot(h, w_ref[:, c0:c0 + LANES]).astype(ref.dtype)
    for ref, c0 in ((fv_ref, _C_FV), (vs_ref, _C_SV), (vw_ref, _C_WV)):
        c1 = c0 + ref.shape[-1]
        ref[...] = (_dot(h, w_ref[:, c0:c1]) + ones_ref[:, c0:c1]).astype(ref.dtype)

    z = _dot(h, w_ref[:, _C_MISC:_C_MISC + LANES]) + bf_ref[...]
    lane = lax.broadcasted_iota(jnp.int32, z.shape, 1)
    is_f = lane < FOX_HEADS
    log_f = jnp.minimum(z, 0.0) - jnp.log(1.0 + jnp.exp(-jnp.abs(z)))
    log_f = jnp.where(is_f, log_f, 0.0)

    row = lax.broadcasted_iota(jnp.int32, (tm, tm), 0)
    col = lax.broadcasted_iota(jnp.int32, (tm, tm), 1)
    tri = jnp.where(row >= col, 1.0, 0.0).astype(BF16)
    hi, mid, lo = _split3(log_f)
    local = _dot(tri, hi) + _dot(tri, mid) + _dot(tri, lo)

    @pl.when(s_idx == 0)
    def _():
        carry_ref[...] = jnp.zeros_like(carry_ref)

    cum = local + carry_ref[...]
    carry_ref[...] = cum[tm - 1:tm, :]
    misc = jnp.where(is_f, cum * LOG2E, _sigmoid(z))
    misc_ref[...] = misc
    misct_ref[...] = misc.T


def _in_proj(x, mod, w_r, bf_row, ones_row):
    B, S, D = x.shape
    tm = ROW_TILE
    widths = (512, 512, 512, LANES, LANES, LANES, LANES, _C_SV - _C_FV, _C_WV - _C_SV,
              _C_MISC - _C_WV)
    wide = lambda w: pl.BlockSpec((None, tm, w), lambda b, s: (b, s, 0))
    out_shape = ([jax.ShapeDtypeStruct((B, S, w), BF16) for w in widths]
                 + [jax.ShapeDtypeStruct((B, S, LANES), F32),
                    jax.ShapeDtypeStruct((B, LANES, S), F32)])
    out_specs = ([wide(w) for w in widths]
                 + [wide(LANES), pl.BlockSpec((None, LANES, tm), lambda b, s: (b, 0, s))])
    return pl.pallas_call(
        _in_proj_kernel,
        grid=(B, S // tm),
        in_specs=[pl.BlockSpec((None, tm, D), lambda b, s: (b, s, 0)),
                  pl.BlockSpec((None, 6, D), lambda b, s: (b, 0, 0)),
                  pl.BlockSpec((D, _IN_COLS), lambda b, s: (0, 0)),
                  pl.BlockSpec((1, LANES), lambda b, s: (0, 0)),
                  pl.BlockSpec((1, _IN_COLS), lambda b, s: (0, 0))],
        out_specs=out_specs,
        out_shape=out_shape,
        scratch_shapes=[pltpu.VMEM((1, LANES), F32)],
        compiler_params=pltpu.CompilerParams(dimension_semantics=("parallel", "arbitrary"),
                                             vmem_limit_bytes=VMEM_LIMIT),
        name="in_proj",
    )(x, mod, w_r, bf_row, ones_row)


def _softmax_weights(s, m):
    return jnp.exp2((s - m).astype(BF16))


def _flash_update(carry, s, vt):
    m, acc = carry
    m_new = jnp.maximum(m, jnp.max(s, axis=-1, keepdims=True))
    alpha = jnp.exp2(m - m_new)
    p = _softmax_weights(s, m_new)
    rows = acc.shape[0]
    acc = alpha.reshape(rows, 1) * acc + _dot(p.reshape(rows, s.shape[-1]), vt)
    return m_new, acc


def _normalize(acc):
    return acc / _swap_halves(acc)


def _fox_kernel(q_ref, k_ref, v_ref, ck_ref, o_ref):
    i = pl.program_id(2)
    tq = q_ref.shape[0]
    tk = ck_ref.shape[2]
    q2 = q_ref[...].astype(F32)
    lane = lax.broadcasted_iota(jnp.int32, (tq, LANES), 1)
    low = lane < HEAD_DIM
    halves = (low, jnp.logical_not(low))
    qh = [jnp.where(h, q2, 0.0).astype(BF16) for h in halves]
    col_minus_row = (lax.broadcasted_iota(jnp.int32, (tq, tk), 1)
                     - lax.broadcasted_iota(jnp.int32, (tq, tk), 0))

    def step(jj, carry, diagonal):
        k0 = pl.multiple_of(jj * tk, tk)
        kt = k_ref[pl.ds(k0, tk), :]
        new = []
        for hh in range(2):
            s = _dot_nt(qh[hh], kt) - ck_ref[hh, pl.ds(jj, 1), :]
            if diagonal:
                s = jnp.where(col_minus_row <= i * tq - jj * tk, s, NEG_BIG)
            vt = v_ref[pl.ds(k0, tk), hh * LANES:(hh + 1) * LANES]
            new.append(_flash_update(carry[hh], s, vt))
        return tuple(new)

    init = tuple((jnp.full((tq, 1), NEG_BIG, F32), jnp.zeros((tq, LANES), F32))
                 for _ in range(2))
    n_full = (i * tq) // tk
    carry = lax.fori_loop(0, n_full, lambda jj, c: step(jj, c, False), init)
    carry = step(n_full, carry, True)
    o_ref[...] = jnp.where(low, _normalize(carry[0][1]),
                           _swap_halves(_normalize(carry[1][1]))).astype(o_ref.dtype)


def _fox(fq, fk, fv, cum_row):
    B, S, W = fq.shape
    tq = tk = FOX_TILE
    n_pairs = W // LANES
    cum_row = cum_row.reshape(B, n_pairs, 2, S // tk, tk)
    return pl.pallas_call(
        _fox_kernel,
        grid=(B, n_pairs, S // tq),
        in_specs=[pl.BlockSpec((None, tq, LANES), lambda b, p, i: (b, i, p)),
                  pl.BlockSpec((None, S, LANES), lambda b, p, i: (b, 0, p)),
                  pl.BlockSpec((None, S, 2 * LANES), lambda b, p, i: (b, 0, p)),
                  pl.BlockSpec((None, None, 2, S // tk, tk), lambda b, p, i: (b, p, 0, 0, 0))],
        out_specs=pl.BlockSpec((None, tq, LANES), lambda b, p, i: (b, i, p)),
        out_shape=jax.ShapeDtypeStruct((B, S, W), BF16),
        compiler_params=pltpu.CompilerParams(
            dimension_semantics=("parallel", "parallel", "arbitrary"),
            vmem_limit_bytes=VMEM_LIMIT),
        name="fox",
    )(fq, fk, fv, cum_row)


def _compress_kernel(xk_ref, xv_ref, pk_ref, pv_ref, wak_ref, wbk_ref, wav_ref, wbv_ref,
                     w2k_ref, w2v_ref, ok_ref, ov_ref):
    n_rows = xk_ref.shape[0]
    for x_ref, p_ref, wa_ref, wb_ref, w2_ref, o_ref in (
            (xk_ref, pk_ref, wak_ref, wbk_ref, w2k_ref, ok_ref),
            (xv_ref, pv_ref, wav_ref, wbv_ref, w2v_ref, ov_ref)):
        x = x_ref[...].astype(F32)
        xa = (x + p_ref[0:1, :]).astype(BF16)
        xb = (x + p_ref[1:2, :]).astype(BF16)
        hb = _dot(xb, wb_ref[...])
        h1 = _dot(xa, wa_ref[...]) + pltpu.roll(hb, n_rows - 1, 0)
        o_ref[...] = _dot(_silu(h1).astype(BF16), w2_ref[...]).astype(o_ref.dtype)


def _compress(xk, xv, pk, pv, wak, wbk, wav, wbv, w2k, w2v):
    B, R, C = xk.shape
    xspec = pl.BlockSpec((None, R, C), lambda b: (b, 0, 0))
    full = lambda a: pl.BlockSpec(a.shape, lambda b: (0,) * a.ndim)
    ospec = pl.BlockSpec((None, R, LANES), lambda b: (b, 0, 0))
    return pl.pallas_call(
        _compress_kernel,
        grid=(B,),
        in_specs=[xspec, xspec] + [full(a) for a in (pk, pv, wak, wbk, wav, wbv, w2k, w2v)],
        out_specs=[ospec, ospec],
        out_shape=[jax.ShapeDtypeStruct((B, R, LANES), BF16)] * 2,
        compiler_params=pltpu.CompilerParams(dimension_semantics=("parallel",),
                                             vmem_limit_bytes=VMEM_LIMIT),
        name="compress",
    )(xk, xv, pk, pv, wak, wbk, wav, wbv, w2k, w2v)


def _rank_rows(score):
    n = score.shape[0]
    j = lax.broadcasted_iota(jnp.int32, score.shape, 0)
    rank = jnp.zeros(score.shape, jnp.int32)
    for i in range(n):
        si = score[i:i + 1, :]
        beats = (si > score) | ((si == score) & (j > i))
        rank = rank + jnp.where(beats, 1, 0)
    return rank


def _dup_head(q4, r):
    pair = q4[:, (r // 2) * LANES:(r // 2 + 1) * LANES].astype(F32)
    lane = lax.broadcasted_iota(jnp.int32, pair.shape, 1)
    swapped = _swap_halves(pair)
    if r % 2 == 0:
        return jnp.where(lane < HEAD_DIM, pair, swapped)
    return jnp.where(lane < HEAD_DIM, swapped, pair)


def _pack_heads(o_list, g):
    lane = lax.broadcasted_iota(jnp.int32, o_list[0].shape, 1)
    in_g = (lane >= g * HEAD_DIM) & (lane < (g + 1) * HEAD_DIM)
    both = []
    for o in o_list:
        om = jnp.where(in_g, o, 0.0)
        both.append(om + _swap_halves(om))
    pairs = [jnp.where(lane < HEAD_DIM, both[2 * p], both[2 * p + 1]) for p in range(2)]
    return jnp.concatenate(pairs, axis=1)


def _cmp_sel_kernel(q_ref, kc_ref, vc_ref, bias_ref, gate_ref, ovl_ref, oc_ref, sel_ref):
    g = pl.program_id(1)
    i = pl.program_id(2)
    tq = q_ref.shape[0]
    n_pad = kc_ref.shape[0]
    q4 = q_ref[...]
    lane = lax.broadcasted_iota(jnp.int32, (n_pad, LANES), 1)
    in_g = (lane >= g * HEAD_DIM) & (lane < (g + 1) * HEAD_DIM)
    kc = jnp.where(in_g, kc_ref[...].astype(F32), 0.0).astype(BF16)
    vc = vc_ref[...]
    t = i * tq + lax.broadcasted_iota(jnp.int32, (tq, n_pad), 0)
    n = lax.broadcasted_iota(jnp.int32, (tq, n_pad), 1)
    valid = t >= n * CMP_STRIDE + (CMP_BLOCK - 1)
    gates = gate_ref[...]
    p_sum = jnp.zeros((tq, n_pad), F32)
    outs = []
    for r in range(NSA_GQA):
        qr = _dup_head(q4, r).astype(BF16)
        s = _dot_nt(qr, kc)
        s = jnp.where(valid, s + bias_ref[r], NEG_BIG)
        m = jnp.max(s, axis=-1, keepdims=True)
        p = jnp.exp2(s - m)
        p = p / jnp.sum(p, axis=-1, keepdims=True)
        p = jnp.where(valid, p, 0.0)
        p_sum = p_sum + p
        outs.append(_dot(p.astype(BF16), vc) * gates[:, 3 * r:3 * r + 1])
    oc_ref[...] = _pack_heads(outs, g).astype(oc_ref.dtype)

    ovl = ovl_ref[...]
    hi, mid, lo = _split3(p_sum)
    imp = _dot_nt(ovl, hi) + _dot_nt(ovl, mid) + _dot_nt(ovl, lo)
    n_blk = imp.shape[0]
    j = lax.broadcasted_iota(jnp.int32, (n_blk, tq), 0)
    qb = jnp.right_shift(i * tq + lax.broadcasted_iota(jnp.int32, (n_blk, tq), 1),
                         int(math.log2(SEL_BLOCK)))
    forced = (j == 0) | (j == qb) | (j == qb - 1)
    causal = j <= qb
    score = jnp.where(causal, imp + jnp.where(forced, FORCE_SCORE, 0.0), -FORCE_SCORE)
    chosen = (_rank_rows(score) < N_SEL) & causal
    sel = jnp.where(chosen, 1.0, 0.0)
    sel = jnp.concatenate([sel, jnp.zeros((LANES - n_blk, tq), F32)], axis=0)
    sel_ref[...] = sel.T


def _cmp_sel(nq, kcmp, vcmp, bias_c, gates_g, ovl_t):
    B, S, _ = nq.shape
    tq = ATT_TILE
    n_pad = kcmp.shape[1]
    return pl.pallas_call(
        _cmp_sel_kernel,
        grid=(B, NSA_GROUPS, S // tq),
        in_specs=[pl.BlockSpec((None, tq, 2 * LANES), lambda b, g, i: (b, i, g)),
                  pl.BlockSpec((None, n_pad, LANES), lambda b, g, i: (b, 0, 0)),
                  pl.BlockSpec((None, n_pad, LANES), lambda b, g, i: (b, 0, 0)),
                  pl.BlockSpec((NSA_GQA, tq, n_pad), lambda b, g, i: (g, i, 0)),
                  pl.BlockSpec((None, None, tq, 3 * NSA_GQA), lambda b, g, i: (b, g, i, 0)),
                  pl.BlockSpec(ovl_t.shape, lambda b, g, i: (0, 0))],
        out_specs=[pl.BlockSpec((None, tq, 2 * LANES), lambda b, g, i: (b, i, g)),
                   pl.BlockSpec((None, None, tq, LANES), lambda b, g, i: (b, g, i, 0))],
        out_shape=[jax.ShapeDtypeStruct((B, S, NSA_HEADS * HEAD_DIM), BF16),
                   jax.ShapeDtypeStruct((B, NSA_GROUPS, S, LANES), F32)],
        compiler_params=pltpu.CompilerParams(
            dimension_semantics=("parallel", "parallel", "arbitrary"),
            vmem_limit_bytes=VMEM_LIMIT),
        name="cmp_sel",
    )(nq, kcmp, vcmp, bias_c, gates_g, ovl_t)


def _nsa_kernel(q_ref, ks_ref, vs_ref, kw_ref, vw_ref, sel_ref, exp_ref, w4_ref, gate_ref,
                oc_ref, o_ref, madd_ref):
    g = pl.program_id(1)
    i = pl.program_id(2)
    tq = q_ref.shape[0]
    n_kt, _, tk = madd_ref.shape
    H = NSA_GQA
    q4 = q_ref[...]
    lane = lax.broadcasted_iota(jnp.int32, (tq, LANES), 1)
    in_g = (lane >= g * HEAD_DIM) & (lane < (g + 1) * HEAD_DIM)
    qs = jnp.concatenate(
        [jnp.where(in_g, _dup_head(q4, r), 0.0).astype(BF16) for r in range(H)], axis=0)

    sel = sel_ref[...].astype(BF16)
    for jj in range(n_kt):
        hit = _dot(sel, exp_ref[:, jj * tk:(jj + 1) * tk])
        madd_ref[jj] = (hit - 1.0) * (-NEG_BIG)

    def sel_step(jj, carry, diagonal):
        k0 = pl.multiple_of(jj * tk, tk)
        dd = jnp.minimum(i - 2 * jj, 3)
        s = (_dot_nt(qs, ks_ref[pl.ds(k0, tk), :]).reshape(H, tq, tk)
             + w4_ref[:, pl.ds(dd, 1)].reshape(H, tq, tk))
        madd = madd_ref[jj]
        if diagonal:
            cmr = (lax.broadcasted_iota(jnp.int32, (tq, tk), 1)
                   - lax.broadcasted_iota(jnp.int32, (tq, tk), 0))
            madd = jnp.where(cmr <= i * tq - jj * tk, madd, NEG_BIG)
        return _flash_update(carry, s + madd[None], vs_ref[pl.ds(k0, tk), :])

    init = (jnp.full((H, tq, 1), NEG_BIG, F32), jnp.zeros((H * tq, LANES), F32))
    n_full = (i * tq) // tk
    carry = lax.fori_loop(0, n_full, lambda jj, c: sel_step(jj, c, False), init)
    _, acc_s = sel_step(n_full, carry, True)

    tiles = (jnp.maximum(i - 2, 0), jnp.maximum(i - 1, 0), i)
    starts = [pl.multiple_of(j * tq, tq) for j in tiles]
    kt = jnp.concatenate([kw_ref[pl.ds(st, tq), :] for st in starts], axis=0)
    vt = jnp.concatenate([vw_ref[pl.ds(st, tq), :] for st in starts], axis=0)
    bias_w = jnp.concatenate([w4_ref[:, 2], w4_ref[:, 0, :, 0:tq]], axis=-1)
    col = lax.broadcasted_iota(jnp.int32, (tq, 3 * tq), 1)
    cmr = col - lax.broadcasted_iota(jnp.int32, (tq, 3 * tq), 0)
    never = 4 * tq
    valid = (((col < tq) & (cmr > jnp.where(i >= 2, 0, never)))
             | ((col >= jnp.where(i >= 1, tq, never)) & (col < 2 * tq))
             | ((col >= 2 * tq) & (cmr <= 2 * tq)))
    s = _dot_nt(qs, kt).reshape(H, tq, 3 * tq) + bias_w
    s = jnp.where(valid[None], s, NEG_BIG)
    p = _softmax_weights(s, jnp.max(s, axis=-1, keepdims=True))
    acc_w = _dot(p.reshape(H * tq, 3 * tq), vt)

    gates = gate_ref[...]
    o_s = _normalize(acc_s)
    o_w = _normalize(acc_w)
    outs = []
    for r in range(H):
        sl = slice(r * tq, (r + 1) * tq)
        outs.append(o_s[sl] * gates[:, 3 * r + 1:3 * r + 2] + o_w[sl] * gates[:, 3 * r + 2:3 * r + 3])
    lane = lax.broadcasted_iota(jnp.int32, (tq, LANES), 1)
    pairs = [jnp.where(lane < HEAD_DIM, outs[2 * p], _swap_halves(outs[2 * p + 1]))
             for p in range(H // 2)]
    o_ref[...] = (jnp.concatenate(pairs, axis=1) + oc_ref[...].astype(F32)).astype(o_ref.dtype)


def _nsa(nq, ks, vs, kw, vw, sel, expand, w4, gates_g, oc):
    B, S, _ = nq.shape
    tq = ATT_TILE
    tk = w4.shape[-1]
    keys = lambda: pl.BlockSpec((None, S, LANES), lambda b, g, i: (b, 0, 0))
    vals = lambda: pl.BlockSpec((None, S, LANES), lambda b, g, i: (b, 0, g))
    return pl.pallas_call(
        _nsa_kernel,
        grid=(B, NSA_GROUPS, S // tq),
        in_specs=[pl.BlockSpec((None, tq, 2 * LANES), lambda b, g, i: (b, i, g)),
                  keys(), vals(), keys(), vals(),
                  pl.BlockSpec((None, None, tq, LANES), lambda b, g, i: (b, g, i, 0)),
                  pl.BlockSpec(expand.shape, lambda b, g, i: (0, 0)),
                  pl.BlockSpec((NSA_GQA, 4, tq, tk), lambda b, g, i: (g, 0, 0, 0)),
                  pl.BlockSpec((None, None, tq, 3 * NSA_GQA), lambda b, g, i: (b, g, i, 0)),
                  pl.BlockSpec((None, tq, 2 * LANES), lambda b, g, i: (b, i, g))],
        out_specs=pl.BlockSpec((None, tq, 2 * LANES), lambda b, g, i: (b, i, g)),
        out_shape=jax.ShapeDtypeStruct((B, S, NSA_HEADS * HEAD_DIM), BF16),
        scratch_shapes=[pltpu.VMEM((S // tk, tq, tk), F32)],
        compiler_params=pltpu.CompilerParams(
            dimension_semantics=("parallel", "parallel", "arbitrary"),
            vmem_limit_bytes=VMEM_LIMIT),
        name="nsa",
    )(nq, ks, vs, kw, vw, sel, expand, w4, gates_g, oc)


def _layer_norm(y, g, b):
    mu = jnp.mean(y, axis=-1, keepdims=True)
    yc = y - mu
    var = jnp.mean(yc * yc, axis=-1, keepdims=True)
    return yc * lax.rsqrt(var + LN_EPS) * g + b


def _router_gates_t(h2, wr_t, eb_col):
    tm = h2.shape[0]
    h_parts = _split3(h2)
    w_parts = _split3(wr_t)
    logit = jnp.zeros((N_EXPERTS, tm), F32)
    for a, wp in enumerate(w_parts):
        for b, hp in enumerate(h_parts):
            if a + b <= 2:
                logit = logit + _dot_nt(wp, hp)
    scores = _sigmoid(logit)
    biased = scores + eb_col
    e_in = lax.broadcasted_iota(jnp.int32, (GROUP_SIZE, tm), 0).astype(F32)
    gs_rows = []
    for gi in range(N_EXPERT_GROUPS):
        grp = biased[gi * GROUP_SIZE:(gi + 1) * GROUP_SIZE, :]
        m1 = jnp.max(grp, axis=0, keepdims=True)
        first = jnp.min(jnp.where(grp == m1, e_in, float(GROUP_SIZE)), axis=0, keepdims=True)
        m2 = jnp.max(jnp.where(e_in == first, -jnp.inf, grp), axis=0, keepdims=True)
        gs_rows.append(m1 + m2)
    gscore = jnp.concatenate(gs_rows, axis=0)
    g_keep = _rank_rows(gscore) < TOPK_GROUPS
    keep = jnp.concatenate(
        [jnp.broadcast_to(g_keep[gi:gi + 1, :], (GROUP_SIZE, tm)) for gi in range(N_EXPERT_GROUPS)],
        axis=0)
    masked = jnp.where(keep, biased, -jnp.inf)
    chosen = _rank_rows(masked) < TOP_K
    w = jnp.where(chosen, scores, 0.0)
    return w / jnp.sum(w, axis=0, keepdims=True) * ROUTED_SCALE


def _out_proj_kernel(alpha, of_ref, on_ref, x_ref, mod_ref, w_ref, lg_ref, lb_ref, wr_ref,
                     eb_ref, x1_ref, h2_ref, gate_ref, gate_t_ref):
    half = of_ref.shape[1]
    mod = mod_ref[...]
    mixed = _dot(of_ref[...], w_ref[0:half, :]) + _dot(on_ref[...], w_ref[half:2 * half, :])
    y = alpha * x_ref[...] + mod[2:3, :] * mixed
    x1 = _layer_norm(y, lg_ref[...], lb_ref[...])
    x1_ref[...] = x1
    h2 = x1 * (1.0 + mod[4:5, :]) + mod[3:4, :]
    h2_ref[...] = h2.astype(h2_ref.dtype)
    gates_t = _router_gates_t(h2, wr_ref[...], eb_ref[...])
    gate_t_ref[...] = gates_t
    tm = h2.shape[0]
    gates_t = jnp.concatenate([gates_t, jnp.zeros((LANES - N_EXPERTS, tm), F32)], axis=0)
    gate_ref[...] = gates_t.T


def _out_proj(alpha, o_fox, o_nsa, x, mod, w_out, ln_g, ln_b, wr_t, eb_col):
    B, S, D = x.shape
    tm = ROW_TILE
    half = o_fox.shape[-1]
    row = lambda a: pl.BlockSpec(a.shape, lambda b, s: (0, 0))
    return pl.pallas_call(
        functools.partial(_out_proj_kernel, alpha),
        grid=(B, S // tm),
        in_specs=[pl.BlockSpec((None, tm, half), lambda b, s: (b, s, 0)),
                  pl.BlockSpec((None, tm, half), lambda b, s: (b, s, 0)),
                  pl.BlockSpec((None, tm, D), lambda b, s: (b, s, 0)),
                  pl.BlockSpec((None, 6, D), lambda b, s: (b, 0, 0)),
                  row(w_out), row(ln_g), row(ln_b), row(wr_t), row(eb_col)],
        out_specs=[pl.BlockSpec((None, tm, D), lambda b, s: (b, s, 0)),
                   pl.BlockSpec((None, tm, D), lambda b, s: (b, s, 0)),
                   pl.BlockSpec((None, tm, LANES), lambda b, s: (b, s, 0)),
                   pl.BlockSpec((N_EXPERTS, tm), lambda b, s: (0, b * (S // tm) + s))],
        out_shape=[jax.ShapeDtypeStruct((B, S, D), F32),
                   jax.ShapeDtypeStruct((B, S, D), BF16),
                   jax.ShapeDtypeStruct((B, S, LANES), F32),
                   jax.ShapeDtypeStruct((N_EXPERTS, B * S), F32)],
        compiler_params=pltpu.CompilerParams(dimension_semantics=("parallel", "parallel"),
                                             vmem_limit_bytes=VMEM_LIMIT),
        name="out_proj",
    )(o_fox, o_nsa, x, mod, w_out, ln_g, ln_b, wr_t, eb_col)


SORT_TILE = 256
ROW_ALIGN = 16
EXP_TILE = 512
P_CHUNK = 256


def _strict_upper(n):
    return jnp.where(lax.broadcasted_iota(jnp.int32, (n, n), 0)
                     < lax.broadcasted_iota(jnp.int32, (n, n), 1), 1.0, 0.0).astype(BF16)


def _strict_lower(n):
    return jnp.where(lax.broadcasted_iota(jnp.int32, (n, n), 1)
                     < lax.broadcasted_iota(jnp.int32, (n, n), 0), 1.0, 0.0).astype(BF16)


def _local_rows_bound(ts):
    rows = TOP_K * ts + N_EXPERTS * (ROW_ALIGN - 1)
    return -(-rows // P_CHUNK) * P_CHUNK


def _piece_cols(ts):
    return -(-(_local_rows_bound(ts) // ROW_ALIGN) // LANES) * LANES


def _sorted_tiles_bound(T):
    rows = TOP_K * T + (T // SORT_TILE) * N_EXPERTS * (ROW_ALIGN - 1)
    return -(-rows // EXP_TILE) + N_EXPERTS


def _moe_meta_kernel(gt_ref, ptab_ref, loc_et_ref, np_et_ref, loc_te_ref, np_te_ref, tot_ref,
                     erow_ref, texp_ref, nused_ref):
    E, T = gt_ref.shape
    mask = jnp.where(gt_ref[...] > 0.0, 1.0, 0.0).astype(BF16)
    t_id = lax.shift_right_logical(lax.broadcasted_iota(jnp.int32, (T, LANES), 0),
                                   int(math.log2(SORT_TILE)))
    tind = jnp.where(t_id == lax.broadcasted_iota(jnp.int32, (T, LANES), 1), 1.0, 0.0)
    cnt = _dot(mask, tind.astype(BF16))
    n16 = jnp.floor((cnt + (ROW_ALIGN - 1.0)) * (1.0 / ROW_ALIGN))
    n16b = n16.astype(BF16)
    q = EXP_TILE // ROW_ALIGN
    len16 = jnp.sum(n16, axis=1, keepdims=True)
    pad16 = jnp.floor((len16 + (q - 1.0)) * (1.0 / q)) * q
    sl = _strict_lower(E)
    hi, mid, lo = _split3(jnp.broadcast_to(pad16, (E, LANES)))
    start16 = _dot(sl, hi) + _dot(sl, mid) + _dot(sl, lo)
    gdst16 = start16 + _dot(n16b, _strict_upper(LANES))
    loc16 = _dot(sl, n16b)

    def t(a):
        return jnp.concatenate([a, jnp.zeros((LANES - E, LANES), F32)], axis=0).T

    scale = float(ROW_ALIGN)
    loc_et_ref[...] = loc16 * scale
    np_et_ref[...] = n16 * scale
    loc_te_ref[...] = t(loc16) * scale
    np_te_ref[...] = t(n16) * scale
    tot_ref[...] = (jnp.sum(n16, axis=0, keepdims=True) * scale).astype(jnp.int32)

    n_t, n_blk = ptab_ref.shape
    blk = lax.broadcasted_iota(jnp.int32, (E, n_blk), 1).astype(F32)
    for tile in range(n_t):
        lo_c = loc16[:, tile:tile + 1]
        inside = (lo_c <= blk) & (blk < lo_c + n16[:, tile:tile + 1])
        dst = jnp.sum(jnp.where(inside, gdst16[:, tile:tile + 1] + (blk - lo_c), 0.0),
                      axis=0, keepdims=True)
        ptab_ref[tile:tile + 1, :] = (dst * scale).astype(jnp.int32)
    ends = jnp.concatenate([t(start16 + len16)[0:1, :], t(start16 + pad16)[0:1, :],
                            jnp.zeros((erow_ref.shape[0] - 2, LANES), F32)], axis=0)
    erow_ref[...] = (ends * scale).astype(jnp.int32)
    n_tab = texp_ref.shape[1]
    tile_row16 = (lax.broadcasted_iota(jnp.int32, (E, n_tab), 1) * q).astype(F32)
    owner = jnp.sum(jnp.where(start16[:, 0:1] <= tile_row16, 1.0, 0.0), axis=0, keepdims=True)
    texp_ref[...] = (owner - 1.0).astype(jnp.int32)
    n_used = jnp.sum(pad16, axis=0, keepdims=True) * (1.0 / q)
    nused_ref[...] = jnp.broadcast_to(n_used, (1, LANES)).astype(jnp.int32)


def _moe_meta(gates_t, n_tab):
    E, T = gates_t.shape
    i32 = jnp.int32
    return pl.pallas_call(
        _moe_meta_kernel,
        out_shape=[jax.ShapeDtypeStruct((T // SORT_TILE, _piece_cols(SORT_TILE)), i32),
                   jax.ShapeDtypeStruct((E, LANES), F32),
                   jax.ShapeDtypeStruct((E, LANES), F32),
                   jax.ShapeDtypeStruct((LANES, LANES), F32),
                   jax.ShapeDtypeStruct((LANES, LANES), F32),
                   jax.ShapeDtypeStruct((1, LANES), i32),
                   jax.ShapeDtypeStruct((8, LANES), i32),
                   jax.ShapeDtypeStruct((1, n_tab), i32),
                   jax.ShapeDtypeStruct((1, LANES), i32)],
        compiler_params=pltpu.CompilerParams(vmem_limit_bytes=VMEM_LIMIT),
        name="moe_meta",
    )(gates_t)


def _start_pieces(tile, ptab_s, tot_s, n_cols, make_copy):
    n_pieces = lax.shift_right_logical(tot_s[tile], int(math.log2(ROW_ALIGN)))

    def body(b, carry):
        make_copy(pl.multiple_of(b * ROW_ALIGN, ROW_ALIGN),
                  pl.multiple_of(ptab_s[tile * n_cols + b], ROW_ALIGN)).start()
        return carry

    lax.fori_loop(0, n_pieces, body, 0)


def _moe_sort_kernel(ptab_s, tot_s, lend_s, rend_s, gt_ref, h_ref, locrow_ref, nprow_ref,
                     xs_hbm, buf, zbuf, sem, zsem):
    tau = pl.program_id(0)
    n_t = pl.num_programs(0)
    slot = lax.rem(tau, 2)
    E, ts = gt_ref.shape
    D = h_ref.shape[1]
    n_cols = _piece_cols(ts)

    def copies(tile, sl, wait):
        if wait:
            rows = pl.multiple_of(tot_s[tile], ROW_ALIGN)

            @pl.when(rows > 0)
            def _():
                pltpu.make_async_copy(buf.at[sl, pl.ds(0, rows)], xs_hbm.at[pl.ds(0, rows)],
                                      sem.at[sl]).wait()
        else:
            def make_copy(loc, dst):
                return pltpu.make_async_copy(buf.at[sl, pl.ds(loc, ROW_ALIGN)],
                                             xs_hbm.at[pl.ds(dst, ROW_ALIGN)], sem.at[sl])
            _start_pieces(tile, ptab_s, tot_s, n_cols, make_copy)

    z_rows = zbuf.shape[0]
    used_rows = rend_s[E - 1]
    n_spare = (xs_hbm.shape[0] - used_rows) // z_rows

    def spare_fill(wait):
        def body(c, carry):
            dst = pl.multiple_of(used_rows + c * z_rows, z_rows)
            cp = pltpu.make_async_copy(zbuf, xs_hbm.at[pl.ds(dst, z_rows)], zsem.at[1])
            if wait:
                cp.wait()
            else:
                cp.start()
            return carry

        lax.fori_loop(0, n_spare, body, 0)

    @pl.when(tau == 0)
    def _():
        zbuf[...] = jnp.zeros_like(zbuf)
        spare_fill(False)

    @pl.when(tau >= 2)
    def _():
        copies(tau - 2, slot, True)

    g = gt_ref[...]
    mask = g > 0.0
    maskb = jnp.where(mask, 1.0, 0.0).astype(BF16)
    pad = jnp.zeros((LANES - E, ts), F32)
    pos = jnp.where(mask, _dot(maskb, _strict_upper(ts)), -1.0)
    pos = jnp.concatenate([pos, pad], axis=0).astype(BF16)
    g_parts = _split3(jnp.concatenate([g, pad], axis=0))
    lo_row = locrow_ref[...]
    hi_row = lo_row + nprow_ref[...]
    h = h_ref[...]
    n_chunks = lax.shift_right_logical(tot_s[tau] + (P_CHUNK - 1), int(math.log2(P_CHUNK)))
    lane = lax.broadcasted_iota(jnp.int32, (P_CHUNK, LANES), 1)

    def chunk(c):
        r0 = pl.multiple_of(c * P_CHUNK, P_CHUNK)
        r = (r0 + lax.broadcasted_iota(jnp.int32, (P_CHUNK, LANES), 0)).astype(F32)
        inside = (lo_row <= r) & (r < hi_row)
        group = jnp.where(inside, 1.0, 0.0).astype(BF16)
        want = r[:, 0:1] - jnp.sum(jnp.where(inside, lo_row, 0.0), axis=1, keepdims=True)
        hit = _dot(group, pos) == want
        buf[slot, pl.ds(r0, P_CHUNK), 0:D] = _dot(jnp.where(hit, 1.0, 0.0).astype(BF16),
                                                  h).astype(buf.dtype)
        extra = jnp.zeros((P_CHUNK, LANES), F32)
        for k, part in enumerate(g_parts):
            term = jnp.sum(jnp.where(hit, _dot(group, part), 0.0), axis=1, keepdims=True)
            extra = jnp.where(lane == k, term, extra)
        buf[slot, pl.ds(r0, P_CHUNK), D:D + LANES] = extra.astype(buf.dtype)

    def pair_body(c, carry):
        chunk(2 * c)
        chunk(2 * c + 1)
        return carry

    lax.fori_loop(0, lax.shift_right_logical(n_chunks + 1, 1), pair_body, 0)
    copies(tau, slot, False)

    @pl.when(tau == n_t - 1)
    def _():
        @pl.when(n_t >= 2)
        def _():
            copies(tau - 1, 1 - slot, True)
        copies(tau, slot, True)
        spare_fill(True)

        sizes =[zbuf.shape[0] >> s for s in range(int(math.log2(zbuf.shape[0] // ROW_ALIGN)) + 1)]

        def fill(wait):
            def e_body(e, carry):
                start = lend_s[e]
                n = rend_s[e] - start
                off = start
                for size in sizes:
                    bit = jnp.bitwise_and(n, size)

                    @pl.when(bit != 0)
                    def _(off=off, size=size):
                        cp = pltpu.make_async_copy(
                            zbuf.at[pl.ds(0, size)],
                            xs_hbm.at[pl.ds(pl.multiple_of(off, ROW_ALIGN), size)], zsem.at[0])
                        if wait:
                            cp.wait()
                        else:
                            cp.start()

                    off = off + bit
                return carry

            lax.fori_loop(0, E, e_body, 0)

        fill(False)
        fill(True)


def _moe_sort(ptab, tot, lend, rend, gates_t, h2, loc_te, np_te, n_rows):
    E, T = gates_t.shape
    D = h2.shape[1]
    ts = SORT_TILE
    grid_spec = pltpu.PrefetchScalarGridSpec(
        num_scalar_prefetch=4,
        grid=(T // ts,),
        in_specs=[pl.BlockSpec((E, ts), lambda t, *_: (0, t)),
                  pl.BlockSpec((ts, D), lambda t, *_: (t, 0)),
                  pl.BlockSpec((None, 1, LANES), lambda t, *_: (t, 0, 0)),
                  pl.BlockSpec((None, 1, LANES), lambda t, *_: (t, 0, 0))],
        out_specs=pl.BlockSpec(memory_space=pl.ANY),
        scratch_shapes=[pltpu.VMEM((2, _local_rows_bound(ts), D + LANES), BF16),
                        pltpu.VMEM((EXP_TILE // 2, D + LANES), BF16),
                        pltpu.SemaphoreType.DMA((2,)),
                        pltpu.SemaphoreType.DMA((2,))])
    return pl.pallas_call(
        _moe_sort_kernel,
        grid_spec=grid_spec,
        out_shape=jax.ShapeDtypeStruct((n_rows, D + LANES), BF16),
        compiler_params=pltpu.CompilerParams(dimension_semantics=("arbitrary",),
                                             vmem_limit_bytes=VMEM_LIMIT),
        name="moe_sort",
    )(ptab, tot, lend, rend, gates_t, h2, loc_te, np_te)


def _moe_expert_kernel(texp_s, nused_s, x_ref, wg_ref, wu_ref, wd_ref, y_ref, wgu_s, wd_s):
    i = pl.program_id(0)
    f = wd_ref.shape[0]

    @pl.when(i < nused_s[0])
    def _():
        @pl.when((i == 0) | (texp_s[i] != texp_s[jnp.maximum(i - 1, 0)]))
        def _():
            wgu_s[:, 0:f] = wg_ref[...].astype(BF16)
            wgu_s[:, f:2 * f] = wu_ref[...].astype(BF16)
            wd_s[...] = wd_ref[...].astype(BF16)

        d = wd_ref.shape[1]
        gate = jnp.sum(x_ref[:, d:].astype(F32), axis=1, keepdims=True)
        a = _dot(x_ref[:, 0:d], wgu_s[...])
        act = _silu(a[:, :f]) * a[:, f:] * gate
        y_ref[...] = _dot(act.astype(BF16), wd_s[...]).astype(y_ref.dtype)

    @pl.when(i >= nused_s[0])
    def _():
        y_ref[...] = jnp.zeros_like(y_ref)


def _moe_expert(texp, nused, xs, w_gate, w_up, w_down, n_tiles):
    n_rows, xw = xs.shape
    D, f = w_gate.shape[-2:]
    tm = EXP_TILE

    def tile(i, texp, nused):
        return jnp.maximum(jnp.minimum(i, nused[0] - 1), 0)

    grid_spec = pltpu.PrefetchScalarGridSpec(
        num_scalar_prefetch=2,
        grid=(n_tiles,),
        in_specs=[pl.BlockSpec((tm, xw), lambda i, te, nu: (tile(i, te, nu), 0)),
                  pl.BlockSpec((None, D, f), lambda i, te, nu: (te[tile(i, te, nu)], 0, 0)),
                  pl.BlockSpec((None, D, f), lambda i, te, nu: (te[tile(i, te, nu)], 0, 0)),
                  pl.BlockSpec((None, f, D), lambda i, te, nu: (te[tile(i, te, nu)], 0, 0))],
        out_specs=pl.BlockSpec((tm, D), lambda i, te, nu: (i, 0)),
        scratch_shapes=[pltpu.VMEM((D, 2 * f), BF16), pltpu.VMEM((f, D), BF16)])
    return pl.pallas_call(
        _moe_expert_kernel,
        grid_spec=grid_spec,
        out_shape=jax.ShapeDtypeStruct((n_rows, D), BF16),
        compiler_params=pltpu.CompilerParams(dimension_semantics=("arbitrary",),
                                             vmem_limit_bytes=VMEM_LIMIT),
        name="moe_expert",
    )(texp, nused, xs, w_gate, w_up, w_down)


def _moe_combine_kernel(alpha, ptab_s, tot_s, g_ref, loc_ref, np_ref, h_ref, x1_ref,
                        mod_ref, sgu_ref, sd_ref, lg_ref, lb_ref, y_hbm, o_ref, ybuf, acc_ref,
                        sem):
    tau = pl.program_id(0)
    n_t = pl.num_programs(0)
    slot = lax.rem(tau, 2)
    ts, n_lane = g_ref.shape
    E = loc_ref.shape[0]
    n_cols = _piece_cols(ts)

    def copies(tile, sl, wait):
        if wait:
            rows = pl.multiple_of(tot_s[tile], ROW_ALIGN)

            @pl.when(rows > 0)
            def _():
                pltpu.make_async_copy(y_hbm.at[pl.ds(0, rows)], ybuf.at[sl, pl.ds(0, rows)],
                                      sem.at[sl]).wait()
        else:
            def make_copy(loc, dst):
                return pltpu.make_async_copy(y_hbm.at[pl.ds(dst, ROW_ALIGN)],
                                             ybuf.at[sl, pl.ds(loc, ROW_ALIGN)], sem.at[sl])
            _start_pieces(tile, ptab_s, tot_s, n_cols, make_copy)

    @pl.when(tau == 0)
    def _():
        ybuf[...] = jnp.zeros_like(ybuf)
        copies(0, 0, False)

    @pl.when(tau + 1 < n_t)
    def _():
        copies(tau + 1, 1 - slot, False)

    g = g_ref[...]
    mask = g > 0.0
    maskb = jnp.where(mask, 1.0, 0.0).astype(BF16)
    pos = jnp.where(mask, _dot(_strict_lower(ts), maskb), -1.0).astype(BF16)
    lane = lax.broadcasted_iota(jnp.int32, loc_ref.shape, 1)
    lo_col = jnp.sum(jnp.where(lane == tau, loc_ref[...], 0.0), axis=1, keepdims=True)
    hi_col = lo_col + jnp.sum(jnp.where(lane == tau, np_ref[...], 0.0), axis=1, keepdims=True)

    f = sd_ref.shape[0]
    a = _dot(h_ref[...], sgu_ref[...])
    acc_ref[...] = _dot((_silu(a[:, :f]) * a[:, f:]).astype(BF16), sd_ref[...])

    copies(tau, slot, True)
    n_chunks = lax.shift_right_logical(tot_s[tau] + (P_CHUNK - 1), int(math.log2(P_CHUNK)))

    def chunk(c):
        r0 = pl.multiple_of(c * P_CHUNK, P_CHUNK)
        r = (r0 + lax.broadcasted_iota(jnp.int32, (E, P_CHUNK), 1)).astype(F32)
        inside = (lo_col <= r) & (r < hi_col)
        group = jnp.concatenate([jnp.where(inside, 1.0, 0.0),
                                 jnp.zeros((n_lane - E, P_CHUNK), F32)], axis=0).astype(BF16)
        want = r[0:1, :] - jnp.sum(jnp.where(inside, lo_col, 0.0), axis=0, keepdims=True)
        hit = _dot(pos, group) == want
        return _dot(jnp.where(hit, 1.0, 0.0).astype(BF16), ybuf[slot, pl.ds(r0, P_CHUNK), :])

    def pair_body(c, carry):
        acc_ref[...] += chunk(2 * c) + chunk(2 * c + 1)
        return carry

    lax.fori_loop(0, lax.shift_right_logical(n_chunks + 1, 1), pair_body, 0)
    y = alpha * x1_ref[...] + mod_ref[5:6, :] * acc_ref[...]
    o_ref[...] = _layer_norm(y, lg_ref[...], lb_ref[...])


def _moe_combine(alpha, ptab, tot, gates, loc_et, np_et, h2, x1, mod, sgu, sd, ln_g, ln_b, ys,
                 S):
    T, D = h2.shape
    ts = SORT_TILE
    per_b = S // ts
    row = lambda a: pl.BlockSpec(a.shape, lambda t, *_: (0, 0))
    grid_spec = pltpu.PrefetchScalarGridSpec(
        num_scalar_prefetch=2,
        grid=(T // ts,),
        in_specs=[pl.BlockSpec((ts, LANES), lambda t, *_: (t, 0)),
                  row(loc_et), row(np_et),
                  pl.BlockSpec((ts, D), lambda t, *_: (t, 0)),
                  pl.BlockSpec((ts, D), lambda t, *_: (t, 0)),
                  pl.BlockSpec((None, 6, D), lambda t, *_: (t // per_b, 0, 0)),
                  row(sgu), row(sd), row(ln_g), row(ln_b),
                  pl.BlockSpec(memory_space=pl.ANY)],
        out_specs=pl.BlockSpec((ts, D), lambda t, *_: (t, 0)),
        scratch_shapes=[pltpu.VMEM((2, _local_rows_bound(ts), D), BF16),
                        pltpu.VMEM((ts, D), F32),
                        pltpu.SemaphoreType.DMA((2,))])
    return pl.pallas_call(
        functools.partial(_moe_combine_kernel, alpha),
        grid_spec=grid_spec,
        out_shape=jax.ShapeDtypeStruct((T, D), F32),
        compiler_params=pltpu.CompilerParams(dimension_semantics=("arbitrary",),
                                             vmem_limit_bytes=VMEM_LIMIT),
        name="moe_combine",
    )(ptab, tot, gates, loc_et, np_et, h2, x1, mod, sgu, sd, ln_g, ln_b, ys)


def _moe(alpha, h2, x1, gates, gates_t, mod, w_gate, w_up, w_down, sgu, sd, ln_g, ln_b):
    B, S, D = x1.shape
    T = B * S
    n_t = T // SORT_TILE
    n_tiles = _sorted_tiles_bound(T)
    n_tab = -(-n_tiles // LANES) * LANES
    ptab, loc_et, np_et, loc_te, np_te, tot, erow, texp, nused = _moe_meta(gates_t, n_tab)
    ptab = ptab.reshape(-1)
    tot = tot[0, :n_t]
    h2 = h2.reshape(T, D)
    xs = _moe_sort(ptab, tot, erow[0, :N_EXPERTS], erow[1, :N_EXPERTS], gates_t, h2,
                   loc_te[:n_t].reshape(n_t, 1, LANES), np_te[:n_t].reshape(n_t, 1, LANES),
                   n_tiles * EXP_TILE)
    ys = _moe_expert(texp[0], nused[0, :1], xs, w_gate, w_up, w_down, n_tiles)
    out = _moe_combine(alpha, ptab, tot, gates.reshape(T, LANES), loc_et, np_et, h2,
                       x1.reshape(T, D), mod, sgu, sd, ln_g, ln_b, ys, S)
    return out.reshape(B, S, D)


def _rearrange_w_in(w):
    d_in = w.shape[0]
    scale = HEAD_DIM ** -0.5 * LOG2E
    fq, fk, fv = w[:, 0:512], w[:, 512:1024], w[:, 1024:1536]
    ff = w[:, 1536:1544]
    nq = w[:, 1544:2056]
    kc, vc, ks, vs, kw, vw = (w[:, 2056 + k * LANES:2056 + (k + 1) * LANES] for k in range(6))
    ng = w[:, 2824:2848]
    pad = jnp.zeros((d_in, LANES - ff.shape[1] - ng.shape[1]), w.dtype)

    def spaced(v):
        v = v.reshape(d_in, -1, HEAD_DIM)
        return jnp.concatenate([v, jnp.zeros_like(v)], axis=2).reshape(d_in, -1)

    cols = [fq * scale, fk, nq * scale, kc, ks, kw, vc, spaced(fv), spaced(vs), spaced(vw), ff, ng,
            pad]
    return jnp.concatenate(cols, axis=1).astype(BF16)


def _ones_row():
    row = np.zeros((1, _IN_COLS), np.float32)
    lanes = np.arange(_C_FV, _C_MISC)
    row[0, lanes[(lanes % LANES) >= HEAD_DIM]] = 1.0
    return row


def _compress_weights(pos, w1, w2):
    half = CMP_BLOCK // 2
    w1r = w1.reshape(2, half, HEAD_DIM, CMP_HIDDEN)
    zeros = jnp.zeros_like(w1r[0])
    def spread(part):
        g0 = jnp.stack([part, zeros], axis=1).reshape(half * 2 * HEAD_DIM, CMP_HIDDEN)
        g1 = jnp.stack([zeros, part], axis=1).reshape(half * 2 * HEAD_DIM, CMP_HIDDEN)
        return jnp.concatenate([g0, g1], axis=1).astype(BF16)
    wa, wb = spread(w1r[0]), spread(w1r[1])
    z2 = jnp.zeros_like(w2)
    w2bd = jnp.concatenate([jnp.concatenate([w2, z2], axis=1),
                            jnp.concatenate([z2, w2], axis=1)], axis=0).astype(BF16)
    posr = pos.reshape(2, half, 1, HEAD_DIM)
    posr = jnp.broadcast_to(posr, (2, half, NSA_GROUPS, HEAD_DIM)).reshape(2, half * 2 * HEAD_DIM)
    return posr, wa, wb, w2bd


@functools.lru_cache(maxsize=None)
def _static_tables(S):
    tq = ATT_TILE
    n_cmp = (S - CMP_BLOCK) // CMP_STRIDE + 1
    n_pad = S // CMP_STRIDE
    n_slc = S // SEL_BLOCK
    t = np.arange(S)[:, None]
    n = np.arange(n_pad)[None, :]
    bucket_c = _t5_bucket_np(t - (n * CMP_STRIDE + CMP_BLOCK - 1)).reshape(1, -1)
    d = (np.arange(4)[:, None, None] * tq + np.arange(tq)[None, :, None]
         - np.arange(2 * tq)[None, None, :])
    bucket_w = _t5_bucket_np(d).reshape(1, -1)
    cs = np.arange(n_pad)[None, :] * CMP_STRIDE
    sj = np.arange(n_slc)[:, None] * SEL_BLOCK
    ovl_t = ((cs < sj + SEL_BLOCK) & (cs + CMP_BLOCK > sj) & (np.arange(n_pad)[None, :] < n_cmp))
    expand = (np.arange(S)[None, :] // SEL_BLOCK) == np.arange(LANES)[:, None]
    return bucket_c, bucket_w, ovl_t.astype(np.float32), expand.astype(np.float32)


def kernel(x, c, w_ada, b_ada, w_in, b_f, cmp_pos_k, cmp_w1_k, cmp_w2_k, cmp_pos_v, cmp_w1_v,
           cmp_w2_v, rel_bias, w_out, ln1_g, ln1_b, w_router, e_bias, w_gate, w_up, w_down,
           ws_gate, ws_up, ws_down, ln2_g, ln2_b):
    B, S, D = x.shape
    depth = w_ada.shape[0]
    alpha = (2 * depth) ** 0.25
    tq = ATT_TILE
    bucket_c, bucket_w, ovl_t, expand = _static_tables(S)
    rel_bias_t = rel_bias.T * LOG2E
    bias_c = _bias_table(jnp.asarray(bucket_c), rel_bias_t).reshape(NSA_HEADS, S, S // CMP_STRIDE)
    w4 = _bias_table(jnp.asarray(bucket_w), rel_bias_t).reshape(NSA_HEADS, 4, tq, 2 * tq)
    ovl_t = jnp.asarray(ovl_t, BF16)
    expand = jnp.asarray(expand, BF16)

    for l in range(depth):
        mod = _ada(c, w_ada[l], b_ada[l]).reshape(B, 6, D)
        bf_row = jnp.zeros((1, LANES), F32).at[0, :FOX_HEADS].set(b_f[l])
        (fq, fk, nq, kc, ks, kw, vc, fv, vs, vw, misc, misc_t) = _in_proj(
            x, mod, _rearrange_w_in(w_in[l]), bf_row, jnp.asarray(_ones_row()))

        o_fox = _fox(fq, fk, fv, misc_t[:, :FOX_HEADS, :])

        pk, wak, wbk, w2k = _compress_weights(cmp_pos_k[l], cmp_w1_k[l], cmp_w2_k[l])
        pv, wav, wbv, w2v = _compress_weights(cmp_pos_v[l], cmp_w1_v[l], cmp_w2_v[l])
        rows = S // CMP_STRIDE
        kcmp, vcmp = _compress(kc.reshape(B, rows, CMP_STRIDE * LANES),
                               vc.reshape(B, rows, CMP_STRIDE * LANES),
                               pk, pv, wak, wbk, wav, wbv, w2k, w2v)

        gates_g = misc[:, :, FOX_HEADS:FOX_HEADS + 3 * NSA_HEADS].reshape(
            B, S, NSA_GROUPS, 3 * NSA_GQA).transpose(0, 2, 1, 3)
        oc, sel = _cmp_sel(nq, kcmp, vcmp, bias_c, gates_g, ovl_t)
        o_nsa = _nsa(nq, ks, vs, kw, vw, sel, expand, w4, gates_g, oc)

        x1, h2, gates, gates_t = _out_proj(
            alpha, o_fox, o_nsa, x, mod, w_out[l].astype(BF16), ln1_g[l].reshape(1, D),
            ln1_b[l].reshape(1, D), w_router[l].T, e_bias[l].reshape(N_EXPERTS, 1))

        sgu = jnp.concatenate([ws_gate[l], ws_up[l]], axis=-1).astype(BF16)
        x = _moe(alpha, h2, x1, gates, gates_t, mod, w_gate[l], w_up[l], w_down[l], sgu,
                 ws_down[l].astype(BF16), ln2_g[l].reshape(1, D), ln2_b[l].reshape(1, D))
    return x
```

```python
import functools
import math

import jax
import jax.numpy as jnp
import numpy as np
from jax import lax
from jax.experimental import pallas as pl
from jax.experimental.pallas import tpu as pltpu

F32 = jnp.float32
BF16 = jnp.bfloat16

HEAD_DIM = 64
FOX_HEADS = 8
NSA_HEADS = 8
NSA_GQA = 4
NSA_GROUPS = NSA_HEADS // NSA_GQA
CMP_BLOCK = 32
CMP_STRIDE = 16
CMP_HIDDEN = 256
SEL_BLOCK = 64
N_SEL = 16
WINDOW = 512
N_BUCKETS = 32
MAX_DISTANCE = 128
N_EXPERTS = 64
N_EXPERT_GROUPS = 8
GROUP_SIZE = N_EXPERTS // N_EXPERT_GROUPS
TOPK_GROUPS = 4
TOP_K = 8
D_EXPERT = 256
ROUTED_SCALE = 2.5
LN_EPS = 1e-5
NEG_BIG = -1e30
FORCE_SCORE = 1e4

LANES = 128
ATT_TILE = 256
FOX_TILE = 512
ROW_TILE = 512
MOE_TILE = 1024
VMEM_LIMIT = 48 * 1024 * 1024

NT_DIMS = (((1,), (1,)), ((), ()))


def _dot(a, b):
    return jnp.dot(a, b, preferred_element_type=F32)


def _dot_nt(a, b):
    return lax.dot_general(a, b, NT_DIMS, preferred_element_type=F32)


def _split3(x):
    hi = x.astype(BF16)
    r1 = x - hi.astype(F32)
    mid = r1.astype(BF16)
    lo = (r1 - mid.astype(F32)).astype(BF16)
    return hi, mid, lo


def _silu(x):
    return x / (1.0 + jnp.exp(-x))


def _sigmoid(x):
    return 1.0 / (1.0 + jnp.exp(-x))


def _swap_halves(x):
    return pltpu.roll(x, HEAD_DIM, 1)


def _t5_bucket_np(dist):
    n = np.maximum(dist, 0)
    max_exact = N_BUCKETS // 2
    nf = np.maximum(n, 1).astype(np.float32)
    large = max_exact + (np.log(nf / max_exact) / math.log(MAX_DISTANCE / max_exact)
                         * (N_BUCKETS - max_exact)).astype(np.int32)
    large = np.minimum(large, N_BUCKETS - 1)
    return np.where(n < max_exact, n, large).astype(np.int32)


def _ada_kernel(c_ref, w_ref, b_ref, o_ref):
    c = c_ref[...]
    o_ref[...] = jnp.dot(_silu(c), w_ref[...], preferred_element_type=F32,
                         precision=lax.Precision.HIGHEST) + b_ref[...]


def _ada(c, w_ada, b_ada):
    B, D = c.shape
    n_out = w_ada.shape[1]
    tn = 1024
    return pl.pallas_call(
        _ada_kernel,
        grid=(n_out // tn,),
        in_specs=[pl.BlockSpec((B, D), lambda j: (0, 0)),
                  pl.BlockSpec((D, tn), lambda j: (0, j)),
                  pl.BlockSpec((1, tn), lambda j: (0, j))],
        out_specs=pl.BlockSpec((B, tn), lambda j: (0, j)),
        out_shape=jax.ShapeDtypeStruct((B, n_out), F32),
        compiler_params=pltpu.CompilerParams(dimension_semantics=("arbitrary",),
                                             vmem_limit_bytes=VMEM_LIMIT),
        name="ada",
    )(c, w_ada, b_ada.reshape(1, n_out))


def _bias_table_kernel(bkt_ref, rbt_ref, o_ref):
    bkt = bkt_ref[...]
    k = lax.broadcasted_iota(jnp.int32, (N_BUCKETS, bkt.shape[1]), 0)
    onehot = jnp.where(k == bkt, 1.0, 0.0).astype(BF16)
    hi, mid, lo = _split3(rbt_ref[...])
    o_ref[...] = _dot(hi, onehot) + _dot(mid, onehot) + _dot(lo, onehot)


def _bias_table(bucket, rel_bias_t):
    n = bucket.shape[1]
    chunk = 8192
    n_heads = rel_bias_t.shape[0]
    return pl.pallas_call(
        _bias_table_kernel,
        grid=(n // chunk,),
        in_specs=[pl.BlockSpec((1, chunk), lambda j: (0, j)),
                  pl.BlockSpec(rel_bias_t.shape, lambda j: (0, 0))],
        out_specs=pl.BlockSpec((n_heads, chunk), lambda j: (0, j)),
        out_shape=jax.ShapeDtypeStruct((n_heads, n), F32),
        compiler_params=pltpu.CompilerParams(dimension_semantics=("parallel",),
                                             vmem_limit_bytes=VMEM_LIMIT),
        name="bias_table",
    )(bucket, rel_bias_t)


_C_FQ, _C_FK, _C_NQ = 0, 512, 1024
_C_K3 = 1536
_C_SK = 1920
_C_FV = 2176
_C_SV = 3200
_C_WV = 3456
_C_MISC = 3712
_IN_COLS = 3840
LOG2E = math.log2(math.e)


def _in_proj_kernel(x_ref, mod_ref, w_ref, bf_ref, ones_ref, fq_ref, fk_ref, nq_ref, kc_ref,
                    kw_ref, vc_ref, ks_ref, fv_ref, vs_ref, vw_ref, misc_ref, misct_ref,
                    carry_ref):
    s_idx = pl.program_id(1)
    tm = x_ref.shape[0]
    mod = mod_ref[...]
    h = (x_ref[...] * (1.0 + mod[1:2, :]) + mod[0:1, :]).astype(BF16)

    for ref, c0 in ((fq_ref, _C_FQ), (fk_ref, _C_FK), (nq_ref, _C_NQ)):
        ref[...] = _dot(h, w_ref[:, c0:c0 + 512]).astype(ref.dtype)
    for k, ref in enumerate((kc_ref, kw_ref, vc_ref)):
        c0 = _C_K3 + k * LANES
        ref[...] = _dot(h, w_ref[:, c0:c0 + LANES]).astype(ref.dtype)
    w_sk = ks_ref.shape[-1]
    key_blk = lax.shift_right_logical(
        s_idx * tm + lax.broadcasted_iota(jnp.int32, (tm, w_sk), 0), int(math.log2(SEL_BLOCK)))
    lane_in_slab = jnp.bitwise_and(lax.broadcasted_iota(jnp.int32, (tm, w_sk), 1), LANES - 1)
    blk_hot = jnp.where(lane_in_slab == HEAD_DIM + key_blk, 1.0, 0.0)
    ks_ref[...] = (_dot(h, w_ref[:, _C_SK:_C_SK + w_sk]) + blk_hot).astype(ks_ref.dtype)
    for ref, c0 in ((fv_ref, _C_FV), (vs_ref, _C_SV), (vw_ref, _C_WV)):
        c1 = c0 + ref.shape[-1]
        ref[...] = (_dot(h, w_ref[:, c0:c1]) + ones_ref[:, c0:c1]).astype(ref.dtype)

    z = _dot(h, w_ref[:, _C_MISC:_C_MISC + LANES]) + bf_ref[...]
    lane = lax.broadcasted_iota(jnp.int32, z.shape, 1)
    is_f = lane < FOX_HEADS
    log_f = jnp.minimum(z, 0.0) - jnp.log(1.0 + jnp.exp(-jnp.abs(z)))
    log_f = jnp.where(is_f, log_f, 0.0)

    row = lax.broadcasted_iota(jnp.int32, (tm, tm), 0)
    col = lax.broadcasted_iota(jnp.int32, (tm, tm), 1)
    tri = jnp.where(row >= col, 1.0, 0.0).astype(BF16)
    hi, mid, lo = _split3(log_f)
    local = _dot(tri, hi) + _dot(tri, mid) + _dot(tri, lo)

    @pl.when(s_idx == 0)
    def _():
        carry_ref[...] = jnp.zeros_like(carry_ref)

    cum = local + carry_ref[...]
    carry_ref[...] = cum[tm - 1:tm, :]
    misc = jnp.where(is_f, cum * LOG2E, _sigmoid(z))
    misc_ref[...] = misc
    misct_ref[...] = misc.T


def _in_proj(x, mod, w_r, bf_row, ones_row):
    B, S, D = x.shape
    tm = ROW_TILE
    widths = (512, 512, 512, LANES, LANES, LANES, _C_FV - _C_SK, _C_SV - _C_FV, _C_WV - _C_SV,
              _C_MISC - _C_WV)
    wide = lambda w: pl.BlockSpec((None, tm, w), lambda b, s: (b, s, 0))
    out_shape = ([jax.ShapeDtypeStruct((B, S, w), BF16) for w in widths]
                 + [jax.ShapeDtypeStruct((B, S, LANES), F32),
                    jax.ShapeDtypeStruct((B, LANES, S), F32)])
    out_specs = ([wide(w) for w in widths]
                 + [wide(LANES), pl.BlockSpec((None, LANES, tm), lambda b, s: (b, 0, s))])
    return pl.pallas_call(
        _in_proj_kernel,
        grid=(B, S // tm),
        in_specs=[pl.BlockSpec((None, tm, D), lambda b, s: (b, s, 0)),
                  pl.BlockSpec((None, 6, D), lambda b, s: (b, 0, 0)),
                  pl.BlockSpec((D, _IN_COLS), lambda b, s: (0, 0)),
                  pl.BlockSpec((1, LANES), lambda b, s: (0, 0)),
                  pl.BlockSpec((1, _IN_COLS), lambda b, s: (0, 0))],
        out_specs=out_specs,
        out_shape=out_shape,
        scratch_shapes=[pltpu.VMEM((1, LANES), F32)],
        compiler_params=pltpu.CompilerParams(dimension_semantics=("parallel", "arbitrary"),
                                             vmem_limit_bytes=VMEM_LIMIT),
        name="in_proj",
    )(x, mod, w_r, bf_row, ones_row)


def _softmax_weights(s, m):
    return jnp.exp2((s - m).astype(BF16))


def _flash_update(carry, s, vt):
    m, acc = carry
    m_new = jnp.maximum(m, jnp.max(s, axis=-1, keepdims=True))
    alpha = jnp.exp2(m - m_new)
    p = _softmax_weights(s, m_new)
    rows = acc.shape[0]
    acc = alpha.reshape(rows, 1) * acc + _dot(p.reshape(rows, s.shape[-1]), vt)
    return m_new, acc


def _normalize(acc):
    return acc / _swap_halves(acc)


def _fox_kernel(q_ref, k_ref, v_ref, ck_ref, o_ref):
    i = pl.program_id(2)
    tq = q_ref.shape[0]
    tk = ck_ref.shape[2]
    q2 = q_ref[...].astype(F32)
    lane = lax.broadcasted_iota(jnp.int32, (tq, LANES), 1)
    low = lane < HEAD_DIM
    halves = (low, jnp.logical_not(low))
    qh = [jnp.where(h, q2, 0.0).astype(BF16) for h in halves]
    col_minus_row = (lax.broadcasted_iota(jnp.int32, (tq, tk), 1)
                     - lax.broadcasted_iota(jnp.int32, (tq, tk), 0))

    def step(jj, carry, diagonal):
        k0 = pl.multiple_of(jj * tk, tk)
        kt = k_ref[pl.ds(k0, tk), :]
        new = []
        for hh in range(2):
            s = _dot_nt(qh[hh], kt) - ck_ref[hh, pl.ds(jj, 1), :]
            if diagonal:
                s = jnp.where(col_minus_row <= i * tq - jj * tk, s, NEG_BIG)
            vt = v_ref[pl.ds(k0, tk), hh * LANES:(hh + 1) * LANES]
            new.append(_flash_update(carry[hh], s, vt))
        return tuple(new)

    init = tuple((jnp.full((tq, 1), NEG_BIG, F32), jnp.zeros((tq, LANES), F32))
                 for _ in range(2))
    n_full = (i * tq) // tk
    carry = lax.fori_loop(0, n_full, lambda jj, c: step(jj, c, False), init)
    carry = step(n_full, carry, True)
    o_ref[...] = jnp.where(low, _normalize(carry[0][1]),
                           _swap_halves(_normalize(carry[1][1]))).astype(o_ref.dtype)


def _fox(fq, fk, fv, cum_row):
    B, S, W = fq.shape
    tq = tk = FOX_TILE
    n_pairs = W // LANES
    cum_row = cum_row.reshape(B, n_pairs, 2, S // tk, tk)
    return pl.pallas_call(
        _fox_kernel,
        grid=(B, n_pairs, S // tq),
        in_specs=[pl.BlockSpec((None, tq, LANES), lambda b, p, i: (b, i, p)),
                  pl.BlockSpec((None, S, LANES), lambda b, p, i: (b, 0, p)),
                  pl.BlockSpec((None, S, 2 * LANES), lambda b, p, i: (b, 0, p)),
                  pl.BlockSpec((None, None, 2, S // tk, tk), lambda b, p, i: (b, p, 0, 0, 0))],
        out_specs=pl.BlockSpec((None, tq, LANES), lambda b, p, i: (b, i, p)),
        out_shape=jax.ShapeDtypeStruct((B, S, W), BF16),
        compiler_params=pltpu.CompilerParams(
            dimension_semantics=("parallel", "parallel", "arbitrary"),
            vmem_limit_bytes=VMEM_LIMIT),
        name="fox",
    )(fq, fk, fv, cum_row)


def _compress_kernel(xk_ref, xv_ref, pk_ref, pv_ref, wak_ref, wbk_ref, wav_ref, wbv_ref,
                     w2k_ref, w2v_ref, ok_ref, ov_ref):
    n_rows = xk_ref.shape[0]
    for x_ref, p_ref, wa_ref, wb_ref, w2_ref, o_ref in (
            (xk_ref, pk_ref, wak_ref, wbk_ref, w2k_ref, ok_ref),
            (xv_ref, pv_ref, wav_ref, wbv_ref, w2v_ref, ov_ref)):
        x = x_ref[...].astype(F32)
        xa = (x + p_ref[0:1, :]).astype(BF16)
        xb = (x + p_ref[1:2, :]).astype(BF16)
        hb = _dot(xb, wb_ref[...])
        h1 = _dot(xa, wa_ref[...]) + pltpu.roll(hb, n_rows - 1, 0)
        o_ref[...] = _dot(_silu(h1).astype(BF16), w2_ref[...]).astype(o_ref.dtype)


def _compress(xk, xv, pk, pv, wak, wbk, wav, wbv, w2k, w2v):
    B, R, C = xk.shape
    xspec = pl.BlockSpec((None, R, C), lambda b: (b, 0, 0))
    full = lambda a: pl.BlockSpec(a.shape, lambda b: (0,) * a.ndim)
    ospec = pl.BlockSpec((None, R, LANES), lambda b: (b, 0, 0))
    return pl.pallas_call(
        _compress_kernel,
        grid=(B,),
        in_specs=[xspec, xspec] + [full(a) for a in (pk, pv, wak, wbk, wav, wbv, w2k, w2v)],
        out_specs=[ospec, ospec],
        out_shape=[jax.ShapeDtypeStruct((B, R, LANES), BF16)] * 2,
        compiler_params=pltpu.CompilerParams(dimension_semantics=("parallel",),
                                             vmem_limit_bytes=VMEM_LIMIT),
        name="compress",
    )(xk, xv, pk, pv, wak, wbk, wav, wbv, w2k, w2v)


def _rank_rows(score):
    n = score.shape[0]
    j = lax.broadcasted_iota(jnp.int32, score.shape, 0)
    rank = jnp.zeros(score.shape, jnp.int32)
    for i in range(n):
        si = score[i:i + 1, :]
        beats = (si > score) | ((si == score) & (j > i))
        rank = rank + jnp.where(beats, 1, 0)
    return rank


def _dup_head(q4, r):
    pair = q4[:, (r // 2) * LANES:(r // 2 + 1) * LANES].astype(F32)
    lane = lax.broadcasted_iota(jnp.int32, pair.shape, 1)
    swapped = _swap_halves(pair)
    if r % 2 == 0:
        return jnp.where(lane < HEAD_DIM, pair, swapped)
    return jnp.where(lane < HEAD_DIM, swapped, pair)


def _pack_heads(o_list, g):
    lane = lax.broadcasted_iota(jnp.int32, o_list[0].shape, 1)
    in_g = (lane >= g * HEAD_DIM) & (lane < (g + 1) * HEAD_DIM)
    both = []
    for o in o_list:
        om = jnp.where(in_g, o, 0.0)
        both.append(om + _swap_halves(om))
    pairs = [jnp.where(lane < HEAD_DIM, both[2 * p], both[2 * p + 1]) for p in range(2)]
    return jnp.concatenate(pairs, axis=1)


def _group_gates(misc, g):
    w = 3 * NSA_GQA
    gates = misc[:, FOX_HEADS:FOX_HEADS + w]
    for other in range(1, NSA_GROUPS):
        gates = jnp.where(g == other, misc[:, FOX_HEADS + other * w:FOX_HEADS + (other + 1) * w],
                          gates)
    return gates


def _cmp_sel_kernel(q_ref, kc_ref, vc_ref, bias_ref, gate_ref, ovl_ref, oc_ref, sel_ref):
    g = pl.program_id(1)
    i = pl.program_id(2)
    tq = q_ref.shape[0]
    n_pad = kc_ref.shape[0]
    q4 = q_ref[...]
    lane = lax.broadcasted_iota(jnp.int32, (n_pad, LANES), 1)
    in_g = (lane >= g * HEAD_DIM) & (lane < (g + 1) * HEAD_DIM)
    kc = jnp.where(in_g, kc_ref[...].astype(F32), 0.0).astype(BF16)
    vc = vc_ref[...]
    t = i * tq + lax.broadcasted_iota(jnp.int32, (tq, n_pad), 0)
    n = lax.broadcasted_iota(jnp.int32, (tq, n_pad), 1)
    valid = t >= n * CMP_STRIDE + (CMP_BLOCK - 1)
    gates = _group_gates(gate_ref[...], g)
    p_sum = jnp.zeros((tq, n_pad), F32)
    outs = []
    for r in range(NSA_GQA):
        qr = _dup_head(q4, r).astype(BF16)
        s = _dot_nt(qr, kc)
        s = jnp.where(valid, s + bias_ref[r], NEG_BIG)
        m = jnp.max(s, axis=-1, keepdims=True)
        p = jnp.exp2(s - m)
        p = p / jnp.sum(p, axis=-1, keepdims=True)
        p = jnp.where(valid, p, 0.0)
        p_sum = p_sum + p
        outs.append(_dot(p.astype(BF16), vc) * gates[:, 3 * r:3 * r + 1])
    oc_ref[...] = _pack_heads(outs, g).astype(oc_ref.dtype)

    ovl = ovl_ref[...]
    hi, mid, lo = _split3(p_sum)
    imp = _dot_nt(ovl, hi) + _dot_nt(ovl, mid) + _dot_nt(ovl, lo)
    n_blk = imp.shape[0]
    j = lax.broadcasted_iota(jnp.int32, (n_blk, tq), 0)
    qb = jnp.right_shift(i * tq + lax.broadcasted_iota(jnp.int32, (n_blk, tq), 1),
                         int(math.log2(SEL_BLOCK)))
    forced = (j == 0) | (j == qb) | (j == qb - 1)
    causal = j <= qb
    score = jnp.where(causal, imp + jnp.where(forced, FORCE_SCORE, 0.0), -FORCE_SCORE)
    chosen = (_rank_rows(score) < N_SEL) & causal
    sel = jnp.where(chosen, 1.0, 0.0)
    sel = jnp.concatenate([sel, jnp.zeros((LANES - n_blk, tq), F32)], axis=0)
    sel_ref[...] = sel.T


def _cmp_sel(nq, kcmp, vcmp, bias_c, gates_g, ovl_t):
    B, S, _ = nq.shape
    tq = ATT_TILE
    n_pad = kcmp.shape[1]
    return pl.pallas_call(
        _cmp_sel_kernel,
        grid=(B, NSA_GROUPS, S // tq),
        in_specs=[pl.BlockSpec((None, tq, 2 * LANES), lambda b, g, i: (b, i, g)),
                  pl.BlockSpec((None, n_pad, LANES), lambda b, g, i: (b, 0, 0)),
                  pl.BlockSpec((None, n_pad, LANES), lambda b, g, i: (b, 0, 0)),
                  pl.BlockSpec((NSA_GQA, tq, n_pad), lambda b, g, i: (g, i, 0)),
                  pl.BlockSpec((None, tq, LANES), lambda b, g, i: (b, i, 0)),
                  pl.BlockSpec(ovl_t.shape, lambda b, g, i: (0, 0))],
        out_specs=[pl.BlockSpec((None, tq, 2 * LANES), lambda b, g, i: (b, i, g)),
                   pl.BlockSpec((None, None, tq, LANES), lambda b, g, i: (b, g, i, 0))],
        out_shape=[jax.ShapeDtypeStruct((B, S, NSA_HEADS * HEAD_DIM), BF16),
                   jax.ShapeDtypeStruct((B, NSA_GROUPS, S, LANES), F32)],
        compiler_params=pltpu.CompilerParams(
            dimension_semantics=("parallel", "parallel", "arbitrary"),
            vmem_limit_bytes=VMEM_LIMIT),
        name="cmp_sel",
    )(nq, kcmp, vcmp, bias_c, gates_g, ovl_t)


def _nsa_kernel(q_ref, ks_ref, vs_ref, kw_ref, vw_ref, sel_ref, w4_ref, gate_ref, oc_ref,
                o_ref):
    g = pl.program_id(1)
    i = pl.program_id(2)
    tq = q_ref.shape[0]
    tk = w4_ref.shape[-1]
    H = NSA_GQA
    q4 = q_ref[...]
    lane = lax.broadcasted_iota(jnp.int32, (tq, LANES), 1)
    in_g = (lane >= g * HEAD_DIM) & (lane < (g + 1) * HEAD_DIM)
    heads = [_dup_head(q4, r) for r in range(H)]
    qs = jnp.concatenate([jnp.where(in_g, hd, 0.0).astype(BF16) for hd in heads], axis=0)

    blk_mask = _swap_halves((sel_ref[...] - 1.0) * (-NEG_BIG))
    qsel = jnp.concatenate(
        [jnp.where(lane < HEAD_DIM, hd, blk_mask).astype(BF16) for hd in heads], axis=0)

    def sel_step(jj, carry, diagonal):
        k0 = pl.multiple_of(jj * tk, tk)
        dd = jnp.minimum(i - 2 * jj, 3)
        s = (_dot_nt(qsel, ks_ref[pl.ds(k0, tk), :]).reshape(H, tq, tk)
             + w4_ref[:, pl.ds(dd, 1)].reshape(H, tq, tk))
        if diagonal:
            cmr = (lax.broadcasted_iota(jnp.int32, (tq, tk), 1)
                   - lax.broadcasted_iota(jnp.int32, (tq, tk), 0))
            s = jnp.where((cmr <= i * tq - jj * tk)[None], s, NEG_BIG)
        return _flash_update(carry, s, vs_ref[pl.ds(k0, tk), :])

    init = (jnp.full((H, tq, 1), NEG_BIG, F32), jnp.zeros((H * tq, LANES), F32))
    n_full = (i * tq) // tk
    carry = lax.fori_loop(0, n_full, lambda jj, c: sel_step(jj, c, False), init)
    _, acc_s = sel_step(n_full, carry, True)

    tiles = (jnp.maximum(i - 2, 0), jnp.maximum(i - 1, 0), i)
    starts = [pl.multiple_of(j * tq, tq) for j in tiles]
    kt = jnp.concatenate([kw_ref[pl.ds(st, tq), :] for st in starts], axis=0)
    vt = jnp.concatenate([vw_ref[pl.ds(st, tq), :] for st in starts], axis=0)
    bias_w = jnp.concatenate([w4_ref[:, 2], w4_ref[:, 0, :, 0:tq]], axis=-1)
    col = lax.broadcasted_iota(jnp.int32, (tq, 3 * tq), 1)
    cmr = col - lax.broadcasted_iota(jnp.int32, (tq, 3 * tq), 0)
    never = 4 * tq
    valid = (((col < tq) & (cmr > jnp.where(i >= 2, 0, never)))
             | ((col >= jnp.where(i >= 1, tq, never)) & (col < 2 * tq))
             | ((col >= 2 * tq) & (cmr <= 2 * tq)))
    s = _dot_nt(qs, kt).reshape(H, tq, 3 * tq) + bias_w
    s = jnp.where(valid[None], s, NEG_BIG)
    p = _softmax_weights(s, jnp.max(s, axis=-1, keepdims=True))
    acc_w = _dot(p.reshape(H * tq, 3 * tq), vt)

    gates = _group_gates(gate_ref[...], g)
    o_s = _normalize(acc_s)
    o_w = _normalize(acc_w)
    outs = []
    for r in range(H):
        sl = slice(r * tq, (r + 1) * tq)
        outs.append(o_s[sl] * gates[:, 3 * r + 1:3 * r + 2] + o_w[sl] * gates[:, 3 * r + 2:3 * r + 3])
    lane = lax.broadcasted_iota(jnp.int32, (tq, LANES), 1)
    pairs = [jnp.where(lane < HEAD_DIM, outs[2 * p], _swap_halves(outs[2 * p + 1]))
             for p in range(H // 2)]
    o_ref[...] = (jnp.concatenate(pairs, axis=1) + oc_ref[...].astype(F32)).astype(o_ref.dtype)


def _nsa(nq, ks, vs, kw, vw, sel, w4, misc, oc):
    B, S, _ = nq.shape
    tq = ATT_TILE
    tk = w4.shape[-1]
    both = lambda: pl.BlockSpec((None, S, LANES), lambda b, g, i: (b, 0, 0))
    mine = lambda: pl.BlockSpec((None, S, LANES), lambda b, g, i: (b, 0, g))
    return pl.pallas_call(
        _nsa_kernel,
        grid=(B, NSA_GROUPS, S // tq),
        in_specs=[pl.BlockSpec((None, tq, 2 * LANES), lambda b, g, i: (b, i, g)),
                  mine(), mine(), both(), mine(),
                  pl.BlockSpec((None, None, tq, LANES), lambda b, g, i: (b, g, i, 0)),
                  pl.BlockSpec((NSA_GQA, 4, tq, tk), lambda b, g, i: (g, 0, 0, 0)),
                  pl.BlockSpec((None, tq, LANES), lambda b, g, i: (b, i, 0)),
                  pl.BlockSpec((None, tq, 2 * LANES), lambda b, g, i: (b, i, g))],
        out_specs=pl.BlockSpec((None, tq, 2 * LANES), lambda b, g, i: (b, i, g)),
        out_shape=jax.ShapeDtypeStruct((B, S, NSA_HEADS * HEAD_DIM), BF16),
        compiler_params=pltpu.CompilerParams(
            dimension_semantics=("parallel", "parallel", "arbitrary"),
            vmem_limit_bytes=VMEM_LIMIT),
        name="nsa",
    )(nq, ks, vs, kw, vw, sel, w4, misc, oc)


def _layer_norm(y, g, b):
    mu = jnp.mean(y, axis=-1, keepdims=True)
    yc = y - mu
    var = jnp.mean(yc * yc, axis=-1, keepdims=True)
    return yc * lax.rsqrt(var + LN_EPS) * g + b


def _router_gates_t(h2, wr_t, eb_col):
    tm = h2.shape[0]
    h_parts = _split3(h2)
    w_parts = _split3(wr_t)
    logit = jnp.zeros((N_EXPERTS, tm), F32)
    for a, wp in enumerate(w_parts):
        for b, hp in enumerate(h_parts):
            if a + b <= 2:
                logit = logit + _dot_nt(wp, hp)
    scores = _sigmoid(logit)
    biased = scores + eb_col
    e_in = lax.broadcasted_iota(jnp.int32, (GROUP_SIZE, tm), 0).astype(F32)
    gs_rows = []
    for gi in range(N_EXPERT_GROUPS):
        grp = biased[gi * GROUP_SIZE:(gi + 1) * GROUP_SIZE, :]
        m1 = jnp.max(grp, axis=0, keepdims=True)
        first = jnp.min(jnp.where(grp == m1, e_in, float(GROUP_SIZE)), axis=0, keepdims=True)
        m2 = jnp.max(jnp.where(e_in == first, -jnp.inf, grp), axis=0, keepdims=True)
        gs_rows.append(m1 + m2)
    gscore = jnp.concatenate(gs_rows, axis=0)
    g_keep = _rank_rows(gscore) < TOPK_GROUPS
    keep = jnp.concatenate(
        [jnp.broadcast_to(g_keep[gi:gi + 1, :], (GROUP_SIZE, tm)) for gi in range(N_EXPERT_GROUPS)],
        axis=0)
    masked = jnp.where(keep, biased, -jnp.inf)
    chosen = _rank_rows(masked) < TOP_K
    w = jnp.where(chosen, scores, 0.0)
    return w / jnp.sum(w, axis=0, keepdims=True) * ROUTED_SCALE


def _out_proj_kernel(alpha, of_ref, on_ref, x_ref, mod_ref, w_ref, lg_ref, lb_ref, wr_ref,
                     eb_ref, x1_ref, h2_ref, gate_ref, gate_t_ref):
    half = of_ref.shape[1]
    mod = mod_ref[...]
    mixed = _dot(of_ref[...], w_ref[0:half, :]) + _dot(on_ref[...], w_ref[half:2 * half, :])
    y = alpha * x_ref[...] + mod[2:3, :] * mixed
    x1 = _layer_norm(y, lg_ref[...], lb_ref[...])
    x1_ref[...] = x1
    h2 = x1 * (1.0 + mod[4:5, :]) + mod[3:4, :]
    h2_ref[...] = h2.astype(h2_ref.dtype)
    gates_t = _router_gates_t(h2, wr_ref[...], eb_ref[...])
    gate_t_ref[...] = gates_t
    tm = h2.shape[0]
    gates_t = jnp.concatenate([gates_t, jnp.zeros((LANES - N_EXPERTS, tm), F32)], axis=0)
    gate_ref[...] = gates_t.T


def _out_proj(alpha, o_fox, o_nsa, x, mod, w_out, ln_g, ln_b, wr_t, eb_col):
    B, S, D = x.shape
    tm = ROW_TILE
    half = o_fox.shape[-1]
    row = lambda a: pl.BlockSpec(a.shape, lambda b, s: (0, 0))
    return pl.pallas_call(
        functools.partial(_out_proj_kernel, alpha),
        grid=(B, S // tm),
        in_specs=[pl.BlockSpec((None, tm, half), lambda b, s: (b, s, 0)),
                  pl.BlockSpec((None, tm, half), lambda b, s: (b, s, 0)),
                  pl.BlockSpec((None, tm, D), lambda b, s: (b, s, 0)),
                  pl.BlockSpec((None, 6, D), lambda b, s: (b, 0, 0)),
                  row(w_out), row(ln_g), row(ln_b), row(wr_t), row(eb_col)],
        out_specs=[pl.BlockSpec((None, tm, D), lambda b, s: (b, s, 0)),
                   pl.BlockSpec((None, tm, D), lambda b, s: (b, s, 0)),
                   pl.BlockSpec((None, tm, LANES), lambda b, s: (b, s, 0)),
                   pl.BlockSpec((N_EXPERTS, tm), lambda b, s: (0, b * (S // tm) + s))],
        out_shape=[jax.ShapeDtypeStruct((B, S, D), F32),
                   jax.ShapeDtypeStruct((B, S, D), BF16),
                   jax.ShapeDtypeStruct((B, S, LANES), F32),
                   jax.ShapeDtypeStruct((N_EXPERTS, B * S), F32)],
        compiler_params=pltpu.CompilerParams(dimension_semantics=("parallel", "parallel"),
                                             vmem_limit_bytes=VMEM_LIMIT),
        name="out_proj",
    )(o_fox, o_nsa, x, mod, w_out, ln_g, ln_b, wr_t, eb_col)


SORT_TILE = 256
ROW_ALIGN = 16
EXP_TILE = 1024
EXP_STRIP = 256
P_CHUNK = 256


def _strict_upper(n):
    return jnp.where(lax.broadcasted_iota(jnp.int32, (n, n), 0)
                     < lax.broadcasted_iota(jnp.int32, (n, n), 1), 1.0, 0.0).astype(BF16)


def _strict_lower(n):
    return jnp.where(lax.broadcasted_iota(jnp.int32, (n, n), 1)
                     < lax.broadcasted_iota(jnp.int32, (n, n), 0), 1.0, 0.0).astype(BF16)


def _local_rows_bound(ts):
    rows = TOP_K * ts + N_EXPERTS * (ROW_ALIGN - 1)
    return -(-rows // P_CHUNK) * P_CHUNK


def _piece_cols(ts):
    return -(-(_local_rows_bound(ts) // ROW_ALIGN) // LANES) * LANES


def _sorted_tiles_bound(T):
    rows = TOP_K * T + (T // SORT_TILE) * N_EXPERTS * (ROW_ALIGN - 1)
    return -(-rows // EXP_TILE) + N_EXPERTS


def _moe_meta_kernel(gt_ref, ptab_ref, loc_et_ref, np_et_ref, loc_te_ref, np_te_ref, tot_ref,
                     erow_ref, texp_ref, nused_ref):
    E, T = gt_ref.shape
    mask = jnp.where(gt_ref[...] > 0.0, 1.0, 0.0).astype(BF16)
    t_id = lax.shift_right_logical(lax.broadcasted_iota(jnp.int32, (T, LANES), 0),
                                   int(math.log2(SORT_TILE)))
    tind = jnp.where(t_id == lax.broadcasted_iota(jnp.int32, (T, LANES), 1), 1.0, 0.0)
    cnt = _dot(mask, tind.astype(BF16))
    n16 = jnp.floor((cnt + (ROW_ALIGN - 1.0)) * (1.0 / ROW_ALIGN))
    n16b = n16.astype(BF16)
    q = EXP_TILE // ROW_ALIGN
    len16 = jnp.sum(n16, axis=1, keepdims=True)
    pad16 = jnp.floor((len16 + (q - 1.0)) * (1.0 / q)) * q
    sl = _strict_lower(E)
    hi, mid, lo = _split3(jnp.broadcast_to(pad16, (E, LANES)))
    start16 = _dot(sl, hi) + _dot(sl, mid) + _dot(sl, lo)
    gdst16 = start16 + _dot(n16b, _strict_upper(LANES))
    loc16 = _dot(sl, n16b)

    def t(a):
        return jnp.concatenate([a, jnp.zeros((LANES - E, LANES), F32)], axis=0).T

    scale = float(ROW_ALIGN)
    loc_et_ref[...] = loc16 * scale
    np_et_ref[...] = n16 * scale
    loc_te_ref[...] = t(loc16) * scale
    np_te_ref[...] = t(n16) * scale
    tot_ref[...] = (jnp.sum(n16, axis=0, keepdims=True) * scale).astype(jnp.int32)

    n_t, n_blk = ptab_ref.shape
    blk = lax.broadcasted_iota(jnp.int32, (E, n_blk), 1).astype(F32)
    for tile in range(n_t):
        lo_c = loc16[:, tile:tile + 1]
        inside = (lo_c <= blk) & (blk < lo_c + n16[:, tile:tile + 1])
        dst = jnp.sum(jnp.where(inside, gdst16[:, tile:tile + 1] + (blk - lo_c), 0.0),
                      axis=0, keepdims=True)
        ptab_ref[tile:tile + 1, :] = (dst * scale).astype(jnp.int32)
    ends = jnp.concatenate([t(start16 + len16)[0:1, :], t(start16 + pad16)[0:1, :],
                            jnp.zeros((erow_ref.shape[0] - 2, LANES), F32)], axis=0)
    erow_ref[...] = (ends * scale).astype(jnp.int32)
    n_tab = texp_ref.shape[1]
    tile_row16 = (lax.broadcasted_iota(jnp.int32, (E, n_tab), 1) * q).astype(F32)
    owner = jnp.sum(jnp.where(start16[:, 0:1] <= tile_row16, 1.0, 0.0), axis=0, keepdims=True)
    texp_ref[...] = (owner - 1.0).astype(jnp.int32)
    n_used = jnp.sum(pad16, axis=0, keepdims=True) * (1.0 / q)
    nused_ref[...] = jnp.broadcast_to(n_used, (1, LANES)).astype(jnp.int32)


def _moe_meta(gates_t, n_tab):
    E, T = gates_t.shape
    i32 = jnp.int32
    return pl.pallas_call(
        _moe_meta_kernel,
        out_shape=[jax.ShapeDtypeStruct((T // SORT_TILE, _piece_cols(SORT_TILE)), i32),
                   jax.ShapeDtypeStruct((E, LANES), F32),
                   jax.ShapeDtypeStruct((E, LANES), F32),
                   jax.ShapeDtypeStruct((LANES, LANES), F32),
                   jax.ShapeDtypeStruct((LANES, LANES), F32),
                   jax.ShapeDtypeStruct((1, LANES), i32),
                   jax.ShapeDtypeStruct((8, LANES), i32),
                   jax.ShapeDtypeStruct((1, n_tab), i32),
                   jax.ShapeDtypeStruct((1, LANES), i32)],
        compiler_params=pltpu.CompilerParams(vmem_limit_bytes=VMEM_LIMIT),
        name="moe_meta",
    )(gates_t)


def _start_pieces(tile, ptab_s, tot_s, n_cols, make_copy):
    n_pieces = lax.shift_right_logical(tot_s[tile], int(math.log2(ROW_ALIGN)))

    def body(b, carry):
        make_copy(pl.multiple_of(b * ROW_ALIGN, ROW_ALIGN),
                  pl.multiple_of(ptab_s[tile * n_cols + b], ROW_ALIGN)).start()
        return carry

    lax.fori_loop(0, n_pieces, body, 0)


def _moe_sort_kernel(ptab_s, tot_s, lend_s, rend_s, gt_ref, h_ref, locrow_ref, nprow_ref,
                     xs_hbm, buf, zbuf, sem, zsem):
    tau = pl.program_id(0)
    n_t = pl.num_programs(0)
    slot = lax.rem(tau, 2)
    E, ts = gt_ref.shape
    D = h_ref.shape[1]
    n_cols = _piece_cols(ts)

    def copies(tile, sl, wait):
        if wait:
            rows = pl.multiple_of(tot_s[tile], ROW_ALIGN)

            @pl.when(rows > 0)
            def _():
                pltpu.make_async_copy(buf.at[sl, pl.ds(0, rows)], xs_hbm.at[pl.ds(0, rows)],
                                      sem.at[sl]).wait()
        else:
            def make_copy(loc, dst):
                return pltpu.make_async_copy(buf.at[sl, pl.ds(loc, ROW_ALIGN)],
                                             xs_hbm.at[pl.ds(dst, ROW_ALIGN)], sem.at[sl])
            _start_pieces(tile, ptab_s, tot_s, n_cols, make_copy)

    z_rows = zbuf.shape[0]
    used_rows = rend_s[E - 1]
    n_spare = (xs_hbm.shape[0] - used_rows) // z_rows

    def spare_fill(wait):
        def body(c, carry):
            dst = pl.multiple_of(used_rows + c * z_rows, z_rows)
            cp = pltpu.make_async_copy(zbuf, xs_hbm.at[pl.ds(dst, z_rows)], zsem.at[1])
            if wait:
                cp.wait()
            else:
                cp.start()
            return carry

        lax.fori_loop(0, n_spare, body, 0)

    @pl.when(tau == 0)
    def _():
        zbuf[...] = jnp.zeros_like(zbuf)
        spare_fill(False)

    @pl.when(tau >= 2)
    def _():
        copies(tau - 2, slot, True)

    g = gt_ref[...]
    mask = g > 0.0
    maskb = jnp.where(mask, 1.0, 0.0).astype(BF16)
    pad = jnp.zeros((LANES - E, ts), F32)
    pos = jnp.where(mask, _dot(maskb, _strict_upper(ts)), -1.0)
    pos = jnp.concatenate([pos, pad], axis=0).astype(BF16)
    g_parts = _split3(jnp.concatenate([g, pad], axis=0))
    lo_row = locrow_ref[...]
    hi_row = lo_row + nprow_ref[...]
    h = h_ref[...]
    n_chunks = lax.shift_right_logical(tot_s[tau] + (P_CHUNK - 1), int(math.log2(P_CHUNK)))
    lane = lax.broadcasted_iota(jnp.int32, (P_CHUNK, LANES), 1)

    def chunk(c):
        r0 = pl.multiple_of(c * P_CHUNK, P_CHUNK)
        r = (r0 + lax.broadcasted_iota(jnp.int32, (P_CHUNK, LANES), 0)).astype(F32)
        inside = (lo_row <= r) & (r < hi_row)
        group = jnp.where(inside, 1.0, 0.0).astype(BF16)
        want = r[:, 0:1] - jnp.sum(jnp.where(inside, lo_row, 0.0), axis=1, keepdims=True)
        hit = _dot(group, pos) == want
        buf[slot, pl.ds(r0, P_CHUNK), 0:D] = _dot(jnp.where(hit, 1.0, 0.0).astype(BF16),
                                                  h).astype(buf.dtype)
        extra = jnp.zeros((P_CHUNK, LANES), F32)
        for k, part in enumerate(g_parts):
            term = jnp.sum(jnp.where(hit, _dot(group, part), 0.0), axis=1, keepdims=True)
            extra = jnp.where(lane == k, term, extra)
        buf[slot, pl.ds(r0, P_CHUNK), D:D + LANES] = extra.astype(buf.dtype)

    def pair_body(c, carry):
        chunk(2 * c)
        chunk(2 * c + 1)
        return carry

    lax.fori_loop(0, lax.shift_right_logical(n_chunks + 1, 1), pair_body, 0)
    copies(tau, slot, False)

    @pl.when(tau == n_t - 1)
    def _():
        @pl.when(n_t >= 2)
        def _():
            copies(tau - 1, 1 - slot, True)
        copies(tau, slot, True)
        spare_fill(True)

        sizes =[zbuf.shape[0] >> s for s in range(int(math.log2(zbuf.shape[0] // ROW_ALIGN)) + 1)]

        def fill(wait):
            def e_body(e, carry):
                start = lend_s[e]
                n = rend_s[e] - start
                off = start
                for size in sizes:
                    bit = jnp.bitwise_and(n, size)

                    @pl.when(bit != 0)
                    def _(off=off, size=size):
                        cp = pltpu.make_async_copy(
                            zbuf.at[pl.ds(0, size)],
                            xs_hbm.at[pl.ds(pl.multiple_of(off, ROW_ALIGN), size)], zsem.at[0])
                        if wait:
                            cp.wait()
                        else:
                            cp.start()

                    off = off + bit
                return carry

            lax.fori_loop(0, E, e_body, 0)

        fill(False)
        fill(True)


def _moe_sort(ptab, tot, lend, rend, gates_t, h2, loc_te, np_te, n_rows):
    E, T = gates_t.shape
    D = h2.shape[1]
    ts = SORT_TILE
    grid_spec = pltpu.PrefetchScalarGridSpec(
        num_scalar_prefetch=4,
        grid=(T // ts,),
        in_specs=[pl.BlockSpec((E, ts), lambda t, *_: (0, t)),
                  pl.BlockSpec((ts, D), lambda t, *_: (t, 0)),
                  pl.BlockSpec((None, 1, LANES), lambda t, *_: (t, 0, 0)),
                  pl.BlockSpec((None, 1, LANES), lambda t, *_: (t, 0, 0))],
        out_specs=pl.BlockSpec(memory_space=pl.ANY),
        scratch_shapes=[pltpu.VMEM((2, _local_rows_bound(ts), D + LANES), BF16),
                        pltpu.VMEM((EXP_TILE // 2, D + LANES), BF16),
                        pltpu.SemaphoreType.DMA((2,)),
                        pltpu.SemaphoreType.DMA((2,))])
    return pl.pallas_call(
        _moe_sort_kernel,
        grid_spec=grid_spec,
        out_shape=jax.ShapeDtypeStruct((n_rows, D + LANES), BF16),
        compiler_params=pltpu.CompilerParams(dimension_semantics=("arbitrary",),
                                             vmem_limit_bytes=VMEM_LIMIT),
        name="moe_sort",
    )(ptab, tot, lend, rend, gates_t, h2, loc_te, np_te)


def _moe_expert_kernel(texp_s, nused_s, x_ref, wg_ref, wu_ref, wd_ref, y_ref, wgu_s, wd_s):
    i = pl.program_id(0)
    f = wd_ref.shape[0]

    @pl.when(i < nused_s[0])
    def _():
        @pl.when((i == 0) | (texp_s[i] != texp_s[jnp.maximum(i - 1, 0)]))
        def _():
            wgu_s[:, 0:f] = wg_ref[...].astype(BF16)
            wgu_s[:, f:2 * f] = wu_ref[...].astype(BF16)
            wd_s[...] = wd_ref[...].astype(BF16)

        d = wd_ref.shape[1]
        for r0 in range(0, x_ref.shape[0], EXP_STRIP):
            rows = slice(r0, r0 + EXP_STRIP)
            gate = jnp.sum(x_ref[rows, d:].astype(F32), axis=1, keepdims=True)
            a = _dot(x_ref[rows, 0:d], wgu_s[...])
            act = _silu(a[:, :f]) * a[:, f:] * gate
            y_ref[rows, :] = _dot(act.astype(BF16), wd_s[...]).astype(y_ref.dtype)

    @pl.when(i >= nused_s[0])
    def _():
        y_ref[...] = jnp.zeros_like(y_ref)


def _moe_expert(texp, nused, xs, w_gate, w_up, w_down, n_tiles):
    n_rows, xw = xs.shape
    D, f = w_gate.shape[-2:]
    tm = EXP_TILE

    def tile(i, texp, nused):
        return jnp.maximum(jnp.minimum(i, nused[0] - 1), 0)

    grid_spec = pltpu.PrefetchScalarGridSpec(
        num_scalar_prefetch=2,
        grid=(n_tiles,),
        in_specs=[pl.BlockSpec((tm, xw), lambda i, te, nu: (tile(i, te, nu), 0)),
                  pl.BlockSpec((None, D, f), lambda i, te, nu: (te[tile(i, te, nu)], 0, 0)),
                  pl.BlockSpec((None, D, f), lambda i, te, nu: (te[tile(i, te, nu)], 0, 0)),
                  pl.BlockSpec((None, f, D), lambda i, te, nu: (te[tile(i, te, nu)], 0, 0))],
        out_specs=pl.BlockSpec((tm, D), lambda i, te, nu: (i, 0)),
        scratch_shapes=[pltpu.VMEM((D, 2 * f), BF16), pltpu.VMEM((f, D), BF16)])
    return pl.pallas_call(
        _moe_expert_kernel,
        grid_spec=grid_spec,
        out_shape=jax.ShapeDtypeStruct((n_rows, D), BF16),
        compiler_params=pltpu.CompilerParams(dimension_semantics=("arbitrary",),
                                             vmem_limit_bytes=VMEM_LIMIT),
        name="moe_expert",
    )(texp, nused, xs, w_gate, w_up, w_down)


def _moe_combine_kernel(alpha, ptab_s, tot_s, g_ref, loc_ref, np_ref, h_ref, x1_ref,
                        mod_ref, sgu_ref, sd_ref, lg_ref, lb_ref, y_hbm, o_ref, ybuf, acc_ref,
                        sem):
    tau = pl.program_id(0)
    n_t = pl.num_programs(0)
    slot = lax.rem(tau, 2)
    ts, n_lane = g_ref.shape
    E = loc_ref.shape[0]
    n_cols = _piece_cols(ts)

    def copies(tile, sl, wait):
        if wait:
            rows = pl.multiple_of(tot_s[tile], ROW_ALIGN)

            @pl.when(rows > 0)
            def _():
                pltpu.make_async_copy(y_hbm.at[pl.ds(0, rows)], ybuf.at[sl, pl.ds(0, rows)],
                                      sem.at[sl]).wait()
        else:
            def make_copy(loc, dst):
                return pltpu.make_async_copy(y_hbm.at[pl.ds(dst, ROW_ALIGN)],
                                             ybuf.at[sl, pl.ds(loc, ROW_ALIGN)], sem.at[sl])
            _start_pieces(tile, ptab_s, tot_s, n_cols, make_copy)

    @pl.when(tau == 0)
    def _():
        ybuf[...] = jnp.zeros_like(ybuf)
        copies(0, 0, False)

    @pl.when(tau + 1 < n_t)
    def _():
        copies(tau + 1, 1 - slot, False)

    g = g_ref[...]
    mask = g > 0.0
    maskb = jnp.where(mask, 1.0, 0.0).astype(BF16)
    pos = jnp.where(mask, _dot(_strict_lower(ts), maskb), -1.0).astype(BF16)
    lane = lax.broadcasted_iota(jnp.int32, loc_ref.shape, 1)
    lo_col = jnp.sum(jnp.where(lane == tau, loc_ref[...], 0.0), axis=1, keepdims=True)
    hi_col = lo_col + jnp.sum(jnp.where(lane == tau, np_ref[...], 0.0), axis=1, keepdims=True)

    f = sd_ref.shape[0]
    a = _dot(h_ref[...], sgu_ref[...])
    acc_ref[...] = _dot((_silu(a[:, :f]) * a[:, f:]).astype(BF16), sd_ref[...])

    copies(tau, slot, True)
    n_chunks = lax.shift_right_logical(tot_s[tau] + (P_CHUNK - 1), int(math.log2(P_CHUNK)))

    def chunk(c):
        r0 = pl.multiple_of(c * P_CHUNK, P_CHUNK)
        r = (r0 + lax.broadcasted_iota(jnp.int32, (E, P_CHUNK), 1)).astype(F32)
        inside = (lo_col <= r) & (r < hi_col)
        group = jnp.concatenate([jnp.where(inside, 1.0, 0.0),
                                 jnp.zeros((n_lane - E, P_CHUNK), F32)], axis=0).astype(BF16)
        want = r[0:1, :] - jnp.sum(jnp.where(inside, lo_col, 0.0), axis=0, keepdims=True)
        hit = _dot(pos, group) == want
        return _dot(jnp.where(hit, 1.0, 0.0).astype(BF16), ybuf[slot, pl.ds(r0, P_CHUNK), :])

    def pair_body(c, carry):
        acc_ref[...] += chunk(2 * c) + chunk(2 * c + 1)
        return carry

    lax.fori_loop(0, lax.shift_right_logical(n_chunks + 1, 1), pair_body, 0)
    y = alpha * x1_ref[...] + mod_ref[5:6, :] * acc_ref[...]
    o_ref[...] = _layer_norm(y, lg_ref[...], lb_ref[...])


def _moe_combine(alpha, ptab, tot, gates, loc_et, np_et, h2, x1, mod, sgu, sd, ln_g, ln_b, ys,
                 S):
    T, D = h2.shape
    ts = SORT_TILE
    per_b = S // ts
    row = lambda a: pl.BlockSpec(a.shape, lambda t, *_: (0, 0))
    grid_spec = pltpu.PrefetchScalarGridSpec(
        num_scalar_prefetch=2,
        grid=(T // ts,),
        in_specs=[pl.BlockSpec((ts, LANES), lambda t, *_: (t, 0)),
                  row(loc_et), row(np_et),
                  pl.BlockSpec((ts, D), lambda t, *_: (t, 0)),
                  pl.BlockSpec((ts, D), lambda t, *_: (t, 0)),
                  pl.BlockSpec((None, 6, D), lambda t, *_: (t // per_b, 0, 0)),
                  row(sgu), row(sd), row(ln_g), row(ln_b),
                  pl.BlockSpec(memory_space=pl.ANY)],
        out_specs=pl.BlockSpec((ts, D), lambda t, *_: (t, 0)),
        scratch_shapes=[pltpu.VMEM((2, _local_rows_bound(ts), D), BF16),
                        pltpu.VMEM((ts, D), F32),
                        pltpu.SemaphoreType.DMA((2,))])
    return pl.pallas_call(
        functools.partial(_moe_combine_kernel, alpha),
        grid_spec=grid_spec,
        out_shape=jax.ShapeDtypeStruct((T, D), F32),
        compiler_params=pltpu.CompilerParams(dimension_semantics=("arbitrary",),
                                             vmem_limit_bytes=VMEM_LIMIT),
        name="moe_combine",
    )(ptab, tot, gates, loc_et, np_et, h2, x1, mod, sgu, sd, ln_g, ln_b, ys)


def _moe(alpha, h2, x1, gates, gates_t, mod, w_gate, w_up, w_down, sgu, sd, ln_g, ln_b):
    B, S, D = x1.shape
    T = B * S
    n_t = T // SORT_TILE
    n_tiles = _sorted_tiles_bound(T)
    n_tab = -(-n_tiles // LANES) * LANES
    ptab, loc_et, np_et, loc_te, np_te, tot, erow, texp, nused = _moe_meta(gates_t, n_tab)
    ptab = ptab.reshape(-1)
    tot = tot[0, :n_t]
    h2 = h2.reshape(T, D)
    xs = _moe_sort(ptab, tot, erow[0, :N_EXPERTS], erow[1, :N_EXPERTS], gates_t, h2,
                   loc_te[:n_t].reshape(n_t, 1, LANES), np_te[:n_t].reshape(n_t, 1, LANES),
                   n_tiles * EXP_TILE)
    ys = _moe_expert(texp[0], nused[0, :1], xs, w_gate, w_up, w_down, n_tiles)
    out = _moe_combine(alpha, ptab, tot, gates.reshape(T, LANES), loc_et, np_et, h2,
                       x1.reshape(T, D), mod, sgu, sd, ln_g, ln_b, ys, S)
    return out.reshape(B, S, D)


def _rearrange_w_in(w):
    d_in = w.shape[0]
    scale = HEAD_DIM ** -0.5 * LOG2E
    fq, fk, fv = w[:, 0:512], w[:, 512:1024], w[:, 1024:1536]
    ff = w[:, 1536:1544]
    nq = w[:, 1544:2056]
    kc, vc, ks, vs, kw, vw = (w[:, 2056 + k * LANES:2056 + (k + 1) * LANES] for k in range(6))
    ng = w[:, 2824:2848]
    pad = jnp.zeros((d_in, LANES - ff.shape[1] - ng.shape[1]), w.dtype)

    def spaced(v):
        v = v.reshape(d_in, -1, HEAD_DIM)
        return jnp.concatenate([v, jnp.zeros_like(v)], axis=2).reshape(d_in, -1)

    cols = [fq * scale, fk, nq * scale, kc, kw, vc, spaced(ks), spaced(fv), spaced(vs), spaced(vw),
            ff, ng, pad]
    return jnp.concatenate(cols, axis=1).astype(BF16)


def _ones_row():
    row = np.zeros((1, _IN_COLS), np.float32)
    lanes = np.arange(_C_FV, _C_MISC)
    row[0, lanes[(lanes % LANES) >= HEAD_DIM]] = 1.0
    return row


def _compress_weights(pos, w1, w2):
    half = CMP_BLOCK // 2
    w1r = w1.reshape(2, half, HEAD_DIM, CMP_HIDDEN)
    zeros = jnp.zeros_like(w1r[0])
    def spread(part):
        g0 = jnp.stack([part, zeros], axis=1).reshape(half * 2 * HEAD_DIM, CMP_HIDDEN)
        g1 = jnp.stack([zeros, part], axis=1).reshape(half * 2 * HEAD_DIM, CMP_HIDDEN)
        return jnp.concatenate([g0, g1], axis=1).astype(BF16)
    wa, wb = spread(w1r[0]), spread(w1r[1])
    z2 = jnp.zeros_like(w2)
    w2bd = jnp.concatenate([jnp.concatenate([w2, z2], axis=1),
                            jnp.concatenate([z2, w2], axis=1)], axis=0).astype(BF16)
    posr = pos.reshape(2, half, 1, HEAD_DIM)
    posr = jnp.broadcast_to(posr, (2, half, NSA_GROUPS, HEAD_DIM)).reshape(2, half * 2 * HEAD_DIM)
    return posr, wa, wb, w2bd


@functools.lru_cache(maxsize=None)
def _static_tables(S):
    tq = ATT_TILE
    n_cmp = (S - CMP_BLOCK) // CMP_STRIDE + 1
    n_pad = S // CMP_STRIDE
    n_slc = S // SEL_BLOCK
    t = np.arange(S)[:, None]
    n = np.arange(n_pad)[None, :]
    bucket_c = _t5_bucket_np(t - (n * CMP_STRIDE + CMP_BLOCK - 1)).reshape(1, -1)
    d = (np.arange(4)[:, None, None] * tq + np.arange(tq)[None, :, None]
         - np.arange(2 * tq)[None, None, :])
    bucket_w = _t5_bucket_np(d).reshape(1, -1)
    cs = np.arange(n_pad)[None, :] * CMP_STRIDE
    sj = np.arange(n_slc)[:, None] * SEL_BLOCK
    ovl_t = ((cs < sj + SEL_BLOCK) & (cs + CMP_BLOCK > sj) & (np.arange(n_pad)[None, :] < n_cmp))
    return bucket_c, bucket_w, ovl_t.astype(np.float32)


def kernel(x, c, w_ada, b_ada, w_in, b_f, cmp_pos_k, cmp_w1_k, cmp_w2_k, cmp_pos_v, cmp_w1_v,
           cmp_w2_v, rel_bias, w_out, ln1_g, ln1_b, w_router, e_bias, w_gate, w_up, w_down,
           ws_gate, ws_up, ws_down, ln2_g, ln2_b):
    B, S, D = x.shape
    depth = w_ada.shape[0]
    alpha = (2 * depth) ** 0.25
    tq = ATT_TILE
    bucket_c, bucket_w, ovl_t = _static_tables(S)
    rel_bias_t = rel_bias.T * LOG2E
    bias_c = _bias_table(jnp.asarray(bucket_c), rel_bias_t).reshape(NSA_HEADS, S, S // CMP_STRIDE)
    w4 = _bias_table(jnp.asarray(bucket_w), rel_bias_t).reshape(NSA_HEADS, 4, tq, 2 * tq)
    ovl_t = jnp.asarray(ovl_t, BF16)

    for l in range(depth):
        mod = _ada(c, w_ada[l], b_ada[l]).reshape(B, 6, D)
        bf_row = jnp.zeros((1, LANES), F32).at[0, :FOX_HEADS].set(b_f[l])
        (fq, fk, nq, kc, kw, vc, ks, fv, vs, vw, misc, misc_t) = _in_proj(
            x, mod, _rearrange_w_in(w_in[l]), bf_row, jnp.asarray(_ones_row()))

        o_fox = _fox(fq, fk, fv, misc_t[:, :FOX_HEADS, :])

        pk, wak, wbk, w2k = _compress_weights(cmp_pos_k[l], cmp_w1_k[l], cmp_w2_k[l])
        pv, wav, wbv, w2v = _compress_weights(cmp_pos_v[l], cmp_w1_v[l], cmp_w2_v[l])
        rows = S // CMP_STRIDE
        kcmp, vcmp = _compress(kc.reshape(B, rows, CMP_STRIDE * LANES),
                               vc.reshape(B, rows, CMP_STRIDE * LANES),
                               pk, pv, wak, wbk, wav, wbv, w2k, w2v)

        oc, sel = _cmp_sel(nq, kcmp, vcmp, bias_c, misc, ovl_t)
        o_nsa = _nsa(nq, ks, vs, kw, vw, sel, w4, misc, oc)

        x1, h2, gates, gates_t = _out_proj(
            alpha, o_fox, o_nsa, x, mod, w_out[l].astype(BF16), ln1_g[l].reshape(1, D),
            ln1_b[l].reshape(1, D), w_router[l].T, e_bias[l].reshape(N_EXPERTS, 1))

        sgu = jnp.concatenate([ws_gate[l], ws_up[l]], axis=-1).astype(BF16)
        x = _moe(alpha, h2, x1, gates, gates_t, mod, w_gate[l], w_up[l], w_down[l], sgu,
                 ws_down[l].astype(BF16), ln2_g[l].reshape(1, D), ln2_b[l].reshape(1, D))
    return x
```

```python
import functools
import math

import jax
import jax.numpy as jnp
import numpy as np
from jax import lax
from jax.experimental import pallas as pl
from jax.experimental.pallas import tpu as pltpu

F32 = jnp.float32
BF16 = jnp.bfloat16

HEAD_DIM = 64
FOX_HEADS = 8
NSA_HEADS = 8
NSA_GQA = 4
NSA_GROUPS = NSA_HEADS // NSA_GQA
CMP_BLOCK = 32
CMP_STRIDE = 16
CMP_HIDDEN = 256
SEL_BLOCK = 64
N_SEL = 16
WINDOW = 512
N_BUCKETS = 32
MAX_DISTANCE = 128
N_EXPERTS = 64
N_EXPERT_GROUPS = 8
GROUP_SIZE = N_EXPERTS // N_EXPERT_GROUPS
TOPK_GROUPS = 4
TOP_K = 8
D_EXPERT = 256
ROUTED_SCALE = 2.5
LN_EPS = 1e-5
NEG_BIG = -1e30
FORCE_SCORE = 1e4

LANES = 128
ATT_TILE = 256
FOX_TILE = 512
CMP_TILE = 512
ROW_TILE = 512
MOE_TILE = 1024
VMEM_LIMIT = 48 * 1024 * 1024

NT_DIMS = (((1,), (1,)), ((), ()))


def _dot(a, b):
    return jnp.dot(a, b, preferred_element_type=F32)


def _dot_nt(a, b):
    return lax.dot_general(a, b, NT_DIMS, preferred_element_type=F32)


def _split3(x):
    hi = x.astype(BF16)
    r1 = x - hi.astype(F32)
    mid = r1.astype(BF16)
    lo = (r1 - mid.astype(F32)).astype(BF16)
    return hi, mid, lo


def _silu(x):
    return x / (1.0 + jnp.exp(-x))


def _sigmoid(x):
    return 1.0 / (1.0 + jnp.exp(-x))


def _swap_halves(x):
    return pltpu.roll(x, HEAD_DIM, 1)


def _t5_bucket_np(dist):
    n = np.maximum(dist, 0)
    max_exact = N_BUCKETS // 2
    nf = np.maximum(n, 1).astype(np.float32)
    large = max_exact + (np.log(nf / max_exact) / math.log(MAX_DISTANCE / max_exact)
                         * (N_BUCKETS - max_exact)).astype(np.int32)
    large = np.minimum(large, N_BUCKETS - 1)
    return np.where(n < max_exact, n, large).astype(np.int32)


def _ada_kernel(c_ref, w_ref, b_ref, o_ref):
    c = c_ref[...]
    o_ref[...] = jnp.dot(_silu(c), w_ref[...], preferred_element_type=F32,
                         precision=lax.Precision.HIGHEST) + b_ref[...]


def _ada(c, w_ada, b_ada):
    B, D = c.shape
    n_out = w_ada.shape[1]
    tn = 1024
    return pl.pallas_call(
        _ada_kernel,
        grid=(n_out // tn,),
        in_specs=[pl.BlockSpec((B, D), lambda j: (0, 0)),
                  pl.BlockSpec((D, tn), lambda j: (0, j)),
                  pl.BlockSpec((1, tn), lambda j: (0, j))],
        out_specs=pl.BlockSpec((B, tn), lambda j: (0, j)),
        out_shape=jax.ShapeDtypeStruct((B, n_out), F32),
        compiler_params=pltpu.CompilerParams(dimension_semantics=("arbitrary",),
                                             vmem_limit_bytes=VMEM_LIMIT),
        name="ada",
    )(c, w_ada, b_ada.reshape(1, n_out))


def _bias_table_kernel(bkt_ref, rbt_ref, o_ref):
    bkt = bkt_ref[...]
    k = lax.broadcasted_iota(jnp.int32, (N_BUCKETS, bkt.shape[1]), 0)
    onehot = jnp.where(k == bkt, 1.0, 0.0).astype(BF16)
    hi, mid, lo = _split3(rbt_ref[...])
    o_ref[...] = _dot(hi, onehot) + _dot(mid, onehot) + _dot(lo, onehot)


def _bias_table(bucket, rel_bias_t):
    n = bucket.shape[1]
    chunk = 8192
    n_heads = rel_bias_t.shape[0]
    return pl.pallas_call(
        _bias_table_kernel,
        grid=(n // chunk,),
        in_specs=[pl.BlockSpec((1, chunk), lambda j: (0, j)),
                  pl.BlockSpec(rel_bias_t.shape, lambda j: (0, 0))],
        out_specs=pl.BlockSpec((n_heads, chunk), lambda j: (0, j)),
        out_shape=jax.ShapeDtypeStruct((n_heads, n), F32),
        compiler_params=pltpu.CompilerParams(dimension_semantics=("parallel",),
                                             vmem_limit_bytes=VMEM_LIMIT),
        name="bias_table",
    )(bucket, rel_bias_t)


_C_FQ, _C_FK, _C_NQ = 0, 512, 1024
_C_K3 = 1536
_C_SK = 1920
_C_FV = 2176
_C_SV = 3200
_C_WV = 3456
_C_MISC = 3712
_IN_COLS = 3840
LOG2E = math.log2(math.e)


def _in_proj_kernel(x_ref, mod_ref, w_ref, bf_ref, ones_ref, fq_ref, fk_ref, nq_ref, kc_ref,
                    kw_ref, vc_ref, ks_ref, fv_ref, vs_ref, vw_ref, misc_ref, misct_ref,
                    carry_ref):
    s_idx = pl.program_id(1)
    tm = x_ref.shape[0]
    mod = mod_ref[...]
    h = (x_ref[...] * (1.0 + mod[1:2, :]) + mod[0:1, :]).astype(BF16)

    for ref, c0 in ((fq_ref, _C_FQ), (fk_ref, _C_FK), (nq_ref, _C_NQ)):
        ref[...] = _dot(h, w_ref[:, c0:c0 + 512]).astype(ref.dtype)
    for k, ref in enumerate((kc_ref, kw_ref, vc_ref)):
        c0 = _C_K3 + k * LANES
        ref[...] = _dot(h, w_ref[:, c0:c0 + LANES]).astype(ref.dtype)
    w_sk = ks_ref.shape[-1]
    key_blk = lax.shift_right_logical(
        s_idx * tm + lax.broadcasted_iota(jnp.int32, (tm, w_sk), 0), int(math.log2(SEL_BLOCK)))
    lane_in_slab = jnp.bitwise_and(lax.broadcasted_iota(jnp.int32, (tm, w_sk), 1), LANES - 1)
    blk_hot = jnp.where(lane_in_slab == HEAD_DIM + key_blk, 1.0, 0.0)
    ks_ref[...] = (_dot(h, w_ref[:, _C_SK:_C_SK + w_sk]) + blk_hot).astype(ks_ref.dtype)
    for ref, c0 in ((fv_ref, _C_FV), (vs_ref, _C_SV), (vw_ref, _C_WV)):
        c1 = c0 + ref.shape[-1]
        ref[...] = (_dot(h, w_ref[:, c0:c1]) + ones_ref[:, c0:c1]).astype(ref.dtype)

    z = _dot(h, w_ref[:, _C_MISC:_C_MISC + LANES]) + bf_ref[...]
    lane = lax.broadcasted_iota(jnp.int32, z.shape, 1)
    is_f = lane < FOX_HEADS
    log_f = jnp.minimum(z, 0.0) - jnp.log(1.0 + jnp.exp(-jnp.abs(z)))
    log_f = jnp.where(is_f, log_f, 0.0)

    row = lax.broadcasted_iota(jnp.int32, (tm, tm), 0)
    col = lax.broadcasted_iota(jnp.int32, (tm, tm), 1)
    tri = jnp.where(row >= col, 1.0, 0.0).astype(BF16)
    hi, mid, lo = _split3(log_f)
    local = _dot(tri, hi) + _dot(tri, mid) + _dot(tri, lo)

    @pl.when(s_idx == 0)
    def _():
        carry_ref[...] = jnp.zeros_like(carry_ref)

    cum = local + carry_ref[...]
    carry_ref[...] = cum[tm - 1:tm, :]
    misc = jnp.where(is_f, cum * LOG2E, _sigmoid(z))
    misc_ref[...] = misc
    misct_ref[...] = misc.T


def _in_proj(x, mod, w_r, bf_row, ones_row):
    B, S, D = x.shape
    tm = ROW_TILE
    widths = (512, 512, 512, LANES, LANES, LANES, _C_FV - _C_SK, _C_SV - _C_FV, _C_WV - _C_SV,
              _C_MISC - _C_WV)
    wide = lambda w: pl.BlockSpec((None, tm, w), lambda b, s: (b, s, 0))
    out_shape = ([jax.ShapeDtypeStruct((B, S, w), BF16) for w in widths]
                 + [jax.ShapeDtypeStruct((B, S, LANES), F32),
                    jax.ShapeDtypeStruct((B, LANES, S), F32)])
    out_specs = ([wide(w) for w in widths]
                 + [wide(LANES), pl.BlockSpec((None, LANES, tm), lambda b, s: (b, 0, s))])
    return pl.pallas_call(
        _in_proj_kernel,
        grid=(B, S // tm),
        in_specs=[pl.BlockSpec((None, tm, D), lambda b, s: (b, s, 0)),
                  pl.BlockSpec((None, 6, D), lambda b, s: (b, 0, 0)),
                  pl.BlockSpec((D, _IN_COLS), lambda b, s: (0, 0)),
                  pl.BlockSpec((1, LANES), lambda b, s: (0, 0)),
                  pl.BlockSpec((1, _IN_COLS), lambda b, s: (0, 0))],
        out_specs=out_specs,
        out_shape=out_shape,
        scratch_shapes=[pltpu.VMEM((1, LANES), F32)],
        compiler_params=pltpu.CompilerParams(dimension_semantics=("parallel", "arbitrary"),
                                             vmem_limit_bytes=VMEM_LIMIT),
        name="in_proj",
    )(x, mod, w_r, bf_row, ones_row)


def _softmax_weights(s, m):
    return jnp.exp2((s - m).astype(BF16))


def _flash_update(carry, s, vt):
    m, acc = carry
    m_new = jnp.maximum(m, jnp.max(s, axis=-1, keepdims=True))
    alpha = jnp.exp2(m - m_new)
    p = _softmax_weights(s, m_new)
    rows = acc.shape[0]
    acc = alpha.reshape(rows, 1) * acc + _dot(p.reshape(rows, s.shape[-1]), vt)
    return m_new, acc


def _normalize(acc):
    return acc / _swap_halves(acc)


def _fox_kernel(q_ref, k_ref, v_ref, ck_ref, o_ref):
    i = pl.program_id(2)
    tq = q_ref.shape[0]
    tk = ck_ref.shape[2]
    q2 = q_ref[...].astype(F32)
    lane = lax.broadcasted_iota(jnp.int32, (tq, LANES), 1)
    low = lane < HEAD_DIM
    halves = (low, jnp.logical_not(low))
    qh = [jnp.where(h, q2, 0.0).astype(BF16) for h in halves]
    col_minus_row = (lax.broadcasted_iota(jnp.int32, (tq, tk), 1)
                     - lax.broadcasted_iota(jnp.int32, (tq, tk), 0))

    def step(jj, carry, diagonal):
        k0 = pl.multiple_of(jj * tk, tk)
        kt = k_ref[pl.ds(k0, tk), :]
        new = []
        for hh in range(2):
            s = _dot_nt(qh[hh], kt) - ck_ref[hh, pl.ds(jj, 1), :]
            if diagonal:
                s = jnp.where(col_minus_row <= i * tq - jj * tk, s, NEG_BIG)
            vt = v_ref[pl.ds(k0, tk), hh * LANES:(hh + 1) * LANES]
            new.append(_flash_update(carry[hh], s, vt))
        return tuple(new)

    init = tuple((jnp.full((tq, 1), NEG_BIG, F32), jnp.zeros((tq, LANES), F32))
                 for _ in range(2))
    n_full = (i * tq) // tk
    carry = lax.fori_loop(0, n_full, lambda jj, c: step(jj, c, False), init)
    carry = step(n_full, carry, True)
    o_ref[...] = jnp.where(low, _normalize(carry[0][1]),
                           _swap_halves(_normalize(carry[1][1]))).astype(o_ref.dtype)


def _fox(fq, fk, fv, cum_row):
    B, S, W = fq.shape
    tq = tk = FOX_TILE
    n_pairs = W // LANES
    cum_row = cum_row.reshape(B, n_pairs, 2, S // tk, tk)
    return pl.pallas_call(
        _fox_kernel,
        grid=(B, n_pairs, S // tq),
        in_specs=[pl.BlockSpec((None, tq, LANES), lambda b, p, i: (b, i, p)),
                  pl.BlockSpec((None, S, LANES), lambda b, p, i: (b, 0, p)),
                  pl.BlockSpec((None, S, 2 * LANES), lambda b, p, i: (b, 0, p)),
                  pl.BlockSpec((None, None, 2, S // tk, tk), lambda b, p, i: (b, p, 0, 0, 0))],
        out_specs=pl.BlockSpec((None, tq, LANES), lambda b, p, i: (b, i, p)),
        out_shape=jax.ShapeDtypeStruct((B, S, W), BF16),
        compiler_params=pltpu.CompilerParams(
            dimension_semantics=("parallel", "parallel", "arbitrary"),
            vmem_limit_bytes=VMEM_LIMIT),
        name="fox",
    )(fq, fk, fv, cum_row)


def _compress_kernel(xk_ref, xv_ref, pk_ref, pv_ref, wak_ref, wbk_ref, wav_ref, wbv_ref,
                     w2k_ref, w2v_ref, ok_ref, ov_ref):
    n_rows = xk_ref.shape[0]
    for x_ref, p_ref, wa_ref, wb_ref, w2_ref, o_ref in (
            (xk_ref, pk_ref, wak_ref, wbk_ref, w2k_ref, ok_ref),
            (xv_ref, pv_ref, wav_ref, wbv_ref, w2v_ref, ov_ref)):
        x = x_ref[...].astype(F32)
        xa = (x + p_ref[0:1, :]).astype(BF16)
        xb = (x + p_ref[1:2, :]).astype(BF16)
        hb = _dot(xb, wb_ref[...])
        h1 = _dot(xa, wa_ref[...]) + pltpu.roll(hb, n_rows - 1, 0)
        o_ref[...] = _dot(_silu(h1).astype(BF16), w2_ref[...]).astype(o_ref.dtype)


def _compress(xk, xv, pk, pv, wak, wbk, wav, wbv, w2k, w2v):
    B, R, C = xk.shape
    xspec = pl.BlockSpec((None, R, C), lambda b: (b, 0, 0))
    full = lambda a: pl.BlockSpec(a.shape, lambda b: (0,) * a.ndim)
    ospec = pl.BlockSpec((None, R, LANES), lambda b: (b, 0, 0))
    return pl.pallas_call(
        _compress_kernel,
        grid=(B,),
        in_specs=[xspec, xspec] + [full(a) for a in (pk, pv, wak, wbk, wav, wbv, w2k, w2v)],
        out_specs=[ospec, ospec],
        out_shape=[jax.ShapeDtypeStruct((B, R, LANES), BF16)] * 2,
        compiler_params=pltpu.CompilerParams(dimension_semantics=("parallel",),
                                             vmem_limit_bytes=VMEM_LIMIT),
        name="compress",
    )(xk, xv, pk, pv, wak, wbk, wav, wbv, w2k, w2v)


def _rank_rows(score):
    n = score.shape[0]
    j = lax.broadcasted_iota(jnp.int32, score.shape, 0)
    rank = jnp.zeros(score.shape, jnp.int32)
    for i in range(n):
        si = score[i:i + 1, :]
        beats = (si > score) | ((si == score) & (j > i))
        rank = rank + jnp.where(beats, 1, 0)
    return rank


def _dup_head(q4, r):
    pair = q4[:, (r // 2) * LANES:(r // 2 + 1) * LANES].astype(F32)
    lane = lax.broadcasted_iota(jnp.int32, pair.shape, 1)
    swapped = _swap_halves(pair)
    if r % 2 == 0:
        return jnp.where(lane < HEAD_DIM, pair, swapped)
    return jnp.where(lane < HEAD_DIM, swapped, pair)


def _pack_heads(o_list, g):
    lane = lax.broadcasted_iota(jnp.int32, o_list[0].shape, 1)
    in_g = (lane >= g * HEAD_DIM) & (lane < (g + 1) * HEAD_DIM)
    both = []
    for o in o_list:
        om = jnp.where(in_g, o, 0.0)
        both.append(om + _swap_halves(om))
    pairs = [jnp.where(lane < HEAD_DIM, both[2 * p], both[2 * p + 1]) for p in range(2)]
    return jnp.concatenate(pairs, axis=1)


def _group_gates(misc, g):
    w = 3 * NSA_GQA
    gates = misc[:, FOX_HEADS:FOX_HEADS + w]
    for other in range(1, NSA_GROUPS):
        gates = jnp.where(g == other, misc[:, FOX_HEADS + other * w:FOX_HEADS + (other + 1) * w],
                          gates)
    return gates


def _cmp_sel_kernel(q_ref, kc_ref, vc_ref, bias_ref, gate_ref, ovl_ref, oc_ref, sel_ref):
    g = pl.program_id(1)
    i = pl.program_id(2)
    tq = q_ref.shape[0]
    n_pad = kc_ref.shape[0]
    q4 = q_ref[...]
    lane = lax.broadcasted_iota(jnp.int32, (n_pad, LANES), 1)
    in_g = (lane >= g * HEAD_DIM) & (lane < (g + 1) * HEAD_DIM)
    kc = jnp.where(in_g, kc_ref[...].astype(F32), 0.0).astype(BF16)
    vc = vc_ref[...]
    t = i * tq + lax.broadcasted_iota(jnp.int32, (tq, n_pad), 0)
    n = lax.broadcasted_iota(jnp.int32, (tq, n_pad), 1)
    valid = t >= n * CMP_STRIDE + (CMP_BLOCK - 1)
    gates = _group_gates(gate_ref[...], g)
    p_sum = jnp.zeros((tq, n_pad), F32)
    outs = []
    for r in range(NSA_GQA):
        qr = _dup_head(q4, r).astype(BF16)
        s = _dot_nt(qr, kc)
        s = jnp.where(valid, s + bias_ref[r], NEG_BIG)
        m = jnp.max(s, axis=-1, keepdims=True)
        p = jnp.exp2(s - m)
        p = p / jnp.sum(p, axis=-1, keepdims=True)
        p = jnp.where(valid, p, 0.0)
        p_sum = p_sum + p
        outs.append(_dot(p.astype(BF16), vc) * gates[:, 3 * r:3 * r + 1])
    oc_ref[...] = _pack_heads(outs, g).astype(oc_ref.dtype)

    ovl = ovl_ref[...]
    hi, mid, lo = _split3(p_sum)
    imp = _dot_nt(ovl, hi) + _dot_nt(ovl, mid) + _dot_nt(ovl, lo)
    n_blk = imp.shape[0]
    j = lax.broadcasted_iota(jnp.int32, (n_blk, tq), 0)
    qb = jnp.right_shift(i * tq + lax.broadcasted_iota(jnp.int32, (n_blk, tq), 1),
                         int(math.log2(SEL_BLOCK)))
    forced = (j == 0) | (j == qb) | (j == qb - 1)
    causal = j <= qb
    score = jnp.where(causal, imp + jnp.where(forced, FORCE_SCORE, 0.0), -FORCE_SCORE)
    chosen = (_rank_rows(score) < N_SEL) & causal
    sel = jnp.where(chosen, 1.0, 0.0)
    sel = jnp.concatenate([sel, jnp.zeros((LANES - n_blk, tq), F32)], axis=0)
    sel_ref[...] = sel.T


def _cmp_sel(nq, kcmp, vcmp, bias_c, gates_g, ovl_t):
    B, S, _ = nq.shape
    tq = CMP_TILE
    n_pad = kcmp.shape[1]
    return pl.pallas_call(
        _cmp_sel_kernel,
        grid=(B, NSA_GROUPS, S // tq),
        in_specs=[pl.BlockSpec((None, tq, 2 * LANES), lambda b, g, i: (b, i, g)),
                  pl.BlockSpec((None, n_pad, LANES), lambda b, g, i: (b, 0, 0)),
                  pl.BlockSpec((None, n_pad, LANES), lambda b, g, i: (b, 0, 0)),
                  pl.BlockSpec((NSA_GQA, tq, n_pad), lambda b, g, i: (g, i, 0)),
                  pl.BlockSpec((None, tq, LANES), lambda b, g, i: (b, i, 0)),
                  pl.BlockSpec(ovl_t.shape, lambda b, g, i: (0, 0))],
        out_specs=[pl.BlockSpec((None, tq, 2 * LANES), lambda b, g, i: (b, i, g)),
                   pl.BlockSpec((None, None, tq, LANES), lambda b, g, i: (b, g, i, 0))],
        out_shape=[jax.ShapeDtypeStruct((B, S, NSA_HEADS * HEAD_DIM), BF16),
                   jax.ShapeDtypeStruct((B, NSA_GROUPS, S, LANES), F32)],
        compiler_params=pltpu.CompilerParams(
            dimension_semantics=("parallel", "parallel", "arbitrary"),
            vmem_limit_bytes=VMEM_LIMIT),
        name="cmp_sel",
    )(nq, kcmp, vcmp, bias_c, gates_g, ovl_t)


def _nsa_kernel(q_ref, ks_ref, vs_ref, kw_ref, vw_ref, sel_ref, w4_ref, gate_ref, oc_ref,
                o_ref):
    g = pl.program_id(1)
    i = pl.program_id(2)
    tq = q_ref.shape[0]
    tk = w4_ref.shape[-1]
    H = NSA_GQA
    q4 = q_ref[...]
    lane = lax.broadcasted_iota(jnp.int32, (tq, LANES), 1)
    in_g = (lane >= g * HEAD_DIM) & (lane < (g + 1) * HEAD_DIM)
    heads = [_dup_head(q4, r) for r in range(H)]
    qs = jnp.concatenate([jnp.where(in_g, hd, 0.0).astype(BF16) for hd in heads], axis=0)

    blk_mask = _swap_halves((sel_ref[...] - 1.0) * (-NEG_BIG))
    qsel = jnp.concatenate(
        [jnp.where(lane < HEAD_DIM, hd, blk_mask).astype(BF16) for hd in heads], axis=0)

    def sel_step(jj, carry, diagonal):
        k0 = pl.multiple_of(jj * tk, tk)
        dd = jnp.minimum(i - 2 * jj, 3)
        s = (_dot_nt(qsel, ks_ref[pl.ds(k0, tk), :]).reshape(H, tq, tk)
             + w4_ref[:, pl.ds(dd, 1)].reshape(H, tq, tk))
        if diagonal:
            cmr = (lax.broadcasted_iota(jnp.int32, (tq, tk), 1)
                   - lax.broadcasted_iota(jnp.int32, (tq, tk), 0))
            s = jnp.where((cmr <= i * tq - jj * tk)[None], s, NEG_BIG)
        return _flash_update(carry, s, vs_ref[pl.ds(k0, tk), :])

    init = (jnp.full((H, tq, 1), NEG_BIG, F32), jnp.zeros((H * tq, LANES), F32))
    n_full = (i * tq) // tk
    carry = lax.fori_loop(0, n_full, lambda jj, c: sel_step(jj, c, False), init)
    _, acc_s = sel_step(n_full, carry, True)

    tiles = (jnp.maximum(i - 2, 0), jnp.maximum(i - 1, 0), i)
    starts = [pl.multiple_of(j * tq, tq) for j in tiles]
    kt = jnp.concatenate([kw_ref[pl.ds(st, tq), :] for st in starts], axis=0)
    vt = jnp.concatenate([vw_ref[pl.ds(st, tq), :] for st in starts], axis=0)
    bias_w = jnp.concatenate([w4_ref[:, 2], w4_ref[:, 0, :, 0:tq]], axis=-1)
    col = lax.broadcasted_iota(jnp.int32, (tq, 3 * tq), 1)
    cmr = col - lax.broadcasted_iota(jnp.int32, (tq, 3 * tq), 0)
    never = 4 * tq
    valid = (((col < tq) & (cmr > jnp.where(i >= 2, 0, never)))
             | ((col >= jnp.where(i >= 1, tq, never)) & (col < 2 * tq))
             | ((col >= 2 * tq) & (cmr <= 2 * tq)))
    s = _dot_nt(qs, kt).reshape(H, tq, 3 * tq) + bias_w
    s = jnp.where(valid[None], s, NEG_BIG)
    p = _softmax_weights(s, jnp.max(s, axis=-1, keepdims=True))
    acc_w = _dot(p.reshape(H * tq, 3 * tq), vt)

    gates = _group_gates(gate_ref[...], g)
    o_s = _normalize(acc_s)
    o_w = _normalize(acc_w)
    outs = []
    for r in range(H):
        sl = slice(r * tq, (r + 1) * tq)
        outs.append(o_s[sl] * gates[:, 3 * r + 1:3 * r + 2] + o_w[sl] * gates[:, 3 * r + 2:3 * r + 3])
    lane = lax.broadcasted_iota(jnp.int32, (tq, LANES), 1)
    pairs = [jnp.where(lane < HEAD_DIM, outs[2 * p], _swap_halves(outs[2 * p + 1]))
             for p in range(H // 2)]
    o_ref[...] = (jnp.concatenate(pairs, axis=1) + oc_ref[...].astype(F32)).astype(o_ref.dtype)


def _nsa(nq, ks, vs, kw, vw, sel, w4, misc, oc):
    B, S, _ = nq.shape
    tq = ATT_TILE
    tk = w4.shape[-1]
    both = lambda: pl.BlockSpec((None, S, LANES), lambda b, g, i: (b, 0, 0))
    mine = lambda: pl.BlockSpec((None, S, LANES), lambda b, g, i: (b, 0, g))
    return pl.pallas_call(
        _nsa_kernel,
        grid=(B, NSA_GROUPS, S // tq),
        in_specs=[pl.BlockSpec((None, tq, 2 * LANES), lambda b, g, i: (b, i, g)),
                  mine(), mine(), both(), mine(),
                  pl.BlockSpec((None, None, tq, LANES), lambda b, g, i: (b, g, i, 0)),
                  pl.BlockSpec((NSA_GQA, 4, tq, tk), lambda b, g, i: (g, 0, 0, 0)),
                  pl.BlockSpec((None, tq, LANES), lambda b, g, i: (b, i, 0)),
                  pl.BlockSpec((None, tq, 2 * LANES), lambda b, g, i: (b, i, g))],
        out_specs=pl.BlockSpec((None, tq, 2 * LANES), lambda b, g, i: (b, i, g)),
        out_shape=jax.ShapeDtypeStruct((B, S, NSA_HEADS * HEAD_DIM), BF16),
        compiler_params=pltpu.CompilerParams(
            dimension_semantics=("parallel", "parallel", "arbitrary"),
            vmem_limit_bytes=VMEM_LIMIT),
        name="nsa",
    )(nq, ks, vs, kw, vw, sel, w4, misc, oc)


def _layer_norm(y, g, b):
    mu = jnp.mean(y, axis=-1, keepdims=True)
    yc = y - mu
    var = jnp.mean(yc * yc, axis=-1, keepdims=True)
    return yc * lax.rsqrt(var + LN_EPS) * g + b


def _router_gates_t(h2, wr_t, eb_col):
    tm = h2.shape[0]
    h_parts = _split3(h2)
    w_parts = _split3(wr_t)
    logit = jnp.zeros((N_EXPERTS, tm), F32)
    for a, wp in enumerate(w_parts):
        for b, hp in enumerate(h_parts):
            if a + b <= 2:
                logit = logit + _dot_nt(wp, hp)
    scores = _sigmoid(logit)
    biased = scores + eb_col
    e_in = lax.broadcasted_iota(jnp.int32, (GROUP_SIZE, tm), 0).astype(F32)
    gs_rows = []
    for gi in range(N_EXPERT_GROUPS):
        grp = biased[gi * GROUP_SIZE:(gi + 1) * GROUP_SIZE, :]
        m1 = jnp.max(grp, axis=0, keepdims=True)
        first = jnp.min(jnp.where(grp == m1, e_in, float(GROUP_SIZE)), axis=0, keepdims=True)
        m2 = jnp.max(jnp.where(e_in == first, -jnp.inf, grp), axis=0, keepdims=True)
        gs_rows.append(m1 + m2)
    gscore = jnp.concatenate(gs_rows, axis=0)
    g_keep = _rank_rows(gscore) < TOPK_GROUPS
    keep = jnp.concatenate(
        [jnp.broadcast_to(g_keep[gi:gi + 1, :], (GROUP_SIZE, tm)) for gi in range(N_EXPERT_GROUPS)],
        axis=0)
    masked = jnp.where(keep, biased, -jnp.inf)
    chosen = _rank_rows(masked) < TOP_K
    w = jnp.where(chosen, scores, 0.0)
    return w / jnp.sum(w, axis=0, keepdims=True) * ROUTED_SCALE


def _out_proj_kernel(alpha, of_ref, on_ref, x_ref, mod_ref, w_ref, lg_ref, lb_ref, wr_ref,
                     eb_ref, x1_ref, h2_ref, gate_ref, gate_t_ref):
    half = of_ref.shape[1]
    mod = mod_ref[...]
    mixed = _dot(of_ref[...], w_ref[0:half, :]) + _dot(on_ref[...], w_ref[half:2 * half, :])
    y = alpha * x_ref[...] + mod[2:3, :] * mixed
    x1 = _layer_norm(y, lg_ref[...], lb_ref[...])
    x1_ref[...] = x1
    h2 = x1 * (1.0 + mod[4:5, :]) + mod[3:4, :]
    h2_ref[...] = h2.astype(h2_ref.dtype)
    gates_t = _router_gates_t(h2, wr_ref[...], eb_ref[...])
    gate_t_ref[...] = gates_t
    tm = h2.shape[0]
    gates_t = jnp.concatenate([gates_t, jnp.zeros((LANES - N_EXPERTS, tm), F32)], axis=0)
    gate_ref[...] = gates_t.T


def _out_proj(alpha, o_fox, o_nsa, x, mod, w_out, ln_g, ln_b, wr_t, eb_col):
    B, S, D = x.shape
    tm = ROW_TILE
    half = o_fox.shape[-1]
    row = lambda a: pl.BlockSpec(a.shape, lambda b, s: (0, 0))
    return pl.pallas_call(
        functools.partial(_out_proj_kernel, alpha),
        grid=(B, S // tm),
        in_specs=[pl.BlockSpec((None, tm, half), lambda b, s: (b, s, 0)),
                  pl.BlockSpec((None, tm, half), lambda b, s: (b, s, 0)),
                  pl.BlockSpec((None, tm, D), lambda b, s: (b, s, 0)),
                  pl.BlockSpec((None, 6, D), lambda b, s: (b, 0, 0)),
                  row(w_out), row(ln_g), row(ln_b), row(wr_t), row(eb_col)],
        out_specs=[pl.BlockSpec((None, tm, D), lambda b, s: (b, s, 0)),
                   pl.BlockSpec((None, tm, D), lambda b, s: (b, s, 0)),
                   pl.BlockSpec((None, tm, LANES), lambda b, s: (b, s, 0)),
                   pl.BlockSpec((N_EXPERTS, tm), lambda b, s: (0, b * (S // tm) + s))],
        out_shape=[jax.ShapeDtypeStruct((B, S, D), F32),
                   jax.ShapeDtypeStruct((B, S, D), BF16),
                   jax.ShapeDtypeStruct((B, S, LANES), F32),
                   jax.ShapeDtypeStruct((N_EXPERTS, B * S), F32)],
        compiler_params=pltpu.CompilerParams(dimension_semantics=("parallel", "parallel"),
                                             vmem_limit_bytes=VMEM_LIMIT),
        name="out_proj",
    )(o_fox, o_nsa, x, mod, w_out, ln_g, ln_b, wr_t, eb_col)


SORT_TILE = 256
ROW_ALIGN = 16
EXP_TILE = 1024
EXP_STRIP = 256
P_CHUNK = 256


def _strict_upper(n):
    return jnp.where(lax.broadcasted_iota(jnp.int32, (n, n), 0)
                     < lax.broadcasted_iota(jnp.int32, (n, n), 1), 1.0, 0.0).astype(BF16)


def _strict_lower(n):
    return jnp.where(lax.broadcasted_iota(jnp.int32, (n, n), 1)
                     < lax.broadcasted_iota(jnp.int32, (n, n), 0), 1.0, 0.0).astype(BF16)


def _local_rows_bound(ts):
    rows = TOP_K * ts + N_EXPERTS * (ROW_ALIGN - 1)
    return -(-rows // P_CHUNK) * P_CHUNK


def _piece_cols(ts):
    return -(-(_local_rows_bound(ts) // ROW_ALIGN) // LANES) * LANES


def _sorted_tiles_bound(T):
    rows = TOP_K * T + (T // SORT_TILE) * N_EXPERTS * (ROW_ALIGN - 1)
    return -(-rows // EXP_TILE) + N_EXPERTS


def _sorted_tiles_expected(T):
    groups = (T // SORT_TILE) * N_EXPERTS
    rows = TOP_K * T + groups * ((ROW_ALIGN - 1) / 2 + 2)
    return int(-(-rows // EXP_TILE) + math.ceil(0.65 * N_EXPERTS))


def _moe_meta_kernel(gt_ref, ptab_ref, loc_et_ref, np_et_ref, loc_te_ref, np_te_ref, tot_ref,
                     erow_ref, texp_ref, nused_ref):
    E, T = gt_ref.shape
    mask = jnp.where(gt_ref[...] > 0.0, 1.0, 0.0).astype(BF16)
    t_id = lax.shift_right_logical(lax.broadcasted_iota(jnp.int32, (T, LANES), 0),
                                   int(math.log2(SORT_TILE)))
    tind = jnp.where(t_id == lax.broadcasted_iota(jnp.int32, (T, LANES), 1), 1.0, 0.0)
    cnt = _dot(mask, tind.astype(BF16))
    n16 = jnp.floor((cnt + (ROW_ALIGN - 1.0)) * (1.0 / ROW_ALIGN))
    n16b = n16.astype(BF16)
    q = EXP_TILE // ROW_ALIGN
    len16 = jnp.sum(n16, axis=1, keepdims=True)
    pad16 = jnp.floor((len16 + (q - 1.0)) * (1.0 / q)) * q
    sl = _strict_lower(E)
    hi, mid, lo = _split3(jnp.broadcast_to(pad16, (E, LANES)))
    start16 = _dot(sl, hi) + _dot(sl, mid) + _dot(sl, lo)
    gdst16 = start16 + _dot(n16b, _strict_upper(LANES))
    loc16 = _dot(sl, n16b)

    def t(a):
        return jnp.concatenate([a, jnp.zeros((LANES - E, LANES), F32)], axis=0).T

    scale = float(ROW_ALIGN)
    loc_et_ref[...] = loc16 * scale
    np_et_ref[...] = n16 * scale
    loc_te_ref[...] = t(loc16) * scale
    np_te_ref[...] = t(n16) * scale
    tot_ref[...] = (jnp.sum(n16, axis=0, keepdims=True) * scale).astype(jnp.int32)

    n_t, n_blk = ptab_ref.shape
    blk = lax.broadcasted_iota(jnp.int32, (E, n_blk), 1).astype(F32)
    for tile in range(n_t):
        lo_c = loc16[:, tile:tile + 1]
        inside = (lo_c <= blk) & (blk < lo_c + n16[:, tile:tile + 1])
        dst = jnp.sum(jnp.where(inside, gdst16[:, tile:tile + 1] + (blk - lo_c), 0.0),
                      axis=0, keepdims=True)
        ptab_ref[tile:tile + 1, :] = (dst * scale).astype(jnp.int32)
    ends = jnp.concatenate([t(start16 + len16)[0:1, :], t(start16 + pad16)[0:1, :],
                            jnp.zeros((erow_ref.shape[0] - 2, LANES), F32)], axis=0)
    erow_ref[...] = (ends * scale).astype(jnp.int32)
    n_tab = texp_ref.shape[1]
    tile_row16 = (lax.broadcasted_iota(jnp.int32, (E, n_tab), 1) * q).astype(F32)
    owner = jnp.sum(jnp.where(start16[:, 0:1] <= tile_row16, 1.0, 0.0), axis=0, keepdims=True)
    texp_ref[...] = (owner - 1.0).astype(jnp.int32)
    n_used = jnp.sum(pad16, axis=0, keepdims=True) * (1.0 / q)
    nused_ref[...] = jnp.broadcast_to(n_used, (1, LANES)).astype(jnp.int32)


def _moe_meta(gates_t, n_tab):
    E, T = gates_t.shape
    i32 = jnp.int32
    return pl.pallas_call(
        _moe_meta_kernel,
        out_shape=[jax.ShapeDtypeStruct((T // SORT_TILE, _piece_cols(SORT_TILE)), i32),
                   jax.ShapeDtypeStruct((E, LANES), F32),
                   jax.ShapeDtypeStruct((E, LANES), F32),
                   jax.ShapeDtypeStruct((LANES, LANES), F32),
                   jax.ShapeDtypeStruct((LANES, LANES), F32),
                   jax.ShapeDtypeStruct((1, LANES), i32),
                   jax.ShapeDtypeStruct((8, LANES), i32),
                   jax.ShapeDtypeStruct((1, n_tab), i32),
                   jax.ShapeDtypeStruct((1, LANES), i32)],
        compiler_params=pltpu.CompilerParams(vmem_limit_bytes=VMEM_LIMIT),
        name="moe_meta",
    )(gates_t)


def _start_pieces(tile, ptab_s, tot_s, n_cols, make_copy):
    n_pieces = lax.shift_right_logical(tot_s[tile], int(math.log2(ROW_ALIGN)))

    def body(b, carry):
        make_copy(pl.multiple_of(b * ROW_ALIGN, ROW_ALIGN),
                  pl.multiple_of(ptab_s[tile * n_cols + b], ROW_ALIGN)).start()
        return carry

    lax.fori_loop(0, n_pieces, body, 0)


def _moe_sort_kernel(ptab_s, tot_s, lend_s, rend_s, gt_ref, h_ref, locrow_ref, nprow_ref,
                     xs_hbm, buf, zbuf, sem, zsem):
    tau = pl.program_id(0)
    n_t = pl.num_programs(0)
    slot = lax.rem(tau, 2)
    E, ts = gt_ref.shape
    D = h_ref.shape[1]
    n_cols = _piece_cols(ts)

    def copies(tile, sl, wait):
        if wait:
            rows = pl.multiple_of(tot_s[tile], ROW_ALIGN)

            @pl.when(rows > 0)
            def _():
                pltpu.make_async_copy(buf.at[sl, pl.ds(0, rows)], xs_hbm.at[pl.ds(0, rows)],
                                      sem.at[sl]).wait()
        else:
            def make_copy(loc, dst):
                return pltpu.make_async_copy(buf.at[sl, pl.ds(loc, ROW_ALIGN)],
                                             xs_hbm.at[pl.ds(dst, ROW_ALIGN)], sem.at[sl])
            _start_pieces(tile, ptab_s, tot_s, n_cols, make_copy)

    z_rows = zbuf.shape[0]
    used_rows = rend_s[E - 1]
    n_spare = (xs_hbm.shape[0] - used_rows) // z_rows

    def spare_fill(wait):
        def body(c, carry):
            dst = pl.multiple_of(used_rows + c * z_rows, z_rows)
            cp = pltpu.make_async_copy(zbuf, xs_hbm.at[pl.ds(dst, z_rows)], zsem.at[1])
            if wait:
                cp.wait()
            else:
                cp.start()
            return carry

        lax.fori_loop(0, n_spare, body, 0)

    @pl.when(tau == 0)
    def _():
        zbuf[...] = jnp.zeros_like(zbuf)
        spare_fill(False)

    @pl.when(tau >= 2)
    def _():
        copies(tau - 2, slot, True)

    g = gt_ref[...]
    mask = g > 0.0
    maskb = jnp.where(mask, 1.0, 0.0).astype(BF16)
    pad = jnp.zeros((LANES - E, ts), F32)
    pos = jnp.where(mask, _dot(maskb, _strict_upper(ts)), -1.0)
    pos = jnp.concatenate([pos, pad], axis=0).astype(BF16)
    g_parts = _split3(jnp.concatenate([g, pad], axis=0))
    lo_row = locrow_ref[...]
    hi_row = lo_row + nprow_ref[...]
    h = h_ref[...]
    n_chunks = lax.shift_right_logical(tot_s[tau] + (P_CHUNK - 1), int(math.log2(P_CHUNK)))
    lane = lax.broadcasted_iota(jnp.int32, (P_CHUNK, LANES), 1)

    def chunk(c):
        r0 = pl.multiple_of(c * P_CHUNK, P_CHUNK)
        r = (r0 + lax.broadcasted_iota(jnp.int32, (P_CHUNK, LANES), 0)).astype(F32)
        inside = (lo_row <= r) & (r < hi_row)
        group = jnp.where(inside, 1.0, 0.0).astype(BF16)
        want = r[:, 0:1] - jnp.sum(jnp.where(inside, lo_row, 0.0), axis=1, keepdims=True)
        hit = _dot(group, pos) == want
        buf[slot, pl.ds(r0, P_CHUNK), 0:D] = _dot(jnp.where(hit, 1.0, 0.0).astype(BF16),
                                                  h).astype(buf.dtype)
        extra = jnp.zeros((P_CHUNK, LANES), F32)
        for k, part in enumerate(g_parts):
            term = jnp.sum(jnp.where(hit, _dot(group, part), 0.0), axis=1, keepdims=True)
            extra = jnp.where(lane == k, term, extra)
        buf[slot, pl.ds(r0, P_CHUNK), D:D + LANES] = extra.astype(buf.dtype)

    def pair_body(c, carry):
        chunk(2 * c)
        chunk(2 * c + 1)
        return carry

    lax.fori_loop(0, lax.shift_right_logical(n_chunks + 1, 1), pair_body, 0)
    copies(tau, slot, False)

    @pl.when(tau == n_t - 1)
    def _():
        @pl.when(n_t >= 2)
        def _():
            copies(tau - 1, 1 - slot, True)
        copies(tau, slot, True)
        spare_fill(True)

        sizes =[zbuf.shape[0] >> s for s in range(int(math.log2(zbuf.shape[0] // ROW_ALIGN)) + 1)]

        def fill(wait):
            def e_body(e, carry):
                start = lend_s[e]
                n = rend_s[e] - start
                off = start
                for size in sizes:
                    bit = jnp.bitwise_and(n, size)

                    @pl.when(bit != 0)
                    def _(off=off, size=size):
                        cp = pltpu.make_async_copy(
                            zbuf.at[pl.ds(0, size)],
                            xs_hbm.at[pl.ds(pl.multiple_of(off, ROW_ALIGN), size)], zsem.at[0])
                        if wait:
                            cp.wait()
                        else:
                            cp.start()

                    off = off + bit
                return carry

            lax.fori_loop(0, E, e_body, 0)

        fill(False)
        fill(True)


def _moe_sort(ptab, tot, lend, rend, gates_t, h2, loc_te, np_te, n_rows):
    E, T = gates_t.shape
    D = h2.shape[1]
    ts = SORT_TILE
    grid_spec = pltpu.PrefetchScalarGridSpec(
        num_scalar_prefetch=4,
        grid=(T // ts,),
        in_specs=[pl.BlockSpec((E, ts), lambda t, *_: (0, t)),
                  pl.BlockSpec((ts, D), lambda t, *_: (t, 0)),
                  pl.BlockSpec((None, 1, LANES), lambda t, *_: (t, 0, 0)),
                  pl.BlockSpec((None, 1, LANES), lambda t, *_: (t, 0, 0))],
        out_specs=pl.BlockSpec(memory_space=pl.ANY),
        scratch_shapes=[pltpu.VMEM((2, _local_rows_bound(ts), D + LANES), BF16),
                        pltpu.VMEM((EXP_TILE // 2, D + LANES), BF16),
                        pltpu.SemaphoreType.DMA((2,)),
                        pltpu.SemaphoreType.DMA((2,))])
    return pl.pallas_call(
        _moe_sort_kernel,
        grid_spec=grid_spec,
        out_shape=jax.ShapeDtypeStruct((n_rows, D + LANES), BF16),
        compiler_params=pltpu.CompilerParams(dimension_semantics=("arbitrary",),
                                             vmem_limit_bytes=VMEM_LIMIT),
        name="moe_sort",
    )(ptab, tot, lend, rend, gates_t, h2, loc_te, np_te)


def _moe_expert_kernel(texp_s, nused_s, x_ref, wg_ref, wu_ref, wd_ref, y_ref, wgu_s, wd_s):
    i = pl.program_id(0)
    f = wd_ref.shape[0]

    @pl.when(i < nused_s[0])
    def _():
        @pl.when((i == 0) | (texp_s[i] != texp_s[jnp.maximum(i - 1, 0)]))
        def _():
            wgu_s[:, 0:f] = wg_ref[...].astype(BF16)
            wgu_s[:, f:2 * f] = wu_ref[...].astype(BF16)
            wd_s[...] = wd_ref[...].astype(BF16)

        d = wd_ref.shape[1]
        for r0 in range(0, x_ref.shape[0], EXP_STRIP):
            rows = slice(r0, r0 + EXP_STRIP)
            gate = jnp.sum(x_ref[rows, d:].astype(F32), axis=1, keepdims=True)
            a = _dot(x_ref[rows, 0:d], wgu_s[...])
            act = _silu(a[:, :f]) * a[:, f:] * gate
            y_ref[rows, :] = _dot(act.astype(BF16), wd_s[...]).astype(y_ref.dtype)

    @pl.when(i >= nused_s[0])
    def _():
        y_ref[...] = jnp.zeros_like(y_ref)


def _moe_expert(texp, nused, xs, w_gate, w_up, w_down, n_tiles):
    n_rows, xw = xs.shape
    D, f = w_gate.shape[-2:]
    tm = EXP_TILE

    def tile(i, texp, nused):
        return jnp.maximum(jnp.minimum(i, nused[0] - 1), 0)

    grid_spec = pltpu.PrefetchScalarGridSpec(
        num_scalar_prefetch=2,
        grid=(n_tiles,),
        in_specs=[pl.BlockSpec((tm, xw), lambda i, te, nu: (tile(i, te, nu), 0)),
                  pl.BlockSpec((None, D, f), lambda i, te, nu: (te[tile(i, te, nu)], 0, 0)),
                  pl.BlockSpec((None, D, f), lambda i, te, nu: (te[tile(i, te, nu)], 0, 0)),
                  pl.BlockSpec((None, f, D), lambda i, te, nu: (te[tile(i, te, nu)], 0, 0))],
        out_specs=pl.BlockSpec((tm, D), lambda i, te, nu: (i, 0)),
        scratch_shapes=[pltpu.VMEM((D, 2 * f), BF16), pltpu.VMEM((f, D), BF16)])
    return pl.pallas_call(
        _moe_expert_kernel,
        grid_spec=grid_spec,
        out_shape=jax.ShapeDtypeStruct((n_rows, D), BF16),
        compiler_params=pltpu.CompilerParams(dimension_semantics=("arbitrary",),
                                             vmem_limit_bytes=VMEM_LIMIT),
        name="moe_expert",
    )(texp, nused, xs, w_gate, w_up, w_down)


def _moe_combine_kernel(alpha, ptab_s, tot_s, g_ref, loc_ref, np_ref, h_ref, x1_ref,
                        mod_ref, sgu_ref, sd_ref, lg_ref, lb_ref, y_hbm, o_ref, ybuf, acc_ref,
                        sem):
    tau = pl.program_id(0)
    n_t = pl.num_programs(0)
    slot = lax.rem(tau, 2)
    ts, n_lane = g_ref.shape
    E = loc_ref.shape[0]
    n_cols = _piece_cols(ts)

    def copies(tile, sl, wait):
        if wait:
            rows = pl.multiple_of(tot_s[tile], ROW_ALIGN)

            @pl.when(rows > 0)
            def _():
                pltpu.make_async_copy(y_hbm.at[pl.ds(0, rows)], ybuf.at[sl, pl.ds(0, rows)],
                                      sem.at[sl]).wait()
        else:
            def make_copy(loc, dst):
                return pltpu.make_async_copy(y_hbm.at[pl.ds(dst, ROW_ALIGN)],
                                             ybuf.at[sl, pl.ds(loc, ROW_ALIGN)], sem.at[sl])
            _start_pieces(tile, ptab_s, tot_s, n_cols, make_copy)

    @pl.when(tau == 0)
    def _():
        ybuf[...] = jnp.zeros_like(ybuf)
        copies(0, 0, False)

    @pl.when(tau + 1 < n_t)
    def _():
        copies(tau + 1, 1 - slot, False)

    g = g_ref[...]
    mask = g > 0.0
    maskb = jnp.where(mask, 1.0, 0.0).astype(BF16)
    pos = jnp.where(mask, _dot(_strict_lower(ts), maskb), -1.0).astype(BF16)
    lane = lax.broadcasted_iota(jnp.int32, loc_ref.shape, 1)
    lo_col = jnp.sum(jnp.where(lane == tau, loc_ref[...], 0.0), axis=1, keepdims=True)
    hi_col = lo_col + jnp.sum(jnp.where(lane == tau, np_ref[...], 0.0), axis=1, keepdims=True)

    f = sd_ref.shape[0]
    a = _dot(h_ref[...], sgu_ref[...])
    acc_ref[...] = _dot((_silu(a[:, :f]) * a[:, f:]).astype(BF16), sd_ref[...])

    copies(tau, slot, True)
    n_chunks = lax.shift_right_logical(tot_s[tau] + (P_CHUNK - 1), int(math.log2(P_CHUNK)))

    def chunk(c):
        r0 = pl.multiple_of(c * P_CHUNK, P_CHUNK)
        r = (r0 + lax.broadcasted_iota(jnp.int32, (E, P_CHUNK), 1)).astype(F32)
        inside = (lo_col <= r) & (r < hi_col)
        group = jnp.concatenate([jnp.where(inside, 1.0, 0.0),
                                 jnp.zeros((n_lane - E, P_CHUNK), F32)], axis=0).astype(BF16)
        want = r[0:1, :] - jnp.sum(jnp.where(inside, lo_col, 0.0), axis=0, keepdims=True)
        hit = _dot(pos, group) == want
        return _dot(jnp.where(hit, 1.0, 0.0).astype(BF16), ybuf[slot, pl.ds(r0, P_CHUNK), :])

    def pair_body(c, carry):
        acc_ref[...] += chunk(2 * c) + chunk(2 * c + 1)
        return carry

    lax.fori_loop(0, lax.shift_right_logical(n_chunks + 1, 1), pair_body, 0)
    y = alpha * x1_ref[...] + mod_ref[5:6, :] * acc_ref[...]
    o_ref[...] = _layer_norm(y, lg_ref[...], lb_ref[...])


def _moe_combine(alpha, ptab, tot, gates, loc_et, np_et, h2, x1, mod, sgu, sd, ln_g, ln_b, ys,
                 S):
    T, D = h2.shape
    ts = SORT_TILE
    per_b = S // ts
    row = lambda a: pl.BlockSpec(a.shape, lambda t, *_: (0, 0))
    grid_spec = pltpu.PrefetchScalarGridSpec(
        num_scalar_prefetch=2,
        grid=(T // ts,),
        in_specs=[pl.BlockSpec((ts, LANES), lambda t, *_: (t, 0)),
                  row(loc_et), row(np_et),
                  pl.BlockSpec((ts, D), lambda t, *_: (t, 0)),
                  pl.BlockSpec((ts, D), lambda t, *_: (t, 0)),
                  pl.BlockSpec((None, 6, D), lambda t, *_: (t // per_b, 0, 0)),
                  row(sgu), row(sd), row(ln_g), row(ln_b),
                  pl.BlockSpec(memory_space=pl.ANY)],
        out_specs=pl.BlockSpec((ts, D), lambda t, *_: (t, 0)),
        scratch_shapes=[pltpu.VMEM((2, _local_rows_bound(ts), D), BF16),
                        pltpu.VMEM((ts, D), F32),
                        pltpu.SemaphoreType.DMA((2,))])
    return pl.pallas_call(
        functools.partial(_moe_combine_kernel, alpha),
        grid_spec=grid_spec,
        out_shape=jax.ShapeDtypeStruct((T, D), F32),
        compiler_params=pltpu.CompilerParams(dimension_semantics=("arbitrary",),
                                             vmem_limit_bytes=VMEM_LIMIT),
        name="moe_combine",
    )(ptab, tot, gates, loc_et, np_et, h2, x1, mod, sgu, sd, ln_g, ln_b, ys)


def _moe(alpha, h2, x1, gates, gates_t, mod, w_gate, w_up, w_down, sgu, sd, ln_g, ln_b):
    B, S, D = x1.shape
    T = B * S
    n_t = T // SORT_TILE
    n_bound = _sorted_tiles_bound(T)
    n_tab = -(-n_bound // LANES) * LANES
    ptab, loc_et, np_et, loc_te, np_te, tot, erow, texp, nused = _moe_meta(gates_t, n_tab)
    ptab = ptab.reshape(-1)
    tot = tot[0, :n_t]
    h2 = h2.reshape(T, D)
    x1 = x1.reshape(T, D)
    gates = gates.reshape(T, LANES)
    loc_rows = loc_te[:n_t].reshape(n_t, 1, LANES)
    np_rows = np_te[:n_t].reshape(n_t, 1, LANES)

    def sized(n_tiles):
        def run():
            xs = _moe_sort(ptab, tot, erow[0, :N_EXPERTS], erow[1, :N_EXPERTS], gates_t, h2,
                           loc_rows, np_rows, n_tiles * EXP_TILE)
            ys = _moe_expert(texp[0], nused[0, :1], xs, w_gate, w_up, w_down, n_tiles)
            return _moe_combine(alpha, ptab, tot, gates, loc_et, np_et, h2, x1, mod, sgu, sd,
                                ln_g, ln_b, ys, S)
        return run

    n_small = min(_sorted_tiles_expected(T), n_bound)
    out = lax.cond(nused[0, 0] <= n_small, sized(n_small), sized(n_bound))
    return out.reshape(B, S, D)


def _rearrange_w_in(w):
    d_in = w.shape[0]
    scale = HEAD_DIM ** -0.5 * LOG2E
    fq, fk, fv = w[:, 0:512], w[:, 512:1024], w[:, 1024:1536]
    ff = w[:, 1536:1544]
    nq = w[:, 1544:2056]
    kc, vc, ks, vs, kw, vw = (w[:, 2056 + k * LANES:2056 + (k + 1) * LANES] for k in range(6))
    ng = w[:, 2824:2848]
    pad = jnp.zeros((d_in, LANES - ff.shape[1] - ng.shape[1]), w.dtype)

    def spaced(v):
        v = v.reshape(d_in, -1, HEAD_DIM)
        return jnp.concatenate([v, jnp.zeros_like(v)], axis=2).reshape(d_in, -1)

    cols = [fq * scale, fk, nq * scale, kc, kw, vc, spaced(ks), spaced(fv), spaced(vs), spaced(vw),
            ff, ng, pad]
    return jnp.concatenate(cols, axis=1).astype(BF16)


def _ones_row():
    row = np.zeros((1, _IN_COLS), np.float32)
    lanes = np.arange(_C_FV, _C_MISC)
    row[0, lanes[(lanes % LANES) >= HEAD_DIM]] = 1.0
    return row


def _compress_weights(pos, w1, w2):
    half = CMP_BLOCK // 2
    w1r = w1.reshape(2, half, HEAD_DIM, CMP_HIDDEN)
    zeros = jnp.zeros_like(w1r[0])
    def spread(part):
        g0 = jnp.stack([part, zeros], axis=1).reshape(half * 2 * HEAD_DIM, CMP_HIDDEN)
        g1 = jnp.stack([zeros, part], axis=1).reshape(half * 2 * HEAD_DIM, CMP_HIDDEN)
        return jnp.concatenate([g0, g1], axis=1).astype(BF16)
    wa, wb = spread(w1r[0]), spread(w1r[1])
    z2 = jnp.zeros_like(w2)
    w2bd = jnp.concatenate([jnp.concatenate([w2, z2], axis=1),
                            jnp.concatenate([z2, w2], axis=1)], axis=0).astype(BF16)
    posr = pos.reshape(2, half, 1, HEAD_DIM)
    posr = jnp.broadcast_to(posr, (2, half, NSA_GROUPS, HEAD_DIM)).reshape(2, half * 2 * HEAD_DIM)
    return posr, wa, wb, w2bd


@functools.lru_cache(maxsize=None)
def _static_tables(S):
    tq = ATT_TILE
    n_cmp = (S - CMP_BLOCK) // CMP_STRIDE + 1
    n_pad = S // CMP_STRIDE
    n_slc = S // SEL_BLOCK
    t = np.arange(S)[:, None]
    n = np.arange(n_pad)[None, :]
    bucket_c = _t5_bucket_np(t - (n * CMP_STRIDE + CMP_BLOCK - 1)).reshape(1, -1)
    d = (np.arange(4)[:, None, None] * tq + np.arange(tq)[None, :, None]
         - np.arange(2 * tq)[None, None, :])
    bucket_w = _t5_bucket_np(d).reshape(1, -1)
    cs = np.arange(n_pad)[None, :] * CMP_STRIDE
    sj = np.arange(n_slc)[:, None] * SEL_BLOCK
    ovl_t = ((cs < sj + SEL_BLOCK) & (cs + CMP_BLOCK > sj) & (np.arange(n_pad)[None, :] < n_cmp))
    return bucket_c, bucket_w, ovl_t.astype(np.float32)


def kernel(x, c, w_ada, b_ada, w_in, b_f, cmp_pos_k, cmp_w1_k, cmp_w2_k, cmp_pos_v, cmp_w1_v,
           cmp_w2_v, rel_bias, w_out, ln1_g, ln1_b, w_router, e_bias, w_gate, w_up, w_down,
           ws_gate, ws_up, ws_down, ln2_g, ln2_b):
    B, S, D = x.shape
    depth = w_ada.shape[0]
    alpha = (2 * depth) ** 0.25
    tq = ATT_TILE
    bucket_c, bucket_w, ovl_t = _static_tables(S)
    rel_bias_t = rel_bias.T * LOG2E
    bias_c = _bias_table(jnp.asarray(bucket_c), rel_bias_t).reshape(NSA_HEADS, S, S // CMP_STRIDE)
    w4 = _bias_table(jnp.asarray(bucket_w), rel_bias_t).reshape(NSA_HEADS, 4, tq, 2 * tq)
    ovl_t = jnp.asarray(ovl_t, BF16)

    for l in range(depth):
        mod = _ada(c, w_ada[l], b_ada[l]).reshape(B, 6, D)
        bf_row = jnp.zeros((1, LANES), F32).at[0, :FOX_HEADS].set(b_f[l])
        (fq, fk, nq, kc, kw, vc, ks, fv, vs, vw, misc, misc_t) = _in_proj(
            x, mod, _rearrange_w_in(w_in[l]), bf_row, jnp.asarray(_ones_row()))

        o_fox = _fox(fq, fk, fv, misc_t[:, :FOX_HEADS, :])

        pk, wak, wbk, w2k = _compress_weights(cmp_pos_k[l], cmp_w1_k[l], cmp_w2_k[l])
        pv, wav, wbv, w2v = _compress_weights(cmp_pos_v[l], cmp_w1_v[l], cmp_w2_v[l])
        rows = S // CMP_STRIDE
        kcmp, vcmp = _compress(kc.reshape(B, rows, CMP_STRIDE * LANES),
                               vc.reshape(B, rows, CMP_STRIDE * LANES),
                               pk, pv, wak, wbk, wav, wbv, w2k, w2v)

        oc, sel = _cmp_sel(nq, kcmp, vcmp, bias_c, misc, ovl_t)
        o_nsa = _nsa(nq, ks, vs, kw, vw, sel, w4, misc, oc)

        x1, h2, gates, gates_t = _out_proj(
            alpha, o_fox, o_nsa, x, mod, w_out[l].astype(BF16), ln1_g[l].reshape(1, D),
            ln1_b[l].reshape(1, D), w_router[l].T, e_bias[l].reshape(N_EXPERTS, 1))

        sgu = jnp.concatenate([ws_gate[l], ws_up[l]], axis=-1).astype(BF16)
        x = _moe(alpha, h2, x1, gates, gates_t, mod, w_gate[l], w_up[l], w_down[l], sgu,
                 ws_down[l].astype(BF16), ln2_g[l].reshape(1, D), ln2_b[l].reshape(1, D))
    return x
```

```python
import functools
import math

import jax
import jax.numpy as jnp
import numpy as np
from jax import lax
from jax.experimental import pallas as pl
from jax.experimental.pallas import tpu as pltpu

F32 = jnp.float32
BF16 = jnp.bfloat16

HEAD_DIM = 64
FOX_HEADS = 8
NSA_HEADS = 8
NSA_GQA = 4
NSA_GROUPS = NSA_HEADS // NSA_GQA
CMP_BLOCK = 32
CMP_STRIDE = 16
CMP_HIDDEN = 256
SEL_BLOCK = 64
N_SEL = 16
WINDOW = 512
N_BUCKETS = 32
MAX_DISTANCE = 128
N_EXPERTS = 64
N_EXPERT_GROUPS = 8
GROUP_SIZE = N_EXPERTS // N_EXPERT_GROUPS
TOPK_GROUPS = 4
TOP_K = 8
D_EXPERT = 256
ROUTED_SCALE = 2.5
LN_EPS = 1e-5
NEG_BIG = -1e30
FORCE_SCORE = 1e4

LANES = 128
ATT_TILE = 256
FOX_TILE = 512
CMP_TILE = 512
ROW_TILE = 512
MOE_TILE = 1024
VMEM_LIMIT = 48 * 1024 * 1024

NT_DIMS = (((1,), (1,)), ((), ()))


def _dot(a, b):
    return jnp.dot(a, b, preferred_element_type=F32)


def _dot_nt(a, b):
    return lax.dot_general(a, b, NT_DIMS, preferred_element_type=F32)


def _split3(x):
    hi = x.astype(BF16)
    r1 = x - hi.astype(F32)
    mid = r1.astype(BF16)
    lo = (r1 - mid.astype(F32)).astype(BF16)
    return hi, mid, lo


def _silu(x):
    return x / (1.0 + jnp.exp(-x))


def _sigmoid(x):
    return 1.0 / (1.0 + jnp.exp(-x))


def _swap_halves(x):
    return pltpu.roll(x, HEAD_DIM, 1)


def _t5_bucket_np(dist):
    n = np.maximum(dist, 0)
    max_exact = N_BUCKETS // 2
    nf = np.maximum(n, 1).astype(np.float32)
    large = max_exact + (np.log(nf / max_exact) / math.log(MAX_DISTANCE / max_exact)
                         * (N_BUCKETS - max_exact)).astype(np.int32)
    large = np.minimum(large, N_BUCKETS - 1)
    return np.where(n < max_exact, n, large).astype(np.int32)


def _ada_kernel(c_ref, w_ref, b_ref, o_ref):
    c = c_ref[...]
    o_ref[...] = jnp.dot(_silu(c), w_ref[...], preferred_element_type=F32,
                         precision=lax.Precision.HIGHEST) + b_ref[...]


def _ada(c, w_ada, b_ada):
    B, D = c.shape
    n_out = w_ada.shape[1]
    tn = 1024
    return pl.pallas_call(
        _ada_kernel,
        grid=(n_out // tn,),
        in_specs=[pl.BlockSpec((B, D), lambda j: (0, 0)),
                  pl.BlockSpec((D, tn), lambda j: (0, j)),
                  pl.BlockSpec((1, tn), lambda j: (0, j))],
        out_specs=pl.BlockSpec((B, tn), lambda j: (0, j)),
        out_shape=jax.ShapeDtypeStruct((B, n_out), F32),
        compiler_params=pltpu.CompilerParams(dimension_semantics=("arbitrary",),
                                             vmem_limit_bytes=VMEM_LIMIT),
        name="ada",
    )(c, w_ada, b_ada.reshape(1, n_out))


def _bias_table_kernel(bkt_ref, rbt_ref, o_ref):
    bkt = bkt_ref[...]
    k = lax.broadcasted_iota(jnp.int32, (N_BUCKETS, bkt.shape[1]), 0)
    onehot = jnp.where(k == bkt, 1.0, 0.0).astype(BF16)
    hi, mid, lo = _split3(rbt_ref[...])
    o_ref[...] = _dot(hi, onehot) + _dot(mid, onehot) + _dot(lo, onehot)


def _bias_table(bucket, rel_bias_t):
    n = bucket.shape[1]
    chunk = 8192
    n_heads = rel_bias_t.shape[0]
    return pl.pallas_call(
        _bias_table_kernel,
        grid=(n // chunk,),
        in_specs=[pl.BlockSpec((1, chunk), lambda j: (0, j)),
                  pl.BlockSpec(rel_bias_t.shape, lambda j: (0, 0))],
        out_specs=pl.BlockSpec((n_heads, chunk), lambda j: (0, j)),
        out_shape=jax.ShapeDtypeStruct((n_heads, n), F32),
        compiler_params=pltpu.CompilerParams(dimension_semantics=("parallel",),
                                             vmem_limit_bytes=VMEM_LIMIT),
        name="bias_table",
    )(bucket, rel_bias_t)


_C_FQ, _C_FK, _C_NQ = 0, 512, 1024
_C_K3 = 1536
_C_SK = 1920
_C_FV = 2176
_C_SV = 3200
_C_WV = 3456
_C_MISC = 3712
_IN_COLS = 3840
LOG2E = math.log2(math.e)


def _in_proj_kernel(x_ref, mod_ref, w_ref, bf_ref, ones_ref, fq_ref, fk_ref, nq_ref, kc_ref,
                    kw_ref, vc_ref, ks_ref, fv_ref, vs_ref, vw_ref, misc_ref, misct_ref,
                    carry_ref):
    s_idx = pl.program_id(1)
    tm = x_ref.shape[0]
    mod = mod_ref[...]
    h = (x_ref[...] * (1.0 + mod[1:2, :]) + mod[0:1, :]).astype(BF16)

    for ref, c0 in ((fq_ref, _C_FQ), (fk_ref, _C_FK), (nq_ref, _C_NQ)):
        ref[...] = _dot(h, w_ref[:, c0:c0 + 512]).astype(ref.dtype)
    for k, ref in enumerate((kc_ref, kw_ref, vc_ref)):
        c0 = _C_K3 + k * LANES
        ref[...] = _dot(h, w_ref[:, c0:c0 + LANES]).astype(ref.dtype)
    w_sk = ks_ref.shape[-1]
    key_blk = lax.shift_right_logical(
        s_idx * tm + lax.broadcasted_iota(jnp.int32, (tm, w_sk), 0), int(math.log2(SEL_BLOCK)))
    lane_in_slab = jnp.bitwise_and(lax.broadcasted_iota(jnp.int32, (tm, w_sk), 1), LANES - 1)
    blk_hot = jnp.where(lane_in_slab == HEAD_DIM + key_blk, 1.0, 0.0)
    ks_ref[...] = (_dot(h, w_ref[:, _C_SK:_C_SK + w_sk]) + blk_hot).astype(ks_ref.dtype)
    for ref, c0 in ((fv_ref, _C_FV), (vs_ref, _C_SV), (vw_ref, _C_WV)):
        c1 = c0 + ref.shape[-1]
        ref[...] = (_dot(h, w_ref[:, c0:c1]) + ones_ref[:, c0:c1]).astype(ref.dtype)

    z = _dot(h, w_ref[:, _C_MISC:_C_MISC + LANES]) + bf_ref[...]
    lane = lax.broadcasted_iota(jnp.int32, z.shape, 1)
    is_f = lane < FOX_HEADS
    log_f = jnp.minimum(z, 0.0) - jnp.log(1.0 + jnp.exp(-jnp.abs(z)))
    log_f = jnp.where(is_f, log_f, 0.0)

    row = lax.broadcasted_iota(jnp.int32, (tm, tm), 0)
    col = lax.broadcasted_iota(jnp.int32, (tm, tm), 1)
    tri = jnp.where(row >= col, 1.0, 0.0).astype(BF16)
    hi, mid, lo = _split3(log_f)
    local = _dot(tri, hi) + _dot(tri, mid) + _dot(tri, lo)

    @pl.when(s_idx == 0)
    def _():
        carry_ref[...] = jnp.zeros_like(carry_ref)

    cum = local + carry_ref[...]
    carry_ref[...] = cum[tm - 1:tm, :]
    misc = jnp.where(is_f, cum * LOG2E, _sigmoid(z))
    misc_ref[...] = misc
    misct_ref[...] = misc.T


def _in_proj(x, mod, w_r, bf_row, ones_row):
    B, S, D = x.shape
    tm = ROW_TILE
    widths = (512, 512, 512, LANES, LANES, LANES, _C_FV - _C_SK, _C_SV - _C_FV, _C_WV - _C_SV,
              _C_MISC - _C_WV)
    wide = lambda w: pl.BlockSpec((None, tm, w), lambda b, s: (b, s, 0))
    out_shape = ([jax.ShapeDtypeStruct((B, S, w), BF16) for w in widths]
                 + [jax.ShapeDtypeStruct((B, S, LANES), F32),
                    jax.ShapeDtypeStruct((B, LANES, S), F32)])
    out_specs = ([wide(w) for w in widths]
                 + [wide(LANES), pl.BlockSpec((None, LANES, tm), lambda b, s: (b, 0, s))])
    return pl.pallas_call(
        _in_proj_kernel,
        grid=(B, S // tm),
        in_specs=[pl.BlockSpec((None, tm, D), lambda b, s: (b, s, 0)),
                  pl.BlockSpec((None, 6, D), lambda b, s: (b, 0, 0)),
                  pl.BlockSpec((D, _IN_COLS), lambda b, s: (0, 0)),
                  pl.BlockSpec((1, LANES), lambda b, s: (0, 0)),
                  pl.BlockSpec((1, _IN_COLS), lambda b, s: (0, 0))],
        out_specs=out_specs,
        out_shape=out_shape,
        scratch_shapes=[pltpu.VMEM((1, LANES), F32)],
        compiler_params=pltpu.CompilerParams(dimension_semantics=("parallel", "arbitrary"),
                                             vmem_limit_bytes=VMEM_LIMIT),
        name="in_proj",
    )(x, mod, w_r, bf_row, ones_row)


def _softmax_weights(s, m):
    return jnp.exp2((s - m).astype(BF16))


def _flash_update(carry, s, vt):
    m, acc = carry
    m_new = jnp.maximum(m, jnp.max(s, axis=-1, keepdims=True))
    alpha = jnp.exp2(m - m_new)
    p = _softmax_weights(s, m_new)
    rows = acc.shape[0]
    acc = alpha.reshape(rows, 1) * acc + _dot(p.reshape(rows, s.shape[-1]), vt)
    return m_new, acc


def _normalize(acc):
    return acc / _swap_halves(acc)


def _fox_kernel(q_ref, k_ref, v_ref, ck_ref, o_ref):
    i = pl.program_id(2)
    tq = q_ref.shape[0]
    tk = ck_ref.shape[2]
    q2 = q_ref[...].astype(F32)
    lane = lax.broadcasted_iota(jnp.int32, (tq, LANES), 1)
    low = lane < HEAD_DIM
    halves = (low, jnp.logical_not(low))
    qh = [jnp.where(h, q2, 0.0).astype(BF16) for h in halves]
    col_minus_row = (lax.broadcasted_iota(jnp.int32, (tq, tk), 1)
                     - lax.broadcasted_iota(jnp.int32, (tq, tk), 0))

    def step(jj, carry, diagonal):
        k0 = pl.multiple_of(jj * tk, tk)
        kt = k_ref[pl.ds(k0, tk), :]
        new = []
        for hh in range(2):
            s = _dot_nt(qh[hh], kt) - ck_ref[hh, pl.ds(jj, 1), :]
            if diagonal:
                s = jnp.where(col_minus_row <= i * tq - jj * tk, s, NEG_BIG)
            vt = v_ref[pl.ds(k0, tk), hh * LANES:(hh + 1) * LANES]
            new.append(_flash_update(carry[hh], s, vt))
        return tuple(new)

    init = tuple((jnp.full((tq, 1), NEG_BIG, F32), jnp.zeros((tq, LANES), F32))
                 for _ in range(2))
    n_full = (i * tq) // tk
    carry = lax.fori_loop(0, n_full, lambda jj, c: step(jj, c, False), init)
    carry = step(n_full, carry, True)
    o_ref[...] = jnp.where(low, _normalize(carry[0][1]),
                           _swap_halves(_normalize(carry[1][1]))).astype(o_ref.dtype)


def _fox(fq, fk, fv, cum_row):
    B, S, W = fq.shape
    tq = tk = FOX_TILE
    n_pairs = W // LANES
    cum_row = cum_row.reshape(B, n_pairs, 2, S // tk, tk)
    return pl.pallas_call(
        _fox_kernel,
        grid=(B, n_pairs, S // tq),
        in_specs=[pl.BlockSpec((None, tq, LANES), lambda b, p, i: (b, i, p)),
                  pl.BlockSpec((None, S, LANES), lambda b, p, i: (b, 0, p)),
                  pl.BlockSpec((None, S, 2 * LANES), lambda b, p, i: (b, 0, p)),
                  pl.BlockSpec((None, None, 2, S // tk, tk), lambda b, p, i: (b, p, 0, 0, 0))],
        out_specs=pl.BlockSpec((None, tq, LANES), lambda b, p, i: (b, i, p)),
        out_shape=jax.ShapeDtypeStruct((B, S, W), BF16),
        compiler_params=pltpu.CompilerParams(
            dimension_semantics=("parallel", "parallel", "arbitrary"),
            vmem_limit_bytes=VMEM_LIMIT),
        name="fox",
    )(fq, fk, fv, cum_row)


def _compress_kernel(xk_ref, xv_ref, pk_ref, pv_ref, wak_ref, wbk_ref, wav_ref, wbv_ref,
                     w2k_ref, w2v_ref, ok_ref, ov_ref):
    n_rows = xk_ref.shape[0]
    for x_ref, p_ref, wa_ref, wb_ref, w2_ref, o_ref in (
            (xk_ref, pk_ref, wak_ref, wbk_ref, w2k_ref, ok_ref),
            (xv_ref, pv_ref, wav_ref, wbv_ref, w2v_ref, ov_ref)):
        x = x_ref[...].astype(F32)
        xa = (x + p_ref[0:1, :]).astype(BF16)
        xb = (x + p_ref[1:2, :]).astype(BF16)
        hb = _dot(xb, wb_ref[...])
        h1 = _dot(xa, wa_ref[...]) + pltpu.roll(hb, n_rows - 1, 0)
        o_ref[...] = _dot(_silu(h1).astype(BF16), w2_ref[...]).astype(o_ref.dtype)


def _compress(xk, xv, pk, pv, wak, wbk, wav, wbv, w2k, w2v):
    B, R, C = xk.shape
    xspec = pl.BlockSpec((None, R, C), lambda b: (b, 0, 0))
    full = lambda a: pl.BlockSpec(a.shape, lambda b: (0,) * a.ndim)
    ospec = pl.BlockSpec((None, R, LANES), lambda b: (b, 0, 0))
    return pl.pallas_call(
        _compress_kernel,
        grid=(B,),
        in_specs=[xspec, xspec] + [full(a) for a in (pk, pv, wak, wbk, wav, wbv, w2k, w2v)],
        out_specs=[ospec, ospec],
        out_shape=[jax.ShapeDtypeStruct((B, R, LANES), BF16)] * 2,
        compiler_params=pltpu.CompilerParams(dimension_semantics=("parallel",),
                                             vmem_limit_bytes=VMEM_LIMIT),
        name="compress",
    )(xk, xv, pk, pv, wak, wbk, wav, wbv, w2k, w2v)


def _rank_rows(score):
    n = score.shape[0]
    j = lax.broadcasted_iota(jnp.int32, score.shape, 0)
    rank = jnp.zeros(score.shape, jnp.int32)
    for i in range(n):
        si = score[i:i + 1, :]
        beats = (si > score) | ((si == score) & (j > i))
        rank = rank + jnp.where(beats, 1, 0)
    return rank


def _dup_head(q4, r):
    pair = q4[:, (r // 2) * LANES:(r // 2 + 1) * LANES].astype(F32)
    lane = lax.broadcasted_iota(jnp.int32, pair.shape, 1)
    swapped = _swap_halves(pair)
    if r % 2 == 0:
        return jnp.where(lane < HEAD_DIM, pair, swapped)
    return jnp.where(lane < HEAD_DIM, swapped, pair)


def _pack_heads(o_list, g):
    lane = lax.broadcasted_iota(jnp.int32, o_list[0].shape, 1)
    in_g = (lane >= g * HEAD_DIM) & (lane < (g + 1) * HEAD_DIM)
    both = []
    for o in o_list:
        om = jnp.where(in_g, o, 0.0)
        both.append(om + _swap_halves(om))
    pairs = [jnp.where(lane < HEAD_DIM, both[2 * p], both[2 * p + 1]) for p in range(2)]
    return jnp.concatenate(pairs, axis=1)


def _group_gates(misc, g):
    w = 3 * NSA_GQA
    gates = misc[:, FOX_HEADS:FOX_HEADS + w]
    for other in range(1, NSA_GROUPS):
        gates = jnp.where(g == other, misc[:, FOX_HEADS + other * w:FOX_HEADS + (other + 1) * w],
                          gates)
    return gates


def _cmp_sel_kernel(q_ref, kc_ref, vc_ref, bias_ref, gate_ref, ovl_ref, oc_ref, sel_ref):
    g = pl.program_id(1)
    i = pl.program_id(2)
    tq = q_ref.shape[0]
    n_pad = kc_ref.shape[0]
    q4 = q_ref[...]
    lane = lax.broadcasted_iota(jnp.int32, (n_pad, LANES), 1)
    in_g = (lane >= g * HEAD_DIM) & (lane < (g + 1) * HEAD_DIM)
    kc = jnp.where(in_g, kc_ref[...].astype(F32), 0.0).astype(BF16)
    vc = vc_ref[...]
    t = i * tq + lax.broadcasted_iota(jnp.int32, (tq, n_pad), 0)
    n = lax.broadcasted_iota(jnp.int32, (tq, n_pad), 1)
    valid = t >= n * CMP_STRIDE + (CMP_BLOCK - 1)
    gates = _group_gates(gate_ref[...], g)
    p_sum = jnp.zeros((tq, n_pad), F32)
    outs = []
    for r in range(NSA_GQA):
        qr = _dup_head(q4, r).astype(BF16)
        s = _dot_nt(qr, kc)
        s = jnp.where(valid, s + bias_ref[r], NEG_BIG)
        m = jnp.max(s, axis=-1, keepdims=True)
        p = jnp.exp2(s - m)
        p = p / jnp.sum(p, axis=-1, keepdims=True)
        p = jnp.where(valid, p, 0.0)
        p_sum = p_sum + p
        outs.append(_dot(p.astype(BF16), vc) * gates[:, 3 * r:3 * r + 1])
    oc_ref[...] = _pack_heads(outs, g).astype(oc_ref.dtype)

    ovl = ovl_ref[...]
    hi, mid, lo = _split3(p_sum)
    imp = _dot_nt(ovl, hi) + _dot_nt(ovl, mid) + _dot_nt(ovl, lo)
    n_blk = imp.shape[0]
    j = lax.broadcasted_iota(jnp.int32, (n_blk, tq), 0)
    qb = jnp.right_shift(i * tq + lax.broadcasted_iota(jnp.int32, (n_blk, tq), 1),
                         int(math.log2(SEL_BLOCK)))
    forced = (j == 0) | (j == qb) | (j == qb - 1)
    causal = j <= qb
    score = jnp.where(causal, imp + jnp.where(forced, FORCE_SCORE, 0.0), -FORCE_SCORE)
    chosen = (_rank_rows(score) < N_SEL) & causal
    sel = jnp.where(chosen, 1.0, 0.0)
    sel = jnp.concatenate([sel, jnp.zeros((LANES - n_blk, tq), F32)], axis=0)
    sel_ref[...] = sel.T


def _cmp_sel(nq, kcmp, vcmp, bias_c, gates_g, ovl_t):
    B, S, _ = nq.shape
    tq = CMP_TILE
    n_pad = kcmp.shape[1]
    return pl.pallas_call(
        _cmp_sel_kernel,
        grid=(B, NSA_GROUPS, S // tq),
        in_specs=[pl.BlockSpec((None, tq, 2 * LANES), lambda b, g, i: (b, i, g)),
                  pl.BlockSpec((None, n_pad, LANES), lambda b, g, i: (b, 0, 0)),
                  pl.BlockSpec((None, n_pad, LANES), lambda b, g, i: (b, 0, 0)),
                  pl.BlockSpec((NSA_GQA, tq, n_pad), lambda b, g, i: (g, i, 0)),
                  pl.BlockSpec((None, tq, LANES), lambda b, g, i: (b, i, 0)),
                  pl.BlockSpec(ovl_t.shape, lambda b, g, i: (0, 0))],
        out_specs=[pl.BlockSpec((None, tq, 2 * LANES), lambda b, g, i: (b, i, g)),
                   pl.BlockSpec((None, None, tq, LANES), lambda b, g, i: (b, g, i, 0))],
        out_shape=[jax.ShapeDtypeStruct((B, S, NSA_HEADS * HEAD_DIM), BF16),
                   jax.ShapeDtypeStruct((B, NSA_GROUPS, S, LANES), F32)],
        compiler_params=pltpu.CompilerParams(
            dimension_semantics=("parallel", "parallel", "arbitrary"),
            vmem_limit_bytes=VMEM_LIMIT),
        name="cmp_sel",
    )(nq, kcmp, vcmp, bias_c, gates_g, ovl_t)


def _nsa_kernel(q_ref, ks_ref, vs_ref, kw_ref, vw_ref, sel_ref, w4_ref, gate_ref, oc_ref,
                o_ref):
    g = pl.program_id(1)
    i = pl.program_id(2)
    tq = q_ref.shape[0]
    tk = w4_ref.shape[-1]
    H = NSA_GQA
    q4 = q_ref[...]
    lane = lax.broadcasted_iota(jnp.int32, (tq, LANES), 1)
    in_g = (lane >= g * HEAD_DIM) & (lane < (g + 1) * HEAD_DIM)
    heads = [_dup_head(q4, r) for r in range(H)]
    qs = jnp.concatenate([jnp.where(in_g, hd, 0.0).astype(BF16) for hd in heads], axis=0)

    blk_mask = _swap_halves((sel_ref[...] - 1.0) * (-NEG_BIG))
    qsel = jnp.concatenate(
        [jnp.where(lane < HEAD_DIM, hd, blk_mask).astype(BF16) for hd in heads], axis=0)

    def sel_step(jj, carry, diagonal):
        k0 = pl.multiple_of(jj * tk, tk)
        dd = jnp.minimum(i - 2 * jj, 3)
        s = (_dot_nt(qsel, ks_ref[pl.ds(k0, tk), :]).reshape(H, tq, tk)
             + w4_ref[:, pl.ds(dd, 1)].reshape(H, tq, tk))
        if diagonal:
            cmr = (lax.broadcasted_iota(jnp.int32, (tq, tk), 1)
                   - lax.broadcasted_iota(jnp.int32, (tq, tk), 0))
            s = jnp.where((cmr <= i * tq - jj * tk)[None], s, NEG_BIG)
        return _flash_update(carry, s, vs_ref[pl.ds(k0, tk), :])

    init = (jnp.full((H, tq, 1), NEG_BIG, F32), jnp.zeros((H * tq, LANES), F32))
    n_full = (i * tq) // tk
    carry = lax.fori_loop(0, n_full, lambda jj, c: sel_step(jj, c, False), init)
    _, acc_s = sel_step(n_full, carry, True)

    tiles = (jnp.maximum(i - 2, 0), jnp.maximum(i - 1, 0), i)
    starts = [pl.multiple_of(j * tq, tq) for j in tiles]
    kt = jnp.concatenate([kw_ref[pl.ds(st, tq), :] for st in starts], axis=0)
    vt = jnp.concatenate([vw_ref[pl.ds(st, tq), :] for st in starts], axis=0)
    bias_w = jnp.concatenate([w4_ref[:, 2], w4_ref[:, 0, :, 0:tq]], axis=-1)
    col = lax.broadcasted_iota(jnp.int32, (tq, 3 * tq), 1)
    cmr = col - lax.broadcasted_iota(jnp.int32, (tq, 3 * tq), 0)
    never = 4 * tq
    valid = (((col < tq) & (cmr > jnp.where(i >= 2, 0, never)))
             | ((col >= jnp.where(i >= 1, tq, never)) & (col < 2 * tq))
             | ((col >= 2 * tq) & (cmr <= 2 * tq)))
    s = _dot_nt(qs, kt).reshape(H, tq, 3 * tq) + bias_w
    s = jnp.where(valid[None], s, NEG_BIG)
    p = _softmax_weights(s, jnp.max(s, axis=-1, keepdims=True))
    acc_w = _dot(p.reshape(H * tq, 3 * tq), vt)

    gates = _group_gates(gate_ref[...], g)
    o_s = _normalize(acc_s)
    o_w = _normalize(acc_w)
    outs = []
    for r in range(H):
        sl = slice(r * tq, (r + 1) * tq)
        outs.append(o_s[sl] * gates[:, 3 * r + 1:3 * r + 2] + o_w[sl] * gates[:, 3 * r + 2:3 * r + 3])
    lane = lax.broadcasted_iota(jnp.int32, (tq, LANES), 1)
    pairs = [jnp.where(lane < HEAD_DIM, outs[2 * p], _swap_halves(outs[2 * p + 1]))
             for p in range(H // 2)]
    o_ref[...] = (jnp.concatenate(pairs, axis=1) + oc_ref[...].astype(F32)).astype(o_ref.dtype)


def _nsa(nq, ks, vs, kw, vw, sel, w4, misc, oc):
    B, S, _ = nq.shape
    tq = ATT_TILE
    tk = w4.shape[-1]
    both = lambda: pl.BlockSpec((None, S, LANES), lambda b, g, i: (b, 0, 0))
    mine = lambda: pl.BlockSpec((None, S, LANES), lambda b, g, i: (b, 0, g))
    return pl.pallas_call(
        _nsa_kernel,
        grid=(B, NSA_GROUPS, S // tq),
        in_specs=[pl.BlockSpec((None, tq, 2 * LANES), lambda b, g, i: (b, i, g)),
                  mine(), mine(), both(), mine(),
                  pl.BlockSpec((None, None, tq, LANES), lambda b, g, i: (b, g, i, 0)),
                  pl.BlockSpec((NSA_GQA, 4, tq, tk), lambda b, g, i: (g, 0, 0, 0)),
                  pl.BlockSpec((None, tq, LANES), lambda b, g, i: (b, i, 0)),
                  pl.BlockSpec((None, tq, 2 * LANES), lambda b, g, i: (b, i, g))],
        out_specs=pl.BlockSpec((None, tq, 2 * LANES), lambda b, g, i: (b, i, g)),
        out_shape=jax.ShapeDtypeStruct((B, S, NSA_HEADS * HEAD_DIM), BF16),
        compiler_params=pltpu.CompilerParams(
            dimension_semantics=("parallel", "parallel", "arbitrary"),
            vmem_limit_bytes=VMEM_LIMIT),
        name="nsa",
    )(nq, ks, vs, kw, vw, sel, w4, misc, oc)


def _layer_norm(y, g, b):
    mu = jnp.mean(y, axis=-1, keepdims=True)
    yc = y - mu
    var = jnp.mean(yc * yc, axis=-1, keepdims=True)
    return yc * lax.rsqrt(var + LN_EPS) * g + b


def _top_rows(score, k):
    n = score.shape[0]
    idx = lax.broadcasted_iota(jnp.int32, score.shape, 0).astype(F32)
    alive = jnp.ones(score.shape, F32)
    for _ in range(k):
        live = alive > 0.0
        best = jnp.max(jnp.where(live, score, -jnp.inf), axis=0, keepdims=True)
        first = jnp.min(jnp.where(live & (score == best), idx, float(n)), axis=0, keepdims=True)
        alive = jnp.where(idx == first, 0.0, alive)
    return alive == 0.0


def _router_gates_t(h2, wr_t, eb_col):
    tm = h2.shape[0]
    h_hi, h_lo, _ = _split3(h2)
    w_hi, w_lo, _ = _split3(wr_t)
    logit = _dot_nt(w_hi, h_hi) + _dot_nt(w_hi, h_lo) + _dot_nt(w_lo, h_hi)
    scores = _sigmoid(logit)
    biased = scores + eb_col
    e_in = lax.broadcasted_iota(jnp.int32, (GROUP_SIZE, tm), 0).astype(F32)
    gs_rows = []
    for gi in range(N_EXPERT_GROUPS):
        grp = biased[gi * GROUP_SIZE:(gi + 1) * GROUP_SIZE, :]
        m1 = jnp.max(grp, axis=0, keepdims=True)
        first = jnp.min(jnp.where(grp == m1, e_in, float(GROUP_SIZE)), axis=0, keepdims=True)
        m2 = jnp.max(jnp.where(e_in == first, -jnp.inf, grp), axis=0, keepdims=True)
        gs_rows.append(m1 + m2)
    gscore = jnp.concatenate(gs_rows, axis=0)
    g_keep = _rank_rows(gscore) < TOPK_GROUPS
    keep = jnp.concatenate(
        [jnp.broadcast_to(g_keep[gi:gi + 1, :], (GROUP_SIZE, tm)) for gi in range(N_EXPERT_GROUPS)],
        axis=0)
    masked = jnp.where(keep, biased, -jnp.inf)
    chosen = _top_rows(masked, TOP_K)
    w = jnp.where(chosen, scores, 0.0)
    return w / jnp.sum(w, axis=0, keepdims=True) * ROUTED_SCALE


def _out_proj_kernel(alpha, of_ref, on_ref, x_ref, mod_ref, w_ref, lg_ref, lb_ref, wr_ref,
                     eb_ref, x1_ref, h2_ref, gate_ref, gate_t_ref):
    half = of_ref.shape[1]
    mod = mod_ref[...]
    mixed = _dot(of_ref[...], w_ref[0:half, :]) + _dot(on_ref[...], w_ref[half:2 * half, :])
    y = alpha * x_ref[...] + mod[2:3, :] * mixed
    x1 = _layer_norm(y, lg_ref[...], lb_ref[...])
    x1_ref[...] = x1
    h2 = x1 * (1.0 + mod[4:5, :]) + mod[3:4, :]
    h2_ref[...] = h2.astype(h2_ref.dtype)
    gates_t = _router_gates_t(h2, wr_ref[...], eb_ref[...])
    gate_t_ref[...] = gates_t
    tm = h2.shape[0]
    gates_t = jnp.concatenate([gates_t, jnp.zeros((LANES - N_EXPERTS, tm), F32)], axis=0)
    gate_ref[...] = gates_t.T


def _out_proj(alpha, o_fox, o_nsa, x, mod, w_out, ln_g, ln_b, wr_t, eb_col):
    B, S, D = x.shape
    tm = ROW_TILE
    half = o_fox.shape[-1]
    row = lambda a: pl.BlockSpec(a.shape, lambda b, s: (0, 0))
    return pl.pallas_call(
        functools.partial(_out_proj_kernel, alpha),
        grid=(B, S // tm),
        in_specs=[pl.BlockSpec((None, tm, half), lambda b, s: (b, s, 0)),
                  pl.BlockSpec((None, tm, half), lambda b, s: (b, s, 0)),
                  pl.BlockSpec((None, tm, D), lambda b, s: (b, s, 0)),
                  pl.BlockSpec((None, 6, D), lambda b, s: (b, 0, 0)),
                  row(w_out), row(ln_g), row(ln_b), row(wr_t), row(eb_col)],
        out_specs=[pl.BlockSpec((None, tm, D), lambda b, s: (b, s, 0)),
                   pl.BlockSpec((None, tm, D), lambda b, s: (b, s, 0)),
                   pl.BlockSpec((None, tm, LANES), lambda b, s: (b, s, 0)),
                   pl.BlockSpec((N_EXPERTS, tm), lambda b, s: (0, b * (S // tm) + s))],
        out_shape=[jax.ShapeDtypeStruct((B, S, D), F32),
                   jax.ShapeDtypeStruct((B, S, D), BF16),
                   jax.ShapeDtypeStruct((B, S, LANES), F32),
                   jax.ShapeDtypeStruct((N_EXPERTS, B * S), F32)],
        compiler_params=pltpu.CompilerParams(dimension_semantics=("parallel", "parallel"),
                                             vmem_limit_bytes=VMEM_LIMIT),
        name="out_proj",
    )(o_fox, o_nsa, x, mod, w_out, ln_g, ln_b, wr_t, eb_col)


SORT_TILE = 256
ROW_ALIGN = 16
EXP_TILE = 1024
EXP_STRIP = 256
P_CHUNK = 256


def _strict_upper(n):
    return jnp.where(lax.broadcasted_iota(jnp.int32, (n, n), 0)
                     < lax.broadcasted_iota(jnp.int32, (n, n), 1), 1.0, 0.0).astype(BF16)


def _strict_lower(n):
    return jnp.where(lax.broadcasted_iota(jnp.int32, (n, n), 1)
                     < lax.broadcasted_iota(jnp.int32, (n, n), 0), 1.0, 0.0).astype(BF16)


def _local_rows_bound(ts):
    rows = TOP_K * ts + N_EXPERTS * (ROW_ALIGN - 1)
    return -(-rows // P_CHUNK) * P_CHUNK


def _piece_cols(ts):
    return -(-(_local_rows_bound(ts) // ROW_ALIGN) // LANES) * LANES


def _sorted_tiles_bound(T):
    rows = TOP_K * T + (T // SORT_TILE) * N_EXPERTS * (ROW_ALIGN - 1)
    return -(-rows // EXP_TILE) + N_EXPERTS


def _sorted_tiles_expected(T):
    groups = (T // SORT_TILE) * N_EXPERTS
    rows = TOP_K * T + groups * ((ROW_ALIGN - 1) / 2 + 2)
    return int(-(-rows // EXP_TILE) + math.ceil(0.65 * N_EXPERTS))


def _moe_meta_kernel(gt_ref, ptab_ref, loc_et_ref, np_et_ref, loc_te_ref, np_te_ref, tot_ref,
                     erow_ref, texp_ref, nused_ref):
    E, T = gt_ref.shape
    mask = jnp.where(gt_ref[...] > 0.0, 1.0, 0.0).astype(BF16)
    t_id = lax.shift_right_logical(lax.broadcasted_iota(jnp.int32, (T, LANES), 0),
                                   int(math.log2(SORT_TILE)))
    tind = jnp.where(t_id == lax.broadcasted_iota(jnp.int32, (T, LANES), 1), 1.0, 0.0)
    cnt = _dot(mask, tind.astype(BF16))
    n16 = jnp.floor((cnt + (ROW_ALIGN - 1.0)) * (1.0 / ROW_ALIGN))
    n16b = n16.astype(BF16)
    q = EXP_TILE // ROW_ALIGN
    len16 = jnp.sum(n16, axis=1, keepdims=True)
    pad16 = jnp.floor((len16 + (q - 1.0)) * (1.0 / q)) * q
    sl = _strict_lower(E)
    hi, mid, lo = _split3(jnp.broadcast_to(pad16, (E, LANES)))
    start16 = _dot(sl, hi) + _dot(sl, mid) + _dot(sl, lo)
    gdst16 = start16 + _dot(n16b, _strict_upper(LANES))
    loc16 = _dot(sl, n16b)

    def t(a):
        return jnp.concatenate([a, jnp.zeros((LANES - E, LANES), F32)], axis=0).T

    scale = float(ROW_ALIGN)
    loc_et_ref[...] = loc16 * scale
    np_et_ref[...] = n16 * scale
    loc_te_ref[...] = t(loc16) * scale
    np_te_ref[...] = t(n16) * scale
    tot_ref[...] = (jnp.sum(n16, axis=0, keepdims=True) * scale).astype(jnp.int32)

    n_t, n_blk = ptab_ref.shape
    blk = lax.broadcasted_iota(jnp.int32, (E, n_blk), 1).astype(F32)
    for tile in range(n_t):
        lo_c = loc16[:, tile:tile + 1]
        inside = (lo_c <= blk) & (blk < lo_c + n16[:, tile:tile + 1])
        dst = jnp.sum(jnp.where(inside, gdst16[:, tile:tile + 1] + (blk - lo_c), 0.0),
                      axis=0, keepdims=True)
        ptab_ref[tile:tile + 1, :] = (dst * scale).astype(jnp.int32)
    ends = jnp.concatenate([t(start16 + len16)[0:1, :], t(start16 + pad16)[0:1, :],
                            jnp.zeros((erow_ref.shape[0] - 2, LANES), F32)], axis=0)
    erow_ref[...] = (ends * scale).astype(jnp.int32)
    n_tab = texp_ref.shape[1]
    tile_row16 = (lax.broadcasted_iota(jnp.int32, (E, n_tab), 1) * q).astype(F32)
    owner = jnp.sum(jnp.where(start16[:, 0:1] <= tile_row16, 1.0, 0.0), axis=0, keepdims=True)
    texp_ref[...] = (owner - 1.0).astype(jnp.int32)
    n_used = jnp.sum(pad16, axis=0, keepdims=True) * (1.0 / q)
    nused_ref[...] = jnp.broadcast_to(n_used, (1, LANES)).astype(jnp.int32)


def _moe_meta(gates_t, n_tab):
    E, T = gates_t.shape
    i32 = jnp.int32
    return pl.pallas_call(
        _moe_meta_kernel,
        out_shape=[jax.ShapeDtypeStruct((T // SORT_TILE, _piece_cols(SORT_TILE)), i32),
                   jax.ShapeDtypeStruct((E, LANES), F32),
                   jax.ShapeDtypeStruct((E, LANES), F32),
                   jax.ShapeDtypeStruct((LANES, LANES), F32),
                   jax.ShapeDtypeStruct((LANES, LANES), F32),
                   jax.ShapeDtypeStruct((1, LANES), i32),
                   jax.ShapeDtypeStruct((8, LANES), i32),
                   jax.ShapeDtypeStruct((1, n_tab), i32),
                   jax.ShapeDtypeStruct((1, LANES), i32)],
        compiler_params=pltpu.CompilerParams(vmem_limit_bytes=VMEM_LIMIT),
        name="moe_meta",
    )(gates_t)


def _start_pieces(tile, ptab_s, tot_s, n_cols, make_copy):
    n_pieces = lax.shift_right_logical(tot_s[tile], int(math.log2(ROW_ALIGN)))

    def body(b, carry):
        make_copy(pl.multiple_of(b * ROW_ALIGN, ROW_ALIGN),
                  pl.multiple_of(ptab_s[tile * n_cols + b], ROW_ALIGN)).start()
        return carry

    lax.fori_loop(0, n_pieces, body, 0)


def _moe_sort_kernel(ptab_s, tot_s, lend_s, rend_s, gt_ref, gtok_ref, h_ref, locrow_ref,
                     nprow_ref, xs_hbm, buf, zbuf, sem, zsem):
    tau = pl.program_id(0)
    n_t = pl.num_programs(0)
    slot = lax.rem(tau, 2)
    E, ts = gt_ref.shape
    D = h_ref.shape[1]
    n_cols = _piece_cols(ts)

    def copies(tile, sl, wait):
        if wait:
            rows = pl.multiple_of(tot_s[tile], ROW_ALIGN)

            @pl.when(rows > 0)
            def _():
                pltpu.make_async_copy(buf.at[sl, pl.ds(0, rows)], xs_hbm.at[pl.ds(0, rows)],
                                      sem.at[sl]).wait()
        else:
            def make_copy(loc, dst):
                return pltpu.make_async_copy(buf.at[sl, pl.ds(loc, ROW_ALIGN)],
                                             xs_hbm.at[pl.ds(dst, ROW_ALIGN)], sem.at[sl])
            _start_pieces(tile, ptab_s, tot_s, n_cols, make_copy)

    z_rows = zbuf.shape[0]
    used_rows = rend_s[E - 1]
    n_spare = (xs_hbm.shape[0] - used_rows) // z_rows

    def spare_fill(wait):
        def body(c, carry):
            dst = pl.multiple_of(used_rows + c * z_rows, z_rows)
            cp = pltpu.make_async_copy(zbuf, xs_hbm.at[pl.ds(dst, z_rows)], zsem.at[1])
            if wait:
                cp.wait()
            else:
                cp.start()
            return carry

        lax.fori_loop(0, n_spare, body, 0)

    @pl.when(tau == 0)
    def _():
        zbuf[...] = jnp.zeros_like(zbuf)
        spare_fill(False)

    @pl.when(tau >= 2)
    def _():
        copies(tau - 2, slot, True)

    g = gt_ref[...]
    mask = g > 0.0
    maskb = jnp.where(mask, 1.0, 0.0).astype(BF16)
    pad = jnp.zeros((LANES - E, ts), F32)
    pos = jnp.where(mask, _dot(maskb, _strict_upper(ts)), -1.0)
    pos = jnp.concatenate([pos, pad], axis=0).astype(BF16)
    lo_row = locrow_ref[...]
    hi_row = lo_row + nprow_ref[...]
    h = jnp.concatenate([h_ref[...]] + list(_split3(gtok_ref[...])), axis=1)
    n_chunks = lax.shift_right_logical(tot_s[tau] + (P_CHUNK - 1), int(math.log2(P_CHUNK)))
    lane = lax.broadcasted_iota(jnp.int32, (P_CHUNK, LANES), 1)

    def chunk(c):
        r0 = pl.multiple_of(c * P_CHUNK, P_CHUNK)
        r = (r0 + lax.broadcasted_iota(jnp.int32, (P_CHUNK, LANES), 0)).astype(F32)
        inside = (lo_row <= r) & (r < hi_row)
        group = jnp.where(inside, 1.0, 0.0).astype(BF16)
        want = r[:, 0:1] - jnp.sum(jnp.where(inside, lo_row, 0.0), axis=1, keepdims=True)
        hit = _dot(group, pos) == want
        rows = _dot(jnp.where(hit, 1.0, 0.0).astype(BF16), h)
        buf[slot, pl.ds(r0, P_CHUNK), 0:D] = rows[:, 0:D].astype(buf.dtype)
        extra = jnp.zeros((P_CHUNK, LANES), F32)
        for k in range(3):
            mine = jnp.where(inside, rows[:, D + k * LANES:D + (k + 1) * LANES], 0.0)
            extra = jnp.where(lane == k, jnp.sum(mine, axis=1, keepdims=True), extra)
        buf[slot, pl.ds(r0, P_CHUNK), D:D + LANES] = extra.astype(buf.dtype)

    def pair_body(c, carry):
        chunk(2 * c)
        chunk(2 * c + 1)
        return carry

    lax.fori_loop(0, lax.shift_right_logical(n_chunks + 1, 1), pair_body, 0)
    copies(tau, slot, False)

    @pl.when(tau == n_t - 1)
    def _():
        @pl.when(n_t >= 2)
        def _():
            copies(tau - 1, 1 - slot, True)
        copies(tau, slot, True)
        spare_fill(True)

        sizes =[zbuf.shape[0] >> s for s in range(int(math.log2(zbuf.shape[0] // ROW_ALIGN)) + 1)]

        def fill(wait):
            def e_body(e, carry):
                start = lend_s[e]
                n = rend_s[e] - start
                off = start
                for size in sizes:
                    bit = jnp.bitwise_and(n, size)

                    @pl.when(bit != 0)
                    def _(off=off, size=size):
                        cp = pltpu.make_async_copy(
                            zbuf.at[pl.ds(0, size)],
                            xs_hbm.at[pl.ds(pl.multiple_of(off, ROW_ALIGN), size)], zsem.at[0])
                        if wait:
                            cp.wait()
                        else:
                            cp.start()

                    off = off + bit
                return carry

            lax.fori_loop(0, E, e_body, 0)

        fill(False)
        fill(True)


def _moe_sort(ptab, tot, lend, rend, gates_t, gates, h2, loc_te, np_te, n_rows):
    E, T = gates_t.shape
    D = h2.shape[1]
    ts = SORT_TILE
    grid_spec = pltpu.PrefetchScalarGridSpec(
        num_scalar_prefetch=4,
        grid=(T // ts,),
        in_specs=[pl.BlockSpec((E, ts), lambda t, *_: (0, t)),
                  pl.BlockSpec((ts, LANES), lambda t, *_: (t, 0)),
                  pl.BlockSpec((ts, D), lambda t, *_: (t, 0)),
                  pl.BlockSpec((None, 1, LANES), lambda t, *_: (t, 0, 0)),
                  pl.BlockSpec((None, 1, LANES), lambda t, *_: (t, 0, 0))],
        out_specs=pl.BlockSpec(memory_space=pl.ANY),
        scratch_shapes=[pltpu.VMEM((2, _local_rows_bound(ts), D + LANES), BF16),
                        pltpu.VMEM((EXP_TILE // 2, D + LANES), BF16),
                        pltpu.SemaphoreType.DMA((2,)),
                        pltpu.SemaphoreType.DMA((2,))])
    return pl.pallas_call(
        _moe_sort_kernel,
        grid_spec=grid_spec,
        out_shape=jax.ShapeDtypeStruct((n_rows, D + LANES), BF16),
        compiler_params=pltpu.CompilerParams(dimension_semantics=("arbitrary",),
                                             vmem_limit_bytes=VMEM_LIMIT),
        name="moe_sort",
    )(ptab, tot, lend, rend, gates_t, gates, h2, loc_te, np_te)


def _moe_expert_kernel(texp_s, nused_s, x_ref, wg_ref, wu_ref, wd_ref, y_ref, wgu_s, wd_s):
    i = pl.program_id(0)
    f = wd_ref.shape[0]

    @pl.when(i < nused_s[0])
    def _():
        @pl.when((i == 0) | (texp_s[i] != texp_s[jnp.maximum(i - 1, 0)]))
        def _():
            wgu_s[:, 0:f] = wg_ref[...].astype(BF16)
            wgu_s[:, f:2 * f] = wu_ref[...].astype(BF16)
            wd_s[...] = wd_ref[...].astype(BF16)

        d = wd_ref.shape[1]
        for r0 in range(0, x_ref.shape[0], EXP_STRIP):
            rows = slice(r0, r0 + EXP_STRIP)
            gate = jnp.sum(x_ref[rows, d:].astype(F32), axis=1, keepdims=True)
            a = _dot(x_ref[rows, 0:d], wgu_s[...])
            act = _silu(a[:, :f]) * a[:, f:] * gate
            y_ref[rows, :] = _dot(act.astype(BF16), wd_s[...]).astype(y_ref.dtype)

    @pl.when(i >= nused_s[0])
    def _():
        y_ref[...] = jnp.zeros_like(y_ref)


def _moe_expert(texp, nused, xs, w_gate, w_up, w_down, n_tiles):
    n_rows, xw = xs.shape
    D, f = w_gate.shape[-2:]
    tm = EXP_TILE

    def tile(i, texp, nused):
        return jnp.maximum(jnp.minimum(i, nused[0] - 1), 0)

    grid_spec = pltpu.PrefetchScalarGridSpec(
        num_scalar_prefetch=2,
        grid=(n_tiles,),
        in_specs=[pl.BlockSpec((tm, xw), lambda i, te, nu: (tile(i, te, nu), 0)),
                  pl.BlockSpec((None, D, f), lambda i, te, nu: (te[tile(i, te, nu)], 0, 0)),
                  pl.BlockSpec((None, D, f), lambda i, te, nu: (te[tile(i, te, nu)], 0, 0)),
                  pl.BlockSpec((None, f, D), lambda i, te, nu: (te[tile(i, te, nu)], 0, 0))],
        out_specs=pl.BlockSpec((tm, D), lambda i, te, nu: (i, 0)),
        scratch_shapes=[pltpu.VMEM((D, 2 * f), BF16), pltpu.VMEM((f, D), BF16)])
    return pl.pallas_call(
        _moe_expert_kernel,
        grid_spec=grid_spec,
        out_shape=jax.ShapeDtypeStruct((n_rows, D), BF16),
        compiler_params=pltpu.CompilerParams(dimension_semantics=("arbitrary",),
                                             vmem_limit_bytes=VMEM_LIMIT),
        name="moe_expert",
    )(texp, nused, xs, w_gate, w_up, w_down)


def _moe_combine_kernel(alpha, ptab_s, tot_s, g_ref, loc_ref, np_ref, h_ref, x1_ref,
                        mod_ref, sgu_ref, sd_ref, lg_ref, lb_ref, y_hbm, o_ref, ybuf, acc_ref,
                        sem):
    tau = pl.program_id(0)
    n_t = pl.num_programs(0)
    slot = lax.rem(tau, 2)
    ts, n_lane = g_ref.shape
    E = loc_ref.shape[0]
    n_cols = _piece_cols(ts)

    def copies(tile, sl, wait):
        if wait:
            rows = pl.multiple_of(tot_s[tile], ROW_ALIGN)

            @pl.when(rows > 0)
            def _():
                pltpu.make_async_copy(y_hbm.at[pl.ds(0, rows)], ybuf.at[sl, pl.ds(0, rows)],
                                      sem.at[sl]).wait()
        else:
            def make_copy(loc, dst):
                return pltpu.make_async_copy(y_hbm.at[pl.ds(dst, ROW_ALIGN)],
                                             ybuf.at[sl, pl.ds(loc, ROW_ALIGN)], sem.at[sl])
            _start_pieces(tile, ptab_s, tot_s, n_cols, make_copy)

    @pl.when(tau == 0)
    def _():
        ybuf[...] = jnp.zeros_like(ybuf)
        copies(0, 0, False)

    @pl.when(tau + 1 < n_t)
    def _():
        copies(tau + 1, 1 - slot, False)

    g = g_ref[...]
    mask = g > 0.0
    maskb = jnp.where(mask, 1.0, 0.0).astype(BF16)
    pos = jnp.where(mask, _dot(_strict_lower(ts), maskb), -1.0).astype(BF16)
    lane = lax.broadcasted_iota(jnp.int32, loc_ref.shape, 1)
    lo_col = jnp.sum(jnp.where(lane == tau, loc_ref[...], 0.0), axis=1, keepdims=True)
    hi_col = lo_col + jnp.sum(jnp.where(lane == tau, np_ref[...], 0.0), axis=1, keepdims=True)

    f = sd_ref.shape[0]
    a = _dot(h_ref[...], sgu_ref[...])
    acc_ref[...] = _dot((_silu(a[:, :f]) * a[:, f:]).astype(BF16), sd_ref[...])

    copies(tau, slot, True)
    n_chunks = lax.shift_right_logical(tot_s[tau] + (P_CHUNK - 1), int(math.log2(P_CHUNK)))

    def chunk(c):
        r0 = pl.multiple_of(c * P_CHUNK, P_CHUNK)
        r = (r0 + lax.broadcasted_iota(jnp.int32, (E, P_CHUNK), 1)).astype(F32)
        inside = (lo_col <= r) & (r < hi_col)
        group = jnp.concatenate([jnp.where(inside, 1.0, 0.0),
                                 jnp.zeros((n_lane - E, P_CHUNK), F32)], axis=0).astype(BF16)
        want = r[0:1, :] - jnp.sum(jnp.where(inside, lo_col, 0.0), axis=0, keepdims=True)
        hit = _dot(pos, group) == want
        return _dot(jnp.where(hit, 1.0, 0.0).astype(BF16), ybuf[slot, pl.ds(r0, P_CHUNK), :])

    def pair_body(c, carry):
        acc_ref[...] += chunk(2 * c) + chunk(2 * c + 1)
        return carry

    lax.fori_loop(0, lax.shift_right_logical(n_chunks + 1, 1), pair_body, 0)
    y = alpha * x1_ref[...] + mod_ref[5:6, :] * acc_ref[...]
    o_ref[...] = _layer_norm(y, lg_ref[...], lb_ref[...])


def _moe_combine(alpha, ptab, tot, gates, loc_et, np_et, h2, x1, mod, sgu, sd, ln_g, ln_b, ys,
                 S):
    T, D = h2.shape
    ts = SORT_TILE
    per_b = S // ts
    row = lambda a: pl.BlockSpec(a.shape, lambda t, *_: (0, 0))
    grid_spec = pltpu.PrefetchScalarGridSpec(
        num_scalar_prefetch=2,
        grid=(T // ts,),
        in_specs=[pl.BlockSpec((ts, LANES), lambda t, *_: (t, 0)),
                  row(loc_et), row(np_et),
                  pl.BlockSpec((ts, D), lambda t, *_: (t, 0)),
                  pl.BlockSpec((ts, D), lambda t, *_: (t, 0)),
                  pl.BlockSpec((None, 6, D), lambda t, *_: (t // per_b, 0, 0)),
                  row(sgu), row(sd), row(ln_g), row(ln_b),
                  pl.BlockSpec(memory_space=pl.ANY)],
        out_specs=pl.BlockSpec((ts, D), lambda t, *_: (t, 0)),
        scratch_shapes=[pltpu.VMEM((2, _local_rows_bound(ts), D), BF16),
                        pltpu.VMEM((ts, D), F32),
                        pltpu.SemaphoreType.DMA((2,))])
    return pl.pallas_call(
        functools.partial(_moe_combine_kernel, alpha),
        grid_spec=grid_spec,
        out_shape=jax.ShapeDtypeStruct((T, D), F32),
        compiler_params=pltpu.CompilerParams(dimension_semantics=("arbitrary",),
                                             vmem_limit_bytes=VMEM_LIMIT),
        name="moe_combine",
    )(ptab, tot, gates, loc_et, np_et, h2, x1, mod, sgu, sd, ln_g, ln_b, ys)


def _moe(alpha, h2, x1, gates, gates_t, mod, w_gate, w_up, w_down, sgu, sd, ln_g, ln_b):
    B, S, D = x1.shape
    T = B * S
    n_t = T // SORT_TILE
    n_bound = _sorted_tiles_bound(T)
    n_tab = -(-n_bound // LANES) * LANES
    ptab, loc_et, np_et, loc_te, np_te, tot, erow, texp, nused = _moe_meta(gates_t, n_tab)
    ptab = ptab.reshape(-1)
    tot = tot[0, :n_t]
    h2 = h2.reshape(T, D)
    x1 = x1.reshape(T, D)
    gates = gates.reshape(T, LANES)
    loc_rows = loc_te[:n_t].reshape(n_t, 1, LANES)
    np_rows = np_te[:n_t].reshape(n_t, 1, LANES)

    def sized(n_tiles):
        def run():
            xs = _moe_sort(ptab, tot, erow[0, :N_EXPERTS], erow[1, :N_EXPERTS], gates_t, gates,
                           h2, loc_rows, np_rows, n_tiles * EXP_TILE)
            ys = _moe_expert(texp[0], nused[0, :1], xs, w_gate, w_up, w_down, n_tiles)
            return _moe_combine(alpha, ptab, tot, gates, loc_et, np_et, h2, x1, mod, sgu, sd,
                                ln_g, ln_b, ys, S)
        return run

    n_small = min(_sorted_tiles_expected(T), n_bound)
    out = lax.cond(nused[0, 0] <= n_small, sized(n_small), sized(n_bound))
    return out.reshape(B, S, D)


def _rearrange_w_in(w):
    d_in = w.shape[0]
    scale = HEAD_DIM ** -0.5 * LOG2E
    fq, fk, fv = w[:, 0:512], w[:, 512:1024], w[:, 1024:1536]
    ff = w[:, 1536:1544]
    nq = w[:, 1544:2056]
    kc, vc, ks, vs, kw, vw = (w[:, 2056 + k * LANES:2056 + (k + 1) * LANES] for k in range(6))
    ng = w[:, 2824:2848]
    pad = jnp.zeros((d_in, LANES - ff.shape[1] - ng.shape[1]), w.dtype)

    def spaced(v):
        v = v.reshape(d_in, -1, HEAD_DIM)
        return jnp.concatenate([v, jnp.zeros_like(v)], axis=2).reshape(d_in, -1)

    cols = [fq * scale, fk, nq * scale, kc, kw, vc, spaced(ks), spaced(fv), spaced(vs), spaced(vw),
            ff, ng, pad]
    return jnp.concatenate(cols, axis=1).astype(BF16)


def _ones_row():
    row = np.zeros((1, _IN_COLS), np.float32)
    lanes = np.arange(_C_FV, _C_MISC)
    row[0, lanes[(lanes % LANES) >= HEAD_DIM]] = 1.0
    return row


def _compress_weights(pos, w1, w2):
    half = CMP_BLOCK // 2
    w1r = w1.reshape(2, half, HEAD_DIM, CMP_HIDDEN)
    zeros = jnp.zeros_like(w1r[0])
    def spread(part):
        g0 = jnp.stack([part, zeros], axis=1).reshape(half * 2 * HEAD_DIM, CMP_HIDDEN)
        g1 = jnp.stack([zeros, part], axis=1).reshape(half * 2 * HEAD_DIM, CMP_HIDDEN)
        return jnp.concatenate([g0, g1], axis=1).astype(BF16)
    wa, wb = spread(w1r[0]), spread(w1r[1])
    z2 = jnp.zeros_like(w2)
    w2bd = jnp.concatenate([jnp.concatenate([w2, z2], axis=1),
                            jnp.concatenate([z2, w2], axis=1)], axis=0).astype(BF16)
    posr = pos.reshape(2, half, 1, HEAD_DIM)
    posr = jnp.broadcast_to(posr, (2, half, NSA_GROUPS, HEAD_DIM)).reshape(2, half * 2 * HEAD_DIM)
    return posr, wa, wb, w2bd


@functools.lru_cache(maxsize=None)
def _static_tables(S):
    tq = ATT_TILE
    n_cmp = (S - CMP_BLOCK) // CMP_STRIDE + 1
    n_pad = S // CMP_STRIDE
    n_slc = S // SEL_BLOCK
    t = np.arange(S)[:, None]
    n = np.arange(n_pad)[None, :]
    bucket_c = _t5_bucket_np(t - (n * CMP_STRIDE + CMP_BLOCK - 1)).reshape(1, -1)
    d = (np.arange(4)[:, None, None] * tq + np.arange(tq)[None, :, None]
         - np.arange(2 * tq)[None, None, :])
    bucket_w = _t5_bucket_np(d).reshape(1, -1)
    cs = np.arange(n_pad)[None, :] * CMP_STRIDE
    sj = np.arange(n_slc)[:, None] * SEL_BLOCK
    ovl_t = ((cs < sj + SEL_BLOCK) & (cs + CMP_BLOCK > sj) & (np.arange(n_pad)[None, :] < n_cmp))
    return bucket_c, bucket_w, ovl_t.astype(np.float32)


def kernel(x, c, w_ada, b_ada, w_in, b_f, cmp_pos_k, cmp_w1_k, cmp_w2_k, cmp_pos_v, cmp_w1_v,
           cmp_w2_v, rel_bias, w_out, ln1_g, ln1_b, w_router, e_bias, w_gate, w_up, w_down,
           ws_gate, ws_up, ws_down, ln2_g, ln2_b):
    B, S, D = x.shape
    depth = w_ada.shape[0]
    alpha = (2 * depth) ** 0.25
    tq = ATT_TILE
    bucket_c, bucket_w, ovl_t = _static_tables(S)
    rel_bias_t = rel_bias.T * LOG2E
    bias_c = _bias_table(jnp.asarray(bucket_c), rel_bias_t).reshape(NSA_HEADS, S, S // CMP_STRIDE)
    w4 = _bias_table(jnp.asarray(bucket_w), rel_bias_t).reshape(NSA_HEADS, 4, tq, 2 * tq)
    ovl_t = jnp.asarray(ovl_t, BF16)

    for l in range(depth):
        mod = _ada(c, w_ada[l], b_ada[l]).reshape(B, 6, D)
        bf_row = jnp.zeros((1, LANES), F32).at[0, :FOX_HEADS].set(b_f[l])
        (fq, fk, nq, kc, kw, vc, ks, fv, vs, vw, misc, misc_t) = _in_proj(
            x, mod, _rearrange_w_in(w_in[l]), bf_row, jnp.asarray(_ones_row()))

        o_fox = _fox(fq, fk, fv, misc_t[:, :FOX_HEADS, :])

        pk, wak, wbk, w2k = _compress_weights(cmp_pos_k[l], cmp_w1_k[l], cmp_w2_k[l])
        pv, wav, wbv, w2v = _compress_weights(cmp_pos_v[l], cmp_w1_v[l], cmp_w2_v[l])
        rows = S // CMP_STRIDE
        kcmp, vcmp = _compress(kc.reshape(B, rows, CMP_STRIDE * LANES),
                               vc.reshape(B, rows, CMP_STRIDE * LANES),
                               pk, pv, wak, wbk, wav, wbv, w2k, w2v)

        oc, sel = _cmp_sel(nq, kcmp, vcmp, bias_c, misc, ovl_t)
        o_nsa = _nsa(nq, ks, vs, kw, vw, sel, w4, misc, oc)

        x1, h2, gates, gates_t = _out_proj(
            alpha, o_fox, o_nsa, x, mod, w_out[l].astype(BF16), ln1_g[l].reshape(1, D),
            ln1_b[l].reshape(1, D), w_router[l].T, e_bias[l].reshape(N_EXPERTS, 1))

        sgu = jnp.concatenate([ws_gate[l], ws_up[l]], axis=-1).astype(BF16)
        x = _moe(alpha, h2, x1, gates, gates_t, mod, w_gate[l], w_up[l], w_down[l], sgu,
                 ws_down[l].astype(BF16), ln2_g[l].reshape(1, D), ln2_b[l].reshape(1, D))
    return x
```

```python
import functools
import math

import jax
import jax.numpy as jnp
import numpy as np
from jax import lax
from jax.experimental import pallas as pl
from jax.experimental.pallas import tpu as pltpu

F32 = jnp.float32
BF16 = jnp.bfloat16

HEAD_DIM = 64
FOX_HEADS = 8
NSA_HEADS = 8
NSA_GQA = 4
NSA_GROUPS = NSA_HEADS // NSA_GQA
CMP_BLOCK = 32
CMP_STRIDE = 16
CMP_HIDDEN = 256
SEL_BLOCK = 64
N_SEL = 16
WINDOW = 512
N_BUCKETS = 32
MAX_DISTANCE = 128
N_EXPERTS = 64
N_EXPERT_GROUPS = 8
GROUP_SIZE = N_EXPERTS // N_EXPERT_GROUPS
TOPK_GROUPS = 4
TOP_K = 8
D_EXPERT = 256
ROUTED_SCALE = 2.5
LN_EPS = 1e-5
NEG_BIG = -1e30
FORCE_SCORE = 1e4

LANES = 128
ATT_TILE = 256
FOX_TILE = 512
CMP_TILE = 512
ROW_TILE = 512
MOE_TILE = 1024
VMEM_LIMIT = 48 * 1024 * 1024

NT_DIMS = (((1,), (1,)), ((), ()))


def _dot(a, b):
    return jnp.dot(a, b, preferred_element_type=F32)


def _dot_nt(a, b):
    return lax.dot_general(a, b, NT_DIMS, preferred_element_type=F32)


def _split3(x):
    hi = x.astype(BF16)
    r1 = x - hi.astype(F32)
    mid = r1.astype(BF16)
    lo = (r1 - mid.astype(F32)).astype(BF16)
    return hi, mid, lo


def _silu(x):
    return x / (1.0 + jnp.exp(-x))


def _sigmoid(x):
    return 1.0 / (1.0 + jnp.exp(-x))


def _swap_halves(x):
    return pltpu.roll(x, HEAD_DIM, 1)


def _t5_bucket_np(dist):
    n = np.maximum(dist, 0)
    max_exact = N_BUCKETS // 2
    nf = np.maximum(n, 1).astype(np.float32)
    large = max_exact + (np.log(nf / max_exact) / math.log(MAX_DISTANCE / max_exact)
                         * (N_BUCKETS - max_exact)).astype(np.int32)
    large = np.minimum(large, N_BUCKETS - 1)
    return np.where(n < max_exact, n, large).astype(np.int32)


def _ada_kernel(c_ref, w_ref, b_ref, o_ref):
    c = c_ref[...]
    o_ref[...] = jnp.dot(_silu(c), w_ref[...], preferred_element_type=F32,
                         precision=lax.Precision.HIGHEST) + b_ref[...]


def _ada(c, w_ada, b_ada):
    B, D = c.shape
    n_out = w_ada.shape[1]
    tn = 1024
    return pl.pallas_call(
        _ada_kernel,
        grid=(n_out // tn,),
        in_specs=[pl.BlockSpec((B, D), lambda j: (0, 0)),
                  pl.BlockSpec((D, tn), lambda j: (0, j)),
                  pl.BlockSpec((1, tn), lambda j: (0, j))],
        out_specs=pl.BlockSpec((B, tn), lambda j: (0, j)),
        out_shape=jax.ShapeDtypeStruct((B, n_out), F32),
        compiler_params=pltpu.CompilerParams(dimension_semantics=("arbitrary",),
                                             vmem_limit_bytes=VMEM_LIMIT),
        name="ada",
    )(c, w_ada, b_ada.reshape(1, n_out))


def _bias_table_kernel(bkt_ref, rbt_ref, o_ref):
    bkt = bkt_ref[...]
    k = lax.broadcasted_iota(jnp.int32, (N_BUCKETS, bkt.shape[1]), 0)
    onehot = jnp.where(k == bkt, 1.0, 0.0).astype(BF16)
    hi, mid, lo = _split3(rbt_ref[...])
    o_ref[...] = _dot(hi, onehot) + _dot(mid, onehot) + _dot(lo, onehot)


def _bias_table(bucket, rel_bias_t):
    n = bucket.shape[1]
    chunk = 32768
    n_heads = rel_bias_t.shape[0]
    return pl.pallas_call(
        _bias_table_kernel,
        grid=(n // chunk,),
        in_specs=[pl.BlockSpec((1, chunk), lambda j: (0, j)),
                  pl.BlockSpec(rel_bias_t.shape, lambda j: (0, 0))],
        out_specs=pl.BlockSpec((n_heads, chunk), lambda j: (0, j)),
        out_shape=jax.ShapeDtypeStruct((n_heads, n), F32),
        compiler_params=pltpu.CompilerParams(dimension_semantics=("parallel",),
                                             vmem_limit_bytes=VMEM_LIMIT),
        name="bias_table",
    )(bucket, rel_bias_t)


_C_FQ, _C_FK, _C_NQ = 0, 512, 1024
_C_K3 = 1536
_C_SK = 1920
_C_FV = 2048
_C_SV = 2560
_C_WV = 2688
_C_MISC = 2816
_IN_COLS = 2944
LOG2E = math.log2(math.e)


def _in_proj_kernel(x_ref, mod_ref, w_ref, bf_ref, fq_ref, fk_ref, nq_ref, kc_ref, kw_ref,
                    vc_ref, ks_ref, fv_ref, vs_ref, vw_ref, misc_ref, misct_ref, carry_ref):
    s_idx = pl.program_id(1)
    tm = x_ref.shape[0]
    mod = mod_ref[...]
    h = (x_ref[...] * (1.0 + mod[1:2, :]) + mod[0:1, :]).astype(BF16)

    for ref, c0 in ((fq_ref, _C_FQ), (fk_ref, _C_FK), (nq_ref, _C_NQ)):
        ref[...] = _dot(h, w_ref[:, c0:c0 + 512]).astype(ref.dtype)
    keys = _dot(h, w_ref[:, _C_K3:_C_FV])
    fox_v = _dot(h, w_ref[:, _C_FV:_C_SV])
    tail = _dot(h, w_ref[:, _C_SV:_IN_COLS])
    for k, ref in enumerate((kc_ref, kw_ref, vc_ref)):
        ref[...] = keys[:, k * LANES:(k + 1) * LANES].astype(ref.dtype)

    lane = lax.broadcasted_iota(jnp.int32, (tm, LANES), 1)
    low = lane < HEAD_DIM

    def spread(ref, cols, fill):
        for p in range(cols.shape[-1] // LANES):
            r = cols[:, p * LANES:(p + 1) * LANES]
            ref[:, 2 * p * LANES:(2 * p + 1) * LANES] = jnp.where(low, r, fill).astype(ref.dtype)
            ref[:, (2 * p + 1) * LANES:(2 * p + 2) * LANES] = jnp.where(
                low, _swap_halves(r), fill).astype(ref.dtype)

    key_blk = lax.shift_right_logical(
        s_idx * tm + lax.broadcasted_iota(jnp.int32, (tm, LANES), 0), int(math.log2(SEL_BLOCK)))
    spread(ks_ref, keys[:, 3 * LANES:4 * LANES],
           jnp.where(lane == HEAD_DIM + key_blk, 1.0, 0.0))
    spread(fv_ref, fox_v, 1.0)
    spread(vs_ref, tail[:, 0:LANES], 1.0)
    spread(vw_ref, tail[:, LANES:2 * LANES], 1.0)

    z = tail[:, 2 * LANES:3 * LANES] + bf_ref[...]
    is_f = lane < FOX_HEADS
    log_f = jnp.minimum(z, 0.0) - jnp.log(1.0 + jnp.exp(-jnp.abs(z)))
    log_f = jnp.where(is_f, log_f, 0.0)

    row = lax.broadcasted_iota(jnp.int32, (tm, tm), 0)
    col = lax.broadcasted_iota(jnp.int32, (tm, tm), 1)
    tri = jnp.where(row >= col, 1.0, 0.0).astype(BF16)
    sums = _dot(tri, jnp.concatenate(_split3(log_f), axis=1))
    local = sums[:, 0:LANES] + sums[:, LANES:2 * LANES] + sums[:, 2 * LANES:3 * LANES]

    @pl.when(s_idx == 0)
    def _():
        carry_ref[...] = jnp.zeros_like(carry_ref)

    cum = local + carry_ref[...]
    carry_ref[...] = cum[tm - 1:tm, :]
    misc = jnp.where(is_f, cum * LOG2E, _sigmoid(z))
    misc_ref[...] = misc
    misct_ref[...] = misc.T


def _in_proj(x, mod, w_r, bf_row):
    B, S, D = x.shape
    tm = ROW_TILE
    widths = (512, 512, 512, LANES, LANES, LANES, 2 * (_C_FV - _C_SK), 2 * (_C_SV - _C_FV),
              2 * (_C_WV - _C_SV), 2 * (_C_MISC - _C_WV))
    wide = lambda w: pl.BlockSpec((None, tm, w), lambda b, s: (b, s, 0))
    out_shape = ([jax.ShapeDtypeStruct((B, S, w), BF16) for w in widths]
                 + [jax.ShapeDtypeStruct((B, S, LANES), F32),
                    jax.ShapeDtypeStruct((B, LANES, S), F32)])
    out_specs = ([wide(w) for w in widths]
                 + [wide(LANES), pl.BlockSpec((None, LANES, tm), lambda b, s: (b, 0, s))])
    return pl.pallas_call(
        _in_proj_kernel,
        grid=(B, S // tm),
        in_specs=[pl.BlockSpec((None, tm, D), lambda b, s: (b, s, 0)),
                  pl.BlockSpec((None, 6, D), lambda b, s: (b, 0, 0)),
                  pl.BlockSpec((D, _IN_COLS), lambda b, s: (0, 0)),
                  pl.BlockSpec((1, LANES), lambda b, s: (0, 0))],
        out_specs=out_specs,
        out_shape=out_shape,
        scratch_shapes=[pltpu.VMEM((1, LANES), F32)],
        compiler_params=pltpu.CompilerParams(dimension_semantics=("parallel", "arbitrary"),
                                             vmem_limit_bytes=VMEM_LIMIT),
        name="in_proj",
    )(x, mod, w_r, bf_row)


def _softmax_weights(s, m):
    return jnp.exp2((s - m).astype(BF16))


def _flash_update(carry, s, vt):
    m, acc = carry
    m_new = jnp.maximum(m, jnp.max(s, axis=-1, keepdims=True))
    alpha = jnp.exp2(m - m_new)
    p = _softmax_weights(s, m_new)
    rows = acc.shape[0]
    acc = alpha.reshape(rows, 1) * acc + _dot(p.reshape(rows, s.shape[-1]), vt)
    return m_new, acc


def _normalize(acc):
    return acc / _swap_halves(acc)


def _fox_kernel(q_ref, k_ref, v_ref, ck_ref, o_ref):
    i = pl.program_id(2)
    tq = q_ref.shape[0]
    tk = ck_ref.shape[2]
    q2 = q_ref[...].astype(F32)
    lane = lax.broadcasted_iota(jnp.int32, (tq, LANES), 1)
    low = lane < HEAD_DIM
    halves = (low, jnp.logical_not(low))
    qh = [jnp.where(h, q2, 0.0).astype(BF16) for h in halves]
    col_minus_row = (lax.broadcasted_iota(jnp.int32, (tq, tk), 1)
                     - lax.broadcasted_iota(jnp.int32, (tq, tk), 0))

    def step(jj, carry, diagonal):
        k0 = pl.multiple_of(jj * tk, tk)
        kt = k_ref[pl.ds(k0, tk), :]
        new = []
        for hh in range(2):
            s = _dot_nt(qh[hh], kt) - ck_ref[hh, pl.ds(jj, 1), :]
            if diagonal:
                s = jnp.where(col_minus_row <= i * tq - jj * tk, s, NEG_BIG)
            vt = v_ref[pl.ds(k0, tk), hh * LANES:(hh + 1) * LANES]
            new.append(_flash_update(carry[hh], s, vt))
        return tuple(new)

    init = tuple((jnp.full((tq, 1), NEG_BIG, F32), jnp.zeros((tq, LANES), F32))
                 for _ in range(2))
    n_full = (i * tq) // tk
    carry = lax.fori_loop(0, n_full, lambda jj, c: step(jj, c, False), init)
    carry = step(n_full, carry, True)
    o_ref[...] = jnp.where(low, _normalize(carry[0][1]),
                           _swap_halves(_normalize(carry[1][1]))).astype(o_ref.dtype)


def _fox(fq, fk, fv, cum_row):
    B, S, W = fq.shape
    tq = tk = FOX_TILE
    n_pairs = W // LANES
    cum_row = cum_row.reshape(B, n_pairs, 2, S // tk, tk)
    return pl.pallas_call(
        _fox_kernel,
        grid=(B, n_pairs, S // tq),
        in_specs=[pl.BlockSpec((None, tq, LANES), lambda b, p, i: (b, i, p)),
                  pl.BlockSpec((None, S, LANES), lambda b, p, i: (b, 0, p)),
                  pl.BlockSpec((None, S, 2 * LANES), lambda b, p, i: (b, 0, p)),
                  pl.BlockSpec((None, None, 2, S // tk, tk), lambda b, p, i: (b, p, 0, 0, 0))],
        out_specs=pl.BlockSpec((None, tq, LANES), lambda b, p, i: (b, i, p)),
        out_shape=jax.ShapeDtypeStruct((B, S, W), BF16),
        compiler_params=pltpu.CompilerParams(
            dimension_semantics=("parallel", "parallel", "arbitrary"),
            vmem_limit_bytes=VMEM_LIMIT),
        name="fox",
    )(fq, fk, fv, cum_row)


def _compress_kernel(xk_ref, xv_ref, pk_ref, pv_ref, wak_ref, wbk_ref, wav_ref, wbv_ref,
                     w2k_ref, w2v_ref, ok_ref, ov_ref):
    n_rows = xk_ref.shape[0]
    for x_ref, p_ref, wa_ref, wb_ref, w2_ref, o_ref in (
            (xk_ref, pk_ref, wak_ref, wbk_ref, w2k_ref, ok_ref),
            (xv_ref, pv_ref, wav_ref, wbv_ref, w2v_ref, ov_ref)):
        x = x_ref[...].astype(F32)
        xa = (x + p_ref[0:1, :]).astype(BF16)
        xb = (x + p_ref[1:2, :]).astype(BF16)
        hb = _dot(xb, wb_ref[...])
        h1 = _dot(xa, wa_ref[...]) + pltpu.roll(hb, n_rows - 1, 0)
        o_ref[...] = _dot(_silu(h1).astype(BF16), w2_ref[...]).astype(o_ref.dtype)


def _compress(xk, xv, pk, pv, wak, wbk, wav, wbv, w2k, w2v):
    B, R, C = xk.shape
    xspec = pl.BlockSpec((None, R, C), lambda b: (b, 0, 0))
    full = lambda a: pl.BlockSpec(a.shape, lambda b: (0,) * a.ndim)
    ospec = pl.BlockSpec((None, R, LANES), lambda b: (b, 0, 0))
    return pl.pallas_call(
        _compress_kernel,
        grid=(B,),
        in_specs=[xspec, xspec] + [full(a) for a in (pk, pv, wak, wbk, wav, wbv, w2k, w2v)],
        out_specs=[ospec, ospec],
        out_shape=[jax.ShapeDtypeStruct((B, R, LANES), BF16)] * 2,
        compiler_params=pltpu.CompilerParams(dimension_semantics=("parallel",),
                                             vmem_limit_bytes=VMEM_LIMIT),
        name="compress",
    )(xk, xv, pk, pv, wak, wbk, wav, wbv, w2k, w2v)


def _rank_rows(score):
    n = score.shape[0]
    j = lax.broadcasted_iota(jnp.int32, score.shape, 0)
    rank = jnp.zeros(score.shape, jnp.int32)
    for i in range(n):
        si = score[i:i + 1, :]
        beats = (si > score) | ((si == score) & (j > i))
        rank = rank + jnp.where(beats, 1, 0)
    return rank


def _dup_head(q4, r):
    pair = q4[:, (r // 2) * LANES:(r // 2 + 1) * LANES].astype(F32)
    lane = lax.broadcasted_iota(jnp.int32, pair.shape, 1)
    swapped = _swap_halves(pair)
    if r % 2 == 0:
        return jnp.where(lane < HEAD_DIM, pair, swapped)
    return jnp.where(lane < HEAD_DIM, swapped, pair)


def _pack_heads(o_list, g):
    lane = lax.broadcasted_iota(jnp.int32, o_list[0].shape, 1)
    in_g = (lane >= g * HEAD_DIM) & (lane < (g + 1) * HEAD_DIM)
    both = []
    for o in o_list:
        om = jnp.where(in_g, o, 0.0)
        both.append(om + _swap_halves(om))
    pairs = [jnp.where(lane < HEAD_DIM, both[2 * p], both[2 * p + 1]) for p in range(2)]
    return jnp.concatenate(pairs, axis=1)


def _group_gates(misc, g):
    w = 3 * NSA_GQA
    gates = misc[:, FOX_HEADS:FOX_HEADS + w]
    for other in range(1, NSA_GROUPS):
        gates = jnp.where(g == other, misc[:, FOX_HEADS + other * w:FOX_HEADS + (other + 1) * w],
                          gates)
    return gates


def _cmp_sel_kernel(q_ref, kc_ref, vc_ref, bias_ref, gate_ref, ovl_ref, oc_ref, sel_ref):
    g = pl.program_id(1)
    i = pl.program_id(2)
    tq = q_ref.shape[0]
    n_pad = kc_ref.shape[0]
    q4 = q_ref[...]
    lane = lax.broadcasted_iota(jnp.int32, (n_pad, LANES), 1)
    in_g = (lane >= g * HEAD_DIM) & (lane < (g + 1) * HEAD_DIM)
    kc = jnp.where(in_g, kc_ref[...].astype(F32), 0.0).astype(BF16)
    vc = vc_ref[...]
    t = i * tq + lax.broadcasted_iota(jnp.int32, (tq, n_pad), 0)
    n = lax.broadcasted_iota(jnp.int32, (tq, n_pad), 1)
    valid = t >= n * CMP_STRIDE + (CMP_BLOCK - 1)
    gates = _group_gates(gate_ref[...], g)
    p_sum = jnp.zeros((tq, n_pad), F32)
    outs = []
    for r in range(NSA_GQA):
        qr = _dup_head(q4, r).astype(BF16)
        s = _dot_nt(qr, kc)
        s = jnp.where(valid, s + bias_ref[r], NEG_BIG)
        m = jnp.max(s, axis=-1, keepdims=True)
        p = jnp.exp2(s - m)
        p = p / jnp.sum(p, axis=-1, keepdims=True)
        p = jnp.where(valid, p, 0.0)
        p_sum = p_sum + p
        outs.append(_dot(p.astype(BF16), vc) * gates[:, 3 * r:3 * r + 1])
    oc_ref[...] = _pack_heads(outs, g).astype(oc_ref.dtype)

    ovl = ovl_ref[...]
    hi, mid, lo = _split3(p_sum)
    imp = _dot_nt(ovl, hi) + _dot_nt(ovl, mid) + _dot_nt(ovl, lo)
    n_blk = imp.shape[0]
    j = lax.broadcasted_iota(jnp.int32, (n_blk, tq), 0)
    qb = jnp.right_shift(i * tq + lax.broadcasted_iota(jnp.int32, (n_blk, tq), 1),
                         int(math.log2(SEL_BLOCK)))
    forced = (j == 0) | (j == qb) | (j == qb - 1)
    causal = j <= qb
    score = jnp.where(causal, imp + jnp.where(forced, FORCE_SCORE, 0.0), -FORCE_SCORE)
    chosen = (_rank_rows(score) < N_SEL) & causal
    sel = jnp.where(chosen, 1.0, 0.0)
    sel = jnp.concatenate([sel, jnp.zeros((LANES - n_blk, tq), F32)], axis=0)
    sel_ref[...] = sel.T


def _cmp_sel(nq, kcmp, vcmp, bias_c, gates_g, ovl_t):
    B, S, _ = nq.shape
    tq = CMP_TILE
    n_pad = kcmp.shape[1]
    return pl.pallas_call(
        _cmp_sel_kernel,
        grid=(B, NSA_GROUPS, S // tq),
        in_specs=[pl.BlockSpec((None, tq, 2 * LANES), lambda b, g, i: (b, i, g)),
                  pl.BlockSpec((None, n_pad, LANES), lambda b, g, i: (b, 0, 0)),
                  pl.BlockSpec((None, n_pad, LANES), lambda b, g, i: (b, 0, 0)),
                  pl.BlockSpec((NSA_GQA, tq, n_pad), lambda b, g, i: (g, i, 0)),
                  pl.BlockSpec((None, tq, LANES), lambda b, g, i: (b, i, 0)),
                  pl.BlockSpec(ovl_t.shape, lambda b, g, i: (0, 0))],
        out_specs=[pl.BlockSpec((None, tq, 2 * LANES), lambda b, g, i: (b, i, g)),
                   pl.BlockSpec((None, None, tq, LANES), lambda b, g, i: (b, g, i, 0))],
        out_shape=[jax.ShapeDtypeStruct((B, S, NSA_HEADS * HEAD_DIM), BF16),
                   jax.ShapeDtypeStruct((B, NSA_GROUPS, S, LANES), F32)],
        compiler_params=pltpu.CompilerParams(
            dimension_semantics=("parallel", "parallel", "arbitrary"),
            vmem_limit_bytes=VMEM_LIMIT),
        name="cmp_sel",
    )(nq, kcmp, vcmp, bias_c, gates_g, ovl_t)


def _nsa_kernel(q_ref, ks_ref, vs_ref, kw_ref, vw_ref, sel_ref, w4_ref, gate_ref, oc_ref,
                o_ref):
    g = pl.program_id(0)
    i = pl.program_id(2)
    tq = q_ref.shape[0]
    tk = w4_ref.shape[-1]
    H = NSA_GQA
    q4 = q_ref[...]
    lane = lax.broadcasted_iota(jnp.int32, (tq, LANES), 1)
    in_g = (lane >= g * HEAD_DIM) & (lane < (g + 1) * HEAD_DIM)
    heads = [_dup_head(q4, r) for r in range(H)]
    qs = jnp.concatenate([jnp.where(in_g, hd, 0.0).astype(BF16) for hd in heads], axis=0)

    blk_mask = _swap_halves((sel_ref[...] - 1.0) * (-NEG_BIG))
    qsel = jnp.concatenate(
        [jnp.where(lane < HEAD_DIM, hd, blk_mask).astype(BF16) for hd in heads], axis=0)

    def sel_step(jj, carry, diagonal):
        k0 = pl.multiple_of(jj * tk, tk)
        dd = jnp.minimum(i - 2 * jj, 3)
        s = (_dot_nt(qsel, ks_ref[pl.ds(k0, tk), :]).reshape(H, tq, tk)
             + w4_ref[:, pl.ds(dd, 1)].reshape(H, tq, tk))
        if diagonal:
            cmr = (lax.broadcasted_iota(jnp.int32, (tq, tk), 1)
                   - lax.broadcasted_iota(jnp.int32, (tq, tk), 0))
            s = jnp.where((cmr <= i * tq - jj * tk)[None], s, NEG_BIG)
        return _flash_update(carry, s, vs_ref[pl.ds(k0, tk), :])

    init = (jnp.full((H, tq, 1), NEG_BIG, F32), jnp.zeros((H * tq, LANES), F32))
    n_full = (i * tq) // tk
    carry = lax.fori_loop(0, n_full, lambda jj, c: sel_step(jj, c, False), init)
    _, acc_s = sel_step(n_full, carry, True)

    old = [pl.multiple_of(jnp.maximum(i - d, 0) * tq, tq) for d in (2, 1)]
    now = pl.multiple_of(i * tq, tq)
    k_old = jnp.concatenate([kw_ref[pl.ds(st, tq), :] for st in old], axis=0)
    v_old = jnp.concatenate([vw_ref[pl.ds(st, tq), :] for st in old], axis=0)
    col = lax.broadcasted_iota(jnp.int32, (tq, 2 * tq), 1)
    cmr = col - lax.broadcasted_iota(jnp.int32, (tq, 2 * tq), 0)
    never = 4 * tq
    ok_old = (((col < tq) & (cmr > jnp.where(i >= 2, 0, never)))
              | (col >= jnp.where(i >= 1, tq, never)))
    s_old = jnp.where(ok_old[None], _dot_nt(qs, k_old).reshape(H, tq, 2 * tq) + w4_ref[:, 2],
                      NEG_BIG)
    s_now = jnp.where((cmr[:, 0:tq] <= 0)[None],
                      _dot_nt(qs, kw_ref[pl.ds(now, tq), :]).reshape(H, tq, tq)
                      + w4_ref[:, 0, :, 0:tq], NEG_BIG)
    m_w = jnp.maximum(jnp.max(s_old, axis=-1, keepdims=True),
                      jnp.max(s_now, axis=-1, keepdims=True))
    acc_w = (_dot(_softmax_weights(s_old, m_w).reshape(H * tq, 2 * tq), v_old)
             + _dot(_softmax_weights(s_now, m_w).reshape(H * tq, tq), vw_ref[pl.ds(now, tq), :]))

    gates = _group_gates(gate_ref[...], g)
    o_s = _normalize(acc_s)
    o_w = _normalize(acc_w)
    outs = []
    for r in range(H):
        sl = slice(r * tq, (r + 1) * tq)
        outs.append(o_s[sl] * gates[:, 3 * r + 1:3 * r + 2] + o_w[sl] * gates[:, 3 * r + 2:3 * r + 3])
    lane = lax.broadcasted_iota(jnp.int32, (tq, LANES), 1)
    pairs = [jnp.where(lane < HEAD_DIM, outs[2 * p], _swap_halves(outs[2 * p + 1]))
             for p in range(H // 2)]
    o_ref[...] = (jnp.concatenate(pairs, axis=1) + oc_ref[...].astype(F32)).astype(o_ref.dtype)


def _nsa(nq, ks, vs, kw, vw, sel, w4, misc, oc):
    B, S, _ = nq.shape
    tq = ATT_TILE
    tk = w4.shape[-1]
    both = lambda: pl.BlockSpec((None, S, LANES), lambda g, b, i: (b, 0, 0))
    mine = lambda: pl.BlockSpec((None, S, LANES), lambda g, b, i: (b, 0, g))
    return pl.pallas_call(
        _nsa_kernel,
        grid=(NSA_GROUPS, B, S // tq),
        in_specs=[pl.BlockSpec((None, tq, 2 * LANES), lambda g, b, i: (b, i, g)),
                  mine(), mine(), both(), mine(),
                  pl.BlockSpec((None, None, tq, LANES), lambda g, b, i: (b, g, i, 0)),
                  pl.BlockSpec((NSA_GQA, 4, tq, tk), lambda g, b, i: (g, 0, 0, 0)),
                  pl.BlockSpec((None, tq, LANES), lambda g, b, i: (b, i, 0)),
                  pl.BlockSpec((None, tq, 2 * LANES), lambda g, b, i: (b, i, g))],
        out_specs=pl.BlockSpec((None, tq, 2 * LANES), lambda g, b, i: (b, i, g)),
        out_shape=jax.ShapeDtypeStruct((B, S, NSA_HEADS * HEAD_DIM), BF16),
        compiler_params=pltpu.CompilerParams(
            dimension_semantics=("parallel", "parallel", "arbitrary"),
            vmem_limit_bytes=VMEM_LIMIT),
        name="nsa",
    )(nq, ks, vs, kw, vw, sel, w4, misc, oc)


def _layer_norm(y, g, b):
    mu = jnp.mean(y, axis=-1, keepdims=True)
    yc = y - mu
    var = jnp.mean(yc * yc, axis=-1, keepdims=True)
    return yc * lax.rsqrt(var + LN_EPS) * g + b


def _top_rows(score, k):
    n = score.shape[0]
    idx = lax.broadcasted_iota(jnp.int32, score.shape, 0).astype(F32)
    alive = jnp.ones(score.shape, F32)
    for _ in range(k):
        live = alive > 0.0
        best = jnp.max(jnp.where(live, score, -jnp.inf), axis=0, keepdims=True)
        first = jnp.min(jnp.where(live & (score == best), idx, float(n)), axis=0, keepdims=True)
        alive = jnp.where(idx == first, 0.0, alive)
    return alive == 0.0


def _router_gates_t(h2, wr_t, eb_col):
    tm = h2.shape[0]
    h_hi, h_lo, _ = _split3(h2)
    w_hi, w_lo, _ = _split3(wr_t)
    logit = _dot_nt(w_hi, h_hi) + _dot_nt(w_hi, h_lo) + _dot_nt(w_lo, h_hi)
    scores = _sigmoid(logit)
    biased = scores + eb_col
    e_in = lax.broadcasted_iota(jnp.int32, (GROUP_SIZE, tm), 0).astype(F32)
    gs_rows = []
    for gi in range(N_EXPERT_GROUPS):
        grp = biased[gi * GROUP_SIZE:(gi + 1) * GROUP_SIZE, :]
        m1 = jnp.max(grp, axis=0, keepdims=True)
        first = jnp.min(jnp.where(grp == m1, e_in, float(GROUP_SIZE)), axis=0, keepdims=True)
        m2 = jnp.max(jnp.where(e_in == first, -jnp.inf, grp), axis=0, keepdims=True)
        gs_rows.append(m1 + m2)
    gscore = jnp.concatenate(gs_rows, axis=0)
    g_keep = _rank_rows(gscore) < TOPK_GROUPS
    keep = jnp.concatenate(
        [jnp.broadcast_to(g_keep[gi:gi + 1, :], (GROUP_SIZE, tm)) for gi in range(N_EXPERT_GROUPS)],
        axis=0)
    masked = jnp.where(keep, biased, -jnp.inf)
    chosen = _top_rows(masked, TOP_K)
    w = jnp.where(chosen, scores, 0.0)
    return w / jnp.sum(w, axis=0, keepdims=True) * ROUTED_SCALE


def _out_proj_kernel(alpha, of_ref, on_ref, x_ref, mod_ref, w_ref, lg_ref, lb_ref, wr_ref,
                     eb_ref, x1_ref, h2_ref, gate_ref, gate_t_ref):
    half = of_ref.shape[1]
    mod = mod_ref[...]
    mixed = _dot(of_ref[...], w_ref[0:half, :]) + _dot(on_ref[...], w_ref[half:2 * half, :])
    y = alpha * x_ref[...] + mod[2:3, :] * mixed
    x1 = _layer_norm(y, lg_ref[...], lb_ref[...])
    x1_ref[...] = x1
    h2 = x1 * (1.0 + mod[4:5, :]) + mod[3:4, :]
    h2_ref[...] = h2.astype(h2_ref.dtype)
    gates_t = _router_gates_t(h2, wr_ref[...], eb_ref[...])
    gate_t_ref[...] = gates_t
    tm = h2.shape[0]
    gates_t = jnp.concatenate([gates_t, jnp.zeros((LANES - N_EXPERTS, tm), F32)], axis=0)
    gate_ref[...] = gates_t.T


def _out_proj(alpha, o_fox, o_nsa, x, mod, w_out, ln_g, ln_b, wr_t, eb_col):
    B, S, D = x.shape
    tm = ROW_TILE
    half = o_fox.shape[-1]
    row = lambda a: pl.BlockSpec(a.shape, lambda b, s: (0, 0))
    return pl.pallas_call(
        functools.partial(_out_proj_kernel, alpha),
        grid=(B, S // tm),
        in_specs=[pl.BlockSpec((None, tm, half), lambda b, s: (b, s, 0)),
                  pl.BlockSpec((None, tm, half), lambda b, s: (b, s, 0)),
                  pl.BlockSpec((None, tm, D), lambda b, s: (b, s, 0)),
                  pl.BlockSpec((None, 6, D), lambda b, s: (b, 0, 0)),
                  row(w_out), row(ln_g), row(ln_b), row(wr_t), row(eb_col)],
        out_specs=[pl.BlockSpec((None, tm, D), lambda b, s: (b, s, 0)),
                   pl.BlockSpec((None, tm, D), lambda b, s: (b, s, 0)),
                   pl.BlockSpec((None, tm, LANES), lambda b, s: (b, s, 0)),
                   pl.BlockSpec((N_EXPERTS, tm), lambda b, s: (0, b * (S // tm) + s))],
        out_shape=[jax.ShapeDtypeStruct((B, S, D), F32),
                   jax.ShapeDtypeStruct((B, S, D), BF16),
                   jax.ShapeDtypeStruct((B, S, LANES), F32),
                   jax.ShapeDtypeStruct((N_EXPERTS, B * S), F32)],
        compiler_params=pltpu.CompilerParams(dimension_semantics=("parallel", "parallel"),
                                             vmem_limit_bytes=VMEM_LIMIT),
        name="out_proj",
    )(o_fox, o_nsa, x, mod, w_out, ln_g, ln_b, wr_t, eb_col)


SORT_TILE = 256
ROW_ALIGN = 16
EXP_TILE = 1024
EXP_STRIP = 256
P_CHUNK = 256


def _strict_upper(n):
    return jnp.where(lax.broadcasted_iota(jnp.int32, (n, n), 0)
                     < lax.broadcasted_iota(jnp.int32, (n, n), 1), 1.0, 0.0).astype(BF16)


def _strict_lower(n):
    return jnp.where(lax.broadcasted_iota(jnp.int32, (n, n), 1)
                     < lax.broadcasted_iota(jnp.int32, (n, n), 0), 1.0, 0.0).astype(BF16)


def _local_rows_bound(ts):
    rows = TOP_K * ts + N_EXPERTS * (ROW_ALIGN - 1)
    return -(-rows // P_CHUNK) * P_CHUNK


def _piece_cols(ts):
    return -(-(_local_rows_bound(ts) // ROW_ALIGN) // LANES) * LANES


def _sorted_tiles_bound(T):
    rows = TOP_K * T + (T // SORT_TILE) * N_EXPERTS * (ROW_ALIGN - 1)
    return -(-rows // EXP_TILE) + N_EXPERTS


def _sorted_tiles_expected(T):
    groups = (T // SORT_TILE) * N_EXPERTS
    rows = TOP_K * T + groups * ((ROW_ALIGN - 1) / 2 + 2)
    return int(-(-rows // EXP_TILE) + math.ceil(0.65 * N_EXPERTS))


def _moe_meta_kernel(gt_ref, ptab_ref, loc_et_ref, np_et_ref, loc_te_ref, np_te_ref, tot_ref,
                     erow_ref, texp_ref, nused_ref):
    E, T = gt_ref.shape
    mask = jnp.where(gt_ref[...] > 0.0, 1.0, 0.0).astype(BF16)
    t_id = lax.shift_right_logical(lax.broadcasted_iota(jnp.int32, (T, LANES), 0),
                                   int(math.log2(SORT_TILE)))
    tind = jnp.where(t_id == lax.broadcasted_iota(jnp.int32, (T, LANES), 1), 1.0, 0.0)
    cnt = _dot(mask, tind.astype(BF16))
    n16 = jnp.floor((cnt + (ROW_ALIGN - 1.0)) * (1.0 / ROW_ALIGN))
    n16b = n16.astype(BF16)
    q = EXP_TILE // ROW_ALIGN
    len16 = jnp.sum(n16, axis=1, keepdims=True)
    pad16 = jnp.floor((len16 + (q - 1.0)) * (1.0 / q)) * q
    sl = _strict_lower(E)
    hi, mid, lo = _split3(jnp.broadcast_to(pad16, (E, LANES)))
    start16 = _dot(sl, hi) + _dot(sl, mid) + _dot(sl, lo)
    gdst16 = start16 + _dot(n16b, _strict_upper(LANES))
    loc16 = _dot(sl, n16b)

    def t(a):
        return jnp.concatenate([a, jnp.zeros((LANES - E, LANES), F32)], axis=0).T

    scale = float(ROW_ALIGN)
    loc_et_ref[...] = loc16 * scale
    np_et_ref[...] = n16 * scale
    loc_te_ref[...] = t(loc16) * scale
    np_te_ref[...] = t(n16) * scale
    tot_ref[...] = (jnp.sum(n16, axis=0, keepdims=True) * scale).astype(jnp.int32)

    n_t, n_blk = ptab_ref.shape
    blk = lax.broadcasted_iota(jnp.int32, (E, n_blk), 1).astype(F32)
    for tile in range(n_t):
        lo_c = loc16[:, tile:tile + 1]
        inside = (lo_c <= blk) & (blk < lo_c + n16[:, tile:tile + 1])
        dst = jnp.sum(jnp.where(inside, gdst16[:, tile:tile + 1] + (blk - lo_c), 0.0),
                      axis=0, keepdims=True)
        ptab_ref[tile:tile + 1, :] = (dst * scale).astype(jnp.int32)
    ends = jnp.concatenate([t(start16 + len16)[0:1, :], t(start16 + pad16)[0:1, :],
                            jnp.zeros((erow_ref.shape[0] - 2, LANES), F32)], axis=0)
    erow_ref[...] = (ends * scale).astype(jnp.int32)
    n_tab = texp_ref.shape[1]
    tile_row16 = (lax.broadcasted_iota(jnp.int32, (E, n_tab), 1) * q).astype(F32)
    owner = jnp.sum(jnp.where(start16[:, 0:1] <= tile_row16, 1.0, 0.0), axis=0, keepdims=True)
    texp_ref[...] = (owner - 1.0).astype(jnp.int32)
    n_used = jnp.sum(pad16, axis=0, keepdims=True) * (1.0 / q)
    nused_ref[...] = jnp.broadcast_to(n_used, (1, LANES)).astype(jnp.int32)


def _moe_meta(gates_t, n_tab):
    E, T = gates_t.shape
    i32 = jnp.int32
    return pl.pallas_call(
        _moe_meta_kernel,
        out_shape=[jax.ShapeDtypeStruct((T // SORT_TILE, _piece_cols(SORT_TILE)), i32),
                   jax.ShapeDtypeStruct((E, LANES), F32),
                   jax.ShapeDtypeStruct((E, LANES), F32),
                   jax.ShapeDtypeStruct((LANES, LANES), F32),
                   jax.ShapeDtypeStruct((LANES, LANES), F32),
                   jax.ShapeDtypeStruct((1, LANES), i32),
                   jax.ShapeDtypeStruct((8, LANES), i32),
                   jax.ShapeDtypeStruct((1, n_tab), i32),
                   jax.ShapeDtypeStruct((1, LANES), i32)],
        compiler_params=pltpu.CompilerParams(vmem_limit_bytes=VMEM_LIMIT),
        name="moe_meta",
    )(gates_t)


def _start_pieces(tile, ptab_s, tot_s, n_cols, make_copy):
    n_pieces = lax.shift_right_logical(tot_s[tile], int(math.log2(ROW_ALIGN)))

    def body(b, carry):
        make_copy(pl.multiple_of(b * ROW_ALIGN, ROW_ALIGN),
                  pl.multiple_of(ptab_s[tile * n_cols + b], ROW_ALIGN)).start()
        return carry

    lax.fori_loop(0, n_pieces, body, 0)


def _moe_sort_kernel(ptab_s, tot_s, lend_s, rend_s, gt_ref, gtok_ref, h_ref, locrow_ref,
                     nprow_ref, xs_hbm, buf, zbuf, sem, zsem):
    tau = pl.program_id(0)
    n_t = pl.num_programs(0)
    slot = lax.rem(tau, 2)
    E, ts = gt_ref.shape
    D = h_ref.shape[1]
    n_cols = _piece_cols(ts)

    def copies(tile, sl, wait):
        if wait:
            rows = pl.multiple_of(tot_s[tile], ROW_ALIGN)

            @pl.when(rows > 0)
            def _():
                pltpu.make_async_copy(buf.at[sl, pl.ds(0, rows)], xs_hbm.at[pl.ds(0, rows)],
                                      sem.at[sl]).wait()
        else:
            def make_copy(loc, dst):
                return pltpu.make_async_copy(buf.at[sl, pl.ds(loc, ROW_ALIGN)],
                                             xs_hbm.at[pl.ds(dst, ROW_ALIGN)], sem.at[sl])
            _start_pieces(tile, ptab_s, tot_s, n_cols, make_copy)

    z_rows = zbuf.shape[0]
    used_rows = rend_s[E - 1]
    n_spare = (xs_hbm.shape[0] - used_rows) // z_rows

    def spare_fill(wait):
        def body(c, carry):
            dst = pl.multiple_of(used_rows + c * z_rows, z_rows)
            cp = pltpu.make_async_copy(zbuf, xs_hbm.at[pl.ds(dst, z_rows)], zsem.at[1])
            if wait:
                cp.wait()
            else:
                cp.start()
            return carry

        lax.fori_loop(0, n_spare, body, 0)

    @pl.when(tau == 0)
    def _():
        zbuf[...] = jnp.zeros_like(zbuf)
        spare_fill(False)

    @pl.when(tau >= 2)
    def _():
        copies(tau - 2, slot, True)

    g = gt_ref[...]
    mask = g > 0.0
    maskb = jnp.where(mask, 1.0, 0.0).astype(BF16)
    pad = jnp.zeros((LANES - E, ts), F32)
    pos = jnp.where(mask, _dot(maskb, _strict_upper(ts)), -1.0)
    pos = jnp.concatenate([pos, pad], axis=0).astype(BF16)
    lo_row = locrow_ref[...]
    hi_row = lo_row + nprow_ref[...]
    h = jnp.concatenate([h_ref[...]] + list(_split3(gtok_ref[...])), axis=1)
    n_chunks = lax.shift_right_logical(tot_s[tau] + (P_CHUNK - 1), int(math.log2(P_CHUNK)))
    lane = lax.broadcasted_iota(jnp.int32, (P_CHUNK, LANES), 1)

    def chunk(c):
        r0 = pl.multiple_of(c * P_CHUNK, P_CHUNK)
        r = (r0 + lax.broadcasted_iota(jnp.int32, (P_CHUNK, LANES), 0)).astype(F32)
        inside = (lo_row <= r) & (r < hi_row)
        group = jnp.where(inside, 1.0, 0.0).astype(BF16)
        want = r[:, 0:1] - jnp.sum(jnp.where(inside, lo_row, 0.0), axis=1, keepdims=True)
        hit = _dot(group, pos) == want
        rows = _dot(jnp.where(hit, 1.0, 0.0).astype(BF16), h)
        buf[slot, pl.ds(r0, P_CHUNK), 0:D] = rows[:, 0:D].astype(buf.dtype)
        extra = jnp.zeros((P_CHUNK, LANES), F32)
        for k in range(3):
            mine = jnp.where(inside, rows[:, D + k * LANES:D + (k + 1) * LANES], 0.0)
            extra = jnp.where(lane == k, jnp.sum(mine, axis=1, keepdims=True), extra)
        buf[slot, pl.ds(r0, P_CHUNK), D:D + LANES] = extra.astype(buf.dtype)

    def pair_body(c, carry):
        chunk(2 * c)
        chunk(2 * c + 1)
        return carry

    lax.fori_loop(0, lax.shift_right_logical(n_chunks + 1, 1), pair_body, 0)
    copies(tau, slot, False)

    @pl.when(tau == n_t - 1)
    def _():
        @pl.when(n_t >= 2)
        def _():
            copies(tau - 1, 1 - slot, True)
        copies(tau, slot, True)
        spare_fill(True)

        sizes =[zbuf.shape[0] >> s for s in range(int(math.log2(zbuf.shape[0] // ROW_ALIGN)) + 1)]

        def fill(wait):
            def e_body(e, carry):
                start = lend_s[e]
                n = rend_s[e] - start
                off = start
                for size in sizes:
                    bit = jnp.bitwise_and(n, size)

                    @pl.when(bit != 0)
                    def _(off=off, size=size):
                        cp = pltpu.make_async_copy(
                            zbuf.at[pl.ds(0, size)],
                            xs_hbm.at[pl.ds(pl.multiple_of(off, ROW_ALIGN), size)], zsem.at[0])
                        if wait:
                            cp.wait()
                        else:
                            cp.start()

                    off = off + bit
                return carry

            lax.fori_loop(0, E, e_body, 0)

        fill(False)
        fill(True)


def _moe_sort(ptab, tot, lend, rend, gates_t, gates, h2, loc_te, np_te, n_rows):
    E, T = gates_t.shape
    D = h2.shape[1]
    ts = SORT_TILE
    grid_spec = pltpu.PrefetchScalarGridSpec(
        num_scalar_prefetch=4,
        grid=(T // ts,),
        in_specs=[pl.BlockSpec((E, ts), lambda t, *_: (0, t)),
                  pl.BlockSpec((ts, LANES), lambda t, *_: (t, 0)),
                  pl.BlockSpec((ts, D), lambda t, *_: (t, 0)),
                  pl.BlockSpec((None, 1, LANES), lambda t, *_: (t, 0, 0)),
                  pl.BlockSpec((None, 1, LANES), lambda t, *_: (t, 0, 0))],
        out_specs=pl.BlockSpec(memory_space=pl.ANY),
        scratch_shapes=[pltpu.VMEM((2, _local_rows_bound(ts), D + LANES), BF16),
                        pltpu.VMEM((EXP_TILE // 2, D + LANES), BF16),
                        pltpu.SemaphoreType.DMA((2,)),
                        pltpu.SemaphoreType.DMA((2,))])
    return pl.pallas_call(
        _moe_sort_kernel,
        grid_spec=grid_spec,
        out_shape=jax.ShapeDtypeStruct((n_rows, D + LANES), BF16),
        compiler_params=pltpu.CompilerParams(dimension_semantics=("arbitrary",),
                                             vmem_limit_bytes=VMEM_LIMIT),
        name="moe_sort",
    )(ptab, tot, lend, rend, gates_t, gates, h2, loc_te, np_te)


def _moe_expert_kernel(texp_s, nused_s, x_ref, wg_ref, wu_ref, wd_ref, y_ref, wgu_s, wd_s):
    i = pl.program_id(0)
    f = wd_ref.shape[0]

    @pl.when(i < nused_s[0])
    def _():
        @pl.when((i == 0) | (texp_s[i] != texp_s[jnp.maximum(i - 1, 0)]))
        def _():
            wgu_s[:, 0:f] = wg_ref[...].astype(BF16)
            wgu_s[:, f:2 * f] = wu_ref[...].astype(BF16)
            wd_s[...] = wd_ref[...].astype(BF16)

        d = wd_ref.shape[1]
        for r0 in range(0, x_ref.shape[0], EXP_STRIP):
            rows = slice(r0, r0 + EXP_STRIP)
            gate = jnp.sum(x_ref[rows, d:].astype(F32), axis=1, keepdims=True)
            a = _dot(x_ref[rows, 0:d], wgu_s[...])
            act = _silu(a[:, :f]) * a[:, f:] * gate
            y_ref[rows, :] = _dot(act.astype(BF16), wd_s[...]).astype(y_ref.dtype)

    @pl.when(i >= nused_s[0])
    def _():
        y_ref[...] = jnp.zeros_like(y_ref)


def _moe_expert(texp, nused, xs, w_gate, w_up, w_down, n_tiles):
    n_rows, xw = xs.shape
    D, f = w_gate.shape[-2:]
    tm = EXP_TILE

    def tile(i, texp, nused):
        return jnp.maximum(jnp.minimum(i, nused[0] - 1), 0)

    grid_spec = pltpu.PrefetchScalarGridSpec(
        num_scalar_prefetch=2,
        grid=(n_tiles,),
        in_specs=[pl.BlockSpec((tm, xw), lambda i, te, nu: (tile(i, te, nu), 0)),
                  pl.BlockSpec((None, D, f), lambda i, te, nu: (te[tile(i, te, nu)], 0, 0)),
                  pl.BlockSpec((None, D, f), lambda i, te, nu: (te[tile(i, te, nu)], 0, 0)),
                  pl.BlockSpec((None, f, D), lambda i, te, nu: (te[tile(i, te, nu)], 0, 0))],
        out_specs=pl.BlockSpec((tm, D), lambda i, te, nu: (i, 0)),
        scratch_shapes=[pltpu.VMEM((D, 2 * f), BF16), pltpu.VMEM((f, D), BF16)])
    return pl.pallas_call(
        _moe_expert_kernel,
        grid_spec=grid_spec,
        out_shape=jax.ShapeDtypeStruct((n_rows, D), BF16),
        compiler_params=pltpu.CompilerParams(dimension_semantics=("arbitrary",),
                                             vmem_limit_bytes=VMEM_LIMIT),
        name="moe_expert",
    )(texp, nused, xs, w_gate, w_up, w_down)


def _moe_combine_kernel(alpha, ptab_s, tot_s, g_ref, loc_ref, np_ref, h_ref, x1_ref,
                        mod_ref, sgu_ref, sd_ref, lg_ref, lb_ref, y_hbm, o_ref, ybuf, acc_ref,
                        sem):
    tau = pl.program_id(0)
    n_t = pl.num_programs(0)
    slot = lax.rem(tau, 2)
    ts, n_lane = g_ref.shape
    E = loc_ref.shape[0]
    n_cols = _piece_cols(ts)

    def copies(tile, sl, wait):
        if wait:
            rows = pl.multiple_of(tot_s[tile], ROW_ALIGN)

            @pl.when(rows > 0)
            def _():
                pltpu.make_async_copy(y_hbm.at[pl.ds(0, rows)], ybuf.at[sl, pl.ds(0, rows)],
                                      sem.at[sl]).wait()
        else:
            def make_copy(loc, dst):
                return pltpu.make_async_copy(y_hbm.at[pl.ds(dst, ROW_ALIGN)],
                                             ybuf.at[sl, pl.ds(loc, ROW_ALIGN)], sem.at[sl])
            _start_pieces(tile, ptab_s, tot_s, n_cols, make_copy)

    @pl.when(tau == 0)
    def _():
        ybuf[...] = jnp.zeros_like(ybuf)
        copies(0, 0, False)

    @pl.when(tau + 1 < n_t)
    def _():
        copies(tau + 1, 1 - slot, False)

    g = g_ref[...]
    mask = g > 0.0
    maskb = jnp.where(mask, 1.0, 0.0).astype(BF16)
    pos = jnp.where(mask, _dot(_strict_lower(ts), maskb), -1.0).astype(BF16)
    lane = lax.broadcasted_iota(jnp.int32, loc_ref.shape, 1)
    lo_col = jnp.sum(jnp.where(lane == tau, loc_ref[...], 0.0), axis=1, keepdims=True)
    hi_col = lo_col + jnp.sum(jnp.where(lane == tau, np_ref[...], 0.0), axis=1, keepdims=True)

    f = sd_ref.shape[0]
    a = _dot(h_ref[...], sgu_ref[...])
    acc_ref[...] = _dot((_silu(a[:, :f]) * a[:, f:]).astype(BF16), sd_ref[...])

    copies(tau, slot, True)
    n_chunks = lax.shift_right_logical(tot_s[tau] + (P_CHUNK - 1), int(math.log2(P_CHUNK)))

    def chunk(c):
        r0 = pl.multiple_of(c * P_CHUNK, P_CHUNK)
        r = (r0 + lax.broadcasted_iota(jnp.int32, (E, P_CHUNK), 1)).astype(F32)
        inside = (lo_col <= r) & (r < hi_col)
        group = jnp.concatenate([jnp.where(inside, 1.0, 0.0),
                                 jnp.zeros((n_lane - E, P_CHUNK), F32)], axis=0).astype(BF16)
        want = r[0:1, :] - jnp.sum(jnp.where(inside, lo_col, 0.0), axis=0, keepdims=True)
        hit = _dot(pos, group) == want
        return _dot(jnp.where(hit, 1.0, 0.0).astype(BF16), ybuf[slot, pl.ds(r0, P_CHUNK), :])

    def pair_body(c, carry):
        acc_ref[...] += chunk(2 * c) + chunk(2 * c + 1)
        return carry

    lax.fori_loop(0, lax.shift_right_logical(n_chunks + 1, 1), pair_body, 0)
    y = alpha * x1_ref[...] + mod_ref[5:6, :] * acc_ref[...]
    o_ref[...] = _layer_norm(y, lg_ref[...], lb_ref[...])


def _moe_combine(alpha, ptab, tot, gates, loc_et, np_et, h2, x1, mod, sgu, sd, ln_g, ln_b, ys,
                 S):
    T, D = h2.shape
    ts = SORT_TILE
    per_b = S // ts
    row = lambda a: pl.BlockSpec(a.shape, lambda t, *_: (0, 0))
    grid_spec = pltpu.PrefetchScalarGridSpec(
        num_scalar_prefetch=2,
        grid=(T // ts,),
        in_specs=[pl.BlockSpec((ts, LANES), lambda t, *_: (t, 0)),
                  row(loc_et), row(np_et),
                  pl.BlockSpec((ts, D), lambda t, *_: (t, 0)),
                  pl.BlockSpec((ts, D), lambda t, *_: (t, 0)),
                  pl.BlockSpec((None, 6, D), lambda t, *_: (t // per_b, 0, 0)),
                  row(sgu), row(sd), row(ln_g), row(ln_b),
                  pl.BlockSpec(memory_space=pl.ANY)],
        out_specs=pl.BlockSpec((ts, D), lambda t, *_: (t, 0)),
        scratch_shapes=[pltpu.VMEM((2, _local_rows_bound(ts), D), BF16),
                        pltpu.VMEM((ts, D), F32),
                        pltpu.SemaphoreType.DMA((2,))])
    return pl.pallas_call(
        functools.partial(_moe_combine_kernel, alpha),
        grid_spec=grid_spec,
        out_shape=jax.ShapeDtypeStruct((T, D), F32),
        compiler_params=pltpu.CompilerParams(dimension_semantics=("arbitrary",),
                                             vmem_limit_bytes=VMEM_LIMIT),
        name="moe_combine",
    )(ptab, tot, gates, loc_et, np_et, h2, x1, mod, sgu, sd, ln_g, ln_b, ys)


def _moe(alpha, h2, x1, gates, gates_t, mod, w_gate, w_up, w_down, sgu, sd, ln_g, ln_b):
    B, S, D = x1.shape
    T = B * S
    n_t = T // SORT_TILE
    n_bound = _sorted_tiles_bound(T)
    n_tab = -(-n_bound // LANES) * LANES
    ptab, loc_et, np_et, loc_te, np_te, tot, erow, texp, nused = _moe_meta(gates_t, n_tab)
    ptab = ptab.reshape(-1)
    tot = tot[0, :n_t]
    h2 = h2.reshape(T, D)
    x1 = x1.reshape(T, D)
    gates = gates.reshape(T, LANES)
    loc_rows = loc_te[:n_t].reshape(n_t, 1, LANES)
    np_rows = np_te[:n_t].reshape(n_t, 1, LANES)

    def sized(n_tiles):
        def run():
            xs = _moe_sort(ptab, tot, erow[0, :N_EXPERTS], erow[1, :N_EXPERTS], gates_t, gates,
                           h2, loc_rows, np_rows, n_tiles * EXP_TILE)
            ys = _moe_expert(texp[0], nused[0, :1], xs, w_gate, w_up, w_down, n_tiles)
            return _moe_combine(alpha, ptab, tot, gates, loc_et, np_et, h2, x1, mod, sgu, sd,
                                ln_g, ln_b, ys, S)
        return run

    n_small = min(_sorted_tiles_expected(T), n_bound)
    out = lax.cond(nused[0, 0] <= n_small, sized(n_small), sized(n_bound))
    return out.reshape(B, S, D)


def _rearrange_w_in(w):
    d_in = w.shape[0]
    scale = HEAD_DIM ** -0.5 * LOG2E
    fq, fk, fv = w[:, 0:512], w[:, 512:1024], w[:, 1024:1536]
    ff = w[:, 1536:1544]
    nq = w[:, 1544:2056]
    kc, vc, ks, vs, kw, vw = (w[:, 2056 + k * LANES:2056 + (k + 1) * LANES] for k in range(6))
    ng = w[:, 2824:2848]
    pad = jnp.zeros((d_in, LANES - ff.shape[1] - ng.shape[1]), w.dtype)
    cols = [fq * scale, fk, nq * scale, kc, kw, vc, ks, fv, vs, vw, ff, ng, pad]
    return jnp.concatenate(cols, axis=1).astype(BF16)


def _compress_weights(pos, w1, w2):
    half = CMP_BLOCK // 2
    w1r = w1.reshape(2, half, HEAD_DIM, CMP_HIDDEN)
    zeros = jnp.zeros_like(w1r[0])
    def spread(part):
        g0 = jnp.stack([part, zeros], axis=1).reshape(half * 2 * HEAD_DIM, CMP_HIDDEN)
        g1 = jnp.stack([zeros, part], axis=1).reshape(half * 2 * HEAD_DIM, CMP_HIDDEN)
        return jnp.concatenate([g0, g1], axis=1).astype(BF16)
    wa, wb = spread(w1r[0]), spread(w1r[1])
    z2 = jnp.zeros_like(w2)
    w2bd = jnp.concatenate([jnp.concatenate([w2, z2], axis=1),
                            jnp.concatenate([z2, w2], axis=1)], axis=0).astype(BF16)
    posr = pos.reshape(2, half, 1, HEAD_DIM)
    posr = jnp.broadcast_to(posr, (2, half, NSA_GROUPS, HEAD_DIM)).reshape(2, half * 2 * HEAD_DIM)
    return posr, wa, wb, w2bd


@functools.lru_cache(maxsize=None)
def _static_tables(S):
    tq = ATT_TILE
    n_cmp = (S - CMP_BLOCK) // CMP_STRIDE + 1
    n_pad = S // CMP_STRIDE
    n_slc = S // SEL_BLOCK
    t = np.arange(S)[:, None]
    n = np.arange(n_pad)[None, :]
    bucket_c = _t5_bucket_np(t - (n * CMP_STRIDE + CMP_BLOCK - 1)).reshape(1, -1)
    d = (np.arange(4)[:, None, None] * tq + np.arange(tq)[None, :, None]
         - np.arange(2 * tq)[None, None, :])
    bucket_w = _t5_bucket_np(d).reshape(1, -1)
    cs = np.arange(n_pad)[None, :] * CMP_STRIDE
    sj = np.arange(n_slc)[:, None] * SEL_BLOCK
    ovl_t = ((cs < sj + SEL_BLOCK) & (cs + CMP_BLOCK > sj) & (np.arange(n_pad)[None, :] < n_cmp))
    return bucket_c, bucket_w, ovl_t.astype(np.float32)


def kernel(x, c, w_ada, b_ada, w_in, b_f, cmp_pos_k, cmp_w1_k, cmp_w2_k, cmp_pos_v, cmp_w1_v,
           cmp_w2_v, rel_bias, w_out, ln1_g, ln1_b, w_router, e_bias, w_gate, w_up, w_down,
           ws_gate, ws_up, ws_down, ln2_g, ln2_b):
    B, S, D = x.shape
    depth = w_ada.shape[0]
    alpha = (2 * depth) ** 0.25
    tq = ATT_TILE
    bucket_c, bucket_w, ovl_t = _static_tables(S)
    rel_bias_t = rel_bias.T * LOG2E
    bias_c = _bias_table(jnp.asarray(bucket_c), rel_bias_t).reshape(NSA_HEADS, S, S // CMP_STRIDE)
    w4 = _bias_table(jnp.asarray(bucket_w), rel_bias_t).reshape(NSA_HEADS, 4, tq, 2 * tq)
    ovl_t = jnp.asarray(ovl_t, BF16)

    for l in range(depth):
        mod = _ada(c, w_ada[l], b_ada[l]).reshape(B, 6, D)
        bf_row = jnp.zeros((1, LANES), F32).at[0, :FOX_HEADS].set(b_f[l])
        (fq, fk, nq, kc, kw, vc, ks, fv, vs, vw, misc, misc_t) = _in_proj(
            x, mod, _rearrange_w_in(w_in[l]), bf_row)

        o_fox = _fox(fq, fk, fv, misc_t[:, :FOX_HEADS, :])

        pk, wak, wbk, w2k = _compress_weights(cmp_pos_k[l], cmp_w1_k[l], cmp_w2_k[l])
        pv, wav, wbv, w2v = _compress_weights(cmp_pos_v[l], cmp_w1_v[l], cmp_w2_v[l])
        rows = S // CMP_STRIDE
        kcmp, vcmp = _compress(kc.reshape(B, rows, CMP_STRIDE * LANES),
                               vc.reshape(B, rows, CMP_STRIDE * LANES),
                               pk, pv, wak, wbk, wav, wbv, w2k, w2v)

        oc, sel = _cmp_sel(nq, kcmp, vcmp, bias_c, misc, ovl_t)
        o_nsa = _nsa(nq, ks, vs, kw, vw, sel, w4, misc, oc)

        x1, h2, gates, gates_t = _out_proj(
            alpha, o_fox, o_nsa, x, mod, w_out[l].astype(BF16), ln1_g[l].reshape(1, D),
            ln1_b[l].reshape(1, D), w_router[l].T, e_bias[l].reshape(N_EXPERTS, 1))

        sgu = jnp.concatenate([ws_gate[l], ws_up[l]], axis=-1).astype(BF16)
        x = _moe(alpha, h2, x1, gates, gates_t, mod, w_gate[l], w_up[l], w_down[l], sgu,
                 ws_down[l].astype(BF16), ln2_g[l].reshape(1, D), ln2_b[l].reshape(1, D))
    return x
```

```python
import functools
import math

import jax
import jax.numpy as jnp
import numpy as np
from jax import lax
from jax.experimental import pallas as pl
from jax.experimental.pallas import tpu as pltpu

F32 = jnp.float32
BF16 = jnp.bfloat16

HEAD_DIM = 64
FOX_HEADS = 8
NSA_HEADS = 8
NSA_GQA = 4
NSA_GROUPS = NSA_HEADS // NSA_GQA
CMP_BLOCK = 32
CMP_STRIDE = 16
CMP_HIDDEN = 256
SEL_BLOCK = 64
N_SEL = 16
WINDOW = 512
N_BUCKETS = 32
MAX_DISTANCE = 128
N_EXPERTS = 64
N_EXPERT_GROUPS = 8
GROUP_SIZE = N_EXPERTS // N_EXPERT_GROUPS
TOPK_GROUPS = 4
TOP_K = 8
ROUTED_SCALE = 2.5
LN_EPS = 1e-5
NEG_BIG = -1e30
FORCE_SCORE = 1e4

LANES = 128
ATT_TILE = 256
FOX_TILE = 512
CMP_TILE = 512
ROW_TILE = 512
VMEM_LIMIT = 48 * 1024 * 1024

NT_DIMS = (((1,), (1,)), ((), ()))


def _dot(a, b):
    return jnp.dot(a, b, preferred_element_type=F32)


def _dot_nt(a, b):
    return lax.dot_general(a, b, NT_DIMS, preferred_element_type=F32)


def _split3(x):
    hi = x.astype(BF16)
    r1 = x - hi.astype(F32)
    mid = r1.astype(BF16)
    lo = (r1 - mid.astype(F32)).astype(BF16)
    return hi, mid, lo


def _silu(x):
    return x / (1.0 + jnp.exp(-x))


def _sigmoid(x):
    return 1.0 / (1.0 + jnp.exp(-x))


def _swap_halves(x):
    return pltpu.roll(x, HEAD_DIM, 1)


def _t5_bucket_np(dist):
    n = np.maximum(dist, 0)
    max_exact = N_BUCKETS // 2
    nf = np.maximum(n, 1).astype(np.float32)
    large = max_exact + (np.log(nf / max_exact) / math.log(MAX_DISTANCE / max_exact)
                         * (N_BUCKETS - max_exact)).astype(np.int32)
    large = np.minimum(large, N_BUCKETS - 1)
    return np.where(n < max_exact, n, large).astype(np.int32)


def _ada_kernel(c_ref, w_ref, b_ref, o_ref):
    c = c_ref[...]
    o_ref[...] = jnp.dot(_silu(c), w_ref[...], preferred_element_type=F32,
                         precision=lax.Precision.HIGHEST) + b_ref[...]


def _ada(c, w_ada, b_ada):
    B, D = c.shape
    n_out = w_ada.shape[1]
    tn = 1024
    return pl.pallas_call(
        _ada_kernel,
        grid=(n_out // tn,),
        in_specs=[pl.BlockSpec((B, D), lambda j: (0, 0)),
                  pl.BlockSpec((D, tn), lambda j: (0, j)),
                  pl.BlockSpec((1, tn), lambda j: (0, j))],
        out_specs=pl.BlockSpec((B, tn), lambda j: (0, j)),
        out_shape=jax.ShapeDtypeStruct((B, n_out), F32),
        compiler_params=pltpu.CompilerParams(dimension_semantics=("arbitrary",),
                                             vmem_limit_bytes=VMEM_LIMIT),
        name="ada",
    )(c, w_ada, b_ada.reshape(1, n_out))


def _bias_table_kernel(bkt_ref, rbt_ref, o_ref):
    bkt = bkt_ref[...]
    k = lax.broadcasted_iota(jnp.int32, (N_BUCKETS, bkt.shape[1]), 0)
    onehot = jnp.where(k == bkt, 1.0, 0.0).astype(BF16)
    hi, mid, lo = _split3(rbt_ref[...])
    o_ref[...] = _dot(hi, onehot) + _dot(mid, onehot) + _dot(lo, onehot)


def _bias_table(bucket, rel_bias_t):
    n = bucket.shape[1]
    chunk = 32768
    n_heads = rel_bias_t.shape[0]
    return pl.pallas_call(
        _bias_table_kernel,
        grid=(n // chunk,),
        in_specs=[pl.BlockSpec((1, chunk), lambda j: (0, j)),
                  pl.BlockSpec(rel_bias_t.shape, lambda j: (0, 0))],
        out_specs=pl.BlockSpec((n_heads, chunk), lambda j: (0, j)),
        out_shape=jax.ShapeDtypeStruct((n_heads, n), F32),
        compiler_params=pltpu.CompilerParams(dimension_semantics=("parallel",),
                                             vmem_limit_bytes=VMEM_LIMIT),
        name="bias_table",
    )(bucket, rel_bias_t)


_C_FQ, _C_FK, _C_NQ = 0, 512, 1024
_C_K3 = 1536
_C_SK = 1920
_C_FV = 2048
_C_SV = 2560
_C_WV = 2688
_C_MISC = 2816
_IN_COLS = 2944
LOG2E = math.log2(math.e)


def _in_proj_kernel(x_ref, mod_ref, w_ref, bf_ref, fq_ref, fk_ref, nq_ref, kc_ref, kw_ref,
                    vc_ref, ks_ref, fv_ref, vs_ref, vw_ref, misc_ref, misct_ref, carry_ref):
    s_idx = pl.program_id(1)
    tm = x_ref.shape[0]
    mod = mod_ref[...]
    h = (x_ref[...] * (1.0 + mod[1:2, :]) + mod[0:1, :]).astype(BF16)

    for ref, c0 in ((fq_ref, _C_FQ), (fk_ref, _C_FK), (nq_ref, _C_NQ)):
        ref[...] = _dot(h, w_ref[:, c0:c0 + 512]).astype(ref.dtype)
    keys = _dot(h, w_ref[:, _C_K3:_C_FV])
    fox_v = _dot(h, w_ref[:, _C_FV:_C_SV])
    tail = _dot(h, w_ref[:, _C_SV:_IN_COLS])
    for k, ref in enumerate((kc_ref, kw_ref, vc_ref)):
        ref[...] = keys[:, k * LANES:(k + 1) * LANES].astype(ref.dtype)

    lane = lax.broadcasted_iota(jnp.int32, (tm, LANES), 1)
    low = lane < HEAD_DIM

    def spread(ref, cols, fill):
        for p in range(cols.shape[-1] // LANES):
            r = cols[:, p * LANES:(p + 1) * LANES]
            ref[:, 2 * p * LANES:(2 * p + 1) * LANES] = jnp.where(low, r, fill).astype(ref.dtype)
            ref[:, (2 * p + 1) * LANES:(2 * p + 2) * LANES] = jnp.where(
                low, _swap_halves(r), fill).astype(ref.dtype)

    key_blk = lax.shift_right_logical(
        s_idx * tm + lax.broadcasted_iota(jnp.int32, (tm, LANES), 0), int(math.log2(SEL_BLOCK)))
    spread(ks_ref, keys[:, 3 * LANES:4 * LANES],
           jnp.where(lane == HEAD_DIM + key_blk, 1.0, 0.0))
    spread(fv_ref, fox_v, 1.0)
    spread(vs_ref, tail[:, 0:LANES], 1.0)
    spread(vw_ref, tail[:, LANES:2 * LANES], 1.0)

    z = tail[:, 2 * LANES:3 * LANES] + bf_ref[...]
    is_f = lane < FOX_HEADS
    log_f = jnp.minimum(z, 0.0) - jnp.log(1.0 + jnp.exp(-jnp.abs(z)))
    log_f = jnp.where(is_f, log_f, 0.0)

    row = lax.broadcasted_iota(jnp.int32, (tm, tm), 0)
    col = lax.broadcasted_iota(jnp.int32, (tm, tm), 1)
    tri = jnp.where(row >= col, 1.0, 0.0).astype(BF16)
    sums = _dot(tri, jnp.concatenate(_split3(log_f), axis=1))
    local = sums[:, 0:LANES] + sums[:, LANES:2 * LANES] + sums[:, 2 * LANES:3 * LANES]

    @pl.when(s_idx == 0)
    def _():
        carry_ref[...] = jnp.zeros_like(carry_ref)

    cum = local + carry_ref[...]
    carry_ref[...] = cum[tm - 1:tm, :]
    misc = jnp.where(is_f, cum * LOG2E, _sigmoid(z))
    misc_ref[...] = misc
    misct_ref[...] = misc.T


def _in_proj(x, mod, w_r, bf_row):
    B, S, D = x.shape
    tm = ROW_TILE
    widths = (512, 512, 512, LANES, LANES, LANES, 2 * (_C_FV - _C_SK), 2 * (_C_SV - _C_FV),
              2 * (_C_WV - _C_SV), 2 * (_C_MISC - _C_WV))
    wide = lambda w: pl.BlockSpec((None, tm, w), lambda b, s: (b, s, 0))
    out_shape = ([jax.ShapeDtypeStruct((B, S, w), BF16) for w in widths]
                 + [jax.ShapeDtypeStruct((B, S, LANES), F32),
                    jax.ShapeDtypeStruct((B, LANES, S), F32)])
    out_specs = ([wide(w) for w in widths]
                 + [wide(LANES), pl.BlockSpec((None, LANES, tm), lambda b, s: (b, 0, s))])
    return pl.pallas_call(
        _in_proj_kernel,
        grid=(B, S // tm),
        in_specs=[pl.BlockSpec((None, tm, D), lambda b, s: (b, s, 0)),
                  pl.BlockSpec((None, 6, D), lambda b, s: (b, 0, 0)),
                  pl.BlockSpec((D, _IN_COLS), lambda b, s: (0, 0)),
                  pl.BlockSpec((1, LANES), lambda b, s: (0, 0))],
        out_specs=out_specs,
        out_shape=out_shape,
        scratch_shapes=[pltpu.VMEM((1, LANES), F32)],
        compiler_params=pltpu.CompilerParams(dimension_semantics=("parallel", "arbitrary"),
                                             vmem_limit_bytes=VMEM_LIMIT),
        name="in_proj",
    )(x, mod, w_r, bf_row)


def _softmax_weights(s, m):
    return jnp.exp2((s - m).astype(BF16))


def _flash_update(carry, s, vt):
    m, acc = carry
    m_new = jnp.maximum(m, jnp.max(s, axis=-1, keepdims=True))
    alpha = jnp.exp2(m - m_new)
    p = _softmax_weights(s, m_new)
    rows = acc.shape[0]
    acc = alpha.reshape(rows, 1) * acc + _dot(p.reshape(rows, s.shape[-1]), vt)
    return m_new, acc


def _normalize(acc):
    return acc / _swap_halves(acc)


def _fox_kernel(q_ref, k_ref, v_ref, ck_ref, o_ref):
    i = pl.program_id(2)
    tq = q_ref.shape[0]
    tk = ck_ref.shape[2]
    q2 = q_ref[...].astype(F32)
    lane = lax.broadcasted_iota(jnp.int32, (tq, LANES), 1)
    low = lane < HEAD_DIM
    halves = (low, jnp.logical_not(low))
    qh = [jnp.where(h, q2, 0.0).astype(BF16) for h in halves]
    col_minus_row = (lax.broadcasted_iota(jnp.int32, (tq, tk), 1)
                     - lax.broadcasted_iota(jnp.int32, (tq, tk), 0))

    def step(jj, carry, diagonal):
        k0 = jj * tk
        kt = k_ref[pl.ds(k0, tk), :]
        new = []
        for hh in range(2):
            s = _dot_nt(qh[hh], kt) - ck_ref[hh, pl.ds(jj, 1), :]
            if diagonal:
                s = jnp.where(col_minus_row <= 0, s, NEG_BIG)
            vt = v_ref[pl.ds(k0, tk), hh * LANES:(hh + 1) * LANES]
            new.append(_flash_update(carry[hh], s, vt))
        return tuple(new)

    init = tuple((jnp.full((tq, 1), NEG_BIG, F32), jnp.zeros((tq, LANES), F32))
                 for _ in range(2))

    assert tq == tk
    for n_full in range(k_ref.shape[0] // tk):
        @pl.when(i == n_full)
        def _(n_full=n_full):
            carry = init
            for jj in range(n_full):
                carry = step(jj, carry, False)
            carry = step(n_full, carry, True)
            o_ref[...] = jnp.where(low, _normalize(carry[0][1]),
                                   _swap_halves(_normalize(carry[1][1]))).astype(o_ref.dtype)


def _fox(fq, fk, fv, cum_row):
    B, S, W = fq.shape
    tq = tk = FOX_TILE
    n_pairs = W // LANES
    cum_row = cum_row.reshape(B, n_pairs, 2, S // tk, tk)
    return pl.pallas_call(
        _fox_kernel,
        grid=(B, n_pairs, S // tq),
        in_specs=[pl.BlockSpec((None, tq, LANES), lambda b, p, i: (b, i, p)),
                  pl.BlockSpec((None, S, LANES), lambda b, p, i: (b, 0, p)),
                  pl.BlockSpec((None, S, 2 * LANES), lambda b, p, i: (b, 0, p)),
                  pl.BlockSpec((None, None, 2, S // tk, tk), lambda b, p, i: (b, p, 0, 0, 0))],
        out_specs=pl.BlockSpec((None, tq, LANES), lambda b, p, i: (b, i, p)),
        out_shape=jax.ShapeDtypeStruct((B, S, W), BF16),
        compiler_params=pltpu.CompilerParams(
            dimension_semantics=("parallel", "parallel", "arbitrary"),
            vmem_limit_bytes=VMEM_LIMIT),
        name="fox",
    )(fq, fk, fv, cum_row)


def _compress_kernel(xk_ref, xv_ref, pk_ref, pv_ref, wak_ref, wbk_ref, wav_ref, wbv_ref,
                     w2k_ref, w2v_ref, ok_ref, ov_ref):
    n_rows = xk_ref.shape[0]
    for x_ref, p_ref, wa_ref, wb_ref, w2_ref, o_ref in (
            (xk_ref, pk_ref, wak_ref, wbk_ref, w2k_ref, ok_ref),
            (xv_ref, pv_ref, wav_ref, wbv_ref, w2v_ref, ov_ref)):
        x = x_ref[...].astype(F32)
        xa = (x + p_ref[0:1, :]).astype(BF16)
        xb = (x + p_ref[1:2, :]).astype(BF16)
        hb = _dot(xb, wb_ref[...])
        h1 = _dot(xa, wa_ref[...]) + pltpu.roll(hb, n_rows - 1, 0)
        o_ref[...] = _dot(_silu(h1).astype(BF16), w2_ref[...]).astype(o_ref.dtype)


def _compress(xk, xv, pk, pv, wak, wbk, wav, wbv, w2k, w2v):
    B, R, C = xk.shape
    xspec = pl.BlockSpec((None, R, C), lambda b: (b, 0, 0))
    full = lambda a: pl.BlockSpec(a.shape, lambda b: (0,) * a.ndim)
    ospec = pl.BlockSpec((None, R, LANES), lambda b: (b, 0, 0))
    return pl.pallas_call(
        _compress_kernel,
        grid=(B,),
        in_specs=[xspec, xspec] + [full(a) for a in (pk, pv, wak, wbk, wav, wbv, w2k, w2v)],
        out_specs=[ospec, ospec],
        out_shape=[jax.ShapeDtypeStruct((B, R, LANES), BF16)] * 2,
        compiler_params=pltpu.CompilerParams(dimension_semantics=("parallel",),
                                             vmem_limit_bytes=VMEM_LIMIT),
        name="compress",
    )(xk, xv, pk, pv, wak, wbk, wav, wbv, w2k, w2v)


def _rank_rows(score):
    n = score.shape[0]
    j = lax.broadcasted_iota(jnp.int32, score.shape, 0)
    rank = jnp.zeros(score.shape, jnp.int32)
    for i in range(n):
        si = score[i:i + 1, :]
        beats = (si > score) | ((si == score) & (j > i))
        rank = rank + jnp.where(beats, 1, 0)
    return rank


def _dup_head(q4, r):
    pair = q4[:, (r // 2) * LANES:(r // 2 + 1) * LANES].astype(F32)
    lane = lax.broadcasted_iota(jnp.int32, pair.shape, 1)
    swapped = _swap_halves(pair)
    if r % 2 == 0:
        return jnp.where(lane < HEAD_DIM, pair, swapped)
    return jnp.where(lane < HEAD_DIM, swapped, pair)


def _pack_heads(o_list, g):
    lane = lax.broadcasted_iota(jnp.int32, o_list[0].shape, 1)
    in_g = (lane >= g * HEAD_DIM) & (lane < (g + 1) * HEAD_DIM)
    both = []
    for o in o_list:
        om = jnp.where(in_g, o, 0.0)
        both.append(om + _swap_halves(om))
    pairs = [jnp.where(lane < HEAD_DIM, both[2 * p], both[2 * p + 1]) for p in range(2)]
    return jnp.concatenate(pairs, axis=1)


def _group_gates(misc, g):
    w = 3 * NSA_GQA
    gates = misc[:, FOX_HEADS:FOX_HEADS + w]
    for other in range(1, NSA_GROUPS):
        gates = jnp.where(g == other, misc[:, FOX_HEADS + other * w:FOX_HEADS + (other + 1) * w],
                          gates)
    return gates


def _cmp_sel_kernel(q_ref, kc_ref, vc_ref, bias_ref, gate_ref, ovl_ref, oc_ref, sel_ref):
    g = pl.program_id(1)
    i = pl.program_id(2)
    tq = q_ref.shape[0]
    n_pad = kc_ref.shape[0]
    q4 = q_ref[...]
    lane = lax.broadcasted_iota(jnp.int32, (n_pad, LANES), 1)
    in_g = (lane >= g * HEAD_DIM) & (lane < (g + 1) * HEAD_DIM)
    kc = jnp.where(in_g, kc_ref[...].astype(F32), 0.0).astype(BF16)
    vc = vc_ref[...]
    t = i * tq + lax.broadcasted_iota(jnp.int32, (tq, n_pad), 0)
    n = lax.broadcasted_iota(jnp.int32, (tq, n_pad), 1)
    valid = t >= n * CMP_STRIDE + (CMP_BLOCK - 1)
    gates = _group_gates(gate_ref[...], g)
    p_sum = jnp.zeros((tq, n_pad), F32)
    outs = []
    for r in range(NSA_GQA):
        qr = _dup_head(q4, r).astype(BF16)
        s = _dot_nt(qr, kc)
        s = jnp.where(valid, s + bias_ref[r], NEG_BIG)
        m = jnp.max(s, axis=-1, keepdims=True)
        p = jnp.exp2(s - m)
        p = p / jnp.sum(p, axis=-1, keepdims=True)
        p = jnp.where(valid, p, 0.0)
        p_sum = p_sum + p
        outs.append(_dot(p.astype(BF16), vc) * gates[:, 3 * r:3 * r + 1])
    oc_ref[...] = _pack_heads(outs, g).astype(oc_ref.dtype)

    ovl = ovl_ref[...]
    hi, mid, lo = _split3(p_sum)
    imp = _dot_nt(ovl, hi) + _dot_nt(ovl, mid) + _dot_nt(ovl, lo)
    n_blk = imp.shape[0]
    j = lax.broadcasted_iota(jnp.int32, (n_blk, tq), 0)
    qb = jnp.right_shift(i * tq + lax.broadcasted_iota(jnp.int32, (n_blk, tq), 1),
                         int(math.log2(SEL_BLOCK)))
    forced = (j == 0) | (j == qb) | (j == qb - 1)
    causal = j <= qb
    score = jnp.where(causal, imp + jnp.where(forced, FORCE_SCORE, 0.0), -FORCE_SCORE)
    chosen = (_rank_rows(score) < N_SEL) & causal
    sel = jnp.where(chosen, 1.0, 0.0)
    sel = jnp.concatenate([sel, jnp.zeros((LANES - n_blk, tq), F32)], axis=0)
    sel_ref[...] = sel.T


def _cmp_sel(nq, kcmp, vcmp, bias_c, gates_g, ovl_t):
    B, S, _ = nq.shape
    tq = CMP_TILE
    n_pad = kcmp.shape[1]
    return pl.pallas_call(
        _cmp_sel_kernel,
        grid=(B, NSA_GROUPS, S // tq),
        in_specs=[pl.BlockSpec((None, tq, 2 * LANES), lambda b, g, i: (b, i, g)),
                  pl.BlockSpec((None, n_pad, LANES), lambda b, g, i: (b, 0, 0)),
                  pl.BlockSpec((None, n_pad, LANES), lambda b, g, i: (b, 0, 0)),
                  pl.BlockSpec((NSA_GQA, tq, n_pad), lambda b, g, i: (g, i, 0)),
                  pl.BlockSpec((None, tq, LANES), lambda b, g, i: (b, i, 0)),
                  pl.BlockSpec(ovl_t.shape, lambda b, g, i: (0, 0))],
        out_specs=[pl.BlockSpec((None, tq, 2 * LANES), lambda b, g, i: (b, i, g)),
                   pl.BlockSpec((None, None, tq, LANES), lambda b, g, i: (b, g, i, 0))],
        out_shape=[jax.ShapeDtypeStruct((B, S, NSA_HEADS * HEAD_DIM), BF16),
                   jax.ShapeDtypeStruct((B, NSA_GROUPS, S, LANES), F32)],
        compiler_params=pltpu.CompilerParams(
            dimension_semantics=("parallel", "parallel", "arbitrary"),
            vmem_limit_bytes=VMEM_LIMIT),
        name="cmp_sel",
    )(nq, kcmp, vcmp, bias_c, gates_g, ovl_t)


def _nsa_kernel(q_ref, ks_ref, vs_ref, kw_ref, vw_ref, sel_ref, w4_ref, gate_ref, oc_ref,
                o_ref):
    for tile in range(ks_ref.shape[0] // q_ref.shape[0]):
        @pl.when(pl.program_id(2) == tile)
        def _(tile=tile):
            _nsa_tile(tile, q_ref, ks_ref, vs_ref, kw_ref, vw_ref, sel_ref, w4_ref, gate_ref,
                      oc_ref, o_ref)


def _nsa_tile(i, q_ref, ks_ref, vs_ref, kw_ref, vw_ref, sel_ref, w4_ref, gate_ref, oc_ref, o_ref):
    g = pl.program_id(0)
    tq = q_ref.shape[0]
    tk = w4_ref.shape[-1]
    H = NSA_GQA
    q4 = q_ref[...]
    lane = lax.broadcasted_iota(jnp.int32, (tq, LANES), 1)
    in_g = (lane >= g * HEAD_DIM) & (lane < (g + 1) * HEAD_DIM)
    heads = [_dup_head(q4, r) for r in range(H)]
    qs = jnp.concatenate([jnp.where(in_g, hd, 0.0).astype(BF16) for hd in heads], axis=0)

    blk_mask = _swap_halves((sel_ref[...] - 1.0) * (-NEG_BIG))
    qsel = jnp.concatenate(
        [jnp.where(lane < HEAD_DIM, hd, blk_mask).astype(BF16) for hd in heads], axis=0)

    def sel_step(jj, carry, diagonal):
        k0 = jj * tk
        s = (_dot_nt(qsel, ks_ref[pl.ds(k0, tk), :]).reshape(H, tq, tk)
             + w4_ref[:, min(i - 2 * jj, 3)])
        if diagonal:
            cmr = (lax.broadcasted_iota(jnp.int32, (tq, tk), 1)
                   - lax.broadcasted_iota(jnp.int32, (tq, tk), 0))
            s = jnp.where((cmr <= i * tq - jj * tk)[None], s, NEG_BIG)
        return _flash_update(carry, s, vs_ref[pl.ds(k0, tk), :])

    carry = (jnp.full((H, tq, 1), NEG_BIG, F32), jnp.zeros((H * tq, LANES), F32))
    n_full = (i * tq) // tk
    for jj in range(n_full):
        carry = sel_step(jj, carry, False)
    _, acc_s = sel_step(n_full, carry, True)

    old = [max(i - d, 0) * tq for d in (2, 1)]
    now = i * tq
    k_old = jnp.concatenate([kw_ref[pl.ds(st, tq), :] for st in old], axis=0)
    v_old = jnp.concatenate([vw_ref[pl.ds(st, tq), :] for st in old], axis=0)
    col = lax.broadcasted_iota(jnp.int32, (tq, 2 * tq), 1)
    cmr = col - lax.broadcasted_iota(jnp.int32, (tq, 2 * tq), 0)
    never = 4 * tq
    ok_old = (((col < tq) & (cmr > (0 if i >= 2 else never)))
              | (col >= (tq if i >= 1 else never)))
    s_old = jnp.where(ok_old[None], _dot_nt(qs, k_old).reshape(H, tq, 2 * tq) + w4_ref[:, 2],
                      NEG_BIG)
    s_now = jnp.where((cmr[:, 0:tq] <= 0)[None],
                      _dot_nt(qs, kw_ref[pl.ds(now, tq), :]).reshape(H, tq, tq)
                      + w4_ref[:, 0, :, 0:tq], NEG_BIG)
    m_w = jnp.maximum(jnp.max(s_old, axis=-1, keepdims=True),
                      jnp.max(s_now, axis=-1, keepdims=True))
    acc_w = (_dot(_softmax_weights(s_old, m_w).reshape(H * tq, 2 * tq), v_old)
             + _dot(_softmax_weights(s_now, m_w).reshape(H * tq, tq), vw_ref[pl.ds(now, tq), :]))

    gates = _group_gates(gate_ref[...], g)
    o_s = _normalize(acc_s)
    o_w = _normalize(acc_w)
    outs = []
    for r in range(H):
        sl = slice(r * tq, (r + 1) * tq)
        outs.append(o_s[sl] * gates[:, 3 * r + 1:3 * r + 2] + o_w[sl] * gates[:, 3 * r + 2:3 * r + 3])
    lane = lax.broadcasted_iota(jnp.int32, (tq, LANES), 1)
    pairs = [jnp.where(lane < HEAD_DIM, outs[2 * p], _swap_halves(outs[2 * p + 1]))
             for p in range(H // 2)]
    o_ref[...] = (jnp.concatenate(pairs, axis=1) + oc_ref[...].astype(F32)).astype(o_ref.dtype)


def _nsa(nq, ks, vs, kw, vw, sel, w4, misc, oc):
    B, S, _ = nq.shape
    tq = ATT_TILE
    tk = w4.shape[-1]
    both = lambda: pl.BlockSpec((None, S, LANES), lambda g, b, i: (b, 0, 0))
    mine = lambda: pl.BlockSpec((None, S, LANES), lambda g, b, i: (b, 0, g))
    return pl.pallas_call(
        _nsa_kernel,
        grid=(NSA_GROUPS, B, S // tq),
        in_specs=[pl.BlockSpec((None, tq, 2 * LANES), lambda g, b, i: (b, i, g)),
                  mine(), mine(), both(), mine(),
                  pl.BlockSpec((None, None, tq, LANES), lambda g, b, i: (b, g, i, 0)),
                  pl.BlockSpec((NSA_GQA, 4, tq, tk), lambda g, b, i: (g, 0, 0, 0)),
                  pl.BlockSpec((None, tq, LANES), lambda g, b, i: (b, i, 0)),
                  pl.BlockSpec((None, tq, 2 * LANES), lambda g, b, i: (b, i, g))],
        out_specs=pl.BlockSpec((None, tq, 2 * LANES), lambda g, b, i: (b, i, g)),
        out_shape=jax.ShapeDtypeStruct((B, S, NSA_HEADS * HEAD_DIM), BF16),
        compiler_params=pltpu.CompilerParams(
            dimension_semantics=("parallel", "parallel", "arbitrary"),
            vmem_limit_bytes=VMEM_LIMIT),
        name="nsa",
    )(nq, ks, vs, kw, vw, sel, w4, misc, oc)


def _layer_norm(y, g, b):
    mu = jnp.mean(y, axis=-1, keepdims=True)
    yc = y - mu
    var = jnp.mean(yc * yc, axis=-1, keepdims=True)
    return yc * lax.rsqrt(var + LN_EPS) * g + b


def _top_rows(score, k):
    n = score.shape[0]
    idx = lax.broadcasted_iota(jnp.int32, score.shape, 0).astype(F32)
    alive = jnp.ones(score.shape, F32)
    for _ in range(k):
        live = alive > 0.0
        best = jnp.max(jnp.where(live, score, -jnp.inf), axis=0, keepdims=True)
        first = jnp.min(jnp.where(live & (score == best), idx, float(n)), axis=0, keepdims=True)
        alive = jnp.where(idx == first, 0.0, alive)
    return alive == 0.0


def _router_gates_t(h2, wr_t, eb_col):
    tm = h2.shape[0]
    h_hi, h_lo, _ = _split3(h2)
    w_hi, w_lo, _ = _split3(wr_t)
    logit = _dot_nt(w_hi, h_hi) + _dot_nt(w_hi, h_lo) + _dot_nt(w_lo, h_hi)
    scores = _sigmoid(logit)
    biased = scores + eb_col
    e_in = lax.broadcasted_iota(jnp.int32, (GROUP_SIZE, tm), 0).astype(F32)
    gs_rows = []
    for gi in range(N_EXPERT_GROUPS):
        grp = biased[gi * GROUP_SIZE:(gi + 1) * GROUP_SIZE, :]
        m1 = jnp.max(grp, axis=0, keepdims=True)
        first = jnp.min(jnp.where(grp == m1, e_in, float(GROUP_SIZE)), axis=0, keepdims=True)
        m2 = jnp.max(jnp.where(e_in == first, -jnp.inf, grp), axis=0, keepdims=True)
        gs_rows.append(m1 + m2)
    gscore = jnp.concatenate(gs_rows, axis=0)
    g_keep = _rank_rows(gscore) < TOPK_GROUPS
    keep = jnp.concatenate(
        [jnp.broadcast_to(g_keep[gi:gi + 1, :], (GROUP_SIZE, tm)) for gi in range(N_EXPERT_GROUPS)],
        axis=0)
    masked = jnp.where(keep, biased, -jnp.inf)
    chosen = _top_rows(masked, TOP_K)
    w = jnp.where(chosen, scores, 0.0)
    return w / jnp.sum(w, axis=0, keepdims=True) * ROUTED_SCALE


def _out_proj_kernel(alpha, of_ref, on_ref, x_ref, mod_ref, w_ref, lg_ref, lb_ref, wr_ref,
                     eb_ref, x1_ref, h2_ref, gate_ref, gate_t_ref):
    half = of_ref.shape[1]
    mod = mod_ref[...]
    mixed = _dot(of_ref[...], w_ref[0:half, :]) + _dot(on_ref[...], w_ref[half:2 * half, :])
    y = alpha * x_ref[...] + mod[2:3, :] * mixed
    x1 = _layer_norm(y, lg_ref[...], lb_ref[...])
    x1_ref[...] = x1
    h2 = x1 * (1.0 + mod[4:5, :]) + mod[3:4, :]
    h2_ref[...] = h2.astype(h2_ref.dtype)
    gates_t = _router_gates_t(h2, wr_ref[...], eb_ref[...])
    gate_t_ref[...] = gates_t
    tm = h2.shape[0]
    gates_t = jnp.concatenate([gates_t, jnp.zeros((LANES - N_EXPERTS, tm), F32)], axis=0)
    gate_ref[...] = gates_t.T


def _out_proj(alpha, o_fox, o_nsa, x, mod, w_out, ln_g, ln_b, wr_t, eb_col):
    B, S, D = x.shape
    tm = ROW_TILE
    half = o_fox.shape[-1]
    row = lambda a: pl.BlockSpec(a.shape, lambda b, s: (0, 0))
    return pl.pallas_call(
        functools.partial(_out_proj_kernel, alpha),
        grid=(B, S // tm),
        in_specs=[pl.BlockSpec((None, tm, half), lambda b, s: (b, s, 0)),
                  pl.BlockSpec((None, tm, half), lambda b, s: (b, s, 0)),
                  pl.BlockSpec((None, tm, D), lambda b, s: (b, s, 0)),
                  pl.BlockSpec((None, 6, D), lambda b, s: (b, 0, 0)),
                  row(w_out), row(ln_g), row(ln_b), row(wr_t), row(eb_col)],
        out_specs=[pl.BlockSpec((None, tm, D), lambda b, s: (b, s, 0)),
                   pl.BlockSpec((None, tm, D), lambda b, s: (b, s, 0)),
                   pl.BlockSpec((None, tm, LANES), lambda b, s: (b, s, 0)),
                   pl.BlockSpec((N_EXPERTS, tm), lambda b, s: (0, b * (S // tm) + s))],
        out_shape=[jax.ShapeDtypeStruct((B, S, D), F32),
                   jax.ShapeDtypeStruct((B, S, D), BF16),
                   jax.ShapeDtypeStruct((B, S, LANES), F32),
                   jax.ShapeDtypeStruct((N_EXPERTS, B * S), F32)],
        compiler_params=pltpu.CompilerParams(dimension_semantics=("parallel", "parallel"),
                                             vmem_limit_bytes=VMEM_LIMIT),
        name="out_proj",
    )(o_fox, o_nsa, x, mod, w_out, ln_g, ln_b, wr_t, eb_col)


SORT_TILE = 256
ROW_ALIGN = 16
EXP_TILE = 1024
EXP_STRIP = 256
P_CHUNK = 256


def _strict_upper(n):
    return jnp.where(lax.broadcasted_iota(jnp.int32, (n, n), 0)
                     < lax.broadcasted_iota(jnp.int32, (n, n), 1), 1.0, 0.0).astype(BF16)


def _strict_lower(n):
    return jnp.where(lax.broadcasted_iota(jnp.int32, (n, n), 1)
                     < lax.broadcasted_iota(jnp.int32, (n, n), 0), 1.0, 0.0).astype(BF16)


def _local_rows_bound(ts):
    rows = TOP_K * ts + N_EXPERTS * (ROW_ALIGN - 1)
    return -(-rows // P_CHUNK) * P_CHUNK


def _piece_cols(ts):
    return -(-(_local_rows_bound(ts) // ROW_ALIGN) // LANES) * LANES


def _sorted_tiles_bound(T):
    rows = TOP_K * T + (T // SORT_TILE) * N_EXPERTS * (ROW_ALIGN - 1)
    return -(-rows // EXP_TILE) + N_EXPERTS


def _sorted_tiles_expected(T):
    groups = (T // SORT_TILE) * N_EXPERTS
    rows = TOP_K * T + groups * ((ROW_ALIGN - 1) / 2 + 2)
    return int(-(-rows // EXP_TILE) + math.ceil(0.65 * N_EXPERTS))


def _moe_meta_kernel(gt_ref, ptab_ref, loc_et_ref, np_et_ref, loc_te_ref, np_te_ref, tot_ref,
                     erow_ref, texp_ref, nused_ref):
    E, T = gt_ref.shape
    mask = jnp.where(gt_ref[...] > 0.0, 1.0, 0.0).astype(BF16)
    t_id = lax.shift_right_logical(lax.broadcasted_iota(jnp.int32, (T, LANES), 0),
                                   int(math.log2(SORT_TILE)))
    tind = jnp.where(t_id == lax.broadcasted_iota(jnp.int32, (T, LANES), 1), 1.0, 0.0)
    cnt = _dot(mask, tind.astype(BF16))
    n16 = jnp.floor((cnt + (ROW_ALIGN - 1.0)) * (1.0 / ROW_ALIGN))
    n16b = n16.astype(BF16)
    q = EXP_TILE // ROW_ALIGN
    len16 = jnp.sum(n16, axis=1, keepdims=True)
    pad16 = jnp.floor((len16 + (q - 1.0)) * (1.0 / q)) * q
    sl = _strict_lower(E)
    hi, mid, lo = _split3(jnp.broadcast_to(pad16, (E, LANES)))
    start16 = _dot(sl, hi) + _dot(sl, mid) + _dot(sl, lo)
    gdst16 = start16 + _dot(n16b, _strict_upper(LANES))
    loc16 = _dot(sl, n16b)

    def t(a):
        return jnp.concatenate([a, jnp.zeros((LANES - E, LANES), F32)], axis=0).T

    scale = float(ROW_ALIGN)
    loc_et_ref[...] = loc16 * scale
    np_et_ref[...] = n16 * scale
    loc_te_ref[...] = t(loc16) * scale
    np_te_ref[...] = t(n16) * scale
    tot_ref[...] = (jnp.sum(n16, axis=0, keepdims=True) * scale).astype(jnp.int32)

    n_t, n_blk = ptab_ref.shape
    blk = lax.broadcasted_iota(jnp.int32, (E, n_blk), 1).astype(F32)
    for tile in range(n_t):
        lo_c = loc16[:, tile:tile + 1]
        inside = (lo_c <= blk) & (blk < lo_c + n16[:, tile:tile + 1])
        dst = jnp.sum(jnp.where(inside, gdst16[:, tile:tile + 1] + (blk - lo_c), 0.0),
                      axis=0, keepdims=True)
        ptab_ref[tile:tile + 1, :] = (dst * scale).astype(jnp.int32)
    ends = jnp.concatenate([t(start16 + len16)[0:1, :], t(start16 + pad16)[0:1, :],
                            jnp.zeros((erow_ref.shape[0] - 2, LANES), F32)], axis=0)
    erow_ref[...] = (ends * scale).astype(jnp.int32)
    n_tab = texp_ref.shape[1]
    tile_row16 = (lax.broadcasted_iota(jnp.int32, (E, n_tab), 1) * q).astype(F32)
    owner = jnp.sum(jnp.where(start16[:, 0:1] <= tile_row16, 1.0, 0.0), axis=0, keepdims=True)
    texp_ref[...] = (owner - 1.0).astype(jnp.int32)
    n_used = jnp.sum(pad16, axis=0, keepdims=True) * (1.0 / q)
    nused_ref[...] = jnp.broadcast_to(n_used, (1, LANES)).astype(jnp.int32)


def _moe_meta(gates_t, n_tab):
    E, T = gates_t.shape
    i32 = jnp.int32
    return pl.pallas_call(
        _moe_meta_kernel,
        out_shape=[jax.ShapeDtypeStruct((T // SORT_TILE, _piece_cols(SORT_TILE)), i32),
                   jax.ShapeDtypeStruct((E, LANES), F32),
                   jax.ShapeDtypeStruct((E, LANES), F32),
                   jax.ShapeDtypeStruct((LANES, LANES), F32),
                   jax.ShapeDtypeStruct((LANES, LANES), F32),
                   jax.ShapeDtypeStruct((1, LANES), i32),
                   jax.ShapeDtypeStruct((8, LANES), i32),
                   jax.ShapeDtypeStruct((1, n_tab), i32),
                   jax.ShapeDtypeStruct((1, LANES), i32)],
        compiler_params=pltpu.CompilerParams(vmem_limit_bytes=VMEM_LIMIT),
        name="moe_meta",
    )(gates_t)


def _start_pieces(tile, ptab_s, tot_s, n_cols, make_copy):
    n_pieces = lax.shift_right_logical(tot_s[tile], int(math.log2(ROW_ALIGN)))

    def body(b, carry):
        make_copy(pl.multiple_of(b * ROW_ALIGN, ROW_ALIGN),
                  pl.multiple_of(ptab_s[tile * n_cols + b], ROW_ALIGN)).start()
        return carry

    lax.fori_loop(0, n_pieces, body, 0)


def _moe_sort_kernel(ptab_s, tot_s, lend_s, rend_s, gt_ref, gtok_ref, h_ref, locrow_ref,
                     nprow_ref, xs_hbm, buf, zbuf, sem, zsem):
    tau = pl.program_id(0)
    n_t = pl.num_programs(0)
    slot = lax.rem(tau, 2)
    E, ts = gt_ref.shape
    D = h_ref.shape[1]
    n_cols = _piece_cols(ts)

    def copies(tile, sl, wait):
        if wait:
            rows = pl.multiple_of(tot_s[tile], ROW_ALIGN)

            @pl.when(rows > 0)
            def _():
                pltpu.make_async_copy(buf.at[sl, pl.ds(0, rows)], xs_hbm.at[pl.ds(0, rows)],
                                      sem.at[sl]).wait()
        else:
            def make_copy(loc, dst):
                return pltpu.make_async_copy(buf.at[sl, pl.ds(loc, ROW_ALIGN)],
                                             xs_hbm.at[pl.ds(dst, ROW_ALIGN)], sem.at[sl])
            _start_pieces(tile, ptab_s, tot_s, n_cols, make_copy)

    z_rows = zbuf.shape[0]
    used_rows = rend_s[E - 1]
    n_spare = (xs_hbm.shape[0] - used_rows) // z_rows

    def spare_fill(wait):
        def body(c, carry):
            dst = pl.multiple_of(used_rows + c * z_rows, z_rows)
            cp = pltpu.make_async_copy(zbuf, xs_hbm.at[pl.ds(dst, z_rows)], zsem.at[1])
            if wait:
                cp.wait()
            else:
                cp.start()
            return carry

        lax.fori_loop(0, n_spare, body, 0)

    @pl.when(tau == 0)
    def _():
        zbuf[...] = jnp.zeros_like(zbuf)
        spare_fill(False)

    @pl.when(tau >= 2)
    def _():
        copies(tau - 2, slot, True)

    g = gt_ref[...]
    mask = g > 0.0
    maskb = jnp.where(mask, 1.0, 0.0).astype(BF16)
    pad = jnp.zeros((LANES - E, ts), F32)
    pos = jnp.where(mask, _dot(maskb, _strict_upper(ts)), -1.0)
    pos = jnp.concatenate([pos, pad], axis=0).astype(BF16)
    lo_row = locrow_ref[...]
    hi_row = lo_row + nprow_ref[...]
    h = jnp.concatenate([h_ref[...]] + list(_split3(gtok_ref[...])), axis=1)
    lane = lax.broadcasted_iota(jnp.int32, (P_CHUNK, LANES), 1)

    def chunk(c):
        r0 = c * P_CHUNK
        r = (r0 + lax.broadcasted_iota(jnp.int32, (P_CHUNK, LANES), 0)).astype(F32)
        inside = (lo_row <= r) & (r < hi_row)
        group = jnp.where(inside, 1.0, 0.0).astype(BF16)
        want = r[:, 0:1] - jnp.sum(jnp.where(inside, lo_row, 0.0), axis=1, keepdims=True)
        hit = _dot(group, pos) == want
        rows = _dot(jnp.where(hit, 1.0, 0.0).astype(BF16), h)
        buf[slot, pl.ds(r0, P_CHUNK), 0:D] = rows[:, 0:D].astype(buf.dtype)
        extra = jnp.zeros((P_CHUNK, LANES), F32)
        for k in range(3):
            mine = jnp.where(inside, rows[:, D + k * LANES:D + (k + 1) * LANES], 0.0)
            extra = jnp.where(lane == k, jnp.sum(mine, axis=1, keepdims=True), extra)
        buf[slot, pl.ds(r0, P_CHUNK), D:D + LANES] = extra.astype(buf.dtype)

    for c in range(buf.shape[1] // P_CHUNK):
        chunk(c)
    copies(tau, slot, False)

    @pl.when(tau == n_t - 1)
    def _():
        @pl.when(n_t >= 2)
        def _():
            copies(tau - 1, 1 - slot, True)
        copies(tau, slot, True)
        spare_fill(True)

        sizes =[zbuf.shape[0] >> s for s in range(int(math.log2(zbuf.shape[0] // ROW_ALIGN)) + 1)]

        def fill(wait):
            def e_body(e, carry):
                start = lend_s[e]
                n = rend_s[e] - start
                off = start
                for size in sizes:
                    bit = jnp.bitwise_and(n, size)

                    @pl.when(bit != 0)
                    def _(off=off, size=size):
                        cp = pltpu.make_async_copy(
                            zbuf.at[pl.ds(0, size)],
                            xs_hbm.at[pl.ds(pl.multiple_of(off, ROW_ALIGN), size)], zsem.at[0])
                        if wait:
                            cp.wait()
                        else:
                            cp.start()

                    off = off + bit
                return carry

            lax.fori_loop(0, E, e_body, 0)

        fill(False)
        fill(True)


def _moe_sort(ptab, tot, lend, rend, gates_t, gates, h2, loc_te, np_te, n_rows):
    E, T = gates_t.shape
    D = h2.shape[1]
    ts = SORT_TILE
    grid_spec = pltpu.PrefetchScalarGridSpec(
        num_scalar_prefetch=4,
        grid=(T // ts,),
        in_specs=[pl.BlockSpec((E, ts), lambda t, *_: (0, t)),
                  pl.BlockSpec((ts, LANES), lambda t, *_: (t, 0)),
                  pl.BlockSpec((ts, D), lambda t, *_: (t, 0)),
                  pl.BlockSpec((None, 1, LANES), lambda t, *_: (t, 0, 0)),
                  pl.BlockSpec((None, 1, LANES), lambda t, *_: (t, 0, 0))],
        out_specs=pl.BlockSpec(memory_space=pl.ANY),
        scratch_shapes=[pltpu.VMEM((2, _local_rows_bound(ts), D + LANES), BF16),
                        pltpu.VMEM((EXP_TILE // 2, D + LANES), BF16),
                        pltpu.SemaphoreType.DMA((2,)),
                        pltpu.SemaphoreType.DMA((2,))])
    return pl.pallas_call(
        _moe_sort_kernel,
        grid_spec=grid_spec,
        out_shape=jax.ShapeDtypeStruct((n_rows, D + LANES), BF16),
        compiler_params=pltpu.CompilerParams(dimension_semantics=("arbitrary",),
                                             vmem_limit_bytes=VMEM_LIMIT),
        name="moe_sort",
    )(ptab, tot, lend, rend, gates_t, gates, h2, loc_te, np_te)


def _moe_expert_kernel(texp_s, nused_s, x_ref, wg_ref, wu_ref, wd_ref, y_ref, wgu_s, wd_s):
    i = pl.program_id(0)
    f = wd_ref.shape[0]

    @pl.when(i < nused_s[0])
    def _():
        @pl.when((i == 0) | (texp_s[i] != texp_s[jnp.maximum(i - 1, 0)]))
        def _():
            wgu_s[:, 0:f] = wg_ref[...].astype(BF16)
            wgu_s[:, f:2 * f] = wu_ref[...].astype(BF16)
            wd_s[...] = wd_ref[...].astype(BF16)

        d = wd_ref.shape[1]
        for r0 in range(0, x_ref.shape[0], EXP_STRIP):
            rows = slice(r0, r0 + EXP_STRIP)
            gate = jnp.sum(x_ref[rows, d:].astype(F32), axis=1, keepdims=True)
            a = _dot(x_ref[rows, 0:d], wgu_s[...])
            act = _silu(a[:, :f]) * a[:, f:] * gate
            y_ref[rows, :] = _dot(act.astype(BF16), wd_s[...]).astype(y_ref.dtype)

    @pl.when(i >= nused_s[0])
    def _():
        y_ref[...] = jnp.zeros_like(y_ref)


def _moe_expert(texp, nused, xs, w_gate, w_up, w_down, n_tiles):
    n_rows, xw = xs.shape
    D, f = w_gate.shape[-2:]
    tm = EXP_TILE

    def tile(i, texp, nused):
        return jnp.maximum(jnp.minimum(i, nused[0] - 1), 0)

    grid_spec = pltpu.PrefetchScalarGridSpec(
        num_scalar_prefetch=2,
        grid=(n_tiles,),
        in_specs=[pl.BlockSpec((tm, xw), lambda i, te, nu: (tile(i, te, nu), 0)),
                  pl.BlockSpec((None, D, f), lambda i, te, nu: (te[tile(i, te, nu)], 0, 0)),
                  pl.BlockSpec((None, D, f), lambda i, te, nu: (te[tile(i, te, nu)], 0, 0)),
                  pl.BlockSpec((None, f, D), lambda i, te, nu: (te[tile(i, te, nu)], 0, 0))],
        out_specs=pl.BlockSpec((tm, D), lambda i, te, nu: (i, 0)),
        scratch_shapes=[pltpu.VMEM((D, 2 * f), BF16), pltpu.VMEM((f, D), BF16)])
    return pl.pallas_call(
        _moe_expert_kernel,
        grid_spec=grid_spec,
        out_shape=jax.ShapeDtypeStruct((n_rows, D), BF16),
        compiler_params=pltpu.CompilerParams(dimension_semantics=("arbitrary",),
                                             vmem_limit_bytes=VMEM_LIMIT),
        name="moe_expert",
    )(texp, nused, xs, w_gate, w_up, w_down)


def _moe_combine_kernel(alpha, ptab_s, tot_s, g_ref, loc_ref, np_ref, h_ref, x1_ref,
                        mod_ref, sgu_ref, sd_ref, lg_ref, lb_ref, y_hbm, o_ref, ybuf, acc_ref,
                        sem):
    tau = pl.program_id(0)
    n_t = pl.num_programs(0)
    slot = lax.rem(tau, 2)
    ts, n_lane = g_ref.shape
    E = loc_ref.shape[0]
    n_cols = _piece_cols(ts)

    def copies(tile, sl, wait):
        if wait:
            rows = pl.multiple_of(tot_s[tile], ROW_ALIGN)

            @pl.when(rows > 0)
            def _():
                pltpu.make_async_copy(y_hbm.at[pl.ds(0, rows)], ybuf.at[sl, pl.ds(0, rows)],
                                      sem.at[sl]).wait()
        else:
            def make_copy(loc, dst):
                return pltpu.make_async_copy(y_hbm.at[pl.ds(dst, ROW_ALIGN)],
                                             ybuf.at[sl, pl.ds(loc, ROW_ALIGN)], sem.at[sl])
            _start_pieces(tile, ptab_s, tot_s, n_cols, make_copy)

    @pl.when(tau == 0)
    def _():
        ybuf[...] = jnp.zeros_like(ybuf)
        copies(0, 0, False)

    @pl.when(tau + 1 < n_t)
    def _():
        copies(tau + 1, 1 - slot, False)

    g = g_ref[...]
    mask = g > 0.0
    maskb = jnp.where(mask, 1.0, 0.0).astype(BF16)
    pos = jnp.where(mask, _dot(_strict_lower(ts), maskb), -1.0).astype(BF16)
    lane = lax.broadcasted_iota(jnp.int32, loc_ref.shape, 1)
    lo_col = jnp.sum(jnp.where(lane == tau, loc_ref[...], 0.0), axis=1, keepdims=True)
    hi_col = lo_col + jnp.sum(jnp.where(lane == tau, np_ref[...], 0.0), axis=1, keepdims=True)

    f = sd_ref.shape[0]
    a = _dot(h_ref[...], sgu_ref[...])
    acc_ref[...] = _dot((_silu(a[:, :f]) * a[:, f:]).astype(BF16), sd_ref[...])

    copies(tau, slot, True)

    def chunk(c):
        r0 = pl.multiple_of(c * P_CHUNK, P_CHUNK)
        r = (r0 + lax.broadcasted_iota(jnp.int32, (E, P_CHUNK), 1)).astype(F32)
        inside = (lo_col <= r) & (r < hi_col)
        group = jnp.concatenate([jnp.where(inside, 1.0, 0.0),
                                 jnp.zeros((n_lane - E, P_CHUNK), F32)], axis=0).astype(BF16)
        want = r[0:1, :] - jnp.sum(jnp.where(inside, lo_col, 0.0), axis=0, keepdims=True)
        hit = _dot(pos, group) == want
        return _dot(jnp.where(hit, 1.0, 0.0).astype(BF16), ybuf[slot, pl.ds(r0, P_CHUNK), :])

    def pair_body(c, carry):
        acc_ref[...] += chunk(2 * c) + chunk(2 * c + 1)
        return carry

    n_chunks = lax.shift_right_logical(tot_s[tau] + (P_CHUNK - 1), int(math.log2(P_CHUNK)))
    lax.fori_loop(0, lax.shift_right_logical(n_chunks + 1, 1), pair_body, 0)
    y = alpha * x1_ref[...] + mod_ref[5:6, :] * acc_ref[...]
    o_ref[...] = _layer_norm(y, lg_ref[...], lb_ref[...])


def _moe_combine(alpha, ptab, tot, gates, loc_et, np_et, h2, x1, mod, sgu, sd, ln_g, ln_b, ys,
                 S):
    T, D = h2.shape
    ts = SORT_TILE
    per_b = S // ts
    row = lambda a: pl.BlockSpec(a.shape, lambda t, *_: (0, 0))
    grid_spec = pltpu.PrefetchScalarGridSpec(
        num_scalar_prefetch=2,
        grid=(T // ts,),
        in_specs=[pl.BlockSpec((ts, LANES), lambda t, *_: (t, 0)),
                  row(loc_et), row(np_et),
                  pl.BlockSpec((ts, D), lambda t, *_: (t, 0)),
                  pl.BlockSpec((ts, D), lambda t, *_: (t, 0)),
                  pl.BlockSpec((None, 6, D), lambda t, *_: (t // per_b, 0, 0)),
                  row(sgu), row(sd), row(ln_g), row(ln_b),
                  pl.BlockSpec(memory_space=pl.ANY)],
        out_specs=pl.BlockSpec((ts, D), lambda t, *_: (t, 0)),
        scratch_shapes=[pltpu.VMEM((2, _local_rows_bound(ts), D), BF16),
                        pltpu.VMEM((ts, D), F32),
                        pltpu.SemaphoreType.DMA((2,))])
    return pl.pallas_call(
        functools.partial(_moe_combine_kernel, alpha),
        grid_spec=grid_spec,
        out_shape=jax.ShapeDtypeStruct((T, D), F32),
        compiler_params=pltpu.CompilerParams(dimension_semantics=("arbitrary",),
                                             vmem_limit_bytes=VMEM_LIMIT),
        name="moe_combine",
    )(ptab, tot, gates, loc_et, np_et, h2, x1, mod, sgu, sd, ln_g, ln_b, ys)


def _moe(alpha, h2, x1, gates, gates_t, mod, w_gate, w_up, w_down, sgu, sd, ln_g, ln_b):
    B, S, D = x1.shape
    T = B * S
    n_t = T // SORT_TILE
    n_bound = _sorted_tiles_bound(T)
    n_tab = -(-n_bound // LANES) * LANES
    ptab, loc_et, np_et, loc_te, np_te, tot, erow, texp, nused = _moe_meta(gates_t, n_tab)
    ptab = ptab.reshape(-1)
    tot = tot[0, :n_t]
    h2 = h2.reshape(T, D)
    x1 = x1.reshape(T, D)
    gates = gates.reshape(T, LANES)
    loc_rows = loc_te[:n_t].reshape(n_t, 1, LANES)
    np_rows = np_te[:n_t].reshape(n_t, 1, LANES)

    def sized(n_tiles):
        def run():
            xs = _moe_sort(ptab, tot, erow[0, :N_EXPERTS], erow[1, :N_EXPERTS], gates_t, gates,
                           h2, loc_rows, np_rows, n_tiles * EXP_TILE)
            ys = _moe_expert(texp[0], nused[0, :1], xs, w_gate, w_up, w_down, n_tiles)
            return _moe_combine(alpha, ptab, tot, gates, loc_et, np_et, h2, x1, mod, sgu, sd,
                                ln_g, ln_b, ys, S)
        return run

    n_small = min(_sorted_tiles_expected(T), n_bound)
    out = lax.cond(nused[0, 0] <= n_small, sized(n_small), sized(n_bound))
    return out.reshape(B, S, D)


def _rearrange_w_in(w):
    d_in = w.shape[0]
    scale = HEAD_DIM ** -0.5 * LOG2E
    fq, fk, fv = w[:, 0:512], w[:, 512:1024], w[:, 1024:1536]
    ff = w[:, 1536:1544]
    nq = w[:, 1544:2056]
    kc, vc, ks, vs, kw, vw = (w[:, 2056 + k * LANES:2056 + (k + 1) * LANES] for k in range(6))
    ng = w[:, 2824:2848]
    pad = jnp.zeros((d_in, LANES - ff.shape[1] - ng.shape[1]), w.dtype)
    cols = [fq * scale, fk, nq * scale, kc, kw, vc, ks, fv, vs, vw, ff, ng, pad]
    return jnp.concatenate(cols, axis=1).astype(BF16)


def _compress_weights(pos, w1, w2):
    half = CMP_BLOCK // 2
    w1r = w1.reshape(2, half, HEAD_DIM, CMP_HIDDEN)
    zeros = jnp.zeros_like(w1r[0])
    def spread(part):
        g0 = jnp.stack([part, zeros], axis=1).reshape(half * 2 * HEAD_DIM, CMP_HIDDEN)
        g1 = jnp.stack([zeros, part], axis=1).reshape(half * 2 * HEAD_DIM, CMP_HIDDEN)
        return jnp.concatenate([g0, g1], axis=1).astype(BF16)
    wa, wb = spread(w1r[0]), spread(w1r[1])
    z2 = jnp.zeros_like(w2)
    w2bd = jnp.concatenate([jnp.concatenate([w2, z2], axis=1),
                            jnp.concatenate([z2, w2], axis=1)], axis=0).astype(BF16)
    posr = pos.reshape(2, half, 1, HEAD_DIM)
    posr = jnp.broadcast_to(posr, (2, half, NSA_GROUPS, HEAD_DIM)).reshape(2, half * 2 * HEAD_DIM)
    return posr, wa, wb, w2bd


@functools.lru_cache(maxsize=None)
def _static_tables(S):
    tq = ATT_TILE
    n_cmp = (S - CMP_BLOCK) // CMP_STRIDE + 1
    n_pad = S // CMP_STRIDE
    n_slc = S // SEL_BLOCK
    t = np.arange(S)[:, None]
    n = np.arange(n_pad)[None, :]
    bucket_c = _t5_bucket_np(t - (n * CMP_STRIDE + CMP_BLOCK - 1)).reshape(1, -1)
    d = (np.arange(4)[:, None, None] * tq + np.arange(tq)[None, :, None]
         - np.arange(2 * tq)[None, None, :])
    bucket_w = _t5_bucket_np(d).reshape(1, -1)
    cs = np.arange(n_pad)[None, :] * CMP_STRIDE
    sj = np.arange(n_slc)[:, None] * SEL_BLOCK
    ovl_t = ((cs < sj + SEL_BLOCK) & (cs + CMP_BLOCK > sj) & (np.arange(n_pad)[None, :] < n_cmp))
    return bucket_c, bucket_w, ovl_t.astype(np.float32)


def kernel(x, c, w_ada, b_ada, w_in, b_f, cmp_pos_k, cmp_w1_k, cmp_w2_k, cmp_pos_v, cmp_w1_v,
           cmp_w2_v, rel_bias, w_out, ln1_g, ln1_b, w_router, e_bias, w_gate, w_up, w_down,
           ws_gate, ws_up, ws_down, ln2_g, ln2_b):
    B, S, D = x.shape
    depth = w_ada.shape[0]
    alpha = (2 * depth) ** 0.25
    tq = ATT_TILE
    assert w_in.shape[-1] == 3 * FOX_HEADS * HEAD_DIM + FOX_HEADS + NSA_HEADS * HEAD_DIM \
        + 6 * NSA_GROUPS * HEAD_DIM + 3 * NSA_HEADS
    assert NSA_GROUPS * HEAD_DIM == LANES and S // CMP_STRIDE == LANES
    assert WINDOW == 2 * tq and S // SEL_BLOCK <= HEAD_DIM // 2
    assert S % FOX_TILE == 0 and S % CMP_TILE == 0 and (B * S) % SORT_TILE == 0
    assert w_gate.shape[1:] == (N_EXPERTS, D, w_down.shape[2])
    bucket_c, bucket_w, ovl_t = _static_tables(S)
    rel_bias_t = rel_bias.T * LOG2E
    bias_c = _bias_table(jnp.asarray(bucket_c), rel_bias_t).reshape(NSA_HEADS, S, S // CMP_STRIDE)
    w4 = _bias_table(jnp.asarray(bucket_w), rel_bias_t).reshape(NSA_HEADS, 4, tq, 2 * tq)
    ovl_t = jnp.asarray(ovl_t, BF16)

    for l in range(depth):
        mod = _ada(c, w_ada[l], b_ada[l]).reshape(B, 6, D)
        bf_row = jnp.zeros((1, LANES), F32).at[0, :FOX_HEADS].set(b_f[l])
        (fq, fk, nq, kc, kw, vc, ks, fv, vs, vw, misc, misc_t) = _in_proj(
            x, mod, _rearrange_w_in(w_in[l]), bf_row)

        o_fox = _fox(fq, fk, fv, misc_t[:, :FOX_HEADS, :])

        pk, wak, wbk, w2k = _compress_weights(cmp_pos_k[l], cmp_w1_k[l], cmp_w2_k[l])
        pv, wav, wbv, w2v = _compress_weights(cmp_pos_v[l], cmp_w1_v[l], cmp_w2_v[l])
        rows = S // CMP_STRIDE
        kcmp, vcmp = _compress(kc.reshape(B, rows, CMP_STRIDE * LANES),
                               vc.reshape(B, rows, CMP_STRIDE * LANES),
                               pk, pv, wak, wbk, wav, wbv, w2k, w2v)

        oc, sel = _cmp_sel(nq, kcmp, vcmp, bias_c, misc, ovl_t)
        o_nsa = _nsa(nq, ks, vs, kw, vw, sel, w4, misc, oc)

        x1, h2, gates, gates_t = _out_proj(
            alpha, o_fox, o_nsa, x, mod, w_out[l].astype(BF16), ln1_g[l].reshape(1, D),
            ln1_b[l].reshape(1, D), w_router[l].T, e_bias[l].reshape(N_EXPERTS, 1))

        sgu = jnp.concatenate([ws_gate[l], ws_up[l]], axis=-1).astype(BF16)
        x = _moe(alpha, h2, x1, gates, gates_t, mod, w_gate[l], w_up[l], w_down[l], sgu,
                 ws_down[l].astype(BF16), ln2_g[l].reshape(1, D), ln2_b[l].reshape(1, D))
    return x
```

```python
import functools
import math

import jax
import jax.numpy as jnp
import numpy as np
from jax import lax
from jax.experimental import pallas as pl
from jax.experimental.pallas import tpu as pltpu

F32 = jnp.float32
BF16 = jnp.bfloat16

HEAD_DIM = 64
FOX_HEADS = 8
NSA_HEADS = 8
NSA_GQA = 4
NSA_GROUPS = NSA_HEADS // NSA_GQA
CMP_BLOCK = 32
CMP_STRIDE = 16
CMP_HIDDEN = 256
SEL_BLOCK = 64
N_SEL = 16
WINDOW = 512
N_BUCKETS = 32
MAX_DISTANCE = 128
N_EXPERTS = 64
N_EXPERT_GROUPS = 8
GROUP_SIZE = N_EXPERTS // N_EXPERT_GROUPS
TOPK_GROUPS = 4
TOP_K = 8
ROUTED_SCALE = 2.5
LN_EPS = 1e-5
NEG_BIG = -1e30
FORCE_SCORE = 1e4

LANES = 128
ATT_TILE = 256
FOX_TILE = 512
CMP_TILE = 512
ROW_TILE = 512
VMEM_LIMIT = 48 * 1024 * 1024

NT_DIMS = (((1,), (1,)), ((), ()))


def _dot(a, b):
    return jnp.dot(a, b, preferred_element_type=F32)


def _dot_nt(a, b):
    return lax.dot_general(a, b, NT_DIMS, preferred_element_type=F32)


def _split3(x):
    hi = x.astype(BF16)
    r1 = x - hi.astype(F32)
    mid = r1.astype(BF16)
    lo = (r1 - mid.astype(F32)).astype(BF16)
    return hi, mid, lo


def _silu(x):
    return x / (1.0 + jnp.exp(-x))


def _sigmoid(x):
    return 1.0 / (1.0 + jnp.exp(-x))


def _swap_halves(x):
    return pltpu.roll(x, HEAD_DIM, 1)


def _t5_bucket_np(dist):
    n = np.maximum(dist, 0)
    max_exact = N_BUCKETS // 2
    nf = np.maximum(n, 1).astype(np.float32)
    large = max_exact + (np.log(nf / max_exact) / math.log(MAX_DISTANCE / max_exact)
                         * (N_BUCKETS - max_exact)).astype(np.int32)
    large = np.minimum(large, N_BUCKETS - 1)
    return np.where(n < max_exact, n, large).astype(np.int32)


def _ada_kernel(c_ref, w_ref, b_ref, o_ref):
    c = c_ref[...]
    o_ref[...] = jnp.dot(_silu(c), w_ref[...], preferred_element_type=F32,
                         precision=lax.Precision.HIGHEST) + b_ref[...]


def _ada(c, w_ada, b_ada):
    B, D = c.shape
    n_out = w_ada.shape[1]
    tn = 1024
    return pl.pallas_call(
        _ada_kernel,
        grid=(n_out // tn,),
        in_specs=[pl.BlockSpec((B, D), lambda j: (0, 0)),
                  pl.BlockSpec((D, tn), lambda j: (0, j)),
                  pl.BlockSpec((1, tn), lambda j: (0, j))],
        out_specs=pl.BlockSpec((B, tn), lambda j: (0, j)),
        out_shape=jax.ShapeDtypeStruct((B, n_out), F32),
        compiler_params=pltpu.CompilerParams(dimension_semantics=("arbitrary",),
                                             vmem_limit_bytes=VMEM_LIMIT),
        name="ada",
    )(c, w_ada, b_ada.reshape(1, n_out))


def _bias_table_kernel(bkt_ref, rbt_ref, o_ref):
    bkt = bkt_ref[...]
    k = lax.broadcasted_iota(jnp.int32, (N_BUCKETS, bkt.shape[1]), 0)
    onehot = jnp.where(k == bkt, 1.0, 0.0).astype(BF16)
    hi, mid, lo = _split3(rbt_ref[...])
    o_ref[...] = _dot(hi, onehot) + _dot(mid, onehot) + _dot(lo, onehot)


def _bias_table(bucket, rel_bias_t):
    n = bucket.shape[1]
    chunk = 32768
    n_heads = rel_bias_t.shape[0]
    return pl.pallas_call(
        _bias_table_kernel,
        grid=(n // chunk,),
        in_specs=[pl.BlockSpec((1, chunk), lambda j: (0, j)),
                  pl.BlockSpec(rel_bias_t.shape, lambda j: (0, 0))],
        out_specs=pl.BlockSpec((n_heads, chunk), lambda j: (0, j)),
        out_shape=jax.ShapeDtypeStruct((n_heads, n), F32),
        compiler_params=pltpu.CompilerParams(dimension_semantics=("parallel",),
                                             vmem_limit_bytes=VMEM_LIMIT),
        name="bias_table",
    )(bucket, rel_bias_t)


_C_FQ, _C_FK, _C_NQ = 0, 512, 1024
_C_K3 = 1536
_C_SK = 1920
_C_FV = 2048
_C_SV = 2560
_C_WV = 2688
_C_MISC = 2816
_IN_COLS = 2944
LOG2E = math.log2(math.e)


def _in_proj_kernel(x_ref, mod_ref, w_ref, bf_ref, fq_ref, fk_ref, nq_ref, kc_ref, kw_ref,
                    vc_ref, ks_ref, fv_ref, vs_ref, vw_ref, misc_ref, misct_ref, carry_ref):
    s_idx = pl.program_id(1)
    tm = x_ref.shape[0]
    mod = mod_ref[...]
    h = (x_ref[...] * (1.0 + mod[1:2, :]) + mod[0:1, :]).astype(BF16)

    for ref, c0 in ((fq_ref, _C_FQ), (fk_ref, _C_FK)):
        ref[...] = _dot(h, w_ref[:, c0:c0 + 512]).astype(ref.dtype)
    nsa_q = _dot(h, w_ref[:, _C_NQ:_C_K3])
    keys = _dot(h, w_ref[:, _C_K3:_C_FV])
    fox_v = _dot(h, w_ref[:, _C_FV:_C_SV])
    tail = _dot(h, w_ref[:, _C_SV:_IN_COLS])
    for k, ref in enumerate((kc_ref, kw_ref, vc_ref)):
        ref[...] = keys[:, k * LANES:(k + 1) * LANES].astype(ref.dtype)

    lane = lax.broadcasted_iota(jnp.int32, (tm, LANES), 1)
    low = lane < HEAD_DIM

    def spread(ref, cols, fill):
        for p in range(cols.shape[-1] // LANES):
            r = cols[:, p * LANES:(p + 1) * LANES]
            ref[:, 2 * p * LANES:(2 * p + 1) * LANES] = jnp.where(low, r, fill).astype(ref.dtype)
            ref[:, (2 * p + 1) * LANES:(2 * p + 2) * LANES] = jnp.where(
                low, _swap_halves(r), fill).astype(ref.dtype)

    for p in range(nsa_q.shape[-1] // LANES):
        r = nsa_q[:, p * LANES:(p + 1) * LANES]
        swapped = _swap_halves(r)
        nq_ref[:, 2 * p * LANES:(2 * p + 1) * LANES] = jnp.where(low, r, swapped).astype(
            nq_ref.dtype)
        nq_ref[:, (2 * p + 1) * LANES:(2 * p + 2) * LANES] = jnp.where(low, swapped, r).astype(
            nq_ref.dtype)

    key_blk = lax.shift_right_logical(
        s_idx * tm + lax.broadcasted_iota(jnp.int32, (tm, LANES), 0), int(math.log2(SEL_BLOCK)))
    spread(ks_ref, keys[:, 3 * LANES:4 * LANES],
           jnp.where(lane == HEAD_DIM + key_blk, 1.0, 0.0))
    spread(fv_ref, fox_v, 1.0)
    spread(vs_ref, tail[:, 0:LANES], 1.0)
    spread(vw_ref, tail[:, LANES:2 * LANES], 1.0)

    z = tail[:, 2 * LANES:3 * LANES] + bf_ref[...]
    is_f = lane < FOX_HEADS
    log_f = jnp.minimum(z, 0.0) - jnp.log(1.0 + jnp.exp(-jnp.abs(z)))
    log_f = jnp.where(is_f, log_f, 0.0)

    row = lax.broadcasted_iota(jnp.int32, (tm, tm), 0)
    col = lax.broadcasted_iota(jnp.int32, (tm, tm), 1)
    tri = jnp.where(row >= col, 1.0, 0.0).astype(BF16)
    sums = _dot(tri, jnp.concatenate(_split3(log_f), axis=1))
    local = sums[:, 0:LANES] + sums[:, LANES:2 * LANES] + sums[:, 2 * LANES:3 * LANES]

    @pl.when(s_idx == 0)
    def _():
        carry_ref[...] = jnp.zeros_like(carry_ref)

    cum = local + carry_ref[...]
    carry_ref[...] = cum[tm - 1:tm, :]
    misc = jnp.where(is_f, cum * LOG2E, _sigmoid(z))
    misc_ref[...] = misc
    misct_ref[...] = misc.T


def _in_proj(x, mod, w_r, bf_row):
    B, S, D = x.shape
    tm = ROW_TILE
    widths = (512, 512, 2 * (_C_K3 - _C_NQ), LANES, LANES, LANES, 2 * (_C_FV - _C_SK),
              2 * (_C_SV - _C_FV),
              2 * (_C_WV - _C_SV), 2 * (_C_MISC - _C_WV))
    wide = lambda w: pl.BlockSpec((None, tm, w), lambda b, s: (b, s, 0))
    out_shape = ([jax.ShapeDtypeStruct((B, S, w), BF16) for w in widths]
                 + [jax.ShapeDtypeStruct((B, S, LANES), F32),
                    jax.ShapeDtypeStruct((B, LANES, S), F32)])
    out_specs = ([wide(w) for w in widths]
                 + [wide(LANES), pl.BlockSpec((None, LANES, tm), lambda b, s: (b, 0, s))])
    return pl.pallas_call(
        _in_proj_kernel,
        grid=(B, S // tm),
        in_specs=[pl.BlockSpec((None, tm, D), lambda b, s: (b, s, 0)),
                  pl.BlockSpec((None, 6, D), lambda b, s: (b, 0, 0)),
                  pl.BlockSpec((D, _IN_COLS), lambda b, s: (0, 0)),
                  pl.BlockSpec((1, LANES), lambda b, s: (0, 0))],
        out_specs=out_specs,
        out_shape=out_shape,
        scratch_shapes=[pltpu.VMEM((1, LANES), F32)],
        compiler_params=pltpu.CompilerParams(dimension_semantics=("parallel", "arbitrary"),
                                             vmem_limit_bytes=VMEM_LIMIT),
        name="in_proj",
    )(x, mod, w_r, bf_row)


def _softmax_weights(s, m):
    return jnp.exp2((s - m).astype(BF16))


def _flash_update(carry, s, vt):
    m, acc = carry
    m_new = jnp.maximum(m, jnp.max(s, axis=-1, keepdims=True))
    alpha = jnp.exp2(m - m_new)
    p = _softmax_weights(s, m_new)
    rows = acc.shape[0]
    acc = alpha.reshape(rows, 1) * acc + _dot(p.reshape(rows, s.shape[-1]), vt)
    return m_new, acc


def _normalize(acc):
    return acc / _swap_halves(acc)


def _fox_kernel(q_ref, k_ref, v_ref, ck_ref, o_ref):
    i = pl.program_id(2)
    tq = q_ref.shape[0]
    tk = ck_ref.shape[2]
    q2 = q_ref[...].astype(F32)
    lane = lax.broadcasted_iota(jnp.int32, (tq, LANES), 1)
    low = lane < HEAD_DIM
    halves = (low, jnp.logical_not(low))
    qh = [jnp.where(h, q2, 0.0).astype(BF16) for h in halves]
    col_minus_row = (lax.broadcasted_iota(jnp.int32, (tq, tk), 1)
                     - lax.broadcasted_iota(jnp.int32, (tq, tk), 0))

    def step(jj, carry, diagonal):
        k0 = jj * tk
        kt = k_ref[pl.ds(k0, tk), :]
        new = []
        for hh in range(2):
            s = _dot_nt(qh[hh], kt) - ck_ref[hh, pl.ds(jj, 1), :]
            if diagonal:
                s = jnp.where(col_minus_row <= 0, s, NEG_BIG)
            vt = v_ref[pl.ds(k0, tk), hh * LANES:(hh + 1) * LANES]
            new.append(_flash_update(carry[hh], s, vt))
        return tuple(new)

    init = tuple((jnp.full((tq, 1), NEG_BIG, F32), jnp.zeros((tq, LANES), F32))
                 for _ in range(2))

    assert tq == tk
    for n_full in range(k_ref.shape[0] // tk):
        @pl.when(i == n_full)
        def _(n_full=n_full):
            carry = init
            for jj in range(n_full):
                carry = step(jj, carry, False)
            carry = step(n_full, carry, True)
            o_ref[...] = jnp.where(low, _normalize(carry[0][1]),
                                   _swap_halves(_normalize(carry[1][1]))).astype(o_ref.dtype)


def _fox(fq, fk, fv, cum_row):
    B, S, W = fq.shape
    tq = tk = FOX_TILE
    n_pairs = W // LANES
    cum_row = cum_row.reshape(B, n_pairs, 2, S // tk, tk)
    return pl.pallas_call(
        _fox_kernel,
        grid=(B, n_pairs, S // tq),
        in_specs=[pl.BlockSpec((None, tq, LANES), lambda b, p, i: (b, i, p)),
                  pl.BlockSpec((None, S, LANES), lambda b, p, i: (b, 0, p)),
                  pl.BlockSpec((None, S, 2 * LANES), lambda b, p, i: (b, 0, p)),
                  pl.BlockSpec((None, None, 2, S // tk, tk), lambda b, p, i: (b, p, 0, 0, 0))],
        out_specs=pl.BlockSpec((None, tq, LANES), lambda b, p, i: (b, i, p)),
        out_shape=jax.ShapeDtypeStruct((B, S, W), BF16),
        compiler_params=pltpu.CompilerParams(
            dimension_semantics=("parallel", "parallel", "arbitrary"),
            vmem_limit_bytes=VMEM_LIMIT),
        name="fox",
    )(fq, fk, fv, cum_row)


def _compress_kernel(xk_ref, xv_ref, pk_ref, pv_ref, wak_ref, wbk_ref, wav_ref, wbv_ref,
                     w2k_ref, w2v_ref, ok_ref, ov_ref):
    n_rows = xk_ref.shape[0]
    for x_ref, p_ref, wa_ref, wb_ref, w2_ref, o_ref in (
            (xk_ref, pk_ref, wak_ref, wbk_ref, w2k_ref, ok_ref),
            (xv_ref, pv_ref, wav_ref, wbv_ref, w2v_ref, ov_ref)):
        x = x_ref[...].astype(F32)
        xa = (x + p_ref[0:1, :]).astype(BF16)
        xb = (x + p_ref[1:2, :]).astype(BF16)
        hb = _dot(xb, wb_ref[...])
        h1 = _dot(xa, wa_ref[...]) + pltpu.roll(hb, n_rows - 1, 0)
        o_ref[...] = _dot(_silu(h1).astype(BF16), w2_ref[...]).astype(o_ref.dtype)


def _compress(xk, xv, pk, pv, wak, wbk, wav, wbv, w2k, w2v):
    B, R, C = xk.shape
    xspec = pl.BlockSpec((None, R, C), lambda b: (b, 0, 0))
    full = lambda a: pl.BlockSpec(a.shape, lambda b: (0,) * a.ndim)
    ospec = pl.BlockSpec((None, R, LANES), lambda b: (b, 0, 0))
    return pl.pallas_call(
        _compress_kernel,
        grid=(B,),
        in_specs=[xspec, xspec] + [full(a) for a in (pk, pv, wak, wbk, wav, wbv, w2k, w2v)],
        out_specs=[ospec, ospec],
        out_shape=[jax.ShapeDtypeStruct((B, R, LANES), BF16)] * 2,
        compiler_params=pltpu.CompilerParams(dimension_semantics=("parallel",),
                                             vmem_limit_bytes=VMEM_LIMIT),
        name="compress",
    )(xk, xv, pk, pv, wak, wbk, wav, wbv, w2k, w2v)


def _rank_rows(score):
    n = score.shape[0]
    j = lax.broadcasted_iota(jnp.int32, score.shape, 0)
    rank = jnp.zeros(score.shape, jnp.int32)
    for i in range(n):
        si = score[i:i + 1, :]
        beats = (si > score) | ((si == score) & (j > i))
        rank = rank + jnp.where(beats, 1, 0)
    return rank


def _dup_head(q_ref, r):
    return q_ref[:, r * LANES:(r + 1) * LANES].astype(F32)


def _pack_heads(o_list, g):
    lane = lax.broadcasted_iota(jnp.int32, o_list[0].shape, 1)
    in_g = (lane >= g * HEAD_DIM) & (lane < (g + 1) * HEAD_DIM)
    both = []
    for o in o_list:
        om = jnp.where(in_g, o, 0.0)
        both.append(om + _swap_halves(om))
    pairs = [jnp.where(lane < HEAD_DIM, both[2 * p], both[2 * p + 1]) for p in range(2)]
    return jnp.concatenate(pairs, axis=1)


def _group_gates(misc, g):
    w = 3 * NSA_GQA
    gates = misc[:, FOX_HEADS:FOX_HEADS + w]
    for other in range(1, NSA_GROUPS):
        gates = jnp.where(g == other, misc[:, FOX_HEADS + other * w:FOX_HEADS + (other + 1) * w],
                          gates)
    return gates


def _cmp_sel_kernel(q_ref, kc_ref, vc_ref, bias_ref, gate_ref, ovl_ref, oc_ref, sel_ref):
    g = pl.program_id(1)
    i = pl.program_id(2)
    tq = q_ref.shape[0]
    n_pad = kc_ref.shape[0]
    lane = lax.broadcasted_iota(jnp.int32, (n_pad, LANES), 1)
    in_g = (lane >= g * HEAD_DIM) & (lane < (g + 1) * HEAD_DIM)
    kc = jnp.where(in_g, kc_ref[...].astype(F32), 0.0).astype(BF16)
    vc = vc_ref[...]
    t = i * tq + lax.broadcasted_iota(jnp.int32, (tq, n_pad), 0)
    n = lax.broadcasted_iota(jnp.int32, (tq, n_pad), 1)
    valid = t >= n * CMP_STRIDE + (CMP_BLOCK - 1)
    gates = _group_gates(gate_ref[...], g)
    p_sum = jnp.zeros((tq, n_pad), F32)
    outs = []
    for r in range(NSA_GQA):
        qr = q_ref[:, r * LANES:(r + 1) * LANES]
        s = _dot_nt(qr, kc)
        s = jnp.where(valid, s + bias_ref[r], NEG_BIG)
        m = jnp.max(s, axis=-1, keepdims=True)
        p = jnp.exp2(s - m)
        p = p / jnp.sum(p, axis=-1, keepdims=True)
        p = jnp.where(valid, p, 0.0)
        p_sum = p_sum + p
        outs.append(_dot(p.astype(BF16), vc) * gates[:, 3 * r:3 * r + 1])
    oc_ref[...] = _pack_heads(outs, g).astype(oc_ref.dtype)

    ovl = ovl_ref[...]
    hi, mid, lo = _split3(p_sum)
    imp = _dot_nt(ovl, hi) + _dot_nt(ovl, mid) + _dot_nt(ovl, lo)
    n_blk = imp.shape[0]
    j = lax.broadcasted_iota(jnp.int32, (n_blk, tq), 0)
    qb = jnp.right_shift(i * tq + lax.broadcasted_iota(jnp.int32, (n_blk, tq), 1),
                         int(math.log2(SEL_BLOCK)))
    forced = (j == 0) | (j == qb) | (j == qb - 1)
    causal = j <= qb
    score = jnp.where(causal, imp + jnp.where(forced, FORCE_SCORE, 0.0), -FORCE_SCORE)
    chosen = (_rank_rows(score) < N_SEL) & causal
    sel = jnp.where(chosen, 1.0, 0.0)
    sel = jnp.concatenate([sel, jnp.zeros((LANES - n_blk, tq), F32)], axis=0)
    sel_ref[...] = sel.T


def _cmp_sel(nq, kcmp, vcmp, bias_c, gates_g, ovl_t):
    B, S, _ = nq.shape
    tq = CMP_TILE
    n_pad = kcmp.shape[1]
    return pl.pallas_call(
        _cmp_sel_kernel,
        grid=(B, NSA_GROUPS, S // tq),
        in_specs=[pl.BlockSpec((None, tq, NSA_GQA * LANES), lambda b, g, i: (b, i, g)),
                  pl.BlockSpec((None, n_pad, LANES), lambda b, g, i: (b, 0, 0)),
                  pl.BlockSpec((None, n_pad, LANES), lambda b, g, i: (b, 0, 0)),
                  pl.BlockSpec((NSA_GQA, tq, n_pad), lambda b, g, i: (g, i, 0)),
                  pl.BlockSpec((None, tq, LANES), lambda b, g, i: (b, i, 0)),
                  pl.BlockSpec(ovl_t.shape, lambda b, g, i: (0, 0))],
        out_specs=[pl.BlockSpec((None, tq, 2 * LANES), lambda b, g, i: (b, i, g)),
                   pl.BlockSpec((None, None, tq, LANES), lambda b, g, i: (b, g, i, 0))],
        out_shape=[jax.ShapeDtypeStruct((B, S, NSA_HEADS * HEAD_DIM), BF16),
                   jax.ShapeDtypeStruct((B, NSA_GROUPS, S, LANES), F32)],
        compiler_params=pltpu.CompilerParams(
            dimension_semantics=("parallel", "parallel", "arbitrary"),
            vmem_limit_bytes=VMEM_LIMIT),
        name="cmp_sel",
    )(nq, kcmp, vcmp, bias_c, gates_g, ovl_t)


def _nsa_kernel(q_ref, ks_ref, vs_ref, kw_ref, vw_ref, sel_ref, w4_ref, gate_ref, oc_ref,
                o_ref):
    for tile in range(ks_ref.shape[0] // q_ref.shape[0]):
        @pl.when(pl.program_id(2) == tile)
        def _(tile=tile):
            _nsa_tile(tile, q_ref, ks_ref, vs_ref, kw_ref, vw_ref, sel_ref, w4_ref, gate_ref,
                      oc_ref, o_ref)


def _nsa_tile(i, q_ref, ks_ref, vs_ref, kw_ref, vw_ref, sel_ref, w4_ref, gate_ref, oc_ref, o_ref):
    g = pl.program_id(0)
    tq = q_ref.shape[0]
    tk = w4_ref.shape[-1]
    H = NSA_GQA
    lane = lax.broadcasted_iota(jnp.int32, (tq, LANES), 1)
    in_g = (lane >= g * HEAD_DIM) & (lane < (g + 1) * HEAD_DIM)
    heads = [_dup_head(q_ref, r) for r in range(H)]
    qs = jnp.concatenate([jnp.where(in_g, hd, 0.0).astype(BF16) for hd in heads], axis=0)

    blk_mask = _swap_halves((sel_ref[...] - 1.0) * (-NEG_BIG))
    qsel = jnp.concatenate(
        [jnp.where(lane < HEAD_DIM, hd, blk_mask).astype(BF16) for hd in heads], axis=0)

    def sel_step(jj, carry, diagonal):
        k0 = jj * tk
        ahead = i * tq - k0
        width = min(ahead + tq, tk) if diagonal else tk
        rows = pl.ds(k0, width)
        s = (_dot_nt(qsel, ks_ref[rows, :]).reshape(H, tq, width)
             + w4_ref[:, min(i - 2 * jj, 3), :, 0:width])
        if diagonal:
            cmr = (lax.broadcasted_iota(jnp.int32, (tq, width), 1)
                   - lax.broadcasted_iota(jnp.int32, (tq, width), 0))
            s = jnp.where((cmr <= ahead)[None], s, NEG_BIG)
        return _flash_update(carry, s, vs_ref[rows, :])

    carry = (jnp.full((H, tq, 1), NEG_BIG, F32), jnp.zeros((H * tq, LANES), F32))
    n_full = (i * tq) // tk
    for jj in range(n_full):
        carry = sel_step(jj, carry, False)
    _, acc_s = sel_step(n_full, carry, True)

    old = [max(i - d, 0) * tq for d in (2, 1)]
    now = i * tq
    k_old = jnp.concatenate([kw_ref[pl.ds(st, tq), :] for st in old], axis=0)
    v_old = jnp.concatenate([vw_ref[pl.ds(st, tq), :] for st in old], axis=0)
    col = lax.broadcasted_iota(jnp.int32, (tq, 2 * tq), 1)
    cmr = col - lax.broadcasted_iota(jnp.int32, (tq, 2 * tq), 0)
    never = 4 * tq
    ok_old = (((col < tq) & (cmr > (0 if i >= 2 else never)))
              | (col >= (tq if i >= 1 else never)))
    s_old = jnp.where(ok_old[None], _dot_nt(qs, k_old).reshape(H, tq, 2 * tq) + w4_ref[:, 2],
                      NEG_BIG)
    s_now = jnp.where((cmr[:, 0:tq] <= 0)[None],
                      _dot_nt(qs, kw_ref[pl.ds(now, tq), :]).reshape(H, tq, tq)
                      + w4_ref[:, 0, :, 0:tq], NEG_BIG)
    m_w = jnp.maximum(jnp.max(s_old, axis=-1, keepdims=True),
                      jnp.max(s_now, axis=-1, keepdims=True))
    acc_w = (_dot(_softmax_weights(s_old, m_w).reshape(H * tq, 2 * tq), v_old)
             + _dot(_softmax_weights(s_now, m_w).reshape(H * tq, tq), vw_ref[pl.ds(now, tq), :]))

    gates = _group_gates(gate_ref[...], g)
    o_s = _normalize(acc_s)
    o_w = _normalize(acc_w)
    outs = []
    for r in range(H):
        sl = slice(r * tq, (r + 1) * tq)
        outs.append(o_s[sl] * gates[:, 3 * r + 1:3 * r + 2] + o_w[sl] * gates[:, 3 * r + 2:3 * r + 3])
    lane = lax.broadcasted_iota(jnp.int32, (tq, LANES), 1)
    pairs = [jnp.where(lane < HEAD_DIM, outs[2 * p], _swap_halves(outs[2 * p + 1]))
             for p in range(H // 2)]
    o_ref[...] = (jnp.concatenate(pairs, axis=1) + oc_ref[...].astype(F32)).astype(o_ref.dtype)


def _nsa(nq, ks, vs, kw, vw, sel, w4, misc, oc):
    B, S, _ = nq.shape
    tq = ATT_TILE
    tk = w4.shape[-1]
    both = lambda: pl.BlockSpec((None, S, LANES), lambda g, b, i: (b, 0, 0))
    mine = lambda: pl.BlockSpec((None, S, LANES), lambda g, b, i: (b, 0, g))
    return pl.pallas_call(
        _nsa_kernel,
        grid=(NSA_GROUPS, B, S // tq),
        in_specs=[pl.BlockSpec((None, tq, NSA_GQA * LANES), lambda g, b, i: (b, i, g)),
                  mine(), mine(), both(), mine(),
                  pl.BlockSpec((None, None, tq, LANES), lambda g, b, i: (b, g, i, 0)),
                  pl.BlockSpec((NSA_GQA, 4, tq, tk), lambda g, b, i: (g, 0, 0, 0)),
                  pl.BlockSpec((None, tq, LANES), lambda g, b, i: (b, i, 0)),
                  pl.BlockSpec((None, tq, 2 * LANES), lambda g, b, i: (b, i, g))],
        out_specs=pl.BlockSpec((None, tq, 2 * LANES), lambda g, b, i: (b, i, g)),
        out_shape=jax.ShapeDtypeStruct((B, S, NSA_HEADS * HEAD_DIM), BF16),
        compiler_params=pltpu.CompilerParams(
            dimension_semantics=("parallel", "parallel", "arbitrary"),
            vmem_limit_bytes=VMEM_LIMIT),
        name="nsa",
    )(nq, ks, vs, kw, vw, sel, w4, misc, oc)


def _layer_norm(y, g, b):
    mu = jnp.mean(y, axis=-1, keepdims=True)
    yc = y - mu
    var = jnp.mean(yc * yc, axis=-1, keepdims=True)
    return yc * lax.rsqrt(var + LN_EPS) * g + b


def _top_rows(score, k):
    n = score.shape[0]
    idx = lax.broadcasted_iota(jnp.int32, score.shape, 0).astype(F32)
    alive = jnp.ones(score.shape, F32)
    for _ in range(k):
        live = alive > 0.0
        best = jnp.max(jnp.where(live, score, -jnp.inf), axis=0, keepdims=True)
        first = jnp.min(jnp.where(live & (score == best), idx, float(n)), axis=0, keepdims=True)
        alive = jnp.where(idx == first, 0.0, alive)
    return alive == 0.0


def _router_gates_t(h2, wr_t, eb_col):
    tm = h2.shape[0]
    h_hi, h_lo, _ = _split3(h2)
    w_hi, w_lo, _ = _split3(wr_t)
    logit = _dot_nt(w_hi, h_hi) + _dot_nt(w_hi, h_lo) + _dot_nt(w_lo, h_hi)
    scores = _sigmoid(logit)
    biased = scores + eb_col
    e_in = lax.broadcasted_iota(jnp.int32, (GROUP_SIZE, tm), 0).astype(F32)
    gs_rows = []
    for gi in range(N_EXPERT_GROUPS):
        grp = biased[gi * GROUP_SIZE:(gi + 1) * GROUP_SIZE, :]
        m1 = jnp.max(grp, axis=0, keepdims=True)
        first = jnp.min(jnp.where(grp == m1, e_in, float(GROUP_SIZE)), axis=0, keepdims=True)
        m2 = jnp.max(jnp.where(e_in == first, -jnp.inf, grp), axis=0, keepdims=True)
        gs_rows.append(m1 + m2)
    gscore = jnp.concatenate(gs_rows, axis=0)
    g_keep = _rank_rows(gscore) < TOPK_GROUPS
    keep = jnp.concatenate(
        [jnp.broadcast_to(g_keep[gi:gi + 1, :], (GROUP_SIZE, tm)) for gi in range(N_EXPERT_GROUPS)],
        axis=0)
    masked = jnp.where(keep, biased, -jnp.inf)
    chosen = _top_rows(masked, TOP_K)
    w = jnp.where(chosen, scores, 0.0)
    return w / jnp.sum(w, axis=0, keepdims=True) * ROUTED_SCALE


def _out_proj_kernel(alpha, of_ref, on_ref, x_ref, mod_ref, w_ref, lg_ref, lb_ref, wr_ref,
                     eb_ref, x1_ref, h2_ref, gate_ref, gate_t_ref):
    half = of_ref.shape[1]
    mod = mod_ref[...]
    mixed = _dot(of_ref[...], w_ref[0:half, :]) + _dot(on_ref[...], w_ref[half:2 * half, :])
    y = alpha * x_ref[...] + mod[2:3, :] * mixed
    x1 = _layer_norm(y, lg_ref[...], lb_ref[...])
    x1_ref[...] = x1
    h2 = x1 * (1.0 + mod[4:5, :]) + mod[3:4, :]
    h2_ref[...] = h2.astype(h2_ref.dtype)
    gates_t = _router_gates_t(h2, wr_ref[...], eb_ref[...])
    gate_t_ref[...] = gates_t
    tm = h2.shape[0]
    gates_t = jnp.concatenate([gates_t, jnp.zeros((LANES - N_EXPERTS, tm), F32)], axis=0)
    gate_ref[...] = gates_t.T


def _out_proj(alpha, o_fox, o_nsa, x, mod, w_out, ln_g, ln_b, wr_t, eb_col):
    B, S, D = x.shape
    tm = ROW_TILE
    half = o_fox.shape[-1]
    row = lambda a: pl.BlockSpec(a.shape, lambda b, s: (0, 0))
    return pl.pallas_call(
        functools.partial(_out_proj_kernel, alpha),
        grid=(B, S // tm),
        in_specs=[pl.BlockSpec((None, tm, half), lambda b, s: (b, s, 0)),
                  pl.BlockSpec((None, tm, half), lambda b, s: (b, s, 0)),
                  pl.BlockSpec((None, tm, D), lambda b, s: (b, s, 0)),
                  pl.BlockSpec((None, 6, D), lambda b, s: (b, 0, 0)),
                  row(w_out), row(ln_g), row(ln_b), row(wr_t), row(eb_col)],
        out_specs=[pl.BlockSpec((None, tm, D), lambda b, s: (b, s, 0)),
                   pl.BlockSpec((None, tm, D), lambda b, s: (b, s, 0)),
                   pl.BlockSpec((None, tm, LANES), lambda b, s: (b, s, 0)),
                   pl.BlockSpec((N_EXPERTS, tm), lambda b, s: (0, b * (S // tm) + s))],
        out_shape=[jax.ShapeDtypeStruct((B, S, D), F32),
                   jax.ShapeDtypeStruct((B, S, D), BF16),
                   jax.ShapeDtypeStruct((B, S, LANES), F32),
                   jax.ShapeDtypeStruct((N_EXPERTS, B * S), F32)],
        compiler_params=pltpu.CompilerParams(dimension_semantics=("parallel", "parallel"),
                                             vmem_limit_bytes=VMEM_LIMIT),
        name="out_proj",
    )(o_fox, o_nsa, x, mod, w_out, ln_g, ln_b, wr_t, eb_col)


SORT_TILE = 256
ROW_ALIGN = 16
EXP_TILE = 1024
EXP_STRIP = 1024
P_CHUNK = 256


def _strict_upper(n):
    return jnp.where(lax.broadcasted_iota(jnp.int32, (n, n), 0)
                     < lax.broadcasted_iota(jnp.int32, (n, n), 1), 1.0, 0.0).astype(BF16)


def _strict_lower(n):
    return jnp.where(lax.broadcasted_iota(jnp.int32, (n, n), 1)
                     < lax.broadcasted_iota(jnp.int32, (n, n), 0), 1.0, 0.0).astype(BF16)


def _local_rows_bound(ts):
    rows = TOP_K * ts + N_EXPERTS * (ROW_ALIGN - 1)
    return -(-rows // P_CHUNK) * P_CHUNK


def _piece_cols(ts):
    return -(-(_local_rows_bound(ts) // ROW_ALIGN) // LANES) * LANES


def _sorted_tiles_bound(T):
    rows = TOP_K * T + (T // SORT_TILE) * N_EXPERTS * (ROW_ALIGN - 1)
    return -(-rows // EXP_TILE) + N_EXPERTS


def _sorted_tiles_expected(T):
    groups = (T // SORT_TILE) * N_EXPERTS
    rows = TOP_K * T + groups * ((ROW_ALIGN - 1) / 2 + 2)
    return int(-(-rows // EXP_TILE) + math.ceil(0.65 * N_EXPERTS))


def _moe_meta_kernel(gt_ref, ptab_ref, loc_et_ref, np_et_ref, loc_te_ref, np_te_ref, tot_ref,
                     erow_ref, texp_ref, nused_ref):
    E, T = gt_ref.shape
    mask = jnp.where(gt_ref[...] > 0.0, 1.0, 0.0).astype(BF16)
    t_id = lax.shift_right_logical(lax.broadcasted_iota(jnp.int32, (T, LANES), 0),
                                   int(math.log2(SORT_TILE)))
    tind = jnp.where(t_id == lax.broadcasted_iota(jnp.int32, (T, LANES), 1), 1.0, 0.0)
    cnt = _dot(mask, tind.astype(BF16))
    n16 = jnp.floor((cnt + (ROW_ALIGN - 1.0)) * (1.0 / ROW_ALIGN))
    n16b = n16.astype(BF16)
    q = EXP_TILE // ROW_ALIGN
    len16 = jnp.sum(n16, axis=1, keepdims=True)
    pad16 = jnp.floor((len16 + (q - 1.0)) * (1.0 / q)) * q
    sl = _strict_lower(E)
    hi, mid, lo = _split3(jnp.broadcast_to(pad16, (E, LANES)))
    start16 = _dot(sl, hi) + _dot(sl, mid) + _dot(sl, lo)
    gdst16 = start16 + _dot(n16b, _strict_upper(LANES))
    loc16 = _dot(sl, n16b)

    def t(a):
        return jnp.concatenate([a, jnp.zeros((LANES - E, LANES), F32)], axis=0).T

    scale = float(ROW_ALIGN)
    loc_et_ref[...] = loc16 * scale
    np_et_ref[...] = n16 * scale
    loc_te_ref[...] = t(loc16) * scale
    np_te_ref[...] = t(n16) * scale
    tot_ref[...] = (jnp.sum(n16, axis=0, keepdims=True) * scale).astype(jnp.int32)

    n_t, n_blk = ptab_ref.shape
    blk = lax.broadcasted_iota(jnp.int32, (E, n_blk), 1).astype(F32)
    for tile in range(n_t):
        lo_c = loc16[:, tile:tile + 1]
        inside = (lo_c <= blk) & (blk < lo_c + n16[:, tile:tile + 1])
        dst = jnp.sum(jnp.where(inside, gdst16[:, tile:tile + 1] + (blk - lo_c), 0.0),
                      axis=0, keepdims=True)
        ptab_ref[tile:tile + 1, :] = (dst * scale).astype(jnp.int32)
    ends = jnp.concatenate([t(start16 + len16)[0:1, :], t(start16 + pad16)[0:1, :],
                            jnp.zeros((erow_ref.shape[0] - 2, LANES), F32)], axis=0)
    erow_ref[...] = (ends * scale).astype(jnp.int32)
    n_tab = texp_ref.shape[1]
    tile_row16 = (lax.broadcasted_iota(jnp.int32, (E, n_tab), 1) * q).astype(F32)
    owner = jnp.sum(jnp.where(start16[:, 0:1] <= tile_row16, 1.0, 0.0), axis=0, keepdims=True)
    texp_ref[...] = (owner - 1.0).astype(jnp.int32)
    n_used = jnp.sum(pad16, axis=0, keepdims=True) * (1.0 / q)
    nused_ref[...] = jnp.broadcast_to(n_used, (1, LANES)).astype(jnp.int32)


def _moe_meta(gates_t, n_tab):
    E, T = gates_t.shape
    i32 = jnp.int32
    return pl.pallas_call(
        _moe_meta_kernel,
        out_shape=[jax.ShapeDtypeStruct((T // SORT_TILE, _piece_cols(SORT_TILE)), i32),
                   jax.ShapeDtypeStruct((E, LANES), F32),
                   jax.ShapeDtypeStruct((E, LANES), F32),
                   jax.ShapeDtypeStruct((LANES, LANES), F32),
                   jax.ShapeDtypeStruct((LANES, LANES), F32),
                   jax.ShapeDtypeStruct((1, LANES), i32),
                   jax.ShapeDtypeStruct((8, LANES), i32),
                   jax.ShapeDtypeStruct((1, n_tab), i32),
                   jax.ShapeDtypeStruct((1, LANES), i32)],
        compiler_params=pltpu.CompilerParams(vmem_limit_bytes=VMEM_LIMIT),
        name="moe_meta",
    )(gates_t)


def _start_pieces(tile, ptab_s, tot_s, n_cols, make_copy):
    n_pieces = lax.shift_right_logical(tot_s[tile], int(math.log2(ROW_ALIGN)))

    def body(b, carry):
        make_copy(pl.multiple_of(b * ROW_ALIGN, ROW_ALIGN),
                  pl.multiple_of(ptab_s[tile * n_cols + b], ROW_ALIGN)).start()
        return carry

    lax.fori_loop(0, n_pieces, body, 0)


def _moe_sort_kernel(ptab_s, tot_s, lend_s, rend_s, gt_ref, gtok_ref, h_ref, locrow_ref,
                     nprow_ref, xs_hbm, buf, zbuf, sem, zsem):
    tau = pl.program_id(0)
    n_t = pl.num_programs(0)
    slot = lax.rem(tau, 2)
    E, ts = gt_ref.shape
    D = h_ref.shape[1]
    n_cols = _piece_cols(ts)

    def copies(tile, sl, wait):
        if wait:
            rows = pl.multiple_of(tot_s[tile], ROW_ALIGN)

            @pl.when(rows > 0)
            def _():
                pltpu.make_async_copy(buf.at[sl, pl.ds(0, rows)], xs_hbm.at[pl.ds(0, rows)],
                                      sem.at[sl]).wait()
        else:
            def make_copy(loc, dst):
                return pltpu.make_async_copy(buf.at[sl, pl.ds(loc, ROW_ALIGN)],
                                             xs_hbm.at[pl.ds(dst, ROW_ALIGN)], sem.at[sl])
            _start_pieces(tile, ptab_s, tot_s, n_cols, make_copy)

    z_rows = zbuf.shape[0]
    used_rows = rend_s[E - 1]
    n_spare = (xs_hbm.shape[0] - used_rows) // z_rows

    def spare_fill(wait):
        def body(c, carry):
            dst = pl.multiple_of(used_rows + c * z_rows, z_rows)
            cp = pltpu.make_async_copy(zbuf, xs_hbm.at[pl.ds(dst, z_rows)], zsem.at[1])
            if wait:
                cp.wait()
            else:
                cp.start()
            return carry

        lax.fori_loop(0, n_spare, body, 0)

    @pl.when(tau == 0)
    def _():
        zbuf[...] = jnp.zeros_like(zbuf)
        spare_fill(False)

    @pl.when(tau >= 2)
    def _():
        copies(tau - 2, slot, True)

    g = gt_ref[...]
    mask = g > 0.0
    maskb = jnp.where(mask, 1.0, 0.0).astype(BF16)
    pad = jnp.zeros((LANES - E, ts), F32)
    pos = jnp.where(mask, _dot(maskb, _strict_upper(ts)), -1.0)
    pos = jnp.concatenate([pos, pad], axis=0).astype(BF16)
    lo_row = locrow_ref[...]
    hi_row = lo_row + nprow_ref[...]
    h = jnp.concatenate([h_ref[...]] + list(_split3(gtok_ref[...])), axis=1)
    lane = lax.broadcasted_iota(jnp.int32, (P_CHUNK, LANES), 1)

    def chunk(c):
        r0 = c * P_CHUNK
        r = (r0 + lax.broadcasted_iota(jnp.int32, (P_CHUNK, LANES), 0)).astype(F32)
        inside = (lo_row <= r) & (r < hi_row)
        group = jnp.where(inside, 1.0, 0.0).astype(BF16)
        want = r[:, 0:1] - jnp.sum(jnp.where(inside, lo_row, 0.0), axis=1, keepdims=True)
        hit = _dot(group, pos) == want
        rows = _dot(jnp.where(hit, 1.0, 0.0).astype(BF16), h)
        buf[slot, pl.ds(r0, P_CHUNK), 0:D] = rows[:, 0:D].astype(buf.dtype)
        extra = jnp.zeros((P_CHUNK, LANES), F32)
        for k in range(3):
            mine = jnp.where(inside, rows[:, D + k * LANES:D + (k + 1) * LANES], 0.0)
            extra = jnp.where(lane == k, jnp.sum(mine, axis=1, keepdims=True), extra)
        buf[slot, pl.ds(r0, P_CHUNK), D:D + LANES] = extra.astype(buf.dtype)

    for c in range(buf.shape[1] // P_CHUNK):
        chunk(c)
    copies(tau, slot, False)

    @pl.when(tau == n_t - 1)
    def _():
        @pl.when(n_t >= 2)
        def _():
            copies(tau - 1, 1 - slot, True)
        copies(tau, slot, True)
        spare_fill(True)

        sizes =[zbuf.shape[0] >> s for s in range(int(math.log2(zbuf.shape[0] // ROW_ALIGN)) + 1)]

        def fill(wait):
            def e_body(e, carry):
                start = lend_s[e]
                n = rend_s[e] - start
                off = start
                for size in sizes:
                    bit = jnp.bitwise_and(n, size)

                    @pl.when(bit != 0)
                    def _(off=off, size=size):
                        cp = pltpu.make_async_copy(
                            zbuf.at[pl.ds(0, size)],
                            xs_hbm.at[pl.ds(pl.multiple_of(off, ROW_ALIGN), size)], zsem.at[0])
                        if wait:
                            cp.wait()
                        else:
                            cp.start()

                    off = off + bit
                return carry

            lax.fori_loop(0, E, e_body, 0)

        fill(False)
        fill(True)


def _moe_sort(ptab, tot, lend, rend, gates_t, gates, h2, loc_te, np_te, n_rows):
    E, T = gates_t.shape
    D = h2.shape[1]
    ts = SORT_TILE
    grid_spec = pltpu.PrefetchScalarGridSpec(
        num_scalar_prefetch=4,
        grid=(T // ts,),
        in_specs=[pl.BlockSpec((E, ts), lambda t, *_: (0, t)),
                  pl.BlockSpec((ts, LANES), lambda t, *_: (t, 0)),
                  pl.BlockSpec((ts, D), lambda t, *_: (t, 0)),
                  pl.BlockSpec((None, 1, LANES), lambda t, *_: (t, 0, 0)),
                  pl.BlockSpec((None, 1, LANES), lambda t, *_: (t, 0, 0))],
        out_specs=pl.BlockSpec(memory_space=pl.ANY),
        scratch_shapes=[pltpu.VMEM((2, _local_rows_bound(ts), D + LANES), BF16),
                        pltpu.VMEM((EXP_TILE // 2, D + LANES), BF16),
                        pltpu.SemaphoreType.DMA((2,)),
                        pltpu.SemaphoreType.DMA((2,))])
    return pl.pallas_call(
        _moe_sort_kernel,
        grid_spec=grid_spec,
        out_shape=jax.ShapeDtypeStruct((n_rows, D + LANES), BF16),
        compiler_params=pltpu.CompilerParams(dimension_semantics=("arbitrary",),
                                             vmem_limit_bytes=VMEM_LIMIT),
        name="moe_sort",
    )(ptab, tot, lend, rend, gates_t, gates, h2, loc_te, np_te)


def _moe_expert_kernel(texp_s, nused_s, x_ref, wg_ref, wu_ref, wd_ref, y_ref, wgu_s, wd_s):
    i = pl.program_id(0)
    f = wd_ref.shape[0]

    @pl.when(i < nused_s[0])
    def _():
        @pl.when((i == 0) | (texp_s[i] != texp_s[jnp.maximum(i - 1, 0)]))
        def _():
            wgu_s[:, 0:f] = wg_ref[...].astype(BF16)
            wgu_s[:, f:2 * f] = wu_ref[...].astype(BF16)
            wd_s[...] = wd_ref[...].astype(BF16)

        d = wd_ref.shape[1]
        for r0 in range(0, x_ref.shape[0], EXP_STRIP):
            rows = slice(r0, r0 + EXP_STRIP)
            gate = jnp.sum(x_ref[rows, d:].astype(F32), axis=1, keepdims=True)
            a = _dot(x_ref[rows, 0:d], wgu_s[...])
            act = _silu(a[:, :f]) * a[:, f:] * gate
            y_ref[rows, :] = _dot(act.astype(BF16), wd_s[...]).astype(y_ref.dtype)

    @pl.when(i >= nused_s[0])
    def _():
        y_ref[...] = jnp.zeros_like(y_ref)


def _moe_expert(texp, nused, xs, w_gate, w_up, w_down, n_tiles):
    n_rows, xw = xs.shape
    D, f = w_gate.shape[-2:]
    tm = EXP_TILE

    def tile(i, texp, nused):
        return jnp.maximum(jnp.minimum(i, nused[0] - 1), 0)

    grid_spec = pltpu.PrefetchScalarGridSpec(
        num_scalar_prefetch=2,
        grid=(n_tiles,),
        in_specs=[pl.BlockSpec((tm, xw), lambda i, te, nu: (tile(i, te, nu), 0)),
                  pl.BlockSpec((None, D, f), lambda i, te, nu: (te[tile(i, te, nu)], 0, 0)),
                  pl.BlockSpec((None, D, f), lambda i, te, nu: (te[tile(i, te, nu)], 0, 0)),
                  pl.BlockSpec((None, f, D), lambda i, te, nu: (te[tile(i, te, nu)], 0, 0))],
        out_specs=pl.BlockSpec((tm, D), lambda i, te, nu: (i, 0)),
        scratch_shapes=[pltpu.VMEM((D, 2 * f), BF16), pltpu.VMEM((f, D), BF16)])
    return pl.pallas_call(
        _moe_expert_kernel,
        grid_spec=grid_spec,
        out_shape=jax.ShapeDtypeStruct((n_rows, D), BF16),
        compiler_params=pltpu.CompilerParams(dimension_semantics=("arbitrary",),
                                             vmem_limit_bytes=VMEM_LIMIT),
        name="moe_expert",
    )(texp, nused, xs, w_gate, w_up, w_down)


def _moe_combine_kernel(alpha, ptab_s, tot_s, g_ref, loc_ref, np_ref, h_ref, x1_ref,
                        mod_ref, sgu_ref, sd_ref, lg_ref, lb_ref, y_hbm, o_ref, ybuf, acc_ref,
                        sem):
    tau = pl.program_id(0)
    n_t = pl.num_programs(0)
    slot = lax.rem(tau, 2)
    ts, n_lane = g_ref.shape
    E = loc_ref.shape[0]
    n_cols = _piece_cols(ts)

    def copies(tile, sl, wait):
        if wait:
            rows = pl.multiple_of(tot_s[tile], ROW_ALIGN)

            @pl.when(rows > 0)
            def _():
                pltpu.make_async_copy(y_hbm.at[pl.ds(0, rows)], ybuf.at[sl, pl.ds(0, rows)],
                                      sem.at[sl]).wait()
        else:
            def make_copy(loc, dst):
                return pltpu.make_async_copy(y_hbm.at[pl.ds(dst, ROW_ALIGN)],
                                             ybuf.at[sl, pl.ds(loc, ROW_ALIGN)], sem.at[sl])
            _start_pieces(tile, ptab_s, tot_s, n_cols, make_copy)

    @pl.when(tau == 0)
    def _():
        ybuf[...] = jnp.zeros_like(ybuf)
        copies(0, 0, False)

    @pl.when(tau + 1 < n_t)
    def _():
        copies(tau + 1, 1 - slot, False)

    g = g_ref[...]
    mask = g > 0.0
    maskb = jnp.where(mask, 1.0, 0.0).astype(BF16)
    pos = jnp.where(mask, _dot(_strict_lower(ts), maskb), -1.0).astype(BF16)
    lane = lax.broadcasted_iota(jnp.int32, loc_ref.shape, 1)
    lo_col = jnp.sum(jnp.where(lane == tau, loc_ref[...], 0.0), axis=1, keepdims=True)
    hi_col = lo_col + jnp.sum(jnp.where(lane == tau, np_ref[...], 0.0), axis=1, keepdims=True)

    f = sd_ref.shape[0]
    a = _dot(h_ref[...], sgu_ref[...])
    acc_ref[...] = _dot((_silu(a[:, :f]) * a[:, f:]).astype(BF16), sd_ref[...])

    copies(tau, slot, True)

    def chunk(c):
        r0 = pl.multiple_of(c * P_CHUNK, P_CHUNK)
        r = (r0 + lax.broadcasted_iota(jnp.int32, (E, P_CHUNK), 1)).astype(F32)
        inside = (lo_col <= r) & (r < hi_col)
        group = jnp.concatenate([jnp.where(inside, 1.0, 0.0),
                                 jnp.zeros((n_lane - E, P_CHUNK), F32)], axis=0).astype(BF16)
        want = r[0:1, :] - jnp.sum(jnp.where(inside, lo_col, 0.0), axis=0, keepdims=True)
        hit = _dot(pos, group) == want
        return _dot(jnp.where(hit, 1.0, 0.0).astype(BF16), ybuf[slot, pl.ds(r0, P_CHUNK), :])

    def pair_body(c, carry):
        acc_ref[...] += chunk(2 * c) + chunk(2 * c + 1)
        return carry

    n_chunks = lax.shift_right_logical(tot_s[tau] + (P_CHUNK - 1), int(math.log2(P_CHUNK)))
    lax.fori_loop(0, lax.shift_right_logical(n_chunks + 1, 1), pair_body, 0)
    y = alpha * x1_ref[...] + mod_ref[5:6, :] * acc_ref[...]
    o_ref[...] = _layer_norm(y, lg_ref[...], lb_ref[...])


def _moe_combine(alpha, ptab, tot, gates, loc_et, np_et, h2, x1, mod, sgu, sd, ln_g, ln_b, ys,
                 S):
    T, D = h2.shape
    ts = SORT_TILE
    per_b = S // ts
    row = lambda a: pl.BlockSpec(a.shape, lambda t, *_: (0, 0))
    grid_spec = pltpu.PrefetchScalarGridSpec(
        num_scalar_prefetch=2,
        grid=(T // ts,),
        in_specs=[pl.BlockSpec((ts, LANES), lambda t, *_: (t, 0)),
                  row(loc_et), row(np_et),
                  pl.BlockSpec((ts, D), lambda t, *_: (t, 0)),
                  pl.BlockSpec((ts, D), lambda t, *_: (t, 0)),
                  pl.BlockSpec((None, 6, D), lambda t, *_: (t // per_b, 0, 0)),
                  row(sgu), row(sd), row(ln_g), row(ln_b),
                  pl.BlockSpec(memory_space=pl.ANY)],
        out_specs=pl.BlockSpec((ts, D), lambda t, *_: (t, 0)),
        scratch_shapes=[pltpu.VMEM((2, _local_rows_bound(ts), D), BF16),
                        pltpu.VMEM((ts, D), F32),
                        pltpu.SemaphoreType.DMA((2,))])
    return pl.pallas_call(
        functools.partial(_moe_combine_kernel, alpha),
        grid_spec=grid_spec,
        out_shape=jax.ShapeDtypeStruct((T, D), F32),
        compiler_params=pltpu.CompilerParams(dimension_semantics=("arbitrary",),
                                             vmem_limit_bytes=VMEM_LIMIT),
        name="moe_combine",
    )(ptab, tot, gates, loc_et, np_et, h2, x1, mod, sgu, sd, ln_g, ln_b, ys)


def _moe(alpha, h2, x1, gates, gates_t, mod, w_gate, w_up, w_down, sgu, sd, ln_g, ln_b):
    B, S, D = x1.shape
    T = B * S
    n_t = T // SORT_TILE
    n_bound = _sorted_tiles_bound(T)
    n_tab = -(-n_bound // LANES) * LANES
    ptab, loc_et, np_et, loc_te, np_te, tot, erow, texp, nused = _moe_meta(gates_t, n_tab)
    ptab = ptab.reshape(-1)
    tot = tot[0, :n_t]
    h2 = h2.reshape(T, D)
    x1 = x1.reshape(T, D)
    gates = gates.reshape(T, LANES)
    loc_rows = loc_te[:n_t].reshape(n_t, 1, LANES)
    np_rows = np_te[:n_t].reshape(n_t, 1, LANES)

    def sized(n_tiles):
        def run():
            xs = _moe_sort(ptab, tot, erow[0, :N_EXPERTS], erow[1, :N_EXPERTS], gates_t, gates,
                           h2, loc_rows, np_rows, n_tiles * EXP_TILE)
            ys = _moe_expert(texp[0], nused[0, :1], xs, w_gate, w_up, w_down, n_tiles)
            return _moe_combine(alpha, ptab, tot, gates, loc_et, np_et, h2, x1, mod, sgu, sd,
                                ln_g, ln_b, ys, S)
        return run

    n_small = min(_sorted_tiles_expected(T), n_bound)
    out = lax.cond(nused[0, 0] <= n_small, sized(n_small), sized(n_bound))
    return out.reshape(B, S, D)


def _rearrange_w_in(w):
    d_in = w.shape[0]
    scale = HEAD_DIM ** -0.5 * LOG2E
    fq, fk, fv = w[:, 0:512], w[:, 512:1024], w[:, 1024:1536]
    ff = w[:, 1536:1544]
    nq = w[:, 1544:2056]
    kc, vc, ks, vs, kw, vw = (w[:, 2056 + k * LANES:2056 + (k + 1) * LANES] for k in range(6))
    ng = w[:, 2824:2848]
    pad = jnp.zeros((d_in, LANES - ff.shape[1] - ng.shape[1]), w.dtype)
    cols = [fq * scale, fk, nq * scale, kc, kw, vc, ks, fv, vs, vw, ff, ng, pad]
    return jnp.concatenate(cols, axis=1).astype(BF16)


def _compress_weights(pos, w1, w2):
    half = CMP_BLOCK // 2
    w1r = w1.reshape(2, half, HEAD_DIM, CMP_HIDDEN)
    zeros = jnp.zeros_like(w1r[0])
    def spread(part):
        g0 = jnp.stack([part, zeros], axis=1).reshape(half * 2 * HEAD_DIM, CMP_HIDDEN)
        g1 = jnp.stack([zeros, part], axis=1).reshape(half * 2 * HEAD_DIM, CMP_HIDDEN)
        return jnp.concatenate([g0, g1], axis=1).astype(BF16)
    wa, wb = spread(w1r[0]), spread(w1r[1])
    z2 = jnp.zeros_like(w2)
    w2bd = jnp.concatenate([jnp.concatenate([w2, z2], axis=1),
                            jnp.concatenate([z2, w2], axis=1)], axis=0).astype(BF16)
    posr = pos.reshape(2, half, 1, HEAD_DIM)
    posr = jnp.broadcast_to(posr, (2, half, NSA_GROUPS, HEAD_DIM)).reshape(2, half * 2 * HEAD_DIM)
    return posr, wa, wb, w2bd


@functools.lru_cache(maxsize=None)
def _static_tables(S):
    tq = ATT_TILE
    n_cmp = (S - CMP_BLOCK) // CMP_STRIDE + 1
    n_pad = S // CMP_STRIDE
    n_slc = S // SEL_BLOCK
    t = np.arange(S)[:, None]
    n = np.arange(n_pad)[None, :]
    bucket_c = _t5_bucket_np(t - (n * CMP_STRIDE + CMP_BLOCK - 1)).reshape(1, -1)
    d = (np.arange(4)[:, None, None] * tq + np.arange(tq)[None, :, None]
         - np.arange(2 * tq)[None, None, :])
    bucket_w = _t5_bucket_np(d).reshape(1, -1)
    cs = np.arange(n_pad)[None, :] * CMP_STRIDE
    sj = np.arange(n_slc)[:, None] * SEL_BLOCK
    ovl_t = ((cs < sj + SEL_BLOCK) & (cs + CMP_BLOCK > sj) & (np.arange(n_pad)[None, :] < n_cmp))
    return bucket_c, bucket_w, ovl_t.astype(np.float32)


def kernel(x, c, w_ada, b_ada, w_in, b_f, cmp_pos_k, cmp_w1_k, cmp_w2_k, cmp_pos_v, cmp_w1_v,
           cmp_w2_v, rel_bias, w_out, ln1_g, ln1_b, w_router, e_bias, w_gate, w_up, w_down,
           ws_gate, ws_up, ws_down, ln2_g, ln2_b):
    B, S, D = x.shape
    depth = w_ada.shape[0]
    alpha = (2 * depth) ** 0.25
    tq = ATT_TILE
    assert w_in.shape[-1] == 3 * FOX_HEADS * HEAD_DIM + FOX_HEADS + NSA_HEADS * HEAD_DIM \
        + 6 * NSA_GROUPS * HEAD_DIM + 3 * NSA_HEADS
    assert NSA_GROUPS * HEAD_DIM == LANES and S // CMP_STRIDE == LANES
    assert WINDOW == 2 * tq and S // SEL_BLOCK <= HEAD_DIM // 2
    assert S % FOX_TILE == 0 and S % CMP_TILE == 0 and (B * S) % SORT_TILE == 0
    assert w_gate.shape[1:] == (N_EXPERTS, D, w_down.shape[2])
    bucket_c, bucket_w, ovl_t = _static_tables(S)
    rel_bias_t = rel_bias.T * LOG2E
    bias_c = _bias_table(jnp.asarray(bucket_c), rel_bias_t).reshape(NSA_HEADS, S, S // CMP_STRIDE)
    w4 = _bias_table(jnp.asarray(bucket_w), rel_bias_t).reshape(NSA_HEADS, 4, tq, 2 * tq)
    ovl_t = jnp.asarray(ovl_t, BF16)

    for l in range(depth):
        mod = _ada(c, w_ada[l], b_ada[l]).reshape(B, 6, D)
        bf_row = jnp.zeros((1, LANES), F32).at[0, :FOX_HEADS].set(b_f[l])
        (fq, fk, nq, kc, kw, vc, ks, fv, vs, vw, misc, misc_t) = _in_proj(
            x, mod, _rearrange_w_in(w_in[l]), bf_row)

        o_fox = _fox(fq, fk, fv, misc_t[:, :FOX_HEADS, :])

        pk, wak, wbk, w2k = _compress_weights(cmp_pos_k[l], cmp_w1_k[l], cmp_w2_k[l])
        pv, wav, wbv, w2v = _compress_weights(cmp_pos_v[l], cmp_w1_v[l], cmp_w2_v[l])
        rows = S // CMP_STRIDE
        kcmp, vcmp = _compress(kc.reshape(B, rows, CMP_STRIDE * LANES),
                               vc.reshape(B, rows, CMP_STRIDE * LANES),
                               pk, pv, wak, wbk, wav, wbv, w2k, w2v)

        oc, sel = _cmp_sel(nq, kcmp, vcmp, bias_c, misc, ovl_t)
        o_nsa = _nsa(nq, ks, vs, kw, vw, sel, w4, misc, oc)

        x1, h2, gates, gates_t = _out_proj(
            alpha, o_fox, o_nsa, x, mod, w_out[l].astype(BF16), ln1_g[l].reshape(1, D),
            ln1_b[l].reshape(1, D), w_router[l].T, e_bias[l].reshape(N_EXPERTS, 1))

        sgu = jnp.concatenate([ws_gate[l], ws_up[l]], axis=-1).astype(BF16)
        x = _moe(alpha, h2, x1, gates, gates_t, mod, w_gate[l], w_up[l], w_down[l], sgu,
                 ws_down[l].astype(BF16), ln2_g[l].reshape(1, D), ln2_b[l].reshape(1, D))
    return x
```

```python
import functools
import math

import jax
import jax.numpy as jnp
import numpy as np
from jax import lax
from jax.experimental import pallas as pl
from jax.experimental.pallas import tpu as pltpu

F32 = jnp.float32
BF16 = jnp.bfloat16

HEAD_DIM = 64
FOX_HEADS = 8
NSA_HEADS = 8
NSA_GQA = 4
NSA_GROUPS = NSA_HEADS // NSA_GQA
CMP_BLOCK = 32
CMP_STRIDE = 16
CMP_HIDDEN = 256
SEL_BLOCK = 64
N_SEL = 16
WINDOW = 512
N_BUCKETS = 32
MAX_DISTANCE = 128
N_EXPERTS = 64
N_EXPERT_GROUPS = 8
GROUP_SIZE = N_EXPERTS // N_EXPERT_GROUPS
TOPK_GROUPS = 4
TOP_K = 8
ROUTED_SCALE = 2.5
LN_EPS = 1e-5
NEG_BIG = -1e30
FORCE_SCORE = 1e4

LANES = 128
ATT_TILE = 256
FOX_TILE = 512
CMP_TILE = 512
ROW_TILE = 512
VMEM_LIMIT = 48 * 1024 * 1024

NT_DIMS = (((1,), (1,)), ((), ()))


def _dot(a, b):
    return jnp.dot(a, b, preferred_element_type=F32)


def _dot_nt(a, b):
    return lax.dot_general(a, b, NT_DIMS, preferred_element_type=F32)


def _split3(x):
    hi = x.astype(BF16)
    r1 = x - hi.astype(F32)
    mid = r1.astype(BF16)
    lo = (r1 - mid.astype(F32)).astype(BF16)
    return hi, mid, lo


def _silu(x):
    return x / (1.0 + jnp.exp(-x))


def _sigmoid(x):
    return 1.0 / (1.0 + jnp.exp(-x))


def _swap_halves(x):
    return pltpu.roll(x, HEAD_DIM, 1)


def _t5_bucket_np(dist):
    n = np.maximum(dist, 0)
    max_exact = N_BUCKETS // 2
    nf = np.maximum(n, 1).astype(np.float32)
    large = max_exact + (np.log(nf / max_exact) / math.log(MAX_DISTANCE / max_exact)
                         * (N_BUCKETS - max_exact)).astype(np.int32)
    large = np.minimum(large, N_BUCKETS - 1)
    return np.where(n < max_exact, n, large).astype(np.int32)


def _ada_kernel(c_ref, w_ref, b_ref, o_ref):
    c = c_ref[...]
    o_ref[...] = jnp.dot(_silu(c), w_ref[...], preferred_element_type=F32,
                         precision=lax.Precision.HIGHEST) + b_ref[...]


def _ada(c, w_ada, b_ada):
    B, D = c.shape
    n_out = w_ada.shape[1]
    tn = 1024
    return pl.pallas_call(
        _ada_kernel,
        grid=(n_out // tn,),
        in_specs=[pl.BlockSpec((B, D), lambda j: (0, 0)),
                  pl.BlockSpec((D, tn), lambda j: (0, j)),
                  pl.BlockSpec((1, tn), lambda j: (0, j))],
        out_specs=pl.BlockSpec((B, tn), lambda j: (0, j)),
        out_shape=jax.ShapeDtypeStruct((B, n_out), F32),
        compiler_params=pltpu.CompilerParams(dimension_semantics=("arbitrary",),
                                             vmem_limit_bytes=VMEM_LIMIT),
        name="ada",
    )(c, w_ada, b_ada.reshape(1, n_out))


def _bias_table_kernel(bkt_ref, rbt_ref, o_ref):
    bkt = bkt_ref[...]
    k = lax.broadcasted_iota(jnp.int32, (N_BUCKETS, bkt.shape[1]), 0)
    onehot = jnp.where(k == bkt, 1.0, 0.0).astype(BF16)
    hi, mid, lo = _split3(rbt_ref[...])
    o_ref[...] = _dot(hi, onehot) + _dot(mid, onehot) + _dot(lo, onehot)


def _bias_table(bucket, rel_bias_t):
    n = bucket.shape[1]
    chunk = 32768
    n_heads = rel_bias_t.shape[0]
    return pl.pallas_call(
        _bias_table_kernel,
        grid=(n // chunk,),
        in_specs=[pl.BlockSpec((1, chunk), lambda j: (0, j)),
                  pl.BlockSpec(rel_bias_t.shape, lambda j: (0, 0))],
        out_specs=pl.BlockSpec((n_heads, chunk), lambda j: (0, j)),
        out_shape=jax.ShapeDtypeStruct((n_heads, n), F32),
        compiler_params=pltpu.CompilerParams(dimension_semantics=("parallel",),
                                             vmem_limit_bytes=VMEM_LIMIT),
        name="bias_table",
    )(bucket, rel_bias_t)


_C_FQ, _C_FK, _C_NQ = 0, 512, 1024
_C_K3 = 1536
_C_SK = 1920
_C_FV = 2048
_C_SV = 2560
_C_WV = 2688
_C_MISC = 2816
_IN_COLS = 2944
LOG2E = math.log2(math.e)


def _in_proj_kernel(x_ref, mod_ref, w_ref, bf_ref, fq_ref, fk_ref, nq_ref, kc_ref, kw_ref,
                    vc_ref, ks_ref, fv_ref, vs_ref, vw_ref, misc_ref, misct_ref, carry_ref):
    s_idx = pl.program_id(1)
    tm = x_ref.shape[0]
    mod = mod_ref[...]
    h = (x_ref[...] * (1.0 + mod[1:2, :]) + mod[0:1, :]).astype(BF16)

    for ref, c0 in ((fq_ref, _C_FQ), (fk_ref, _C_FK)):
        ref[...] = _dot(h, w_ref[:, c0:c0 + 512]).astype(ref.dtype)
    nsa_q = _dot(h, w_ref[:, _C_NQ:_C_K3])
    keys = _dot(h, w_ref[:, _C_K3:_C_FV])
    fox_v = _dot(h, w_ref[:, _C_FV:_C_SV])
    tail = _dot(h, w_ref[:, _C_SV:_IN_COLS])
    for k, ref in enumerate((kc_ref, kw_ref, vc_ref)):
        ref[...] = keys[:, k * LANES:(k + 1) * LANES].astype(ref.dtype)

    lane = lax.broadcasted_iota(jnp.int32, (tm, LANES), 1)
    low = lane < HEAD_DIM

    def spread(ref, cols, fill):
        for p in range(cols.shape[-1] // LANES):
            r = cols[:, p * LANES:(p + 1) * LANES]
            ref[:, 2 * p * LANES:(2 * p + 1) * LANES] = jnp.where(low, r, fill).astype(ref.dtype)
            ref[:, (2 * p + 1) * LANES:(2 * p + 2) * LANES] = jnp.where(
                low, _swap_halves(r), fill).astype(ref.dtype)

    for p in range(nsa_q.shape[-1] // LANES):
        r = nsa_q[:, p * LANES:(p + 1) * LANES]
        swapped = _swap_halves(r)
        nq_ref[:, 2 * p * LANES:(2 * p + 1) * LANES] = jnp.where(low, r, swapped).astype(
            nq_ref.dtype)
        nq_ref[:, (2 * p + 1) * LANES:(2 * p + 2) * LANES] = jnp.where(low, swapped, r).astype(
            nq_ref.dtype)

    key_blk = lax.shift_right_logical(
        s_idx * tm + lax.broadcasted_iota(jnp.int32, (tm, LANES), 0), int(math.log2(SEL_BLOCK)))
    spread(ks_ref, keys[:, 3 * LANES:4 * LANES],
           jnp.where(lane == HEAD_DIM + key_blk, 1.0, 0.0))
    spread(fv_ref, fox_v, 1.0)
    spread(vs_ref, tail[:, 0:LANES], 1.0)
    spread(vw_ref, tail[:, LANES:2 * LANES], 1.0)

    z = tail[:, 2 * LANES:3 * LANES] + bf_ref[...]
    is_f = lane < FOX_HEADS
    log_f = jnp.minimum(z, 0.0) - jnp.log(1.0 + jnp.exp(-jnp.abs(z)))
    log_f = jnp.where(is_f, log_f, 0.0)

    row = lax.broadcasted_iota(jnp.int32, (tm, tm), 0)
    col = lax.broadcasted_iota(jnp.int32, (tm, tm), 1)
    tri = jnp.where(row >= col, 1.0, 0.0).astype(BF16)
    sums = _dot(tri, jnp.concatenate(_split3(log_f), axis=1))
    local = sums[:, 0:LANES] + sums[:, LANES:2 * LANES] + sums[:, 2 * LANES:3 * LANES]

    @pl.when(s_idx == 0)
    def _():
        carry_ref[...] = jnp.zeros_like(carry_ref)

    cum = local + carry_ref[...]
    carry_ref[...] = cum[tm - 1:tm, :]
    misc = jnp.where(is_f, cum * LOG2E, _sigmoid(z))
    misc_ref[...] = misc
    misct_ref[...] = misc.T


def _in_proj(x, mod, w_r, bf_row):
    B, S, D = x.shape
    tm = ROW_TILE
    widths = (512, 512, 2 * (_C_K3 - _C_NQ), LANES, LANES, LANES, 2 * (_C_FV - _C_SK),
              2 * (_C_SV - _C_FV),
              2 * (_C_WV - _C_SV), 2 * (_C_MISC - _C_WV))
    wide = lambda w: pl.BlockSpec((None, tm, w), lambda b, s: (b, s, 0))
    out_shape = ([jax.ShapeDtypeStruct((B, S, w), BF16) for w in widths]
                 + [jax.ShapeDtypeStruct((B, S, LANES), F32),
                    jax.ShapeDtypeStruct((B, LANES, S), F32)])
    out_specs = ([wide(w) for w in widths]
                 + [wide(LANES), pl.BlockSpec((None, LANES, tm), lambda b, s: (b, 0, s))])
    return pl.pallas_call(
        _in_proj_kernel,
        grid=(B, S // tm),
        in_specs=[pl.BlockSpec((None, tm, D), lambda b, s: (b, s, 0)),
                  pl.BlockSpec((None, 6, D), lambda b, s: (b, 0, 0)),
                  pl.BlockSpec((D, _IN_COLS), lambda b, s: (0, 0)),
                  pl.BlockSpec((1, LANES), lambda b, s: (0, 0))],
        out_specs=out_specs,
        out_shape=out_shape,
        scratch_shapes=[pltpu.VMEM((1, LANES), F32)],
        compiler_params=pltpu.CompilerParams(dimension_semantics=("parallel", "arbitrary"),
                                             vmem_limit_bytes=VMEM_LIMIT),
        name="in_proj",
    )(x, mod, w_r, bf_row)


def _softmax_weights(s, m):
    return jnp.exp2((s - m).astype(BF16))


def _flash_update(carry, s, vt):
    m, acc = carry
    m_new = jnp.maximum(m, jnp.max(s, axis=-1, keepdims=True))
    alpha = jnp.exp2(m - m_new)
    p = _softmax_weights(s, m_new)
    rows = acc.shape[0]
    acc = alpha.reshape(rows, 1) * acc + _dot(p.reshape(rows, s.shape[-1]), vt)
    return m_new, acc


def _normalize(acc):
    return acc / _swap_halves(acc)


def _fox_kernel(q_ref, k_ref, v_ref, ck_ref, o_ref):
    i = pl.program_id(2)
    tq = q_ref.shape[0]
    tk = ck_ref.shape[2]
    q2 = q_ref[...].astype(F32)
    lane = lax.broadcasted_iota(jnp.int32, (tq, LANES), 1)
    low = lane < HEAD_DIM
    halves = (low, jnp.logical_not(low))
    qh = [jnp.where(h, q2, 0.0).astype(BF16) for h in halves]
    col_minus_row = (lax.broadcasted_iota(jnp.int32, (tq, tk), 1)
                     - lax.broadcasted_iota(jnp.int32, (tq, tk), 0))

    def step(jj, carry, diagonal):
        k0 = jj * tk
        kt = k_ref[pl.ds(k0, tk), :]
        new = []
        for hh in range(2):
            s = _dot_nt(qh[hh], kt) - ck_ref[hh, pl.ds(jj, 1), :]
            if diagonal:
                s = jnp.where(col_minus_row <= 0, s, NEG_BIG)
            vt = v_ref[pl.ds(k0, tk), hh * LANES:(hh + 1) * LANES]
            new.append(_flash_update(carry[hh], s, vt))
        return tuple(new)

    init = tuple((jnp.full((tq, 1), NEG_BIG, F32), jnp.zeros((tq, LANES), F32))
                 for _ in range(2))

    assert tq == tk
    for n_full in range(k_ref.shape[0] // tk):
        @pl.when(i == n_full)
        def _(n_full=n_full):
            carry = init
            for jj in range(n_full):
                carry = step(jj, carry, False)
            carry = step(n_full, carry, True)
            o_ref[...] = jnp.where(low, _normalize(carry[0][1]),
                                   _swap_halves(_normalize(carry[1][1]))).astype(o_ref.dtype)


def _fox(fq, fk, fv, cum_row):
    B, S, W = fq.shape
    tq = tk = FOX_TILE
    n_pairs = W // LANES
    cum_row = cum_row.reshape(B, n_pairs, 2, S // tk, tk)
    return pl.pallas_call(
        _fox_kernel,
        grid=(B, n_pairs, S // tq),
        in_specs=[pl.BlockSpec((None, tq, LANES), lambda b, p, i: (b, i, p)),
                  pl.BlockSpec((None, S, LANES), lambda b, p, i: (b, 0, p)),
                  pl.BlockSpec((None, S, 2 * LANES), lambda b, p, i: (b, 0, p)),
                  pl.BlockSpec((None, None, 2, S // tk, tk), lambda b, p, i: (b, p, 0, 0, 0))],
        out_specs=pl.BlockSpec((None, tq, LANES), lambda b, p, i: (b, i, p)),
        out_shape=jax.ShapeDtypeStruct((B, S, W), BF16),
        compiler_params=pltpu.CompilerParams(
            dimension_semantics=("parallel", "parallel", "arbitrary"),
            vmem_limit_bytes=VMEM_LIMIT),
        name="fox",
    )(fq, fk, fv, cum_row)


def _compress_kernel(xk_ref, xv_ref, pk_ref, pv_ref, wak_ref, wbk_ref, wav_ref, wbv_ref,
                     w2k_ref, w2v_ref, ok_ref, ov_ref):
    n_rows = xk_ref.shape[0]
    for x_ref, p_ref, wa_ref, wb_ref, w2_ref, o_ref in (
            (xk_ref, pk_ref, wak_ref, wbk_ref, w2k_ref, ok_ref),
            (xv_ref, pv_ref, wav_ref, wbv_ref, w2v_ref, ov_ref)):
        x = x_ref[...].astype(F32)
        xa = (x + p_ref[0:1, :]).astype(BF16)
        xb = (x + p_ref[1:2, :]).astype(BF16)
        hb = _dot(xb, wb_ref[...])
        h1 = _dot(xa, wa_ref[...]) + pltpu.roll(hb, n_rows - 1, 0)
        o_ref[...] = _dot(_silu(h1).astype(BF16), w2_ref[...]).astype(o_ref.dtype)


def _compress(xk, xv, pk, pv, wak, wbk, wav, wbv, w2k, w2v):
    B, R, C = xk.shape
    xspec = pl.BlockSpec((None, R, C), lambda b: (b, 0, 0))
    full = lambda a: pl.BlockSpec(a.shape, lambda b: (0,) * a.ndim)
    ospec = pl.BlockSpec((None, R, LANES), lambda b: (b, 0, 0))
    return pl.pallas_call(
        _compress_kernel,
        grid=(B,),
        in_specs=[xspec, xspec] + [full(a) for a in (pk, pv, wak, wbk, wav, wbv, w2k, w2v)],
        out_specs=[ospec, ospec],
        out_shape=[jax.ShapeDtypeStruct((B, R, LANES), BF16)] * 2,
        compiler_params=pltpu.CompilerParams(dimension_semantics=("parallel",),
                                             vmem_limit_bytes=VMEM_LIMIT),
        name="compress",
    )(xk, xv, pk, pv, wak, wbk, wav, wbv, w2k, w2v)


def _rank_rows(score):
    n = score.shape[0]
    j = lax.broadcasted_iota(jnp.int32, score.shape, 0)
    rank = jnp.zeros(score.shape, jnp.int32)
    for i in range(n):
        si = score[i:i + 1, :]
        beats = (si > score) | ((si == score) & (j > i))
        rank = rank + jnp.where(beats, 1, 0)
    return rank


def _dup_head(q_ref, r):
    return q_ref[:, r * LANES:(r + 1) * LANES].astype(F32)


def _pack_heads(o_list, g):
    lane = lax.broadcasted_iota(jnp.int32, o_list[0].shape, 1)
    in_g = (lane >= g * HEAD_DIM) & (lane < (g + 1) * HEAD_DIM)
    both = []
    for o in o_list:
        om = jnp.where(in_g, o, 0.0)
        both.append(om + _swap_halves(om))
    pairs = [jnp.where(lane < HEAD_DIM, both[2 * p], both[2 * p + 1]) for p in range(2)]
    return jnp.concatenate(pairs, axis=1)


def _group_gates(misc, g):
    w = 3 * NSA_GQA
    gates = misc[:, FOX_HEADS:FOX_HEADS + w]
    for other in range(1, NSA_GROUPS):
        gates = jnp.where(g == other, misc[:, FOX_HEADS + other * w:FOX_HEADS + (other + 1) * w],
                          gates)
    return gates


def _cmp_sel_kernel(q_ref, kc_ref, vc_ref, bias_ref, gate_ref, ovl_ref, oc_ref, sel_ref):
    g = pl.program_id(1)
    i = pl.program_id(2)
    tq = q_ref.shape[0]
    n_pad = kc_ref.shape[0]
    lane = lax.broadcasted_iota(jnp.int32, (n_pad, LANES), 1)
    in_g = (lane >= g * HEAD_DIM) & (lane < (g + 1) * HEAD_DIM)
    kc = jnp.where(in_g, kc_ref[...].astype(F32), 0.0).astype(BF16)
    vc = vc_ref[...]
    t = i * tq + lax.broadcasted_iota(jnp.int32, (tq, n_pad), 0)
    n = lax.broadcasted_iota(jnp.int32, (tq, n_pad), 1)
    valid = t >= n * CMP_STRIDE + (CMP_BLOCK - 1)
    gates = _group_gates(gate_ref[...], g)
    p_sum = jnp.zeros((tq, n_pad), F32)
    outs = []
    for r in range(NSA_GQA):
        qr = q_ref[:, r * LANES:(r + 1) * LANES]
        s = _dot_nt(qr, kc)
        s = jnp.where(valid, s + bias_ref[r], NEG_BIG)
        m = jnp.max(s, axis=-1, keepdims=True)
        p = jnp.exp2(s - m)
        p = p / jnp.sum(p, axis=-1, keepdims=True)
        p = jnp.where(valid, p, 0.0)
        p_sum = p_sum + p
        outs.append(_dot(p.astype(BF16), vc) * gates[:, 3 * r:3 * r + 1])
    oc_ref[...] = _pack_heads(outs, g).astype(oc_ref.dtype)

    ovl = ovl_ref[...]
    hi, mid, lo = _split3(p_sum)
    imp = _dot_nt(ovl, hi) + _dot_nt(ovl, mid) + _dot_nt(ovl, lo)
    n_blk = imp.shape[0]
    j = lax.broadcasted_iota(jnp.int32, (n_blk, tq), 0)
    qb = jnp.right_shift(i * tq + lax.broadcasted_iota(jnp.int32, (n_blk, tq), 1),
                         int(math.log2(SEL_BLOCK)))
    forced = (j == 0) | (j == qb) | (j == qb - 1)
    causal = j <= qb
    score = jnp.where(causal, imp + jnp.where(forced, FORCE_SCORE, 0.0), -FORCE_SCORE)
    chosen = (_rank_rows(score) < N_SEL) & causal
    sel = jnp.where(chosen, 1.0, 0.0)
    sel = jnp.concatenate([sel, jnp.zeros((LANES - n_blk, tq), F32)], axis=0)
    sel_ref[...] = sel.T


def _cmp_sel(nq, kcmp, vcmp, bias_c, gates_g, ovl_t):
    B, S, _ = nq.shape
    tq = CMP_TILE
    n_pad = kcmp.shape[1]
    return pl.pallas_call(
        _cmp_sel_kernel,
        grid=(B, NSA_GROUPS, S // tq),
        in_specs=[pl.BlockSpec((None, tq, NSA_GQA * LANES), lambda b, g, i: (b, i, g)),
                  pl.BlockSpec((None, n_pad, LANES), lambda b, g, i: (b, 0, 0)),
                  pl.BlockSpec((None, n_pad, LANES), lambda b, g, i: (b, 0, 0)),
                  pl.BlockSpec((NSA_GQA, tq, n_pad), lambda b, g, i: (g, i, 0)),
                  pl.BlockSpec((None, tq, LANES), lambda b, g, i: (b, i, 0)),
                  pl.BlockSpec(ovl_t.shape, lambda b, g, i: (0, 0))],
        out_specs=[pl.BlockSpec((None, tq, 2 * LANES), lambda b, g, i: (b, i, g)),
                   pl.BlockSpec((None, None, tq, LANES), lambda b, g, i: (b, g, i, 0))],
        out_shape=[jax.ShapeDtypeStruct((B, S, NSA_HEADS * HEAD_DIM), BF16),
                   jax.ShapeDtypeStruct((B, NSA_GROUPS, S, LANES), F32)],
        compiler_params=pltpu.CompilerParams(
            dimension_semantics=("parallel", "parallel", "arbitrary"),
            vmem_limit_bytes=VMEM_LIMIT),
        name="cmp_sel",
    )(nq, kcmp, vcmp, bias_c, gates_g, ovl_t)


def _nsa_kernel(q_ref, ks_ref, vs_ref, kw_ref, vw_ref, sel_ref, w4_ref, gate_ref, oc_ref,
                o_ref):
    for tile in range(ks_ref.shape[0] // q_ref.shape[0]):
        @pl.when(pl.program_id(2) == tile)
        def _(tile=tile):
            _nsa_tile(tile, q_ref, ks_ref, vs_ref, kw_ref, vw_ref, sel_ref, w4_ref, gate_ref,
                      oc_ref, o_ref)


def _nsa_tile(i, q_ref, ks_ref, vs_ref, kw_ref, vw_ref, sel_ref, w4_ref, gate_ref, oc_ref, o_ref):
    g = pl.program_id(0)
    tq = q_ref.shape[0]
    tk = w4_ref.shape[-1]
    H = NSA_GQA
    lane = lax.broadcasted_iota(jnp.int32, (tq, LANES), 1)
    in_g = (lane >= g * HEAD_DIM) & (lane < (g + 1) * HEAD_DIM)
    heads = [_dup_head(q_ref, r) for r in range(H)]
    qs = jnp.concatenate([jnp.where(in_g, hd, 0.0).astype(BF16) for hd in heads], axis=0)

    blk_mask = _swap_halves((sel_ref[...] - 1.0) * (-NEG_BIG))
    qsel = jnp.concatenate(
        [jnp.where(lane < HEAD_DIM, hd, blk_mask).astype(BF16) for hd in heads], axis=0)

    def sel_step(jj, carry, diagonal):
        k0 = jj * tk
        ahead = i * tq - k0
        width = min(ahead + tq, tk) if diagonal else tk
        rows = pl.ds(k0, width)
        s = (_dot_nt(qsel, ks_ref[rows, :]).reshape(H, tq, width)
             + w4_ref[:, min(i - 2 * jj, 3), :, 0:width])
        if diagonal:
            cmr = (lax.broadcasted_iota(jnp.int32, (tq, width), 1)
                   - lax.broadcasted_iota(jnp.int32, (tq, width), 0))
            s = jnp.where((cmr <= ahead)[None], s, NEG_BIG)
        return _flash_update(carry, s, vs_ref[rows, :])

    carry = (jnp.full((H, tq, 1), NEG_BIG, F32), jnp.zeros((H * tq, LANES), F32))
    n_full = (i * tq) // tk
    for jj in range(n_full):
        carry = sel_step(jj, carry, False)
    _, acc_s = sel_step(n_full, carry, True)

    old = [max(i - d, 0) * tq for d in (2, 1)]
    now = i * tq
    k_old = jnp.concatenate([kw_ref[pl.ds(st, tq), :] for st in old], axis=0)
    v_old = jnp.concatenate([vw_ref[pl.ds(st, tq), :] for st in old], axis=0)
    col = lax.broadcasted_iota(jnp.int32, (tq, 2 * tq), 1)
    cmr = col - lax.broadcasted_iota(jnp.int32, (tq, 2 * tq), 0)
    never = 4 * tq
    ok_old = (((col < tq) & (cmr > (0 if i >= 2 else never)))
              | (col >= (tq if i >= 1 else never)))
    s_old = jnp.where(ok_old[None], _dot_nt(qs, k_old).reshape(H, tq, 2 * tq) + w4_ref[:, 2],
                      NEG_BIG)
    s_now = jnp.where((cmr[:, 0:tq] <= 0)[None],
                      _dot_nt(qs, kw_ref[pl.ds(now, tq), :]).reshape(H, tq, tq)
                      + w4_ref[:, 0, :, 0:tq], NEG_BIG)
    m_w = jnp.maximum(jnp.max(s_old, axis=-1, keepdims=True),
                      jnp.max(s_now, axis=-1, keepdims=True))
    acc_w = (_dot(_softmax_weights(s_old, m_w).reshape(H * tq, 2 * tq), v_old)
             + _dot(_softmax_weights(s_now, m_w).reshape(H * tq, tq), vw_ref[pl.ds(now, tq), :]))

    gates = _group_gates(gate_ref[...], g)
    o_s = _normalize(acc_s)
    o_w = _normalize(acc_w)
    outs = []
    for r in range(H):
        sl = slice(r * tq, (r + 1) * tq)
        outs.append(o_s[sl] * gates[:, 3 * r + 1:3 * r + 2] + o_w[sl] * gates[:, 3 * r + 2:3 * r + 3])
    lane = lax.broadcasted_iota(jnp.int32, (tq, LANES), 1)
    pairs = [jnp.where(lane < HEAD_DIM, outs[2 * p], _swap_halves(outs[2 * p + 1]))
             for p in range(H // 2)]
    o_ref[...] = (jnp.concatenate(pairs, axis=1) + oc_ref[...].astype(F32)).astype(o_ref.dtype)


def _nsa(nq, ks, vs, kw, vw, sel, w4, misc, oc):
    B, S, _ = nq.shape
    tq = ATT_TILE
    tk = w4.shape[-1]
    both = lambda: pl.BlockSpec((None, S, LANES), lambda g, b, i: (b, 0, 0))
    mine = lambda: pl.BlockSpec((None, S, LANES), lambda g, b, i: (b, 0, g))
    return pl.pallas_call(
        _nsa_kernel,
        grid=(NSA_GROUPS, B, S // tq),
        in_specs=[pl.BlockSpec((None, tq, NSA_GQA * LANES), lambda g, b, i: (b, i, g)),
                  mine(), mine(), both(), mine(),
                  pl.BlockSpec((None, None, tq, LANES), lambda g, b, i: (b, g, i, 0)),
                  pl.BlockSpec((NSA_GQA, 4, tq, tk), lambda g, b, i: (g, 0, 0, 0)),
                  pl.BlockSpec((None, tq, LANES), lambda g, b, i: (b, i, 0)),
                  pl.BlockSpec((None, tq, 2 * LANES), lambda g, b, i: (b, i, g))],
        out_specs=pl.BlockSpec((None, tq, 2 * LANES), lambda g, b, i: (b, i, g)),
        out_shape=jax.ShapeDtypeStruct((B, S, NSA_HEADS * HEAD_DIM), BF16),
        compiler_params=pltpu.CompilerParams(
            dimension_semantics=("parallel", "parallel", "arbitrary"),
            vmem_limit_bytes=VMEM_LIMIT),
        name="nsa",
    )(nq, ks, vs, kw, vw, sel, w4, misc, oc)


def _layer_norm(y, g, b):
    mu = jnp.mean(y, axis=-1, keepdims=True)
    yc = y - mu
    var = jnp.mean(yc * yc, axis=-1, keepdims=True)
    return yc * lax.rsqrt(var + LN_EPS) * g + b


def _top_rows(score, k):
    n = score.shape[0]
    idx = lax.broadcasted_iota(jnp.int32, score.shape, 0).astype(F32)
    alive = jnp.ones(score.shape, F32)
    for _ in range(k):
        live = alive > 0.0
        best = jnp.max(jnp.where(live, score, -jnp.inf), axis=0, keepdims=True)
        first = jnp.min(jnp.where(live & (score == best), idx, float(n)), axis=0, keepdims=True)
        alive = jnp.where(idx == first, 0.0, alive)
    return alive == 0.0


def _router_gates_t(h2, wr_t, eb_col):
    tm = h2.shape[0]
    h_hi, h_lo, _ = _split3(h2)
    w_hi, w_lo, _ = _split3(wr_t)
    logit = _dot_nt(w_hi, h_hi) + _dot_nt(w_hi, h_lo) + _dot_nt(w_lo, h_hi)
    scores = _sigmoid(logit)
    biased = scores + eb_col
    e_in = lax.broadcasted_iota(jnp.int32, (GROUP_SIZE, tm), 0).astype(F32)
    gs_rows = []
    for gi in range(N_EXPERT_GROUPS):
        grp = biased[gi * GROUP_SIZE:(gi + 1) * GROUP_SIZE, :]
        m1 = jnp.max(grp, axis=0, keepdims=True)
        first = jnp.min(jnp.where(grp == m1, e_in, float(GROUP_SIZE)), axis=0, keepdims=True)
        m2 = jnp.max(jnp.where(e_in == first, -jnp.inf, grp), axis=0, keepdims=True)
        gs_rows.append(m1 + m2)
    gscore = jnp.concatenate(gs_rows, axis=0)
    g_keep = _rank_rows(gscore) < TOPK_GROUPS
    keep = jnp.concatenate(
        [jnp.broadcast_to(g_keep[gi:gi + 1, :], (GROUP_SIZE, tm)) for gi in range(N_EXPERT_GROUPS)],
        axis=0)
    masked = jnp.where(keep, biased, -jnp.inf)
    chosen = _top_rows(masked, TOP_K)
    w = jnp.where(chosen, scores, 0.0)
    return w / jnp.sum(w, axis=0, keepdims=True) * ROUTED_SCALE


def _out_proj_kernel(alpha, of_ref, on_ref, x_ref, mod_ref, w_ref, lg_ref, lb_ref, wr_ref,
                     eb_ref, x1_ref, h2_ref, gate_ref, gate_t_ref):
    half = of_ref.shape[1]
    mod = mod_ref[...]
    mixed = _dot(of_ref[...], w_ref[0:half, :]) + _dot(on_ref[...], w_ref[half:2 * half, :])
    y = alpha * x_ref[...] + mod[2:3, :] * mixed
    x1 = _layer_norm(y, lg_ref[...], lb_ref[...])
    x1_ref[...] = x1
    h2 = x1 * (1.0 + mod[4:5, :]) + mod[3:4, :]
    h2_ref[...] = h2.astype(h2_ref.dtype)
    gates_t = _router_gates_t(h2, wr_ref[...], eb_ref[...])
    gate_t_ref[...] = gates_t
    tm = h2.shape[0]
    gates_t = jnp.concatenate([gates_t, jnp.zeros((LANES - N_EXPERTS, tm), F32)], axis=0)
    gate_ref[...] = gates_t.T


def _out_proj(alpha, o_fox, o_nsa, x, mod, w_out, ln_g, ln_b, wr_t, eb_col):
    B, S, D = x.shape
    tm = ROW_TILE
    half = o_fox.shape[-1]
    row = lambda a: pl.BlockSpec(a.shape, lambda b, s: (0, 0))
    return pl.pallas_call(
        functools.partial(_out_proj_kernel, alpha),
        grid=(B, S // tm),
        in_specs=[pl.BlockSpec((None, tm, half), lambda b, s: (b, s, 0)),
                  pl.BlockSpec((None, tm, half), lambda b, s: (b, s, 0)),
                  pl.BlockSpec((None, tm, D), lambda b, s: (b, s, 0)),
                  pl.BlockSpec((None, 6, D), lambda b, s: (b, 0, 0)),
                  row(w_out), row(ln_g), row(ln_b), row(wr_t), row(eb_col)],
        out_specs=[pl.BlockSpec((None, tm, D), lambda b, s: (b, s, 0)),
                   pl.BlockSpec((None, tm, D), lambda b, s: (b, s, 0)),
                   pl.BlockSpec((None, tm, LANES), lambda b, s: (b, s, 0)),
                   pl.BlockSpec((N_EXPERTS, tm), lambda b, s: (0, b * (S // tm) + s))],
        out_shape=[jax.ShapeDtypeStruct((B, S, D), F32),
                   jax.ShapeDtypeStruct((B, S, D), BF16),
                   jax.ShapeDtypeStruct((B, S, LANES), F32),
                   jax.ShapeDtypeStruct((N_EXPERTS, B * S), F32)],
        compiler_params=pltpu.CompilerParams(dimension_semantics=("parallel", "parallel"),
                                             vmem_limit_bytes=VMEM_LIMIT),
        name="out_proj",
    )(o_fox, o_nsa, x, mod, w_out, ln_g, ln_b, wr_t, eb_col)


SORT_TILE = 256
ROW_ALIGN = 16
EXP_TILE = 1024
EXP_STRIP = 1024
P_CHUNK = 256


def _strict_upper(n):
    return jnp.where(lax.broadcasted_iota(jnp.int32, (n, n), 0)
                     < lax.broadcasted_iota(jnp.int32, (n, n), 1), 1.0, 0.0).astype(BF16)


def _strict_lower(n):
    return jnp.where(lax.broadcasted_iota(jnp.int32, (n, n), 1)
                     < lax.broadcasted_iota(jnp.int32, (n, n), 0), 1.0, 0.0).astype(BF16)


def _local_rows_bound(ts):
    rows = TOP_K * ts + N_EXPERTS * (ROW_ALIGN - 1)
    return -(-rows // P_CHUNK) * P_CHUNK


def _piece_cols(ts):
    return -(-(_local_rows_bound(ts) // ROW_ALIGN) // LANES) * LANES


def _sorted_tiles_bound(T):
    rows = TOP_K * T + (T // SORT_TILE) * N_EXPERTS * (ROW_ALIGN - 1)
    return -(-rows // EXP_TILE) + N_EXPERTS


def _sorted_tiles_expected(T):
    groups = (T // SORT_TILE) * N_EXPERTS
    rows = TOP_K * T + groups * ((ROW_ALIGN - 1) / 2 + 2)
    return int(-(-rows // EXP_TILE) + math.ceil(0.65 * N_EXPERTS))


def _moe_meta_kernel(gt_ref, ptab_ref, loc_et_ref, np_et_ref, loc_te_ref, np_te_ref, tot_ref,
                     erow_ref, texp_ref, nused_ref):
    E, T = gt_ref.shape
    mask = jnp.where(gt_ref[...] > 0.0, 1.0, 0.0).astype(BF16)
    t_id = lax.shift_right_logical(lax.broadcasted_iota(jnp.int32, (T, LANES), 0),
                                   int(math.log2(SORT_TILE)))
    tind = jnp.where(t_id == lax.broadcasted_iota(jnp.int32, (T, LANES), 1), 1.0, 0.0)
    cnt = _dot(mask, tind.astype(BF16))
    n16 = jnp.floor((cnt + (ROW_ALIGN - 1.0)) * (1.0 / ROW_ALIGN))
    n16b = n16.astype(BF16)
    q = EXP_TILE // ROW_ALIGN
    len16 = jnp.sum(n16, axis=1, keepdims=True)
    pad16 = jnp.floor((len16 + (q - 1.0)) * (1.0 / q)) * q
    sl = _strict_lower(E)
    hi, mid, lo = _split3(jnp.broadcast_to(pad16, (E, LANES)))
    start16 = _dot(sl, hi) + _dot(sl, mid) + _dot(sl, lo)
    gdst16 = start16 + _dot(n16b, _strict_upper(LANES))
    loc16 = _dot(sl, n16b)

    def t(a):
        return jnp.concatenate([a, jnp.zeros((LANES - E, LANES), F32)], axis=0).T

    scale = float(ROW_ALIGN)
    loc_et_ref[...] = loc16 * scale
    np_et_ref[...] = n16 * scale
    loc_te_ref[...] = t(loc16) * scale
    np_te_ref[...] = t(n16) * scale
    tot_ref[...] = (jnp.sum(n16, axis=0, keepdims=True) * scale).astype(jnp.int32)

    n_t, n_blk = ptab_ref.shape
    blk = lax.broadcasted_iota(jnp.int32, (E, n_blk), 1).astype(F32)
    for tile in range(n_t):
        lo_c = loc16[:, tile:tile + 1]
        inside = (lo_c <= blk) & (blk < lo_c + n16[:, tile:tile + 1])
        dst = jnp.sum(jnp.where(inside, gdst16[:, tile:tile + 1] + (blk - lo_c), 0.0),
                      axis=0, keepdims=True)
        ptab_ref[tile:tile + 1, :] = (dst * scale).astype(jnp.int32)
    ends = jnp.concatenate([t(start16 + len16)[0:1, :], t(start16 + pad16)[0:1, :],
                            jnp.zeros((erow_ref.shape[0] - 2, LANES), F32)], axis=0)
    erow_ref[...] = (ends * scale).astype(jnp.int32)
    n_tab = texp_ref.shape[1]
    tile_row16 = (lax.broadcasted_iota(jnp.int32, (E, n_tab), 1) * q).astype(F32)
    owner = jnp.sum(jnp.where(start16[:, 0:1] <= tile_row16, 1.0, 0.0), axis=0, keepdims=True)
    texp_ref[...] = (owner - 1.0).astype(jnp.int32)
    n_used = jnp.sum(pad16, axis=0, keepdims=True) * (1.0 / q)
    nused_ref[...] = jnp.broadcast_to(n_used, (1, LANES)).astype(jnp.int32)


def _moe_meta(gates_t, n_tab):
    E, T = gates_t.shape
    i32 = jnp.int32
    return pl.pallas_call(
        _moe_meta_kernel,
        out_shape=[jax.ShapeDtypeStruct((T // SORT_TILE, _piece_cols(SORT_TILE)), i32),
                   jax.ShapeDtypeStruct((E, LANES), F32),
                   jax.ShapeDtypeStruct((E, LANES), F32),
                   jax.ShapeDtypeStruct((LANES, LANES), F32),
                   jax.ShapeDtypeStruct((LANES, LANES), F32),
                   jax.ShapeDtypeStruct((1, LANES), i32),
                   jax.ShapeDtypeStruct((8, LANES), i32),
                   jax.ShapeDtypeStruct((1, n_tab), i32),
                   jax.ShapeDtypeStruct((1, LANES), i32)],
        compiler_params=pltpu.CompilerParams(vmem_limit_bytes=VMEM_LIMIT),
        name="moe_meta",
    )(gates_t)


def _start_pieces(tile, ptab_s, tot_s, n_cols, make_copy):
    n_pieces = lax.shift_right_logical(tot_s[tile], int(math.log2(ROW_ALIGN)))

    def body(b, carry):
        make_copy(pl.multiple_of(b * ROW_ALIGN, ROW_ALIGN),
                  pl.multiple_of(ptab_s[tile * n_cols + b], ROW_ALIGN)).start()
        return carry

    lax.fori_loop(0, n_pieces, body, 0)


def _moe_sort_kernel(ptab_s, tot_s, lend_s, rend_s, gt_ref, gtok_ref, h_ref, locrow_ref,
                     nprow_ref, xs_hbm, buf, zbuf, sem, zsem):
    tau = pl.program_id(0)
    n_t = pl.num_programs(0)
    slot = lax.rem(tau, 2)
    E, ts = gt_ref.shape
    D = h_ref.shape[1]
    n_cols = _piece_cols(ts)

    def copies(tile, sl, wait):
        if wait:
            rows = pl.multiple_of(tot_s[tile], ROW_ALIGN)

            @pl.when(rows > 0)
            def _():
                pltpu.make_async_copy(buf.at[sl, pl.ds(0, rows)], xs_hbm.at[pl.ds(0, rows)],
                                      sem.at[sl]).wait()
        else:
            def make_copy(loc, dst):
                return pltpu.make_async_copy(buf.at[sl, pl.ds(loc, ROW_ALIGN)],
                                             xs_hbm.at[pl.ds(dst, ROW_ALIGN)], sem.at[sl])
            _start_pieces(tile, ptab_s, tot_s, n_cols, make_copy)

    z_rows = zbuf.shape[0]
    used_rows = rend_s[E - 1]
    n_spare = (xs_hbm.shape[0] - used_rows) // z_rows

    def spare_fill(wait):
        def body(c, carry):
            dst = pl.multiple_of(used_rows + c * z_rows, z_rows)
            cp = pltpu.make_async_copy(zbuf, xs_hbm.at[pl.ds(dst, z_rows)], zsem.at[1])
            if wait:
                cp.wait()
            else:
                cp.start()
            return carry

        lax.fori_loop(0, n_spare, body, 0)

    @pl.when(tau == 0)
    def _():
        zbuf[...] = jnp.zeros_like(zbuf)
        spare_fill(False)

    @pl.when(tau >= 2)
    def _():
        copies(tau - 2, slot, True)

    g = gt_ref[...]
    mask = g > 0.0
    maskb = jnp.where(mask, 1.0, 0.0).astype(BF16)
    pad = jnp.zeros((LANES - E, ts), F32)
    pos = jnp.where(mask, _dot(maskb, _strict_upper(ts)), -1.0)
    pos = jnp.concatenate([pos, pad], axis=0).astype(BF16)
    lo_row = locrow_ref[...]
    hi_row = lo_row + nprow_ref[...]
    h = jnp.concatenate([h_ref[...]] + list(_split3(gtok_ref[...])), axis=1)
    lane = lax.broadcasted_iota(jnp.int32, (P_CHUNK, LANES), 1)

    def chunk(c):
        r0 = c * P_CHUNK
        r = (r0 + lax.broadcasted_iota(jnp.int32, (P_CHUNK, LANES), 0)).astype(F32)
        inside = (lo_row <= r) & (r < hi_row)
        group = jnp.where(inside, 1.0, 0.0).astype(BF16)
        want = r[:, 0:1] - jnp.sum(jnp.where(inside, lo_row, 0.0), axis=1, keepdims=True)
        hit = _dot(group, pos) == want
        rows = _dot(jnp.where(hit, 1.0, 0.0).astype(BF16), h)
        buf[slot, pl.ds(r0, P_CHUNK), 0:D] = rows[:, 0:D].astype(buf.dtype)
        extra = jnp.zeros((P_CHUNK, LANES), F32)
        for k in range(3):
            mine = jnp.where(inside, rows[:, D + k * LANES:D + (k + 1) * LANES], 0.0)
            extra = jnp.where(lane == k, jnp.sum(mine, axis=1, keepdims=True), extra)
        buf[slot, pl.ds(r0, P_CHUNK), D:D + LANES] = extra.astype(buf.dtype)

    n_static = -(-(TOP_K * ts + E * (ROW_ALIGN // 2)) // P_CHUNK)
    for c in range(buf.shape[1] // P_CHUNK):
        if c < n_static:
            chunk(c)
        else:
            pl.when(tot_s[tau] > c * P_CHUNK)(functools.partial(chunk, c))
    copies(tau, slot, False)

    @pl.when(tau == n_t - 1)
    def _():
        @pl.when(n_t >= 2)
        def _():
            copies(tau - 1, 1 - slot, True)
        copies(tau, slot, True)
        spare_fill(True)

        sizes =[zbuf.shape[0] >> s for s in range(int(math.log2(zbuf.shape[0] // ROW_ALIGN)) + 1)]

        def fill(wait):
            def e_body(e, carry):
                start = lend_s[e]
                n = rend_s[e] - start
                off = start
                for size in sizes:
                    bit = jnp.bitwise_and(n, size)

                    @pl.when(bit != 0)
                    def _(off=off, size=size):
                        cp = pltpu.make_async_copy(
                            zbuf.at[pl.ds(0, size)],
                            xs_hbm.at[pl.ds(pl.multiple_of(off, ROW_ALIGN), size)], zsem.at[0])
                        if wait:
                            cp.wait()
                        else:
                            cp.start()

                    off = off + bit
                return carry

            lax.fori_loop(0, E, e_body, 0)

        fill(False)
        fill(True)


def _moe_sort(ptab, tot, lend, rend, gates_t, gates, h2, loc_te, np_te, n_rows):
    E, T = gates_t.shape
    D = h2.shape[1]
    ts = SORT_TILE
    grid_spec = pltpu.PrefetchScalarGridSpec(
        num_scalar_prefetch=4,
        grid=(T // ts,),
        in_specs=[pl.BlockSpec((E, ts), lambda t, *_: (0, t)),
                  pl.BlockSpec((ts, LANES), lambda t, *_: (t, 0)),
                  pl.BlockSpec((ts, D), lambda t, *_: (t, 0)),
                  pl.BlockSpec((None, 1, LANES), lambda t, *_: (t, 0, 0)),
                  pl.BlockSpec((None, 1, LANES), lambda t, *_: (t, 0, 0))],
        out_specs=pl.BlockSpec(memory_space=pl.ANY),
        scratch_shapes=[pltpu.VMEM((2, _local_rows_bound(ts), D + LANES), BF16),
                        pltpu.VMEM((EXP_TILE // 2, D + LANES), BF16),
                        pltpu.SemaphoreType.DMA((2,)),
                        pltpu.SemaphoreType.DMA((2,))])
    return pl.pallas_call(
        _moe_sort_kernel,
        grid_spec=grid_spec,
        out_shape=jax.ShapeDtypeStruct((n_rows, D + LANES), BF16),
        compiler_params=pltpu.CompilerParams(dimension_semantics=("arbitrary",),
                                             vmem_limit_bytes=VMEM_LIMIT),
        name="moe_sort",
    )(ptab, tot, lend, rend, gates_t, gates, h2, loc_te, np_te)


def _moe_expert_kernel(texp_s, nused_s, x_ref, wg_ref, wu_ref, wd_ref, y_ref, wgu_s, wd_s):
    i = pl.program_id(0)
    f = wd_ref.shape[0]

    @pl.when(i < nused_s[0])
    def _():
        @pl.when((i == 0) | (texp_s[i] != texp_s[jnp.maximum(i - 1, 0)]))
        def _():
            wgu_s[:, 0:f] = wg_ref[...].astype(BF16)
            wgu_s[:, f:2 * f] = wu_ref[...].astype(BF16)
            wd_s[...] = wd_ref[...].astype(BF16)

        d = wd_ref.shape[1]
        for r0 in range(0, x_ref.shape[0], EXP_STRIP):
            rows = slice(r0, r0 + EXP_STRIP)
            gate = jnp.sum(x_ref[rows, d:].astype(F32), axis=1, keepdims=True)
            a = _dot(x_ref[rows, 0:d], wgu_s[...])
            act = _silu(a[:, :f]) * a[:, f:] * gate
            y_ref[rows, :] = _dot(act.astype(BF16), wd_s[...]).astype(y_ref.dtype)

    @pl.when(i >= nused_s[0])
    def _():
        y_ref[...] = jnp.zeros_like(y_ref)


def _moe_expert(texp, nused, xs, w_gate, w_up, w_down, n_tiles):
    n_rows, xw = xs.shape
    D, f = w_gate.shape[-2:]
    tm = EXP_TILE

    def tile(i, texp, nused):
        return jnp.maximum(jnp.minimum(i, nused[0] - 1), 0)

    grid_spec = pltpu.PrefetchScalarGridSpec(
        num_scalar_prefetch=2,
        grid=(n_tiles,),
        in_specs=[pl.BlockSpec((tm, xw), lambda i, te, nu: (tile(i, te, nu), 0)),
                  pl.BlockSpec((None, D, f), lambda i, te, nu: (te[tile(i, te, nu)], 0, 0)),
                  pl.BlockSpec((None, D, f), lambda i, te, nu: (te[tile(i, te, nu)], 0, 0)),
                  pl.BlockSpec((None, f, D), lambda i, te, nu: (te[tile(i, te, nu)], 0, 0))],
        out_specs=pl.BlockSpec((tm, D), lambda i, te, nu: (i, 0)),
        scratch_shapes=[pltpu.VMEM((D, 2 * f), BF16), pltpu.VMEM((f, D), BF16)])
    return pl.pallas_call(
        _moe_expert_kernel,
        grid_spec=grid_spec,
        out_shape=jax.ShapeDtypeStruct((n_rows, D), BF16),
        compiler_params=pltpu.CompilerParams(dimension_semantics=("arbitrary",),
                                             vmem_limit_bytes=VMEM_LIMIT),
        name="moe_expert",
    )(texp, nused, xs, w_gate, w_up, w_down)


def _moe_combine_kernel(alpha, ptab_s, tot_s, g_ref, loc_ref, np_ref, h_ref, x1_ref,
                        mod_ref, sgu_ref, sd_ref, lg_ref, lb_ref, y_hbm, o_ref, ybuf, acc_ref,
                        sem):
    tau = pl.program_id(0)
    n_t = pl.num_programs(0)
    slot = lax.rem(tau, 2)
    ts, n_lane = g_ref.shape
    E = loc_ref.shape[0]
    n_cols = _piece_cols(ts)

    def copies(tile, sl, wait):
        if wait:
            rows = pl.multiple_of(tot_s[tile], ROW_ALIGN)

            @pl.when(rows > 0)
            def _():
                pltpu.make_async_copy(y_hbm.at[pl.ds(0, rows)], ybuf.at[sl, pl.ds(0, rows)],
                                      sem.at[sl]).wait()
        else:
            def make_copy(loc, dst):
                return pltpu.make_async_copy(y_hbm.at[pl.ds(dst, ROW_ALIGN)],
                                             ybuf.at[sl, pl.ds(loc, ROW_ALIGN)], sem.at[sl])
            _start_pieces(tile, ptab_s, tot_s, n_cols, make_copy)

    @pl.when(tau == 0)
    def _():
        ybuf[...] = jnp.zeros_like(ybuf)
        copies(0, 0, False)

    @pl.when(tau + 1 < n_t)
    def _():
        copies(tau + 1, 1 - slot, False)

    g = g_ref[...]
    mask = g > 0.0
    maskb = jnp.where(mask, 1.0, 0.0).astype(BF16)
    pos = jnp.where(mask, _dot(_strict_lower(ts), maskb), -1.0).astype(BF16)
    lane = lax.broadcasted_iota(jnp.int32, loc_ref.shape, 1)
    lo_col = jnp.sum(jnp.where(lane == tau, loc_ref[...], 0.0), axis=1, keepdims=True)
    hi_col = lo_col + jnp.sum(jnp.where(lane == tau, np_ref[...], 0.0), axis=1, keepdims=True)

    f = sd_ref.shape[0]
    a = _dot(h_ref[...], sgu_ref[...])
    acc_ref[...] = _dot((_silu(a[:, :f]) * a[:, f:]).astype(BF16), sd_ref[...])

    copies(tau, slot, True)

    def chunk(c):
        r0 = c * P_CHUNK
        r = (r0 + lax.broadcasted_iota(jnp.int32, (E, P_CHUNK), 1)).astype(F32)
        inside = (lo_col <= r) & (r < hi_col)
        group = jnp.concatenate([jnp.where(inside, 1.0, 0.0),
                                 jnp.zeros((n_lane - E, P_CHUNK), F32)], axis=0).astype(BF16)
        want = r[0:1, :] - jnp.sum(jnp.where(inside, lo_col, 0.0), axis=0, keepdims=True)
        hit = _dot(pos, group) == want
        return _dot(jnp.where(hit, 1.0, 0.0).astype(BF16), ybuf[slot, pl.ds(r0, P_CHUNK), :])

    n_static = -(-(TOP_K * ts + E * (ROW_ALIGN // 2)) // P_CHUNK)
    acc_ref[...] += sum(chunk(c) for c in range(n_static))
    for c in range(n_static, ybuf.shape[1] // P_CHUNK):
        @pl.when(tot_s[tau] > c * P_CHUNK)
        def _(c=c):
            acc_ref[...] += chunk(c)

    y = alpha * x1_ref[...] + mod_ref[5:6, :] * acc_ref[...]
    o_ref[...] = _layer_norm(y, lg_ref[...], lb_ref[...])


def _moe_combine(alpha, ptab, tot, gates, loc_et, np_et, h2, x1, mod, sgu, sd, ln_g, ln_b, ys,
                 S):
    T, D = h2.shape
    ts = SORT_TILE
    per_b = S // ts
    row = lambda a: pl.BlockSpec(a.shape, lambda t, *_: (0, 0))
    grid_spec = pltpu.PrefetchScalarGridSpec(
        num_scalar_prefetch=2,
        grid=(T // ts,),
        in_specs=[pl.BlockSpec((ts, LANES), lambda t, *_: (t, 0)),
                  row(loc_et), row(np_et),
                  pl.BlockSpec((ts, D), lambda t, *_: (t, 0)),
                  pl.BlockSpec((ts, D), lambda t, *_: (t, 0)),
                  pl.BlockSpec((None, 6, D), lambda t, *_: (t // per_b, 0, 0)),
                  row(sgu), row(sd), row(ln_g), row(ln_b),
                  pl.BlockSpec(memory_space=pl.ANY)],
        out_specs=pl.BlockSpec((ts, D), lambda t, *_: (t, 0)),
        scratch_shapes=[pltpu.VMEM((2, _local_rows_bound(ts), D), BF16),
                        pltpu.VMEM((ts, D), F32),
                        pltpu.SemaphoreType.DMA((2,))])
    return pl.pallas_call(
        functools.partial(_moe_combine_kernel, alpha),
        grid_spec=grid_spec,
        out_shape=jax.ShapeDtypeStruct((T, D), F32),
        compiler_params=pltpu.CompilerParams(dimension_semantics=("arbitrary",),
                                             vmem_limit_bytes=VMEM_LIMIT),
        name="moe_combine",
    )(ptab, tot, gates, loc_et, np_et, h2, x1, mod, sgu, sd, ln_g, ln_b, ys)


def _moe(alpha, h2, x1, gates, gates_t, mod, w_gate, w_up, w_down, sgu, sd, ln_g, ln_b):
    B, S, D = x1.shape
    T = B * S
    n_t = T // SORT_TILE
    n_bound = _sorted_tiles_bound(T)
    n_tab = -(-n_bound // LANES) * LANES
    ptab, loc_et, np_et, loc_te, np_te, tot, erow, texp, nused = _moe_meta(gates_t, n_tab)
    ptab = ptab.reshape(-1)
    tot = tot[0, :n_t]
    h2 = h2.reshape(T, D)
    x1 = x1.reshape(T, D)
    gates = gates.reshape(T, LANES)
    loc_rows = loc_te[:n_t].reshape(n_t, 1, LANES)
    np_rows = np_te[:n_t].reshape(n_t, 1, LANES)

    def sized(n_tiles):
        def run():
            xs = _moe_sort(ptab, tot, erow[0, :N_EXPERTS], erow[1, :N_EXPERTS], gates_t, gates,
                           h2, loc_rows, np_rows, n_tiles * EXP_TILE)
            ys = _moe_expert(texp[0], nused[0, :1], xs, w_gate, w_up, w_down, n_tiles)
            return _moe_combine(alpha, ptab, tot, gates, loc_et, np_et, h2, x1, mod, sgu, sd,
                                ln_g, ln_b, ys, S)
        return run

    n_small = min(_sorted_tiles_expected(T), n_bound)
    out = lax.cond(nused[0, 0] <= n_small, sized(n_small), sized(n_bound))
    return out.reshape(B, S, D)


def _rearrange_w_in(w):
    d_in = w.shape[0]
    scale = HEAD_DIM ** -0.5 * LOG2E
    fq, fk, fv = w[:, 0:512], w[:, 512:1024], w[:, 1024:1536]
    ff = w[:, 1536:1544]
    nq = w[:, 1544:2056]
    kc, vc, ks, vs, kw, vw = (w[:, 2056 + k * LANES:2056 + (k + 1) * LANES] for k in range(6))
    ng = w[:, 2824:2848]
    pad = jnp.zeros((d_in, LANES - ff.shape[1] - ng.shape[1]), w.dtype)
    cols = [fq * scale, fk, nq * scale, kc, kw, vc, ks, fv, vs, vw, ff, ng, pad]
    return jnp.concatenate(cols, axis=1).astype(BF16)


def _compress_weights(pos, w1, w2):
    half = CMP_BLOCK // 2
    w1r = w1.reshape(2, half, HEAD_DIM, CMP_HIDDEN)
    zeros = jnp.zeros_like(w1r[0])
    def spread(part):
        g0 = jnp.stack([part, zeros], axis=1).reshape(half * 2 * HEAD_DIM, CMP_HIDDEN)
        g1 = jnp.stack([zeros, part], axis=1).reshape(half * 2 * HEAD_DIM, CMP_HIDDEN)
        return jnp.concatenate([g0, g1], axis=1).astype(BF16)
    wa, wb = spread(w1r[0]), spread(w1r[1])
    z2 = jnp.zeros_like(w2)
    w2bd = jnp.concatenate([jnp.concatenate([w2, z2], axis=1),
                            jnp.concatenate([z2, w2], axis=1)], axis=0).astype(BF16)
    posr = pos.reshape(2, half, 1, HEAD_DIM)
    posr = jnp.broadcast_to(posr, (2, half, NSA_GROUPS, HEAD_DIM)).reshape(2, half * 2 * HEAD_DIM)
    return posr, wa, wb, w2bd


@functools.lru_cache(maxsize=None)
def _static_tables(S):
    tq = ATT_TILE
    n_cmp = (S - CMP_BLOCK) // CMP_STRIDE + 1
    n_pad = S // CMP_STRIDE
    n_slc = S // SEL_BLOCK
    t = np.arange(S)[:, None]
    n = np.arange(n_pad)[None, :]
    bucket_c = _t5_bucket_np(t - (n * CMP_STRIDE + CMP_BLOCK - 1)).reshape(1, -1)
    d = (np.arange(4)[:, None, None] * tq + np.arange(tq)[None, :, None]
         - np.arange(2 * tq)[None, None, :])
    bucket_w = _t5_bucket_np(d).reshape(1, -1)
    cs = np.arange(n_pad)[None, :] * CMP_STRIDE
    sj = np.arange(n_slc)[:, None] * SEL_BLOCK
    ovl_t = ((cs < sj + SEL_BLOCK) & (cs + CMP_BLOCK > sj) & (np.arange(n_pad)[None, :] < n_cmp))
    return bucket_c, bucket_w, ovl_t.astype(np.float32)


def kernel(x, c, w_ada, b_ada, w_in, b_f, cmp_pos_k, cmp_w1_k, cmp_w2_k, cmp_pos_v, cmp_w1_v,
           cmp_w2_v, rel_bias, w_out, ln1_g, ln1_b, w_router, e_bias, w_gate, w_up, w_down,
           ws_gate, ws_up, ws_down, ln2_g, ln2_b):
    B, S, D = x.shape
    depth = w_ada.shape[0]
    alpha = (2 * depth) ** 0.25
    tq = ATT_TILE
    assert w_in.shape[-1] == 3 * FOX_HEADS * HEAD_DIM + FOX_HEADS + NSA_HEADS * HEAD_DIM \
        + 6 * NSA_GROUPS * HEAD_DIM + 3 * NSA_HEADS
    assert NSA_GROUPS * HEAD_DIM == LANES and S // CMP_STRIDE == LANES
    assert WINDOW == 2 * tq and S // SEL_BLOCK <= HEAD_DIM // 2
    assert S % FOX_TILE == 0 and S % CMP_TILE == 0 and (B * S) % SORT_TILE == 0
    assert w_gate.shape[1:] == (N_EXPERTS, D, w_down.shape[2])
    bucket_c, bucket_w, ovl_t = _static_tables(S)
    rel_bias_t = rel_bias.T * LOG2E
    bias_c = _bias_table(jnp.asarray(bucket_c), rel_bias_t).reshape(NSA_HEADS, S, S // CMP_STRIDE)
    w4 = _bias_table(jnp.asarray(bucket_w), rel_bias_t).reshape(NSA_HEADS, 4, tq, 2 * tq)
    ovl_t = jnp.asarray(ovl_t, BF16)

    for l in range(depth):
        mod = _ada(c, w_ada[l], b_ada[l]).reshape(B, 6, D)
        bf_row = jnp.zeros((1, LANES), F32).at[0, :FOX_HEADS].set(b_f[l])
        (fq, fk, nq, kc, kw, vc, ks, fv, vs, vw, misc, misc_t) = _in_proj(
            x, mod, _rearrange_w_in(w_in[l]), bf_row)

        o_fox = _fox(fq, fk, fv, misc_t[:, :FOX_HEADS, :])

        pk, wak, wbk, w2k = _compress_weights(cmp_pos_k[l], cmp_w1_k[l], cmp_w2_k[l])
        pv, wav, wbv, w2v = _compress_weights(cmp_pos_v[l], cmp_w1_v[l], cmp_w2_v[l])
        rows = S // CMP_STRIDE
        kcmp, vcmp = _compress(kc.reshape(B, rows, CMP_STRIDE * LANES),
                               vc.reshape(B, rows, CMP_STRIDE * LANES),
                               pk, pv, wak, wbk, wav, wbv, w2k, w2v)

        oc, sel = _cmp_sel(nq, kcmp, vcmp, bias_c, misc, ovl_t)
        o_nsa = _nsa(nq, ks, vs, kw, vw, sel, w4, misc, oc)

        x1, h2, gates, gates_t = _out_proj(
            alpha, o_fox, o_nsa, x, mod, w_out[l].astype(BF16), ln1_g[l].reshape(1, D),
            ln1_b[l].reshape(1, D), w_router[l].T, e_bias[l].reshape(N_EXPERTS, 1))

        sgu = jnp.concatenate([ws_gate[l], ws_up[l]], axis=-1).astype(BF16)
        x = _moe(alpha, h2, x1, gates, gates_t, mod, w_gate[l], w_up[l], w_down[l], sgu,
                 ws_down[l].astype(BF16), ln2_g[l].reshape(1, D), ln2_b[l].reshape(1, D))
    return x
```

```python
import functools
import math

import jax
import jax.numpy as jnp
import numpy as np
from jax import lax
from jax.experimental import pallas as pl
from jax.experimental.pallas import tpu as pltpu

F32 = jnp.float32
BF16 = jnp.bfloat16

HEAD_DIM = 64
FOX_HEADS = 8
NSA_HEADS = 8
NSA_GQA = 4
NSA_GROUPS = NSA_HEADS // NSA_GQA
CMP_BLOCK = 32
CMP_STRIDE = 16
CMP_HIDDEN = 256
SEL_BLOCK = 64
N_SEL = 16
WINDOW = 512
N_BUCKETS = 32
MAX_DISTANCE = 128
N_EXPERTS = 64
N_EXPERT_GROUPS = 8
GROUP_SIZE = N_EXPERTS // N_EXPERT_GROUPS
TOPK_GROUPS = 4
TOP_K = 8
ROUTED_SCALE = 2.5
LN_EPS = 1e-5
NEG_BIG = -1e30
FORCE_SCORE = 1e4

LANES = 128
ATT_TILE = 256
FOX_TILE = 512
CMP_TILE = 512
ROW_TILE = 512
VMEM_LIMIT = 48 * 1024 * 1024

NT_DIMS = (((1,), (1,)), ((), ()))


def _dot(a, b):
    return jnp.dot(a, b, preferred_element_type=F32)


def _dot_nt(a, b):
    return lax.dot_general(a, b, NT_DIMS, preferred_element_type=F32)


def _split3(x):
    hi = x.astype(BF16)
    r1 = x - hi.astype(F32)
    mid = r1.astype(BF16)
    lo = (r1 - mid.astype(F32)).astype(BF16)
    return hi, mid, lo


def _silu(x):
    return x / (1.0 + jnp.exp(-x))


def _sigmoid(x):
    return 1.0 / (1.0 + jnp.exp(-x))


def _swap_halves(x):
    return pltpu.roll(x, HEAD_DIM, 1)


def _t5_bucket_np(dist):
    n = np.maximum(dist, 0)
    max_exact = N_BUCKETS // 2
    nf = np.maximum(n, 1).astype(np.float32)
    large = max_exact + (np.log(nf / max_exact) / math.log(MAX_DISTANCE / max_exact)
                         * (N_BUCKETS - max_exact)).astype(np.int32)
    large = np.minimum(large, N_BUCKETS - 1)
    return np.where(n < max_exact, n, large).astype(np.int32)


def _ada_kernel(c_ref, w_ref, b_ref, o_ref):
    c = c_ref[...]
    o_ref[...] = jnp.dot(_silu(c), w_ref[...], preferred_element_type=F32,
                         precision=lax.Precision.HIGHEST) + b_ref[...]


def _ada(c, w_ada, b_ada):
    B, D = c.shape
    n_out = w_ada.shape[1]
    tn = 1024
    return pl.pallas_call(
        _ada_kernel,
        grid=(n_out // tn,),
        in_specs=[pl.BlockSpec((B, D), lambda j: (0, 0)),
                  pl.BlockSpec((D, tn), lambda j: (0, j)),
                  pl.BlockSpec((1, tn), lambda j: (0, j))],
        out_specs=pl.BlockSpec((B, tn), lambda j: (0, j)),
        out_shape=jax.ShapeDtypeStruct((B, n_out), F32),
        compiler_params=pltpu.CompilerParams(dimension_semantics=("arbitrary",),
                                             vmem_limit_bytes=VMEM_LIMIT),
        name="ada",
    )(c, w_ada, b_ada.reshape(1, n_out))


def _bias_table_kernel(bkt_ref, rbt_ref, o_ref):
    bkt = bkt_ref[...]
    k = lax.broadcasted_iota(jnp.int32, (N_BUCKETS, bkt.shape[1]), 0)
    onehot = jnp.where(k == bkt, 1.0, 0.0).astype(BF16)
    hi, mid, lo = _split3(rbt_ref[...])
    o_ref[...] = _dot(hi, onehot) + _dot(mid, onehot) + _dot(lo, onehot)


def _bias_table(bucket, rel_bias_t):
    n = bucket.shape[1]
    chunk = 32768
    n_heads = rel_bias_t.shape[0]
    return pl.pallas_call(
        _bias_table_kernel,
        grid=(n // chunk,),
        in_specs=[pl.BlockSpec((1, chunk), lambda j: (0, j)),
                  pl.BlockSpec(rel_bias_t.shape, lambda j: (0, 0))],
        out_specs=pl.BlockSpec((n_heads, chunk), lambda j: (0, j)),
        out_shape=jax.ShapeDtypeStruct((n_heads, n), F32),
        compiler_params=pltpu.CompilerParams(dimension_semantics=("parallel",),
                                             vmem_limit_bytes=VMEM_LIMIT),
        name="bias_table",
    )(bucket, rel_bias_t)


_C_FQ, _C_FK, _C_NQ = 0, 512, 1024
_C_K3 = 1536
_C_SK = 1920
_C_FV = 2048
_C_SV = 2560
_C_WV = 2688
_C_MISC = 2816
_IN_COLS = 2944
LOG2E = math.log2(math.e)


def _in_proj_kernel(x_ref, mod_ref, w_ref, bf_ref, fq_ref, fk_ref, nq_ref, kc_ref, kw_ref,
                    vc_ref, ks_ref, fv_ref, vs_ref, vw_ref, misc_ref, misct_ref, carry_ref):
    s_idx = pl.program_id(1)
    tm = x_ref.shape[0]
    mod = mod_ref[...]
    h = (x_ref[...] * (1.0 + mod[1:2, :]) + mod[0:1, :]).astype(BF16)

    for ref, c0 in ((fq_ref, _C_FQ), (fk_ref, _C_FK)):
        ref[...] = _dot(h, w_ref[:, c0:c0 + 512]).astype(ref.dtype)
    nsa_q = _dot(h, w_ref[:, _C_NQ:_C_K3])
    keys = _dot(h, w_ref[:, _C_K3:_C_FV])
    fox_v = _dot(h, w_ref[:, _C_FV:_C_SV])
    tail = _dot(h, w_ref[:, _C_SV:_IN_COLS])
    for k, ref in enumerate((kc_ref, kw_ref, vc_ref)):
        ref[...] = keys[:, k * LANES:(k + 1) * LANES].astype(ref.dtype)

    lane = lax.broadcasted_iota(jnp.int32, (tm, LANES), 1)
    low = lane < HEAD_DIM

    def spread(ref, cols, fill):
        for p in range(cols.shape[-1] // LANES):
            r = cols[:, p * LANES:(p + 1) * LANES]
            ref[:, 2 * p * LANES:(2 * p + 1) * LANES] = jnp.where(low, r, fill).astype(ref.dtype)
            ref[:, (2 * p + 1) * LANES:(2 * p + 2) * LANES] = jnp.where(
                low, _swap_halves(r), fill).astype(ref.dtype)

    for p in range(nsa_q.shape[-1] // LANES):
        r = nsa_q[:, p * LANES:(p + 1) * LANES]
        swapped = _swap_halves(r)
        nq_ref[:, 2 * p * LANES:(2 * p + 1) * LANES] = jnp.where(low, r, swapped).astype(
            nq_ref.dtype)
        nq_ref[:, (2 * p + 1) * LANES:(2 * p + 2) * LANES] = jnp.where(low, swapped, r).astype(
            nq_ref.dtype)

    key_blk = lax.shift_right_logical(
        s_idx * tm + lax.broadcasted_iota(jnp.int32, (tm, LANES), 0), int(math.log2(SEL_BLOCK)))
    spread(ks_ref, keys[:, 3 * LANES:4 * LANES],
           jnp.where(lane == HEAD_DIM + key_blk, 1.0, 0.0))
    spread(fv_ref, fox_v, 1.0)
    spread(vs_ref, tail[:, 0:LANES], 1.0)
    spread(vw_ref, tail[:, LANES:2 * LANES], 1.0)

    z = tail[:, 2 * LANES:3 * LANES] + bf_ref[...]
    is_f = lane < FOX_HEADS
    log_f = jnp.minimum(z, 0.0) - jnp.log(1.0 + jnp.exp(-jnp.abs(z)))
    log_f = jnp.where(is_f, log_f, 0.0)

    row = lax.broadcasted_iota(jnp.int32, (tm, tm), 0)
    col = lax.broadcasted_iota(jnp.int32, (tm, tm), 1)
    tri = jnp.where(row >= col, 1.0, 0.0).astype(BF16)
    sums = _dot(tri, jnp.concatenate(_split3(log_f), axis=1))
    local = sums[:, 0:LANES] + sums[:, LANES:2 * LANES] + sums[:, 2 * LANES:3 * LANES]

    @pl.when(s_idx == 0)
    def _():
        carry_ref[...] = jnp.zeros_like(carry_ref)

    cum = local + carry_ref[...]
    carry_ref[...] = cum[tm - 1:tm, :]
    misc = jnp.where(is_f, cum * LOG2E, _sigmoid(z))
    misc_ref[...] = misc
    misct_ref[...] = misc.T


def _in_proj(x, mod, w_r, bf_row):
    B, S, D = x.shape
    tm = ROW_TILE
    widths = (512, 512, 2 * (_C_K3 - _C_NQ), LANES, LANES, LANES, 2 * (_C_FV - _C_SK),
              2 * (_C_SV - _C_FV),
              2 * (_C_WV - _C_SV), 2 * (_C_MISC - _C_WV))
    wide = lambda w: pl.BlockSpec((None, tm, w), lambda b, s: (b, s, 0))
    out_shape = ([jax.ShapeDtypeStruct((B, S, w), BF16) for w in widths]
                 + [jax.ShapeDtypeStruct((B, S, LANES), F32),
                    jax.ShapeDtypeStruct((B, LANES, S), F32)])
    out_specs = ([wide(w) for w in widths]
                 + [wide(LANES), pl.BlockSpec((None, LANES, tm), lambda b, s: (b, 0, s))])
    return pl.pallas_call(
        _in_proj_kernel,
        grid=(B, S // tm),
        in_specs=[pl.BlockSpec((None, tm, D), lambda b, s: (b, s, 0)),
                  pl.BlockSpec((None, 6, D), lambda b, s: (b, 0, 0)),
                  pl.BlockSpec((D, _IN_COLS), lambda b, s: (0, 0)),
                  pl.BlockSpec((1, LANES), lambda b, s: (0, 0))],
        out_specs=out_specs,
        out_shape=out_shape,
        scratch_shapes=[pltpu.VMEM((1, LANES), F32)],
        compiler_params=pltpu.CompilerParams(dimension_semantics=("parallel", "arbitrary"),
                                             vmem_limit_bytes=VMEM_LIMIT),
        name="in_proj",
    )(x, mod, w_r, bf_row)


def _softmax_weights(s, m):
    return jnp.exp2((s - m).astype(BF16))


def _flash_update(carry, s, vt):
    m, acc = carry
    m_new = jnp.maximum(m, jnp.max(s, axis=-1, keepdims=True))
    alpha = jnp.exp2(m - m_new)
    p = _softmax_weights(s, m_new)
    rows = acc.shape[0]
    acc = alpha.reshape(rows, 1) * acc + _dot(p.reshape(rows, s.shape[-1]), vt)
    return m_new, acc


def _normalize(acc):
    return acc / _swap_halves(acc)


def _fox_kernel(q_ref, k_ref, v_ref, ck_ref, o_ref):
    i = pl.program_id(2)
    tq = q_ref.shape[0]
    tk = ck_ref.shape[2]
    q2 = q_ref[...].astype(F32)
    lane = lax.broadcasted_iota(jnp.int32, (tq, LANES), 1)
    low = lane < HEAD_DIM
    halves = (low, jnp.logical_not(low))
    qh = [jnp.where(h, q2, 0.0).astype(BF16) for h in halves]
    col_minus_row = (lax.broadcasted_iota(jnp.int32, (tq, tk), 1)
                     - lax.broadcasted_iota(jnp.int32, (tq, tk), 0))

    def step(jj, carry, diagonal):
        k0 = jj * tk
        kt = k_ref[pl.ds(k0, tk), :]
        new = []
        for hh in range(2):
            s = _dot_nt(qh[hh], kt) - ck_ref[hh, pl.ds(jj, 1), :]
            if diagonal:
                s = jnp.where(col_minus_row <= 0, s, NEG_BIG)
            vt = v_ref[pl.ds(k0, tk), hh * LANES:(hh + 1) * LANES]
            new.append(_flash_update(carry[hh], s, vt))
        return tuple(new)

    init = tuple((jnp.full((tq, 1), NEG_BIG, F32), jnp.zeros((tq, LANES), F32))
                 for _ in range(2))

    assert tq == tk
    for n_full in range(k_ref.shape[0] // tk):
        @pl.when(i == n_full)
        def _(n_full=n_full):
            carry = init
            for jj in range(n_full):
                carry = step(jj, carry, False)
            carry = step(n_full, carry, True)
            o_ref[...] = jnp.where(low, _normalize(carry[0][1]),
                                   _swap_halves(_normalize(carry[1][1]))).astype(o_ref.dtype)


def _fox(fq, fk, fv, cum_row):
    B, S, W = fq.shape
    tq = tk = FOX_TILE
    n_pairs = W // LANES
    cum_row = cum_row.reshape(B, n_pairs, 2, S // tk, tk)
    return pl.pallas_call(
        _fox_kernel,
        grid=(B, n_pairs, S // tq),
        in_specs=[pl.BlockSpec((None, tq, LANES), lambda b, p, i: (b, i, p)),
                  pl.BlockSpec((None, S, LANES), lambda b, p, i: (b, 0, p)),
                  pl.BlockSpec((None, S, 2 * LANES), lambda b, p, i: (b, 0, p)),
                  pl.BlockSpec((None, None, 2, S // tk, tk), lambda b, p, i: (b, p, 0, 0, 0))],
        out_specs=pl.BlockSpec((None, tq, LANES), lambda b, p, i: (b, i, p)),
        out_shape=jax.ShapeDtypeStruct((B, S, W), BF16),
        compiler_params=pltpu.CompilerParams(
            dimension_semantics=("parallel", "parallel", "arbitrary"),
            vmem_limit_bytes=VMEM_LIMIT),
        name="fox",
    )(fq, fk, fv, cum_row)


def _compress_kernel(xk_ref, xv_ref, pk_ref, pv_ref, wak_ref, wbk_ref, wav_ref, wbv_ref,
                     w2k_ref, w2v_ref, ok_ref, ov_ref):
    n_rows = xk_ref.shape[0]
    for x_ref, p_ref, wa_ref, wb_ref, w2_ref, o_ref in (
            (xk_ref, pk_ref, wak_ref, wbk_ref, w2k_ref, ok_ref),
            (xv_ref, pv_ref, wav_ref, wbv_ref, w2v_ref, ov_ref)):
        x = x_ref[...].astype(F32)
        xa = (x + p_ref[0:1, :]).astype(BF16)
        xb = (x + p_ref[1:2, :]).astype(BF16)
        hb = _dot(xb, wb_ref[...])
        h1 = _dot(xa, wa_ref[...]) + pltpu.roll(hb, n_rows - 1, 0)
        o_ref[...] = _dot(_silu(h1).astype(BF16), w2_ref[...]).astype(o_ref.dtype)


def _compress(xk, xv, pk, pv, wak, wbk, wav, wbv, w2k, w2v):
    B, R, C = xk.shape
    xspec = pl.BlockSpec((None, R, C), lambda b: (b, 0, 0))
    full = lambda a: pl.BlockSpec(a.shape, lambda b: (0,) * a.ndim)
    ospec = pl.BlockSpec((None, R, LANES), lambda b: (b, 0, 0))
    return pl.pallas_call(
        _compress_kernel,
        grid=(B,),
        in_specs=[xspec, xspec] + [full(a) for a in (pk, pv, wak, wbk, wav, wbv, w2k, w2v)],
        out_specs=[ospec, ospec],
        out_shape=[jax.ShapeDtypeStruct((B, R, LANES), BF16)] * 2,
        compiler_params=pltpu.CompilerParams(dimension_semantics=("parallel",),
                                             vmem_limit_bytes=VMEM_LIMIT),
        name="compress",
    )(xk, xv, pk, pv, wak, wbk, wav, wbv, w2k, w2v)


def _rank_rows(score):
    n = score.shape[0]
    j = lax.broadcasted_iota(jnp.int32, score.shape, 0)
    rank = jnp.zeros(score.shape, jnp.int32)
    for i in range(n):
        si = score[i:i + 1, :]
        beats = (si > score) | ((si == score) & (j > i))
        rank = rank + jnp.where(beats, 1, 0)
    return rank


def _dup_head(q_ref, r):
    return q_ref[:, r * LANES:(r + 1) * LANES].astype(F32)


def _pack_heads(o_list, g):
    lane = lax.broadcasted_iota(jnp.int32, o_list[0].shape, 1)
    in_g = (lane >= g * HEAD_DIM) & (lane < (g + 1) * HEAD_DIM)
    both = []
    for o in o_list:
        om = jnp.where(in_g, o, 0.0)
        both.append(om + _swap_halves(om))
    pairs = [jnp.where(lane < HEAD_DIM, both[2 * p], both[2 * p + 1]) for p in range(2)]
    return jnp.concatenate(pairs, axis=1)


def _group_gates(misc, g):
    w = 3 * NSA_GQA
    gates = misc[:, FOX_HEADS:FOX_HEADS + w]
    for other in range(1, NSA_GROUPS):
        gates = jnp.where(g == other, misc[:, FOX_HEADS + other * w:FOX_HEADS + (other + 1) * w],
                          gates)
    return gates


def _cmp_sel_kernel(q_ref, kc_ref, vc_ref, bias_ref, gate_ref, ovl_ref, oc_ref, sel_ref):
    g = pl.program_id(1)
    i = pl.program_id(2)
    tq = q_ref.shape[0]
    n_pad = kc_ref.shape[0]
    lane = lax.broadcasted_iota(jnp.int32, (n_pad, LANES), 1)
    in_g = (lane >= g * HEAD_DIM) & (lane < (g + 1) * HEAD_DIM)
    kc = jnp.where(in_g, kc_ref[...].astype(F32), 0.0).astype(BF16)
    vc = vc_ref[...]
    t = i * tq + lax.broadcasted_iota(jnp.int32, (tq, n_pad), 0)
    n = lax.broadcasted_iota(jnp.int32, (tq, n_pad), 1)
    valid = t >= n * CMP_STRIDE + (CMP_BLOCK - 1)
    gates = _group_gates(gate_ref[...], g)
    p_sum = jnp.zeros((tq, n_pad), F32)
    outs = []
    for r in range(NSA_GQA):
        qr = q_ref[:, r * LANES:(r + 1) * LANES]
        s = _dot_nt(qr, kc)
        s = jnp.where(valid, s + bias_ref[r], NEG_BIG)
        m = jnp.max(s, axis=-1, keepdims=True)
        p = jnp.exp2(s - m)
        p = p / jnp.sum(p, axis=-1, keepdims=True)
        p = jnp.where(valid, p, 0.0)
        p_sum = p_sum + p
        outs.append(_dot(p.astype(BF16), vc) * gates[:, 3 * r:3 * r + 1])
    oc_ref[...] = _pack_heads(outs, g).astype(oc_ref.dtype)

    ovl = ovl_ref[...]
    hi, mid, lo = _split3(p_sum)
    imp = _dot_nt(ovl, hi) + _dot_nt(ovl, mid) + _dot_nt(ovl, lo)
    n_blk = imp.shape[0]
    j = lax.broadcasted_iota(jnp.int32, (n_blk, tq), 0)
    qb = jnp.right_shift(i * tq + lax.broadcasted_iota(jnp.int32, (n_blk, tq), 1),
                         int(math.log2(SEL_BLOCK)))
    forced = (j == 0) | (j == qb) | (j == qb - 1)
    causal = j <= qb
    score = jnp.where(causal, imp + jnp.where(forced, FORCE_SCORE, 0.0), -FORCE_SCORE)
    chosen = (_rank_rows(score) < N_SEL) & causal
    sel = jnp.where(chosen, 1.0, 0.0)
    sel = jnp.concatenate([sel, jnp.zeros((LANES - n_blk, tq), F32)], axis=0)
    sel_ref[...] = sel.T


def _cmp_sel(nq, kcmp, vcmp, bias_c, gates_g, ovl_t):
    B, S, _ = nq.shape
    tq = CMP_TILE
    n_pad = kcmp.shape[1]
    return pl.pallas_call(
        _cmp_sel_kernel,
        grid=(B, NSA_GROUPS, S // tq),
        in_specs=[pl.BlockSpec((None, tq, NSA_GQA * LANES), lambda b, g, i: (b, i, g)),
                  pl.BlockSpec((None, n_pad, LANES), lambda b, g, i: (b, 0, 0)),
                  pl.BlockSpec((None, n_pad, LANES), lambda b, g, i: (b, 0, 0)),
                  pl.BlockSpec((NSA_GQA, tq, n_pad), lambda b, g, i: (g, i, 0)),
                  pl.BlockSpec((None, tq, LANES), lambda b, g, i: (b, i, 0)),
                  pl.BlockSpec(ovl_t.shape, lambda b, g, i: (0, 0))],
        out_specs=[pl.BlockSpec((None, tq, 2 * LANES), lambda b, g, i: (b, i, g)),
                   pl.BlockSpec((None, None, tq, LANES), lambda b, g, i: (b, g, i, 0))],
        out_shape=[jax.ShapeDtypeStruct((B, S, NSA_HEADS * HEAD_DIM), BF16),
                   jax.ShapeDtypeStruct((B, NSA_GROUPS, S, LANES), F32)],
        compiler_params=pltpu.CompilerParams(
            dimension_semantics=("parallel", "parallel", "arbitrary"),
            vmem_limit_bytes=VMEM_LIMIT),
        name="cmp_sel",
    )(nq, kcmp, vcmp, bias_c, gates_g, ovl_t)


def _nsa_kernel(q_ref, ks_ref, vs_ref, kw_ref, vw_ref, sel_ref, w4_ref, gate_ref, oc_ref,
                o_ref):
    for tile in range(ks_ref.shape[0] // q_ref.shape[0]):
        @pl.when(pl.program_id(2) == tile)
        def _(tile=tile):
            _nsa_tile(tile, q_ref, ks_ref, vs_ref, kw_ref, vw_ref, sel_ref, w4_ref, gate_ref,
                      oc_ref, o_ref)


def _nsa_tile(i, q_ref, ks_ref, vs_ref, kw_ref, vw_ref, sel_ref, w4_ref, gate_ref, oc_ref, o_ref):
    g = pl.program_id(0)
    tq = q_ref.shape[0]
    tk = w4_ref.shape[-1]
    H = NSA_GQA
    lane = lax.broadcasted_iota(jnp.int32, (tq, LANES), 1)
    in_g = (lane >= g * HEAD_DIM) & (lane < (g + 1) * HEAD_DIM)
    heads = [_dup_head(q_ref, r) for r in range(H)]
    qs = jnp.concatenate([jnp.where(in_g, hd, 0.0).astype(BF16) for hd in heads], axis=0)

    blk_mask = _swap_halves((sel_ref[...] - 1.0) * (-NEG_BIG))
    qsel = jnp.concatenate(
        [jnp.where(lane < HEAD_DIM, hd, blk_mask).astype(BF16) for hd in heads], axis=0)

    def sel_step(jj, carry, diagonal):
        k0 = jj * tk
        ahead = i * tq - k0
        width = min(ahead + tq, tk) if diagonal else tk
        rows = pl.ds(k0, width)
        s = (_dot_nt(qsel, ks_ref[rows, :]).reshape(H, tq, width)
             + w4_ref[:, min(i - 2 * jj, 3), :, 0:width])
        if diagonal:
            cmr = (lax.broadcasted_iota(jnp.int32, (tq, width), 1)
                   - lax.broadcasted_iota(jnp.int32, (tq, width), 0))
            s = jnp.where((cmr <= ahead)[None], s, NEG_BIG)
        return _flash_update(carry, s, vs_ref[rows, :])

    carry = (jnp.full((H, tq, 1), NEG_BIG, F32), jnp.zeros((H * tq, LANES), F32))
    n_full = (i * tq) // tk
    for jj in range(n_full):
        carry = sel_step(jj, carry, False)
    _, acc_s = sel_step(n_full, carry, True)

    old = [max(i - d, 0) * tq for d in (2, 1)]
    now = i * tq
    k_old = jnp.concatenate([kw_ref[pl.ds(st, tq), :] for st in old], axis=0)
    v_old = jnp.concatenate([vw_ref[pl.ds(st, tq), :] for st in old], axis=0)
    col = lax.broadcasted_iota(jnp.int32, (tq, 2 * tq), 1)
    cmr = col - lax.broadcasted_iota(jnp.int32, (tq, 2 * tq), 0)
    never = 4 * tq
    ok_old = (((col < tq) & (cmr > (0 if i >= 2 else never)))
              | (col >= (tq if i >= 1 else never)))
    s_old = jnp.where(ok_old[None], _dot_nt(qs, k_old).reshape(H, tq, 2 * tq) + w4_ref[:, 2],
                      NEG_BIG)
    s_now = jnp.where((cmr[:, 0:tq] <= 0)[None],
                      _dot_nt(qs, kw_ref[pl.ds(now, tq), :]).reshape(H, tq, tq)
                      + w4_ref[:, 0, :, 0:tq], NEG_BIG)
    m_w = jnp.maximum(jnp.max(s_old, axis=-1, keepdims=True),
                      jnp.max(s_now, axis=-1, keepdims=True))
    acc_w = (_dot(_softmax_weights(s_old, m_w).reshape(H * tq, 2 * tq), v_old)
             + _dot(_softmax_weights(s_now, m_w).reshape(H * tq, tq), vw_ref[pl.ds(now, tq), :]))

    gates = _group_gates(gate_ref[...], g)
    o_s = _normalize(acc_s)
    o_w = _normalize(acc_w)
    outs = []
    for r in range(H):
        sl = slice(r * tq, (r + 1) * tq)
        outs.append(o_s[sl] * gates[:, 3 * r + 1:3 * r + 2] + o_w[sl] * gates[:, 3 * r + 2:3 * r + 3])
    lane = lax.broadcasted_iota(jnp.int32, (tq, LANES), 1)
    pairs = [jnp.where(lane < HEAD_DIM, outs[2 * p], _swap_halves(outs[2 * p + 1]))
             for p in range(H // 2)]
    o_ref[...] = (jnp.concatenate(pairs, axis=1) + oc_ref[...].astype(F32)).astype(o_ref.dtype)


def _nsa(nq, ks, vs, kw, vw, sel, w4, misc, oc):
    B, S, _ = nq.shape
    tq = ATT_TILE
    tk = w4.shape[-1]
    both = lambda: pl.BlockSpec((None, S, LANES), lambda g, b, i: (b, 0, 0))
    mine = lambda: pl.BlockSpec((None, S, LANES), lambda g, b, i: (b, 0, g))
    return pl.pallas_call(
        _nsa_kernel,
        grid=(NSA_GROUPS, B, S // tq),
        in_specs=[pl.BlockSpec((None, tq, NSA_GQA * LANES), lambda g, b, i: (b, i, g)),
                  mine(), mine(), both(), mine(),
                  pl.BlockSpec((None, None, tq, LANES), lambda g, b, i: (b, g, i, 0)),
                  pl.BlockSpec((NSA_GQA, 4, tq, tk), lambda g, b, i: (g, 0, 0, 0)),
                  pl.BlockSpec((None, tq, LANES), lambda g, b, i: (b, i, 0)),
                  pl.BlockSpec((None, tq, 2 * LANES), lambda g, b, i: (b, i, g))],
        out_specs=pl.BlockSpec((None, tq, 2 * LANES), lambda g, b, i: (b, i, g)),
        out_shape=jax.ShapeDtypeStruct((B, S, NSA_HEADS * HEAD_DIM), BF16),
        compiler_params=pltpu.CompilerParams(
            dimension_semantics=("parallel", "parallel", "arbitrary"),
            vmem_limit_bytes=VMEM_LIMIT),
        name="nsa",
    )(nq, ks, vs, kw, vw, sel, w4, misc, oc)


def _layer_norm(y, g, b):
    mu = jnp.mean(y, axis=-1, keepdims=True)
    yc = y - mu
    var = jnp.mean(yc * yc, axis=-1, keepdims=True)
    return yc * lax.rsqrt(var + LN_EPS) * g + b


def _top_rows(score, k):
    n = score.shape[0]
    idx = lax.broadcasted_iota(jnp.int32, score.shape, 0).astype(F32)
    alive = jnp.ones(score.shape, F32)
    for _ in range(k):
        live = alive > 0.0
        best = jnp.max(jnp.where(live, score, -jnp.inf), axis=0, keepdims=True)
        first = jnp.min(jnp.where(live & (score == best), idx, float(n)), axis=0, keepdims=True)
        alive = jnp.where(idx == first, 0.0, alive)
    return alive == 0.0


def _router_gates_t(h2, wr_t, eb_col):
    tm = h2.shape[0]
    h_hi, h_lo, _ = _split3(h2)
    w_hi, w_lo, _ = _split3(wr_t)
    logit = _dot_nt(w_hi, h_hi) + _dot_nt(w_hi, h_lo) + _dot_nt(w_lo, h_hi)
    scores = _sigmoid(logit)
    biased = scores + eb_col
    e_in = lax.broadcasted_iota(jnp.int32, (GROUP_SIZE, tm), 0).astype(F32)
    gs_rows = []
    for gi in range(N_EXPERT_GROUPS):
        grp = biased[gi * GROUP_SIZE:(gi + 1) * GROUP_SIZE, :]
        m1 = jnp.max(grp, axis=0, keepdims=True)
        first = jnp.min(jnp.where(grp == m1, e_in, float(GROUP_SIZE)), axis=0, keepdims=True)
        m2 = jnp.max(jnp.where(e_in == first, -jnp.inf, grp), axis=0, keepdims=True)
        gs_rows.append(m1 + m2)
    gscore = jnp.concatenate(gs_rows, axis=0)
    g_keep = _rank_rows(gscore) < TOPK_GROUPS
    keep = jnp.concatenate(
        [jnp.broadcast_to(g_keep[gi:gi + 1, :], (GROUP_SIZE, tm)) for gi in range(N_EXPERT_GROUPS)],
        axis=0)
    masked = jnp.where(keep, biased, -jnp.inf)
    chosen = _top_rows(masked, TOP_K)
    w = jnp.where(chosen, scores, 0.0)
    return w / jnp.sum(w, axis=0, keepdims=True) * ROUTED_SCALE


def _out_proj_kernel(alpha, of_ref, on_ref, x_ref, mod_ref, w_ref, lg_ref, lb_ref, wr_ref,
                     eb_ref, x1_ref, h2_ref, gate_ref, gate_t_ref):
    half = of_ref.shape[1]
    mod = mod_ref[...]
    mixed = _dot(of_ref[...], w_ref[0:half, :]) + _dot(on_ref[...], w_ref[half:2 * half, :])
    y = alpha * x_ref[...] + mod[2:3, :] * mixed
    x1 = _layer_norm(y, lg_ref[...], lb_ref[...])
    x1_ref[...] = x1
    h2 = x1 * (1.0 + mod[4:5, :]) + mod[3:4, :]
    h2_ref[...] = h2.astype(h2_ref.dtype)
    gates_t = _router_gates_t(h2, wr_ref[...], eb_ref[...])
    gate_t_ref[...] = gates_t
    tm = h2.shape[0]
    gates_t = jnp.concatenate([gates_t, jnp.zeros((LANES - N_EXPERTS, tm), F32)], axis=0)
    gate_ref[...] = gates_t.T


def _out_proj(alpha, o_fox, o_nsa, x, mod, w_out, ln_g, ln_b, wr_t, eb_col):
    B, S, D = x.shape
    tm = ROW_TILE
    half = o_fox.shape[-1]
    row = lambda a: pl.BlockSpec(a.shape, lambda b, s: (0, 0))
    return pl.pallas_call(
        functools.partial(_out_proj_kernel, alpha),
        grid=(B, S // tm),
        in_specs=[pl.BlockSpec((None, tm, half), lambda b, s: (b, s, 0)),
                  pl.BlockSpec((None, tm, half), lambda b, s: (b, s, 0)),
                  pl.BlockSpec((None, tm, D), lambda b, s: (b, s, 0)),
                  pl.BlockSpec((None, 6, D), lambda b, s: (b, 0, 0)),
                  row(w_out), row(ln_g), row(ln_b), row(wr_t), row(eb_col)],
        out_specs=[pl.BlockSpec((None, tm, D), lambda b, s: (b, s, 0)),
                   pl.BlockSpec((None, tm, D), lambda b, s: (b, s, 0)),
                   pl.BlockSpec((None, tm, LANES), lambda b, s: (b, s, 0)),
                   pl.BlockSpec((N_EXPERTS, tm), lambda b, s: (0, b * (S // tm) + s))],
        out_shape=[jax.ShapeDtypeStruct((B, S, D), F32),
                   jax.ShapeDtypeStruct((B, S, D), BF16),
                   jax.ShapeDtypeStruct((B, S, LANES), F32),
                   jax.ShapeDtypeStruct((N_EXPERTS, B * S), F32)],
        compiler_params=pltpu.CompilerParams(dimension_semantics=("parallel", "parallel"),
                                             vmem_limit_bytes=VMEM_LIMIT),
        name="out_proj",
    )(o_fox, o_nsa, x, mod, w_out, ln_g, ln_b, wr_t, eb_col)


SORT_TILE = 256
ROW_ALIGN = 16
EXP_TILE = 1024
EXP_STRIP = 1024
P_CHUNK = 256


def _strict_upper(n):
    return jnp.where(lax.broadcasted_iota(jnp.int32, (n, n), 0)
                     < lax.broadcasted_iota(jnp.int32, (n, n), 1), 1.0, 0.0).astype(BF16)


def _strict_lower(n):
    return jnp.where(lax.broadcasted_iota(jnp.int32, (n, n), 1)
                     < lax.broadcasted_iota(jnp.int32, (n, n), 0), 1.0, 0.0).astype(BF16)


def _local_rows_bound(ts):
    rows = TOP_K * ts + N_EXPERTS * (ROW_ALIGN - 1)
    return -(-rows // P_CHUNK) * P_CHUNK


def _piece_cols(ts):
    return -(-(_local_rows_bound(ts) // ROW_ALIGN) // LANES) * LANES


def _sorted_tiles_bound(T):
    rows = TOP_K * T + (T // SORT_TILE) * N_EXPERTS * (ROW_ALIGN - 1)
    return -(-rows // EXP_TILE) + N_EXPERTS


def _sorted_tiles_expected(T):
    groups = (T // SORT_TILE) * N_EXPERTS
    rows = TOP_K * T + groups * ((ROW_ALIGN - 1) / 2 + 2)
    return int(-(-rows // EXP_TILE) + math.ceil(0.65 * N_EXPERTS))


def _moe_meta_kernel(gt_ref, ptab_ref, loc_et_ref, np_et_ref, loc_te_ref, np_te_ref, tot_ref,
                     erow_ref, texp_ref, nused_ref):
    E, T = gt_ref.shape
    mask = jnp.where(gt_ref[...] > 0.0, 1.0, 0.0).astype(BF16)
    t_id = lax.shift_right_logical(lax.broadcasted_iota(jnp.int32, (T, LANES), 0),
                                   int(math.log2(SORT_TILE)))
    tind = jnp.where(t_id == lax.broadcasted_iota(jnp.int32, (T, LANES), 1), 1.0, 0.0)
    cnt = _dot(mask, tind.astype(BF16))
    n16 = jnp.floor((cnt + (ROW_ALIGN - 1.0)) * (1.0 / ROW_ALIGN))
    n16b = n16.astype(BF16)
    q = EXP_TILE // ROW_ALIGN
    len16 = jnp.sum(n16, axis=1, keepdims=True)
    pad16 = jnp.floor((len16 + (q - 1.0)) * (1.0 / q)) * q
    sl = _strict_lower(E)
    hi, mid, lo = _split3(jnp.broadcast_to(pad16, (E, LANES)))
    start16 = _dot(sl, hi) + _dot(sl, mid) + _dot(sl, lo)
    gdst16 = start16 + _dot(n16b, _strict_upper(LANES))
    loc16 = _dot(sl, n16b)

    def t(a):
        return jnp.concatenate([a, jnp.zeros((LANES - E, LANES), F32)], axis=0).T

    scale = float(ROW_ALIGN)
    loc_et_ref[...] = loc16 * scale
    np_et_ref[...] = n16 * scale
    loc_te_ref[...] = t(loc16) * scale
    np_te_ref[...] = t(n16) * scale
    tot_ref[...] = (jnp.sum(n16, axis=0, keepdims=True) * scale).astype(jnp.int32)

    n_t, n_blk = ptab_ref.shape
    blk = lax.broadcasted_iota(jnp.int32, (E, n_blk), 1).astype(F32)
    for tile in range(n_t):
        lo_c = loc16[:, tile:tile + 1]
        inside = (lo_c <= blk) & (blk < lo_c + n16[:, tile:tile + 1])
        dst = jnp.sum(jnp.where(inside, gdst16[:, tile:tile + 1] + (blk - lo_c), 0.0),
                      axis=0, keepdims=True)
        ptab_ref[tile:tile + 1, :] = (dst * scale).astype(jnp.int32)
    ends = jnp.concatenate([t(start16 + len16)[0:1, :], t(start16 + pad16)[0:1, :],
                            jnp.zeros((erow_ref.shape[0] - 2, LANES), F32)], axis=0)
    erow_ref[...] = (ends * scale).astype(jnp.int32)
    n_tab = texp_ref.shape[1]
    tile_row16 = (lax.broadcasted_iota(jnp.int32, (E, n_tab), 1) * q).astype(F32)
    owner = jnp.sum(jnp.where(start16[:, 0:1] <= tile_row16, 1.0, 0.0), axis=0, keepdims=True)
    texp_ref[...] = (owner - 1.0).astype(jnp.int32)
    n_used = jnp.sum(pad16, axis=0, keepdims=True) * (1.0 / q)
    nused_ref[...] = jnp.broadcast_to(n_used, (1, LANES)).astype(jnp.int32)


def _moe_meta(gates_t, n_tab):
    E, T = gates_t.shape
    i32 = jnp.int32
    return pl.pallas_call(
        _moe_meta_kernel,
        out_shape=[jax.ShapeDtypeStruct((T // SORT_TILE, _piece_cols(SORT_TILE)), i32),
                   jax.ShapeDtypeStruct((E, LANES), F32),
                   jax.ShapeDtypeStruct((E, LANES), F32),
                   jax.ShapeDtypeStruct((LANES, LANES), F32),
                   jax.ShapeDtypeStruct((LANES, LANES), F32),
                   jax.ShapeDtypeStruct((1, LANES), i32),
                   jax.ShapeDtypeStruct((8, LANES), i32),
                   jax.ShapeDtypeStruct((1, n_tab), i32),
                   jax.ShapeDtypeStruct((1, LANES), i32)],
        compiler_params=pltpu.CompilerParams(vmem_limit_bytes=VMEM_LIMIT),
        name="moe_meta",
    )(gates_t)


def _start_pieces(tile, ptab_s, tot_s, n_cols, make_copy):
    n_pieces = lax.shift_right_logical(tot_s[tile], int(math.log2(ROW_ALIGN)))
    group = 4

    def start(b):
        make_copy(pl.multiple_of(b * ROW_ALIGN, ROW_ALIGN),
                  pl.multiple_of(ptab_s[tile * n_cols + b], ROW_ALIGN)).start()

    def grouped(q, carry):
        for k in range(group):
            start(q * group + k)
        return carry

    def single(b, carry):
        start(b)
        return carry

    n_grouped = lax.shift_right_logical(n_pieces, int(math.log2(group)))
    lax.fori_loop(0, n_grouped, grouped, 0)
    lax.fori_loop(n_grouped * group, n_pieces, single, 0)


def _moe_sort_kernel(ptab_s, tot_s, lend_s, rend_s, gt_ref, gtok_ref, h_ref, locrow_ref,
                     nprow_ref, xs_hbm, buf, zbuf, sem, zsem):
    tau = pl.program_id(0)
    n_t = pl.num_programs(0)
    slot = lax.rem(tau, 2)
    E, ts = gt_ref.shape
    D = h_ref.shape[1]
    n_cols = _piece_cols(ts)

    def copies(tile, sl, wait):
        if wait:
            rows = pl.multiple_of(tot_s[tile], ROW_ALIGN)

            @pl.when(rows > 0)
            def _():
                pltpu.make_async_copy(buf.at[sl, pl.ds(0, rows)], xs_hbm.at[pl.ds(0, rows)],
                                      sem.at[sl]).wait()
        else:
            def make_copy(loc, dst):
                return pltpu.make_async_copy(buf.at[sl, pl.ds(loc, ROW_ALIGN)],
                                             xs_hbm.at[pl.ds(dst, ROW_ALIGN)], sem.at[sl])
            _start_pieces(tile, ptab_s, tot_s, n_cols, make_copy)

    z_rows = zbuf.shape[0]
    used_rows = rend_s[E - 1]
    n_spare = (xs_hbm.shape[0] - used_rows) // z_rows

    def spare_fill(wait):
        def body(c, carry):
            dst = pl.multiple_of(used_rows + c * z_rows, z_rows)
            cp = pltpu.make_async_copy(zbuf, xs_hbm.at[pl.ds(dst, z_rows)], zsem.at[1])
            if wait:
                cp.wait()
            else:
                cp.start()
            return carry

        lax.fori_loop(0, n_spare, body, 0)

    @pl.when(tau == 0)
    def _():
        zbuf[...] = jnp.zeros_like(zbuf)
        spare_fill(False)

    @pl.when(tau >= 2)
    def _():
        copies(tau - 2, slot, True)

    g = gt_ref[...]
    mask = g > 0.0
    maskb = jnp.where(mask, 1.0, 0.0).astype(BF16)
    pad = jnp.zeros((LANES - E, ts), F32)
    pos = jnp.where(mask, _dot(maskb, _strict_upper(ts)), -1.0)
    pos = jnp.concatenate([pos, pad], axis=0).astype(BF16)
    lo_row = locrow_ref[...]
    hi_row = lo_row + nprow_ref[...]
    h = jnp.concatenate([h_ref[...]] + list(_split3(gtok_ref[...])), axis=1)
    lane = lax.broadcasted_iota(jnp.int32, (P_CHUNK, LANES), 1)

    def chunk(c):
        r0 = c * P_CHUNK
        r = (r0 + lax.broadcasted_iota(jnp.int32, (P_CHUNK, LANES), 0)).astype(F32)
        inside = (lo_row <= r) & (r < hi_row)
        group = jnp.where(inside, 1.0, 0.0).astype(BF16)
        want = r[:, 0:1] - jnp.sum(jnp.where(inside, lo_row, 0.0), axis=1, keepdims=True)
        hit = _dot(group, pos) == want
        rows = _dot(jnp.where(hit, 1.0, 0.0).astype(BF16), h)
        buf[slot, pl.ds(r0, P_CHUNK), 0:D] = rows[:, 0:D].astype(buf.dtype)
        extra = jnp.zeros((P_CHUNK, LANES), F32)
        for k in range(3):
            mine = jnp.where(inside, rows[:, D + k * LANES:D + (k + 1) * LANES], 0.0)
            extra = jnp.where(lane == k, jnp.sum(mine, axis=1, keepdims=True), extra)
        buf[slot, pl.ds(r0, P_CHUNK), D:D + LANES] = extra.astype(buf.dtype)

    n_static = -(-(TOP_K * ts + E * (ROW_ALIGN // 2)) // P_CHUNK)
    for c in range(buf.shape[1] // P_CHUNK):
        if c < n_static:
            chunk(c)
        else:
            pl.when(tot_s[tau] > c * P_CHUNK)(functools.partial(chunk, c))
    copies(tau, slot, False)

    @pl.when(tau == n_t - 1)
    def _():
        @pl.when(n_t >= 2)
        def _():
            copies(tau - 1, 1 - slot, True)
        copies(tau, slot, True)
        spare_fill(True)

        sizes =[zbuf.shape[0] >> s for s in range(int(math.log2(zbuf.shape[0] // ROW_ALIGN)) + 1)]

        def fill(wait):
            def e_body(e, carry):
                start = lend_s[e]
                n = rend_s[e] - start
                off = start
                for size in sizes:
                    bit = jnp.bitwise_and(n, size)

                    @pl.when(bit != 0)
                    def _(off=off, size=size):
                        cp = pltpu.make_async_copy(
                            zbuf.at[pl.ds(0, size)],
                            xs_hbm.at[pl.ds(pl.multiple_of(off, ROW_ALIGN), size)], zsem.at[0])
                        if wait:
                            cp.wait()
                        else:
                            cp.start()

                    off = off + bit
                return carry

            lax.fori_loop(0, E, e_body, 0)

        fill(False)
        fill(True)


def _moe_sort(ptab, tot, lend, rend, gates_t, gates, h2, loc_te, np_te, n_rows):
    E, T = gates_t.shape
    D = h2.shape[1]
    ts = SORT_TILE
    grid_spec = pltpu.PrefetchScalarGridSpec(
        num_scalar_prefetch=4,
        grid=(T // ts,),
        in_specs=[pl.BlockSpec((E, ts), lambda t, *_: (0, t)),
                  pl.BlockSpec((ts, LANES), lambda t, *_: (t, 0)),
                  pl.BlockSpec((ts, D), lambda t, *_: (t, 0)),
                  pl.BlockSpec((None, 1, LANES), lambda t, *_: (t, 0, 0)),
                  pl.BlockSpec((None, 1, LANES), lambda t, *_: (t, 0, 0))],
        out_specs=pl.BlockSpec(memory_space=pl.ANY),
        scratch_shapes=[pltpu.VMEM((2, _local_rows_bound(ts), D + LANES), BF16),
                        pltpu.VMEM((EXP_TILE // 2, D + LANES), BF16),
                        pltpu.SemaphoreType.DMA((2,)),
                        pltpu.SemaphoreType.DMA((2,))])
    return pl.pallas_call(
        _moe_sort_kernel,
        grid_spec=grid_spec,
        out_shape=jax.ShapeDtypeStruct((n_rows, D + LANES), BF16),
        compiler_params=pltpu.CompilerParams(dimension_semantics=("arbitrary",),
                                             vmem_limit_bytes=VMEM_LIMIT),
        name="moe_sort",
    )(ptab, tot, lend, rend, gates_t, gates, h2, loc_te, np_te)


def _moe_expert_kernel(texp_s, nused_s, x_ref, wg_ref, wu_ref, wd_ref, y_ref, wgu_s, wd_s):
    i = pl.program_id(0)
    f = wd_ref.shape[0]

    @pl.when(i < nused_s[0])
    def _():
        @pl.when((i == 0) | (texp_s[i] != texp_s[jnp.maximum(i - 1, 0)]))
        def _():
            wgu_s[:, 0:f] = wg_ref[...].astype(BF16)
            wgu_s[:, f:2 * f] = wu_ref[...].astype(BF16)
            wd_s[...] = wd_ref[...].astype(BF16)

        d = wd_ref.shape[1]
        for r0 in range(0, x_ref.shape[0], EXP_STRIP):
            rows = slice(r0, r0 + EXP_STRIP)
            gate = jnp.sum(x_ref[rows, d:].astype(F32), axis=1, keepdims=True)
            a = _dot(x_ref[rows, 0:d], wgu_s[...])
            act = _silu(a[:, :f]) * a[:, f:] * gate
            y_ref[rows, :] = _dot(act.astype(BF16), wd_s[...]).astype(y_ref.dtype)

    @pl.when(i >= nused_s[0])
    def _():
        y_ref[...] = jnp.zeros_like(y_ref)


def _moe_expert(texp, nused, xs, w_gate, w_up, w_down, n_tiles):
    n_rows, xw = xs.shape
    D, f = w_gate.shape[-2:]
    tm = EXP_TILE

    def tile(i, texp, nused):
        return jnp.maximum(jnp.minimum(i, nused[0] - 1), 0)

    grid_spec = pltpu.PrefetchScalarGridSpec(
        num_scalar_prefetch=2,
        grid=(n_tiles,),
        in_specs=[pl.BlockSpec((tm, xw), lambda i, te, nu: (tile(i, te, nu), 0)),
                  pl.BlockSpec((None, D, f), lambda i, te, nu: (te[tile(i, te, nu)], 0, 0)),
                  pl.BlockSpec((None, D, f), lambda i, te, nu: (te[tile(i, te, nu)], 0, 0)),
                  pl.BlockSpec((None, f, D), lambda i, te, nu: (te[tile(i, te, nu)], 0, 0))],
        out_specs=pl.BlockSpec((tm, D), lambda i, te, nu: (i, 0)),
        scratch_shapes=[pltpu.VMEM((D, 2 * f), BF16), pltpu.VMEM((f, D), BF16)])
    return pl.pallas_call(
        _moe_expert_kernel,
        grid_spec=grid_spec,
        out_shape=jax.ShapeDtypeStruct((n_rows, D), BF16),
        compiler_params=pltpu.CompilerParams(dimension_semantics=("arbitrary",),
                                             vmem_limit_bytes=VMEM_LIMIT),
        name="moe_expert",
    )(texp, nused, xs, w_gate, w_up, w_down)


def _moe_combine_kernel(alpha, ptab_s, tot_s, g_ref, loc_ref, np_ref, h_ref, x1_ref,
                        mod_ref, sgu_ref, sd_ref, lg_ref, lb_ref, y_hbm, o_ref, ybuf, acc_ref,
                        sem):
    tau = pl.program_id(0)
    n_t = pl.num_programs(0)
    slot = lax.rem(tau, 2)
    ts, n_lane = g_ref.shape
    E = loc_ref.shape[0]
    n_cols = _piece_cols(ts)

    def copies(tile, sl, wait):
        if wait:
            rows = pl.multiple_of(tot_s[tile], ROW_ALIGN)

            @pl.when(rows > 0)
            def _():
                pltpu.make_async_copy(y_hbm.at[pl.ds(0, rows)], ybuf.at[sl, pl.ds(0, rows)],
                                      sem.at[sl]).wait()
        else:
            def make_copy(loc, dst):
                return pltpu.make_async_copy(y_hbm.at[pl.ds(dst, ROW_ALIGN)],
                                             ybuf.at[sl, pl.ds(loc, ROW_ALIGN)], sem.at[sl])
            _start_pieces(tile, ptab_s, tot_s, n_cols, make_copy)

    @pl.when(tau == 0)
    def _():
        ybuf[...] = jnp.zeros_like(ybuf)
        copies(0, 0, False)

    @pl.when(tau + 1 < n_t)
    def _():
        copies(tau + 1, 1 - slot, False)

    g = g_ref[...]
    mask = g > 0.0
    maskb = jnp.where(mask, 1.0, 0.0).astype(BF16)
    pos = jnp.where(mask, _dot(_strict_lower(ts), maskb), -1.0).astype(BF16)
    lane = lax.broadcasted_iota(jnp.int32, loc_ref.shape, 1)
    lo_col = jnp.sum(jnp.where(lane == tau, loc_ref[...], 0.0), axis=1, keepdims=True)
    hi_col = lo_col + jnp.sum(jnp.where(lane == tau, np_ref[...], 0.0), axis=1, keepdims=True)

    f = sd_ref.shape[0]
    a = _dot(h_ref[...], sgu_ref[...])
    acc_ref[...] = _dot((_silu(a[:, :f]) * a[:, f:]).astype(BF16), sd_ref[...])

    copies(tau, slot, True)

    def chunk(c):
        r0 = c * P_CHUNK
        r = (r0 + lax.broadcasted_iota(jnp.int32, (E, P_CHUNK), 1)).astype(F32)
        inside = (lo_col <= r) & (r < hi_col)
        group = jnp.concatenate([jnp.where(inside, 1.0, 0.0),
                                 jnp.zeros((n_lane - E, P_CHUNK), F32)], axis=0).astype(BF16)
        want = r[0:1, :] - jnp.sum(jnp.where(inside, lo_col, 0.0), axis=0, keepdims=True)
        hit = _dot(pos, group) == want
        return _dot(jnp.where(hit, 1.0, 0.0).astype(BF16), ybuf[slot, pl.ds(r0, P_CHUNK), :])

    n_static = -(-(TOP_K * ts + E * (ROW_ALIGN // 2)) // P_CHUNK)
    acc_ref[...] += sum(chunk(c) for c in range(n_static))
    for c in range(n_static, ybuf.shape[1] // P_CHUNK):
        @pl.when(tot_s[tau] > c * P_CHUNK)
        def _(c=c):
            acc_ref[...] += chunk(c)

    y = alpha * x1_ref[...] + mod_ref[5:6, :] * acc_ref[...]
    o_ref[...] = _layer_norm(y, lg_ref[...], lb_ref[...])


def _moe_combine(alpha, ptab, tot, gates, loc_et, np_et, h2, x1, mod, sgu, sd, ln_g, ln_b, ys,
                 S):
    T, D = h2.shape
    ts = SORT_TILE
    per_b = S // ts
    row = lambda a: pl.BlockSpec(a.shape, lambda t, *_: (0, 0))
    grid_spec = pltpu.PrefetchScalarGridSpec(
        num_scalar_prefetch=2,
        grid=(T // ts,),
        in_specs=[pl.BlockSpec((ts, LANES), lambda t, *_: (t, 0)),
                  row(loc_et), row(np_et),
                  pl.BlockSpec((ts, D), lambda t, *_: (t, 0)),
                  pl.BlockSpec((ts, D), lambda t, *_: (t, 0)),
                  pl.BlockSpec((None, 6, D), lambda t, *_: (t // per_b, 0, 0)),
                  row(sgu), row(sd), row(ln_g), row(ln_b),
                  pl.BlockSpec(memory_space=pl.ANY)],
        out_specs=pl.BlockSpec((ts, D), lambda t, *_: (t, 0)),
        scratch_shapes=[pltpu.VMEM((2, _local_rows_bound(ts), D), BF16),
                        pltpu.VMEM((ts, D), F32),
                        pltpu.SemaphoreType.DMA((2,))])
    return pl.pallas_call(
        functools.partial(_moe_combine_kernel, alpha),
        grid_spec=grid_spec,
        out_shape=jax.ShapeDtypeStruct((T, D), F32),
        compiler_params=pltpu.CompilerParams(dimension_semantics=("arbitrary",),
                                             vmem_limit_bytes=VMEM_LIMIT),
        name="moe_combine",
    )(ptab, tot, gates, loc_et, np_et, h2, x1, mod, sgu, sd, ln_g, ln_b, ys)


def _moe(alpha, h2, x1, gates, gates_t, mod, w_gate, w_up, w_down, sgu, sd, ln_g, ln_b):
    B, S, D = x1.shape
    T = B * S
    n_t = T // SORT_TILE
    n_bound = _sorted_tiles_bound(T)
    n_tab = -(-n_bound // LANES) * LANES
    ptab, loc_et, np_et, loc_te, np_te, tot, erow, texp, nused = _moe_meta(gates_t, n_tab)
    ptab = ptab.reshape(-1)
    tot = tot[0, :n_t]
    h2 = h2.reshape(T, D)
    x1 = x1.reshape(T, D)
    gates = gates.reshape(T, LANES)
    loc_rows = loc_te[:n_t].reshape(n_t, 1, LANES)
    np_rows = np_te[:n_t].reshape(n_t, 1, LANES)

    def sized(n_tiles):
        def run():
            xs = _moe_sort(ptab, tot, erow[0, :N_EXPERTS], erow[1, :N_EXPERTS], gates_t, gates,
                           h2, loc_rows, np_rows, n_tiles * EXP_TILE)
            ys = _moe_expert(texp[0], nused[0, :1], xs, w_gate, w_up, w_down, n_tiles)
            return _moe_combine(alpha, ptab, tot, gates, loc_et, np_et, h2, x1, mod, sgu, sd,
                                ln_g, ln_b, ys, S)
        return run

    n_small = min(_sorted_tiles_expected(T), n_bound)
    out = lax.cond(nused[0, 0] <= n_small, sized(n_small), sized(n_bound))
    return out.reshape(B, S, D)


def _rearrange_w_in(w):
    d_in = w.shape[0]
    scale = HEAD_DIM ** -0.5 * LOG2E
    fq, fk, fv = w[:, 0:512], w[:, 512:1024], w[:, 1024:1536]
    ff = w[:, 1536:1544]
    nq = w[:, 1544:2056]
    kc, vc, ks, vs, kw, vw = (w[:, 2056 + k * LANES:2056 + (k + 1) * LANES] for k in range(6))
    ng = w[:, 2824:2848]
    pad = jnp.zeros((d_in, LANES - ff.shape[1] - ng.shape[1]), w.dtype)
    cols = [fq * scale, fk, nq * scale, kc, kw, vc, ks, fv, vs, vw, ff, ng, pad]
    return jnp.concatenate(cols, axis=1).astype(BF16)


def _compress_weights(pos, w1, w2):
    half = CMP_BLOCK // 2
    w1r = w1.reshape(2, half, HEAD_DIM, CMP_HIDDEN)
    zeros = jnp.zeros_like(w1r[0])
    def spread(part):
        g0 = jnp.stack([part, zeros], axis=1).reshape(half * 2 * HEAD_DIM, CMP_HIDDEN)
        g1 = jnp.stack([zeros, part], axis=1).reshape(half * 2 * HEAD_DIM, CMP_HIDDEN)
        return jnp.concatenate([g0, g1], axis=1).astype(BF16)
    wa, wb = spread(w1r[0]), spread(w1r[1])
    z2 = jnp.zeros_like(w2)
    w2bd = jnp.concatenate([jnp.concatenate([w2, z2], axis=1),
                            jnp.concatenate([z2, w2], axis=1)], axis=0).astype(BF16)
    posr = pos.reshape(2, half, 1, HEAD_DIM)
    posr = jnp.broadcast_to(posr, (2, half, NSA_GROUPS, HEAD_DIM)).reshape(2, half * 2 * HEAD_DIM)
    return posr, wa, wb, w2bd


@functools.lru_cache(maxsize=None)
def _static_tables(S):
    tq = ATT_TILE
    n_cmp = (S - CMP_BLOCK) // CMP_STRIDE + 1
    n_pad = S // CMP_STRIDE
    n_slc = S // SEL_BLOCK
    t = np.arange(S)[:, None]
    n = np.arange(n_pad)[None, :]
    bucket_c = _t5_bucket_np(t - (n * CMP_STRIDE + CMP_BLOCK - 1)).reshape(1, -1)
    d = (np.arange(4)[:, None, None] * tq + np.arange(tq)[None, :, None]
         - np.arange(2 * tq)[None, None, :])
    bucket_w = _t5_bucket_np(d).reshape(1, -1)
    cs = np.arange(n_pad)[None, :] * CMP_STRIDE
    sj = np.arange(n_slc)[:, None] * SEL_BLOCK
    ovl_t = ((cs < sj + SEL_BLOCK) & (cs + CMP_BLOCK > sj) & (np.arange(n_pad)[None, :] < n_cmp))
    return bucket_c, bucket_w, ovl_t.astype(np.float32)


def kernel(x, c, w_ada, b_ada, w_in, b_f, cmp_pos_k, cmp_w1_k, cmp_w2_k, cmp_pos_v, cmp_w1_v,
           cmp_w2_v, rel_bias, w_out, ln1_g, ln1_b, w_router, e_bias, w_gate, w_up, w_down,
           ws_gate, ws_up, ws_down, ln2_g, ln2_b):
    B, S, D = x.shape
    depth = w_ada.shape[0]
    alpha = (2 * depth) ** 0.25
    tq = ATT_TILE
    assert w_in.shape[-1] == 3 * FOX_HEADS * HEAD_DIM + FOX_HEADS + NSA_HEADS * HEAD_DIM \
        + 6 * NSA_GROUPS * HEAD_DIM + 3 * NSA_HEADS
    assert NSA_GROUPS * HEAD_DIM == LANES and S // CMP_STRIDE == LANES
    assert WINDOW == 2 * tq and S // SEL_BLOCK <= HEAD_DIM // 2
    assert S % FOX_TILE == 0 and S % CMP_TILE == 0 and (B * S) % SORT_TILE == 0
    assert w_gate.shape[1:] == (N_EXPERTS, D, w_down.shape[2])
    bucket_c, bucket_w, ovl_t = _static_tables(S)
    rel_bias_t = rel_bias.T * LOG2E
    bias_c = _bias_table(jnp.asarray(bucket_c), rel_bias_t).reshape(NSA_HEADS, S, S // CMP_STRIDE)
    w4 = _bias_table(jnp.asarray(bucket_w), rel_bias_t).reshape(NSA_HEADS, 4, tq, 2 * tq)
    ovl_t = jnp.asarray(ovl_t, BF16)

    for l in range(depth):
        mod = _ada(c, w_ada[l], b_ada[l]).reshape(B, 6, D)
        bf_row = jnp.zeros((1, LANES), F32).at[0, :FOX_HEADS].set(b_f[l])
        (fq, fk, nq, kc, kw, vc, ks, fv, vs, vw, misc, misc_t) = _in_proj(
            x, mod, _rearrange_w_in(w_in[l]), bf_row)

        o_fox = _fox(fq, fk, fv, misc_t[:, :FOX_HEADS, :])

        pk, wak, wbk, w2k = _compress_weights(cmp_pos_k[l], cmp_w1_k[l], cmp_w2_k[l])
        pv, wav, wbv, w2v = _compress_weights(cmp_pos_v[l], cmp_w1_v[l], cmp_w2_v[l])
        rows = S // CMP_STRIDE
        kcmp, vcmp = _compress(kc.reshape(B, rows, CMP_STRIDE * LANES),
                               vc.reshape(B, rows, CMP_STRIDE * LANES),
                               pk, pv, wak, wbk, wav, wbv, w2k, w2v)

        oc, sel = _cmp_sel(nq, kcmp, vcmp, bias_c, misc, ovl_t)
        o_nsa = _nsa(nq, ks, vs, kw, vw, sel, w4, misc, oc)

        x1, h2, gates, gates_t = _out_proj(
            alpha, o_fox, o_nsa, x, mod, w_out[l].astype(BF16), ln1_g[l].reshape(1, D),
            ln1_b[l].reshape(1, D), w_router[l].T, e_bias[l].reshape(N_EXPERTS, 1))

        sgu = jnp.concatenate([ws_gate[l], ws_up[l]], axis=-1).astype(BF16)
        x = _moe(alpha, h2, x1, gates, gates_t, mod, w_gate[l], w_up[l], w_down[l], sgu,
                 ws_down[l].astype(BF16), ln2_g[l].reshape(1, D), ln2_b[l].reshape(1, D))
    return x
```

```python
import functools
import math

import jax
import jax.numpy as jnp
import numpy as np
from jax import lax
from jax.experimental import pallas as pl
from jax.experimental.pallas import tpu as pltpu

F32 = jnp.float32
BF16 = jnp.bfloat16

HEAD_DIM = 64
FOX_HEADS = 8
NSA_HEADS = 8
NSA_GQA = 4
NSA_GROUPS = NSA_HEADS // NSA_GQA
CMP_BLOCK = 32
CMP_STRIDE = 16
CMP_HIDDEN = 256
SEL_BLOCK = 64
N_SEL = 16
WINDOW = 512
N_BUCKETS = 32
MAX_DISTANCE = 128
N_EXPERTS = 64
N_EXPERT_GROUPS = 8
GROUP_SIZE = N_EXPERTS // N_EXPERT_GROUPS
TOPK_GROUPS = 4
TOP_K = 8
ROUTED_SCALE = 2.5
LN_EPS = 1e-5
NEG_BIG = -1e30
FORCE_SCORE = 1e4

LANES = 128
ATT_TILE = 256
FOX_TILE = 512
CMP_TILE = 512
ROW_TILE = 512
VMEM_LIMIT = 48 * 1024 * 1024

NT_DIMS = (((1,), (1,)), ((), ()))


def _dot(a, b):
    return jnp.dot(a, b, preferred_element_type=F32)


def _dot_nt(a, b):
    return lax.dot_general(a, b, NT_DIMS, preferred_element_type=F32)


def _split3(x):
    hi = x.astype(BF16)
    r1 = x - hi.astype(F32)
    mid = r1.astype(BF16)
    lo = (r1 - mid.astype(F32)).astype(BF16)
    return hi, mid, lo


def _silu(x):
    return x / (1.0 + jnp.exp(-x))


def _sigmoid(x):
    return 1.0 / (1.0 + jnp.exp(-x))


def _swap_halves(x):
    return pltpu.roll(x, HEAD_DIM, 1)


def _t5_bucket_np(dist):
    n = np.maximum(dist, 0)
    max_exact = N_BUCKETS // 2
    nf = np.maximum(n, 1).astype(np.float32)
    large = max_exact + (np.log(nf / max_exact) / math.log(MAX_DISTANCE / max_exact)
                         * (N_BUCKETS - max_exact)).astype(np.int32)
    large = np.minimum(large, N_BUCKETS - 1)
    return np.where(n < max_exact, n, large).astype(np.int32)


def _ada_kernel(c_ref, w_ref, b_ref, o_ref):
    c = c_ref[...]
    o_ref[...] = jnp.dot(_silu(c), w_ref[...], preferred_element_type=F32,
                         precision=lax.Precision.HIGHEST) + b_ref[...]


def _ada(c, w_ada, b_ada):
    B, D = c.shape
    n_out = w_ada.shape[1]
    tn = 1024
    return pl.pallas_call(
        _ada_kernel,
        grid=(n_out // tn,),
        in_specs=[pl.BlockSpec((B, D), lambda j: (0, 0)),
                  pl.BlockSpec((D, tn), lambda j: (0, j)),
                  pl.BlockSpec((1, tn), lambda j: (0, j))],
        out_specs=pl.BlockSpec((B, tn), lambda j: (0, j)),
        out_shape=jax.ShapeDtypeStruct((B, n_out), F32),
        compiler_params=pltpu.CompilerParams(dimension_semantics=("arbitrary",),
                                             vmem_limit_bytes=VMEM_LIMIT),
        name="ada",
    )(c, w_ada, b_ada.reshape(1, n_out))


def _bias_table_kernel(bkt_ref, rbt_ref, o_ref):
    bkt = bkt_ref[...]
    k = lax.broadcasted_iota(jnp.int32, (N_BUCKETS, bkt.shape[1]), 0)
    onehot = jnp.where(k == bkt, 1.0, 0.0).astype(BF16)
    hi, mid, lo = _split3(rbt_ref[...])
    o_ref[...] = _dot(hi, onehot) + _dot(mid, onehot) + _dot(lo, onehot)


def _bias_table(bucket, rel_bias_t):
    n = bucket.shape[1]
    chunk = 32768
    n_heads = rel_bias_t.shape[0]
    return pl.pallas_call(
        _bias_table_kernel,
        grid=(n // chunk,),
        in_specs=[pl.BlockSpec((1, chunk), lambda j: (0, j)),
                  pl.BlockSpec(rel_bias_t.shape, lambda j: (0, 0))],
        out_specs=pl.BlockSpec((n_heads, chunk), lambda j: (0, j)),
        out_shape=jax.ShapeDtypeStruct((n_heads, n), F32),
        compiler_params=pltpu.CompilerParams(dimension_semantics=("parallel",),
                                             vmem_limit_bytes=VMEM_LIMIT),
        name="bias_table",
    )(bucket, rel_bias_t)


_C_FQ, _C_FK, _C_NQ = 0, 512, 1024
_C_K3 = 1536
_C_SK = 1920
_C_FV = 2048
_C_SV = 2560
_C_WV = 2688
_C_MISC = 2816
_IN_COLS = 2944
LOG2E = math.log2(math.e)


def _in_proj_kernel(x_ref, mod_ref, w_ref, bf_ref, fq_ref, fk_ref, nq_ref, kc_ref, kw_ref,
                    vc_ref, ks_ref, fv_ref, vs_ref, vw_ref, misc_ref, misct_ref, carry_ref):
    s_idx = pl.program_id(1)
    tm = x_ref.shape[0]
    mod = mod_ref[...]
    h = (x_ref[...] * (1.0 + mod[1:2, :]) + mod[0:1, :]).astype(BF16)

    for ref, c0 in ((fq_ref, _C_FQ), (fk_ref, _C_FK)):
        ref[...] = _dot(h, w_ref[:, c0:c0 + 512]).astype(ref.dtype)
    nsa_q = _dot(h, w_ref[:, _C_NQ:_C_K3])
    keys = _dot(h, w_ref[:, _C_K3:_C_FV])
    fox_v = _dot(h, w_ref[:, _C_FV:_C_SV])
    tail = _dot(h, w_ref[:, _C_SV:_IN_COLS])
    for k, ref in enumerate((kc_ref, kw_ref, vc_ref)):
        ref[...] = keys[:, k * LANES:(k + 1) * LANES].astype(ref.dtype)

    lane = lax.broadcasted_iota(jnp.int32, (tm, LANES), 1)
    low = lane < HEAD_DIM

    def spread(ref, cols, fill):
        for p in range(cols.shape[-1] // LANES):
            r = cols[:, p * LANES:(p + 1) * LANES]
            ref[:, 2 * p * LANES:(2 * p + 1) * LANES] = jnp.where(low, r, fill).astype(ref.dtype)
            ref[:, (2 * p + 1) * LANES:(2 * p + 2) * LANES] = jnp.where(
                low, _swap_halves(r), fill).astype(ref.dtype)

    for p in range(nsa_q.shape[-1] // LANES):
        r = nsa_q[:, p * LANES:(p + 1) * LANES]
        swapped = _swap_halves(r)
        nq_ref[:, 2 * p * LANES:(2 * p + 1) * LANES] = jnp.where(low, r, swapped).astype(
            nq_ref.dtype)
        nq_ref[:, (2 * p + 1) * LANES:(2 * p + 2) * LANES] = jnp.where(low, swapped, r).astype(
            nq_ref.dtype)

    key_blk = lax.shift_right_logical(
        s_idx * tm + lax.broadcasted_iota(jnp.int32, (tm, LANES), 0), int(math.log2(SEL_BLOCK)))
    spread(ks_ref, keys[:, 3 * LANES:4 * LANES],
           jnp.where(lane == HEAD_DIM + key_blk, 1.0, 0.0))
    spread(fv_ref, fox_v, 1.0)
    spread(vs_ref, tail[:, 0:LANES], 1.0)
    spread(vw_ref, tail[:, LANES:2 * LANES], 1.0)

    z = tail[:, 2 * LANES:3 * LANES] + bf_ref[...]
    is_f = lane < FOX_HEADS
    log_f = jnp.minimum(z, 0.0) - jnp.log(1.0 + jnp.exp(-jnp.abs(z)))
    log_f = jnp.where(is_f, log_f, 0.0)

    row = lax.broadcasted_iota(jnp.int32, (tm, tm), 0)
    col = lax.broadcasted_iota(jnp.int32, (tm, tm), 1)
    tri = jnp.where(row >= col, 1.0, 0.0).astype(BF16)
    sums = _dot(tri, jnp.concatenate(_split3(log_f), axis=1))
    local = sums[:, 0:LANES] + sums[:, LANES:2 * LANES] + sums[:, 2 * LANES:3 * LANES]

    @pl.when(s_idx == 0)
    def _():
        carry_ref[...] = jnp.zeros_like(carry_ref)

    cum = local + carry_ref[...]
    carry_ref[...] = cum[tm - 1:tm, :]
    misc = jnp.where(is_f, cum * LOG2E, _sigmoid(z))
    misc_ref[...] = misc
    misct_ref[...] = misc.T


def _in_proj(x, mod, w_r, bf_row):
    B, S, D = x.shape
    tm = ROW_TILE
    widths = (512, 512, 2 * (_C_K3 - _C_NQ), LANES, LANES, LANES, 2 * (_C_FV - _C_SK),
              2 * (_C_SV - _C_FV),
              2 * (_C_WV - _C_SV), 2 * (_C_MISC - _C_WV))
    wide = lambda w: pl.BlockSpec((None, tm, w), lambda b, s: (b, s, 0))
    out_shape = ([jax.ShapeDtypeStruct((B, S, w), BF16) for w in widths]
                 + [jax.ShapeDtypeStruct((B, S, LANES), F32),
                    jax.ShapeDtypeStruct((B, LANES, S), F32)])
    out_specs = ([wide(w) for w in widths]
                 + [wide(LANES), pl.BlockSpec((None, LANES, tm), lambda b, s: (b, 0, s))])
    return pl.pallas_call(
        _in_proj_kernel,
        grid=(B, S // tm),
        in_specs=[pl.BlockSpec((None, tm, D), lambda b, s: (b, s, 0)),
                  pl.BlockSpec((None, 6, D), lambda b, s: (b, 0, 0)),
                  pl.BlockSpec((D, _IN_COLS), lambda b, s: (0, 0)),
                  pl.BlockSpec((1, LANES), lambda b, s: (0, 0))],
        out_specs=out_specs,
        out_shape=out_shape,
        scratch_shapes=[pltpu.VMEM((1, LANES), F32)],
        compiler_params=pltpu.CompilerParams(dimension_semantics=("parallel", "arbitrary"),
                                             vmem_limit_bytes=VMEM_LIMIT),
        name="in_proj",
    )(x, mod, w_r, bf_row)


def _softmax_weights(s, m):
    return jnp.exp2((s - m).astype(BF16))


def _flash_update(carry, s, vt):
    m, acc = carry
    m_new = jnp.maximum(m, jnp.max(s, axis=-1, keepdims=True))
    alpha = jnp.exp2(m - m_new)
    p = _softmax_weights(s, m_new)
    rows = acc.shape[0]
    acc = alpha.reshape(rows, 1) * acc + _dot(p.reshape(rows, s.shape[-1]), vt)
    return m_new, acc


def _normalize(acc):
    return acc / _swap_halves(acc)


def _fox_kernel(q_ref, k_ref, v_ref, ck_ref, o_ref):
    i = pl.program_id(2)
    tq = q_ref.shape[0]
    tk = ck_ref.shape[2]
    q2 = q_ref[...].astype(F32)
    lane = lax.broadcasted_iota(jnp.int32, (tq, LANES), 1)
    low = lane < HEAD_DIM
    halves = (low, jnp.logical_not(low))
    qh = [jnp.where(h, q2, 0.0).astype(BF16) for h in halves]
    col_minus_row = (lax.broadcasted_iota(jnp.int32, (tq, tk), 1)
                     - lax.broadcasted_iota(jnp.int32, (tq, tk), 0))

    def step(jj, carry, diagonal):
        k0 = jj * tk
        kt = k_ref[pl.ds(k0, tk), :]
        new = []
        for hh in range(2):
            s = _dot_nt(qh[hh], kt) - ck_ref[hh, pl.ds(jj, 1), :]
            if diagonal:
                s = jnp.where(col_minus_row <= 0, s, NEG_BIG)
            vt = v_ref[pl.ds(k0, tk), hh * LANES:(hh + 1) * LANES]
            new.append(_flash_update(carry[hh], s, vt))
        return tuple(new)

    init = tuple((jnp.full((tq, 1), NEG_BIG, F32), jnp.zeros((tq, LANES), F32))
                 for _ in range(2))

    assert tq == tk
    for n_full in range(k_ref.shape[0] // tk):
        @pl.when(i == n_full)
        def _(n_full=n_full):
            carry = init
            for jj in range(n_full):
                carry = step(jj, carry, False)
            carry = step(n_full, carry, True)
            o_ref[...] = jnp.where(low, _normalize(carry[0][1]),
                                   _swap_halves(_normalize(carry[1][1]))).astype(o_ref.dtype)


def _fox(fq, fk, fv, cum_row):
    B, S, W = fq.shape
    tq = tk = FOX_TILE
    n_pairs = W // LANES
    cum_row = cum_row.reshape(B, n_pairs, 2, S // tk, tk)
    return pl.pallas_call(
        _fox_kernel,
        grid=(B, n_pairs, S // tq),
        in_specs=[pl.BlockSpec((None, tq, LANES), lambda b, p, i: (b, i, p)),
                  pl.BlockSpec((None, S, LANES), lambda b, p, i: (b, 0, p)),
                  pl.BlockSpec((None, S, 2 * LANES), lambda b, p, i: (b, 0, p)),
                  pl.BlockSpec((None, None, 2, S // tk, tk), lambda b, p, i: (b, p, 0, 0, 0))],
        out_specs=pl.BlockSpec((None, tq, LANES), lambda b, p, i: (b, i, p)),
        out_shape=jax.ShapeDtypeStruct((B, S, W), BF16),
        compiler_params=pltpu.CompilerParams(
            dimension_semantics=("parallel", "parallel", "arbitrary"),
            vmem_limit_bytes=VMEM_LIMIT),
        name="fox",
    )(fq, fk, fv, cum_row)


def _compress_kernel(xk_ref, xv_ref, pk_ref, pv_ref, wak_ref, wbk_ref, wav_ref, wbv_ref,
                     w2k_ref, w2v_ref, ok_ref, ov_ref):
    n_rows = xk_ref.shape[0]
    for x_ref, p_ref, wa_ref, wb_ref, w2_ref, o_ref in (
            (xk_ref, pk_ref, wak_ref, wbk_ref, w2k_ref, ok_ref),
            (xv_ref, pv_ref, wav_ref, wbv_ref, w2v_ref, ov_ref)):
        x = x_ref[...].astype(F32)
        xa = (x + p_ref[0:1, :]).astype(BF16)
        xb = (x + p_ref[1:2, :]).astype(BF16)
        hb = _dot(xb, wb_ref[...])
        h1 = _dot(xa, wa_ref[...]) + pltpu.roll(hb, n_rows - 1, 0)
        o_ref[...] = _dot(_silu(h1).astype(BF16), w2_ref[...]).astype(o_ref.dtype)


def _compress(xk, xv, pk, pv, wak, wbk, wav, wbv, w2k, w2v):
    B, R, C = xk.shape
    xspec = pl.BlockSpec((None, R, C), lambda b: (b, 0, 0))
    full = lambda a: pl.BlockSpec(a.shape, lambda b: (0,) * a.ndim)
    ospec = pl.BlockSpec((None, R, LANES), lambda b: (b, 0, 0))
    return pl.pallas_call(
        _compress_kernel,
        grid=(B,),
        in_specs=[xspec, xspec] + [full(a) for a in (pk, pv, wak, wbk, wav, wbv, w2k, w2v)],
        out_specs=[ospec, ospec],
        out_shape=[jax.ShapeDtypeStruct((B, R, LANES), BF16)] * 2,
        compiler_params=pltpu.CompilerParams(dimension_semantics=("parallel",),
                                             vmem_limit_bytes=VMEM_LIMIT),
        name="compress",
    )(xk, xv, pk, pv, wak, wbk, wav, wbv, w2k, w2v)


def _rank_rows(score):
    n = score.shape[0]
    j = lax.broadcasted_iota(jnp.int32, score.shape, 0)
    rank = jnp.zeros(score.shape, jnp.int32)
    for i in range(n):
        si = score[i:i + 1, :]
        beats = (si > score) | ((si == score) & (j > i))
        rank = rank + jnp.where(beats, 1, 0)
    return rank


def _dup_head(q_ref, r):
    return q_ref[:, r * LANES:(r + 1) * LANES].astype(F32)


def _pack_heads(o_list, g):
    lane = lax.broadcasted_iota(jnp.int32, o_list[0].shape, 1)
    in_g = (lane >= g * HEAD_DIM) & (lane < (g + 1) * HEAD_DIM)
    both = []
    for o in o_list:
        om = jnp.where(in_g, o, 0.0)
        both.append(om + _swap_halves(om))
    pairs = [jnp.where(lane < HEAD_DIM, both[2 * p], both[2 * p + 1]) for p in range(2)]
    return jnp.concatenate(pairs, axis=1)


def _group_gates(misc, g):
    w = 3 * NSA_GQA
    gates = misc[:, FOX_HEADS:FOX_HEADS + w]
    for other in range(1, NSA_GROUPS):
        gates = jnp.where(g == other, misc[:, FOX_HEADS + other * w:FOX_HEADS + (other + 1) * w],
                          gates)
    return gates


def _cmp_sel_kernel(q_ref, kc_ref, vc_ref, bias_ref, gate_ref, ovl_ref, oc_ref, sel_ref):
    g = pl.program_id(1)
    i = pl.program_id(2)
    tq = q_ref.shape[0]
    n_pad = kc_ref.shape[0]
    lane = lax.broadcasted_iota(jnp.int32, (n_pad, LANES), 1)
    in_g = (lane >= g * HEAD_DIM) & (lane < (g + 1) * HEAD_DIM)
    kc = jnp.where(in_g, kc_ref[...].astype(F32), 0.0).astype(BF16)
    vc = vc_ref[...]
    t = i * tq + lax.broadcasted_iota(jnp.int32, (tq, n_pad), 0)
    n = lax.broadcasted_iota(jnp.int32, (tq, n_pad), 1)
    valid = t >= n * CMP_STRIDE + (CMP_BLOCK - 1)
    gates = _group_gates(gate_ref[...], g)
    p_sum = jnp.zeros((tq, n_pad), F32)
    outs = []
    for r in range(NSA_GQA):
        qr = q_ref[:, r * LANES:(r + 1) * LANES]
        s = _dot_nt(qr, kc)
        s = jnp.where(valid, s + bias_ref[r], NEG_BIG)
        m = jnp.max(s, axis=-1, keepdims=True)
        p = jnp.exp2(s - m)
        p = p / jnp.sum(p, axis=-1, keepdims=True)
        p = jnp.where(valid, p, 0.0)
        p_sum = p_sum + p
        outs.append(_dot(p.astype(BF16), vc) * gates[:, 3 * r:3 * r + 1])
    oc_ref[...] = _pack_heads(outs, g).astype(oc_ref.dtype)

    ovl = ovl_ref[...]
    hi, mid, lo = _split3(p_sum)
    imp = _dot_nt(ovl, hi) + _dot_nt(ovl, mid) + _dot_nt(ovl, lo)
    n_blk = imp.shape[0]
    j = lax.broadcasted_iota(jnp.int32, (n_blk, tq), 0)
    qb = jnp.right_shift(i * tq + lax.broadcasted_iota(jnp.int32, (n_blk, tq), 1),
                         int(math.log2(SEL_BLOCK)))
    forced = (j == 0) | (j == qb) | (j == qb - 1)
    causal = j <= qb
    score = jnp.where(causal, imp + jnp.where(forced, FORCE_SCORE, 0.0), -FORCE_SCORE)
    chosen = (_rank_rows(score) < N_SEL) & causal
    sel = jnp.where(chosen, 1.0, 0.0)
    sel = jnp.concatenate([sel, jnp.zeros((LANES - n_blk, tq), F32)], axis=0)
    sel_ref[...] = sel.T


def _cmp_sel(nq, kcmp, vcmp, bias_c, gates_g, ovl_t):
    B, S, _ = nq.shape
    tq = CMP_TILE
    n_pad = kcmp.shape[1]
    return pl.pallas_call(
        _cmp_sel_kernel,
        grid=(B, NSA_GROUPS, S // tq),
        in_specs=[pl.BlockSpec((None, tq, NSA_GQA * LANES), lambda b, g, i: (b, i, g)),
                  pl.BlockSpec((None, n_pad, LANES), lambda b, g, i: (b, 0, 0)),
                  pl.BlockSpec((None, n_pad, LANES), lambda b, g, i: (b, 0, 0)),
                  pl.BlockSpec((NSA_GQA, tq, n_pad), lambda b, g, i: (g, i, 0)),
                  pl.BlockSpec((None, tq, LANES), lambda b, g, i: (b, i, 0)),
                  pl.BlockSpec(ovl_t.shape, lambda b, g, i: (0, 0))],
        out_specs=[pl.BlockSpec((None, tq, 2 * LANES), lambda b, g, i: (b, i, g)),
                   pl.BlockSpec((None, None, tq, LANES), lambda b, g, i: (b, g, i, 0))],
        out_shape=[jax.ShapeDtypeStruct((B, S, NSA_HEADS * HEAD_DIM), BF16),
                   jax.ShapeDtypeStruct((B, NSA_GROUPS, S, LANES), F32)],
        compiler_params=pltpu.CompilerParams(
            dimension_semantics=("parallel", "parallel", "arbitrary"),
            vmem_limit_bytes=VMEM_LIMIT),
        name="cmp_sel",
    )(nq, kcmp, vcmp, bias_c, gates_g, ovl_t)


def _nsa_kernel(q_ref, ks_ref, vs_ref, kw_ref, vw_ref, sel_ref, w4_ref, gate_ref, oc_ref,
                o_ref):
    for tile in range(ks_ref.shape[0] // q_ref.shape[0]):
        @pl.when(pl.program_id(2) == tile)
        def _(tile=tile):
            _nsa_tile(tile, q_ref, ks_ref, vs_ref, kw_ref, vw_ref, sel_ref, w4_ref, gate_ref,
                      oc_ref, o_ref)


def _nsa_tile(i, q_ref, ks_ref, vs_ref, kw_ref, vw_ref, sel_ref, w4_ref, gate_ref, oc_ref, o_ref):
    g = pl.program_id(0)
    tq = q_ref.shape[0]
    tk = w4_ref.shape[-1]
    H = NSA_GQA
    lane = lax.broadcasted_iota(jnp.int32, (tq, LANES), 1)
    in_g = (lane >= g * HEAD_DIM) & (lane < (g + 1) * HEAD_DIM)
    heads = [_dup_head(q_ref, r) for r in range(H)]
    qs = jnp.concatenate([jnp.where(in_g, hd, 0.0).astype(BF16) for hd in heads], axis=0)

    blk_mask = _swap_halves((sel_ref[...] - 1.0) * (-NEG_BIG))
    qsel = jnp.concatenate(
        [jnp.where(lane < HEAD_DIM, hd, blk_mask).astype(BF16) for hd in heads], axis=0)

    def sel_step(jj, carry, diagonal):
        k0 = jj * tk
        ahead = i * tq - k0
        width = min(ahead + tq, tk) if diagonal else tk
        rows = pl.ds(k0, width)
        s = (_dot_nt(qsel, ks_ref[rows, :]).reshape(H, tq, width)
             + w4_ref[:, min(i - 2 * jj, 3), :, 0:width])
        if diagonal:
            cmr = (lax.broadcasted_iota(jnp.int32, (tq, width), 1)
                   - lax.broadcasted_iota(jnp.int32, (tq, width), 0))
            s = jnp.where((cmr <= ahead)[None], s, NEG_BIG)
        return _flash_update(carry, s, vs_ref[rows, :])

    carry = (jnp.full((H, tq, 1), NEG_BIG, F32), jnp.zeros((H * tq, LANES), F32))
    n_full = (i * tq) // tk
    for jj in range(n_full):
        carry = sel_step(jj, carry, False)
    _, acc_s = sel_step(n_full, carry, True)

    old = [max(i - d, 0) * tq for d in (2, 1)]
    now = i * tq
    k_old = jnp.concatenate([kw_ref[pl.ds(st, tq), :] for st in old], axis=0)
    v_old = jnp.concatenate([vw_ref[pl.ds(st, tq), :] for st in old], axis=0)
    col = lax.broadcasted_iota(jnp.int32, (tq, 2 * tq), 1)
    cmr = col - lax.broadcasted_iota(jnp.int32, (tq, 2 * tq), 0)
    never = 4 * tq
    ok_old = (((col < tq) & (cmr > (0 if i >= 2 else never)))
              | (col >= (tq if i >= 1 else never)))
    s_old = jnp.where(ok_old[None], _dot_nt(qs, k_old).reshape(H, tq, 2 * tq) + w4_ref[:, 2],
                      NEG_BIG)
    s_now = jnp.where((cmr[:, 0:tq] <= 0)[None],
                      _dot_nt(qs, kw_ref[pl.ds(now, tq), :]).reshape(H, tq, tq)
                      + w4_ref[:, 0, :, 0:tq], NEG_BIG)
    m_w = jnp.maximum(jnp.max(s_old, axis=-1, keepdims=True),
                      jnp.max(s_now, axis=-1, keepdims=True))
    acc_w = (_dot(_softmax_weights(s_old, m_w).reshape(H * tq, 2 * tq), v_old)
             + _dot(_softmax_weights(s_now, m_w).reshape(H * tq, tq), vw_ref[pl.ds(now, tq), :]))

    gates = _group_gates(gate_ref[...], g)
    o_s = _normalize(acc_s)
    o_w = _normalize(acc_w)
    outs = []
    for r in range(H):
        sl = slice(r * tq, (r + 1) * tq)
        outs.append(o_s[sl] * gates[:, 3 * r + 1:3 * r + 2] + o_w[sl] * gates[:, 3 * r + 2:3 * r + 3])
    lane = lax.broadcasted_iota(jnp.int32, (tq, LANES), 1)
    pairs = [jnp.where(lane < HEAD_DIM, outs[2 * p], _swap_halves(outs[2 * p + 1]))
             for p in range(H // 2)]
    o_ref[...] = (jnp.concatenate(pairs, axis=1) + oc_ref[...].astype(F32)).astype(o_ref.dtype)


def _nsa(nq, ks, vs, kw, vw, sel, w4, misc, oc):
    B, S, _ = nq.shape
    tq = ATT_TILE
    tk = w4.shape[-1]
    both = lambda: pl.BlockSpec((None, S, LANES), lambda g, b, i: (b, 0, 0))
    mine = lambda: pl.BlockSpec((None, S, LANES), lambda g, b, i: (b, 0, g))
    return pl.pallas_call(
        _nsa_kernel,
        grid=(NSA_GROUPS, B, S // tq),
        in_specs=[pl.BlockSpec((None, tq, NSA_GQA * LANES), lambda g, b, i: (b, i, g)),
                  mine(), mine(), both(), mine(),
                  pl.BlockSpec((None, None, tq, LANES), lambda g, b, i: (b, g, i, 0)),
                  pl.BlockSpec((NSA_GQA, 4, tq, tk), lambda g, b, i: (g, 0, 0, 0)),
                  pl.BlockSpec((None, tq, LANES), lambda g, b, i: (b, i, 0)),
                  pl.BlockSpec((None, tq, 2 * LANES), lambda g, b, i: (b, i, g))],
        out_specs=pl.BlockSpec((None, tq, 2 * LANES), lambda g, b, i: (b, i, g)),
        out_shape=jax.ShapeDtypeStruct((B, S, NSA_HEADS * HEAD_DIM), BF16),
        compiler_params=pltpu.CompilerParams(
            dimension_semantics=("parallel", "parallel", "arbitrary"),
            vmem_limit_bytes=VMEM_LIMIT),
        name="nsa",
    )(nq, ks, vs, kw, vw, sel, w4, misc, oc)


def _layer_norm(y, g, b):
    mu = jnp.mean(y, axis=-1, keepdims=True)
    yc = y - mu
    var = jnp.mean(yc * yc, axis=-1, keepdims=True)
    return yc * lax.rsqrt(var + LN_EPS) * g + b


def _top_rows(score, k):
    n = score.shape[0]
    idx = lax.broadcasted_iota(jnp.int32, score.shape, 0).astype(F32)
    alive = jnp.ones(score.shape, F32)
    for _ in range(k):
        live = alive > 0.0
        best = jnp.max(jnp.where(live, score, -jnp.inf), axis=0, keepdims=True)
        first = jnp.min(jnp.where(live & (score == best), idx, float(n)), axis=0, keepdims=True)
        alive = jnp.where(idx == first, 0.0, alive)
    return alive == 0.0


def _router_gates_t(h2, wr_t, eb_col):
    tm = h2.shape[0]
    h_hi, h_lo, _ = _split3(h2)
    w_hi, w_lo, _ = _split3(wr_t)
    logit = _dot_nt(w_hi, h_hi) + _dot_nt(w_hi, h_lo) + _dot_nt(w_lo, h_hi)
    scores = _sigmoid(logit)
    biased = scores + eb_col
    e_in = lax.broadcasted_iota(jnp.int32, (GROUP_SIZE, tm), 0).astype(F32)
    gs_rows = []
    for gi in range(N_EXPERT_GROUPS):
        grp = biased[gi * GROUP_SIZE:(gi + 1) * GROUP_SIZE, :]
        m1 = jnp.max(grp, axis=0, keepdims=True)
        first = jnp.min(jnp.where(grp == m1, e_in, float(GROUP_SIZE)), axis=0, keepdims=True)
        m2 = jnp.max(jnp.where(e_in == first, -jnp.inf, grp), axis=0, keepdims=True)
        gs_rows.append(m1 + m2)
    gscore = jnp.concatenate(gs_rows, axis=0)
    g_keep = _rank_rows(gscore) < TOPK_GROUPS
    keep = jnp.concatenate(
        [jnp.broadcast_to(g_keep[gi:gi + 1, :], (GROUP_SIZE, tm)) for gi in range(N_EXPERT_GROUPS)],
        axis=0)
    masked = jnp.where(keep, biased, -jnp.inf)
    chosen = _top_rows(masked, TOP_K)
    w = jnp.where(chosen, scores, 0.0)
    return w / jnp.sum(w, axis=0, keepdims=True) * ROUTED_SCALE


def _out_proj_kernel(alpha, of_ref, on_ref, x_ref, mod_ref, w_ref, lg_ref, lb_ref, wr_ref,
                     eb_ref, x1_ref, h2_ref, gate_ref, gate_t_ref):
    half = of_ref.shape[1]
    mod = mod_ref[...]
    mixed = _dot(of_ref[...], w_ref[0:half, :]) + _dot(on_ref[...], w_ref[half:2 * half, :])
    y = alpha * x_ref[...] + mod[2:3, :] * mixed
    x1 = _layer_norm(y, lg_ref[...], lb_ref[...])
    x1_ref[...] = x1
    h2 = x1 * (1.0 + mod[4:5, :]) + mod[3:4, :]
    h2_ref[...] = h2.astype(h2_ref.dtype)
    gates_t = _router_gates_t(h2, wr_ref[...], eb_ref[...])
    gate_t_ref[...] = gates_t
    tm = h2.shape[0]
    gates_t = jnp.concatenate([gates_t, jnp.zeros((LANES - N_EXPERTS, tm), F32)], axis=0)
    gate_ref[...] = gates_t.T


def _out_proj(alpha, o_fox, o_nsa, x, mod, w_out, ln_g, ln_b, wr_t, eb_col):
    B, S, D = x.shape
    tm = ROW_TILE
    half = o_fox.shape[-1]
    row = lambda a: pl.BlockSpec(a.shape, lambda b, s: (0, 0))
    return pl.pallas_call(
        functools.partial(_out_proj_kernel, alpha),
        grid=(B, S // tm),
        in_specs=[pl.BlockSpec((None, tm, half), lambda b, s: (b, s, 0)),
                  pl.BlockSpec((None, tm, half), lambda b, s: (b, s, 0)),
                  pl.BlockSpec((None, tm, D), lambda b, s: (b, s, 0)),
                  pl.BlockSpec((None, 6, D), lambda b, s: (b, 0, 0)),
                  row(w_out), row(ln_g), row(ln_b), row(wr_t), row(eb_col)],
        out_specs=[pl.BlockSpec((None, tm, D), lambda b, s: (b, s, 0)),
                   pl.BlockSpec((None, tm, D), lambda b, s: (b, s, 0)),
                   pl.BlockSpec((None, tm, LANES), lambda b, s: (b, s, 0)),
                   pl.BlockSpec((N_EXPERTS, tm), lambda b, s: (0, b * (S // tm) + s))],
        out_shape=[jax.ShapeDtypeStruct((B, S, D), F32),
                   jax.ShapeDtypeStruct((B, S, D), BF16),
                   jax.ShapeDtypeStruct((B, S, LANES), F32),
                   jax.ShapeDtypeStruct((N_EXPERTS, B * S), F32)],
        compiler_params=pltpu.CompilerParams(dimension_semantics=("parallel", "parallel"),
                                             vmem_limit_bytes=VMEM_LIMIT),
        name="out_proj",
    )(o_fox, o_nsa, x, mod, w_out, ln_g, ln_b, wr_t, eb_col)


SORT_TILE = 256
ROW_ALIGN = 16
EXP_TILE = 1024
P_CHUNK = 256


def _strict_upper(n):
    return jnp.where(lax.broadcasted_iota(jnp.int32, (n, n), 0)
                     < lax.broadcasted_iota(jnp.int32, (n, n), 1), 1.0, 0.0).astype(BF16)


def _strict_lower(n):
    return jnp.where(lax.broadcasted_iota(jnp.int32, (n, n), 1)
                     < lax.broadcasted_iota(jnp.int32, (n, n), 0), 1.0, 0.0).astype(BF16)


def _local_rows_bound(ts):
    rows = TOP_K * ts + N_EXPERTS * (ROW_ALIGN - 1)
    return -(-rows // P_CHUNK) * P_CHUNK


def _piece_cols(ts):
    return -(-(_local_rows_bound(ts) // ROW_ALIGN) // LANES) * LANES


def _sorted_tiles_bound(T):
    rows = TOP_K * T + (T // SORT_TILE) * N_EXPERTS * (ROW_ALIGN - 1)
    return -(-rows // EXP_TILE) + N_EXPERTS


def _sorted_tiles_expected(T):
    groups = (T // SORT_TILE) * N_EXPERTS
    rows = TOP_K * T + groups * ((ROW_ALIGN - 1) / 2 + 2)
    return int(-(-rows // EXP_TILE) + math.ceil(0.65 * N_EXPERTS))


def _moe_meta_kernel(gt_ref, ptab_ref, loc_et_ref, np_et_ref, loc_te_ref, np_te_ref, tot_ref,
                     erow_ref, texp_ref, nused_ref):
    E, T = gt_ref.shape
    mask = jnp.where(gt_ref[...] > 0.0, 1.0, 0.0).astype(BF16)
    t_id = lax.shift_right_logical(lax.broadcasted_iota(jnp.int32, (T, LANES), 0),
                                   int(math.log2(SORT_TILE)))
    tind = jnp.where(t_id == lax.broadcasted_iota(jnp.int32, (T, LANES), 1), 1.0, 0.0)
    cnt = _dot(mask, tind.astype(BF16))
    n16 = jnp.floor((cnt + (ROW_ALIGN - 1.0)) * (1.0 / ROW_ALIGN))
    n16b = n16.astype(BF16)
    q = EXP_TILE // ROW_ALIGN
    len16 = jnp.sum(n16, axis=1, keepdims=True)
    pad16 = jnp.floor((len16 + (q - 1.0)) * (1.0 / q)) * q
    sl = _strict_lower(E)
    hi, mid, lo = _split3(jnp.broadcast_to(pad16, (E, LANES)))
    start16 = _dot(sl, hi) + _dot(sl, mid) + _dot(sl, lo)
    gdst16 = start16 + _dot(n16b, _strict_upper(LANES))
    loc16 = _dot(sl, n16b)

    def t(a):
        return jnp.concatenate([a, jnp.zeros((LANES - E, LANES), F32)], axis=0).T

    scale = float(ROW_ALIGN)
    loc_et_ref[...] = loc16 * scale
    np_et_ref[...] = n16 * scale
    loc_te_ref[...] = t(loc16) * scale
    np_te_ref[...] = t(n16) * scale
    tot_ref[...] = (jnp.sum(n16, axis=0, keepdims=True) * scale).astype(jnp.int32)

    n_t, n_blk = ptab_ref.shape
    blk = lax.broadcasted_iota(jnp.int32, (E, n_blk), 1).astype(F32)
    for tile in range(n_t):
        lo_c = loc16[:, tile:tile + 1]
        inside = (lo_c <= blk) & (blk < lo_c + n16[:, tile:tile + 1])
        dst = jnp.sum(jnp.where(inside, gdst16[:, tile:tile + 1] + (blk - lo_c), 0.0),
                      axis=0, keepdims=True)
        ptab_ref[tile:tile + 1, :] = (dst * scale).astype(jnp.int32)
    ends = jnp.concatenate([t(start16 + len16)[0:1, :], t(start16 + pad16)[0:1, :],
                            jnp.zeros((erow_ref.shape[0] - 2, LANES), F32)], axis=0)
    erow_ref[...] = (ends * scale).astype(jnp.int32)
    n_tab = texp_ref.shape[1]
    tile_row16 = (lax.broadcasted_iota(jnp.int32, (E, n_tab), 1) * q).astype(F32)
    owner = jnp.sum(jnp.where(start16[:, 0:1] <= tile_row16, 1.0, 0.0), axis=0, keepdims=True)
    texp_ref[...] = (owner - 1.0).astype(jnp.int32)
    n_used = jnp.sum(pad16, axis=0, keepdims=True) * (1.0 / q)
    nused_ref[...] = jnp.broadcast_to(n_used, (1, LANES)).astype(jnp.int32)


def _moe_meta(gates_t, n_tab):
    E, T = gates_t.shape
    i32 = jnp.int32
    return pl.pallas_call(
        _moe_meta_kernel,
        out_shape=[jax.ShapeDtypeStruct((T // SORT_TILE, _piece_cols(SORT_TILE)), i32),
                   jax.ShapeDtypeStruct((E, LANES), F32),
                   jax.ShapeDtypeStruct((E, LANES), F32),
                   jax.ShapeDtypeStruct((LANES, LANES), F32),
                   jax.ShapeDtypeStruct((LANES, LANES), F32),
                   jax.ShapeDtypeStruct((1, LANES), i32),
                   jax.ShapeDtypeStruct((8, LANES), i32),
                   jax.ShapeDtypeStruct((1, n_tab), i32),
                   jax.ShapeDtypeStruct((1, LANES), i32)],
        compiler_params=pltpu.CompilerParams(vmem_limit_bytes=VMEM_LIMIT),
        name="moe_meta",
    )(gates_t)


def _start_pieces(tile, ptab_s, tot_s, n_cols, make_copy):
    n_pieces = lax.shift_right_logical(tot_s[tile], int(math.log2(ROW_ALIGN)))
    group = 8

    def start(b):
        make_copy(pl.multiple_of(b * ROW_ALIGN, ROW_ALIGN),
                  pl.multiple_of(ptab_s[tile * n_cols + b], ROW_ALIGN)).start()

    def grouped(q, carry):
        for k in range(group):
            start(q * group + k)
        return carry

    def single(b, carry):
        start(b)
        return carry

    n_grouped = lax.shift_right_logical(n_pieces, int(math.log2(group)))
    lax.fori_loop(0, n_grouped, grouped, 0)
    lax.fori_loop(n_grouped * group, n_pieces, single, 0)


def _moe_sort_kernel(ptab_s, tot_s, lend_s, rend_s, gt_ref, gtok_ref, h_ref, locrow_ref,
                     nprow_ref, xs_hbm, buf, zbuf, sem, zsem):
    tau = pl.program_id(0)
    n_t = pl.num_programs(0)
    slot = lax.rem(tau, 2)
    E, ts = gt_ref.shape
    D = h_ref.shape[1]
    n_cols = _piece_cols(ts)

    def copies(tile, sl, wait):
        if wait:
            rows = pl.multiple_of(tot_s[tile], ROW_ALIGN)

            @pl.when(rows > 0)
            def _():
                pltpu.make_async_copy(buf.at[sl, pl.ds(0, rows)], xs_hbm.at[pl.ds(0, rows)],
                                      sem.at[sl]).wait()
        else:
            def make_copy(loc, dst):
                return pltpu.make_async_copy(buf.at[sl, pl.ds(loc, ROW_ALIGN)],
                                             xs_hbm.at[pl.ds(dst, ROW_ALIGN)], sem.at[sl])
            _start_pieces(tile, ptab_s, tot_s, n_cols, make_copy)

    z_rows = zbuf.shape[0]
    used_rows = rend_s[E - 1]
    n_spare = (xs_hbm.shape[0] - used_rows) // z_rows

    def spare_fill(wait):
        def body(c, carry):
            dst = pl.multiple_of(used_rows + c * z_rows, z_rows)
            cp = pltpu.make_async_copy(zbuf, xs_hbm.at[pl.ds(dst, z_rows)], zsem.at[1])
            if wait:
                cp.wait()
            else:
                cp.start()
            return carry

        lax.fori_loop(0, n_spare, body, 0)

    @pl.when(tau == 0)
    def _():
        zbuf[...] = jnp.zeros_like(zbuf)
        spare_fill(False)

    @pl.when(tau >= 2)
    def _():
        copies(tau - 2, slot, True)

    g = gt_ref[...]
    mask = g > 0.0
    maskb = jnp.where(mask, 1.0, 0.0).astype(BF16)
    pad = jnp.zeros((LANES - E, ts), F32)
    pos = jnp.where(mask, _dot(maskb, _strict_upper(ts)), -1.0)
    pos = jnp.concatenate([pos, pad], axis=0).astype(BF16)
    lo_row = locrow_ref[...]
    hi_row = lo_row + nprow_ref[...]
    h = jnp.concatenate([h_ref[...]] + list(_split3(gtok_ref[...])), axis=1)
    lane = lax.broadcasted_iota(jnp.int32, (P_CHUNK, LANES), 1)

    def chunk(c):
        r0 = c * P_CHUNK
        r = (r0 + lax.broadcasted_iota(jnp.int32, (P_CHUNK, LANES), 0)).astype(F32)
        inside = (lo_row <= r) & (r < hi_row)
        group = jnp.where(inside, 1.0, 0.0).astype(BF16)
        want = r[:, 0:1] - jnp.sum(jnp.where(inside, lo_row, 0.0), axis=1, keepdims=True)
        hit = _dot(group, pos) == want
        rows = _dot(jnp.where(hit, 1.0, 0.0).astype(BF16), h)
        buf[slot, pl.ds(r0, P_CHUNK), 0:D] = rows[:, 0:D].astype(buf.dtype)
        extra = jnp.zeros((P_CHUNK, LANES), F32)
        for k in range(3):
            mine = jnp.where(inside, rows[:, D + k * LANES:D + (k + 1) * LANES], 0.0)
            extra = jnp.where(lane == k, jnp.sum(mine, axis=1, keepdims=True), extra)
        buf[slot, pl.ds(r0, P_CHUNK), D:D + LANES] = extra.astype(buf.dtype)

    n_static = -(-(TOP_K * ts + E * (ROW_ALIGN // 2)) // P_CHUNK)
    for c in range(buf.shape[1] // P_CHUNK):
        if c < n_static:
            chunk(c)
        else:
            pl.when(tot_s[tau] > c * P_CHUNK)(functools.partial(chunk, c))
    copies(tau, slot, False)

    @pl.when(tau == n_t - 1)
    def _():
        @pl.when(n_t >= 2)
        def _():
            copies(tau - 1, 1 - slot, True)
        copies(tau, slot, True)
        spare_fill(True)

        sizes =[zbuf.shape[0] >> s for s in range(int(math.log2(zbuf.shape[0] // ROW_ALIGN)) + 1)]

        def fill(wait):
            def e_body(e, carry):
                start = lend_s[e]
                n = rend_s[e] - start
                off = start
                for size in sizes:
                    bit = jnp.bitwise_and(n, size)

                    @pl.when(bit != 0)
                    def _(off=off, size=size):
                        cp = pltpu.make_async_copy(
                            zbuf.at[pl.ds(0, size)],
                            xs_hbm.at[pl.ds(pl.multiple_of(off, ROW_ALIGN), size)], zsem.at[0])
                        if wait:
                            cp.wait()
                        else:
                            cp.start()

                    off = off + bit
                return carry

            lax.fori_loop(0, E, e_body, 0)

        fill(False)
        fill(True)


def _moe_sort(ptab, tot, lend, rend, gates_t, gates, h2, loc_te, np_te, n_rows):
    E, T = gates_t.shape
    D = h2.shape[1]
    ts = SORT_TILE
    grid_spec = pltpu.PrefetchScalarGridSpec(
        num_scalar_prefetch=4,
        grid=(T // ts,),
        in_specs=[pl.BlockSpec((E, ts), lambda t, *_: (0, t)),
                  pl.BlockSpec((ts, LANES), lambda t, *_: (t, 0)),
                  pl.BlockSpec((ts, D), lambda t, *_: (t, 0)),
                  pl.BlockSpec((None, 1, LANES), lambda t, *_: (t, 0, 0)),
                  pl.BlockSpec((None, 1, LANES), lambda t, *_: (t, 0, 0))],
        out_specs=pl.BlockSpec(memory_space=pl.ANY),
        scratch_shapes=[pltpu.VMEM((2, _local_rows_bound(ts), D + LANES), BF16),
                        pltpu.VMEM((EXP_TILE // 2, D + LANES), BF16),
                        pltpu.SemaphoreType.DMA((2,)),
                        pltpu.SemaphoreType.DMA((2,))])
    return pl.pallas_call(
        _moe_sort_kernel,
        grid_spec=grid_spec,
        out_shape=jax.ShapeDtypeStruct((n_rows, D + LANES), BF16),
        compiler_params=pltpu.CompilerParams(dimension_semantics=("arbitrary",),
                                             vmem_limit_bytes=VMEM_LIMIT),
        name="moe_sort",
    )(ptab, tot, lend, rend, gates_t, gates, h2, loc_te, np_te)


def _moe_expert_kernel(texp_s, nused_s, x_ref, wg_ref, wu_ref, wd_ref, y_ref, wgu_s, wd_s):
    i = pl.program_id(0)
    f = wd_ref.shape[0]

    @pl.when(i < nused_s[0])
    def _():
        @pl.when((i == 0) | (texp_s[i] != texp_s[jnp.maximum(i - 1, 0)]))
        def _():
            wgu_s[:, 0:f] = wg_ref[...].astype(BF16)
            wgu_s[:, f:2 * f] = wu_ref[...].astype(BF16)
            wd_s[...] = wd_ref[...].astype(BF16)

        d = wd_ref.shape[1]
        gate = jnp.sum(x_ref[:, d:].astype(F32), axis=1, keepdims=True)
        a = _dot(x_ref[:, 0:d], wgu_s[...])
        act = _silu(a[:, :f]) * a[:, f:] * gate
        y_ref[...] = _dot(act.astype(BF16), wd_s[...]).astype(y_ref.dtype)

    @pl.when(i >= nused_s[0])
    def _():
        y_ref[...] = jnp.zeros_like(y_ref)


def _moe_expert(texp, nused, xs, w_gate, w_up, w_down, n_tiles):
    n_rows, xw = xs.shape
    D, f = w_gate.shape[-2:]
    tm = EXP_TILE

    def tile(i, texp, nused):
        return jnp.maximum(jnp.minimum(i, nused[0] - 1), 0)

    grid_spec = pltpu.PrefetchScalarGridSpec(
        num_scalar_prefetch=2,
        grid=(n_tiles,),
        in_specs=[pl.BlockSpec((tm, xw), lambda i, te, nu: (tile(i, te, nu), 0)),
                  pl.BlockSpec((None, D, f), lambda i, te, nu: (te[tile(i, te, nu)], 0, 0)),
                  pl.BlockSpec((None, D, f), lambda i, te, nu: (te[tile(i, te, nu)], 0, 0)),
                  pl.BlockSpec((None, f, D), lambda i, te, nu: (te[tile(i, te, nu)], 0, 0))],
        out_specs=pl.BlockSpec((tm, D), lambda i, te, nu: (i, 0)),
        scratch_shapes=[pltpu.VMEM((D, 2 * f), BF16), pltpu.VMEM((f, D), BF16)])
    return pl.pallas_call(
        _moe_expert_kernel,
        grid_spec=grid_spec,
        out_shape=jax.ShapeDtypeStruct((n_rows, D), BF16),
        compiler_params=pltpu.CompilerParams(dimension_semantics=("arbitrary",),
                                             vmem_limit_bytes=VMEM_LIMIT),
        name="moe_expert",
    )(texp, nused, xs, w_gate, w_up, w_down)


def _moe_combine_kernel(alpha, ptab_s, tot_s, g_ref, loc_ref, np_ref, h_ref, x1_ref,
                        mod_ref, sgu_ref, sd_ref, lg_ref, lb_ref, y_hbm, o_ref, ybuf, acc_ref,
                        sem):
    tau = pl.program_id(0)
    n_t = pl.num_programs(0)
    slot = lax.rem(tau, 2)
    ts, n_lane = g_ref.shape
    E = loc_ref.shape[0]
    n_cols = _piece_cols(ts)

    def copies(tile, sl, wait):
        if wait:
            rows = pl.multiple_of(tot_s[tile], ROW_ALIGN)

            @pl.when(rows > 0)
            def _():
                pltpu.make_async_copy(y_hbm.at[pl.ds(0, rows)], ybuf.at[sl, pl.ds(0, rows)],
                                      sem.at[sl]).wait()
        else:
            def make_copy(loc, dst):
                return pltpu.make_async_copy(y_hbm.at[pl.ds(dst, ROW_ALIGN)],
                                             ybuf.at[sl, pl.ds(loc, ROW_ALIGN)], sem.at[sl])
            _start_pieces(tile, ptab_s, tot_s, n_cols, make_copy)

    @pl.when(tau == 0)
    def _():
        ybuf[...] = jnp.zeros_like(ybuf)
        copies(0, 0, False)

    @pl.when(tau + 1 < n_t)
    def _():
        copies(tau + 1, 1 - slot, False)

    g = g_ref[...]
    mask = g > 0.0
    maskb = jnp.where(mask, 1.0, 0.0).astype(BF16)
    pos = jnp.where(mask, _dot(_strict_lower(ts), maskb), -1.0).astype(BF16)
    lane = lax.broadcasted_iota(jnp.int32, loc_ref.shape, 1)
    lo_col = jnp.sum(jnp.where(lane == tau, loc_ref[...], 0.0), axis=1, keepdims=True)
    hi_col = lo_col + jnp.sum(jnp.where(lane == tau, np_ref[...], 0.0), axis=1, keepdims=True)

    f = sd_ref.shape[0]
    a = _dot(h_ref[...], sgu_ref[...])
    acc_ref[...] = _dot((_silu(a[:, :f]) * a[:, f:]).astype(BF16), sd_ref[...])

    copies(tau, slot, True)

    def chunk(c):
        r0 = c * P_CHUNK
        r = (r0 + lax.broadcasted_iota(jnp.int32, (E, P_CHUNK), 1)).astype(F32)
        inside = (lo_col <= r) & (r < hi_col)
        group = jnp.concatenate([jnp.where(inside, 1.0, 0.0),
                                 jnp.zeros((n_lane - E, P_CHUNK), F32)], axis=0).astype(BF16)
        want = r[0:1, :] - jnp.sum(jnp.where(inside, lo_col, 0.0), axis=0, keepdims=True)
        hit = _dot(pos, group) == want
        return _dot(jnp.where(hit, 1.0, 0.0).astype(BF16), ybuf[slot, pl.ds(r0, P_CHUNK), :])

    n_static = -(-(TOP_K * ts + E * (ROW_ALIGN // 2)) // P_CHUNK)
    acc_ref[...] += sum(chunk(c) for c in range(n_static))
    for c in range(n_static, ybuf.shape[1] // P_CHUNK):
        @pl.when(tot_s[tau] > c * P_CHUNK)
        def _(c=c):
            acc_ref[...] += chunk(c)

    y = alpha * x1_ref[...] + mod_ref[5:6, :] * acc_ref[...]
    o_ref[...] = _layer_norm(y, lg_ref[...], lb_ref[...])


def _moe_combine(alpha, ptab, tot, gates, loc_et, np_et, h2, x1, mod, sgu, sd, ln_g, ln_b, ys,
                 S):
    T, D = h2.shape
    ts = SORT_TILE
    per_b = S // ts
    row = lambda a: pl.BlockSpec(a.shape, lambda t, *_: (0, 0))
    grid_spec = pltpu.PrefetchScalarGridSpec(
        num_scalar_prefetch=2,
        grid=(T // ts,),
        in_specs=[pl.BlockSpec((ts, LANES), lambda t, *_: (t, 0)),
                  row(loc_et), row(np_et),
                  pl.BlockSpec((ts, D), lambda t, *_: (t, 0)),
                  pl.BlockSpec((ts, D), lambda t, *_: (t, 0)),
                  pl.BlockSpec((None, 6, D), lambda t, *_: (t // per_b, 0, 0)),
                  row(sgu), row(sd), row(ln_g), row(ln_b),
                  pl.BlockSpec(memory_space=pl.ANY)],
        out_specs=pl.BlockSpec((ts, D), lambda t, *_: (t, 0)),
        scratch_shapes=[pltpu.VMEM((2, _local_rows_bound(ts), D), BF16),
                        pltpu.VMEM((ts, D), F32),
                        pltpu.SemaphoreType.DMA((2,))])
    return pl.pallas_call(
        functools.partial(_moe_combine_kernel, alpha),
        grid_spec=grid_spec,
        out_shape=jax.ShapeDtypeStruct((T, D), F32),
        compiler_params=pltpu.CompilerParams(dimension_semantics=("arbitrary",),
                                             vmem_limit_bytes=VMEM_LIMIT),
        name="moe_combine",
    )(ptab, tot, gates, loc_et, np_et, h2, x1, mod, sgu, sd, ln_g, ln_b, ys)


def _moe(alpha, h2, x1, gates, gates_t, mod, w_gate, w_up, w_down, sgu, sd, ln_g, ln_b):
    B, S, D = x1.shape
    T = B * S
    n_t = T // SORT_TILE
    n_bound = _sorted_tiles_bound(T)
    n_tab = -(-n_bound // LANES) * LANES
    ptab, loc_et, np_et, loc_te, np_te, tot, erow, texp, nused = _moe_meta(gates_t, n_tab)
    ptab = ptab.reshape(-1)
    tot = tot[0, :n_t]
    h2 = h2.reshape(T, D)
    x1 = x1.reshape(T, D)
    gates = gates.reshape(T, LANES)
    loc_rows = loc_te[:n_t].reshape(n_t, 1, LANES)
    np_rows = np_te[:n_t].reshape(n_t, 1, LANES)

    def sized(n_tiles):
        def run():
            xs = _moe_sort(ptab, tot, erow[0, :N_EXPERTS], erow[1, :N_EXPERTS], gates_t, gates,
                           h2, loc_rows, np_rows, n_tiles * EXP_TILE)
            ys = _moe_expert(texp[0], nused[0, :1], xs, w_gate, w_up, w_down, n_tiles)
            return _moe_combine(alpha, ptab, tot, gates, loc_et, np_et, h2, x1, mod, sgu, sd,
                                ln_g, ln_b, ys, S)
        return run

    n_small = min(_sorted_tiles_expected(T), n_bound)
    out = lax.cond(nused[0, 0] <= n_small, sized(n_small), sized(n_bound))
    return out.reshape(B, S, D)


def _rearrange_w_in(w):
    d_in = w.shape[0]
    scale = HEAD_DIM ** -0.5 * LOG2E
    fq, fk, fv = w[:, 0:512], w[:, 512:1024], w[:, 1024:1536]
    ff = w[:, 1536:1544]
    nq = w[:, 1544:2056]
    kc, vc, ks, vs, kw, vw = (w[:, 2056 + k * LANES:2056 + (k + 1) * LANES] for k in range(6))
    ng = w[:, 2824:2848]
    pad = jnp.zeros((d_in, LANES - ff.shape[1] - ng.shape[1]), w.dtype)
    cols = [fq * scale, fk, nq * scale, kc, kw, vc, ks, fv, vs, vw, ff, ng, pad]
    return jnp.concatenate(cols, axis=1).astype(BF16)


def _compress_weights(pos, w1, w2):
    half = CMP_BLOCK // 2
    w1r = w1.reshape(2, half, HEAD_DIM, CMP_HIDDEN)
    zeros = jnp.zeros_like(w1r[0])
    def spread(part):
        g0 = jnp.stack([part, zeros], axis=1).reshape(half * 2 * HEAD_DIM, CMP_HIDDEN)
        g1 = jnp.stack([zeros, part], axis=1).reshape(half * 2 * HEAD_DIM, CMP_HIDDEN)
        return jnp.concatenate([g0, g1], axis=1).astype(BF16)
    wa, wb = spread(w1r[0]), spread(w1r[1])
    z2 = jnp.zeros_like(w2)
    w2bd = jnp.concatenate([jnp.concatenate([w2, z2], axis=1),
                            jnp.concatenate([z2, w2], axis=1)], axis=0).astype(BF16)
    posr = pos.reshape(2, half, 1, HEAD_DIM)
    posr = jnp.broadcast_to(posr, (2, half, NSA_GROUPS, HEAD_DIM)).reshape(2, half * 2 * HEAD_DIM)
    return posr, wa, wb, w2bd


@functools.lru_cache(maxsize=None)
def _static_tables(S):
    tq = ATT_TILE
    n_cmp = (S - CMP_BLOCK) // CMP_STRIDE + 1
    n_pad = S // CMP_STRIDE
    n_slc = S // SEL_BLOCK
    t = np.arange(S)[:, None]
    n = np.arange(n_pad)[None, :]
    bucket_c = _t5_bucket_np(t - (n * CMP_STRIDE + CMP_BLOCK - 1)).reshape(1, -1)
    d = (np.arange(4)[:, None, None] * tq + np.arange(tq)[None, :, None]
         - np.arange(2 * tq)[None, None, :])
    bucket_w = _t5_bucket_np(d).reshape(1, -1)
    cs = np.arange(n_pad)[None, :] * CMP_STRIDE
    sj = np.arange(n_slc)[:, None] * SEL_BLOCK
    ovl_t = ((cs < sj + SEL_BLOCK) & (cs + CMP_BLOCK > sj) & (np.arange(n_pad)[None, :] < n_cmp))
    return bucket_c, bucket_w, ovl_t.astype(np.float32)


def kernel(x, c, w_ada, b_ada, w_in, b_f, cmp_pos_k, cmp_w1_k, cmp_w2_k, cmp_pos_v, cmp_w1_v,
           cmp_w2_v, rel_bias, w_out, ln1_g, ln1_b, w_router, e_bias, w_gate, w_up, w_down,
           ws_gate, ws_up, ws_down, ln2_g, ln2_b):
    B, S, D = x.shape
    depth = w_ada.shape[0]
    alpha = (2 * depth) ** 0.25
    tq = ATT_TILE
    assert w_in.shape[-1] == 3 * FOX_HEADS * HEAD_DIM + FOX_HEADS + NSA_HEADS * HEAD_DIM \
        + 6 * NSA_GROUPS * HEAD_DIM + 3 * NSA_HEADS
    assert NSA_GROUPS * HEAD_DIM == LANES and S // CMP_STRIDE == LANES
    assert WINDOW == 2 * tq and S // SEL_BLOCK <= HEAD_DIM // 2
    assert S % FOX_TILE == 0 and S % CMP_TILE == 0 and (B * S) % SORT_TILE == 0
    assert w_gate.shape[1:] == (N_EXPERTS, D, w_down.shape[2])
    bucket_c, bucket_w, ovl_t = _static_tables(S)
    rel_bias_t = rel_bias.T * LOG2E
    bias_c = _bias_table(jnp.asarray(bucket_c), rel_bias_t).reshape(NSA_HEADS, S, S // CMP_STRIDE)
    w4 = _bias_table(jnp.asarray(bucket_w), rel_bias_t).reshape(NSA_HEADS, 4, tq, 2 * tq)
    ovl_t = jnp.asarray(ovl_t, BF16)

    for l in range(depth):
        mod = _ada(c, w_ada[l], b_ada[l]).reshape(B, 6, D)
        bf_row = jnp.zeros((1, LANES), F32).at[0, :FOX_HEADS].set(b_f[l])
        (fq, fk, nq, kc, kw, vc, ks, fv, vs, vw, misc, misc_t) = _in_proj(
            x, mod, _rearrange_w_in(w_in[l]), bf_row)

        o_fox = _fox(fq, fk, fv, misc_t[:, :FOX_HEADS, :])

        pk, wak, wbk, w2k = _compress_weights(cmp_pos_k[l], cmp_w1_k[l], cmp_w2_k[l])
        pv, wav, wbv, w2v = _compress_weights(cmp_pos_v[l], cmp_w1_v[l], cmp_w2_v[l])
        rows = S // CMP_STRIDE
        kcmp, vcmp = _compress(kc.reshape(B, rows, CMP_STRIDE * LANES),
                               vc.reshape(B, rows, CMP_STRIDE * LANES),
                               pk, pv, wak, wbk, wav, wbv, w2k, w2v)

        oc, sel = _cmp_sel(nq, kcmp, vcmp, bias_c, misc, ovl_t)
        o_nsa = _nsa(nq, ks, vs, kw, vw, sel, w4, misc, oc)

        x1, h2, gates, gates_t = _out_proj(
            alpha, o_fox, o_nsa, x, mod, w_out[l].astype(BF16), ln1_g[l].reshape(1, D),
            ln1_b[l].reshape(1, D), w_router[l].T, e_bias[l].reshape(N_EXPERTS, 1))

        sgu = jnp.concatenate([ws_gate[l], ws_up[l]], axis=-1).astype(BF16)
        x = _moe(alpha, h2, x1, gates, gates_t, mod, w_gate[l], w_up[l], w_down[l], sgu,
                 ws_down[l].astype(BF16), ln2_g[l].reshape(1, D), ln2_b[l].reshape(1, D))
    return x
```

```python
import functools
import math

import jax
import jax.numpy as jnp
import numpy as np
from jax import lax
from jax.experimental import pallas as pl
from jax.experimental.pallas import tpu as pltpu

F32 = jnp.float32
BF16 = jnp.bfloat16

HEAD_DIM = 64
FOX_HEADS = 8
NSA_HEADS = 8
NSA_GQA = 4
NSA_GROUPS = NSA_HEADS // NSA_GQA
CMP_BLOCK = 32
CMP_STRIDE = 16
CMP_HIDDEN = 256
SEL_BLOCK = 64
N_SEL = 16
WINDOW = 512
N_BUCKETS = 32
MAX_DISTANCE = 128
N_EXPERTS = 64
N_EXPERT_GROUPS = 8
GROUP_SIZE = N_EXPERTS // N_EXPERT_GROUPS
TOPK_GROUPS = 4
TOP_K = 8
ROUTED_SCALE = 2.5
LN_EPS = 1e-5
NEG_BIG = -1e30
FORCE_SCORE = 1e4

LANES = 128
ATT_TILE = 256
FOX_TILE = 512
CMP_TILE = 512
ROW_TILE = 512
VMEM_LIMIT = 48 * 1024 * 1024

NT_DIMS = (((1,), (1,)), ((), ()))


def _dot(a, b):
    return jnp.dot(a, b, preferred_element_type=F32)


def _dot_nt(a, b):
    return lax.dot_general(a, b, NT_DIMS, preferred_element_type=F32)


def _split3(x):
    hi = x.astype(BF16)
    r1 = x - hi.astype(F32)
    mid = r1.astype(BF16)
    lo = (r1 - mid.astype(F32)).astype(BF16)
    return hi, mid, lo


def _silu(x):
    return x / (1.0 + jnp.exp(-x))


def _sigmoid(x):
    return 1.0 / (1.0 + jnp.exp(-x))


def _swap_halves(x):
    return pltpu.roll(x, HEAD_DIM, 1)


def _t5_bucket_np(dist):
    n = np.maximum(dist, 0)
    max_exact = N_BUCKETS // 2
    nf = np.maximum(n, 1).astype(np.float32)
    large = max_exact + (np.log(nf / max_exact) / math.log(MAX_DISTANCE / max_exact)
                         * (N_BUCKETS - max_exact)).astype(np.int32)
    large = np.minimum(large, N_BUCKETS - 1)
    return np.where(n < max_exact, n, large).astype(np.int32)


def _ada_kernel(c_ref, w_ref, b_ref, o_ref):
    c = c_ref[...]
    o_ref[...] = jnp.dot(_silu(c), w_ref[...], preferred_element_type=F32,
                         precision=lax.Precision.HIGHEST) + b_ref[...]


def _ada(c, w_ada, b_ada):
    B, D = c.shape
    n_out = w_ada.shape[1]
    tn = 1024
    return pl.pallas_call(
        _ada_kernel,
        grid=(n_out // tn,),
        in_specs=[pl.BlockSpec((B, D), lambda j: (0, 0)),
                  pl.BlockSpec((D, tn), lambda j: (0, j)),
                  pl.BlockSpec((1, tn), lambda j: (0, j))],
        out_specs=pl.BlockSpec((B, tn), lambda j: (0, j)),
        out_shape=jax.ShapeDtypeStruct((B, n_out), F32),
        compiler_params=pltpu.CompilerParams(dimension_semantics=("arbitrary",),
                                             vmem_limit_bytes=VMEM_LIMIT),
        name="ada",
    )(c, w_ada, b_ada.reshape(1, n_out))


def _bias_table_kernel(bkt_ref, rbt_ref, o_ref):
    bkt = bkt_ref[...]
    k = lax.broadcasted_iota(jnp.int32, (N_BUCKETS, bkt.shape[1]), 0)
    onehot = jnp.where(k == bkt, 1.0, 0.0).astype(BF16)
    hi, mid, lo = _split3(rbt_ref[...])
    o_ref[...] = _dot(hi, onehot) + _dot(mid, onehot) + _dot(lo, onehot)


def _bias_table(bucket, rel_bias_t):
    n = bucket.shape[1]
    chunk = 32768
    n_heads = rel_bias_t.shape[0]
    return pl.pallas_call(
        _bias_table_kernel,
        grid=(n // chunk,),
        in_specs=[pl.BlockSpec((1, chunk), lambda j: (0, j)),
                  pl.BlockSpec(rel_bias_t.shape, lambda j: (0, 0))],
        out_specs=pl.BlockSpec((n_heads, chunk), lambda j: (0, j)),
        out_shape=jax.ShapeDtypeStruct((n_heads, n), F32),
        compiler_params=pltpu.CompilerParams(dimension_semantics=("parallel",),
                                             vmem_limit_bytes=VMEM_LIMIT),
        name="bias_table",
    )(bucket, rel_bias_t)


_C_FQ, _C_FK, _C_NQ = 0, 512, 1024
_C_K3 = 1536
_C_SK = 1920
_C_FV = 2048
_C_SV = 2560
_C_WV = 2688
_C_MISC = 2816
_IN_COLS = 2944
LOG2E = math.log2(math.e)


def _in_proj_kernel(x_ref, mod_ref, w_ref, bf_ref, fq_ref, fk_ref, nq_ref, kc_ref, kw_ref,
                    vc_ref, ks_ref, fv_ref, vs_ref, vw_ref, misc_ref, misct_ref, carry_ref):
    s_idx = pl.program_id(1)
    tm = x_ref.shape[0]
    mod = mod_ref[...]
    h = (x_ref[...] * (1.0 + mod[1:2, :]) + mod[0:1, :]).astype(BF16)

    for ref, c0 in ((fq_ref, _C_FQ), (fk_ref, _C_FK)):
        ref[...] = _dot(h, w_ref[:, c0:c0 + 512]).astype(ref.dtype)
    nsa_q = _dot(h, w_ref[:, _C_NQ:_C_K3])
    keys = _dot(h, w_ref[:, _C_K3:_C_FV])
    fox_v = _dot(h, w_ref[:, _C_FV:_C_SV])
    tail = _dot(h, w_ref[:, _C_SV:_IN_COLS])
    for k, ref in enumerate((kc_ref, kw_ref, vc_ref)):
        ref[...] = keys[:, k * LANES:(k + 1) * LANES].astype(ref.dtype)

    lane = lax.broadcasted_iota(jnp.int32, (tm, LANES), 1)
    low = lane < HEAD_DIM

    def spread(ref, cols, fill):
        for p in range(cols.shape[-1] // LANES):
            r = cols[:, p * LANES:(p + 1) * LANES]
            ref[:, 2 * p * LANES:(2 * p + 1) * LANES] = jnp.where(low, r, fill).astype(ref.dtype)
            ref[:, (2 * p + 1) * LANES:(2 * p + 2) * LANES] = jnp.where(
                low, _swap_halves(r), fill).astype(ref.dtype)

    for p in range(nsa_q.shape[-1] // LANES):
        r = nsa_q[:, p * LANES:(p + 1) * LANES]
        swapped = _swap_halves(r)
        nq_ref[:, 2 * p * LANES:(2 * p + 1) * LANES] = jnp.where(low, r, swapped).astype(
            nq_ref.dtype)
        nq_ref[:, (2 * p + 1) * LANES:(2 * p + 2) * LANES] = jnp.where(low, swapped, r).astype(
            nq_ref.dtype)

    key_blk = lax.shift_right_logical(
        s_idx * tm + lax.broadcasted_iota(jnp.int32, (tm, LANES), 0), int(math.log2(SEL_BLOCK)))
    spread(ks_ref, keys[:, 3 * LANES:4 * LANES],
           jnp.where(lane == HEAD_DIM + key_blk, 1.0, 0.0))
    spread(fv_ref, fox_v, 1.0)
    spread(vs_ref, tail[:, 0:LANES], 1.0)
    spread(vw_ref, tail[:, LANES:2 * LANES], 1.0)

    z = tail[:, 2 * LANES:3 * LANES] + bf_ref[...]
    is_f = lane < FOX_HEADS
    log_f = jnp.minimum(z, 0.0) - jnp.log(1.0 + jnp.exp(-jnp.abs(z)))
    log_f = jnp.where(is_f, log_f, 0.0)

    row = lax.broadcasted_iota(jnp.int32, (tm, tm), 0)
    col = lax.broadcasted_iota(jnp.int32, (tm, tm), 1)
    tri = jnp.where(row >= col, 1.0, 0.0).astype(BF16)
    sums = _dot(tri, jnp.concatenate(_split3(log_f), axis=1))
    local = sums[:, 0:LANES] + sums[:, LANES:2 * LANES] + sums[:, 2 * LANES:3 * LANES]

    @pl.when(s_idx == 0)
    def _():
        carry_ref[...] = jnp.zeros_like(carry_ref)

    cum = local + carry_ref[...]
    carry_ref[...] = cum[tm - 1:tm, :]
    misc = jnp.where(is_f, cum * LOG2E, _sigmoid(z))
    misc_ref[...] = misc
    misct_ref[...] = misc.T


def _in_proj(x, mod, w_r, bf_row):
    B, S, D = x.shape
    tm = ROW_TILE
    widths = (512, 512, 2 * (_C_K3 - _C_NQ), LANES, LANES, LANES, 2 * (_C_FV - _C_SK),
              2 * (_C_SV - _C_FV),
              2 * (_C_WV - _C_SV), 2 * (_C_MISC - _C_WV))
    wide = lambda w: pl.BlockSpec((None, tm, w), lambda b, s: (b, s, 0))
    out_shape = ([jax.ShapeDtypeStruct((B, S, w), BF16) for w in widths]
                 + [jax.ShapeDtypeStruct((B, S, LANES), F32),
                    jax.ShapeDtypeStruct((B, LANES, S), F32)])
    out_specs = ([wide(w) for w in widths]
                 + [wide(LANES), pl.BlockSpec((None, LANES, tm), lambda b, s: (b, 0, s))])
    return pl.pallas_call(
        _in_proj_kernel,
        grid=(B, S // tm),
        in_specs=[pl.BlockSpec((None, tm, D), lambda b, s: (b, s, 0)),
                  pl.BlockSpec((None, 6, D), lambda b, s: (b, 0, 0)),
                  pl.BlockSpec((D, _IN_COLS), lambda b, s: (0, 0)),
                  pl.BlockSpec((1, LANES), lambda b, s: (0, 0))],
        out_specs=out_specs,
        out_shape=out_shape,
        scratch_shapes=[pltpu.VMEM((1, LANES), F32)],
        compiler_params=pltpu.CompilerParams(dimension_semantics=("parallel", "arbitrary"),
                                             vmem_limit_bytes=VMEM_LIMIT),
        name="in_proj",
    )(x, mod, w_r, bf_row)


def _softmax_weights(s, m):
    return jnp.exp2((s - m).astype(BF16))


def _flash_update(carry, s, vt):
    m, acc = carry
    m_new = jnp.maximum(m, jnp.max(s, axis=-1, keepdims=True))
    alpha = jnp.exp2(m - m_new)
    p = _softmax_weights(s, m_new)
    rows = acc.shape[0]
    acc = alpha.reshape(rows, 1) * acc + _dot(p.reshape(rows, s.shape[-1]), vt)
    return m_new, acc


def _normalize(acc):
    return acc / _swap_halves(acc)


def _fox_kernel(q_ref, k_ref, v_ref, ck_ref, o_ref):
    i = pl.program_id(2)
    tq = q_ref.shape[0]
    tk = ck_ref.shape[2]
    q2 = q_ref[...].astype(F32)
    lane = lax.broadcasted_iota(jnp.int32, (tq, LANES), 1)
    low = lane < HEAD_DIM
    halves = (low, jnp.logical_not(low))
    qh = [jnp.where(h, q2, 0.0).astype(BF16) for h in halves]
    col_minus_row = (lax.broadcasted_iota(jnp.int32, (tq, tk), 1)
                     - lax.broadcasted_iota(jnp.int32, (tq, tk), 0))

    def step(jj, carry, diagonal):
        k0 = jj * tk
        kt = k_ref[pl.ds(k0, tk), :]
        new = []
        for hh in range(2):
            s = _dot_nt(qh[hh], kt) - ck_ref[hh, pl.ds(jj, 1), :]
            if diagonal:
                s = jnp.where(col_minus_row <= 0, s, NEG_BIG)
            vt = v_ref[pl.ds(k0, tk), hh * LANES:(hh + 1) * LANES]
            new.append(_flash_update(carry[hh], s, vt))
        return tuple(new)

    init = tuple((jnp.full((tq, 1), NEG_BIG, F32), jnp.zeros((tq, LANES), F32))
                 for _ in range(2))

    assert tq == tk
    for n_full in range(k_ref.shape[0] // tk):
        @pl.when(i == n_full)
        def _(n_full=n_full):
            carry = init
            for jj in range(n_full):
                carry = step(jj, carry, False)
            carry = step(n_full, carry, True)
            o_ref[...] = jnp.where(low, _normalize(carry[0][1]),
                                   _swap_halves(_normalize(carry[1][1]))).astype(o_ref.dtype)


def _fox(fq, fk, fv, cum_row):
    B, S, W = fq.shape
    tq = tk = FOX_TILE
    n_pairs = W // LANES
    cum_row = cum_row.reshape(B, n_pairs, 2, S // tk, tk)
    return pl.pallas_call(
        _fox_kernel,
        grid=(B, n_pairs, S // tq),
        in_specs=[pl.BlockSpec((None, tq, LANES), lambda b, p, i: (b, i, p)),
                  pl.BlockSpec((None, S, LANES), lambda b, p, i: (b, 0, p)),
                  pl.BlockSpec((None, S, 2 * LANES), lambda b, p, i: (b, 0, p)),
                  pl.BlockSpec((None, None, 2, S // tk, tk), lambda b, p, i: (b, p, 0, 0, 0))],
        out_specs=pl.BlockSpec((None, tq, LANES), lambda b, p, i: (b, i, p)),
        out_shape=jax.ShapeDtypeStruct((B, S, W), BF16),
        compiler_params=pltpu.CompilerParams(
            dimension_semantics=("parallel", "parallel", "arbitrary"),
            vmem_limit_bytes=VMEM_LIMIT),
        name="fox",
    )(fq, fk, fv, cum_row)


def _compress_kernel(xk_ref, xv_ref, pk_ref, pv_ref, wak_ref, wbk_ref, wav_ref, wbv_ref,
                     w2k_ref, w2v_ref, ok_ref, ov_ref):
    n_rows = xk_ref.shape[0]
    for x_ref, p_ref, wa_ref, wb_ref, w2_ref, o_ref in (
            (xk_ref, pk_ref, wak_ref, wbk_ref, w2k_ref, ok_ref),
            (xv_ref, pv_ref, wav_ref, wbv_ref, w2v_ref, ov_ref)):
        x = x_ref[...].astype(F32)
        xa = (x + p_ref[0:1, :]).astype(BF16)
        xb = (x + p_ref[1:2, :]).astype(BF16)
        hb = _dot(xb, wb_ref[...])
        h1 = _dot(xa, wa_ref[...]) + pltpu.roll(hb, n_rows - 1, 0)
        o_ref[...] = _dot(_silu(h1).astype(BF16), w2_ref[...]).astype(o_ref.dtype)


def _compress(xk, xv, pk, pv, wak, wbk, wav, wbv, w2k, w2v):
    B, R, C = xk.shape
    xspec = pl.BlockSpec((None, R, C), lambda b: (b, 0, 0))
    full = lambda a: pl.BlockSpec(a.shape, lambda b: (0,) * a.ndim)
    ospec = pl.BlockSpec((None, R, LANES), lambda b: (b, 0, 0))
    return pl.pallas_call(
        _compress_kernel,
        grid=(B,),
        in_specs=[xspec, xspec] + [full(a) for a in (pk, pv, wak, wbk, wav, wbv, w2k, w2v)],
        out_specs=[ospec, ospec],
        out_shape=[jax.ShapeDtypeStruct((B, R, LANES), BF16)] * 2,
        compiler_params=pltpu.CompilerParams(dimension_semantics=("parallel",),
                                             vmem_limit_bytes=VMEM_LIMIT),
        name="compress",
    )(xk, xv, pk, pv, wak, wbk, wav, wbv, w2k, w2v)


def _rank_rows(score):
    n = score.shape[0]
    j = lax.broadcasted_iota(jnp.int32, score.shape, 0)
    rank = jnp.zeros(score.shape, jnp.int32)
    for i in range(n):
        si = score[i:i + 1, :]
        beats = (si > score) | ((si == score) & (j > i))
        rank = rank + jnp.where(beats, 1, 0)
    return rank


def _dup_head(q_ref, r):
    return q_ref[:, r * LANES:(r + 1) * LANES].astype(F32)


def _pack_heads(o_list, g):
    lane = lax.broadcasted_iota(jnp.int32, o_list[0].shape, 1)
    in_g = (lane >= g * HEAD_DIM) & (lane < (g + 1) * HEAD_DIM)
    both = []
    for o in o_list:
        om = jnp.where(in_g, o, 0.0)
        both.append(om + _swap_halves(om))
    pairs = [jnp.where(lane < HEAD_DIM, both[2 * p], both[2 * p + 1]) for p in range(2)]
    return jnp.concatenate(pairs, axis=1)


def _group_gates(misc, g):
    w = 3 * NSA_GQA
    gates = misc[:, FOX_HEADS:FOX_HEADS + w]
    for other in range(1, NSA_GROUPS):
        gates = jnp.where(g == other, misc[:, FOX_HEADS + other * w:FOX_HEADS + (other + 1) * w],
                          gates)
    return gates


def _cmp_sel_kernel(q_ref, kc_ref, vc_ref, bias_ref, gate_ref, ovl_ref, oc_ref, sel_ref):
    g = pl.program_id(1)
    i = pl.program_id(2)
    tq = q_ref.shape[0]
    n_pad = kc_ref.shape[0]
    lane = lax.broadcasted_iota(jnp.int32, (n_pad, LANES), 1)
    in_g = (lane >= g * HEAD_DIM) & (lane < (g + 1) * HEAD_DIM)
    kc = jnp.where(in_g, kc_ref[...].astype(F32), 0.0).astype(BF16)
    vc = vc_ref[...]
    t = i * tq + lax.broadcasted_iota(jnp.int32, (tq, n_pad), 0)
    n = lax.broadcasted_iota(jnp.int32, (tq, n_pad), 1)
    valid = t >= n * CMP_STRIDE + (CMP_BLOCK - 1)
    gates = _group_gates(gate_ref[...], g)
    p_sum = jnp.zeros((tq, n_pad), F32)
    outs = []
    for r in range(NSA_GQA):
        qr = q_ref[:, r * LANES:(r + 1) * LANES]
        s = _dot_nt(qr, kc)
        s = jnp.where(valid, s + bias_ref[r], NEG_BIG)
        m = jnp.max(s, axis=-1, keepdims=True)
        p = jnp.exp2(s - m)
        p = p / jnp.sum(p, axis=-1, keepdims=True)
        p = jnp.where(valid, p, 0.0)
        p_sum = p_sum + p
        outs.append(_dot(p.astype(BF16), vc) * gates[:, 3 * r:3 * r + 1])
    oc_ref[...] = _pack_heads(outs, g).astype(oc_ref.dtype)

    ovl = ovl_ref[...]
    hi, mid, lo = _split3(p_sum)
    imp = _dot_nt(ovl, hi) + _dot_nt(ovl, mid) + _dot_nt(ovl, lo)
    n_blk = imp.shape[0]
    j = lax.broadcasted_iota(jnp.int32, (n_blk, tq), 0)
    qb = jnp.right_shift(i * tq + lax.broadcasted_iota(jnp.int32, (n_blk, tq), 1),
                         int(math.log2(SEL_BLOCK)))
    forced = (j == 0) | (j == qb) | (j == qb - 1)
    causal = j <= qb
    score = jnp.where(causal, imp + jnp.where(forced, FORCE_SCORE, 0.0), -FORCE_SCORE)
    chosen = (_rank_rows(score) < N_SEL) & causal
    sel = jnp.where(chosen, 1.0, 0.0)
    sel = jnp.concatenate([sel, jnp.zeros((LANES - n_blk, tq), F32)], axis=0)
    sel_ref[...] = sel.T


def _cmp_sel(nq, kcmp, vcmp, bias_c, gates_g, ovl_t):
    B, S, _ = nq.shape
    tq = CMP_TILE
    n_pad = kcmp.shape[1]
    return pl.pallas_call(
        _cmp_sel_kernel,
        grid=(B, NSA_GROUPS, S // tq),
        in_specs=[pl.BlockSpec((None, tq, NSA_GQA * LANES), lambda b, g, i: (b, i, g)),
                  pl.BlockSpec((None, n_pad, LANES), lambda b, g, i: (b, 0, 0)),
                  pl.BlockSpec((None, n_pad, LANES), lambda b, g, i: (b, 0, 0)),
                  pl.BlockSpec((NSA_GQA, tq, n_pad), lambda b, g, i: (g, i, 0)),
                  pl.BlockSpec((None, tq, LANES), lambda b, g, i: (b, i, 0)),
                  pl.BlockSpec(ovl_t.shape, lambda b, g, i: (0, 0))],
        out_specs=[pl.BlockSpec((None, tq, 2 * LANES), lambda b, g, i: (b, i, g)),
                   pl.BlockSpec((None, None, tq, LANES), lambda b, g, i: (b, g, i, 0))],
        out_shape=[jax.ShapeDtypeStruct((B, S, NSA_HEADS * HEAD_DIM), BF16),
                   jax.ShapeDtypeStruct((B, NSA_GROUPS, S, LANES), F32)],
        compiler_params=pltpu.CompilerParams(
            dimension_semantics=("parallel", "parallel", "arbitrary"),
            vmem_limit_bytes=VMEM_LIMIT),
        name="cmp_sel",
    )(nq, kcmp, vcmp, bias_c, gates_g, ovl_t)


def _nsa_kernel(q_ref, ks_ref, vs_ref, kw_ref, vw_ref, sel_ref, w4_ref, gate_ref, oc_ref,
                o_ref):
    for tile in range(ks_ref.shape[0] // q_ref.shape[0]):
        @pl.when(pl.program_id(2) == tile)
        def _(tile=tile):
            _nsa_tile(tile, q_ref, ks_ref, vs_ref, kw_ref, vw_ref, sel_ref, w4_ref, gate_ref,
                      oc_ref, o_ref)


def _nsa_tile(i, q_ref, ks_ref, vs_ref, kw_ref, vw_ref, sel_ref, w4_ref, gate_ref, oc_ref, o_ref):
    g = pl.program_id(0)
    tq = q_ref.shape[0]
    tk = w4_ref.shape[-1]
    H = NSA_GQA
    lane = lax.broadcasted_iota(jnp.int32, (tq, LANES), 1)
    in_g = (lane >= g * HEAD_DIM) & (lane < (g + 1) * HEAD_DIM)
    heads = [_dup_head(q_ref, r) for r in range(H)]
    qs = jnp.concatenate([jnp.where(in_g, hd, 0.0).astype(BF16) for hd in heads], axis=0)

    blk_mask = _swap_halves((sel_ref[...] - 1.0) * (-NEG_BIG))
    qsel = jnp.concatenate(
        [jnp.where(lane < HEAD_DIM, hd, blk_mask).astype(BF16) for hd in heads], axis=0)

    def sel_step(jj, carry, diagonal):
        k0 = jj * tk
        ahead = i * tq - k0
        width = min(ahead + tq, tk) if diagonal else tk
        rows = pl.ds(k0, width)
        s = (_dot_nt(qsel, ks_ref[rows, :]).reshape(H, tq, width)
             + w4_ref[:, min(i - 2 * jj, 3), :, 0:width])
        if diagonal:
            cmr = (lax.broadcasted_iota(jnp.int32, (tq, width), 1)
                   - lax.broadcasted_iota(jnp.int32, (tq, width), 0))
            s = jnp.where((cmr <= ahead)[None], s, NEG_BIG)
        return _flash_update(carry, s, vs_ref[rows, :])

    carry = (jnp.full((H, tq, 1), NEG_BIG, F32), jnp.zeros((H * tq, LANES), F32))
    n_full = (i * tq) // tk
    for jj in range(n_full):
        carry = sel_step(jj, carry, False)
    _, acc_s = sel_step(n_full, carry, True)

    old = [max(i - d, 0) * tq for d in (2, 1)]
    now = i * tq
    k_old = jnp.concatenate([kw_ref[pl.ds(st, tq), :] for st in old], axis=0)
    v_old = jnp.concatenate([vw_ref[pl.ds(st, tq), :] for st in old], axis=0)
    col = lax.broadcasted_iota(jnp.int32, (tq, 2 * tq), 1)
    cmr = col - lax.broadcasted_iota(jnp.int32, (tq, 2 * tq), 0)
    never = 4 * tq
    ok_old = (((col < tq) & (cmr > (0 if i >= 2 else never)))
              | (col >= (tq if i >= 1 else never)))
    s_old = jnp.where(ok_old[None], _dot_nt(qs, k_old).reshape(H, tq, 2 * tq) + w4_ref[:, 2],
                      NEG_BIG)
    s_now = jnp.where((cmr[:, 0:tq] <= 0)[None],
                      _dot_nt(qs, kw_ref[pl.ds(now, tq), :]).reshape(H, tq, tq)
                      + w4_ref[:, 0, :, 0:tq], NEG_BIG)
    m_w = jnp.maximum(jnp.max(s_old, axis=-1, keepdims=True),
                      jnp.max(s_now, axis=-1, keepdims=True))
    acc_w = (_dot(_softmax_weights(s_old, m_w).reshape(H * tq, 2 * tq), v_old)
             + _dot(_softmax_weights(s_now, m_w).reshape(H * tq, tq), vw_ref[pl.ds(now, tq), :]))

    gates = _group_gates(gate_ref[...], g)
    o_s = _normalize(acc_s)
    o_w = _normalize(acc_w)
    outs = []
    for r in range(H):
        sl = slice(r * tq, (r + 1) * tq)
        outs.append(o_s[sl] * gates[:, 3 * r + 1:3 * r + 2] + o_w[sl] * gates[:, 3 * r + 2:3 * r + 3])
    lane = lax.broadcasted_iota(jnp.int32, (tq, LANES), 1)
    pairs = [jnp.where(lane < HEAD_DIM, outs[2 * p], _swap_halves(outs[2 * p + 1]))
             for p in range(H // 2)]
    o_ref[...] = (jnp.concatenate(pairs, axis=1) + oc_ref[...].astype(F32)).astype(o_ref.dtype)


def _nsa(nq, ks, vs, kw, vw, sel, w4, misc, oc):
    B, S, _ = nq.shape
    tq = ATT_TILE
    tk = w4.shape[-1]
    both = lambda: pl.BlockSpec((None, S, LANES), lambda g, b, i: (b, 0, 0))
    mine = lambda: pl.BlockSpec((None, S, LANES), lambda g, b, i: (b, 0, g))
    return pl.pallas_call(
        _nsa_kernel,
        grid=(NSA_GROUPS, B, S // tq),
        in_specs=[pl.BlockSpec((None, tq, NSA_GQA * LANES), lambda g, b, i: (b, i, g)),
                  mine(), mine(), both(), mine(),
                  pl.BlockSpec((None, None, tq, LANES), lambda g, b, i: (b, g, i, 0)),
                  pl.BlockSpec((NSA_GQA, 4, tq, tk), lambda g, b, i: (g, 0, 0, 0)),
                  pl.BlockSpec((None, tq, LANES), lambda g, b, i: (b, i, 0)),
                  pl.BlockSpec((None, tq, 2 * LANES), lambda g, b, i: (b, i, g))],
        out_specs=pl.BlockSpec((None, tq, 2 * LANES), lambda g, b, i: (b, i, g)),
        out_shape=jax.ShapeDtypeStruct((B, S, NSA_HEADS * HEAD_DIM), BF16),
        compiler_params=pltpu.CompilerParams(
            dimension_semantics=("parallel", "parallel", "arbitrary"),
            vmem_limit_bytes=VMEM_LIMIT),
        name="nsa",
    )(nq, ks, vs, kw, vw, sel, w4, misc, oc)


def _layer_norm(y, g, b):
    mu = jnp.mean(y, axis=-1, keepdims=True)
    yc = y - mu
    var = jnp.mean(yc * yc, axis=-1, keepdims=True)
    return yc * lax.rsqrt(var + LN_EPS) * g + b


def _top_rows(score, k):
    n = score.shape[0]
    idx = lax.broadcasted_iota(jnp.int32, score.shape, 0).astype(F32)
    alive = jnp.ones(score.shape, F32)
    for _ in range(k):
        live = alive > 0.0
        best = jnp.max(jnp.where(live, score, -jnp.inf), axis=0, keepdims=True)
        first = jnp.min(jnp.where(live & (score == best), idx, float(n)), axis=0, keepdims=True)
        alive = jnp.where(idx == first, 0.0, alive)
    return alive == 0.0


def _router_gates_t(h2, wr_t, eb_col):
    tm = h2.shape[0]
    h_hi, h_lo, _ = _split3(h2)
    w_hi, w_lo, _ = _split3(wr_t)
    logit = _dot_nt(w_hi, h_hi) + _dot_nt(w_hi, h_lo) + _dot_nt(w_lo, h_hi)
    scores = _sigmoid(logit)
    biased = scores + eb_col
    e_in = lax.broadcasted_iota(jnp.int32, (GROUP_SIZE, tm), 0).astype(F32)
    gs_rows = []
    for gi in range(N_EXPERT_GROUPS):
        grp = biased[gi * GROUP_SIZE:(gi + 1) * GROUP_SIZE, :]
        m1 = jnp.max(grp, axis=0, keepdims=True)
        first = jnp.min(jnp.where(grp == m1, e_in, float(GROUP_SIZE)), axis=0, keepdims=True)
        m2 = jnp.max(jnp.where(e_in == first, -jnp.inf, grp), axis=0, keepdims=True)
        gs_rows.append(m1 + m2)
    gscore = jnp.concatenate(gs_rows, axis=0)
    g_keep = _rank_rows(gscore) < TOPK_GROUPS
    keep = jnp.concatenate(
        [jnp.broadcast_to(g_keep[gi:gi + 1, :], (GROUP_SIZE, tm)) for gi in range(N_EXPERT_GROUPS)],
        axis=0)
    masked = jnp.where(keep, biased, -jnp.inf)
    chosen = _top_rows(masked, TOP_K)
    w = jnp.where(chosen, scores, 0.0)
    return w / jnp.sum(w, axis=0, keepdims=True) * ROUTED_SCALE


def _out_proj_kernel(alpha, of_ref, on_ref, x_ref, mod_ref, w_ref, lg_ref, lb_ref, wr_ref,
                     eb_ref, x1_ref, h2_ref, gate_ref, gate_t_ref):
    half = of_ref.shape[1]
    mod = mod_ref[...]
    mixed = _dot(of_ref[...], w_ref[0:half, :]) + _dot(on_ref[...], w_ref[half:2 * half, :])
    y = alpha * x_ref[...] + mod[2:3, :] * mixed
    x1 = _layer_norm(y, lg_ref[...], lb_ref[...])
    x1_ref[...] = x1
    h2 = x1 * (1.0 + mod[4:5, :]) + mod[3:4, :]
    h2_ref[...] = h2.astype(h2_ref.dtype)
    gates_t = _router_gates_t(h2, wr_ref[...], eb_ref[...])
    gate_t_ref[...] = gates_t
    tm = h2.shape[0]
    gates_t = jnp.concatenate([gates_t, jnp.zeros((LANES - N_EXPERTS, tm), F32)], axis=0)
    gate_ref[...] = gates_t.T


def _out_proj(alpha, o_fox, o_nsa, x, mod, w_out, ln_g, ln_b, wr_t, eb_col):
    B, S, D = x.shape
    tm = ROW_TILE
    half = o_fox.shape[-1]
    row = lambda a: pl.BlockSpec(a.shape, lambda b, s: (0, 0))
    return pl.pallas_call(
        functools.partial(_out_proj_kernel, alpha),
        grid=(B, S // tm),
        in_specs=[pl.BlockSpec((None, tm, half), lambda b, s: (b, s, 0)),
                  pl.BlockSpec((None, tm, half), lambda b, s: (b, s, 0)),
                  pl.BlockSpec((None, tm, D), lambda b, s: (b, s, 0)),
                  pl.BlockSpec((None, 6, D), lambda b, s: (b, 0, 0)),
                  row(w_out), row(ln_g), row(ln_b), row(wr_t), row(eb_col)],
        out_specs=[pl.BlockSpec((None, tm, D), lambda b, s: (b, s, 0)),
                   pl.BlockSpec((None, tm, D), lambda b, s: (b, s, 0)),
                   pl.BlockSpec((None, tm, LANES), lambda b, s: (b, s, 0)),
                   pl.BlockSpec((N_EXPERTS, tm), lambda b, s: (0, b * (S // tm) + s))],
        out_shape=[jax.ShapeDtypeStruct((B, S, D), F32),
                   jax.ShapeDtypeStruct((B, S, D), BF16),
                   jax.ShapeDtypeStruct((B, S, LANES), F32),
                   jax.ShapeDtypeStruct((N_EXPERTS, B * S), F32)],
        compiler_params=pltpu.CompilerParams(dimension_semantics=("parallel", "parallel"),
                                             vmem_limit_bytes=VMEM_LIMIT),
        name="out_proj",
    )(o_fox, o_nsa, x, mod, w_out, ln_g, ln_b, wr_t, eb_col)


SORT_TILE = 256
ROW_ALIGN = 16
EXP_TILE = 1024
P_CHUNK = 512


def _strict_upper(n):
    return jnp.where(lax.broadcasted_iota(jnp.int32, (n, n), 0)
                     < lax.broadcasted_iota(jnp.int32, (n, n), 1), 1.0, 0.0).astype(BF16)


def _strict_lower(n):
    return jnp.where(lax.broadcasted_iota(jnp.int32, (n, n), 1)
                     < lax.broadcasted_iota(jnp.int32, (n, n), 0), 1.0, 0.0).astype(BF16)


def _local_rows_bound(ts):
    rows = TOP_K * ts + N_EXPERTS * (ROW_ALIGN - 1)
    return -(-rows // P_CHUNK) * P_CHUNK


def _piece_cols(ts):
    return -(-(_local_rows_bound(ts) // ROW_ALIGN) // LANES) * LANES


def _sorted_tiles_bound(T):
    rows = TOP_K * T + (T // SORT_TILE) * N_EXPERTS * (ROW_ALIGN - 1)
    return -(-rows // EXP_TILE) + N_EXPERTS


def _sorted_tiles_expected(T):
    groups = (T // SORT_TILE) * N_EXPERTS
    rows = TOP_K * T + groups * ((ROW_ALIGN - 1) / 2 + 2)
    return int(-(-rows // EXP_TILE) + math.ceil(0.65 * N_EXPERTS))


def _moe_meta_kernel(gt_ref, ptab_ref, loc_et_ref, np_et_ref, loc_te_ref, np_te_ref, tot_ref,
                     erow_ref, texp_ref, nused_ref):
    E, T = gt_ref.shape
    mask = jnp.where(gt_ref[...] > 0.0, 1.0, 0.0).astype(BF16)
    t_id = lax.shift_right_logical(lax.broadcasted_iota(jnp.int32, (T, LANES), 0),
                                   int(math.log2(SORT_TILE)))
    tind = jnp.where(t_id == lax.broadcasted_iota(jnp.int32, (T, LANES), 1), 1.0, 0.0)
    cnt = _dot(mask, tind.astype(BF16))
    n16 = jnp.floor((cnt + (ROW_ALIGN - 1.0)) * (1.0 / ROW_ALIGN))
    n16b = n16.astype(BF16)
    q = EXP_TILE // ROW_ALIGN
    len16 = jnp.sum(n16, axis=1, keepdims=True)
    pad16 = jnp.floor((len16 + (q - 1.0)) * (1.0 / q)) * q
    sl = _strict_lower(E)
    hi, mid, lo = _split3(jnp.broadcast_to(pad16, (E, LANES)))
    start16 = _dot(sl, hi) + _dot(sl, mid) + _dot(sl, lo)
    gdst16 = start16 + _dot(n16b, _strict_upper(LANES))
    loc16 = _dot(sl, n16b)

    def t(a):
        return jnp.concatenate([a, jnp.zeros((LANES - E, LANES), F32)], axis=0).T

    scale = float(ROW_ALIGN)
    loc_et_ref[...] = loc16 * scale
    np_et_ref[...] = n16 * scale
    loc_te_ref[...] = t(loc16) * scale
    np_te_ref[...] = t(n16) * scale
    tot_ref[...] = (jnp.sum(n16, axis=0, keepdims=True) * scale).astype(jnp.int32)

    n_t, n_blk = ptab_ref.shape
    blk = lax.broadcasted_iota(jnp.int32, (E, n_blk), 1).astype(F32)
    for tile in range(n_t):
        lo_c = loc16[:, tile:tile + 1]
        inside = (lo_c <= blk) & (blk < lo_c + n16[:, tile:tile + 1])
        dst = jnp.sum(jnp.where(inside, gdst16[:, tile:tile + 1] + (blk - lo_c), 0.0),
                      axis=0, keepdims=True)
        ptab_ref[tile:tile + 1, :] = (dst * scale).astype(jnp.int32)
    ends = jnp.concatenate([t(start16 + len16)[0:1, :], t(start16 + pad16)[0:1, :],
                            jnp.zeros((erow_ref.shape[0] - 2, LANES), F32)], axis=0)
    erow_ref[...] = (ends * scale).astype(jnp.int32)
    n_tab = texp_ref.shape[1]
    tile_row16 = (lax.broadcasted_iota(jnp.int32, (E, n_tab), 1) * q).astype(F32)
    owner = jnp.sum(jnp.where(start16[:, 0:1] <= tile_row16, 1.0, 0.0), axis=0, keepdims=True)
    texp_ref[...] = (owner - 1.0).astype(jnp.int32)
    n_used = jnp.sum(pad16, axis=0, keepdims=True) * (1.0 / q)
    nused_ref[...] = jnp.broadcast_to(n_used, (1, LANES)).astype(jnp.int32)


def _moe_meta(gates_t, n_tab):
    E, T = gates_t.shape
    i32 = jnp.int32
    return pl.pallas_call(
        _moe_meta_kernel,
        out_shape=[jax.ShapeDtypeStruct((T // SORT_TILE, _piece_cols(SORT_TILE)), i32),
                   jax.ShapeDtypeStruct((E, LANES), F32),
                   jax.ShapeDtypeStruct((E, LANES), F32),
                   jax.ShapeDtypeStruct((LANES, LANES), F32),
                   jax.ShapeDtypeStruct((LANES, LANES), F32),
                   jax.ShapeDtypeStruct((1, LANES), i32),
                   jax.ShapeDtypeStruct((8, LANES), i32),
                   jax.ShapeDtypeStruct((1, n_tab), i32),
                   jax.ShapeDtypeStruct((1, LANES), i32)],
        compiler_params=pltpu.CompilerParams(vmem_limit_bytes=VMEM_LIMIT),
        name="moe_meta",
    )(gates_t)


def _start_pieces(tile, ptab_s, tot_s, n_cols, make_copy):
    n_pieces = lax.shift_right_logical(tot_s[tile], int(math.log2(ROW_ALIGN)))
    group = 8

    def start(b):
        make_copy(pl.multiple_of(b * ROW_ALIGN, ROW_ALIGN),
                  pl.multiple_of(ptab_s[tile * n_cols + b], ROW_ALIGN)).start()

    def grouped(q, carry):
        for k in range(group):
            start(q * group + k)
        return carry

    def single(b, carry):
        start(b)
        return carry

    n_grouped = lax.shift_right_logical(n_pieces, int(math.log2(group)))
    lax.fori_loop(0, n_grouped, grouped, 0)
    lax.fori_loop(n_grouped * group, n_pieces, single, 0)


def _moe_sort_kernel(ptab_s, tot_s, lend_s, rend_s, gt_ref, gtok_ref, h_ref, locrow_ref,
                     nprow_ref, xs_hbm, buf, zbuf, sem, zsem):
    tau = pl.program_id(0)
    n_t = pl.num_programs(0)
    slot = lax.rem(tau, 2)
    E, ts = gt_ref.shape
    D = h_ref.shape[1]
    n_cols = _piece_cols(ts)

    def copies(tile, sl, wait):
        if wait:
            rows = pl.multiple_of(tot_s[tile], ROW_ALIGN)

            @pl.when(rows > 0)
            def _():
                pltpu.make_async_copy(buf.at[sl, pl.ds(0, rows)], xs_hbm.at[pl.ds(0, rows)],
                                      sem.at[sl]).wait()
        else:
            def make_copy(loc, dst):
                return pltpu.make_async_copy(buf.at[sl, pl.ds(loc, ROW_ALIGN)],
                                             xs_hbm.at[pl.ds(dst, ROW_ALIGN)], sem.at[sl])
            _start_pieces(tile, ptab_s, tot_s, n_cols, make_copy)

    z_rows = zbuf.shape[0]
    used_rows = rend_s[E - 1]
    n_spare = (xs_hbm.shape[0] - used_rows) // z_rows

    def spare_fill(wait):
        def body(c, carry):
            dst = pl.multiple_of(used_rows + c * z_rows, z_rows)
            cp = pltpu.make_async_copy(zbuf, xs_hbm.at[pl.ds(dst, z_rows)], zsem.at[1])
            if wait:
                cp.wait()
            else:
                cp.start()
            return carry

        lax.fori_loop(0, n_spare, body, 0)

    @pl.when(tau == 0)
    def _():
        zbuf[...] = jnp.zeros_like(zbuf)
        spare_fill(False)

    @pl.when(tau >= 2)
    def _():
        copies(tau - 2, slot, True)

    g = gt_ref[...]
    mask = g > 0.0
    maskb = jnp.where(mask, 1.0, 0.0).astype(BF16)
    pad = jnp.zeros((LANES - E, ts), F32)
    pos = jnp.where(mask, _dot(maskb, _strict_upper(ts)), -1.0)
    pos = jnp.concatenate([pos, pad], axis=0).astype(BF16)
    lo_row = locrow_ref[...]
    hi_row = lo_row + nprow_ref[...]
    h = jnp.concatenate([h_ref[...]] + list(_split3(gtok_ref[...])), axis=1)
    lane = lax.broadcasted_iota(jnp.int32, (P_CHUNK, LANES), 1)

    def chunk(c):
        r0 = c * P_CHUNK
        r = (r0 + lax.broadcasted_iota(jnp.int32, (P_CHUNK, LANES), 0)).astype(F32)
        inside = (lo_row <= r) & (r < hi_row)
        group = jnp.where(inside, 1.0, 0.0).astype(BF16)
        want = r[:, 0:1] - jnp.sum(jnp.where(inside, lo_row, 0.0), axis=1, keepdims=True)
        hit = _dot(group, pos) == want
        rows = _dot(jnp.where(hit, 1.0, 0.0).astype(BF16), h)
        buf[slot, pl.ds(r0, P_CHUNK), 0:D] = rows[:, 0:D].astype(buf.dtype)
        extra = jnp.zeros((P_CHUNK, LANES), F32)
        for k in range(3):
            mine = jnp.where(inside, rows[:, D + k * LANES:D + (k + 1) * LANES], 0.0)
            extra = jnp.where(lane == k, jnp.sum(mine, axis=1, keepdims=True), extra)
        buf[slot, pl.ds(r0, P_CHUNK), D:D + LANES] = extra.astype(buf.dtype)

    n_static = -(-(TOP_K * ts + E * (ROW_ALIGN // 2)) // P_CHUNK)
    for c in range(buf.shape[1] // P_CHUNK):
        if c < n_static:
            chunk(c)
        else:
            pl.when(tot_s[tau] > c * P_CHUNK)(functools.partial(chunk, c))
    copies(tau, slot, False)

    @pl.when(tau == n_t - 1)
    def _():
        @pl.when(n_t >= 2)
        def _():
            copies(tau - 1, 1 - slot, True)
        copies(tau, slot, True)
        spare_fill(True)

        sizes =[zbuf.shape[0] >> s for s in range(int(math.log2(zbuf.shape[0] // ROW_ALIGN)) + 1)]

        def fill(wait):
            def e_body(e, carry):
                start = lend_s[e]
                n = rend_s[e] - start
                off = start
                for size in sizes:
                    bit = jnp.bitwise_and(n, size)

                    @pl.when(bit != 0)
                    def _(off=off, size=size):
                        cp = pltpu.make_async_copy(
                            zbuf.at[pl.ds(0, size)],
                            xs_hbm.at[pl.ds(pl.multiple_of(off, ROW_ALIGN), size)], zsem.at[0])
                        if wait:
                            cp.wait()
                        else:
                            cp.start()

                    off = off + bit
                return carry

            lax.fori_loop(0, E, e_body, 0)

        fill(False)
        fill(True)


def _moe_sort(ptab, tot, lend, rend, gates_t, gates, h2, loc_te, np_te, n_rows):
    E, T = gates_t.shape
    D = h2.shape[1]
    ts = SORT_TILE
    grid_spec = pltpu.PrefetchScalarGridSpec(
        num_scalar_prefetch=4,
        grid=(T // ts,),
        in_specs=[pl.BlockSpec((E, ts), lambda t, *_: (0, t)),
                  pl.BlockSpec((ts, LANES), lambda t, *_: (t, 0)),
                  pl.BlockSpec((ts, D), lambda t, *_: (t, 0)),
                  pl.BlockSpec((None, 1, LANES), lambda t, *_: (t, 0, 0)),
                  pl.BlockSpec((None, 1, LANES), lambda t, *_: (t, 0, 0))],
        out_specs=pl.BlockSpec(memory_space=pl.ANY),
        scratch_shapes=[pltpu.VMEM((2, _local_rows_bound(ts), D + LANES), BF16),
                        pltpu.VMEM((EXP_TILE // 2, D + LANES), BF16),
                        pltpu.SemaphoreType.DMA((2,)),
                        pltpu.SemaphoreType.DMA((2,))])
    return pl.pallas_call(
        _moe_sort_kernel,
        grid_spec=grid_spec,
        out_shape=jax.ShapeDtypeStruct((n_rows, D + LANES), BF16),
        compiler_params=pltpu.CompilerParams(dimension_semantics=("arbitrary",),
                                             vmem_limit_bytes=VMEM_LIMIT),
        name="moe_sort",
    )(ptab, tot, lend, rend, gates_t, gates, h2, loc_te, np_te)


def _moe_expert_kernel(texp_s, nused_s, x_ref, wg_ref, wu_ref, wd_ref, y_ref, wgu_s, wd_s):
    i = pl.program_id(0)
    f = wd_ref.shape[0]

    @pl.when(i < nused_s[0])
    def _():
        @pl.when((i == 0) | (texp_s[i] != texp_s[jnp.maximum(i - 1, 0)]))
        def _():
            wgu_s[:, 0:f] = wg_ref[...].astype(BF16)
            wgu_s[:, f:2 * f] = wu_ref[...].astype(BF16)
            wd_s[...] = wd_ref[...].astype(BF16)

        d = wd_ref.shape[1]
        gate = jnp.sum(x_ref[:, d:].astype(F32), axis=1, keepdims=True)
        a = _dot(x_ref[:, 0:d], wgu_s[...])
        act = _silu(a[:, :f]) * a[:, f:] * gate
        y_ref[...] = _dot(act.astype(BF16), wd_s[...]).astype(y_ref.dtype)

    @pl.when(i >= nused_s[0])
    def _():
        y_ref[...] = jnp.zeros_like(y_ref)


def _moe_expert(texp, nused, xs, w_gate, w_up, w_down, n_tiles):
    n_rows, xw = xs.shape
    D, f = w_gate.shape[-2:]
    tm = EXP_TILE

    def tile(i, texp, nused):
        return jnp.maximum(jnp.minimum(i, nused[0] - 1), 0)

    grid_spec = pltpu.PrefetchScalarGridSpec(
        num_scalar_prefetch=2,
        grid=(n_tiles,),
        in_specs=[pl.BlockSpec((tm, xw), lambda i, te, nu: (tile(i, te, nu), 0)),
                  pl.BlockSpec((None, D, f), lambda i, te, nu: (te[tile(i, te, nu)], 0, 0)),
                  pl.BlockSpec((None, D, f), lambda i, te, nu: (te[tile(i, te, nu)], 0, 0)),
                  pl.BlockSpec((None, f, D), lambda i, te, nu: (te[tile(i, te, nu)], 0, 0))],
        out_specs=pl.BlockSpec((tm, D), lambda i, te, nu: (i, 0)),
        scratch_shapes=[pltpu.VMEM((D, 2 * f), BF16), pltpu.VMEM((f, D), BF16)])
    return pl.pallas_call(
        _moe_expert_kernel,
        grid_spec=grid_spec,
        out_shape=jax.ShapeDtypeStruct((n_rows, D), BF16),
        compiler_params=pltpu.CompilerParams(dimension_semantics=("arbitrary",),
                                             vmem_limit_bytes=VMEM_LIMIT),
        name="moe_expert",
    )(texp, nused, xs, w_gate, w_up, w_down)


def _moe_combine_kernel(alpha, ptab_s, tot_s, g_ref, loc_ref, np_ref, h_ref, x1_ref,
                        mod_ref, sgu_ref, sd_ref, lg_ref, lb_ref, y_hbm, o_ref, ybuf, acc_ref,
                        sem):
    tau = pl.program_id(0)
    n_t = pl.num_programs(0)
    slot = lax.rem(tau, 2)
    ts, n_lane = g_ref.shape
    E = loc_ref.shape[0]
    n_cols = _piece_cols(ts)

    def copies(tile, sl, wait):
        if wait:
            rows = pl.multiple_of(tot_s[tile], ROW_ALIGN)

            @pl.when(rows > 0)
            def _():
                pltpu.make_async_copy(y_hbm.at[pl.ds(0, rows)], ybuf.at[sl, pl.ds(0, rows)],
                                      sem.at[sl]).wait()
        else:
            def make_copy(loc, dst):
                return pltpu.make_async_copy(y_hbm.at[pl.ds(dst, ROW_ALIGN)],
                                             ybuf.at[sl, pl.ds(loc, ROW_ALIGN)], sem.at[sl])
            _start_pieces(tile, ptab_s, tot_s, n_cols, make_copy)

    @pl.when(tau == 0)
    def _():
        ybuf[...] = jnp.zeros_like(ybuf)
        copies(0, 0, False)

    @pl.when(tau + 1 < n_t)
    def _():
        copies(tau + 1, 1 - slot, False)

    g = g_ref[...]
    mask = g > 0.0
    maskb = jnp.where(mask, 1.0, 0.0).astype(BF16)
    pos = jnp.where(mask, _dot(_strict_lower(ts), maskb), -1.0).astype(BF16)
    lane = lax.broadcasted_iota(jnp.int32, loc_ref.shape, 1)
    lo_col = jnp.sum(jnp.where(lane == tau, loc_ref[...], 0.0), axis=1, keepdims=True)
    hi_col = lo_col + jnp.sum(jnp.where(lane == tau, np_ref[...], 0.0), axis=1, keepdims=True)

    f = sd_ref.shape[0]
    a = _dot(h_ref[...], sgu_ref[...])
    acc_ref[...] = _dot((_silu(a[:, :f]) * a[:, f:]).astype(BF16), sd_ref[...])

    copies(tau, slot, True)

    def chunk(c):
        r0 = c * P_CHUNK
        r = (r0 + lax.broadcasted_iota(jnp.int32, (E, P_CHUNK), 1)).astype(F32)
        inside = (lo_col <= r) & (r < hi_col)
        group = jnp.concatenate([jnp.where(inside, 1.0, 0.0),
                                 jnp.zeros((n_lane - E, P_CHUNK), F32)], axis=0).astype(BF16)
        want = r[0:1, :] - jnp.sum(jnp.where(inside, lo_col, 0.0), axis=0, keepdims=True)
        hit = _dot(pos, group) == want
        return _dot(jnp.where(hit, 1.0, 0.0).astype(BF16), ybuf[slot, pl.ds(r0, P_CHUNK), :])

    n_static = -(-(TOP_K * ts + E * (ROW_ALIGN // 2)) // P_CHUNK)
    acc_ref[...] += sum(chunk(c) for c in range(n_static))
    for c in range(n_static, ybuf.shape[1] // P_CHUNK):
        @pl.when(tot_s[tau] > c * P_CHUNK)
        def _(c=c):
            acc_ref[...] += chunk(c)

    y = alpha * x1_ref[...] + mod_ref[5:6, :] * acc_ref[...]
    o_ref[...] = _layer_norm(y, lg_ref[...], lb_ref[...])


def _moe_combine(alpha, ptab, tot, gates, loc_et, np_et, h2, x1, mod, sgu, sd, ln_g, ln_b, ys,
                 S):
    T, D = h2.shape
    ts = SORT_TILE
    per_b = S // ts
    row = lambda a: pl.BlockSpec(a.shape, lambda t, *_: (0, 0))
    grid_spec = pltpu.PrefetchScalarGridSpec(
        num_scalar_prefetch=2,
        grid=(T // ts,),
        in_specs=[pl.BlockSpec((ts, LANES), lambda t, *_: (t, 0)),
                  row(loc_et), row(np_et),
                  pl.BlockSpec((ts, D), lambda t, *_: (t, 0)),
                  pl.BlockSpec((ts, D), lambda t, *_: (t, 0)),
                  pl.BlockSpec((None, 6, D), lambda t, *_: (t // per_b, 0, 0)),
                  row(sgu), row(sd), row(ln_g), row(ln_b),
                  pl.BlockSpec(memory_space=pl.ANY)],
        out_specs=pl.BlockSpec((ts, D), lambda t, *_: (t, 0)),
        scratch_shapes=[pltpu.VMEM((2, _local_rows_bound(ts), D), BF16),
                        pltpu.VMEM((ts, D), F32),
                        pltpu.SemaphoreType.DMA((2,))])
    return pl.pallas_call(
        functools.partial(_moe_combine_kernel, alpha),
        grid_spec=grid_spec,
        out_shape=jax.ShapeDtypeStruct((T, D), F32),
        compiler_params=pltpu.CompilerParams(dimension_semantics=("arbitrary",),
                                             vmem_limit_bytes=VMEM_LIMIT),
        name="moe_combine",
    )(ptab, tot, gates, loc_et, np_et, h2, x1, mod, sgu, sd, ln_g, ln_b, ys)


def _moe(alpha, h2, x1, gates, gates_t, mod, w_gate, w_up, w_down, sgu, sd, ln_g, ln_b):
    B, S, D = x1.shape
    T = B * S
    n_t = T // SORT_TILE
    n_bound = _sorted_tiles_bound(T)
    n_tab = -(-n_bound // LANES) * LANES
    ptab, loc_et, np_et, loc_te, np_te, tot, erow, texp, nused = _moe_meta(gates_t, n_tab)
    ptab = ptab.reshape(-1)
    tot = tot[0, :n_t]
    h2 = h2.reshape(T, D)
    x1 = x1.reshape(T, D)
    gates = gates.reshape(T, LANES)
    loc_rows = loc_te[:n_t].reshape(n_t, 1, LANES)
    np_rows = np_te[:n_t].reshape(n_t, 1, LANES)

    def sized(n_tiles):
        def run():
            xs = _moe_sort(ptab, tot, erow[0, :N_EXPERTS], erow[1, :N_EXPERTS], gates_t, gates,
                           h2, loc_rows, np_rows, n_tiles * EXP_TILE)
            ys = _moe_expert(texp[0], nused[0, :1], xs, w_gate, w_up, w_down, n_tiles)
            return _moe_combine(alpha, ptab, tot, gates, loc_et, np_et, h2, x1, mod, sgu, sd,
                                ln_g, ln_b, ys, S)
        return run

    n_small = min(_sorted_tiles_expected(T), n_bound)
    out = lax.cond(nused[0, 0] <= n_small, sized(n_small), sized(n_bound))
    return out.reshape(B, S, D)


def _rearrange_w_in(w):
    d_in = w.shape[0]
    scale = HEAD_DIM ** -0.5 * LOG2E
    fq, fk, fv = w[:, 0:512], w[:, 512:1024], w[:, 1024:1536]
    ff = w[:, 1536:1544]
    nq = w[:, 1544:2056]
    kc, vc, ks, vs, kw, vw = (w[:, 2056 + k * LANES:2056 + (k + 1) * LANES] for k in range(6))
    ng = w[:, 2824:2848]
    pad = jnp.zeros((d_in, LANES - ff.shape[1] - ng.shape[1]), w.dtype)
    cols = [fq * scale, fk, nq * scale, kc, kw, vc, ks, fv, vs, vw, ff, ng, pad]
    return jnp.concatenate(cols, axis=1).astype(BF16)


def _compress_weights(pos, w1, w2):
    half = CMP_BLOCK // 2
    w1r = w1.reshape(2, half, HEAD_DIM, CMP_HIDDEN)
    zeros = jnp.zeros_like(w1r[0])
    def spread(part):
        g0 = jnp.stack([part, zeros], axis=1).reshape(half * 2 * HEAD_DIM, CMP_HIDDEN)
        g1 = jnp.stack([zeros, part], axis=1).reshape(half * 2 * HEAD_DIM, CMP_HIDDEN)
        return jnp.concatenate([g0, g1], axis=1).astype(BF16)
    wa, wb = spread(w1r[0]), spread(w1r[1])
    z2 = jnp.zeros_like(w2)
    w2bd = jnp.concatenate([jnp.concatenate([w2, z2], axis=1),
                            jnp.concatenate([z2, w2], axis=1)], axis=0).astype(BF16)
    posr = pos.reshape(2, half, 1, HEAD_DIM)
    posr = jnp.broadcast_to(posr, (2, half, NSA_GROUPS, HEAD_DIM)).reshape(2, half * 2 * HEAD_DIM)
    return posr, wa, wb, w2bd


@functools.lru_cache(maxsize=None)
def _static_tables(S):
    tq = ATT_TILE
    n_cmp = (S - CMP_BLOCK) // CMP_STRIDE + 1
    n_pad = S // CMP_STRIDE
    n_slc = S // SEL_BLOCK
    t = np.arange(S)[:, None]
    n = np.arange(n_pad)[None, :]
    bucket_c = _t5_bucket_np(t - (n * CMP_STRIDE + CMP_BLOCK - 1)).reshape(1, -1)
    d = (np.arange(4)[:, None, None] * tq + np.arange(tq)[None, :, None]
         - np.arange(2 * tq)[None, None, :])
    bucket_w = _t5_bucket_np(d).reshape(1, -1)
    cs = np.arange(n_pad)[None, :] * CMP_STRIDE
    sj = np.arange(n_slc)[:, None] * SEL_BLOCK
    ovl_t = ((cs < sj + SEL_BLOCK) & (cs + CMP_BLOCK > sj) & (np.arange(n_pad)[None, :] < n_cmp))
    return bucket_c, bucket_w, ovl_t.astype(np.float32)


def kernel(x, c, w_ada, b_ada, w_in, b_f, cmp_pos_k, cmp_w1_k, cmp_w2_k, cmp_pos_v, cmp_w1_v,
           cmp_w2_v, rel_bias, w_out, ln1_g, ln1_b, w_router, e_bias, w_gate, w_up, w_down,
           ws_gate, ws_up, ws_down, ln2_g, ln2_b):
    B, S, D = x.shape
    depth = w_ada.shape[0]
    alpha = (2 * depth) ** 0.25
    tq = ATT_TILE
    assert w_in.shape[-1] == 3 * FOX_HEADS * HEAD_DIM + FOX_HEADS + NSA_HEADS * HEAD_DIM \
        + 6 * NSA_GROUPS * HEAD_DIM + 3 * NSA_HEADS
    assert NSA_GROUPS * HEAD_DIM == LANES and S // CMP_STRIDE == LANES
    assert WINDOW == 2 * tq and S // SEL_BLOCK <= HEAD_DIM // 2
    assert S % FOX_TILE == 0 and S % CMP_TILE == 0 and (B * S) % SORT_TILE == 0
    assert w_gate.shape[1:] == (N_EXPERTS, D, w_down.shape[2])
    bucket_c, bucket_w, ovl_t = _static_tables(S)
    rel_bias_t = rel_bias.T * LOG2E
    bias_c = _bias_table(jnp.asarray(bucket_c), rel_bias_t).reshape(NSA_HEADS, S, S // CMP_STRIDE)
    w4 = _bias_table(jnp.asarray(bucket_w), rel_bias_t).reshape(NSA_HEADS, 4, tq, 2 * tq)
    ovl_t = jnp.asarray(ovl_t, BF16)

    for l in range(depth):
        mod = _ada(c, w_ada[l], b_ada[l]).reshape(B, 6, D)
        bf_row = jnp.zeros((1, LANES), F32).at[0, :FOX_HEADS].set(b_f[l])
        (fq, fk, nq, kc, kw, vc, ks, fv, vs, vw, misc, misc_t) = _in_proj(
            x, mod, _rearrange_w_in(w_in[l]), bf_row)

        o_fox = _fox(fq, fk, fv, misc_t[:, :FOX_HEADS, :])

        pk, wak, wbk, w2k = _compress_weights(cmp_pos_k[l], cmp_w1_k[l], cmp_w2_k[l])
        pv, wav, wbv, w2v = _compress_weights(cmp_pos_v[l], cmp_w1_v[l], cmp_w2_v[l])
        rows = S // CMP_STRIDE
        kcmp, vcmp = _compress(kc.reshape(B, rows, CMP_STRIDE * LANES),
                               vc.reshape(B, rows, CMP_STRIDE * LANES),
                               pk, pv, wak, wbk, wav, wbv, w2k, w2v)

        oc, sel = _cmp_sel(nq, kcmp, vcmp, bias_c, misc, ovl_t)
        o_nsa = _nsa(nq, ks, vs, kw, vw, sel, w4, misc, oc)

        x1, h2, gates, gates_t = _out_proj(
            alpha, o_fox, o_nsa, x, mod, w_out[l].astype(BF16), ln1_g[l].reshape(1, D),
            ln1_b[l].reshape(1, D), w_router[l].T, e_bias[l].reshape(N_EXPERTS, 1))

        sgu = jnp.concatenate([ws_gate[l], ws_up[l]], axis=-1).astype(BF16)
        x = _moe(alpha, h2, x1, gates, gates_t, mod, w_gate[l], w_up[l], w_down[l], sgu,
                 ws_down[l].astype(BF16), ln2_g[l].reshape(1, D), ln2_b[l].reshape(1, D))
    return x
```

```python
import functools
import math

import jax
import jax.numpy as jnp
import numpy as np
from jax import lax
from jax.experimental import pallas as pl
from jax.experimental.pallas import tpu as pltpu

F32 = jnp.float32
BF16 = jnp.bfloat16

HEAD_DIM = 64
FOX_HEADS = 8
NSA_HEADS = 8
NSA_GQA = 4
NSA_GROUPS = NSA_HEADS // NSA_GQA
CMP_BLOCK = 32
CMP_STRIDE = 16
CMP_HIDDEN = 256
SEL_BLOCK = 64
N_SEL = 16
WINDOW = 512
N_BUCKETS = 32
MAX_DISTANCE = 128
N_EXPERTS = 64
N_EXPERT_GROUPS = 8
GROUP_SIZE = N_EXPERTS // N_EXPERT_GROUPS
TOPK_GROUPS = 4
TOP_K = 8
ROUTED_SCALE = 2.5
LN_EPS = 1e-5
NEG_BIG = -1e30
FORCE_SCORE = 1e4

LANES = 128
ATT_TILE = 256
FOX_TILE = 512
CMP_TILE = 1024
ROW_TILE = 512
VMEM_LIMIT = 48 * 1024 * 1024

NT_DIMS = (((1,), (1,)), ((), ()))


def _dot(a, b):
    return jnp.dot(a, b, preferred_element_type=F32)


def _dot_nt(a, b):
    return lax.dot_general(a, b, NT_DIMS, preferred_element_type=F32)


def _split3(x):
    hi = x.astype(BF16)
    r1 = x - hi.astype(F32)
    mid = r1.astype(BF16)
    lo = (r1 - mid.astype(F32)).astype(BF16)
    return hi, mid, lo


def _silu(x):
    return x / (1.0 + jnp.exp(-x))


def _sigmoid(x):
    return 1.0 / (1.0 + jnp.exp(-x))


def _swap_halves(x):
    return pltpu.roll(x, HEAD_DIM, 1)


def _t5_bucket_np(dist):
    n = np.maximum(dist, 0)
    max_exact = N_BUCKETS // 2
    nf = np.maximum(n, 1).astype(np.float32)
    large = max_exact + (np.log(nf / max_exact) / math.log(MAX_DISTANCE / max_exact)
                         * (N_BUCKETS - max_exact)).astype(np.int32)
    large = np.minimum(large, N_BUCKETS - 1)
    return np.where(n < max_exact, n, large).astype(np.int32)


def _ada_kernel(c_ref, w_ref, b_ref, o_ref):
    c = c_ref[...]
    o_ref[...] = jnp.dot(_silu(c), w_ref[...], preferred_element_type=F32,
                         precision=lax.Precision.HIGHEST) + b_ref[...]


def _ada(c, w_ada, b_ada):
    B, D = c.shape
    n_out = w_ada.shape[1]
    tn = 1024
    return pl.pallas_call(
        _ada_kernel,
        grid=(n_out // tn,),
        in_specs=[pl.BlockSpec((B, D), lambda j: (0, 0)),
                  pl.BlockSpec((D, tn), lambda j: (0, j)),
                  pl.BlockSpec((1, tn), lambda j: (0, j))],
        out_specs=pl.BlockSpec((B, tn), lambda j: (0, j)),
        out_shape=jax.ShapeDtypeStruct((B, n_out), F32),
        compiler_params=pltpu.CompilerParams(dimension_semantics=("arbitrary",),
                                             vmem_limit_bytes=VMEM_LIMIT),
        name="ada",
    )(c, w_ada, b_ada.reshape(1, n_out))


def _bias_table_kernel(bkt_ref, rbt_ref, o_ref):
    bkt = bkt_ref[...]
    k = lax.broadcasted_iota(jnp.int32, (N_BUCKETS, bkt.shape[1]), 0)
    onehot = jnp.where(k == bkt, 1.0, 0.0).astype(BF16)
    hi, mid, lo = _split3(rbt_ref[...])
    o_ref[...] = _dot(hi, onehot) + _dot(mid, onehot) + _dot(lo, onehot)


def _bias_table(bucket, rel_bias_t):
    n = bucket.shape[1]
    chunk = 32768
    n_heads = rel_bias_t.shape[0]
    return pl.pallas_call(
        _bias_table_kernel,
        grid=(n // chunk,),
        in_specs=[pl.BlockSpec((1, chunk), lambda j: (0, j)),
                  pl.BlockSpec(rel_bias_t.shape, lambda j: (0, 0))],
        out_specs=pl.BlockSpec((n_heads, chunk), lambda j: (0, j)),
        out_shape=jax.ShapeDtypeStruct((n_heads, n), F32),
        compiler_params=pltpu.CompilerParams(dimension_semantics=("parallel",),
                                             vmem_limit_bytes=VMEM_LIMIT),
        name="bias_table",
    )(bucket, rel_bias_t)


_C_FQ, _C_FK, _C_NQ = 0, 512, 1024
_C_K3 = 1536
_C_SK = 1920
_C_FV = 2048
_C_SV = 2560
_C_WV = 2688
_C_MISC = 2816
_IN_COLS = 2944
LOG2E = math.log2(math.e)


def _in_proj_kernel(x_ref, mod_ref, w_ref, bf_ref, fq_ref, fk_ref, nq_ref, kc_ref, kw_ref,
                    vc_ref, ks_ref, fv_ref, vs_ref, vw_ref, misc_ref, misct_ref, carry_ref):
    s_idx = pl.program_id(1)
    tm = x_ref.shape[0]
    mod = mod_ref[...]
    h = (x_ref[...] * (1.0 + mod[1:2, :]) + mod[0:1, :]).astype(BF16)

    for ref, c0 in ((fq_ref, _C_FQ), (fk_ref, _C_FK)):
        ref[...] = _dot(h, w_ref[:, c0:c0 + 512]).astype(ref.dtype)
    nsa_q = _dot(h, w_ref[:, _C_NQ:_C_K3])
    keys = _dot(h, w_ref[:, _C_K3:_C_FV])
    fox_v = _dot(h, w_ref[:, _C_FV:_C_SV])
    tail = _dot(h, w_ref[:, _C_SV:_IN_COLS])
    for k, ref in enumerate((kc_ref, kw_ref, vc_ref)):
        ref[...] = keys[:, k * LANES:(k + 1) * LANES].astype(ref.dtype)

    lane = lax.broadcasted_iota(jnp.int32, (tm, LANES), 1)
    low = lane < HEAD_DIM

    def spread(ref, cols, fill):
        for p in range(cols.shape[-1] // LANES):
            r = cols[:, p * LANES:(p + 1) * LANES]
            ref[:, 2 * p * LANES:(2 * p + 1) * LANES] = jnp.where(low, r, fill).astype(ref.dtype)
            ref[:, (2 * p + 1) * LANES:(2 * p + 2) * LANES] = jnp.where(
                low, _swap_halves(r), fill).astype(ref.dtype)

    for p in range(nsa_q.shape[-1] // LANES):
        r = nsa_q[:, p * LANES:(p + 1) * LANES]
        swapped = _swap_halves(r)
        nq_ref[:, 2 * p * LANES:(2 * p + 1) * LANES] = jnp.where(low, r, swapped).astype(
            nq_ref.dtype)
        nq_ref[:, (2 * p + 1) * LANES:(2 * p + 2) * LANES] = jnp.where(low, swapped, r).astype(
            nq_ref.dtype)

    key_blk = lax.shift_right_logical(
        s_idx * tm + lax.broadcasted_iota(jnp.int32, (tm, LANES), 0), int(math.log2(SEL_BLOCK)))
    spread(ks_ref, keys[:, 3 * LANES:4 * LANES],
           jnp.where(lane == HEAD_DIM + key_blk, 1.0, 0.0))
    spread(fv_ref, fox_v, 1.0)
    spread(vs_ref, tail[:, 0:LANES], 1.0)
    spread(vw_ref, tail[:, LANES:2 * LANES], 1.0)

    z = tail[:, 2 * LANES:3 * LANES] + bf_ref[...]
    is_f = lane < FOX_HEADS
    log_f = jnp.minimum(z, 0.0) - jnp.log(1.0 + jnp.exp(-jnp.abs(z)))
    log_f = jnp.where(is_f, log_f, 0.0)

    row = lax.broadcasted_iota(jnp.int32, (tm, tm), 0)
    col = lax.broadcasted_iota(jnp.int32, (tm, tm), 1)
    tri = jnp.where(row >= col, 1.0, 0.0).astype(BF16)
    sums = _dot(tri, jnp.concatenate(_split3(log_f), axis=1))
    local = sums[:, 0:LANES] + sums[:, LANES:2 * LANES] + sums[:, 2 * LANES:3 * LANES]

    @pl.when(s_idx == 0)
    def _():
        carry_ref[...] = jnp.zeros_like(carry_ref)

    cum = local + carry_ref[...]
    carry_ref[...] = cum[tm - 1:tm, :]
    misc = jnp.where(is_f, cum * LOG2E, _sigmoid(z))
    misc_ref[...] = misc
    misct_ref[...] = misc.T


def _in_proj(x, mod, w_r, bf_row):
    B, S, D = x.shape
    tm = ROW_TILE
    widths = (512, 512, 2 * (_C_K3 - _C_NQ), LANES, LANES, LANES, 2 * (_C_FV - _C_SK),
              2 * (_C_SV - _C_FV),
              2 * (_C_WV - _C_SV), 2 * (_C_MISC - _C_WV))
    wide = lambda w: pl.BlockSpec((None, tm, w), lambda b, s: (b, s, 0))
    out_shape = ([jax.ShapeDtypeStruct((B, S, w), BF16) for w in widths]
                 + [jax.ShapeDtypeStruct((B, S, LANES), F32),
                    jax.ShapeDtypeStruct((B, LANES, S), F32)])
    out_specs = ([wide(w) for w in widths]
                 + [wide(LANES), pl.BlockSpec((None, LANES, tm), lambda b, s: (b, 0, s))])
    return pl.pallas_call(
        _in_proj_kernel,
        grid=(B, S // tm),
        in_specs=[pl.BlockSpec((None, tm, D), lambda b, s: (b, s, 0)),
                  pl.BlockSpec((None, 6, D), lambda b, s: (b, 0, 0)),
                  pl.BlockSpec((D, _IN_COLS), lambda b, s: (0, 0)),
                  pl.BlockSpec((1, LANES), lambda b, s: (0, 0))],
        out_specs=out_specs,
        out_shape=out_shape,
        scratch_shapes=[pltpu.VMEM((1, LANES), F32)],
        compiler_params=pltpu.CompilerParams(dimension_semantics=("parallel", "arbitrary"),
                                             vmem_limit_bytes=VMEM_LIMIT),
        name="in_proj",
    )(x, mod, w_r, bf_row)


def _softmax_weights(s, m):
    return jnp.exp2((s - m).astype(BF16))


def _flash_update(carry, s, vt):
    m, acc = carry
    m_new = jnp.maximum(m, jnp.max(s, axis=-1, keepdims=True))
    alpha = jnp.exp2(m - m_new)
    p = _softmax_weights(s, m_new)
    rows = acc.shape[0]
    acc = alpha.reshape(rows, 1) * acc + _dot(p.reshape(rows, s.shape[-1]), vt)
    return m_new, acc


def _normalize(acc):
    return acc / _swap_halves(acc)


def _fox_kernel(q_ref, k_ref, v_ref, ck_ref, o_ref):
    i = pl.program_id(2)
    tq = q_ref.shape[0]
    tk = ck_ref.shape[2]
    q2 = q_ref[...].astype(F32)
    lane = lax.broadcasted_iota(jnp.int32, (tq, LANES), 1)
    low = lane < HEAD_DIM
    halves = (low, jnp.logical_not(low))
    qh = [jnp.where(h, q2, 0.0).astype(BF16) for h in halves]
    col_minus_row = (lax.broadcasted_iota(jnp.int32, (tq, tk), 1)
                     - lax.broadcasted_iota(jnp.int32, (tq, tk), 0))

    def step(jj, carry, diagonal):
        k0 = jj * tk
        kt = k_ref[pl.ds(k0, tk), :]
        new = []
        for hh in range(2):
            s = _dot_nt(qh[hh], kt) - ck_ref[hh, pl.ds(jj, 1), :]
            if diagonal:
                s = jnp.where(col_minus_row <= 0, s, NEG_BIG)
            vt = v_ref[pl.ds(k0, tk), hh * LANES:(hh + 1) * LANES]
            new.append(_flash_update(carry[hh], s, vt))
        return tuple(new)

    init = tuple((jnp.full((tq, 1), NEG_BIG, F32), jnp.zeros((tq, LANES), F32))
                 for _ in range(2))

    assert tq == tk
    for n_full in range(k_ref.shape[0] // tk):
        @pl.when(i == n_full)
        def _(n_full=n_full):
            carry = init
            for jj in range(n_full):
                carry = step(jj, carry, False)
            carry = step(n_full, carry, True)
            o_ref[...] = jnp.where(low, _normalize(carry[0][1]),
                                   _swap_halves(_normalize(carry[1][1]))).astype(o_ref.dtype)


def _fox(fq, fk, fv, cum_row):
    B, S, W = fq.shape
    tq = tk = FOX_TILE
    n_pairs = W // LANES
    cum_row = cum_row.reshape(B, n_pairs, 2, S // tk, tk)
    return pl.pallas_call(
        _fox_kernel,
        grid=(B, n_pairs, S // tq),
        in_specs=[pl.BlockSpec((None, tq, LANES), lambda b, p, i: (b, i, p)),
                  pl.BlockSpec((None, S, LANES), lambda b, p, i: (b, 0, p)),
                  pl.BlockSpec((None, S, 2 * LANES), lambda b, p, i: (b, 0, p)),
                  pl.BlockSpec((None, None, 2, S // tk, tk), lambda b, p, i: (b, p, 0, 0, 0))],
        out_specs=pl.BlockSpec((None, tq, LANES), lambda b, p, i: (b, i, p)),
        out_shape=jax.ShapeDtypeStruct((B, S, W), BF16),
        compiler_params=pltpu.CompilerParams(
            dimension_semantics=("parallel", "parallel", "arbitrary"),
            vmem_limit_bytes=VMEM_LIMIT),
        name="fox",
    )(fq, fk, fv, cum_row)


def _compress_kernel(xk_ref, xv_ref, pk_ref, pv_ref, wak_ref, wbk_ref, wav_ref, wbv_ref,
                     w2k_ref, w2v_ref, ok_ref, ov_ref):
    n_rows = xk_ref.shape[0]
    for x_ref, p_ref, wa_ref, wb_ref, w2_ref, o_ref in (
            (xk_ref, pk_ref, wak_ref, wbk_ref, w2k_ref, ok_ref),
            (xv_ref, pv_ref, wav_ref, wbv_ref, w2v_ref, ov_ref)):
        x = x_ref[...].astype(F32)
        xa = (x + p_ref[0:1, :]).astype(BF16)
        xb = (x + p_ref[1:2, :]).astype(BF16)
        hb = _dot(xb, wb_ref[...])
        h1 = _dot(xa, wa_ref[...]) + pltpu.roll(hb, n_rows - 1, 0)
        o_ref[...] = _dot(_silu(h1).astype(BF16), w2_ref[...]).astype(o_ref.dtype)


def _compress(xk, xv, pk, pv, wak, wbk, wav, wbv, w2k, w2v):
    B, R, C = xk.shape
    xspec = pl.BlockSpec((None, R, C), lambda b: (b, 0, 0))
    full = lambda a: pl.BlockSpec(a.shape, lambda b: (0,) * a.ndim)
    ospec = pl.BlockSpec((None, R, LANES), lambda b: (b, 0, 0))
    return pl.pallas_call(
        _compress_kernel,
        grid=(B,),
        in_specs=[xspec, xspec] + [full(a) for a in (pk, pv, wak, wbk, wav, wbv, w2k, w2v)],
        out_specs=[ospec, ospec],
        out_shape=[jax.ShapeDtypeStruct((B, R, LANES), BF16)] * 2,
        compiler_params=pltpu.CompilerParams(dimension_semantics=("parallel",),
                                             vmem_limit_bytes=VMEM_LIMIT),
        name="compress",
    )(xk, xv, pk, pv, wak, wbk, wav, wbv, w2k, w2v)


def _rank_rows(score):
    n = score.shape[0]
    j = lax.broadcasted_iota(jnp.int32, score.shape, 0)
    rank = jnp.zeros(score.shape, jnp.int32)
    for i in range(n):
        si = score[i:i + 1, :]
        beats = (si > score) | ((si == score) & (j > i))
        rank = rank + jnp.where(beats, 1, 0)
    return rank


def _dup_head(q_ref, r):
    return q_ref[:, r * LANES:(r + 1) * LANES].astype(F32)


def _pack_heads(o_list, g):
    lane = lax.broadcasted_iota(jnp.int32, o_list[0].shape, 1)
    in_g = (lane >= g * HEAD_DIM) & (lane < (g + 1) * HEAD_DIM)
    both = []
    for o in o_list:
        om = jnp.where(in_g, o, 0.0)
        both.append(om + _swap_halves(om))
    pairs = [jnp.where(lane < HEAD_DIM, both[2 * p], both[2 * p + 1]) for p in range(2)]
    return jnp.concatenate(pairs, axis=1)


def _group_gates(misc, g):
    w = 3 * NSA_GQA
    gates = misc[:, FOX_HEADS:FOX_HEADS + w]
    for other in range(1, NSA_GROUPS):
        gates = jnp.where(g == other, misc[:, FOX_HEADS + other * w:FOX_HEADS + (other + 1) * w],
                          gates)
    return gates


def _cmp_sel_kernel(q_ref, kc_ref, vc_ref, bias_ref, gate_ref, ovl_ref, oc_ref, sel_ref):
    g = pl.program_id(1)
    i = pl.program_id(2)
    tq = q_ref.shape[0]
    n_pad = kc_ref.shape[0]
    lane = lax.broadcasted_iota(jnp.int32, (n_pad, LANES), 1)
    in_g = (lane >= g * HEAD_DIM) & (lane < (g + 1) * HEAD_DIM)
    kc = jnp.where(in_g, kc_ref[...].astype(F32), 0.0).astype(BF16)
    vc = vc_ref[...]
    t = i * tq + lax.broadcasted_iota(jnp.int32, (tq, n_pad), 0)
    n = lax.broadcasted_iota(jnp.int32, (tq, n_pad), 1)
    valid = t >= n * CMP_STRIDE + (CMP_BLOCK - 1)
    gates = _group_gates(gate_ref[...], g)
    p_sum = jnp.zeros((tq, n_pad), F32)
    outs = []
    for r in range(NSA_GQA):
        qr = q_ref[:, r * LANES:(r + 1) * LANES]
        s = _dot_nt(qr, kc)
        s = jnp.where(valid, s + bias_ref[r], NEG_BIG)
        m = jnp.max(s, axis=-1, keepdims=True)
        p = jnp.exp2(s - m)
        p = p / jnp.sum(p, axis=-1, keepdims=True)
        p = jnp.where(valid, p, 0.0)
        p_sum = p_sum + p
        outs.append(_dot(p.astype(BF16), vc) * gates[:, 3 * r:3 * r + 1])
    oc_ref[...] = _pack_heads(outs, g).astype(oc_ref.dtype)

    ovl = ovl_ref[...]
    hi, mid, lo = _split3(p_sum)
    imp = _dot_nt(ovl, hi) + _dot_nt(ovl, mid) + _dot_nt(ovl, lo)
    n_blk = imp.shape[0]
    j = lax.broadcasted_iota(jnp.int32, (n_blk, tq), 0)
    qb = jnp.right_shift(i * tq + lax.broadcasted_iota(jnp.int32, (n_blk, tq), 1),
                         int(math.log2(SEL_BLOCK)))
    forced = (j == 0) | (j == qb) | (j == qb - 1)
    causal = j <= qb
    score = jnp.where(causal, imp + jnp.where(forced, FORCE_SCORE, 0.0), -FORCE_SCORE)
    chosen = (_rank_rows(score) < N_SEL) & causal
    sel = jnp.where(chosen, 1.0, 0.0)
    sel = jnp.concatenate([sel, jnp.zeros((LANES - n_blk, tq), F32)], axis=0)
    sel_ref[...] = sel.T


def _cmp_sel(nq, kcmp, vcmp, bias_c, gates_g, ovl_t):
    B, S, _ = nq.shape
    tq = CMP_TILE
    n_pad = kcmp.shape[1]
    return pl.pallas_call(
        _cmp_sel_kernel,
        grid=(B, NSA_GROUPS, S // tq),
        in_specs=[pl.BlockSpec((None, tq, NSA_GQA * LANES), lambda b, g, i: (b, i, g)),
                  pl.BlockSpec((None, n_pad, LANES), lambda b, g, i: (b, 0, 0)),
                  pl.BlockSpec((None, n_pad, LANES), lambda b, g, i: (b, 0, 0)),
                  pl.BlockSpec((NSA_GQA, tq, n_pad), lambda b, g, i: (g, i, 0)),
                  pl.BlockSpec((None, tq, LANES), lambda b, g, i: (b, i, 0)),
                  pl.BlockSpec(ovl_t.shape, lambda b, g, i: (0, 0))],
        out_specs=[pl.BlockSpec((None, tq, 2 * LANES), lambda b, g, i: (b, i, g)),
                   pl.BlockSpec((None, None, tq, LANES), lambda b, g, i: (b, g, i, 0))],
        out_shape=[jax.ShapeDtypeStruct((B, S, NSA_HEADS * HEAD_DIM), BF16),
                   jax.ShapeDtypeStruct((B, NSA_GROUPS, S, LANES), F32)],
        compiler_params=pltpu.CompilerParams(
            dimension_semantics=("parallel", "parallel", "arbitrary"),
            vmem_limit_bytes=VMEM_LIMIT),
        name="cmp_sel",
    )(nq, kcmp, vcmp, bias_c, gates_g, ovl_t)


def _nsa_kernel(q_ref, ks_ref, vs_ref, kw_ref, vw_ref, sel_ref, w4_ref, gate_ref, oc_ref,
                o_ref):
    for tile in range(ks_ref.shape[0] // q_ref.shape[0]):
        @pl.when(pl.program_id(2) == tile)
        def _(tile=tile):
            _nsa_tile(tile, q_ref, ks_ref, vs_ref, kw_ref, vw_ref, sel_ref, w4_ref, gate_ref,
                      oc_ref, o_ref)


def _nsa_tile(i, q_ref, ks_ref, vs_ref, kw_ref, vw_ref, sel_ref, w4_ref, gate_ref, oc_ref, o_ref):
    g = pl.program_id(0)
    tq = q_ref.shape[0]
    tk = w4_ref.shape[-1]
    H = NSA_GQA
    lane = lax.broadcasted_iota(jnp.int32, (tq, LANES), 1)
    in_g = (lane >= g * HEAD_DIM) & (lane < (g + 1) * HEAD_DIM)
    heads = [_dup_head(q_ref, r) for r in range(H)]
    qs = jnp.concatenate([jnp.where(in_g, hd, 0.0).astype(BF16) for hd in heads], axis=0)

    blk_mask = _swap_halves((sel_ref[...] - 1.0) * (-NEG_BIG))
    qsel = jnp.concatenate(
        [jnp.where(lane < HEAD_DIM, hd, blk_mask).astype(BF16) for hd in heads], axis=0)

    def sel_step(jj, carry, diagonal):
        k0 = jj * tk
        ahead = i * tq - k0
        width = min(ahead + tq, tk) if diagonal else tk
        rows = pl.ds(k0, width)
        s = (_dot_nt(qsel, ks_ref[rows, :]).reshape(H, tq, width)
             + w4_ref[:, min(i - 2 * jj, 3), :, 0:width])
        if diagonal:
            cmr = (lax.broadcasted_iota(jnp.int32, (tq, width), 1)
                   - lax.broadcasted_iota(jnp.int32, (tq, width), 0))
            s = jnp.where((cmr <= ahead)[None], s, NEG_BIG)
        return _flash_update(carry, s, vs_ref[rows, :])

    carry = (jnp.full((H, tq, 1), NEG_BIG, F32), jnp.zeros((H * tq, LANES), F32))
    n_full = (i * tq) // tk
    for jj in range(n_full):
        carry = sel_step(jj, carry, False)
    _, acc_s = sel_step(n_full, carry, True)

    old = [max(i - d, 0) * tq for d in (2, 1)]
    now = i * tq
    k_old = jnp.concatenate([kw_ref[pl.ds(st, tq), :] for st in old], axis=0)
    v_old = jnp.concatenate([vw_ref[pl.ds(st, tq), :] for st in old], axis=0)
    col = lax.broadcasted_iota(jnp.int32, (tq, 2 * tq), 1)
    cmr = col - lax.broadcasted_iota(jnp.int32, (tq, 2 * tq), 0)
    never = 4 * tq
    ok_old = (((col < tq) & (cmr > (0 if i >= 2 else never)))
              | (col >= (tq if i >= 1 else never)))
    s_old = jnp.where(ok_old[None], _dot_nt(qs, k_old).reshape(H, tq, 2 * tq) + w4_ref[:, 2],
                      NEG_BIG)
    s_now = jnp.where((cmr[:, 0:tq] <= 0)[None],
                      _dot_nt(qs, kw_ref[pl.ds(now, tq), :]).reshape(H, tq, tq)
                      + w4_ref[:, 0, :, 0:tq], NEG_BIG)
    m_w = jnp.maximum(jnp.max(s_old, axis=-1, keepdims=True),
                      jnp.max(s_now, axis=-1, keepdims=True))
    acc_w = (_dot(_softmax_weights(s_old, m_w).reshape(H * tq, 2 * tq), v_old)
             + _dot(_softmax_weights(s_now, m_w).reshape(H * tq, tq), vw_ref[pl.ds(now, tq), :]))

    gates = _group_gates(gate_ref[...], g)
    o_s = _normalize(acc_s)
    o_w = _normalize(acc_w)
    outs = []
    for r in range(H):
        sl = slice(r * tq, (r + 1) * tq)
        outs.append(o_s[sl] * gates[:, 3 * r + 1:3 * r + 2] + o_w[sl] * gates[:, 3 * r + 2:3 * r + 3])
    lane = lax.broadcasted_iota(jnp.int32, (tq, LANES), 1)
    pairs = [jnp.where(lane < HEAD_DIM, outs[2 * p], _swap_halves(outs[2 * p + 1]))
             for p in range(H // 2)]
    o_ref[...] = (jnp.concatenate(pairs, axis=1) + oc_ref[...].astype(F32)).astype(o_ref.dtype)


def _nsa(nq, ks, vs, kw, vw, sel, w4, misc, oc):
    B, S, _ = nq.shape
    tq = ATT_TILE
    tk = w4.shape[-1]
    both = lambda: pl.BlockSpec((None, S, LANES), lambda g, b, i: (b, 0, 0))
    mine = lambda: pl.BlockSpec((None, S, LANES), lambda g, b, i: (b, 0, g))
    return pl.pallas_call(
        _nsa_kernel,
        grid=(NSA_GROUPS, B, S // tq),
        in_specs=[pl.BlockSpec((None, tq, NSA_GQA * LANES), lambda g, b, i: (b, i, g)),
                  mine(), mine(), both(), mine(),
                  pl.BlockSpec((None, None, tq, LANES), lambda g, b, i: (b, g, i, 0)),
                  pl.BlockSpec((NSA_GQA, 4, tq, tk), lambda g, b, i: (g, 0, 0, 0)),
                  pl.BlockSpec((None, tq, LANES), lambda g, b, i: (b, i, 0)),
                  pl.BlockSpec((None, tq, 2 * LANES), lambda g, b, i: (b, i, g))],
        out_specs=pl.BlockSpec((None, tq, 2 * LANES), lambda g, b, i: (b, i, g)),
        out_shape=jax.ShapeDtypeStruct((B, S, NSA_HEADS * HEAD_DIM), BF16),
        compiler_params=pltpu.CompilerParams(
            dimension_semantics=("parallel", "parallel", "arbitrary"),
            vmem_limit_bytes=VMEM_LIMIT),
        name="nsa",
    )(nq, ks, vs, kw, vw, sel, w4, misc, oc)


def _layer_norm(y, g, b):
    mu = jnp.mean(y, axis=-1, keepdims=True)
    yc = y - mu
    var = jnp.mean(yc * yc, axis=-1, keepdims=True)
    return yc * lax.rsqrt(var + LN_EPS) * g + b


def _top_rows(score, k):
    n = score.shape[0]
    idx = lax.broadcasted_iota(jnp.int32, score.shape, 0).astype(F32)
    alive = jnp.ones(score.shape, F32)
    for _ in range(k):
        live = alive > 0.0
        best = jnp.max(jnp.where(live, score, -jnp.inf), axis=0, keepdims=True)
        first = jnp.min(jnp.where(live & (score == best), idx, float(n)), axis=0, keepdims=True)
        alive = jnp.where(idx == first, 0.0, alive)
    return alive == 0.0


def _router_gates_t(h2, wr_t, eb_col):
    tm = h2.shape[0]
    h_hi, h_lo, _ = _split3(h2)
    w_hi, w_lo, _ = _split3(wr_t)
    logit = _dot_nt(w_hi, h_hi) + _dot_nt(w_hi, h_lo) + _dot_nt(w_lo, h_hi)
    scores = _sigmoid(logit)
    biased = scores + eb_col
    e_in = lax.broadcasted_iota(jnp.int32, (GROUP_SIZE, tm), 0).astype(F32)
    gs_rows = []
    for gi in range(N_EXPERT_GROUPS):
        grp = biased[gi * GROUP_SIZE:(gi + 1) * GROUP_SIZE, :]
        m1 = jnp.max(grp, axis=0, keepdims=True)
        first = jnp.min(jnp.where(grp == m1, e_in, float(GROUP_SIZE)), axis=0, keepdims=True)
        m2 = jnp.max(jnp.where(e_in == first, -jnp.inf, grp), axis=0, keepdims=True)
        gs_rows.append(m1 + m2)
    gscore = jnp.concatenate(gs_rows, axis=0)
    g_keep = _rank_rows(gscore) < TOPK_GROUPS
    keep = jnp.concatenate(
        [jnp.broadcast_to(g_keep[gi:gi + 1, :], (GROUP_SIZE, tm)) for gi in range(N_EXPERT_GROUPS)],
        axis=0)
    masked = jnp.where(keep, biased, -jnp.inf)
    chosen = _top_rows(masked, TOP_K)
    w = jnp.where(chosen, scores, 0.0)
    return w / jnp.sum(w, axis=0, keepdims=True) * ROUTED_SCALE


def _out_proj_kernel(alpha, of_ref, on_ref, x_ref, mod_ref, w_ref, lg_ref, lb_ref, wr_ref,
                     eb_ref, x1_ref, h2_ref, gate_ref, gate_t_ref):
    half = of_ref.shape[1]
    mod = mod_ref[...]
    mixed = _dot(of_ref[...], w_ref[0:half, :]) + _dot(on_ref[...], w_ref[half:2 * half, :])
    y = alpha * x_ref[...] + mod[2:3, :] * mixed
    x1 = _layer_norm(y, lg_ref[...], lb_ref[...])
    x1_ref[...] = x1
    h2 = x1 * (1.0 + mod[4:5, :]) + mod[3:4, :]
    h2_ref[...] = h2.astype(h2_ref.dtype)
    gates_t = _router_gates_t(h2, wr_ref[...], eb_ref[...])
    gate_t_ref[...] = gates_t
    tm = h2.shape[0]
    gates_t = jnp.concatenate([gates_t, jnp.zeros((LANES - N_EXPERTS, tm), F32)], axis=0)
    gate_ref[...] = gates_t.T


def _out_proj(alpha, o_fox, o_nsa, x, mod, w_out, ln_g, ln_b, wr_t, eb_col):
    B, S, D = x.shape
    tm = ROW_TILE
    half = o_fox.shape[-1]
    row = lambda a: pl.BlockSpec(a.shape, lambda b, s: (0, 0))
    return pl.pallas_call(
        functools.partial(_out_proj_kernel, alpha),
        grid=(B, S // tm),
        in_specs=[pl.BlockSpec((None, tm, half), lambda b, s: (b, s, 0)),
                  pl.BlockSpec((None, tm, half), lambda b, s: (b, s, 0)),
                  pl.BlockSpec((None, tm, D), lambda b, s: (b, s, 0)),
                  pl.BlockSpec((None, 6, D), lambda b, s: (b, 0, 0)),
                  row(w_out), row(ln_g), row(ln_b), row(wr_t), row(eb_col)],
        out_specs=[pl.BlockSpec((None, tm, D), lambda b, s: (b, s, 0)),
                   pl.BlockSpec((None, tm, D), lambda b, s: (b, s, 0)),
                   pl.BlockSpec((None, tm, LANES), lambda b, s: (b, s, 0)),
                   pl.BlockSpec((N_EXPERTS, tm), lambda b, s: (0, b * (S // tm) + s))],
        out_shape=[jax.ShapeDtypeStruct((B, S, D), F32),
                   jax.ShapeDtypeStruct((B, S, D), BF16),
                   jax.ShapeDtypeStruct((B, S, LANES), F32),
                   jax.ShapeDtypeStruct((N_EXPERTS, B * S), F32)],
        compiler_params=pltpu.CompilerParams(dimension_semantics=("parallel", "parallel"),
                                             vmem_limit_bytes=VMEM_LIMIT),
        name="out_proj",
    )(o_fox, o_nsa, x, mod, w_out, ln_g, ln_b, wr_t, eb_col)


SORT_TILE = 256
ROW_ALIGN = 16
EXP_TILE = 1024
P_CHUNK = 512


def _strict_upper(n):
    return jnp.where(lax.broadcasted_iota(jnp.int32, (n, n), 0)
                     < lax.broadcasted_iota(jnp.int32, (n, n), 1), 1.0, 0.0).astype(BF16)


def _strict_lower(n):
    return jnp.where(lax.broadcasted_iota(jnp.int32, (n, n), 1)
                     < lax.broadcasted_iota(jnp.int32, (n, n), 0), 1.0, 0.0).astype(BF16)


def _local_rows_bound(ts):
    rows = TOP_K * ts + N_EXPERTS * (ROW_ALIGN - 1)
    return -(-rows // P_CHUNK) * P_CHUNK


def _piece_cols(ts):
    return -(-(_local_rows_bound(ts) // ROW_ALIGN) // LANES) * LANES


def _sorted_tiles_bound(T):
    rows = TOP_K * T + (T // SORT_TILE) * N_EXPERTS * (ROW_ALIGN - 1)
    return -(-rows // EXP_TILE) + N_EXPERTS


def _sorted_tiles_expected(T):
    groups = (T // SORT_TILE) * N_EXPERTS
    rows = TOP_K * T + groups * ((ROW_ALIGN - 1) / 2 + 2)
    return int(-(-rows // EXP_TILE) + math.ceil(0.65 * N_EXPERTS))


def _moe_meta_kernel(gt_ref, ptab_ref, loc_et_ref, np_et_ref, loc_te_ref, np_te_ref, tot_ref,
                     erow_ref, texp_ref, nused_ref):
    E, T = gt_ref.shape
    mask = jnp.where(gt_ref[...] > 0.0, 1.0, 0.0).astype(BF16)
    t_id = lax.shift_right_logical(lax.broadcasted_iota(jnp.int32, (T, LANES), 0),
                                   int(math.log2(SORT_TILE)))
    tind = jnp.where(t_id == lax.broadcasted_iota(jnp.int32, (T, LANES), 1), 1.0, 0.0)
    cnt = _dot(mask, tind.astype(BF16))
    n16 = jnp.floor((cnt + (ROW_ALIGN - 1.0)) * (1.0 / ROW_ALIGN))
    n16b = n16.astype(BF16)
    q = EXP_TILE // ROW_ALIGN
    len16 = jnp.sum(n16, axis=1, keepdims=True)
    pad16 = jnp.floor((len16 + (q - 1.0)) * (1.0 / q)) * q
    sl = _strict_lower(E)
    hi, mid, lo = _split3(jnp.broadcast_to(pad16, (E, LANES)))
    start16 = _dot(sl, hi) + _dot(sl, mid) + _dot(sl, lo)
    gdst16 = start16 + _dot(n16b, _strict_upper(LANES))
    loc16 = _dot(sl, n16b)

    def t(a):
        return jnp.concatenate([a, jnp.zeros((LANES - E, LANES), F32)], axis=0).T

    scale = float(ROW_ALIGN)
    loc_et_ref[...] = loc16 * scale
    np_et_ref[...] = n16 * scale
    loc_te_ref[...] = t(loc16) * scale
    np_te_ref[...] = t(n16) * scale
    tot_ref[...] = (jnp.sum(n16, axis=0, keepdims=True) * scale).astype(jnp.int32)

    n_t, n_blk = ptab_ref.shape
    blk = lax.broadcasted_iota(jnp.int32, (E, n_blk), 1).astype(F32)
    for tile in range(n_t):
        lo_c = loc16[:, tile:tile + 1]
        inside = (lo_c <= blk) & (blk < lo_c + n16[:, tile:tile + 1])
        dst = jnp.sum(jnp.where(inside, gdst16[:, tile:tile + 1] + (blk - lo_c), 0.0),
                      axis=0, keepdims=True)
        ptab_ref[tile:tile + 1, :] = (dst * scale).astype(jnp.int32)
    ends = jnp.concatenate([t(start16 + len16)[0:1, :], t(start16 + pad16)[0:1, :],
                            jnp.zeros((erow_ref.shape[0] - 2, LANES), F32)], axis=0)
    erow_ref[...] = (ends * scale).astype(jnp.int32)
    n_tab = texp_ref.shape[1]
    tile_row16 = (lax.broadcasted_iota(jnp.int32, (E, n_tab), 1) * q).astype(F32)
    owner = jnp.sum(jnp.where(start16[:, 0:1] <= tile_row16, 1.0, 0.0), axis=0, keepdims=True)
    texp_ref[...] = (owner - 1.0).astype(jnp.int32)
    n_used = jnp.sum(pad16, axis=0, keepdims=True) * (1.0 / q)
    nused_ref[...] = jnp.broadcast_to(n_used, (1, LANES)).astype(jnp.int32)


def _moe_meta(gates_t, n_tab):
    E, T = gates_t.shape
    i32 = jnp.int32
    return pl.pallas_call(
        _moe_meta_kernel,
        out_shape=[jax.ShapeDtypeStruct((T // SORT_TILE, _piece_cols(SORT_TILE)), i32),
                   jax.ShapeDtypeStruct((E, LANES), F32),
                   jax.ShapeDtypeStruct((E, LANES), F32),
                   jax.ShapeDtypeStruct((LANES, LANES), F32),
                   jax.ShapeDtypeStruct((LANES, LANES), F32),
                   jax.ShapeDtypeStruct((1, LANES), i32),
                   jax.ShapeDtypeStruct((8, LANES), i32),
                   jax.ShapeDtypeStruct((1, n_tab), i32),
                   jax.ShapeDtypeStruct((1, LANES), i32)],
        compiler_params=pltpu.CompilerParams(vmem_limit_bytes=VMEM_LIMIT),
        name="moe_meta",
    )(gates_t)


def _start_pieces(tile, ptab_s, tot_s, n_cols, make_copy):
    n_pieces = lax.shift_right_logical(tot_s[tile], int(math.log2(ROW_ALIGN)))
    group = 8

    def start(b):
        make_copy(pl.multiple_of(b * ROW_ALIGN, ROW_ALIGN),
                  pl.multiple_of(ptab_s[tile * n_cols + b], ROW_ALIGN)).start()

    def grouped(q, carry):
        for k in range(group):
            start(q * group + k)
        return carry

    def single(b, carry):
        start(b)
        return carry

    n_grouped = lax.shift_right_logical(n_pieces, int(math.log2(group)))
    lax.fori_loop(0, n_grouped, grouped, 0)
    lax.fori_loop(n_grouped * group, n_pieces, single, 0)


def _moe_sort_kernel(ptab_s, tot_s, lend_s, rend_s, gt_ref, gtok_ref, h_ref, locrow_ref,
                     nprow_ref, xs_hbm, buf, zbuf, sem, zsem):
    tau = pl.program_id(0)
    n_t = pl.num_programs(0)
    slot = lax.rem(tau, 2)
    E, ts = gt_ref.shape
    D = h_ref.shape[1]
    n_cols = _piece_cols(ts)

    def copies(tile, sl, wait):
        if wait:
            rows = pl.multiple_of(tot_s[tile], ROW_ALIGN)

            @pl.when(rows > 0)
            def _():
                pltpu.make_async_copy(buf.at[sl, pl.ds(0, rows)], xs_hbm.at[pl.ds(0, rows)],
                                      sem.at[sl]).wait()
        else:
            def make_copy(loc, dst):
                return pltpu.make_async_copy(buf.at[sl, pl.ds(loc, ROW_ALIGN)],
                                             xs_hbm.at[pl.ds(dst, ROW_ALIGN)], sem.at[sl])
            _start_pieces(tile, ptab_s, tot_s, n_cols, make_copy)

    z_rows = zbuf.shape[0]
    used_rows = rend_s[E - 1]
    n_spare = (xs_hbm.shape[0] - used_rows) // z_rows

    def spare_fill(wait):
        def body(c, carry):
            dst = pl.multiple_of(used_rows + c * z_rows, z_rows)
            cp = pltpu.make_async_copy(zbuf, xs_hbm.at[pl.ds(dst, z_rows)], zsem.at[1])
            if wait:
                cp.wait()
            else:
                cp.start()
            return carry

        lax.fori_loop(0, n_spare, body, 0)

    @pl.when(tau == 0)
    def _():
        zbuf[...] = jnp.zeros_like(zbuf)
        spare_fill(False)

    @pl.when(tau >= 2)
    def _():
        copies(tau - 2, slot, True)

    g = gt_ref[...]
    mask = g > 0.0
    maskb = jnp.where(mask, 1.0, 0.0).astype(BF16)
    pad = jnp.zeros((LANES - E, ts), F32)
    pos = jnp.where(mask, _dot(maskb, _strict_upper(ts)), -1.0)
    pos = jnp.concatenate([pos, pad], axis=0).astype(BF16)
    lo_row = locrow_ref[...]
    hi_row = lo_row + nprow_ref[...]
    h = jnp.concatenate([h_ref[...]] + list(_split3(gtok_ref[...])), axis=1)
    lane = lax.broadcasted_iota(jnp.int32, (P_CHUNK, LANES), 1)

    def chunk(c):
        r0 = c * P_CHUNK
        r = (r0 + lax.broadcasted_iota(jnp.int32, (P_CHUNK, LANES), 0)).astype(F32)
        inside = (lo_row <= r) & (r < hi_row)
        group = jnp.where(inside, 1.0, 0.0).astype(BF16)
        want = r[:, 0:1] - jnp.sum(jnp.where(inside, lo_row, 0.0), axis=1, keepdims=True)
        hit = _dot(group, pos) == want
        rows = _dot(jnp.where(hit, 1.0, 0.0).astype(BF16), h)
        buf[slot, pl.ds(r0, P_CHUNK), 0:D] = rows[:, 0:D].astype(buf.dtype)
        extra = jnp.zeros((P_CHUNK, LANES), F32)
        for k in range(3):
            mine = jnp.where(inside, rows[:, D + k * LANES:D + (k + 1) * LANES], 0.0)
            extra = jnp.where(lane == k, jnp.sum(mine, axis=1, keepdims=True), extra)
        buf[slot, pl.ds(r0, P_CHUNK), D:D + LANES] = extra.astype(buf.dtype)

    n_static = -(-(TOP_K * ts + E * (ROW_ALIGN // 2)) // P_CHUNK)
    for c in range(buf.shape[1] // P_CHUNK):
        if c < n_static:
            chunk(c)
        else:
            pl.when(tot_s[tau] > c * P_CHUNK)(functools.partial(chunk, c))
    copies(tau, slot, False)

    @pl.when(tau == n_t - 1)
    def _():
        @pl.when(n_t >= 2)
        def _():
            copies(tau - 1, 1 - slot, True)
        copies(tau, slot, True)
        spare_fill(True)

        sizes =[zbuf.shape[0] >> s for s in range(int(math.log2(zbuf.shape[0] // ROW_ALIGN)) + 1)]

        def fill(wait):
            def e_body(e, carry):
                start = lend_s[e]
                n = rend_s[e] - start
                off = start
                for size in sizes:
                    bit = jnp.bitwise_and(n, size)

                    @pl.when(bit != 0)
                    def _(off=off, size=size):
                        cp = pltpu.make_async_copy(
                            zbuf.at[pl.ds(0, size)],
                            xs_hbm.at[pl.ds(pl.multiple_of(off, ROW_ALIGN), size)], zsem.at[0])
                        if wait:
                            cp.wait()
                        else:
                            cp.start()

                    off = off + bit
                return carry

            lax.fori_loop(0, E, e_body, 0)

        fill(False)
        fill(True)


def _moe_sort(ptab, tot, lend, rend, gates_t, gates, h2, loc_te, np_te, n_rows):
    E, T = gates_t.shape
    D = h2.shape[1]
    ts = SORT_TILE
    grid_spec = pltpu.PrefetchScalarGridSpec(
        num_scalar_prefetch=4,
        grid=(T // ts,),
        in_specs=[pl.BlockSpec((E, ts), lambda t, *_: (0, t)),
                  pl.BlockSpec((ts, LANES), lambda t, *_: (t, 0)),
                  pl.BlockSpec((ts, D), lambda t, *_: (t, 0)),
                  pl.BlockSpec((None, 1, LANES), lambda t, *_: (t, 0, 0)),
                  pl.BlockSpec((None, 1, LANES), lambda t, *_: (t, 0, 0))],
        out_specs=pl.BlockSpec(memory_space=pl.ANY),
        scratch_shapes=[pltpu.VMEM((2, _local_rows_bound(ts), D + LANES), BF16),
                        pltpu.VMEM((EXP_TILE // 2, D + LANES), BF16),
                        pltpu.SemaphoreType.DMA((2,)),
                        pltpu.SemaphoreType.DMA((2,))])
    return pl.pallas_call(
        _moe_sort_kernel,
        grid_spec=grid_spec,
        out_shape=jax.ShapeDtypeStruct((n_rows, D + LANES), BF16),
        compiler_params=pltpu.CompilerParams(dimension_semantics=("arbitrary",),
                                             vmem_limit_bytes=VMEM_LIMIT),
        name="moe_sort",
    )(ptab, tot, lend, rend, gates_t, gates, h2, loc_te, np_te)


def _moe_expert_kernel(texp_s, nused_s, x_ref, wg_ref, wu_ref, wd_ref, y_ref, wgu_s, wd_s):
    i = pl.program_id(0)
    f = wd_ref.shape[0]

    @pl.when(i < nused_s[0])
    def _():
        @pl.when((i == 0) | (texp_s[i] != texp_s[jnp.maximum(i - 1, 0)]))
        def _():
            wgu_s[:, 0:f] = wg_ref[...].astype(BF16)
            wgu_s[:, f:2 * f] = wu_ref[...].astype(BF16)
            wd_s[...] = wd_ref[...].astype(BF16)

        d = wd_ref.shape[1]
        gate = jnp.sum(x_ref[:, d:].astype(F32), axis=1, keepdims=True)
        a = _dot(x_ref[:, 0:d], wgu_s[...])
        act = _silu(a[:, :f]) * a[:, f:] * gate
        y_ref[...] = _dot(act.astype(BF16), wd_s[...]).astype(y_ref.dtype)

    @pl.when(i >= nused_s[0])
    def _():
        y_ref[...] = jnp.zeros_like(y_ref)


def _moe_expert(texp, nused, xs, w_gate, w_up, w_down, n_tiles):
    n_rows, xw = xs.shape
    D, f = w_gate.shape[-2:]
    tm = EXP_TILE

    def tile(i, texp, nused):
        return jnp.maximum(jnp.minimum(i, nused[0] - 1), 0)

    grid_spec = pltpu.PrefetchScalarGridSpec(
        num_scalar_prefetch=2,
        grid=(n_tiles,),
        in_specs=[pl.BlockSpec((tm, xw), lambda i, te, nu: (tile(i, te, nu), 0)),
                  pl.BlockSpec((None, D, f), lambda i, te, nu: (te[tile(i, te, nu)], 0, 0)),
                  pl.BlockSpec((None, D, f), lambda i, te, nu: (te[tile(i, te, nu)], 0, 0)),
                  pl.BlockSpec((None, f, D), lambda i, te, nu: (te[tile(i, te, nu)], 0, 0))],
        out_specs=pl.BlockSpec((tm, D), lambda i, te, nu: (i, 0)),
        scratch_shapes=[pltpu.VMEM((D, 2 * f), BF16), pltpu.VMEM((f, D), BF16)])
    return pl.pallas_call(
        _moe_expert_kernel,
        grid_spec=grid_spec,
        out_shape=jax.ShapeDtypeStruct((n_rows, D), BF16),
        compiler_params=pltpu.CompilerParams(dimension_semantics=("arbitrary",),
                                             vmem_limit_bytes=VMEM_LIMIT),
        name="moe_expert",
    )(texp, nused, xs, w_gate, w_up, w_down)


def _moe_combine_kernel(alpha, ptab_s, tot_s, g_ref, loc_ref, np_ref, h_ref, x1_ref,
                        mod_ref, sgu_ref, sd_ref, lg_ref, lb_ref, y_hbm, o_ref, ybuf, acc_ref,
                        sem):
    tau = pl.program_id(0)
    n_t = pl.num_programs(0)
    slot = lax.rem(tau, 2)
    ts, n_lane = g_ref.shape
    E = loc_ref.shape[0]
    n_cols = _piece_cols(ts)

    def copies(tile, sl, wait):
        if wait:
            rows = pl.multiple_of(tot_s[tile], ROW_ALIGN)

            @pl.when(rows > 0)
            def _():
                pltpu.make_async_copy(y_hbm.at[pl.ds(0, rows)], ybuf.at[sl, pl.ds(0, rows)],
                                      sem.at[sl]).wait()
        else:
            def make_copy(loc, dst):
                return pltpu.make_async_copy(y_hbm.at[pl.ds(dst, ROW_ALIGN)],
                                             ybuf.at[sl, pl.ds(loc, ROW_ALIGN)], sem.at[sl])
            _start_pieces(tile, ptab_s, tot_s, n_cols, make_copy)

    @pl.when(tau == 0)
    def _():
        ybuf[...] = jnp.zeros_like(ybuf)
        copies(0, 0, False)

    @pl.when(tau + 1 < n_t)
    def _():
        copies(tau + 1, 1 - slot, False)

    g = g_ref[...]
    mask = g > 0.0
    maskb = jnp.where(mask, 1.0, 0.0).astype(BF16)
    pos = jnp.where(mask, _dot(_strict_lower(ts), maskb), -1.0).astype(BF16)
    lane = lax.broadcasted_iota(jnp.int32, loc_ref.shape, 1)
    lo_col = jnp.sum(jnp.where(lane == tau, loc_ref[...], 0.0), axis=1, keepdims=True)
    hi_col = lo_col + jnp.sum(jnp.where(lane == tau, np_ref[...], 0.0), axis=1, keepdims=True)

    f = sd_ref.shape[0]
    a = _dot(h_ref[...], sgu_ref[...])
    acc_ref[...] = _dot((_silu(a[:, :f]) * a[:, f:]).astype(BF16), sd_ref[...])

    copies(tau, slot, True)

    def chunk(c):
        r0 = c * P_CHUNK
        r = (r0 + lax.broadcasted_iota(jnp.int32, (E, P_CHUNK), 1)).astype(F32)
        inside = (lo_col <= r) & (r < hi_col)
        group = jnp.concatenate([jnp.where(inside, 1.0, 0.0),
                                 jnp.zeros((n_lane - E, P_CHUNK), F32)], axis=0).astype(BF16)
        want = r[0:1, :] - jnp.sum(jnp.where(inside, lo_col, 0.0), axis=0, keepdims=True)
        hit = _dot(pos, group) == want
        return _dot(jnp.where(hit, 1.0, 0.0).astype(BF16), ybuf[slot, pl.ds(r0, P_CHUNK), :])

    n_static = -(-(TOP_K * ts + E * (ROW_ALIGN // 2)) // P_CHUNK)
    acc_ref[...] += sum(chunk(c) for c in range(n_static))
    for c in range(n_static, ybuf.shape[1] // P_CHUNK):
        @pl.when(tot_s[tau] > c * P_CHUNK)
        def _(c=c):
            acc_ref[...] += chunk(c)

    y = alpha * x1_ref[...] + mod_ref[5:6, :] * acc_ref[...]
    o_ref[...] = _layer_norm(y, lg_ref[...], lb_ref[...])


def _moe_combine(alpha, ptab, tot, gates, loc_et, np_et, h2, x1, mod, sgu, sd, ln_g, ln_b, ys,
                 S):
    T, D = h2.shape
    ts = SORT_TILE
    per_b = S // ts
    row = lambda a: pl.BlockSpec(a.shape, lambda t, *_: (0, 0))
    grid_spec = pltpu.PrefetchScalarGridSpec(
        num_scalar_prefetch=2,
        grid=(T // ts,),
        in_specs=[pl.BlockSpec((ts, LANES), lambda t, *_: (t, 0)),
                  row(loc_et), row(np_et),
                  pl.BlockSpec((ts, D), lambda t, *_: (t, 0)),
                  pl.BlockSpec((ts, D), lambda t, *_: (t, 0)),
                  pl.BlockSpec((None, 6, D), lambda t, *_: (t // per_b, 0, 0)),
                  row(sgu), row(sd), row(ln_g), row(ln_b),
                  pl.BlockSpec(memory_space=pl.ANY)],
        out_specs=pl.BlockSpec((ts, D), lambda t, *_: (t, 0)),
        scratch_shapes=[pltpu.VMEM((2, _local_rows_bound(ts), D), BF16),
                        pltpu.VMEM((ts, D), F32),
                        pltpu.SemaphoreType.DMA((2,))])
    return pl.pallas_call(
        functools.partial(_moe_combine_kernel, alpha),
        grid_spec=grid_spec,
        out_shape=jax.ShapeDtypeStruct((T, D), F32),
        compiler_params=pltpu.CompilerParams(dimension_semantics=("arbitrary",),
                                             vmem_limit_bytes=VMEM_LIMIT),
        name="moe_combine",
    )(ptab, tot, gates, loc_et, np_et, h2, x1, mod, sgu, sd, ln_g, ln_b, ys)


def _moe(alpha, h2, x1, gates, gates_t, mod, w_gate, w_up, w_down, sgu, sd, ln_g, ln_b):
    B, S, D = x1.shape
    T = B * S
    n_t = T // SORT_TILE
    n_bound = _sorted_tiles_bound(T)
    n_tab = -(-n_bound // LANES) * LANES
    ptab, loc_et, np_et, loc_te, np_te, tot, erow, texp, nused = _moe_meta(gates_t, n_tab)
    ptab = ptab.reshape(-1)
    tot = tot[0, :n_t]
    h2 = h2.reshape(T, D)
    x1 = x1.reshape(T, D)
    gates = gates.reshape(T, LANES)
    loc_rows = loc_te[:n_t].reshape(n_t, 1, LANES)
    np_rows = np_te[:n_t].reshape(n_t, 1, LANES)

    def sized(n_tiles):
        def run():
            xs = _moe_sort(ptab, tot, erow[0, :N_EXPERTS], erow[1, :N_EXPERTS], gates_t, gates,
                           h2, loc_rows, np_rows, n_tiles * EXP_TILE)
            ys = _moe_expert(texp[0], nused[0, :1], xs, w_gate, w_up, w_down, n_tiles)
            return _moe_combine(alpha, ptab, tot, gates, loc_et, np_et, h2, x1, mod, sgu, sd,
                                ln_g, ln_b, ys, S)
        return run

    n_small = min(_sorted_tiles_expected(T), n_bound)
    out = lax.cond(nused[0, 0] <= n_small, sized(n_small), sized(n_bound))
    return out.reshape(B, S, D)


def _rearrange_w_in(w):
    d_in = w.shape[0]
    scale = HEAD_DIM ** -0.5 * LOG2E
    fq, fk, fv = w[:, 0:512], w[:, 512:1024], w[:, 1024:1536]
    ff = w[:, 1536:1544]
    nq = w[:, 1544:2056]
    kc, vc, ks, vs, kw, vw = (w[:, 2056 + k * LANES:2056 + (k + 1) * LANES] for k in range(6))
    ng = w[:, 2824:2848]
    pad = jnp.zeros((d_in, LANES - ff.shape[1] - ng.shape[1]), w.dtype)
    cols = [fq * scale, fk, nq * scale, kc, kw, vc, ks, fv, vs, vw, ff, ng, pad]
    return jnp.concatenate(cols, axis=1).astype(BF16)


def _compress_weights(pos, w1, w2):
    half = CMP_BLOCK // 2
    w1r = w1.reshape(2, half, HEAD_DIM, CMP_HIDDEN)
    zeros = jnp.zeros_like(w1r[0])
    def spread(part):
        g0 = jnp.stack([part, zeros], axis=1).reshape(half * 2 * HEAD_DIM, CMP_HIDDEN)
        g1 = jnp.stack([zeros, part], axis=1).reshape(half * 2 * HEAD_DIM, CMP_HIDDEN)
        return jnp.concatenate([g0, g1], axis=1).astype(BF16)
    wa, wb = spread(w1r[0]), spread(w1r[1])
    z2 = jnp.zeros_like(w2)
    w2bd = jnp.concatenate([jnp.concatenate([w2, z2], axis=1),
                            jnp.concatenate([z2, w2], axis=1)], axis=0).astype(BF16)
    posr = pos.reshape(2, half, 1, HEAD_DIM)
    posr = jnp.broadcast_to(posr, (2, half, NSA_GROUPS, HEAD_DIM)).reshape(2, half * 2 * HEAD_DIM)
    return posr, wa, wb, w2bd


@functools.lru_cache(maxsize=None)
def _static_tables(S):
    tq = ATT_TILE
    n_cmp = (S - CMP_BLOCK) // CMP_STRIDE + 1
    n_pad = S // CMP_STRIDE
    n_slc = S // SEL_BLOCK
    t = np.arange(S)[:, None]
    n = np.arange(n_pad)[None, :]
    bucket_c = _t5_bucket_np(t - (n * CMP_STRIDE + CMP_BLOCK - 1)).reshape(1, -1)
    d = (np.arange(4)[:, None, None] * tq + np.arange(tq)[None, :, None]
         - np.arange(2 * tq)[None, None, :])
    bucket_w = _t5_bucket_np(d).reshape(1, -1)
    cs = np.arange(n_pad)[None, :] * CMP_STRIDE
    sj = np.arange(n_slc)[:, None] * SEL_BLOCK
    ovl_t = ((cs < sj + SEL_BLOCK) & (cs + CMP_BLOCK > sj) & (np.arange(n_pad)[None, :] < n_cmp))
    return bucket_c, bucket_w, ovl_t.astype(np.float32)


def kernel(x, c, w_ada, b_ada, w_in, b_f, cmp_pos_k, cmp_w1_k, cmp_w2_k, cmp_pos_v, cmp_w1_v,
           cmp_w2_v, rel_bias, w_out, ln1_g, ln1_b, w_router, e_bias, w_gate, w_up, w_down,
           ws_gate, ws_up, ws_down, ln2_g, ln2_b):
    B, S, D = x.shape
    depth = w_ada.shape[0]
    alpha = (2 * depth) ** 0.25
    tq = ATT_TILE
    assert w_in.shape[-1] == 3 * FOX_HEADS * HEAD_DIM + FOX_HEADS + NSA_HEADS * HEAD_DIM \
        + 6 * NSA_GROUPS * HEAD_DIM + 3 * NSA_HEADS
    assert NSA_GROUPS * HEAD_DIM == LANES and S // CMP_STRIDE == LANES
    assert WINDOW == 2 * tq and S // SEL_BLOCK <= HEAD_DIM // 2
    assert S % FOX_TILE == 0 and S % CMP_TILE == 0 and (B * S) % SORT_TILE == 0
    assert w_gate.shape[1:] == (N_EXPERTS, D, w_down.shape[2])
    bucket_c, bucket_w, ovl_t = _static_tables(S)
    rel_bias_t = rel_bias.T * LOG2E
    bias_c = _bias_table(jnp.asarray(bucket_c), rel_bias_t).reshape(NSA_HEADS, S, S // CMP_STRIDE)
    w4 = _bias_table(jnp.asarray(bucket_w), rel_bias_t).reshape(NSA_HEADS, 4, tq, 2 * tq)
    ovl_t = jnp.asarray(ovl_t, BF16)

    for l in range(depth):
        mod = _ada(c, w_ada[l], b_ada[l]).reshape(B, 6, D)
        bf_row = jnp.zeros((1, LANES), F32).at[0, :FOX_HEADS].set(b_f[l])
        (fq, fk, nq, kc, kw, vc, ks, fv, vs, vw, misc, misc_t) = _in_proj(
            x, mod, _rearrange_w_in(w_in[l]), bf_row)

        o_fox = _fox(fq, fk, fv, misc_t[:, :FOX_HEADS, :])

        pk, wak, wbk, w2k = _compress_weights(cmp_pos_k[l], cmp_w1_k[l], cmp_w2_k[l])
        pv, wav, wbv, w2v = _compress_weights(cmp_pos_v[l], cmp_w1_v[l], cmp_w2_v[l])
        rows = S // CMP_STRIDE
        kcmp, vcmp = _compress(kc.reshape(B, rows, CMP_STRIDE * LANES),
                               vc.reshape(B, rows, CMP_STRIDE * LANES),
                               pk, pv, wak, wbk, wav, wbv, w2k, w2v)

        oc, sel = _cmp_sel(nq, kcmp, vcmp, bias_c, misc, ovl_t)
        o_nsa = _nsa(nq, ks, vs, kw, vw, sel, w4, misc, oc)

        x1, h2, gates, gates_t = _out_proj(
            alpha, o_fox, o_nsa, x, mod, w_out[l].astype(BF16), ln1_g[l].reshape(1, D),
            ln1_b[l].reshape(1, D), w_router[l].T, e_bias[l].reshape(N_EXPERTS, 1))

        sgu = jnp.concatenate([ws_gate[l], ws_up[l]], axis=-1).astype(BF16)
        x = _moe(alpha, h2, x1, gates, gates_t, mod, w_gate[l], w_up[l], w_down[l], sgu,
                 ws_down[l].astype(BF16), ln2_g[l].reshape(1, D), ln2_b[l].reshape(1, D))
    return x
```
